```python
import math
import jax, jax.numpy as jnp
from jax import lax
import numpy as np

D_MODEL = 1024
BATCH = 32
SEQ = 2048
DEPTH = 1

N_META = 16
NORM_EPS = 1e-6
HG_HEADS = 8
HG_DK = D_MODEL // HG_HEADS
HG_DV = D_MODEL // HG_HEADS
HG_WIDTH = HG_HEADS * HG_DK
HG_CHUNK = 16
MLA_HEADS = 8
QK_NOPE = 128
QK_ROPE = 64
V_HEAD = 128
Q_LORA = 256
KV_LORA = 256
ROPE_THETA = 10000.0
ATTN_BLOCK = 128
FFN_HIDDEN = ((8 * D_MODEL + 3 * 256 - 1) // (3 * 256)) * 256
IN_SIZES = (HG_WIDTH, HG_WIDTH, HG_WIDTH, HG_WIDTH,
            Q_LORA, KV_LORA, QK_ROPE,
            D_MODEL, D_MODEL)
IN_COLS = sum(IN_SIZES)

kernel_name = "hybrid_hgrn2_mla_gated_block"


def rms_norm(x, g):
    xf = x.astype(jnp.float32)
    y = xf * lax.rsqrt(jnp.mean(xf * xf, axis=-1, keepdims=True) + NORM_EPS)
    return (y * g.astype(jnp.float32)).astype(x.dtype)


def rope_tables(length):
    pos = jnp.arange(length, dtype=jnp.float32)
    inv_freq = 1.0 / (ROPE_THETA ** (jnp.arange(0, QK_ROPE, 2, dtype=jnp.float32) / QK_ROPE))
    ang = pos[:, None] * inv_freq[None, :]
    return jnp.cos(ang), jnp.sin(ang)


def apply_rope(t, cos, sin):
    t32 = t.astype(jnp.float32)
    t1, t2 = jnp.split(t32, 2, axis=-1)
    return jnp.concatenate([t1 * cos - t2 * sin, t2 * cos + t1 * sin], axis=-1).astype(t.dtype)


def hgrn2_chunk_scan(q, k, v, logf):
    B, L, H, _ = q.shape
    n = L // HG_CHUNK

    def to_chunks(t):
        return t.reshape(B, n, HG_CHUNK, H, t.shape[-1]).transpose(1, 0, 3, 2, 4)

    xs = (to_chunks(q), to_chunks(k), to_chunks(v), to_chunks(logf))
    causal = jnp.tril(jnp.ones((HG_CHUNK, HG_CHUNK), dtype=bool))[:, :, None]

    def step(S, inp):
        qb, kb, vb, gb = inp
        b = jnp.cumsum(gb, axis=2)
        o_inter = jnp.einsum('bhtk,bhkv->bhtv', qb * jnp.exp(b), S)
        diff = b[:, :, :, None, :] - b[:, :, None, :, :]
        decay = jnp.exp(jnp.where(causal, diff, -jnp.inf))
        A = jnp.einsum('bhtsk,bhsk->bhts', decay * qb[:, :, :, None, :], kb)
        o_intra = jnp.einsum('bhts,bhsv->bhtv', A, vb)
        b_last = b[:, :, -1:, :]
        S_new = jnp.exp(b_last[:, :, 0, :])[..., None] * S + jnp.einsum(
            'bhsk,bhsv->bhkv', kb * jnp.exp(b_last - b), vb)
        return S_new, o_inter + o_intra

    S0 = jnp.zeros((B, H, q.shape[-1], v.shape[-1]), jnp.float32)
    _, ys = lax.scan(step, S0, xs)
    return ys.transpose(1, 0, 3, 2, 4).reshape(B, L, H, v.shape[-1])


def hgrn2_mixer(q, f_pre, i, g, lb, norm_g):
    B, L, _ = q.shape
    split = lambda t: t.reshape(B, L, HG_HEADS, -1).astype(jnp.float32)
    qh = jax.nn.silu(split(q))
    lbh = lb.astype(jnp.float32).reshape(HG_HEADS, HG_DK)
    fgate = lbh + (1.0 - lbh) * jax.nn.sigmoid(split(f_pre))
    o = hgrn2_chunk_scan(qh, 1.0 - fgate, split(i), jnp.log(fgate))
    o = rms_norm(o, norm_g) * jax.nn.silu(split(g))
    return o.reshape(B, L, HG_WIDTH).astype(q.dtype)


def mla_mixer(c_q, c_kv, k_pe, q_norm_g, w_q_b, kv_norm_g, w_kv_b, cos, sin):
    B, L, _ = c_q.shape
    q = (rms_norm(c_q, q_norm_g) @ w_q_b).reshape(B, L, MLA_HEADS, QK_NOPE + QK_ROPE)
    q_nope = q[..., :QK_NOPE]
    q_pe = apply_rope(q[..., QK_NOPE:], cos[:, None, :], sin[:, None, :])
    kv = (rms_norm(c_kv, kv_norm_g) @ w_kv_b).reshape(B, L, MLA_HEADS, QK_NOPE + V_HEAD)
    k_nope, v = kv[..., :QK_NOPE], kv[..., QK_NOPE:]
    k_pe = apply_rope(k_pe, cos, sin)
    scale = (QK_NOPE + QK_ROPE) ** -0.5
    bounds = [(0, N_META)] + [(N_META + s, min(N_META + s + ATTN_BLOCK, L))
                              for s in range(0, L - N_META, ATTN_BLOCK)]
    outs = []
    for start, end in bounds:
        s = (jnp.einsum('bqhd,bkhd->bhqk', q_nope[:, start:end], k_nope[:, :end])
             + jnp.einsum('bqhr,bkr->bhqk', q_pe[:, start:end], k_pe[:, :end]))
        s = s.astype(jnp.float32) * scale
        qpos = jnp.arange(start, end)[:, None]
        kpos = jnp.arange(end)[None, :]
        s = jnp.where(kpos <= qpos, s, -jnp.inf)
        p = jax.nn.softmax(s, axis=-1).astype(v.dtype)
        outs.append(jnp.einsum('bhqk,bkhd->bqhd', p, v[:, :end]))
    o = jnp.concatenate(outs, axis=1)
    return o.reshape(B, L, MLA_HEADS * V_HEAD)


def _fwd_setup_inputs(seed: int = 0) -> dict:
    key = jax.random.key(seed)
    ks = jax.random.split(key, 20)
    nrm = lambda k, shape, scale: jax.random.normal(k, shape, jnp.float32) * scale
    gain = lambda k, shape: 1.0 + 0.02 * jax.random.normal(k, shape, jnp.float32)
    return {
        "x": nrm(ks[0], (BATCH, SEQ, D_MODEL), 1.0),
        "meta_tokens": nrm(ks[1], (N_META, D_MODEL), 1.0),
        "w_in": nrm(ks[2], (DEPTH, D_MODEL, IN_COLS), D_MODEL ** -0.5),
        "b_gate": nrm(ks[3], (DEPTH, 2 * D_MODEL), 0.01),
        "lb_logits": nrm(ks[4], (DEPTH + 1, HG_WIDTH), 0.1),
        "hg_norm_g": gain(ks[5], (DEPTH, HG_DV)),
        "w_hg_o": nrm(ks[6], (DEPTH, HG_WIDTH, D_MODEL), HG_WIDTH ** -0.5),
        "q_a_norm_g": gain(ks[7], (DEPTH, Q_LORA)),
        "w_q_b": nrm(ks[8], (DEPTH, Q_LORA, MLA_HEADS * (QK_NOPE + QK_ROPE)), Q_LORA ** -0.5),
        "kv_a_norm_g": gain(ks[9], (DEPTH, KV_LORA)),
        "w_kv_b": nrm(ks[10], (DEPTH, KV_LORA, MLA_HEADS * (QK_NOPE + V_HEAD)), KV_LORA ** -0.5),
        "w_mla_o": nrm(ks[11], (DEPTH, MLA_HEADS * V_HEAD, D_MODEL), (MLA_HEADS * V_HEAD) ** -0.5),
        "w_out": nrm(ks[12], (DEPTH, D_MODEL, D_MODEL), D_MODEL ** -0.5),
        "mix_pre_g": gain(ks[13], (DEPTH, D_MODEL)),
        "mix_post_g": gain(ks[14], (DEPTH, D_MODEL)),
        "ffn_pre_g": gain(ks[15], (DEPTH, D_MODEL)),
        "ffn_post_g": gain(ks[16], (DEPTH, D_MODEL)),
        "w_ffn_in": nrm(ks[17], (DEPTH, D_MODEL, 2 * FFN_HIDDEN), D_MODEL ** -0.5),
        "w_ffn_out": nrm(ks[18], (DEPTH, FFN_HIDDEN, D_MODEL), FFN_HIDDEN ** -0.5),
    }


def _fwd_reference(x, meta_tokens, w_in, b_gate, lb_logits, hg_norm_g, w_hg_o, q_a_norm_g, w_q_b,
              kv_a_norm_g, w_kv_b, w_mla_o, w_out, mix_pre_g, mix_post_g, ffn_pre_g, ffn_post_g,
              w_ffn_in, w_ffn_out):
    B = x.shape[0]
    meta = jnp.broadcast_to(meta_tokens[None].astype(x.dtype), (B, N_META, D_MODEL))
    h = jnp.concatenate([meta, x], axis=1)
    L = h.shape[1]
    cos, sin = rope_tables(L)
    lower_bounds = jnp.cumsum(jax.nn.softmax(lb_logits.astype(jnp.float32), axis=0), axis=0)
    splits = []
    acc = 0
    for sz in IN_SIZES[:-2]:
        acc += sz
        splits.append(acc)
    for l in range(DEPTH):
        u = rms_norm(h, mix_pre_g[l])
        proj = u @ w_in[l]
        hq, hf, hi, hg, cq, ckv, kpe, gates = jnp.split(proj, splits, axis=-1)
        y_a = hgrn2_mixer(hq, hf, hi, hg, lower_bounds[l], hg_norm_g[l]) @ w_hg_o[l]
        y_b = mla_mixer(cq, ckv, kpe, q_a_norm_g[l], w_q_b[l], kv_a_norm_g[l], w_kv_b[l],
                        cos, sin) @ w_mla_o[l]
        gate_a, gate_b = jnp.split(jax.nn.sigmoid(gates + b_gate[l]), 2, axis=-1)
        mixed = (gate_a * y_a + gate_b * y_b) @ w_out[l]
        h = h + rms_norm(mixed, mix_post_g[l])
        u = rms_norm(h, ffn_pre_g[l])
        gt, up = jnp.split(u @ w_ffn_in[l], 2, axis=-1)
        h = h + rms_norm((jax.nn.silu(gt) * up) @ w_ffn_out[l], ffn_post_g[l])
    return h[:, N_META:, :]


import jax as _jax
import jax.numpy as _jnp

TWIN_FORMAT = 'train_step'
FWD_PARAMS = ['x', 'meta_tokens', 'w_in', 'b_gate', 'lb_logits', 'hg_norm_g', 'w_hg_o', 'q_a_norm_g', 'w_q_b', 'kv_a_norm_g', 'w_kv_b', 'w_mla_o', 'w_out', 'mix_pre_g', 'mix_post_g', 'ffn_pre_g', 'ffn_post_g', 'w_ffn_in', 'w_ffn_out']
TWIN_WEIGHTS = ['meta_tokens', 'w_in', 'b_gate', 'lb_logits', 'hg_norm_g', 'w_hg_o', 'q_a_norm_g', 'w_q_b', 'kv_a_norm_g', 'w_kv_b', 'w_mla_o', 'w_out', 'mix_pre_g', 'mix_post_g', 'ffn_pre_g', 'ffn_post_g', 'w_ffn_in', 'w_ffn_out']
TWIN_DIFF_INPUT = 'x'
TWIN_INPUTS = ['x', 'meta_tokens', 'w_in', 'b_gate', 'lb_logits', 'hg_norm_g', 'w_hg_o', 'q_a_norm_g', 'w_q_b', 'kv_a_norm_g', 'w_kv_b', 'w_mla_o', 'w_out', 'mix_pre_g', 'mix_post_g', 'ffn_pre_g', 'ffn_post_g', 'w_ffn_in', 'w_ffn_out', 'loss_target', 'm_meta_tokens', 'm_w_in', 'm_b_gate', 'm_lb_logits', 'm_hg_norm_g', 'm_w_hg_o', 'm_q_a_norm_g', 'm_w_q_b', 'm_kv_a_norm_g', 'm_w_kv_b', 'm_w_mla_o', 'm_w_out', 'm_mix_pre_g', 'm_mix_post_g', 'm_ffn_pre_g', 'm_ffn_post_g', 'm_w_ffn_in', 'm_w_ffn_out', 'v_meta_tokens', 'v_w_in', 'v_b_gate', 'v_lb_logits', 'v_hg_norm_g', 'v_w_hg_o', 'v_q_a_norm_g', 'v_w_q_b', 'v_kv_a_norm_g', 'v_w_kv_b', 'v_w_mla_o', 'v_w_out', 'v_mix_pre_g', 'v_mix_post_g', 'v_ffn_pre_g', 'v_ffn_post_g', 'v_w_ffn_in', 'v_w_ffn_out']
TWIN_OUTPUTS = ['loss', 'grad_x', 'grad_meta_tokens', 'grad_w_in', 'grad_b_gate', 'grad_lb_logits', 'grad_hg_norm_g', 'grad_w_hg_o', 'grad_q_a_norm_g', 'grad_w_q_b', 'grad_kv_a_norm_g', 'grad_w_kv_b', 'grad_w_mla_o', 'grad_w_out', 'grad_mix_pre_g', 'grad_mix_post_g', 'grad_ffn_pre_g', 'grad_ffn_post_g', 'grad_w_ffn_in', 'grad_w_ffn_out', 'delta_meta_tokens', 'delta_w_in', 'delta_b_gate', 'delta_lb_logits', 'delta_hg_norm_g', 'delta_w_hg_o', 'delta_q_a_norm_g', 'delta_w_q_b', 'delta_kv_a_norm_g', 'delta_w_kv_b', 'delta_w_mla_o', 'delta_w_out', 'delta_mix_pre_g', 'delta_mix_post_g', 'delta_ffn_pre_g', 'delta_ffn_post_g', 'delta_w_ffn_in', 'delta_w_ffn_out', 'new_m_meta_tokens', 'new_m_w_in', 'new_m_b_gate', 'new_m_lb_logits', 'new_m_hg_norm_g', 'new_m_w_hg_o', 'new_m_q_a_norm_g', 'new_m_w_q_b', 'new_m_kv_a_norm_g', 'new_m_w_kv_b', 'new_m_w_mla_o', 'new_m_w_out', 'new_m_mix_pre_g', 'new_m_mix_post_g', 'new_m_ffn_pre_g', 'new_m_ffn_post_g', 'new_m_w_ffn_in', 'new_m_w_ffn_out', 'new_v_meta_tokens', 'new_v_w_in', 'new_v_b_gate', 'new_v_lb_logits', 'new_v_hg_norm_g', 'new_v_w_hg_o', 'new_v_q_a_norm_g', 'new_v_w_q_b', 'new_v_kv_a_norm_g', 'new_v_w_kv_b', 'new_v_w_mla_o', 'new_v_w_out', 'new_v_mix_pre_g', 'new_v_mix_post_g', 'new_v_ffn_pre_g', 'new_v_ffn_post_g', 'new_v_w_ffn_in', 'new_v_w_ffn_out']
TWIN_LEAF_KINDS = {'loss': 'loss', 'grad_x': 'grad_x', 'grad_meta_tokens': 'grad_w', 'grad_w_in': 'grad_w', 'grad_b_gate': 'grad_w', 'grad_lb_logits': 'grad_w', 'grad_hg_norm_g': 'grad_w', 'grad_w_hg_o': 'grad_w', 'grad_q_a_norm_g': 'grad_w', 'grad_w_q_b': 'grad_w', 'grad_kv_a_norm_g': 'grad_w', 'grad_w_kv_b': 'grad_w', 'grad_w_mla_o': 'grad_w', 'grad_w_out': 'grad_w', 'grad_mix_pre_g': 'grad_w', 'grad_mix_post_g': 'grad_w', 'grad_ffn_pre_g': 'grad_w', 'grad_ffn_post_g': 'grad_w', 'grad_w_ffn_in': 'grad_w', 'grad_w_ffn_out': 'grad_w', 'delta_meta_tokens': 'delta_w', 'delta_w_in': 'delta_w', 'delta_b_gate': 'delta_w', 'delta_lb_logits': 'delta_w', 'delta_hg_norm_g': 'delta_w', 'delta_w_hg_o': 'delta_w', 'delta_q_a_norm_g': 'delta_w', 'delta_w_q_b': 'delta_w', 'delta_kv_a_norm_g': 'delta_w', 'delta_w_kv_b': 'delta_w', 'delta_w_mla_o': 'delta_w', 'delta_w_out': 'delta_w', 'delta_mix_pre_g': 'delta_w', 'delta_mix_post_g': 'delta_w', 'delta_ffn_pre_g': 'delta_w', 'delta_ffn_post_g': 'delta_w', 'delta_w_ffn_in': 'delta_w', 'delta_w_ffn_out': 'delta_w', 'new_m_meta_tokens': 'new_m', 'new_m_w_in': 'new_m', 'new_m_b_gate': 'new_m', 'new_m_lb_logits': 'new_m', 'new_m_hg_norm_g': 'new_m', 'new_m_w_hg_o': 'new_m', 'new_m_q_a_norm_g': 'new_m', 'new_m_w_q_b': 'new_m', 'new_m_kv_a_norm_g': 'new_m', 'new_m_w_kv_b': 'new_m', 'new_m_w_mla_o': 'new_m', 'new_m_w_out': 'new_m', 'new_m_mix_pre_g': 'new_m', 'new_m_mix_post_g': 'new_m', 'new_m_ffn_pre_g': 'new_m', 'new_m_ffn_post_g': 'new_m', 'new_m_w_ffn_in': 'new_m', 'new_m_w_ffn_out': 'new_m', 'new_v_meta_tokens': 'new_v', 'new_v_w_in': 'new_v', 'new_v_b_gate': 'new_v', 'new_v_lb_logits': 'new_v', 'new_v_hg_norm_g': 'new_v', 'new_v_w_hg_o': 'new_v', 'new_v_q_a_norm_g': 'new_v', 'new_v_w_q_b': 'new_v', 'new_v_kv_a_norm_g': 'new_v', 'new_v_w_kv_b': 'new_v', 'new_v_w_mla_o': 'new_v', 'new_v_w_out': 'new_v', 'new_v_mix_pre_g': 'new_v', 'new_v_mix_post_g': 'new_v', 'new_v_ffn_pre_g': 'new_v', 'new_v_ffn_post_g': 'new_v', 'new_v_w_ffn_in': 'new_v', 'new_v_w_ffn_out': 'new_v'}


def _forward(args):
    return _fwd_reference(*[args[k] for k in FWD_PARAMS])


def _output_shape():
    out = _jax.eval_shape(lambda: _forward(_fwd_setup_inputs(0)))
    return out.shape, out.dtype

N_MICROBATCH = 1
ADAM_LR = 0.001
ADAM_B1 = 0.9
ADAM_B2 = 0.999
ADAM_EPS = 1e-08
ADAM_WD = 0.01
ADAM_STEP = 10
PER_EXAMPLE_BATCH_AXIS = {'x': 0, 'loss_target': 0}
SHARED_INPUTS = []
_WEIGHT_DTYPES = {'meta_tokens': _jnp.float32, 'w_in': _jnp.float32, 'b_gate': _jnp.float32, 'lb_logits': _jnp.float32, 'hg_norm_g': _jnp.float32, 'w_hg_o': _jnp.float32, 'q_a_norm_g': _jnp.float32, 'w_q_b': _jnp.float32, 'kv_a_norm_g': _jnp.float32, 'w_kv_b': _jnp.float32, 'w_mla_o': _jnp.float32, 'w_out': _jnp.float32, 'mix_pre_g': _jnp.float32, 'mix_post_g': _jnp.float32, 'ffn_pre_g': _jnp.float32, 'ffn_post_g': _jnp.float32, 'w_ffn_in': _jnp.float32, 'w_ffn_out': _jnp.float32}
MOMENT_SCALE = {'meta_tokens': 3.278293e-02, 'w_in': 3.704235e-01, 'b_gate': 2.170659e-01, 'lb_logits': 6.096196e-02, 'hg_norm_g': 3.183535e+00, 'w_hg_o': 7.090989e-01, 'q_a_norm_g': 3.299230e-01, 'w_q_b': 1.309382e-01, 'kv_a_norm_g': 4.868009e-01, 'w_kv_b': 1.602561e-01, 'w_mla_o': 1.799178e-01, 'w_out': 7.576645e-01, 'mix_pre_g': 9.973438e-01, 'mix_post_g': 6.358530e+01, 'ffn_pre_g': 7.779534e-01, 'ffn_post_g': 6.406145e+01, 'w_ffn_in': 3.414355e-01, 'w_ffn_out': 6.881411e-01}


def _to_microbatches(a, axis):
    t = _jnp.moveaxis(a, axis, 0)
    t = t.reshape((N_MICROBATCH, t.shape[0] // N_MICROBATCH) + t.shape[1:])
    return _jnp.moveaxis(t, 1, axis + 1)


def setup_inputs(seed: int = 0) -> dict:
    inp = _fwd_setup_inputs(seed)
    key = _jax.random.fold_in(_jax.random.key(seed), 7919)
    shape, _ = _output_shape()
    out = dict(inp)
    out["loss_target"] = _jax.random.normal(_jax.random.fold_in(key, 0), shape, _jnp.float32)
    for i, name in enumerate(TWIN_WEIGHTS):
        w = inp[name].astype(_jnp.float32)
        if MOMENT_SCALE is None:
            s = _jnp.sqrt(_jnp.mean(_jnp.square(w)) + 1e-30)
        else:
            s = MOMENT_SCALE[name]
        km, kv = _jax.random.split(_jax.random.fold_in(key, i + 1))
        out[name] = w
        out["m_" + name] = s * _jax.random.normal(km, w.shape, _jnp.float32)
        out["v_" + name] = (s * s) * _jax.random.uniform(kv, w.shape, _jnp.float32, 0.5, 1.5)
    if N_MICROBATCH > 1:
        for name, axis in PER_EXAMPLE_BATCH_AXIS.items():
            out[name] = _to_microbatches(out[name], axis)
    return {'x': out['x'], 'meta_tokens': out['meta_tokens'], 'w_in': out['w_in'], 'b_gate': out['b_gate'], 'lb_logits': out['lb_logits'], 'hg_norm_g': out['hg_norm_g'], 'w_hg_o': out['w_hg_o'], 'q_a_norm_g': out['q_a_norm_g'], 'w_q_b': out['w_q_b'], 'kv_a_norm_g': out['kv_a_norm_g'], 'w_kv_b': out['w_kv_b'], 'w_mla_o': out['w_mla_o'], 'w_out': out['w_out'], 'mix_pre_g': out['mix_pre_g'], 'mix_post_g': out['mix_post_g'], 'ffn_pre_g': out['ffn_pre_g'], 'ffn_post_g': out['ffn_post_g'], 'w_ffn_in': out['w_ffn_in'], 'w_ffn_out': out['w_ffn_out'], 'loss_target': out['loss_target'], 'm_meta_tokens': out['m_meta_tokens'], 'm_w_in': out['m_w_in'], 'm_b_gate': out['m_b_gate'], 'm_lb_logits': out['m_lb_logits'], 'm_hg_norm_g': out['m_hg_norm_g'], 'm_w_hg_o': out['m_w_hg_o'], 'm_q_a_norm_g': out['m_q_a_norm_g'], 'm_w_q_b': out['m_w_q_b'], 'm_kv_a_norm_g': out['m_kv_a_norm_g'], 'm_w_kv_b': out['m_w_kv_b'], 'm_w_mla_o': out['m_w_mla_o'], 'm_w_out': out['m_w_out'], 'm_mix_pre_g': out['m_mix_pre_g'], 'm_mix_post_g': out['m_mix_post_g'], 'm_ffn_pre_g': out['m_ffn_pre_g'], 'm_ffn_post_g': out['m_ffn_post_g'], 'm_w_ffn_in': out['m_w_ffn_in'], 'm_w_ffn_out': out['m_w_ffn_out'], 'v_meta_tokens': out['v_meta_tokens'], 'v_w_in': out['v_w_in'], 'v_b_gate': out['v_b_gate'], 'v_lb_logits': out['v_lb_logits'], 'v_hg_norm_g': out['v_hg_norm_g'], 'v_w_hg_o': out['v_w_hg_o'], 'v_q_a_norm_g': out['v_q_a_norm_g'], 'v_w_q_b': out['v_w_q_b'], 'v_kv_a_norm_g': out['v_kv_a_norm_g'], 'v_w_kv_b': out['v_w_kv_b'], 'v_w_mla_o': out['v_w_mla_o'], 'v_w_out': out['v_w_out'], 'v_mix_pre_g': out['v_mix_pre_g'], 'v_mix_post_g': out['v_mix_post_g'], 'v_ffn_pre_g': out['v_ffn_pre_g'], 'v_ffn_post_g': out['v_ffn_post_g'], 'v_w_ffn_in': out['v_w_ffn_in'], 'v_w_ffn_out': out['v_w_ffn_out']}


def _loss(weights, diff, rest, loss_target):
    with _jax.named_scope("forward"):
        args = {**rest, TWIN_DIFF_INPUT: diff, **{k: w.astype(_WEIGHT_DTYPES[k]) for k, w in weights.items()}}
        y = _forward(args)
    with _jax.named_scope("loss_head"):
        err = _jnp.square(y.astype(_jnp.float32) - loss_target)
        return 0.5 * _jnp.sum(_jnp.mean(err, axis=-1)) if err.ndim else 0.5 * err


def _adamw(w, g, m, v):
    m = ADAM_B1 * m + (1.0 - ADAM_B1) * g
    v = ADAM_B2 * v + (1.0 - ADAM_B2) * _jnp.square(g)
    m_hat = m / (1.0 - ADAM_B1 ** ADAM_STEP)
    v_hat = v / (1.0 - ADAM_B2 ** ADAM_STEP)
    delta = -ADAM_LR * (m_hat / (_jnp.sqrt(v_hat) + ADAM_EPS) + ADAM_WD * w)
    return delta, m, v


def reference(x, meta_tokens, w_in, b_gate, lb_logits, hg_norm_g, w_hg_o, q_a_norm_g, w_q_b, kv_a_norm_g, w_kv_b, w_mla_o, w_out, mix_pre_g, mix_post_g, ffn_pre_g, ffn_post_g, w_ffn_in, w_ffn_out, loss_target, m_meta_tokens, m_w_in, m_b_gate, m_lb_logits, m_hg_norm_g, m_w_hg_o, m_q_a_norm_g, m_w_q_b, m_kv_a_norm_g, m_w_kv_b, m_w_mla_o, m_w_out, m_mix_pre_g, m_mix_post_g, m_ffn_pre_g, m_ffn_post_g, m_w_ffn_in, m_w_ffn_out, v_meta_tokens, v_w_in, v_b_gate, v_lb_logits, v_hg_norm_g, v_w_hg_o, v_q_a_norm_g, v_w_q_b, v_kv_a_norm_g, v_w_kv_b, v_w_mla_o, v_w_out, v_mix_pre_g, v_mix_post_g, v_ffn_pre_g, v_ffn_post_g, v_w_ffn_in, v_w_ffn_out):
    given = dict(x=x, meta_tokens=meta_tokens, w_in=w_in, b_gate=b_gate, lb_logits=lb_logits, hg_norm_g=hg_norm_g, w_hg_o=w_hg_o, q_a_norm_g=q_a_norm_g, w_q_b=w_q_b, kv_a_norm_g=kv_a_norm_g, w_kv_b=w_kv_b, w_mla_o=w_mla_o, w_out=w_out, mix_pre_g=mix_pre_g, mix_post_g=mix_post_g, ffn_pre_g=ffn_pre_g, ffn_post_g=ffn_post_g, w_ffn_in=w_ffn_in, w_ffn_out=w_ffn_out, loss_target=loss_target, m_meta_tokens=m_meta_tokens, m_w_in=m_w_in, m_b_gate=m_b_gate, m_lb_logits=m_lb_logits, m_hg_norm_g=m_hg_norm_g, m_w_hg_o=m_w_hg_o, m_q_a_norm_g=m_q_a_norm_g, m_w_q_b=m_w_q_b, m_kv_a_norm_g=m_kv_a_norm_g, m_w_kv_b=m_w_kv_b, m_w_mla_o=m_w_mla_o, m_w_out=m_w_out, m_mix_pre_g=m_mix_pre_g, m_mix_post_g=m_mix_post_g, m_ffn_pre_g=m_ffn_pre_g, m_ffn_post_g=m_ffn_post_g, m_w_ffn_in=m_w_ffn_in, m_w_ffn_out=m_w_ffn_out, v_meta_tokens=v_meta_tokens, v_w_in=v_w_in, v_b_gate=v_b_gate, v_lb_logits=v_lb_logits, v_hg_norm_g=v_hg_norm_g, v_w_hg_o=v_w_hg_o, v_q_a_norm_g=v_q_a_norm_g, v_w_q_b=v_w_q_b, v_kv_a_norm_g=v_kv_a_norm_g, v_w_kv_b=v_w_kv_b, v_w_mla_o=v_w_mla_o, v_w_out=v_w_out, v_mix_pre_g=v_mix_pre_g, v_mix_post_g=v_mix_post_g, v_ffn_pre_g=v_ffn_pre_g, v_ffn_post_g=v_ffn_post_g, v_w_ffn_in=v_w_ffn_in, v_w_ffn_out=v_w_ffn_out)
    weights = {n: given[n] for n in TWIN_WEIGHTS}
    shared = {n: given[n] for n in SHARED_INPUTS}
    per_example = {n: given[n] for n in ['x']}
    grad_fn = _jax.value_and_grad(_loss, argnums=(0, 1))

    def one_microbatch(ex, loss_target):
        ex = dict(ex)
        diff = ex.pop(TWIN_DIFF_INPUT)
        return grad_fn(weights, diff, {**shared, **ex}, loss_target)

    if N_MICROBATCH == 1:
        loss, (grad_w, grad_x) = one_microbatch(per_example, given["loss_target"])
    else:
        def body(carry, xs):
            loss_sum, grad_sum = carry
            l_k, (gw_k, gx_k) = one_microbatch(xs[0], xs[1])
            with _jax.named_scope("update"):
                return (loss_sum + l_k, _jax.tree.map(_jnp.add, grad_sum, gw_k)), gx_k

        init = (_jnp.zeros((), _jnp.float32), _jax.tree.map(_jnp.zeros_like, weights))
        (loss, grad_w), grad_x = _jax.lax.scan(body, init, (per_example, given["loss_target"]))
    with _jax.named_scope("update"):
        delta_w, new_m, new_v = {}, {}, {}
        for n in TWIN_WEIGHTS:
            delta_w[n], new_m[n], new_v[n] = _adamw(weights[n], grad_w[n], given["m_" + n], given["v_" + n])
    return (loss, grad_x, *[grad_w[n] for n in TWIN_WEIGHTS], *[delta_w[n] for n in TWIN_WEIGHTS],
            *[new_m[n] for n in TWIN_WEIGHTS], *[new_v[n] for n in TWIN_WEIGHTS])
```

```python
import functools
import math

import jax
import jax.numpy as jnp
from jax import lax
from jax.experimental import pallas as pl
from jax.experimental.pallas import tpu as pltpu

F32 = jnp.float32
BF16 = jnp.bfloat16
MESH = pl.DeviceIdType.MESH

N_META = 16
NORM_EPS = 1e-6
HEAD = 128
ROPE = 64
ROPE_HALF = ROPE // 2
QK_PAD = 2 * HEAD
CHUNK = 16
ROPE_THETA = 10000.0
SEQ_BLOCK = 256
PAD_FRONT = SEQ_BLOCK - N_META
PACK_W = 1024
NEG = -1e30
VMEM_LIMIT = 56 * 1024 * 1024

ADAM_LR, ADAM_B1, ADAM_B2, ADAM_EPS, ADAM_WD, ADAM_STEP = 0.001, 0.9, 0.999, 1e-08, 0.01, 10

BIG = ("w_in", "w_hg_o", "w_q_b", "w_kv_b", "w_mla_o", "w_out", "w_ffn_in", "w_ffn_out")
COL_SHARDED = ("w_in", "w_q_b", "w_kv_b", "w_ffn_in")
SMALL = ("b_gate", "lb_logits", "hg_norm_g", "q_a_norm_g", "kv_a_norm_g", "mix_pre_g", "mix_post_g",
         "ffn_pre_g", "ffn_post_g")
WEIGHTS = ("meta_tokens", "w_in", "b_gate", "lb_logits", "hg_norm_g", "w_hg_o", "q_a_norm_g", "w_q_b",
           "kv_a_norm_g", "w_kv_b", "w_mla_o", "w_out", "mix_pre_g", "mix_post_g", "ffn_pre_g", "ffn_post_g",
           "w_ffn_in", "w_ffn_out")


def _tile(n, cap, unit=128):
    if n <= cap:
        return n
    best = None
    for t in range(unit, cap + 1, unit):
        if n % t == 0:
            best = t
    assert best is not None, (n, cap, unit)
    return best


def _sigmoid(x):
    return 1.0 / (1.0 + jnp.exp(-x))


def _bf(x):
    return x.astype(BF16)


def rowwise(name, fn, row_ins, seq_tabs, consts, row_outs, acc_outs=(), tm=SEQ_BLOCK, n_rows=None):
    t_rows = row_ins[0][0].shape[0] if n_rows is None else n_rows
    nt = t_rows // tm
    assert t_rows % tm == 0
    n_in = len(row_ins) + len(seq_tabs) + len(consts)
    n_row = len(row_outs)

    def body(*refs):
        vals = [r[...] for r in refs[:n_in]]
        res = fn(*vals)
        if not isinstance(res, (tuple, list)):
            res = (res,)
        outs = refs[n_in:]
        for k in range(n_row):
            outs[k][...] = res[k].astype(outs[k].dtype)
        if acc_outs:
            @pl.when(pl.program_id(0) == 0)
            def _():
                for k in range(len(acc_outs)):
                    outs[n_row + k][...] = jnp.zeros_like(outs[n_row + k])

            for k in range(len(acc_outs)):
                outs[n_row + k][...] += res[n_row + k]

    row_ins = [tuple(e) + (0,) * (4 - len(e)) for e in row_ins]
    in_specs = [pl.BlockSpec((tm, w), functools.partial(lambda i, j, ro: (i + ro, j), j=j, ro=ro))
                for (_, w, j, ro) in row_ins]
    for tab in seq_tabs:
        per = tab.shape[0] // tm
        in_specs.append(pl.BlockSpec((tm, tab.shape[1]), functools.partial(lambda i, per: (i % per, 0), per=per)))
    for c in consts:
        in_specs.append(pl.BlockSpec(c.shape, lambda i: (0, 0)))
    out_specs = [pl.BlockSpec((tm, w), lambda i: (i, 0)) for (w, _) in row_outs]
    out_specs += [pl.BlockSpec(s, lambda i: (0, 0)) for s in acc_outs]
    out_shape = [jax.ShapeDtypeStruct((t_rows, w), dt) for (w, dt) in row_outs]
    out_shape += [jax.ShapeDtypeStruct(s, F32) for s in acc_outs]
    res = pl.pallas_call(
        body, name=name, grid=(nt,), in_specs=in_specs, out_specs=out_specs, out_shape=out_shape,
        compiler_params=pltpu.CompilerParams(dimension_semantics=("arbitrary",)),
    )(*[e[0] for e in row_ins], *seq_tabs, *consts)
    return res


def matmul(name, a, b, mode, out_dtype=F32, addend=None):
    if mode == "tn":
        kdim, m = a.shape
        n = b.shape[1]
        tm, tn, tk = _tile(m, 1024), _tile(n, 1536), _tile(kdim, 512)
        a_spec = pl.BlockSpec((tk, tm), lambda i, j, k: (k, i))
        b_spec = pl.BlockSpec((tk, tn), lambda i, j, k: (k, j))
        dims = (((0,), (0,)), ((), ()))
    else:
        m, kdim = a.shape
        n = b.shape[1] if mode == "nn" else b.shape[0]
        tn, tk = _tile(n, 1536), _tile(kdim, 1536)
        tm = _tile(m, 1024 if tn <= 1024 else 512)
        a_spec = pl.BlockSpec((tm, tk), lambda i, j, k: (i, k))
        if mode == "nn":
            b_spec = pl.BlockSpec((tk, tn), lambda i, j, k: (k, j))
            dims = (((1,), (0,)), ((), ()))
        else:
            b_spec = pl.BlockSpec((tn, tk), lambda i, j, k: (j, k))
            dims = (((1,), (1,)), ((), ()))
    nk = kdim // tk
    has_add = addend is not None

    def body(*refs):
        a_ref, b_ref = refs[0], refs[1]
        add_ref = refs[2] if has_add else None
        o_ref, acc_ref = refs[-2], refs[-1]
        k = pl.program_id(2)

        @pl.when(k == 0)
        def _():
            acc_ref[...] = jnp.zeros_like(acc_ref)

        acc_ref[...] += lax.dot_general(a_ref[...], b_ref[...], dims, preferred_element_type=F32)

        @pl.when(k == nk - 1)
        def _():
            r = acc_ref[...]
            if has_add:
                r = r + add_ref[...]
            o_ref[...] = r.astype(o_ref.dtype)

    in_specs = [a_spec, b_spec]
    args = [a, b]
    if has_add:
        in_specs.append(pl.BlockSpec((tm, tn), lambda i, j, k: (i, j)))
        args.append(addend)
    return pl.pallas_call(
        body, name=name, grid=(m // tm, n // tn, nk), in_specs=in_specs,
        out_specs=pl.BlockSpec((tm, tn), lambda i, j, k: (i, j)),
        out_shape=jax.ShapeDtypeStruct((m, n), out_dtype),
        scratch_shapes=[pltpu.VMEM((tm, tn), F32)],
        compiler_params=pltpu.CompilerParams(dimension_semantics=("arbitrary", "arbitrary", "arbitrary"),
                                             vmem_limit_bytes=VMEM_LIMIT),
    )(*args)


def _rms(x, g):
    r = lax.rsqrt(jnp.mean(x * x, axis=-1, keepdims=True) + NORM_EPS)
    return x * r * g


def _rms_bwd(x, g, dy):
    r = lax.rsqrt(jnp.mean(x * x, axis=-1, keepdims=True) + NORM_EPS)
    xh = x * r
    dyg = dy * g
    dx = r * (dyg - xh * jnp.mean(dyg * xh, axis=-1, keepdims=True))
    return dx, jnp.sum(dy * xh, axis=0, keepdims=True)


def _silu(x):
    return x * _sigmoid(x)


def _silu_grad(x):
    s = _sigmoid(x)
    return s * (1.0 + x * (1.0 - s))


def _rope(xs, cos, s_up, s_dn):
    return xs * cos + pltpu.roll(xs, ROPE_HALF, 1) * s_up + pltpu.roll(xs, HEAD - ROPE_HALF, 1) * s_dn


def _rope_bwd(dy, cos, s_up, s_dn):
    return dy * cos + pltpu.roll(dy * s_up, HEAD - ROPE_HALF, 1) + pltpu.roll(dy * s_dn, ROPE_HALF, 1)


def _hg_chunk(s_in, hq, hf, hi, lb):
    rows = lax.broadcasted_iota(jnp.int32, (CHUNK, 1), 0)
    tril = (lax.broadcasted_iota(jnp.int32, (CHUNK, CHUNK), 1)
            <= lax.broadcasted_iota(jnp.int32, (CHUNK, CHUNK), 0)).astype(F32)
    q = _silu(hq)
    fg = lb + (1.0 - lb) * _sigmoid(hf)
    k = 1.0 - fg
    v = hi
    g = jnp.log(fg)
    b = jnp.dot(tril, g, precision=lax.Precision.HIGHEST, preferred_element_type=F32)
    o = jnp.dot(_bf(q * jnp.exp(b)), _bf(s_in), preferred_element_type=F32)
    qk_all = None
    for t in range(CHUNK):
        sel = rows == t
        b_t = jnp.sum(jnp.where(sel, b, 0.0), axis=0, keepdims=True)
        q_t = jnp.sum(jnp.where(sel, q, 0.0), axis=0, keepdims=True)
        dec = jnp.exp(jnp.where(rows <= t, b_t - b, -jnp.inf))
        a_t = jnp.sum(dec * (q_t * k), axis=1, keepdims=True)
        o_t = jnp.sum(a_t * v, axis=0, keepdims=True)
        o = o + jnp.where(sel, o_t, 0.0)
    b_last = jnp.sum(jnp.where(rows == CHUNK - 1, b, 0.0), axis=0, keepdims=True)
    kdec = k * jnp.exp(b_last - b)
    eye = (lax.broadcasted_iota(jnp.int32, (HEAD, HEAD), 0) == lax.broadcasted_iota(jnp.int32, (HEAD, HEAD), 1))
    e_col = jnp.sum(jnp.where(eye, jnp.exp(b_last), 0.0), axis=1, keepdims=True)
    s_out = e_col * s_in + lax.dot_general(_bf(kdec), _bf(v), (((0,), (0,)), ((), ())),
                                           preferred_element_type=F32)
    return s_out, o


def hgrn_fwd(proj_main, lb, bl, lp, d):
    nh = d // HEAD
    nb = lp // SEQ_BLOCK
    cpb = SEQ_BLOCK // CHUNK
    pad_chunks = PAD_FRONT // CHUNK

    def body(hq_ref, hf_ref, hi_ref, lb_ref, o_ref, st_ref, s_ref):
        j = pl.program_id(2)

        @pl.when(j == 0)
        def _():
            s_ref[...] = jnp.zeros_like(s_ref)
            o_ref[...] = jnp.zeros_like(o_ref)

        lbv = lb_ref[...]

        def chunk(c, carry):
            r = pl.multiple_of(c * CHUNK, CHUNK)
            s_in = s_ref[...]
            st_ref[0, 0, pl.ds(c, 1)] = s_in[None]
            s_out, o = _hg_chunk(s_in, hq_ref[pl.ds(r, CHUNK), :], hf_ref[pl.ds(r, CHUNK), :],
                                 hi_ref[pl.ds(r, CHUNK), :], lbv)
            s_ref[...] = s_out
            o_ref[pl.ds(r, CHUNK), :] = o
            return carry

        lax.fori_loop(jnp.where(j == 0, pad_chunks, 0), cpb, chunk, 0)

    def colspec(off):
        return pl.BlockSpec((SEQ_BLOCK, HEAD), functools.partial(lambda h, b, j, off: (b * nb + j, off + h), off=off))

    return pl.pallas_call(
        body, name="hgrn_fwd", grid=(nh, bl, nb),
        in_specs=[colspec(0), colspec(nh), colspec(2 * nh), pl.BlockSpec((1, HEAD), lambda h, b, j: (0, h))],
        out_specs=[pl.BlockSpec((SEQ_BLOCK, HEAD), lambda h, b, j: (b * nb + j, h)),
                   pl.BlockSpec((1, 1, cpb, HEAD, HEAD), lambda h, b, j: (b, h, j, 0, 0))],
        out_shape=[jax.ShapeDtypeStruct((bl * lp, d), F32),
                   jax.ShapeDtypeStruct((bl, nh, lp // CHUNK, HEAD, HEAD), F32)],
        scratch_shapes=[pltpu.VMEM((HEAD, HEAD), F32)],
        compiler_params=pltpu.CompilerParams(dimension_semantics=("arbitrary", "arbitrary", "arbitrary")),
    )(proj_main, proj_main, proj_main, lb)


def hgrn_bwd(proj_main, lb, states, do_scan, bl, lp, d):
    nh = d // HEAD
    nb = lp // SEQ_BLOCK
    cpb = SEQ_BLOCK // CHUNK
    pad_chunks = PAD_FRONT // CHUNK

    def body(hq_ref, hf_ref, hi_ref, lb_ref, st_ref, do_ref, dq_ref, df_ref, di_ref, dlb_ref, ds_ref):
        b_id, j = pl.program_id(1), pl.program_id(2)
        blk = nb - 1 - j

        @pl.when(j == 0)
        def _():
            ds_ref[...] = jnp.zeros_like(ds_ref)

        @pl.when((j == 0) & (b_id == 0))
        def _():
            dlb_ref[...] = jnp.zeros_like(dlb_ref)

        @pl.when(blk == 0)
        def _():
            dq_ref[...] = jnp.zeros_like(dq_ref)
            df_ref[...] = jnp.zeros_like(df_ref)
            di_ref[...] = jnp.zeros_like(di_ref)

        lbv = lb_ref[...]

        def chunk(n, carry):
            c = cpb - 1 - n
            r = pl.multiple_of(c * CHUNK, CHUNK)
            s_in = st_ref[0, 0, pl.ds(c, 1)][0]
            args = (s_in, hq_ref[pl.ds(r, CHUNK), :], hf_ref[pl.ds(r, CHUNK), :], hi_ref[pl.ds(r, CHUNK), :], lbv)
            _, vjp = jax.vjp(_hg_chunk, *args)
            ds_in, dhq, dhf, dhi, dlb = vjp((ds_ref[...], do_ref[pl.ds(r, CHUNK), :]))
            ds_ref[...] = ds_in
            dq_ref[pl.ds(r, CHUNK), :] = dhq.astype(dq_ref.dtype)
            df_ref[pl.ds(r, CHUNK), :] = dhf.astype(df_ref.dtype)
            di_ref[pl.ds(r, CHUNK), :] = dhi.astype(di_ref.dtype)
            dlb_ref[...] += dlb
            return carry

        lax.fori_loop(0, jnp.where(blk == 0, cpb - pad_chunks, cpb), chunk, 0)

    def colspec(off):
        return pl.BlockSpec((SEQ_BLOCK, HEAD),
                            functools.partial(lambda h, b, j, off: (b * nb + nb - 1 - j, off + h), off=off))

    t_rows = bl * lp
    return pl.pallas_call(
        body, name="hgrn_bwd", grid=(nh, bl, nb),
        in_specs=[colspec(0), colspec(nh), colspec(2 * nh), pl.BlockSpec((1, HEAD), lambda h, b, j: (0, h)),
                  pl.BlockSpec((1, 1, cpb, HEAD, HEAD), lambda h, b, j: (b, h, nb - 1 - j, 0, 0)),
                  colspec(0)],
        out_specs=[colspec(0), colspec(0), colspec(0), pl.BlockSpec((1, HEAD), lambda h, b, j: (0, h))],
        out_shape=[jax.ShapeDtypeStruct((t_rows, d), BF16)] * 3 + [jax.ShapeDtypeStruct((1, d), F32)],
        scratch_shapes=[pltpu.VMEM((HEAD, HEAD), F32)],
        compiler_params=pltpu.CompilerParams(dimension_semantics=("arbitrary", "arbitrary", "arbitrary")),
    )(proj_main, proj_main, proj_main, lb, states, do_scan)


def _allowed(row0, col0, nr, nc, transposed=False):
    if transposed:
        col = col0 + lax.broadcasted_iota(jnp.int32, (nc, 1), 0)
        row = row0 + lax.broadcasted_iota(jnp.int32, (1, nr), 1)
    else:
        row = row0 + lax.broadcasted_iota(jnp.int32, (nr, 1), 0)
        col = col0 + lax.broadcasted_iota(jnp.int32, (1, nc), 1)
    return (col <= row) & ((col >= PAD_FRONT) | (row < PAD_FRONT))


NT_DIMS = (((1,), (1,)), ((), ()))


def attn_fwd(q_cat, k_cat, v, bl, lp, nm, scale):
    tq = tk = SEQ_BLOCK
    nq = lp // tq

    def body(q_ref, k_ref, v_ref, o_ref, lse_ref, m_ref, l_ref, acc_ref):
        i = pl.program_id(2)
        q = q_ref[...]
        m_ref[...] = jnp.full_like(m_ref, NEG)
        l_ref[...] = jnp.zeros_like(l_ref)
        acc_ref[...] = jnp.zeros_like(acc_ref)

        def kstep(c, carry):
            c0 = pl.multiple_of(c * tk, tk)
            s = lax.dot_general(q, k_ref[pl.ds(c0, tk), :], NT_DIMS, preferred_element_type=F32) * scale
            s = jnp.where(_allowed(i * tq, c * tk, tq, tk), s, NEG)
            m_old = m_ref[...]
            m_new = jnp.maximum(m_old, jnp.max(s, axis=1, keepdims=True))
            alpha = jnp.exp(m_old - m_new)
            p = jnp.exp(s - m_new)
            l_ref[...] = alpha * l_ref[...] + jnp.sum(p, axis=1, keepdims=True)
            acc_ref[...] = alpha * acc_ref[...] + jnp.dot(_bf(p), v_ref[pl.ds(c0, tk), :],
                                                          preferred_element_type=F32)
            m_ref[...] = m_new
            return carry

        lax.fori_loop(0, i + 1, kstep, 0)
        o_ref[...] = (acc_ref[...] / l_ref[...]).astype(o_ref.dtype)
        lse_ref[0, 0] = m_ref[...] + jnp.log(l_ref[...])

    return pl.pallas_call(
        body, name="attn_fwd", grid=(bl, nm, nq),
        in_specs=[pl.BlockSpec((tq, QK_PAD), lambda b, h, i: (b * nq + i, h)),
                  pl.BlockSpec((lp, QK_PAD), lambda b, h, i: (b, h)),
                  pl.BlockSpec((lp, HEAD), lambda b, h, i: (b, h))],
        out_specs=[pl.BlockSpec((tq, HEAD), lambda b, h, i: (b * nq + i, h)),
                   pl.BlockSpec((1, 1, tq, 1), lambda b, h, i: (b, h, i, 0))],
        out_shape=[jax.ShapeDtypeStruct((bl * lp, nm * HEAD), BF16),
                   jax.ShapeDtypeStruct((bl, nm, lp, 1), F32)],
        scratch_shapes=[pltpu.VMEM((tq, 1), F32), pltpu.VMEM((tq, 1), F32), pltpu.VMEM((tq, HEAD), F32)],
        compiler_params=pltpu.CompilerParams(dimension_semantics=("arbitrary", "arbitrary", "arbitrary")),
    )(q_cat, k_cat, v)


def attn_bwd_dq(q_cat, k_cat, v, o, do, lse, bl, lp, nm, scale):
    tq = tk = SEQ_BLOCK
    nq = lp // tq

    def body(q_ref, k_ref, v_ref, o_ref, do_ref, lse_ref, dq_ref, dl_ref, acc_ref):
        i = pl.program_id(2)
        q = q_ref[...]
        do_b = do_ref[...]
        delta = jnp.sum(o_ref[...].astype(F32) * do_b.astype(F32), axis=1, keepdims=True)
        lse_b = lse_ref[0, 0]
        acc_ref[...] = jnp.zeros_like(acc_ref)

        def kstep(c, carry):
            c0 = pl.multiple_of(c * tk, tk)
            ks = k_ref[pl.ds(c0, tk), :]
            s = lax.dot_general(q, ks, NT_DIMS, preferred_element_type=F32) * scale
            p = jnp.where(_allowed(i * tq, c * tk, tq, tk), jnp.exp(s - lse_b), 0.0)
            dp = lax.dot_general(do_b, v_ref[pl.ds(c0, tk), :], NT_DIMS, preferred_element_type=F32)
            ds = p * (dp - delta)
            acc_ref[...] += jnp.dot(_bf(ds), ks, preferred_element_type=F32)
            return carry

        lax.fori_loop(0, i + 1, kstep, 0)
        dq_ref[...] = acc_ref[...] * scale
        dl_ref[0, 0] = delta

    return pl.pallas_call(
        body, name="attn_bwd_dq", grid=(bl, nm, nq),
        in_specs=[pl.BlockSpec((tq, QK_PAD), lambda b, h, i: (b * nq + i, h)),
                  pl.BlockSpec((lp, QK_PAD), lambda b, h, i: (b, h)),
                  pl.BlockSpec((lp, HEAD), lambda b, h, i: (b, h)),
                  pl.BlockSpec((tq, HEAD), lambda b, h, i: (b * nq + i, h)),
                  pl.BlockSpec((tq, HEAD), lambda b, h, i: (b * nq + i, h)),
                  pl.BlockSpec((1, 1, tq, 1), lambda b, h, i: (b, h, i, 0))],
        out_specs=[pl.BlockSpec((tq, QK_PAD), lambda b, h, i: (b * nq + i, h)),
                   pl.BlockSpec((1, 1, tq, 1), lambda b, h, i: (b, h, i, 0))],
        out_shape=[jax.ShapeDtypeStruct((bl * lp, nm * QK_PAD), F32),
                   jax.ShapeDtypeStruct((bl, nm, lp, 1), F32)],
        scratch_shapes=[pltpu.VMEM((tq, QK_PAD), F32)],
        compiler_params=pltpu.CompilerParams(dimension_semantics=("arbitrary", "arbitrary", "arbitrary")),
    )(q_cat, k_cat, v, o, do, lse)


def attn_bwd_dkv(q_cat, k_cat, v, do, lse_row, delta_row, bl, lp, nm, scale):
    tq = tk = SEQ_BLOCK
    nq = lp // tq

    def body(q_ref, k_ref, v_ref, do_ref, lse_ref, dl_ref, dk_ref, dv_ref):
        i = pl.program_id(2)
        kt = k_ref[...]
        vt = v_ref[...]
        dk_ref[...] = jnp.zeros_like(dk_ref)
        dv_ref[...] = jnp.zeros_like(dv_ref)

        def qstep(c, carry):
            c0 = pl.multiple_of(c * tq, tq)
            qs = q_ref[pl.ds(c0, tq), :]
            dos = do_ref[pl.ds(c0, tq), :]
            st = lax.dot_general(kt, qs, NT_DIMS, preferred_element_type=F32) * scale
            pt = jnp.where(_allowed(c * tq, i * tk, tq, tk, transposed=True),
                           jnp.exp(st - lse_ref[0, 0, pl.ds(c, 1)][0]), 0.0)
            dv_ref[...] += jnp.dot(_bf(pt), dos, preferred_element_type=F32)
            dpt = lax.dot_general(vt, dos, NT_DIMS, preferred_element_type=F32)
            dst = pt * (dpt - dl_ref[0, 0, pl.ds(c, 1)][0])
            dk_ref[...] += jnp.dot(_bf(dst), qs, preferred_element_type=F32)
            return carry

        lax.fori_loop(i, nq, qstep, 0)
        dk_ref[...] = dk_ref[...] * scale

    return pl.pallas_call(
        body, name="attn_bwd_dkv", grid=(bl, nm, nq),
        in_specs=[pl.BlockSpec((lp, QK_PAD), lambda b, h, i: (b, h)),
                  pl.BlockSpec((tk, QK_PAD), lambda b, h, i: (b * nq + i, h)),
                  pl.BlockSpec((tk, HEAD), lambda b, h, i: (b * nq + i, h)),
                  pl.BlockSpec((lp, HEAD), lambda b, h, i: (b, h)),
                  pl.BlockSpec((1, 1, nq, 1, tq), lambda b, h, i: (b, h, 0, 0, 0)),
                  pl.BlockSpec((1, 1, nq, 1, tq), lambda b, h, i: (b, h, 0, 0, 0))],
        out_specs=[pl.BlockSpec((tk, QK_PAD), lambda b, h, i: (b * nq + i, h)),
                   pl.BlockSpec((tk, HEAD), lambda b, h, i: (b * nq + i, h))],
        out_shape=[jax.ShapeDtypeStruct((bl * lp, nm * QK_PAD), F32),
                   jax.ShapeDtypeStruct((bl * lp, nm * HEAD), F32)],
        compiler_params=pltpu.CompilerParams(dimension_semantics=("arbitrary", "arbitrary", "arbitrary")),
    )(q_cat, k_cat, v, do, lse_row, delta_row)


def _place():
    return lax.axis_index("x"), lax.axis_index("y"), lax.axis_index("c")


def gather_shards(packed):
    hbm = pl.BlockSpec(memory_space=pl.ANY)

    def body(src_ref, out_ref, send_sems, recv_sems, local_sem):
        x, y, c = _place()
        me = 2 * x + y
        chips = [(1 - x, y), (x, 1 - y), (1 - x, 1 - y)]
        local = pltpu.make_async_copy(src_ref, out_ref.at[me], local_sem)
        local.start()
        sends = []
        for k, (px, py) in enumerate(chips):
            cp = pltpu.make_async_remote_copy(src_ref=src_ref, dst_ref=out_ref.at[me], send_sem=send_sems.at[k],
                                              recv_sem=recv_sems.at[k], device_id=(px, py, c), device_id_type=MESH)
            cp.start()
            sends.append(cp)
        for k, (px, py) in enumerate(chips):
            pltpu.make_async_remote_copy(src_ref=src_ref, dst_ref=out_ref.at[2 * px + py], send_sem=send_sems.at[k],
                                         recv_sem=recv_sems.at[k], device_id=(px, py, c),
                                         device_id_type=MESH).wait_recv()
        for cp in sends:
            cp.wait_send()
        local.wait()

    return pl.pallas_call(
        body, name="gather_shards", in_specs=[hbm], out_specs=hbm,
        out_shape=jax.ShapeDtypeStruct((4,) + packed.shape, packed.dtype),
        scratch_shapes=[pltpu.SemaphoreType.DMA((3,)), pltpu.SemaphoreType.DMA((3,)), pltpu.SemaphoreType.DMA],
    )(packed)


def exchange_grads(send, small):
    hbm = pl.BlockSpec(memory_space=pl.ANY)

    def body(send_ref, small_ref, recv_ref, all_ref, send_sems, recv_sems, ssend_sems, srecv_sems, local_sem):
        x, y, c = _place()
        me = 4 * x + 2 * y + c
        chips = [(1 - x, y), (x, 1 - y), (1 - x, 1 - y)]
        local = pltpu.make_async_copy(small_ref, all_ref.at[me], local_sem)
        local.start()
        sends = []
        for k, (px, py) in enumerate(chips):
            cp = pltpu.make_async_remote_copy(src_ref=send_ref.at[2 * px + py], dst_ref=recv_ref.at[k],
                                              send_sem=send_sems.at[k], recv_sem=recv_sems.at[k],
                                              device_id=(px, py, c), device_id_type=MESH)
            cp.start()
            sends.append(cp)
        others = [(x ^ ((r >> 2) & 1), y ^ ((r >> 1) & 1), c ^ (r & 1)) for r in range(1, 8)]
        for r, peer in enumerate(others):
            cp = pltpu.make_async_remote_copy(src_ref=small_ref, dst_ref=all_ref.at[me], send_sem=ssend_sems.at[r],
                                              recv_sem=srecv_sems.at[r], device_id=peer, device_id_type=MESH)
            cp.start()
            sends.append(cp)
        for k, (px, py) in enumerate(chips):
            pltpu.make_async_remote_copy(src_ref=send_ref.at[2 * px + py], dst_ref=recv_ref.at[k],
                                         send_sem=send_sems.at[k], recv_sem=recv_sems.at[k],
                                         device_id=(px, py, c), device_id_type=MESH).wait_recv()
        for r, (px, py, pc) in enumerate(others):
            pltpu.make_async_remote_copy(src_ref=small_ref, dst_ref=all_ref.at[4 * px + 2 * py + pc],
                                         send_sem=ssend_sems.at[r], recv_sem=srecv_sems.at[r],
                                         device_id=(px, py, pc), device_id_type=MESH).wait_recv()
        for cp in sends:
            cp.wait_send()
        local.wait()

    return pl.pallas_call(
        body, name="exchange_grads", in_specs=[hbm, hbm], out_specs=[hbm, hbm],
        out_shape=[jax.ShapeDtypeStruct((3,) + send.shape[1:], send.dtype),
                   jax.ShapeDtypeStruct((8,) + small.shape, small.dtype)],
        scratch_shapes=[pltpu.SemaphoreType.DMA((3,)), pltpu.SemaphoreType.DMA((3,)),
                        pltpu.SemaphoreType.DMA((7,)), pltpu.SemaphoreType.DMA((7,)), pltpu.SemaphoreType.DMA],
    )(send, small)


def swap_with_sibling(part):
    hbm = pl.BlockSpec(memory_space=pl.ANY)

    def body(src_ref, out_ref, send_sem, recv_sem):
        x, y, c = _place()
        cp = pltpu.make_async_remote_copy(src_ref=src_ref, dst_ref=out_ref, send_sem=send_sem, recv_sem=recv_sem,
                                          device_id=(x, y, 1 - c), device_id_type=MESH)
        cp.start()
        cp.wait()

    return pl.pallas_call(
        body, name="swap_with_sibling", in_specs=[hbm], out_specs=hbm,
        out_shape=jax.ShapeDtypeStruct(part.shape, part.dtype),
        scratch_shapes=[pltpu.SemaphoreType.DMA, pltpu.SemaphoreType.DMA],
    )(part)


def adamw(name, w, g, m, v):
    r, c = w.shape
    tr = r if r * c <= 65536 else _tile(r, 128, 8)

    def body(w_ref, g_ref, m_ref, v_ref, d_ref, nm_ref, nv_ref):
        gv = g_ref[...]
        m_new = ADAM_B1 * m_ref[...] + (1.0 - ADAM_B1) * gv
        v_new = ADAM_B2 * v_ref[...] + (1.0 - ADAM_B2) * (gv * gv)
        m_hat = m_new / (1.0 - ADAM_B1 ** ADAM_STEP)
        v_hat = v_new / (1.0 - ADAM_B2 ** ADAM_STEP)
        d_ref[...] = -ADAM_LR * (m_hat / (jnp.sqrt(v_hat) + ADAM_EPS) + ADAM_WD * w_ref[...])
        nm_ref[...] = m_new
        nv_ref[...] = v_new

    spec = pl.BlockSpec((tr, c), lambda i: (i, 0))
    return pl.pallas_call(
        body, name=name, grid=(r // tr,), in_specs=[spec] * 4, out_specs=[spec] * 3,
        out_shape=[jax.ShapeDtypeStruct((r, c), F32)] * 3,
        compiler_params=pltpu.CompilerParams(dimension_semantics=("arbitrary",)),
    )(w, g, m, v)


def _pack_rows(shapes):
    rows = [(r * c) // PACK_W for (r, c) in shapes]
    for (r, c) in shapes:
        assert (r * c) % PACK_W == 0
    total = sum(rows)
    return rows, -(-total // 16) * 16


def pack_shard(parts, total_rows):
    flat = jnp.concatenate([p.reshape(-1, PACK_W) for p in parts], axis=0)
    return jnp.pad(flat, ((0, total_rows - flat.shape[0]), (0, 0)))


def split_full(name, full, s):
    if name in COL_SHARDED:
        c = full.shape[1] // 4
        return full[:, s * c:(s + 1) * c]
    r = full.shape[0] // 4
    return full[s * r:(s + 1) * r]


def join_shards(name, shards):
    return jnp.concatenate(shards, axis=1 if name in COL_SHARDED else 0)


def kernel(x, meta_tokens, w_in, b_gate, lb_logits, hg_norm_g, w_hg_o, q_a_norm_g, w_q_b, kv_a_norm_g, w_kv_b, w_mla_o, w_out, mix_pre_g, mix_post_g, ffn_pre_g, ffn_post_g, w_ffn_in, w_ffn_out, loss_target, m_meta_tokens, m_w_in, m_b_gate, m_lb_logits, m_hg_norm_g, m_w_hg_o, m_q_a_norm_g, m_w_q_b, m_kv_a_norm_g, m_w_kv_b, m_w_mla_o, m_w_out, m_mix_pre_g, m_mix_post_g, m_ffn_pre_g, m_ffn_post_g, m_w_ffn_in, m_w_ffn_out, v_meta_tokens, v_w_in, v_b_gate, v_lb_logits, v_hg_norm_g, v_w_hg_o, v_q_a_norm_g, v_w_q_b, v_kv_a_norm_g, v_w_kv_b, v_w_mla_o, v_w_out, v_mix_pre_g, v_mix_post_g, v_ffn_pre_g, v_ffn_post_g, v_w_ffn_in, v_w_ffn_out):
    wts = dict(meta_tokens=meta_tokens, w_in=w_in[0], b_gate=b_gate, lb_logits=lb_logits, hg_norm_g=hg_norm_g,
               w_hg_o=w_hg_o[0], q_a_norm_g=q_a_norm_g, w_q_b=w_q_b[0], kv_a_norm_g=kv_a_norm_g, w_kv_b=w_kv_b[0],
               w_mla_o=w_mla_o[0], w_out=w_out[0], mix_pre_g=mix_pre_g, mix_post_g=mix_post_g, ffn_pre_g=ffn_pre_g,
               ffn_post_g=ffn_post_g, w_ffn_in=w_ffn_in[0], w_ffn_out=w_ffn_out[0])
    mom_m = dict(meta_tokens=m_meta_tokens, w_in=m_w_in[0], b_gate=m_b_gate, lb_logits=m_lb_logits,
                 hg_norm_g=m_hg_norm_g, w_hg_o=m_w_hg_o[0], q_a_norm_g=m_q_a_norm_g, w_q_b=m_w_q_b[0],
                 kv_a_norm_g=m_kv_a_norm_g, w_kv_b=m_w_kv_b[0], w_mla_o=m_w_mla_o[0], w_out=m_w_out[0],
                 mix_pre_g=m_mix_pre_g, mix_post_g=m_mix_post_g, ffn_pre_g=m_ffn_pre_g, ffn_post_g=m_ffn_post_g,
                 w_ffn_in=m_w_ffn_in[0], w_ffn_out=m_w_ffn_out[0])
    mom_v = dict(meta_tokens=v_meta_tokens, w_in=v_w_in[0], b_gate=v_b_gate, lb_logits=v_lb_logits,
                 hg_norm_g=v_hg_norm_g, w_hg_o=v_w_hg_o[0], q_a_norm_g=v_q_a_norm_g, w_q_b=v_w_q_b[0],
                 kv_a_norm_g=v_kv_a_norm_g, w_kv_b=v_w_kv_b[0], w_mla_o=v_w_mla_o[0], w_out=v_w_out[0],
                 mix_pre_g=v_mix_pre_g, mix_post_g=v_mix_post_g, ffn_pre_g=v_ffn_pre_g, ffn_post_g=v_ffn_post_g,
                 w_ffn_in=v_w_ffn_in[0], w_ffn_out=v_w_ffn_out[0])

    bl, seq, d = x.shape
    lp = PAD_FRONT + N_META + seq
    t_rows = bl * lp
    nh = d // HEAD
    ql, kvl = wts["w_q_b"].shape[0], wts["w_kv_b"].shape[0]
    nm = (4 * wts["w_mla_o"].shape[0]) // HEAD
    ffn = 4 * wts["w_ffn_out"].shape[0]
    mla_w = ql + kvl + HEAD
    assert ql == kvl and ql % HEAD == 0 and seq % SEQ_BLOCK == 0 and d % HEAD == 0
    scale = (HEAD + ROPE) ** -0.5
    my_chip = 2 * lax.axis_index("x") + lax.axis_index("y")

    shard_shapes = [wts[n].shape for n in BIG]
    pack_rows, pack_total = _pack_rows(shard_shapes)
    mcols = meta_tokens.shape[1]
    meta_rows = (N_META * mcols * 2) // PACK_W
    assert (N_META * mcols * 2) % PACK_W == 0
    meta_bits = lax.bitcast_convert_type(meta_tokens, BF16).reshape(meta_rows, PACK_W)
    gathered = gather_shards(pack_shard([meta_bits] + [_bf(wts[n]) for n in BIG],
                                        -(-(meta_rows + sum(pack_rows)) // 16) * 16))
    meta_full = jnp.concatenate(
        [lax.bitcast_convert_type(gathered[s, :meta_rows].reshape(N_META, mcols, 2), F32) for s in range(4)], axis=1)
    full = {}
    off = meta_rows
    for n, rows, (r, c) in zip(BIG, pack_rows, shard_shapes):
        full[n] = join_shards(n, [gathered[s, off:off + rows].reshape(r, c) for s in range(4)])
        off += rows
    w_main = jnp.concatenate([full["w_in"][:, :4 * d], full["w_in"][:, -2 * d:]], axis=1)
    w_mla = jnp.pad(full["w_in"][:, 4 * d:4 * d + ql + kvl + ROPE], ((0, 0), (0, HEAD - ROPE)))
    w_qb = jnp.pad(full["w_q_b"].reshape(ql, nm, HEAD + ROPE), ((0, 0), (0, 0), (0, QK_PAD - HEAD - ROPE))
                   ).reshape(ql, nm * QK_PAD)
    w_kvb = full["w_kv_b"]

    h0 = jnp.concatenate([jnp.zeros((bl, PAD_FRONT, d), F32), jnp.broadcast_to(meta_full[None], (bl, N_META, d)), x],
                         axis=1).reshape(t_rows, d)
    tgt = jnp.concatenate([jnp.zeros((bl, PAD_FRONT + N_META, d), F32), loss_target], axis=1).reshape(t_rows, d)
    pos = (jnp.arange(lp, dtype=jnp.int32) - PAD_FRONT).astype(F32)
    inv_freq = 1.0 / (ROPE_THETA ** (jnp.arange(0, ROPE, 2, dtype=F32) / ROPE))
    ang = pos[:, None] * inv_freq[None, :]
    zeros32 = jnp.zeros((lp, ROPE_HALF), F32)
    zeros64 = jnp.zeros((lp, HEAD - ROPE), F32)
    t_cos = jnp.concatenate([jnp.cos(ang), jnp.cos(ang), zeros64], axis=1)
    t_up = jnp.concatenate([zeros32, jnp.sin(ang), zeros64], axis=1)
    t_dn = jnp.concatenate([-jnp.sin(ang), zeros32, zeros64], axis=1)
    real = jnp.broadcast_to((jnp.arange(lp) >= PAD_FRONT + N_META).astype(F32)[:, None], (lp, d))
    lb_soft = jax.nn.softmax(lb_logits.astype(F32), axis=0)
    lb = lb_soft[0:1]

    (u1,) = rowwise("norm_mix_pre", lambda h, g: _rms(h, g), [(h0, d, 0)], [], [mix_pre_g], [(d, BF16)])
    proj_main = matmul("proj_main", u1, w_main, "nn")
    proj_mla = matmul("proj_mla", u1, w_mla, "nn")
    o_scan, states = hgrn_fwd(proj_main, lb, bl, lp, d)

    def hg_out_fn(o, hg, g):
        return jnp.concatenate([_rms(o[:, h * HEAD:(h + 1) * HEAD], g) for h in range(nh)], axis=1) * _silu(hg)

    (o_hg,) = rowwise("hgrn_out", hg_out_fn, [(o_scan, d, 0), (proj_main, d, 3)], [], [hg_norm_g], [(d, BF16)])
    y_a = matmul("y_a", o_hg, _bf(full["w_hg_o"]), "nn")

    qn, kvn = rowwise("mla_norms", lambda cq, ckv, gq, gk: (_rms(cq, gq), _rms(ckv, gk)),
                      [(proj_mla, ql, 0), (proj_mla, kvl, 1)], [], [q_a_norm_g, kv_a_norm_g],
                      [(ql, BF16), (kvl, BF16)])
    q_full = matmul("q_up", qn, w_qb, "nn")
    kv_full = matmul("kv_up", kvn, w_kvb, "nn")

    def mla_prep_fn(qf, kvf, kpe, cos, s_up, s_dn):
        kpe_r = _rope(kpe, cos, s_up, s_dn)
        qs, ks, vs = [], [], []
        for h in range(nm):
            qs += [qf[:, h * QK_PAD:h * QK_PAD + HEAD], _rope(qf[:, h * QK_PAD + HEAD:(h + 1) * QK_PAD], cos, s_up, s_dn)]
            ks += [kvf[:, h * QK_PAD:h * QK_PAD + HEAD], kpe_r]
            vs += [kvf[:, h * QK_PAD + HEAD:(h + 1) * QK_PAD]]
        return jnp.concatenate(qs, axis=1), jnp.concatenate(ks, axis=1), jnp.concatenate(vs, axis=1)

    kpe_blk = (ql + kvl) // HEAD
    q_cat, k_cat, v_att = rowwise("mla_prep", mla_prep_fn,
                                  [(q_full, nm * QK_PAD, 0), (kv_full, nm * QK_PAD, 0), (proj_mla, HEAD, kpe_blk)],
                                  [t_cos, t_up, t_dn], [], [(nm * QK_PAD, BF16), (nm * QK_PAD, BF16), (nm * HEAD, BF16)])
    o_mla, lse = attn_fwd(q_cat, k_cat, v_att, bl, lp, nm, scale)
    y_b = matmul("y_b", o_mla, _bf(full["w_mla_o"]), "nn")

    def gate_fn(ya, yb, ga, gb, bias):
        return _sigmoid(ga + bias[:, :d]) * ya + _sigmoid(gb + bias[:, d:]) * yb

    (z,) = rowwise("gate_mix", gate_fn, [(y_a, d, 0), (y_b, d, 0), (proj_main, d, 4), (proj_main, d, 5)], [],
                   [b_gate], [(d, BF16)])
    mixed = matmul("mixed", z, _bf(full["w_out"]), "nn")

    def mid_fn(h, mx, g_post, g_pre):
        h1 = h + _rms(mx, g_post)
        return h1, _rms(h1, g_pre)

    h1, u2 = rowwise("norm_mid", mid_fn, [(h0, d, 0), (mixed, d, 0)], [], [mix_post_g, ffn_pre_g],
                     [(d, F32), (d, BF16)])
    gu = matmul("ffn_in", u2, _bf(full["w_ffn_in"]), "nn")
    (act,) = rowwise("swiglu", lambda gt, up: _silu(gt) * up, [(gu, ffn, 0), (gu, ffn, 1)], [], [], [(ffn, BF16)])
    f_out = matmul("ffn_out", act, _bf(full["w_ffn_out"]), "nn")

    def loss_fn(h1v, fv, tg, realv, g_post):
        h2 = h1v + _rms(fv, g_post)
        diff = (h2 - tg) * realv
        part = jnp.broadcast_to(0.5 * jnp.sum(diff * diff, keepdims=True) / d, (1, HEAD))
        dy = diff / d
        df, dg = _rms_bwd(fv, g_post, dy)
        return dy, df, part, dg

    dy, df, loss_part, g_ffn_post = rowwise("loss_head", loss_fn, [(h1, d, 0), (f_out, d, 0), (tgt, d, 0)], [real],
                                            [ffn_post_g], [(d, F32), (d, BF16)], [(1, HEAD), (1, d)])
    grads = {}
    d_act = matmul("d_act", df, _bf(full["w_ffn_out"]), "nt")
    grads["w_ffn_out"] = matmul("gw_ffn_out", act, df, "tn")

    def swiglu_bwd_fn(gt, up, da):
        return jnp.concatenate([da * up * _silu_grad(gt), da * _silu(gt)], axis=1)

    (dgu,) = rowwise("swiglu_bwd", swiglu_bwd_fn, [(gu, ffn, 0), (gu, ffn, 1), (d_act, ffn, 0)], [], [],
                     [(2 * ffn, BF16)])
    du2 = matmul("d_u2", dgu, _bf(full["w_ffn_in"]), "nt")
    grads["w_ffn_in"] = matmul("gw_ffn_in", u2, dgu, "tn")

    def mid_bwd_fn(dyv, h1v, du2v, mx, g_pre, g_post):
        dx, dg_pre = _rms_bwd(h1v, g_pre, du2v)
        dh1 = dyv + dx
        dmx, dg_post = _rms_bwd(mx, g_post, dh1)
        return dh1, dmx, dg_pre, dg_post

    dh1, dmixed, g_ffn_pre, g_mix_post = rowwise("norm_mid_bwd", mid_bwd_fn,
                                                 [(dy, d, 0), (h1, d, 0), (du2, d, 0), (mixed, d, 0)], [],
                                                 [ffn_pre_g, mix_post_g], [(d, F32), (d, BF16)], [(1, d), (1, d)])
    dz = matmul("d_z", dmixed, _bf(full["w_out"]), "nt")
    grads["w_out"] = matmul("gw_out", z, dmixed, "tn")

    def gate_bwd_fn(dzv, ya, yb, ga, gb, bias):
        sa, sb = _sigmoid(ga + bias[:, :d]), _sigmoid(gb + bias[:, d:])
        dga = dzv * ya * sa * (1.0 - sa)
        dgb = dzv * yb * sb * (1.0 - sb)
        dgates = jnp.concatenate([dga, dgb], axis=1)
        return dzv * sa, dzv * sb, dgates, jnp.sum(dgates, axis=0, keepdims=True)

    dy_a, dy_b, dgates, g_b_gate = rowwise("gate_mix_bwd", gate_bwd_fn,
                                           [(dz, d, 0), (y_a, d, 0), (y_b, d, 0), (proj_main, d, 4), (proj_main, d, 5)],
                                           [], [b_gate], [(d, BF16), (d, BF16), (2 * d, BF16)], [(1, 2 * d)])
    do_hg = matmul("d_o_hg", dy_a, _bf(full["w_hg_o"]), "nt")
    grads["w_hg_o"] = matmul("gw_hg_o", o_hg, dy_a, "tn")
    do_mla = matmul("d_o_mla", dy_b, _bf(full["w_mla_o"]), "nt", out_dtype=BF16)
    grads["w_mla_o"] = matmul("gw_mla_o", o_mla, dy_b, "tn")

    def hg_out_bwd_fn(do, o, hg, g):
        sg = _silu(hg)
        dn = do * sg
        dos, dgs, ons = [], 0.0, []
        for h in range(nh):
            sl = slice(h * HEAD, (h + 1) * HEAD)
            dx, dg = _rms_bwd(o[:, sl], g, dn[:, sl])
            dos.append(dx)
            dgs = dgs + dg
            ons.append(_rms(o[:, sl], g))
        dhg = do * jnp.concatenate(ons, axis=1) * _silu_grad(hg)
        return jnp.concatenate(dos, axis=1), dhg, dgs

    do_scan, dhg, g_hg_norm = rowwise("hgrn_out_bwd", hg_out_bwd_fn, [(do_hg, d, 0), (o_scan, d, 0), (proj_main, d, 3)],
                                      [], [hg_norm_g], [(d, F32), (d, BF16)], [(1, HEAD)])
    dhq, dhf, dhi, g_lb = hgrn_bwd(proj_main, lb, states, do_scan, bl, lp, d)

    dq_cat, delta = attn_bwd_dq(q_cat, k_cat, v_att, o_mla, do_mla, lse, bl, lp, nm, scale)
    nq = lp // SEQ_BLOCK
    lse_row = lse.reshape(bl, nm, nq, 1, SEQ_BLOCK)
    delta_row = delta.reshape(bl, nm, nq, 1, SEQ_BLOCK)
    dk_cat, dv_att = attn_bwd_dkv(q_cat, k_cat, v_att, do_mla, lse_row, delta_row, bl, lp, nm, scale)

    def mla_prep_bwd_fn(dqc, dkc, dvv, cos, s_up, s_dn):
        dqs, dkvs, dkpe = [], [], 0.0
        for h in range(nm):
            dqs += [dqc[:, h * QK_PAD:h * QK_PAD + HEAD],
                    _rope_bwd(dqc[:, h * QK_PAD + HEAD:(h + 1) * QK_PAD], cos, s_up, s_dn)]
            dkvs += [dkc[:, h * QK_PAD:h * QK_PAD + HEAD], dvv[:, h * HEAD:(h + 1) * HEAD]]
            dkpe = dkpe + dkc[:, h * QK_PAD + HEAD:(h + 1) * QK_PAD]
        return jnp.concatenate(dqs, axis=1), jnp.concatenate(dkvs, axis=1), _rope_bwd(dkpe, cos, s_up, s_dn)

    dq_full, dkv_full, dkpe = rowwise("mla_prep_bwd", mla_prep_bwd_fn,
                                      [(dq_cat, nm * QK_PAD, 0), (dk_cat, nm * QK_PAD, 0), (dv_att, nm * HEAD, 0)],
                                      [t_cos, t_up, t_dn], [],
                                      [(nm * QK_PAD, BF16), (nm * QK_PAD, BF16), (HEAD, F32)])
    dqn = matmul("d_qn", dq_full, w_qb, "nt")
    g_wqb = matmul("gw_q_b", qn, dq_full, "tn")
    grads["w_q_b"] = g_wqb.reshape(ql, nm, QK_PAD)[:, :, :HEAD + ROPE].reshape(ql, nm * (HEAD + ROPE))
    dkvn = matmul("d_kvn", dkv_full, w_kvb, "nt")
    grads["w_kv_b"] = matmul("gw_kv_b", kvn, dkv_full, "tn")

    def mla_norms_bwd_fn(dqnv, dkvnv, cq, ckv, dkpev, gq, gk):
        dcq, dgq = _rms_bwd(cq, gq, dqnv)
        dckv, dgk = _rms_bwd(ckv, gk, dkvnv)
        return jnp.concatenate([dcq, dckv, dkpev], axis=1), dgq, dgk

    dmla, g_q_norm, g_kv_norm = rowwise("mla_norms_bwd", mla_norms_bwd_fn,
                                        [(dqn, ql, 0), (dkvn, kvl, 0), (proj_mla, ql, 0), (proj_mla, kvl, 1),
                                         (dkpe, HEAD, 0)], [], [q_a_norm_g, kv_a_norm_g],
                                        [(mla_w, BF16)], [(1, ql), (1, kvl)])

    w_main_bf = w_main
    pieces = [(dhq, w_main_bf[:, 0:d]), (dhf, w_main_bf[:, d:2 * d]), (dhi, w_main_bf[:, 2 * d:3 * d]),
              (dhg, w_main_bf[:, 3 * d:4 * d]), (dgates, w_main_bf[:, 4 * d:6 * d]), (dmla, w_mla)]
    du1 = None
    gw_parts = []
    for k, (dp, wp) in enumerate(pieces):
        du1 = matmul(f"d_u1_{k}", dp, wp, "nt", addend=du1)
        gw_parts.append(matmul(f"gw_in_{k}", u1, dp, "tn"))
    grads["w_in"] = jnp.concatenate(gw_parts[:4] + [gw_parts[5][:, :ql + kvl + ROPE], gw_parts[4]], axis=1)

    def first_bwd_fn(dh1v, h, du1v, g):
        dx, dg = _rms_bwd(h, g, du1v)
        return dh1v + dx, dg

    dh0, g_mix_pre = rowwise("norm_mix_pre_bwd", first_bwd_fn, [(dh1, d, 0), (h0, d, 0), (du1, d, 0)], [],
                             [mix_pre_g], [(d, F32)], [(1, d)])
    dh0 = dh0.reshape(bl, lp, d)
    grad_x = dh0[:, PAD_FRONT + N_META:]

    send = jnp.stack([pack_shard([split_full(n, grads[n], s) for n in BIG], pack_total) for s in range(4)])
    mine = lax.dynamic_index_in_dim(send, my_chip, axis=0, keepdims=False)
    p0 = lb_soft[0:1]
    g_lb_logits = jnp.concatenate([g_lb * p0 * (1.0 - p0), -g_lb * p0 * (1.0 - p0)], axis=0)

    def row_of(vec):
        return vec.reshape(-1, d) if vec.size >= d else jnp.pad(vec.reshape(1, -1), ((0, 0), (0, d - vec.size)))

    small_parts = dict(b_gate=g_b_gate, lb_logits=g_lb_logits, hg_norm_g=g_hg_norm, q_a_norm_g=g_q_norm,
                       kv_a_norm_g=g_kv_norm, mix_pre_g=g_mix_pre, mix_post_g=g_mix_post, ffn_pre_g=g_ffn_pre,
                       ffn_post_g=g_ffn_post)
    g_meta = jnp.sum(dh0[:, PAD_FRONT:PAD_FRONT + N_META], axis=0)
    small_rows = [row_of(small_parts[n]) for n in SMALL] + [row_of(g_meta)]
    n_small = sum(r.shape[0] for r in small_rows)
    small = jnp.pad(jnp.concatenate(small_rows, axis=0), ((0, -(-n_small // 8) * 8 - n_small), (0, 0)))
    recv, all_small = exchange_grads(_bf(send), small)

    rt = _tile(pack_total, 512, 16)
    recv2 = recv.reshape(3 * pack_total, PACK_W)
    (part_sum,) = rowwise("sum_chips", lambda a, r0, r1, r2: a + r0.astype(F32) + r1.astype(F32) + r2.astype(F32),
                          [(mine, PACK_W, 0)] + [(recv2, PACK_W, 0, k * (pack_total // rt)) for k in range(3)],
                          [], [], [(PACK_W, F32)], tm=rt)
    sib_sum = swap_with_sibling(part_sum)
    (g_packed,) = rowwise("sum_cores", lambda a, b: a + b, [(part_sum, PACK_W, 0), (sib_sum, PACK_W, 0)], [], [],
                          [(PACK_W, F32)], tm=rt)
    small_t = small.shape[0]

    def sum8_fn(*slabs):
        acc = slabs[0]
        for s in slabs[1:]:
            acc = acc + s
        return acc

    (g_small,) = rowwise("sum_small", sum8_fn, [(all_small.reshape(8 * small_t, d), d, 0, k) for k in range(8)],
                         [], [], [(d, F32)], tm=small_t, n_rows=small_t)

    g_final = {}
    off = 0
    for n, rows, (r, c) in zip(BIG, pack_rows, shard_shapes):
        g_final[n] = g_packed[off:off + rows].reshape(r, c)
        off += rows
    off = 0
    for n, part in zip(SMALL, small_rows[:-1]):
        rows = part.shape[0]
        g_final[n] = g_small[off:off + rows, :d].reshape(-1)[:wts[n].size].reshape(wts[n].shape)
        off += rows
    mcols = meta_tokens.shape[1]
    g_final["meta_tokens"] = lax.dynamic_slice_in_dim(g_small[off:off + N_META, :d], my_chip * mcols, mcols, axis=1)

    delta, new_m, new_v = {}, {}, {}
    for n in WEIGHTS:
        w2 = wts[n].reshape(-1, wts[n].shape[-1])
        dl, mn, vn = adamw("adamw_" + n, w2, g_final[n].reshape(w2.shape), mom_m[n].reshape(w2.shape),
                           mom_v[n].reshape(w2.shape))
        delta[n], new_m[n], new_v[n] = dl, mn, vn

    loss = lax.psum(loss_part[0, 0], ("x", "y", "c"))

    def shaped(n, a):
        return a.reshape((1,) + wts[n].shape) if n in BIG else a.reshape(wts[n].shape)

    return (loss, grad_x, *[shaped(n, g_final[n]) for n in WEIGHTS], *[shaped(n, delta[n]) for n in WEIGHTS],
            *[shaped(n, new_m[n]) for n in WEIGHTS], *[shaped(n, new_v[n]) for n in WEIGHTS])
```

```python
import functools
import math

import jax
import jax.numpy as jnp
from jax import lax
from jax.experimental import pallas as pl
from jax.experimental.pallas import tpu as pltpu

F32 = jnp.float32
BF16 = jnp.bfloat16
MESH = pl.DeviceIdType.MESH

N_META = 16
NORM_EPS = 1e-6
HEAD = 128
ROPE = 64
ROPE_HALF = ROPE // 2
QK_PAD = 2 * HEAD
CHUNK = 16
ROPE_THETA = 10000.0
SEQ_BLOCK = 256
PAD_FRONT = SEQ_BLOCK - N_META
PACK_W = 1024
NEG = -1e30
VMEM_LIMIT = 56 * 1024 * 1024

ADAM_LR, ADAM_B1, ADAM_B2, ADAM_EPS, ADAM_WD, ADAM_STEP = 0.001, 0.9, 0.999, 1e-08, 0.01, 10

BIG = ("w_in", "w_hg_o", "w_q_b", "w_kv_b", "w_mla_o", "w_out", "w_ffn_in", "w_ffn_out")
COL_SHARDED = ("w_in", "w_q_b", "w_kv_b", "w_ffn_in")
SMALL = ("b_gate", "lb_logits", "hg_norm_g", "q_a_norm_g", "kv_a_norm_g", "mix_pre_g", "mix_post_g",
         "ffn_pre_g", "ffn_post_g")
WEIGHTS = ("meta_tokens", "w_in", "b_gate", "lb_logits", "hg_norm_g", "w_hg_o", "q_a_norm_g", "w_q_b",
           "kv_a_norm_g", "w_kv_b", "w_mla_o", "w_out", "mix_pre_g", "mix_post_g", "ffn_pre_g", "ffn_post_g",
           "w_ffn_in", "w_ffn_out")


def _tile(n, cap, unit=128):
    if n <= cap:
        return n
    best = None
    for t in range(unit, cap + 1, unit):
        if n % t == 0:
            best = t
    assert best is not None, (n, cap, unit)
    return best


def _sigmoid(x):
    return 1.0 / (1.0 + jnp.exp(-x))


def _bf(x):
    return x.astype(BF16)


def rowwise(name, fn, row_ins, seq_tabs, consts, row_outs, acc_outs=(), tm=SEQ_BLOCK, n_rows=None):
    t_rows = row_ins[0][0].shape[0] if n_rows is None else n_rows
    nt = t_rows // tm
    assert t_rows % tm == 0
    n_in = len(row_ins) + len(seq_tabs) + len(consts)
    n_row = len(row_outs)

    def body(*refs):
        vals = [r[...] for r in refs[:n_in]]
        res = fn(*vals)
        if not isinstance(res, (tuple, list)):
            res = (res,)
        outs = refs[n_in:]
        for k in range(n_row):
            outs[k][...] = res[k].astype(outs[k].dtype)
        if acc_outs:
            @pl.when(pl.program_id(0) == 0)
            def _():
                for k in range(len(acc_outs)):
                    outs[n_row + k][...] = jnp.zeros_like(outs[n_row + k])

            for k in range(len(acc_outs)):
                outs[n_row + k][...] += res[n_row + k]

    row_ins = [tuple(e) + (0,) * (4 - len(e)) for e in row_ins]
    in_specs = [pl.BlockSpec((tm, w), functools.partial(lambda i, j, ro: (i + ro, j), j=j, ro=ro))
                for (_, w, j, ro) in row_ins]
    for tab in seq_tabs:
        per = tab.shape[0] // tm
        in_specs.append(pl.BlockSpec((tm, tab.shape[1]), functools.partial(lambda i, per: (i % per, 0), per=per)))
    for c in consts:
        in_specs.append(pl.BlockSpec(c.shape, lambda i: (0, 0)))
    out_specs = [pl.BlockSpec((tm, w), lambda i: (i, 0)) for (w, _) in row_outs]
    out_specs += [pl.BlockSpec(s, lambda i: (0, 0)) for s in acc_outs]
    out_shape = [jax.ShapeDtypeStruct((t_rows, w), dt) for (w, dt) in row_outs]
    out_shape += [jax.ShapeDtypeStruct(s, F32) for s in acc_outs]
    res = pl.pallas_call(
        body, name=name, grid=(nt,), in_specs=in_specs, out_specs=out_specs, out_shape=out_shape,
        compiler_params=pltpu.CompilerParams(dimension_semantics=("arbitrary",)),
    )(*[e[0] for e in row_ins], *seq_tabs, *consts)
    return res


def matmul(name, a, b, mode, out_dtype=F32, addend=None):
    if mode == "tn":
        kdim, m = a.shape
        n = b.shape[1]
        tm, tn, tk = _tile(m, 1024), _tile(n, 1536), _tile(kdim, 512)
        a_spec = pl.BlockSpec((tk, tm), lambda i, j, k: (k, i))
        b_spec = pl.BlockSpec((tk, tn), lambda i, j, k: (k, j))
        dims = (((0,), (0,)), ((), ()))
    else:
        m, kdim = a.shape
        n = b.shape[1] if mode == "nn" else b.shape[0]
        tn, tk = _tile(n, 1536), _tile(kdim, 1536)
        tm = _tile(m, 1024 if tn <= 1024 else 512)
        a_spec = pl.BlockSpec((tm, tk), lambda i, j, k: (i, k))
        if mode == "nn":
            b_spec = pl.BlockSpec((tk, tn), lambda i, j, k: (k, j))
            dims = (((1,), (0,)), ((), ()))
        else:
            b_spec = pl.BlockSpec((tn, tk), lambda i, j, k: (j, k))
            dims = (((1,), (1,)), ((), ()))
    nk = kdim // tk
    has_add = addend is not None

    def body(*refs):
        a_ref, b_ref = refs[0], refs[1]
        add_ref = refs[2] if has_add else None
        o_ref, acc_ref = refs[-2], refs[-1]
        k = pl.program_id(2)

        @pl.when(k == 0)
        def _():
            acc_ref[...] = jnp.zeros_like(acc_ref)

        acc_ref[...] += lax.dot_general(a_ref[...], b_ref[...], dims, preferred_element_type=F32)

        @pl.when(k == nk - 1)
        def _():
            r = acc_ref[...]
            if has_add:
                r = r + add_ref[...]
            o_ref[...] = r.astype(o_ref.dtype)

    in_specs = [a_spec, b_spec]
    args = [a, b]
    if has_add:
        in_specs.append(pl.BlockSpec((tm, tn), lambda i, j, k: (i, j)))
        args.append(addend)
    return pl.pallas_call(
        body, name=name, grid=(m // tm, n // tn, nk), in_specs=in_specs,
        out_specs=pl.BlockSpec((tm, tn), lambda i, j, k: (i, j)),
        out_shape=jax.ShapeDtypeStruct((m, n), out_dtype),
        scratch_shapes=[pltpu.VMEM((tm, tn), F32)],
        compiler_params=pltpu.CompilerParams(dimension_semantics=("arbitrary", "arbitrary", "arbitrary"),
                                             vmem_limit_bytes=VMEM_LIMIT),
    )(*args)


def _rms(x, g):
    r = lax.rsqrt(jnp.mean(x * x, axis=-1, keepdims=True) + NORM_EPS)
    return x * r * g


def _rms_bwd(x, g, dy):
    r = lax.rsqrt(jnp.mean(x * x, axis=-1, keepdims=True) + NORM_EPS)
    xh = x * r
    dyg = dy * g
    dx = r * (dyg - xh * jnp.mean(dyg * xh, axis=-1, keepdims=True))
    return dx, jnp.sum(dy * xh, axis=0, keepdims=True)


def _silu(x):
    return x * _sigmoid(x)


def _silu_grad(x):
    s = _sigmoid(x)
    return s * (1.0 + x * (1.0 - s))


def _rope(xs, cos, s_up, s_dn):
    return xs * cos + pltpu.roll(xs, ROPE_HALF, 1) * s_up + pltpu.roll(xs, HEAD - ROPE_HALF, 1) * s_dn


def _rope_bwd(dy, cos, s_up, s_dn):
    return dy * cos + pltpu.roll(dy * s_up, HEAD - ROPE_HALF, 1) + pltpu.roll(dy * s_dn, ROPE_HALF, 1)


HG_SUB = 128
HG_LEVELS = 7
HG_E_ROWS = (HG_LEVELS + 1) * HG_SUB
TN_DIMS = (((0,), (0,)), ((), ()))
NT_DIMS = (((1,), (1,)), ((), ()))


def _hg_constants():
    import numpy as np
    n = HG_SUB
    r = np.arange(n)[:, None]
    c = np.arange(n)[None, :]
    cs, ps = [], []
    for lvl in range(HG_LEVELS):
        m = (n // 2) >> lvl
        upper = (r % (2 * m)) >= m
        mid = (r // (2 * m)) * (2 * m) + m - 1
        cs.append(np.where(upper, (c > mid) & (c <= r), (c > r) & (c <= mid)))
        ps.append(((r // (2 * m)) == (c // (2 * m))) & upper & ((c % (2 * m)) < m))
    cs.append(c <= r)
    cs.append(np.ones((8, n), bool))
    cstack = np.concatenate(cs, 0).astype(np.float32)
    pstack = np.concatenate(ps, 0).astype(np.float32)
    pstack_t = np.concatenate([p.T for p in ps], 0).astype(np.float32)
    return (jnp.asarray(cstack, BF16), jnp.asarray(cstack[:HG_E_ROWS].T, BF16), jnp.asarray(pstack, F32),
            jnp.asarray(pstack_t, F32))


def _split_dot(c_bf, x):
    hi = _bf(x)
    lo = _bf(x - hi.astype(F32))
    r2 = jnp.dot(c_bf, jnp.concatenate([hi, lo], axis=1), preferred_element_type=F32)
    return r2[:, :HEAD] + r2[:, HEAD:]


def _hg_gates(hq, hf, lb):
    sq = _sigmoid(hq)
    sg = _sigmoid(hf)
    fg = lb + (1.0 - lb) * sg
    return sq, hq * sq, sg, fg, 1.0 - fg, jnp.log(fg)


def _hg_block_fwd(st, hq, hf, hi, lb, cstack, p_ref):
    _, q, _, _, k, g = _hg_gates(hq, hf, lb)
    v = hi
    e = _split_dot(cstack, g)
    bc = e[HG_LEVELS * HG_SUB:HG_E_ROWS]
    b_last = jnp.tile(e[HG_E_ROWS:], (HG_SUB // 8, 1))
    a = jnp.zeros((HG_SUB, HG_SUB), F32)
    for lvl in range(HG_LEVELS):
        x = jnp.exp(e[lvl * HG_SUB:(lvl + 1) * HG_SUB])
        a = a + p_ref[pl.ds(lvl * HG_SUB, HG_SUB), :] * lax.dot_general(_bf(q * x), _bf(k * x), NT_DIMS,
                                                                          preferred_element_type=F32)
    a_bf = _bf(a)
    diag = jnp.sum(q * k, axis=1, keepdims=True)
    o = (jnp.dot(a_bf, _bf(v), preferred_element_type=F32) + diag * v
         + lax.dot_general(_bf(q * jnp.exp(bc)), _bf(st), NT_DIMS, preferred_element_type=F32))
    kd = k * jnp.exp(b_last - bc)
    st_out = st * jnp.exp(b_last) + lax.dot_general(_bf(v), _bf(kd), TN_DIMS, preferred_element_type=F32)
    return st_out, o, a_bf


def _hg_block_bwd(st, dst_out, do, hq, hf, hi, lb, a_bf, cstack, cstack_t, p_ref, pt_ref):
    sq, q, sg, fg, k, g = _hg_gates(hq, hf, lb)
    v = hi
    e = _split_dot(cstack, g)
    bc = e[HG_LEVELS * HG_SUB:HG_E_ROWS]
    b_last = jnp.tile(e[HG_E_ROWS:], (HG_SUB // 8, 1))
    eb = jnp.exp(bc)
    qb = q * eb
    er = jnp.exp(b_last - bc)
    kd = k * er
    e_last = jnp.exp(b_last)
    do_bf, v_bf, dst_bf = _bf(do), _bf(v), _bf(dst_out)
    da = lax.dot_general(do_bf, v_bf, NT_DIMS, preferred_element_type=F32)
    dat = lax.dot_general(v_bf, do_bf, NT_DIMS, preferred_element_type=F32)
    d_diag = jnp.sum(do * v, axis=1, keepdims=True)
    dv = (lax.dot_general(a_bf, do_bf, TN_DIMS, preferred_element_type=F32)
          + jnp.sum(q * k, axis=1, keepdims=True) * do
          + lax.dot_general(_bf(kd), dst_bf, NT_DIMS, preferred_element_type=F32))
    dqb = jnp.dot(do_bf, _bf(st), preferred_element_type=F32)
    dst = dst_out * e_last + lax.dot_general(do_bf, _bf(qb), TN_DIMS, preferred_element_type=F32)
    dkd = jnp.dot(v_bf, dst_bf, preferred_element_type=F32)
    dq = dqb * eb + d_diag * k
    dk = dkd * er + d_diag * q
    d_last = (jnp.sum(dst_out * st * e_last, axis=0, keepdims=True)
              + jnp.sum(dkd * kd, axis=0, keepdims=True))
    des = []
    for lvl in range(HG_LEVELS):
        x = jnp.exp(e[lvl * HG_SUB:(lvl + 1) * HG_SUB])
        qh, kh = q * x, k * x
        dm = _bf(p_ref[pl.ds(lvl * HG_SUB, HG_SUB), :] * da)
        dmt = _bf(pt_ref[pl.ds(lvl * HG_SUB, HG_SUB), :] * dat)
        dqh = jnp.dot(dm, _bf(kh), preferred_element_type=F32)
        dkh = jnp.dot(dmt, _bf(qh), preferred_element_type=F32)
        dq = dq + dqh * x
        dk = dk + dkh * x
        des.append(dqh * qh + dkh * kh)
    des.append(dqb * qb - dkd * kd)
    dg = _split_dot(cstack_t, jnp.concatenate(des, axis=0)) + d_last
    dfg = dg / fg - dk
    dhq = dq * (sq * (1.0 + hq * (1.0 - sq)))
    dhf = dfg * (1.0 - lb) * sg * (1.0 - sg)
    return dst, dhq, dhf, dv, jnp.sum(dfg * (1.0 - sg), axis=0, keepdims=True)


def hgrn_fwd(proj_main, lb, consts, bl, lp, d):
    nh = d // HEAD
    rows_blk = _tile(lp, 768, SEQ_BLOCK)
    nb = lp // rows_blk
    spb = rows_blk // HG_SUB
    cstack, _, pstack, _ = consts

    def body(hq_ref, hf_ref, hi_ref, lb_ref, c_ref, p_ref, o_ref, st_ref, a_ref, s_ref):
        j = pl.program_id(2)

        @pl.when(j == 0)
        def _():
            s_ref[...] = jnp.zeros_like(s_ref)
            o_ref[pl.ds(0, HG_SUB), :] = jnp.zeros((HG_SUB, HEAD), F32)
            st_ref[0, 0, pl.ds(0, 1)] = jnp.zeros((1, HEAD, HEAD), F32)
            a_ref[0, 0, pl.ds(0, 1)] = jnp.zeros((1, HG_SUB, HG_SUB), BF16)

        lbv = lb_ref[...]
        cs = c_ref[...]

        def sub(n, carry):
            r = pl.multiple_of(n * HG_SUB, HG_SUB)
            st = s_ref[...]
            st_ref[0, 0, pl.ds(n, 1)] = st[None]
            st_out, o, a_bf = _hg_block_fwd(st, hq_ref[pl.ds(r, HG_SUB), :], hf_ref[pl.ds(r, HG_SUB), :],
                                            hi_ref[pl.ds(r, HG_SUB), :], lbv, cs, p_ref)
            s_ref[...] = st_out
            o_ref[pl.ds(r, HG_SUB), :] = o
            a_ref[0, 0, pl.ds(n, 1)] = a_bf[None]
            return carry

        lax.fori_loop(jnp.where(j == 0, 1, 0), spb, sub, 0)

    def colspec(off):
        return pl.BlockSpec((rows_blk, HEAD), functools.partial(lambda h, b, j, off: (b * nb + j, off + h), off=off))

    whole = lambda arr: pl.BlockSpec(arr.shape, lambda h, b, j: (0, 0))
    return pl.pallas_call(
        body, name="hgrn_fwd", grid=(nh, bl, nb),
        in_specs=[colspec(0), colspec(nh), colspec(2 * nh), pl.BlockSpec((1, HEAD), lambda h, b, j: (0, h)),
                  whole(cstack), whole(pstack)],
        out_specs=[pl.BlockSpec((rows_blk, HEAD), lambda h, b, j: (b * nb + j, h)),
                   pl.BlockSpec((1, 1, spb, HEAD, HEAD), lambda h, b, j: (b, h, j, 0, 0)),
                   pl.BlockSpec((1, 1, spb, HG_SUB, HG_SUB), lambda h, b, j: (b, h, j, 0, 0))],
        out_shape=[jax.ShapeDtypeStruct((bl * lp, d), F32),
                   jax.ShapeDtypeStruct((bl, nh, lp // HG_SUB, HEAD, HEAD), F32),
                   jax.ShapeDtypeStruct((bl, nh, lp // HG_SUB, HG_SUB, HG_SUB), BF16)],
        scratch_shapes=[pltpu.VMEM((HEAD, HEAD), F32)],
        compiler_params=pltpu.CompilerParams(dimension_semantics=("arbitrary", "arbitrary", "arbitrary")),
    )(proj_main, proj_main, proj_main, lb, cstack, pstack)


def hgrn_bwd(proj_main, lb, consts, states, a_mats, do_scan, bl, lp, d):
    nh = d // HEAD
    rows_blk = _tile(lp, 768, SEQ_BLOCK)
    nb = lp // rows_blk
    spb = rows_blk // HG_SUB
    cstack, cstack_t, pstack, pstack_t = consts

    def body(hq_ref, hf_ref, hi_ref, lb_ref, c_ref, ct_ref, p_ref, pt_ref, st_ref, a_ref, do_ref,
             dq_ref, df_ref, di_ref, dlb_ref, ds_ref):
        b_id, j = pl.program_id(1), pl.program_id(2)
        blk = nb - 1 - j

        @pl.when(j == 0)
        def _():
            ds_ref[...] = jnp.zeros_like(ds_ref)

        @pl.when((j == 0) & (b_id == 0))
        def _():
            dlb_ref[...] = jnp.zeros_like(dlb_ref)

        @pl.when(blk == 0)
        def _():
            for ref in (dq_ref, df_ref, di_ref):
                ref[pl.ds(0, HG_SUB), :] = jnp.zeros((HG_SUB, HEAD), ref.dtype)

        lbv = lb_ref[...]
        cs = c_ref[...]
        cst = ct_ref[...]

        def sub(i, carry):
            n = spb - 1 - i
            r = pl.multiple_of(n * HG_SUB, HG_SUB)
            dst, dhq, dhf, dhi, dlb = _hg_block_bwd(
                st_ref[0, 0, pl.ds(n, 1)][0], ds_ref[...], do_ref[pl.ds(r, HG_SUB), :],
                hq_ref[pl.ds(r, HG_SUB), :], hf_ref[pl.ds(r, HG_SUB), :], hi_ref[pl.ds(r, HG_SUB), :], lbv,
                a_ref[0, 0, pl.ds(n, 1)][0], cs, cst, p_ref, pt_ref)
            ds_ref[...] = dst
            dq_ref[pl.ds(r, HG_SUB), :] = dhq.astype(dq_ref.dtype)
            df_ref[pl.ds(r, HG_SUB), :] = dhf.astype(df_ref.dtype)
            di_ref[pl.ds(r, HG_SUB), :] = dhi.astype(di_ref.dtype)
            dlb_ref[...] += dlb
            return carry

        lax.fori_loop(0, jnp.where(blk == 0, spb - 1, spb), sub, 0)

    def colspec(off):
        return pl.BlockSpec((rows_blk, HEAD),
                            functools.partial(lambda h, b, j, off: (b * nb + nb - 1 - j, off + h), off=off))

    whole = lambda arr: pl.BlockSpec(arr.shape, lambda h, b, j: (0, 0))
    mats = lambda: pl.BlockSpec((1, 1, spb, HEAD, HEAD), lambda h, b, j: (b, h, nb - 1 - j, 0, 0))
    t_rows = bl * lp
    return pl.pallas_call(
        body, name="hgrn_bwd", grid=(nh, bl, nb),
        in_specs=[colspec(0), colspec(nh), colspec(2 * nh), pl.BlockSpec((1, HEAD), lambda h, b, j: (0, h)),
                  whole(cstack), whole(cstack_t), whole(pstack), whole(pstack_t), mats(), mats(), colspec(0)],
        out_specs=[colspec(0), colspec(0), colspec(0), pl.BlockSpec((1, HEAD), lambda h, b, j: (0, h))],
        out_shape=[jax.ShapeDtypeStruct((t_rows, d), BF16)] * 3 + [jax.ShapeDtypeStruct((1, d), F32)],
        scratch_shapes=[pltpu.VMEM((HEAD, HEAD), F32)],
        compiler_params=pltpu.CompilerParams(dimension_semantics=("arbitrary", "arbitrary", "arbitrary")),
    )(proj_main, proj_main, proj_main, lb, cstack, cstack_t, pstack, pstack_t, states, a_mats, do_scan)


def _allowed(row0, col0, nr, nc, transposed=False):
    if transposed:
        col = col0 + lax.broadcasted_iota(jnp.int32, (nc, 1), 0)
        row = row0 + lax.broadcasted_iota(jnp.int32, (1, nr), 1)
    else:
        row = row0 + lax.broadcasted_iota(jnp.int32, (nr, 1), 0)
        col = col0 + lax.broadcasted_iota(jnp.int32, (1, nc), 1)
    return (col <= row) & ((col >= PAD_FRONT) | (row < PAD_FRONT))


def attn_fwd(q_cat, k_cat, v, bl, lp, nm, scale):
    tq = tk = SEQ_BLOCK
    nq = lp // tq

    def body(q_ref, k_ref, v_ref, o_ref, lse_ref, m_ref, l_ref, acc_ref):
        i = pl.program_id(2)
        q = q_ref[...]
        m_ref[...] = jnp.full_like(m_ref, NEG)
        l_ref[...] = jnp.zeros_like(l_ref)
        acc_ref[...] = jnp.zeros_like(acc_ref)

        def kstep(c, carry):
            c0 = pl.multiple_of(c * tk, tk)
            s = lax.dot_general(q, k_ref[pl.ds(c0, tk), :], NT_DIMS, preferred_element_type=F32) * scale
            s = jnp.where(_allowed(i * tq, c * tk, tq, tk), s, NEG)
            m_old = m_ref[...]
            m_new = jnp.maximum(m_old, jnp.max(s, axis=1, keepdims=True))
            alpha = jnp.exp(m_old - m_new)
            p = jnp.exp(s - m_new)
            l_ref[...] = alpha * l_ref[...] + jnp.sum(p, axis=1, keepdims=True)
            acc_ref[...] = alpha * acc_ref[...] + jnp.dot(_bf(p), v_ref[pl.ds(c0, tk), :],
                                                          preferred_element_type=F32)
            m_ref[...] = m_new
            return carry

        lax.fori_loop(0, i + 1, kstep, 0)
        o_ref[...] = (acc_ref[...] / l_ref[...]).astype(o_ref.dtype)
        lse_ref[0, 0] = m_ref[...] + jnp.log(l_ref[...])

    return pl.pallas_call(
        body, name="attn_fwd", grid=(bl, nm, nq),
        in_specs=[pl.BlockSpec((tq, QK_PAD), lambda b, h, i: (b * nq + i, h)),
                  pl.BlockSpec((lp, QK_PAD), lambda b, h, i: (b, h)),
                  pl.BlockSpec((lp, HEAD), lambda b, h, i: (b, h))],
        out_specs=[pl.BlockSpec((tq, HEAD), lambda b, h, i: (b * nq + i, h)),
                   pl.BlockSpec((1, 1, tq, 1), lambda b, h, i: (b, h, i, 0))],
        out_shape=[jax.ShapeDtypeStruct((bl * lp, nm * HEAD), BF16),
                   jax.ShapeDtypeStruct((bl, nm, lp, 1), F32)],
        scratch_shapes=[pltpu.VMEM((tq, 1), F32), pltpu.VMEM((tq, 1), F32), pltpu.VMEM((tq, HEAD), F32)],
        compiler_params=pltpu.CompilerParams(dimension_semantics=("arbitrary", "arbitrary", "arbitrary")),
    )(q_cat, k_cat, v)


def attn_bwd_dq(q_cat, k_cat, v, o, do, lse, bl, lp, nm, scale):
    tq = tk = SEQ_BLOCK
    nq = lp // tq

    def body(q_ref, k_ref, v_ref, o_ref, do_ref, lse_ref, dq_ref, dl_ref, acc_ref):
        i = pl.program_id(2)
        q = q_ref[...]
        do_b = do_ref[...]
        delta = jnp.sum(o_ref[...].astype(F32) * do_b.astype(F32), axis=1, keepdims=True)
        lse_b = lse_ref[0, 0]
        acc_ref[...] = jnp.zeros_like(acc_ref)

        def kstep(c, carry):
            c0 = pl.multiple_of(c * tk, tk)
            ks = k_ref[pl.ds(c0, tk), :]
            s = lax.dot_general(q, ks, NT_DIMS, preferred_element_type=F32) * scale
            p = jnp.where(_allowed(i * tq, c * tk, tq, tk), jnp.exp(s - lse_b), 0.0)
            dp = lax.dot_general(do_b, v_ref[pl.ds(c0, tk), :], NT_DIMS, preferred_element_type=F32)
            ds = p * (dp - delta)
            acc_ref[...] += jnp.dot(_bf(ds), ks, preferred_element_type=F32)
            return carry

        lax.fori_loop(0, i + 1, kstep, 0)
        dq_ref[...] = acc_ref[...] * scale
        dl_ref[0, 0] = delta

    return pl.pallas_call(
        body, name="attn_bwd_dq", grid=(bl, nm, nq),
        in_specs=[pl.BlockSpec((tq, QK_PAD), lambda b, h, i: (b * nq + i, h)),
                  pl.BlockSpec((lp, QK_PAD), lambda b, h, i: (b, h)),
                  pl.BlockSpec((lp, HEAD), lambda b, h, i: (b, h)),
                  pl.BlockSpec((tq, HEAD), lambda b, h, i: (b * nq + i, h)),
                  pl.BlockSpec((tq, HEAD), lambda b, h, i: (b * nq + i, h)),
                  pl.BlockSpec((1, 1, tq, 1), lambda b, h, i: (b, h, i, 0))],
        out_specs=[pl.BlockSpec((tq, QK_PAD), lambda b, h, i: (b * nq + i, h)),
                   pl.BlockSpec((1, 1, tq, 1), lambda b, h, i: (b, h, i, 0))],
        out_shape=[jax.ShapeDtypeStruct((bl * lp, nm * QK_PAD), F32),
                   jax.ShapeDtypeStruct((bl, nm, lp, 1), F32)],
        scratch_shapes=[pltpu.VMEM((tq, QK_PAD), F32)],
        compiler_params=pltpu.CompilerParams(dimension_semantics=("arbitrary", "arbitrary", "arbitrary")),
    )(q_cat, k_cat, v, o, do, lse)


def attn_bwd_dkv(q_cat, k_cat, v, do, lse_row, delta_row, bl, lp, nm, scale):
    tq = tk = SEQ_BLOCK
    nq = lp // tq

    def body(q_ref, k_ref, v_ref, do_ref, lse_ref, dl_ref, dk_ref, dv_ref):
        i = pl.program_id(2)
        kt = k_ref[...]
        vt = v_ref[...]
        dk_ref[...] = jnp.zeros_like(dk_ref)
        dv_ref[...] = jnp.zeros_like(dv_ref)

        def qstep(c, carry):
            c0 = pl.multiple_of(c * tq, tq)
            qs = q_ref[pl.ds(c0, tq), :]
            dos = do_ref[pl.ds(c0, tq), :]
            st = lax.dot_general(kt, qs, NT_DIMS, preferred_element_type=F32) * scale
            pt = jnp.where(_allowed(c * tq, i * tk, tq, tk, transposed=True),
                           jnp.exp(st - lse_ref[0, 0, pl.ds(c, 1)][0]), 0.0)
            dv_ref[...] += jnp.dot(_bf(pt), dos, preferred_element_type=F32)
            dpt = lax.dot_general(vt, dos, NT_DIMS, preferred_element_type=F32)
            dst = pt * (dpt - dl_ref[0, 0, pl.ds(c, 1)][0])
            dk_ref[...] += jnp.dot(_bf(dst), qs, preferred_element_type=F32)
            return carry

        lax.fori_loop(i, nq, qstep, 0)
        dk_ref[...] = dk_ref[...] * scale

    return pl.pallas_call(
        body, name="attn_bwd_dkv", grid=(bl, nm, nq),
        in_specs=[pl.BlockSpec((lp, QK_PAD), lambda b, h, i: (b, h)),
                  pl.BlockSpec((tk, QK_PAD), lambda b, h, i: (b * nq + i, h)),
                  pl.BlockSpec((tk, HEAD), lambda b, h, i: (b * nq + i, h)),
                  pl.BlockSpec((lp, HEAD), lambda b, h, i: (b, h)),
                  pl.BlockSpec((1, 1, nq, 1, tq), lambda b, h, i: (b, h, 0, 0, 0)),
                  pl.BlockSpec((1, 1, nq, 1, tq), lambda b, h, i: (b, h, 0, 0, 0))],
        out_specs=[pl.BlockSpec((tk, QK_PAD), lambda b, h, i: (b * nq + i, h)),
                   pl.BlockSpec((tk, HEAD), lambda b, h, i: (b * nq + i, h))],
        out_shape=[jax.ShapeDtypeStruct((bl * lp, nm * QK_PAD), F32),
                   jax.ShapeDtypeStruct((bl * lp, nm * HEAD), F32)],
        compiler_params=pltpu.CompilerParams(dimension_semantics=("arbitrary", "arbitrary", "arbitrary")),
    )(q_cat, k_cat, v, do, lse_row, delta_row)


def _place():
    return lax.axis_index("x"), lax.axis_index("y"), lax.axis_index("c")


def gather_shards(packed):
    hbm = pl.BlockSpec(memory_space=pl.ANY)

    def body(src_ref, out_ref, send_sems, recv_sems, local_sem):
        x, y, c = _place()
        me = 2 * x + y
        chips = [(1 - x, y), (x, 1 - y), (1 - x, 1 - y)]
        local = pltpu.make_async_copy(src_ref, out_ref.at[me], local_sem)
        local.start()
        sends = []
        for k, (px, py) in enumerate(chips):
            cp = pltpu.make_async_remote_copy(src_ref=src_ref, dst_ref=out_ref.at[me], send_sem=send_sems.at[k],
                                              recv_sem=recv_sems.at[k], device_id=(px, py, c), device_id_type=MESH)
            cp.start()
            sends.append(cp)
        for k, (px, py) in enumerate(chips):
            pltpu.make_async_remote_copy(src_ref=src_ref, dst_ref=out_ref.at[2 * px + py], send_sem=send_sems.at[k],
                                         recv_sem=recv_sems.at[k], device_id=(px, py, c),
                                         device_id_type=MESH).wait_recv()
        for cp in sends:
            cp.wait_send()
        local.wait()

    return pl.pallas_call(
        body, name="gather_shards", in_specs=[hbm], out_specs=hbm,
        out_shape=jax.ShapeDtypeStruct((4,) + packed.shape, packed.dtype),
        scratch_shapes=[pltpu.SemaphoreType.DMA((3,)), pltpu.SemaphoreType.DMA((3,)), pltpu.SemaphoreType.DMA],
    )(packed)


def exchange_grads(send, small):
    hbm = pl.BlockSpec(memory_space=pl.ANY)

    def body(send_ref, small_ref, recv_ref, all_ref, send_sems, recv_sems, ssend_sems, srecv_sems, local_sem):
        x, y, c = _place()
        me = 4 * x + 2 * y + c
        chips = [(1 - x, y), (x, 1 - y), (1 - x, 1 - y)]
        local = pltpu.make_async_copy(small_ref, all_ref.at[me], local_sem)
        local.start()
        sends = []
        for k, (px, py) in enumerate(chips):
            cp = pltpu.make_async_remote_copy(src_ref=send_ref.at[2 * px + py], dst_ref=recv_ref.at[k],
                                              send_sem=send_sems.at[k], recv_sem=recv_sems.at[k],
                                              device_id=(px, py, c), device_id_type=MESH)
            cp.start()
            sends.append(cp)
        others = [(x ^ ((r >> 2) & 1), y ^ ((r >> 1) & 1), c ^ (r & 1)) for r in range(1, 8)]
        for r, peer in enumerate(others):
            cp = pltpu.make_async_remote_copy(src_ref=small_ref, dst_ref=all_ref.at[me], send_sem=ssend_sems.at[r],
                                              recv_sem=srecv_sems.at[r], device_id=peer, device_id_type=MESH)
            cp.start()
            sends.append(cp)
        for k, (px, py) in enumerate(chips):
            pltpu.make_async_remote_copy(src_ref=send_ref.at[2 * px + py], dst_ref=recv_ref.at[k],
                                         send_sem=send_sems.at[k], recv_sem=recv_sems.at[k],
                                         device_id=(px, py, c), device_id_type=MESH).wait_recv()
        for r, (px, py, pc) in enumerate(others):
            pltpu.make_async_remote_copy(src_ref=small_ref, dst_ref=all_ref.at[4 * px + 2 * py + pc],
                                         send_sem=ssend_sems.at[r], recv_sem=srecv_sems.at[r],
                                         device_id=(px, py, pc), device_id_type=MESH).wait_recv()
        for cp in sends:
            cp.wait_send()
        local.wait()

    return pl.pallas_call(
        body, name="exchange_grads", in_specs=[hbm, hbm], out_specs=[hbm, hbm],
        out_shape=[jax.ShapeDtypeStruct((3,) + send.shape[1:], send.dtype),
                   jax.ShapeDtypeStruct((8,) + small.shape, small.dtype)],
        scratch_shapes=[pltpu.SemaphoreType.DMA((3,)), pltpu.SemaphoreType.DMA((3,)),
                        pltpu.SemaphoreType.DMA((7,)), pltpu.SemaphoreType.DMA((7,)), pltpu.SemaphoreType.DMA],
    )(send, small)


def swap_with_sibling(part):
    hbm = pl.BlockSpec(memory_space=pl.ANY)

    def body(src_ref, out_ref, send_sem, recv_sem):
        x, y, c = _place()
        cp = pltpu.make_async_remote_copy(src_ref=src_ref, dst_ref=out_ref, send_sem=send_sem, recv_sem=recv_sem,
                                          device_id=(x, y, 1 - c), device_id_type=MESH)
        cp.start()
        cp.wait()

    return pl.pallas_call(
        body, name="swap_with_sibling", in_specs=[hbm], out_specs=hbm,
        out_shape=jax.ShapeDtypeStruct(part.shape, part.dtype),
        scratch_shapes=[pltpu.SemaphoreType.DMA, pltpu.SemaphoreType.DMA],
    )(part)


def adamw(name, w, g, m, v):
    r, c = w.shape
    tr = r if r * c <= 65536 else _tile(r, 128, 8)

    def body(w_ref, g_ref, m_ref, v_ref, d_ref, nm_ref, nv_ref):
        gv = g_ref[...]
        m_new = ADAM_B1 * m_ref[...] + (1.0 - ADAM_B1) * gv
        v_new = ADAM_B2 * v_ref[...] + (1.0 - ADAM_B2) * (gv * gv)
        m_hat = m_new / (1.0 - ADAM_B1 ** ADAM_STEP)
        v_hat = v_new / (1.0 - ADAM_B2 ** ADAM_STEP)
        d_ref[...] = -ADAM_LR * (m_hat / (jnp.sqrt(v_hat) + ADAM_EPS) + ADAM_WD * w_ref[...])
        nm_ref[...] = m_new
        nv_ref[...] = v_new

    spec = pl.BlockSpec((tr, c), lambda i: (i, 0))
    return pl.pallas_call(
        body, name=name, grid=(r // tr,), in_specs=[spec] * 4, out_specs=[spec] * 3,
        out_shape=[jax.ShapeDtypeStruct((r, c), F32)] * 3,
        compiler_params=pltpu.CompilerParams(dimension_semantics=("arbitrary",)),
    )(w, g, m, v)


def _pack_rows(shapes):
    rows = [(r * c) // PACK_W for (r, c) in shapes]
    for (r, c) in shapes:
        assert (r * c) % PACK_W == 0
    total = sum(rows)
    return rows, -(-total // 16) * 16


def pack_shard(parts, total_rows):
    flat = jnp.concatenate([p.reshape(-1, PACK_W) for p in parts], axis=0)
    return jnp.pad(flat, ((0, total_rows - flat.shape[0]), (0, 0)))


def split_full(name, full, s):
    if name in COL_SHARDED:
        c = full.shape[1] // 4
        return full[:, s * c:(s + 1) * c]
    r = full.shape[0] // 4
    return full[s * r:(s + 1) * r]


def join_shards(name, shards):
    return jnp.concatenate(shards, axis=1 if name in COL_SHARDED else 0)


def kernel(x, meta_tokens, w_in, b_gate, lb_logits, hg_norm_g, w_hg_o, q_a_norm_g, w_q_b, kv_a_norm_g, w_kv_b, w_mla_o, w_out, mix_pre_g, mix_post_g, ffn_pre_g, ffn_post_g, w_ffn_in, w_ffn_out, loss_target, m_meta_tokens, m_w_in, m_b_gate, m_lb_logits, m_hg_norm_g, m_w_hg_o, m_q_a_norm_g, m_w_q_b, m_kv_a_norm_g, m_w_kv_b, m_w_mla_o, m_w_out, m_mix_pre_g, m_mix_post_g, m_ffn_pre_g, m_ffn_post_g, m_w_ffn_in, m_w_ffn_out, v_meta_tokens, v_w_in, v_b_gate, v_lb_logits, v_hg_norm_g, v_w_hg_o, v_q_a_norm_g, v_w_q_b, v_kv_a_norm_g, v_w_kv_b, v_w_mla_o, v_w_out, v_mix_pre_g, v_mix_post_g, v_ffn_pre_g, v_ffn_post_g, v_w_ffn_in, v_w_ffn_out):
    wts = dict(meta_tokens=meta_tokens, w_in=w_in[0], b_gate=b_gate, lb_logits=lb_logits, hg_norm_g=hg_norm_g,
               w_hg_o=w_hg_o[0], q_a_norm_g=q_a_norm_g, w_q_b=w_q_b[0], kv_a_norm_g=kv_a_norm_g, w_kv_b=w_kv_b[0],
               w_mla_o=w_mla_o[0], w_out=w_out[0], mix_pre_g=mix_pre_g, mix_post_g=mix_post_g, ffn_pre_g=ffn_pre_g,
               ffn_post_g=ffn_post_g, w_ffn_in=w_ffn_in[0], w_ffn_out=w_ffn_out[0])
    mom_m = dict(meta_tokens=m_meta_tokens, w_in=m_w_in[0], b_gate=m_b_gate, lb_logits=m_lb_logits,
                 hg_norm_g=m_hg_norm_g, w_hg_o=m_w_hg_o[0], q_a_norm_g=m_q_a_norm_g, w_q_b=m_w_q_b[0],
                 kv_a_norm_g=m_kv_a_norm_g, w_kv_b=m_w_kv_b[0], w_mla_o=m_w_mla_o[0], w_out=m_w_out[0],
                 mix_pre_g=m_mix_pre_g, mix_post_g=m_mix_post_g, ffn_pre_g=m_ffn_pre_g, ffn_post_g=m_ffn_post_g,
                 w_ffn_in=m_w_ffn_in[0], w_ffn_out=m_w_ffn_out[0])
    mom_v = dict(meta_tokens=v_meta_tokens, w_in=v_w_in[0], b_gate=v_b_gate, lb_logits=v_lb_logits,
                 hg_norm_g=v_hg_norm_g, w_hg_o=v_w_hg_o[0], q_a_norm_g=v_q_a_norm_g, w_q_b=v_w_q_b[0],
                 kv_a_norm_g=v_kv_a_norm_g, w_kv_b=v_w_kv_b[0], w_mla_o=v_w_mla_o[0], w_out=v_w_out[0],
                 mix_pre_g=v_mix_pre_g, mix_post_g=v_mix_post_g, ffn_pre_g=v_ffn_pre_g, ffn_post_g=v_ffn_post_g,
                 w_ffn_in=v_w_ffn_in[0], w_ffn_out=v_w_ffn_out[0])

    bl, seq, d = x.shape
    lp = PAD_FRONT + N_META + seq
    t_rows = bl * lp
    nh = d // HEAD
    ql, kvl = wts["w_q_b"].shape[0], wts["w_kv_b"].shape[0]
    nm = (4 * wts["w_mla_o"].shape[0]) // HEAD
    ffn = 4 * wts["w_ffn_out"].shape[0]
    mla_w = ql + kvl + HEAD
    assert ql == kvl and ql % HEAD == 0 and seq % SEQ_BLOCK == 0 and d % HEAD == 0
    scale = (HEAD + ROPE) ** -0.5
    my_chip = 2 * lax.axis_index("x") + lax.axis_index("y")

    shard_shapes = [wts[n].shape for n in BIG]
    pack_rows, pack_total = _pack_rows(shard_shapes)
    mcols = meta_tokens.shape[1]
    meta_rows = (N_META * mcols * 2) // PACK_W
    assert (N_META * mcols * 2) % PACK_W == 0
    meta_bits = lax.bitcast_convert_type(meta_tokens, BF16).reshape(meta_rows, PACK_W)
    gathered = gather_shards(pack_shard([meta_bits] + [_bf(wts[n]) for n in BIG],
                                        -(-(meta_rows + sum(pack_rows)) // 16) * 16))
    meta_full = jnp.concatenate(
        [lax.bitcast_convert_type(gathered[s, :meta_rows].reshape(N_META, mcols, 2), F32) for s in range(4)], axis=1)
    full = {}
    off = meta_rows
    for n, rows, (r, c) in zip(BIG, pack_rows, shard_shapes):
        full[n] = join_shards(n, [gathered[s, off:off + rows].reshape(r, c) for s in range(4)])
        off += rows
    w_main = jnp.concatenate([full["w_in"][:, :4 * d], full["w_in"][:, -2 * d:]], axis=1)
    w_mla = jnp.pad(full["w_in"][:, 4 * d:4 * d + ql + kvl + ROPE], ((0, 0), (0, HEAD - ROPE)))
    w_qb = jnp.pad(full["w_q_b"].reshape(ql, nm, HEAD + ROPE), ((0, 0), (0, 0), (0, QK_PAD - HEAD - ROPE))
                   ).reshape(ql, nm * QK_PAD)
    w_kvb = full["w_kv_b"]

    h0 = jnp.concatenate([jnp.zeros((bl, PAD_FRONT, d), F32), jnp.broadcast_to(meta_full[None], (bl, N_META, d)), x],
                         axis=1).reshape(t_rows, d)
    tgt = jnp.concatenate([jnp.zeros((bl, PAD_FRONT + N_META, d), F32), loss_target], axis=1).reshape(t_rows, d)
    pos = (jnp.arange(lp, dtype=jnp.int32) - PAD_FRONT).astype(F32)
    inv_freq = 1.0 / (ROPE_THETA ** (jnp.arange(0, ROPE, 2, dtype=F32) / ROPE))
    ang = pos[:, None] * inv_freq[None, :]
    zeros32 = jnp.zeros((lp, ROPE_HALF), F32)
    zeros64 = jnp.zeros((lp, HEAD - ROPE), F32)
    t_cos = jnp.concatenate([jnp.cos(ang), jnp.cos(ang), zeros64], axis=1)
    t_up = jnp.concatenate([zeros32, jnp.sin(ang), zeros64], axis=1)
    t_dn = jnp.concatenate([-jnp.sin(ang), zeros32, zeros64], axis=1)
    real = jnp.broadcast_to((jnp.arange(lp) >= PAD_FRONT + N_META).astype(F32)[:, None], (lp, d))
    lb_soft = jax.nn.softmax(lb_logits.astype(F32), axis=0)
    lb = lb_soft[0:1]

    (u1,) = rowwise("norm_mix_pre", lambda h, g: _rms(h, g), [(h0, d, 0)], [], [mix_pre_g], [(d, BF16)])
    proj_main = matmul("proj_main", u1, w_main, "nn")
    proj_mla = matmul("proj_mla", u1, w_mla, "nn")
    hg_consts = _hg_constants()
    o_scan, states, a_mats = hgrn_fwd(proj_main, lb, hg_consts, bl, lp, d)

    def hg_out_fn(o, hg, g):
        return jnp.concatenate([_rms(o[:, h * HEAD:(h + 1) * HEAD], g) for h in range(nh)], axis=1) * _silu(hg)

    (o_hg,) = rowwise("hgrn_out", hg_out_fn, [(o_scan, d, 0), (proj_main, d, 3)], [], [hg_norm_g], [(d, BF16)])
    y_a = matmul("y_a", o_hg, _bf(full["w_hg_o"]), "nn")

    qn, kvn = rowwise("mla_norms", lambda cq, ckv, gq, gk: (_rms(cq, gq), _rms(ckv, gk)),
                      [(proj_mla, ql, 0), (proj_mla, kvl, 1)], [], [q_a_norm_g, kv_a_norm_g],
                      [(ql, BF16), (kvl, BF16)])
    q_full = matmul("q_up", qn, w_qb, "nn")
    kv_full = matmul("kv_up", kvn, w_kvb, "nn")

    def mla_prep_fn(qf, kvf, kpe, cos, s_up, s_dn):
        kpe_r = _rope(kpe, cos, s_up, s_dn)
        qs, ks, vs = [], [], []
        for h in range(nm):
            qs += [qf[:, h * QK_PAD:h * QK_PAD + HEAD], _rope(qf[:, h * QK_PAD + HEAD:(h + 1) * QK_PAD], cos, s_up, s_dn)]
            ks += [kvf[:, h * QK_PAD:h * QK_PAD + HEAD], kpe_r]
            vs += [kvf[:, h * QK_PAD + HEAD:(h + 1) * QK_PAD]]
        return jnp.concatenate(qs, axis=1), jnp.concatenate(ks, axis=1), jnp.concatenate(vs, axis=1)

    kpe_blk = (ql + kvl) // HEAD
    q_cat, k_cat, v_att = rowwise("mla_prep", mla_prep_fn,
                                  [(q_full, nm * QK_PAD, 0), (kv_full, nm * QK_PAD, 0), (proj_mla, HEAD, kpe_blk)],
                                  [t_cos, t_up, t_dn], [], [(nm * QK_PAD, BF16), (nm * QK_PAD, BF16), (nm * HEAD, BF16)])
    o_mla, lse = attn_fwd(q_cat, k_cat, v_att, bl, lp, nm, scale)
    y_b = matmul("y_b", o_mla, _bf(full["w_mla_o"]), "nn")

    def gate_fn(ya, yb, ga, gb, bias):
        return _sigmoid(ga + bias[:, :d]) * ya + _sigmoid(gb + bias[:, d:]) * yb

    (z,) = rowwise("gate_mix", gate_fn, [(y_a, d, 0), (y_b, d, 0), (proj_main, d, 4), (proj_main, d, 5)], [],
                   [b_gate], [(d, BF16)])
    mixed = matmul("mixed", z, _bf(full["w_out"]), "nn")

    def mid_fn(h, mx, g_post, g_pre):
        h1 = h + _rms(mx, g_post)
        return h1, _rms(h1, g_pre)

    h1, u2 = rowwise("norm_mid", mid_fn, [(h0, d, 0), (mixed, d, 0)], [], [mix_post_g, ffn_pre_g],
                     [(d, F32), (d, BF16)])
    gu = matmul("ffn_in", u2, _bf(full["w_ffn_in"]), "nn")
    (act,) = rowwise("swiglu", lambda gt, up: _silu(gt) * up, [(gu, ffn, 0), (gu, ffn, 1)], [], [], [(ffn, BF16)])
    f_out = matmul("ffn_out", act, _bf(full["w_ffn_out"]), "nn")

    def loss_fn(h1v, fv, tg, realv, g_post):
        h2 = h1v + _rms(fv, g_post)
        diff = (h2 - tg) * realv
        part = jnp.broadcast_to(0.5 * jnp.sum(diff * diff, keepdims=True) / d, (1, HEAD))
        dy = diff / d
        df, dg = _rms_bwd(fv, g_post, dy)
        return dy, df, part, dg

    dy, df, loss_part, g_ffn_post = rowwise("loss_head", loss_fn, [(h1, d, 0), (f_out, d, 0), (tgt, d, 0)], [real],
                                            [ffn_post_g], [(d, F32), (d, BF16)], [(1, HEAD), (1, d)])
    grads = {}
    d_act = matmul("d_act", df, _bf(full["w_ffn_out"]), "nt")
    grads["w_ffn_out"] = matmul("gw_ffn_out", act, df, "tn")

    def swiglu_bwd_fn(gt, up, da):
        return jnp.concatenate([da * up * _silu_grad(gt), da * _silu(gt)], axis=1)

    (dgu,) = rowwise("swiglu_bwd", swiglu_bwd_fn, [(gu, ffn, 0), (gu, ffn, 1), (d_act, ffn, 0)], [], [],
                     [(2 * ffn, BF16)])
    du2 = matmul("d_u2", dgu, _bf(full["w_ffn_in"]), "nt")
    grads["w_ffn_in"] = matmul("gw_ffn_in", u2, dgu, "tn")

    def mid_bwd_fn(dyv, h1v, du2v, mx, g_pre, g_post):
        dx, dg_pre = _rms_bwd(h1v, g_pre, du2v)
        dh1 = dyv + dx
        dmx, dg_post = _rms_bwd(mx, g_post, dh1)
        return dh1, dmx, dg_pre, dg_post

    dh1, dmixed, g_ffn_pre, g_mix_post = rowwise("norm_mid_bwd", mid_bwd_fn,
                                                 [(dy, d, 0), (h1, d, 0), (du2, d, 0), (mixed, d, 0)], [],
                                                 [ffn_pre_g, mix_post_g], [(d, F32), (d, BF16)], [(1, d), (1, d)])
    dz = matmul("d_z", dmixed, _bf(full["w_out"]), "nt")
    grads["w_out"] = matmul("gw_out", z, dmixed, "tn")

    def gate_bwd_fn(dzv, ya, yb, ga, gb, bias):
        sa, sb = _sigmoid(ga + bias[:, :d]), _sigmoid(gb + bias[:, d:])
        dga = dzv * ya * sa * (1.0 - sa)
        dgb = dzv * yb * sb * (1.0 - sb)
        dgates = jnp.concatenate([dga, dgb], axis=1)
        return dzv * sa, dzv * sb, dgates, jnp.sum(dgates, axis=0, keepdims=True)

    dy_a, dy_b, dgates, g_b_gate = rowwise("gate_mix_bwd", gate_bwd_fn,
                                           [(dz, d, 0), (y_a, d, 0), (y_b, d, 0), (proj_main, d, 4), (proj_main, d, 5)],
                                           [], [b_gate], [(d, BF16), (d, BF16), (2 * d, BF16)], [(1, 2 * d)])
    do_hg = matmul("d_o_hg", dy_a, _bf(full["w_hg_o"]), "nt")
    grads["w_hg_o"] = matmul("gw_hg_o", o_hg, dy_a, "tn")
    do_mla = matmul("d_o_mla", dy_b, _bf(full["w_mla_o"]), "nt", out_dtype=BF16)
    grads["w_mla_o"] = matmul("gw_mla_o", o_mla, dy_b, "tn")

    def hg_out_bwd_fn(do, o, hg, g):
        sg = _silu(hg)
        dn = do * sg
        dos, dgs, ons = [], 0.0, []
        for h in range(nh):
            sl = slice(h * HEAD, (h + 1) * HEAD)
            dx, dg = _rms_bwd(o[:, sl], g, dn[:, sl])
            dos.append(dx)
            dgs = dgs + dg
            ons.append(_rms(o[:, sl], g))
        dhg = do * jnp.concatenate(ons, axis=1) * _silu_grad(hg)
        return jnp.concatenate(dos, axis=1), dhg, dgs

    do_scan, dhg, g_hg_norm = rowwise("hgrn_out_bwd", hg_out_bwd_fn, [(do_hg, d, 0), (o_scan, d, 0), (proj_main, d, 3)],
                                      [], [hg_norm_g], [(d, F32), (d, BF16)], [(1, HEAD)])
    dhq, dhf, dhi, g_lb = hgrn_bwd(proj_main, lb, hg_consts, states, a_mats, do_scan, bl, lp, d)

    dq_cat, delta = attn_bwd_dq(q_cat, k_cat, v_att, o_mla, do_mla, lse, bl, lp, nm, scale)
    nq = lp // SEQ_BLOCK
    lse_row = lse.reshape(bl, nm, nq, 1, SEQ_BLOCK)
    delta_row = delta.reshape(bl, nm, nq, 1, SEQ_BLOCK)
    dk_cat, dv_att = attn_bwd_dkv(q_cat, k_cat, v_att, do_mla, lse_row, delta_row, bl, lp, nm, scale)

    def mla_prep_bwd_fn(dqc, dkc, dvv, cos, s_up, s_dn):
        dqs, dkvs, dkpe = [], [], 0.0
        for h in range(nm):
            dqs += [dqc[:, h * QK_PAD:h * QK_PAD + HEAD],
                    _rope_bwd(dqc[:, h * QK_PAD + HEAD:(h + 1) * QK_PAD], cos, s_up, s_dn)]
            dkvs += [dkc[:, h * QK_PAD:h * QK_PAD + HEAD], dvv[:, h * HEAD:(h + 1) * HEAD]]
            dkpe = dkpe + dkc[:, h * QK_PAD + HEAD:(h + 1) * QK_PAD]
        return jnp.concatenate(dqs, axis=1), jnp.concatenate(dkvs, axis=1), _rope_bwd(dkpe, cos, s_up, s_dn)

    dq_full, dkv_full, dkpe = rowwise("mla_prep_bwd", mla_prep_bwd_fn,
                                      [(dq_cat, nm * QK_PAD, 0), (dk_cat, nm * QK_PAD, 0), (dv_att, nm * HEAD, 0)],
                                      [t_cos, t_up, t_dn], [],
                                      [(nm * QK_PAD, BF16), (nm * QK_PAD, BF16), (HEAD, F32)])
    dqn = matmul("d_qn", dq_full, w_qb, "nt")
    g_wqb = matmul("gw_q_b", qn, dq_full, "tn")
    grads["w_q_b"] = g_wqb.reshape(ql, nm, QK_PAD)[:, :, :HEAD + ROPE].reshape(ql, nm * (HEAD + ROPE))
    dkvn = matmul("d_kvn", dkv_full, w_kvb, "nt")
    grads["w_kv_b"] = matmul("gw_kv_b", kvn, dkv_full, "tn")

    def mla_norms_bwd_fn(dqnv, dkvnv, cq, ckv, dkpev, gq, gk):
        dcq, dgq = _rms_bwd(cq, gq, dqnv)
        dckv, dgk = _rms_bwd(ckv, gk, dkvnv)
        return jnp.concatenate([dcq, dckv, dkpev], axis=1), dgq, dgk

    dmla, g_q_norm, g_kv_norm = rowwise("mla_norms_bwd", mla_norms_bwd_fn,
                                        [(dqn, ql, 0), (dkvn, kvl, 0), (proj_mla, ql, 0), (proj_mla, kvl, 1),
                                         (dkpe, HEAD, 0)], [], [q_a_norm_g, kv_a_norm_g],
                                        [(mla_w, BF16)], [(1, ql), (1, kvl)])

    w_main_bf = w_main
    pieces = [(dhq, w_main_bf[:, 0:d]), (dhf, w_main_bf[:, d:2 * d]), (dhi, w_main_bf[:, 2 * d:3 * d]),
              (dhg, w_main_bf[:, 3 * d:4 * d]), (dgates, w_main_bf[:, 4 * d:6 * d]), (dmla, w_mla)]
    du1 = None
    gw_parts = []
    for k, (dp, wp) in enumerate(pieces):
        du1 = matmul(f"d_u1_{k}", dp, wp, "nt", addend=du1)
        gw_parts.append(matmul(f"gw_in_{k}", u1, dp, "tn"))
    grads["w_in"] = jnp.concatenate(gw_parts[:4] + [gw_parts[5][:, :ql + kvl + ROPE], gw_parts[4]], axis=1)

    def first_bwd_fn(dh1v, h, du1v, g):
        dx, dg = _rms_bwd(h, g, du1v)
        return dh1v + dx, dg

    dh0, g_mix_pre = rowwise("norm_mix_pre_bwd", first_bwd_fn, [(dh1, d, 0), (h0, d, 0), (du1, d, 0)], [],
                             [mix_pre_g], [(d, F32)], [(1, d)])
    dh0 = dh0.reshape(bl, lp, d)
    grad_x = dh0[:, PAD_FRONT + N_META:]

    send = jnp.stack([pack_shard([split_full(n, grads[n], s) for n in BIG], pack_total) for s in range(4)])
    mine = lax.dynamic_index_in_dim(send, my_chip, axis=0, keepdims=False)
    p0 = lb_soft[0:1]
    g_lb_logits = jnp.concatenate([g_lb * p0 * (1.0 - p0), -g_lb * p0 * (1.0 - p0)], axis=0)

    def row_of(vec):
        return vec.reshape(-1, d) if vec.size >= d else jnp.pad(vec.reshape(1, -1), ((0, 0), (0, d - vec.size)))

    small_parts = dict(b_gate=g_b_gate, lb_logits=g_lb_logits, hg_norm_g=g_hg_norm, q_a_norm_g=g_q_norm,
                       kv_a_norm_g=g_kv_norm, mix_pre_g=g_mix_pre, mix_post_g=g_mix_post, ffn_pre_g=g_ffn_pre,
                       ffn_post_g=g_ffn_post)
    g_meta = jnp.sum(dh0[:, PAD_FRONT:PAD_FRONT + N_META], axis=0)
    small_rows = [row_of(small_parts[n]) for n in SMALL] + [row_of(g_meta)]
    n_small = sum(r.shape[0] for r in small_rows)
    small = jnp.pad(jnp.concatenate(small_rows, axis=0), ((0, -(-n_small // 8) * 8 - n_small), (0, 0)))
    recv, all_small = exchange_grads(_bf(send), small)

    rt = _tile(pack_total, 512, 16)
    recv2 = recv.reshape(3 * pack_total, PACK_W)
    (part_sum,) = rowwise("sum_chips", lambda a, r0, r1, r2: a + r0.astype(F32) + r1.astype(F32) + r2.astype(F32),
                          [(mine, PACK_W, 0)] + [(recv2, PACK_W, 0, k * (pack_total // rt)) for k in range(3)],
                          [], [], [(PACK_W, F32)], tm=rt)
    sib_sum = swap_with_sibling(part_sum)
    (g_packed,) = rowwise("sum_cores", lambda a, b: a + b, [(part_sum, PACK_W, 0), (sib_sum, PACK_W, 0)], [], [],
                          [(PACK_W, F32)], tm=rt)
    small_t = small.shape[0]

    def sum8_fn(*slabs):
        acc = slabs[0]
        for s in slabs[1:]:
            acc = acc + s
        return acc

    (g_small,) = rowwise("sum_small", sum8_fn, [(all_small.reshape(8 * small_t, d), d, 0, k) for k in range(8)],
                         [], [], [(d, F32)], tm=small_t, n_rows=small_t)

    g_final = {}
    off = 0
    for n, rows, (r, c) in zip(BIG, pack_rows, shard_shapes):
        g_final[n] = g_packed[off:off + rows].reshape(r, c)
        off += rows
    off = 0
    for n, part in zip(SMALL, small_rows[:-1]):
        rows = part.shape[0]
        g_final[n] = g_small[off:off + rows, :d].reshape(-1)[:wts[n].size].reshape(wts[n].shape)
        off += rows
    mcols = meta_tokens.shape[1]
    g_final["meta_tokens"] = lax.dynamic_slice_in_dim(g_small[off:off + N_META, :d], my_chip * mcols, mcols, axis=1)

    delta, new_m, new_v = {}, {}, {}
    for n in WEIGHTS:
        w2 = wts[n].reshape(-1, wts[n].shape[-1])
        dl, mn, vn = adamw("adamw_" + n, w2, g_final[n].reshape(w2.shape), mom_m[n].reshape(w2.shape),
                           mom_v[n].reshape(w2.shape))
        delta[n], new_m[n], new_v[n] = dl, mn, vn

    loss = lax.psum(loss_part[0, 0], ("x", "y", "c"))

    def shaped(n, a):
        return a.reshape((1,) + wts[n].shape) if n in BIG else a.reshape(wts[n].shape)

    return (loss, grad_x, *[shaped(n, g_final[n]) for n in WEIGHTS], *[shaped(n, delta[n]) for n in WEIGHTS],
            *[shaped(n, new_m[n]) for n in WEIGHTS], *[shaped(n, new_v[n]) for n in WEIGHTS])
```

```python
import functools
import math

import jax
import jax.numpy as jnp
from jax import lax
from jax.experimental import pallas as pl
from jax.experimental.pallas import tpu as pltpu

F32 = jnp.float32
BF16 = jnp.bfloat16
MESH = pl.DeviceIdType.MESH

N_META = 16
NORM_EPS = 1e-6
HEAD = 128
ROPE = 64
ROPE_HALF = ROPE // 2
QK_PAD = 2 * HEAD
CHUNK = 16
ROPE_THETA = 10000.0
SEQ_BLOCK = 256
PAD_FRONT = SEQ_BLOCK - N_META
PACK_W = 1024
NEG = -1e30
VMEM_LIMIT = 56 * 1024 * 1024

ADAM_LR, ADAM_B1, ADAM_B2, ADAM_EPS, ADAM_WD, ADAM_STEP = 0.001, 0.9, 0.999, 1e-08, 0.01, 10

BIG = ("w_in", "w_hg_o", "w_q_b", "w_kv_b", "w_mla_o", "w_out", "w_ffn_in", "w_ffn_out")
COL_SHARDED = ("w_in", "w_q_b", "w_kv_b", "w_ffn_in")
SMALL = ("b_gate", "lb_logits", "hg_norm_g", "q_a_norm_g", "kv_a_norm_g", "mix_pre_g", "mix_post_g",
         "ffn_pre_g", "ffn_post_g")
WEIGHTS = ("meta_tokens", "w_in", "b_gate", "lb_logits", "hg_norm_g", "w_hg_o", "q_a_norm_g", "w_q_b",
           "kv_a_norm_g", "w_kv_b", "w_mla_o", "w_out", "mix_pre_g", "mix_post_g", "ffn_pre_g", "ffn_post_g",
           "w_ffn_in", "w_ffn_out")


def _tile(n, cap, unit=128):
    if n <= cap:
        return n
    best = None
    for t in range(unit, cap + 1, unit):
        if n % t == 0:
            best = t
    assert best is not None, (n, cap, unit)
    return best


def _sigmoid(x):
    return 1.0 / (1.0 + jnp.exp(-x))


def _bf(x):
    return x.astype(BF16)


def rowwise(name, fn, row_ins, seq_tabs, consts, row_outs, acc_outs=(), tm=SEQ_BLOCK, n_rows=None):
    t_rows = row_ins[0][0].shape[0] if n_rows is None else n_rows
    nt = t_rows // tm
    assert t_rows % tm == 0
    n_in = len(row_ins) + len(seq_tabs) + len(consts)
    n_row = len(row_outs)

    def body(*refs):
        vals = [r[...] for r in refs[:n_in]]
        res = fn(*vals)
        if not isinstance(res, (tuple, list)):
            res = (res,)
        outs = refs[n_in:]
        for k in range(n_row):
            outs[k][...] = res[k].astype(outs[k].dtype)
        if acc_outs:
            @pl.when(pl.program_id(0) == 0)
            def _():
                for k in range(len(acc_outs)):
                    outs[n_row + k][...] = jnp.zeros_like(outs[n_row + k])

            for k in range(len(acc_outs)):
                outs[n_row + k][...] += res[n_row + k]

    row_ins = [tuple(e) + (0,) * (4 - len(e)) for e in row_ins]
    in_specs = [pl.BlockSpec((tm, w), functools.partial(lambda i, j, ro: (i + ro, j), j=j, ro=ro))
                for (_, w, j, ro) in row_ins]
    for tab in seq_tabs:
        per = tab.shape[0] // tm
        in_specs.append(pl.BlockSpec((tm, tab.shape[1]), functools.partial(lambda i, per: (i % per, 0), per=per)))
    for c in consts:
        in_specs.append(pl.BlockSpec(c.shape, lambda i: (0, 0)))
    out_specs = [pl.BlockSpec((tm, w), lambda i: (i, 0)) for (w, _) in row_outs]
    out_specs += [pl.BlockSpec(s, lambda i: (0, 0)) for s in acc_outs]
    out_shape = [jax.ShapeDtypeStruct((t_rows, w), dt) for (w, dt) in row_outs]
    out_shape += [jax.ShapeDtypeStruct(s, F32) for s in acc_outs]
    res = pl.pallas_call(
        body, name=name, grid=(nt,), in_specs=in_specs, out_specs=out_specs, out_shape=out_shape,
        compiler_params=pltpu.CompilerParams(dimension_semantics=("arbitrary",)),
    )(*[e[0] for e in row_ins], *seq_tabs, *consts)
    return res


def matmul(name, a, b, mode, out_dtype=F32, addend=None):
    if mode == "tn":
        kdim, m = a.shape
        n = b.shape[1]
        tm, tn, tk = _tile(m, 1024), _tile(n, 1536), _tile(kdim, 512)
        a_spec = pl.BlockSpec((tk, tm), lambda i, j, k: (k, i))
        b_spec = pl.BlockSpec((tk, tn), lambda i, j, k: (k, j))
        dims = (((0,), (0,)), ((), ()))
    else:
        m, kdim = a.shape
        n = b.shape[1] if mode == "nn" else b.shape[0]
        tn, tk = _tile(n, 1536), _tile(kdim, 1536)
        tm = _tile(m, 1024 if tn <= 1024 else 512)
        a_spec = pl.BlockSpec((tm, tk), lambda i, j, k: (i, k))
        if mode == "nn":
            b_spec = pl.BlockSpec((tk, tn), lambda i, j, k: (k, j))
            dims = (((1,), (0,)), ((), ()))
        else:
            b_spec = pl.BlockSpec((tn, tk), lambda i, j, k: (j, k))
            dims = (((1,), (1,)), ((), ()))
    nk = kdim // tk
    has_add = addend is not None

    def body(*refs):
        a_ref, b_ref = refs[0], refs[1]
        add_ref = refs[2] if has_add else None
        o_ref, acc_ref = refs[-2], refs[-1]
        k = pl.program_id(2)

        @pl.when(k == 0)
        def _():
            acc_ref[...] = jnp.zeros_like(acc_ref)

        acc_ref[...] += lax.dot_general(a_ref[...], b_ref[...], dims, preferred_element_type=F32)

        @pl.when(k == nk - 1)
        def _():
            r = acc_ref[...]
            if has_add:
                r = r + add_ref[...]
            o_ref[...] = r.astype(o_ref.dtype)

    in_specs = [a_spec, b_spec]
    args = [a, b]
    if has_add:
        in_specs.append(pl.BlockSpec((tm, tn), lambda i, j, k: (i, j)))
        args.append(addend)
    return pl.pallas_call(
        body, name=name, grid=(m // tm, n // tn, nk), in_specs=in_specs,
        out_specs=pl.BlockSpec((tm, tn), lambda i, j, k: (i, j)),
        out_shape=jax.ShapeDtypeStruct((m, n), out_dtype),
        scratch_shapes=[pltpu.VMEM((tm, tn), F32)],
        compiler_params=pltpu.CompilerParams(dimension_semantics=("arbitrary", "arbitrary", "arbitrary"),
                                             vmem_limit_bytes=VMEM_LIMIT),
    )(*args)


def _rms(x, g):
    r = lax.rsqrt(jnp.mean(x * x, axis=-1, keepdims=True) + NORM_EPS)
    return x * r * g


def _rms_bwd(x, g, dy):
    r = lax.rsqrt(jnp.mean(x * x, axis=-1, keepdims=True) + NORM_EPS)
    xh = x * r
    dyg = dy * g
    dx = r * (dyg - xh * jnp.mean(dyg * xh, axis=-1, keepdims=True))
    return dx, jnp.sum(dy * xh, axis=0, keepdims=True)


def _silu(x):
    return x * _sigmoid(x)


def _silu_grad(x):
    s = _sigmoid(x)
    return s * (1.0 + x * (1.0 - s))


def _rope(xs, cos, s_up, s_dn):
    return xs * cos + pltpu.roll(xs, ROPE_HALF, 1) * s_up + pltpu.roll(xs, HEAD - ROPE_HALF, 1) * s_dn


def _rope_bwd(dy, cos, s_up, s_dn):
    return dy * cos + pltpu.roll(dy * s_up, HEAD - ROPE_HALF, 1) + pltpu.roll(dy * s_dn, ROPE_HALF, 1)


HG_SUB = 128
HG_LEVELS = 7
HG_E_ROWS = (HG_LEVELS + 1) * HG_SUB
TN_DIMS = (((0,), (0,)), ((), ()))
NT_DIMS = (((1,), (1,)), ((), ()))


def _hg_constants():
    import numpy as np
    n = HG_SUB
    r = np.arange(n)[:, None]
    c = np.arange(n)[None, :]
    cs, ps = [], []
    for lvl in range(HG_LEVELS):
        m = (n // 2) >> lvl
        upper = (r % (2 * m)) >= m
        mid = (r // (2 * m)) * (2 * m) + m - 1
        cs.append(np.where(upper, (c > mid) & (c <= r), (c > r) & (c <= mid)))
        ps.append(((r // (2 * m)) == (c // (2 * m))) & upper & ((c % (2 * m)) < m))
    cs.append(c <= r)
    cs.append(np.ones((8, n), bool))
    cstack = np.concatenate(cs, 0).astype(np.float32)
    pstack = np.concatenate(ps, 0).astype(np.float32)
    pstack_t = np.concatenate([p.T for p in ps], 0).astype(np.float32)
    return (jnp.asarray(cstack, BF16), jnp.asarray(cstack[:HG_E_ROWS].T, BF16), jnp.asarray(pstack, F32),
            jnp.asarray(pstack_t, F32))


def _split_dot(c_bf, x):
    hi = _bf(x)
    lo = _bf(x - hi.astype(F32))
    r2 = jnp.dot(c_bf, jnp.concatenate([hi, lo], axis=1), preferred_element_type=F32)
    return r2[:, :HEAD] + r2[:, HEAD:]


def _hg_gates(hq, hf, lb):
    sq = _sigmoid(hq)
    sg = _sigmoid(hf)
    fg = lb + (1.0 - lb) * sg
    return sq, hq * sq, sg, fg, 1.0 - fg, jnp.log(fg)


def _hg_block_fwd(st, hq, hf, hi, lb, cstack, p_ref):
    _, q, _, _, k, g = _hg_gates(hq, hf, lb)
    v = hi
    e = _split_dot(cstack, g)
    bc = e[HG_LEVELS * HG_SUB:HG_E_ROWS]
    b_last = jnp.tile(e[HG_E_ROWS:], (HG_SUB // 8, 1))
    a = jnp.zeros((HG_SUB, HG_SUB), F32)
    for lvl in range(HG_LEVELS):
        x = jnp.exp(e[lvl * HG_SUB:(lvl + 1) * HG_SUB])
        a = a + p_ref[pl.ds(lvl * HG_SUB, HG_SUB), :] * lax.dot_general(_bf(q * x), _bf(k * x), NT_DIMS,
                                                                          preferred_element_type=F32)
    a_bf = _bf(a)
    diag = jnp.sum(q * k, axis=1, keepdims=True)
    o = (jnp.dot(a_bf, _bf(v), preferred_element_type=F32) + diag * v
         + lax.dot_general(_bf(q * jnp.exp(bc)), _bf(st), NT_DIMS, preferred_element_type=F32))
    kd = k * jnp.exp(b_last - bc)
    st_out = st * jnp.exp(b_last) + lax.dot_general(_bf(v), _bf(kd), TN_DIMS, preferred_element_type=F32)
    return st_out, o, a_bf


def _hg_block_bwd(st, dst_out, do, hq, hf, hi, lb, a_bf, cstack, cstack_t, p_ref, pt_ref):
    sq, q, sg, fg, k, g = _hg_gates(hq, hf, lb)
    v = hi
    e = _split_dot(cstack, g)
    bc = e[HG_LEVELS * HG_SUB:HG_E_ROWS]
    b_last = jnp.tile(e[HG_E_ROWS:], (HG_SUB // 8, 1))
    eb = jnp.exp(bc)
    qb = q * eb
    er = jnp.exp(b_last - bc)
    kd = k * er
    e_last = jnp.exp(b_last)
    do_bf, v_bf, dst_bf = _bf(do), _bf(v), _bf(dst_out)
    da = lax.dot_general(do_bf, v_bf, NT_DIMS, preferred_element_type=F32)
    dat = lax.dot_general(v_bf, do_bf, NT_DIMS, preferred_element_type=F32)
    d_diag = jnp.sum(do * v, axis=1, keepdims=True)
    dv = (lax.dot_general(a_bf, do_bf, TN_DIMS, preferred_element_type=F32)
          + jnp.sum(q * k, axis=1, keepdims=True) * do
          + lax.dot_general(_bf(kd), dst_bf, NT_DIMS, preferred_element_type=F32))
    dqb = jnp.dot(do_bf, _bf(st), preferred_element_type=F32)
    dst = dst_out * e_last + lax.dot_general(do_bf, _bf(qb), TN_DIMS, preferred_element_type=F32)
    dkd = jnp.dot(v_bf, dst_bf, preferred_element_type=F32)
    dq = dqb * eb + d_diag * k
    dk = dkd * er + d_diag * q
    d_last = (jnp.sum(dst_out * st * e_last, axis=0, keepdims=True)
              + jnp.sum(dkd * kd, axis=0, keepdims=True))
    des = []
    for lvl in range(HG_LEVELS):
        x = jnp.exp(e[lvl * HG_SUB:(lvl + 1) * HG_SUB])
        qh, kh = q * x, k * x
        dm = _bf(p_ref[pl.ds(lvl * HG_SUB, HG_SUB), :] * da)
        dmt = _bf(pt_ref[pl.ds(lvl * HG_SUB, HG_SUB), :] * dat)
        dqh = jnp.dot(dm, _bf(kh), preferred_element_type=F32)
        dkh = jnp.dot(dmt, _bf(qh), preferred_element_type=F32)
        dq = dq + dqh * x
        dk = dk + dkh * x
        des.append(dqh * qh + dkh * kh)
    des.append(dqb * qb - dkd * kd)
    dg = _split_dot(cstack_t, jnp.concatenate(des, axis=0)) + d_last
    dfg = dg / fg - dk
    dhq = dq * (sq * (1.0 + hq * (1.0 - sq)))
    dhf = dfg * (1.0 - lb) * sg * (1.0 - sg)
    return dst, dhq, dhf, dv, jnp.sum(dfg * (1.0 - sg), axis=0, keepdims=True)


def hgrn_fwd(proj_main, lb, consts, bl, lp, d):
    nh = d // HEAD
    rows_blk = _tile(lp, 768, SEQ_BLOCK)
    nb = lp // rows_blk
    spb = rows_blk // HG_SUB
    cstack, _, pstack, _ = consts

    def body(hq_ref, hf_ref, hi_ref, lb_ref, c_ref, p_ref, o_ref, st_ref, a_ref, s_ref):
        j = pl.program_id(2)

        @pl.when(j == 0)
        def _():
            s_ref[...] = jnp.zeros_like(s_ref)
            o_ref[pl.ds(0, HG_SUB), :] = jnp.zeros((HG_SUB, HEAD), F32)
            st_ref[0, 0, pl.ds(0, 1)] = jnp.zeros((1, HEAD, HEAD), F32)
            a_ref[0, 0, pl.ds(0, 1)] = jnp.zeros((1, HG_SUB, HG_SUB), BF16)

        lbv = lb_ref[...]
        cs = c_ref[...]

        def sub(n, carry):
            r = pl.multiple_of(n * HG_SUB, HG_SUB)
            st = s_ref[...]
            st_ref[0, 0, pl.ds(n, 1)] = st[None]
            st_out, o, a_bf = _hg_block_fwd(st, hq_ref[pl.ds(r, HG_SUB), :], hf_ref[pl.ds(r, HG_SUB), :],
                                            hi_ref[pl.ds(r, HG_SUB), :], lbv, cs, p_ref)
            s_ref[...] = st_out
            o_ref[pl.ds(r, HG_SUB), :] = o
            a_ref[0, 0, pl.ds(n, 1)] = a_bf[None]
            return carry

        lax.fori_loop(jnp.where(j == 0, 1, 0), spb, sub, 0)

    def colspec(off):
        return pl.BlockSpec((rows_blk, HEAD), functools.partial(lambda h, b, j, off: (b * nb + j, off + h), off=off))

    whole = lambda arr: pl.BlockSpec(arr.shape, lambda h, b, j: (0, 0))
    return pl.pallas_call(
        body, name="hgrn_fwd", grid=(nh, bl, nb),
        in_specs=[colspec(0), colspec(nh), colspec(2 * nh), pl.BlockSpec((1, HEAD), lambda h, b, j: (0, h)),
                  whole(cstack), whole(pstack)],
        out_specs=[pl.BlockSpec((rows_blk, HEAD), lambda h, b, j: (b * nb + j, h)),
                   pl.BlockSpec((1, 1, spb, HEAD, HEAD), lambda h, b, j: (b, h, j, 0, 0)),
                   pl.BlockSpec((1, 1, spb, HG_SUB, HG_SUB), lambda h, b, j: (b, h, j, 0, 0))],
        out_shape=[jax.ShapeDtypeStruct((bl * lp, d), F32),
                   jax.ShapeDtypeStruct((bl, nh, lp // HG_SUB, HEAD, HEAD), F32),
                   jax.ShapeDtypeStruct((bl, nh, lp // HG_SUB, HG_SUB, HG_SUB), BF16)],
        scratch_shapes=[pltpu.VMEM((HEAD, HEAD), F32)],
        compiler_params=pltpu.CompilerParams(dimension_semantics=("arbitrary", "arbitrary", "arbitrary")),
    )(proj_main, proj_main, proj_main, lb, cstack, pstack)


def hgrn_bwd(proj_main, lb, consts, states, a_mats, do_scan, bl, lp, d):
    nh = d // HEAD
    rows_blk = _tile(lp, 768, SEQ_BLOCK)
    nb = lp // rows_blk
    spb = rows_blk // HG_SUB
    cstack, cstack_t, pstack, pstack_t = consts

    def body(hq_ref, hf_ref, hi_ref, lb_ref, c_ref, ct_ref, p_ref, pt_ref, st_ref, a_ref, do_ref,
             dq_ref, df_ref, di_ref, dlb_ref, ds_ref):
        b_id, j = pl.program_id(1), pl.program_id(2)
        blk = nb - 1 - j

        @pl.when(j == 0)
        def _():
            ds_ref[...] = jnp.zeros_like(ds_ref)

        @pl.when((j == 0) & (b_id == 0))
        def _():
            dlb_ref[...] = jnp.zeros_like(dlb_ref)

        @pl.when(blk == 0)
        def _():
            for ref in (dq_ref, df_ref, di_ref):
                ref[pl.ds(0, HG_SUB), :] = jnp.zeros((HG_SUB, HEAD), ref.dtype)

        lbv = lb_ref[...]
        cs = c_ref[...]
        cst = ct_ref[...]

        def sub(i, carry):
            n = spb - 1 - i
            r = pl.multiple_of(n * HG_SUB, HG_SUB)
            dst, dhq, dhf, dhi, dlb = _hg_block_bwd(
                st_ref[0, 0, pl.ds(n, 1)][0], ds_ref[...], do_ref[pl.ds(r, HG_SUB), :],
                hq_ref[pl.ds(r, HG_SUB), :], hf_ref[pl.ds(r, HG_SUB), :], hi_ref[pl.ds(r, HG_SUB), :], lbv,
                a_ref[0, 0, pl.ds(n, 1)][0], cs, cst, p_ref, pt_ref)
            ds_ref[...] = dst
            dq_ref[pl.ds(r, HG_SUB), :] = dhq.astype(dq_ref.dtype)
            df_ref[pl.ds(r, HG_SUB), :] = dhf.astype(df_ref.dtype)
            di_ref[pl.ds(r, HG_SUB), :] = dhi.astype(di_ref.dtype)
            dlb_ref[...] += dlb
            return carry

        lax.fori_loop(0, jnp.where(blk == 0, spb - 1, spb), sub, 0)

    def colspec(off):
        return pl.BlockSpec((rows_blk, HEAD),
                            functools.partial(lambda h, b, j, off: (b * nb + nb - 1 - j, off + h), off=off))

    whole = lambda arr: pl.BlockSpec(arr.shape, lambda h, b, j: (0, 0))
    mats = lambda: pl.BlockSpec((1, 1, spb, HEAD, HEAD), lambda h, b, j: (b, h, nb - 1 - j, 0, 0))
    t_rows = bl * lp
    return pl.pallas_call(
        body, name="hgrn_bwd", grid=(nh, bl, nb),
        in_specs=[colspec(0), colspec(nh), colspec(2 * nh), pl.BlockSpec((1, HEAD), lambda h, b, j: (0, h)),
                  whole(cstack), whole(cstack_t), whole(pstack), whole(pstack_t), mats(), mats(), colspec(0)],
        out_specs=[colspec(0), colspec(0), colspec(0), pl.BlockSpec((1, HEAD), lambda h, b, j: (0, h))],
        out_shape=[jax.ShapeDtypeStruct((t_rows, d), BF16)] * 3 + [jax.ShapeDtypeStruct((1, d), F32)],
        scratch_shapes=[pltpu.VMEM((HEAD, HEAD), F32)],
        compiler_params=pltpu.CompilerParams(dimension_semantics=("arbitrary", "arbitrary", "arbitrary")),
    )(proj_main, proj_main, proj_main, lb, cstack, cstack_t, pstack, pstack_t, states, a_mats, do_scan)


def _allowed(row0, col0, nr, nc, transposed=False):
    if transposed:
        col = col0 + lax.broadcasted_iota(jnp.int32, (nc, 1), 0)
        row = row0 + lax.broadcasted_iota(jnp.int32, (1, nr), 1)
    else:
        row = row0 + lax.broadcasted_iota(jnp.int32, (nr, 1), 0)
        col = col0 + lax.broadcasted_iota(jnp.int32, (1, nc), 1)
    return (col <= row) & ((col >= PAD_FRONT) | (row < PAD_FRONT))


def attn_fwd(q_cat, k_cat, v, bl, lp, nm, scale):
    tq = tk = SEQ_BLOCK
    nq = lp // tq

    def body(q_ref, k_ref, v_ref, o_ref, lse_ref, m_ref, l_ref, acc_ref):
        i = pl.program_id(2)
        q = q_ref[...]
        m_ref[...] = jnp.full_like(m_ref, NEG)
        l_ref[...] = jnp.zeros_like(l_ref)
        acc_ref[...] = jnp.zeros_like(acc_ref)

        def kstep(c, carry):
            c0 = pl.multiple_of(c * tk, tk)
            s = lax.dot_general(q, k_ref[pl.ds(c0, tk), :], NT_DIMS, preferred_element_type=F32) * scale
            s = jnp.where(_allowed(i * tq, c * tk, tq, tk), s, NEG)
            m_old = m_ref[...]
            m_new = jnp.maximum(m_old, jnp.max(s, axis=1, keepdims=True))
            alpha = jnp.exp(m_old - m_new)
            p = jnp.exp(s - m_new)
            l_ref[...] = alpha * l_ref[...] + jnp.sum(p, axis=1, keepdims=True)
            acc_ref[...] = alpha * acc_ref[...] + jnp.dot(_bf(p), v_ref[pl.ds(c0, tk), :],
                                                          preferred_element_type=F32)
            m_ref[...] = m_new
            return carry

        lax.fori_loop(0, i + 1, kstep, 0)
        o_ref[...] = (acc_ref[...] / l_ref[...]).astype(o_ref.dtype)
        lse_ref[0, 0] = m_ref[...] + jnp.log(l_ref[...])

    return pl.pallas_call(
        body, name="attn_fwd", grid=(bl, nm, nq),
        in_specs=[pl.BlockSpec((tq, QK_PAD), lambda b, h, i: (b * nq + i, h)),
                  pl.BlockSpec((lp, QK_PAD), lambda b, h, i: (b, h)),
                  pl.BlockSpec((lp, HEAD), lambda b, h, i: (b, h))],
        out_specs=[pl.BlockSpec((tq, HEAD), lambda b, h, i: (b * nq + i, h)),
                   pl.BlockSpec((1, 1, tq, 1), lambda b, h, i: (b, h, i, 0))],
        out_shape=[jax.ShapeDtypeStruct((bl * lp, nm * HEAD), BF16),
                   jax.ShapeDtypeStruct((bl, nm, lp, 1), F32)],
        scratch_shapes=[pltpu.VMEM((tq, 1), F32), pltpu.VMEM((tq, 1), F32), pltpu.VMEM((tq, HEAD), F32)],
        compiler_params=pltpu.CompilerParams(dimension_semantics=("arbitrary", "arbitrary", "arbitrary")),
    )(q_cat, k_cat, v)


def attn_bwd_dq(q_cat, k_cat, v, o, do, lse, bl, lp, nm, scale):
    tq = tk = SEQ_BLOCK
    nq = lp // tq

    def body(q_ref, k_ref, v_ref, o_ref, do_ref, lse_ref, dq_ref, dl_ref, acc_ref):
        i = pl.program_id(2)
        q = q_ref[...]
        do_b = do_ref[...]
        delta = jnp.sum(o_ref[...].astype(F32) * do_b.astype(F32), axis=1, keepdims=True)
        lse_b = lse_ref[0, 0]
        acc_ref[...] = jnp.zeros_like(acc_ref)

        def kstep(c, carry):
            c0 = pl.multiple_of(c * tk, tk)
            ks = k_ref[pl.ds(c0, tk), :]
            s = lax.dot_general(q, ks, NT_DIMS, preferred_element_type=F32) * scale
            p = jnp.where(_allowed(i * tq, c * tk, tq, tk), jnp.exp(s - lse_b), 0.0)
            dp = lax.dot_general(do_b, v_ref[pl.ds(c0, tk), :], NT_DIMS, preferred_element_type=F32)
            ds = p * (dp - delta)
            acc_ref[...] += jnp.dot(_bf(ds), ks, preferred_element_type=F32)
            return carry

        lax.fori_loop(0, i + 1, kstep, 0)
        dq_ref[...] = acc_ref[...] * scale
        dl_ref[0, 0] = delta

    return pl.pallas_call(
        body, name="attn_bwd_dq", grid=(bl, nm, nq),
        in_specs=[pl.BlockSpec((tq, QK_PAD), lambda b, h, i: (b * nq + i, h)),
                  pl.BlockSpec((lp, QK_PAD), lambda b, h, i: (b, h)),
                  pl.BlockSpec((lp, HEAD), lambda b, h, i: (b, h)),
                  pl.BlockSpec((tq, HEAD), lambda b, h, i: (b * nq + i, h)),
                  pl.BlockSpec((tq, HEAD), lambda b, h, i: (b * nq + i, h)),
                  pl.BlockSpec((1, 1, tq, 1), lambda b, h, i: (b, h, i, 0))],
        out_specs=[pl.BlockSpec((tq, QK_PAD), lambda b, h, i: (b * nq + i, h)),
                   pl.BlockSpec((1, 1, tq, 1), lambda b, h, i: (b, h, i, 0))],
        out_shape=[jax.ShapeDtypeStruct((bl * lp, nm * QK_PAD), F32),
                   jax.ShapeDtypeStruct((bl, nm, lp, 1), F32)],
        scratch_shapes=[pltpu.VMEM((tq, QK_PAD), F32)],
        compiler_params=pltpu.CompilerParams(dimension_semantics=("arbitrary", "arbitrary", "arbitrary")),
    )(q_cat, k_cat, v, o, do, lse)


def attn_bwd_dkv(q_cat, k_cat, v, do, lse_row, delta_row, bl, lp, nm, scale):
    tq = tk = SEQ_BLOCK
    nq = lp // tq

    def body(q_ref, k_ref, v_ref, do_ref, lse_ref, dl_ref, dk_ref, dv_ref):
        i = pl.program_id(2)
        kt = k_ref[...]
        vt = v_ref[...]
        dk_ref[...] = jnp.zeros_like(dk_ref)
        dv_ref[...] = jnp.zeros_like(dv_ref)

        def qstep(c, carry):
            c0 = pl.multiple_of(c * tq, tq)
            qs = q_ref[pl.ds(c0, tq), :]
            dos = do_ref[pl.ds(c0, tq), :]
            st = lax.dot_general(kt, qs, NT_DIMS, preferred_element_type=F32) * scale
            pt = jnp.where(_allowed(c * tq, i * tk, tq, tk, transposed=True),
                           jnp.exp(st - lse_ref[0, 0, pl.ds(c, 1)][0]), 0.0)
            dv_ref[...] += jnp.dot(_bf(pt), dos, preferred_element_type=F32)
            dpt = lax.dot_general(vt, dos, NT_DIMS, preferred_element_type=F32)
            dst = pt * (dpt - dl_ref[0, 0, pl.ds(c, 1)][0])
            dk_ref[...] += jnp.dot(_bf(dst), qs, preferred_element_type=F32)
            return carry

        lax.fori_loop(i, nq, qstep, 0)
        dk_ref[...] = dk_ref[...] * scale

    return pl.pallas_call(
        body, name="attn_bwd_dkv", grid=(bl, nm, nq),
        in_specs=[pl.BlockSpec((lp, QK_PAD), lambda b, h, i: (b, h)),
                  pl.BlockSpec((tk, QK_PAD), lambda b, h, i: (b * nq + i, h)),
                  pl.BlockSpec((tk, HEAD), lambda b, h, i: (b * nq + i, h)),
                  pl.BlockSpec((lp, HEAD), lambda b, h, i: (b, h)),
                  pl.BlockSpec((1, 1, nq, 1, tq), lambda b, h, i: (b, h, 0, 0, 0)),
                  pl.BlockSpec((1, 1, nq, 1, tq), lambda b, h, i: (b, h, 0, 0, 0))],
        out_specs=[pl.BlockSpec((tk, QK_PAD), lambda b, h, i: (b * nq + i, h)),
                   pl.BlockSpec((tk, HEAD), lambda b, h, i: (b * nq + i, h))],
        out_shape=[jax.ShapeDtypeStruct((bl * lp, nm * QK_PAD), F32),
                   jax.ShapeDtypeStruct((bl * lp, nm * HEAD), F32)],
        compiler_params=pltpu.CompilerParams(dimension_semantics=("arbitrary", "arbitrary", "arbitrary")),
    )(q_cat, k_cat, v, do, lse_row, delta_row)


def _key_query_mask(key0, qry0, nk, nq_):
    key = key0 + lax.broadcasted_iota(jnp.int32, (nk, 1), 0)
    qry = qry0 + lax.broadcasted_iota(jnp.int32, (1, nq_), 1)
    return (key <= qry) & (key >= PAD_FRONT)


def attn_fwd_t(q_cat, k_cat, v_t, bl, lp, nm, scale):
    tq = tk = SEQ_BLOCK
    nq = lp // tq

    def body(q_ref, k_ref, vt_ref, o_ref, lse_ref, m_ref, l_ref, acc_ref):
        i = pl.program_id(2)
        q = q_ref[...]
        m_ref[...] = jnp.full_like(m_ref, NEG)
        l_ref[...] = jnp.zeros_like(l_ref)
        acc_ref[...] = jnp.zeros_like(acc_ref)

        def step(c, masked):
            c0 = pl.multiple_of(c * tk, tk)
            st = lax.dot_general(k_ref[pl.ds(c0, tk), :], q, NT_DIMS, preferred_element_type=F32) * scale
            if masked:
                st = jnp.where(_key_query_mask(c * tk, i * tq, tk, tq), st, NEG)
            m_old = m_ref[...]
            m_new = jnp.maximum(m_old, jnp.max(st, axis=0, keepdims=True))
            alpha = jnp.exp(m_old - m_new)
            pt = jnp.exp(st - m_new)
            l_ref[...] = alpha * l_ref[...] + jnp.sum(pt, axis=0, keepdims=True)
            acc_ref[...] = alpha * acc_ref[...] + jnp.dot(vt_ref[0, 0, pl.ds(c, 1)][0], _bf(pt),
                                                          preferred_element_type=F32)
            m_ref[...] = m_new

        step(0, True)

        def mid(c, carry):
            step(c, False)
            return carry

        lax.fori_loop(1, i, mid, 0)

        @pl.when(i > 0)
        def _():
            step(i, True)

        o_ref[...] = jnp.transpose(acc_ref[...] / l_ref[...]).astype(o_ref.dtype)
        lse_ref[0, 0, 0] = m_ref[...] + jnp.log(l_ref[...])

    return pl.pallas_call(
        body, name="attn_fwd", grid=(bl, nm, nq),
        in_specs=[pl.BlockSpec((tq, QK_PAD), lambda b, h, i: (b * nq + i, h)),
                  pl.BlockSpec((lp, QK_PAD), lambda b, h, i: (b, h)),
                  pl.BlockSpec((1, 1, nq, HEAD, tk), lambda b, h, i: (b, h, 0, 0, 0))],
        out_specs=[pl.BlockSpec((tq, HEAD), lambda b, h, i: (b * nq + i, h)),
                   pl.BlockSpec((1, 1, 1, 1, tq), lambda b, h, i: (b, h, i, 0, 0))],
        out_shape=[jax.ShapeDtypeStruct((bl * lp, nm * HEAD), BF16),
                   jax.ShapeDtypeStruct((bl, nm, nq, 1, tq), F32)],
        scratch_shapes=[pltpu.VMEM((1, tq), F32), pltpu.VMEM((1, tq), F32), pltpu.VMEM((HEAD, tq), F32)],
        compiler_params=pltpu.CompilerParams(dimension_semantics=("arbitrary", "arbitrary", "arbitrary")),
    )(q_cat, k_cat, v_t)


def attn_bwd_t(q_cat, k_cat, k_t, v, o, do, lse, bl, lp, nm, scale):
    tq = tk = SEQ_BLOCK
    nq = lp // tq

    def body(q_ref, k_ref, kt_ref, v_ref, o_ref, do_ref, lse_ref, dq_ref, dk_ref, dv_ref, dqt_ref):
        i = pl.program_id(2)

        @pl.when(i == 0)
        def _():
            dqt_ref[...] = jnp.zeros_like(dqt_ref)

        kt = k_ref[...]
        ktt = kt_ref[0, 0, 0]
        vt = v_ref[...]
        dk_ref[...] = jnp.zeros_like(dk_ref)
        dv_ref[...] = jnp.zeros_like(dv_ref)
        ones8 = jnp.ones((8, HEAD), BF16)

        def step(c, masked):
            c0 = pl.multiple_of(c * tq, tq)
            qs = q_ref[pl.ds(c0, tq), :]
            dos = do_ref[pl.ds(c0, tq), :]
            prod = dos.astype(F32) * o_ref[pl.ds(c0, tq), :].astype(F32)
            hi = _bf(prod)
            lo = _bf(prod - hi.astype(F32))
            delta8 = (lax.dot_general(ones8, hi, NT_DIMS, preferred_element_type=F32)
                      + lax.dot_general(ones8, lo, NT_DIMS, preferred_element_type=F32))
            st = lax.dot_general(kt, qs, NT_DIMS, preferred_element_type=F32) * scale
            pt = jnp.exp(st - lse_ref[0, 0, pl.ds(c, 1)][0])
            if masked:
                pt = jnp.where(_key_query_mask(i * tk, c * tq, tk, tq), pt, 0.0)
            dv_ref[...] += jnp.dot(_bf(pt), dos, preferred_element_type=F32)
            dpt = lax.dot_general(vt, dos, NT_DIMS, preferred_element_type=F32)
            dst = _bf(pt * (dpt - jnp.tile(delta8, (tk // 8, 1))))
            dk_ref[...] += jnp.dot(dst, qs, preferred_element_type=F32)
            dqt_ref[pl.ds(c, 1)] += jnp.dot(ktt, dst, preferred_element_type=F32)[None]

        step(i, True)

        def rest_masked(c, carry):
            step(c, True)
            return carry

        def rest(c, carry):
            step(c, False)
            return carry

        @pl.when(i == 0)
        def _():
            lax.fori_loop(1, nq, rest_masked, 0)

        @pl.when(i > 0)
        def _():
            lax.fori_loop(i + 1, nq, rest, 0)

        dk_ref[...] = dk_ref[...] * scale

        @pl.when(i == nq - 1)
        def _():
            for c in range(nq):
                dq_ref[pl.ds(c * tq, tq), :] = jnp.transpose(dqt_ref[c]) * scale

    return pl.pallas_call(
        body, name="attn_bwd", grid=(bl, nm, nq),
        in_specs=[pl.BlockSpec((lp, QK_PAD), lambda b, h, i: (b, h)),
                  pl.BlockSpec((tk, QK_PAD), lambda b, h, i: (b * nq + i, h)),
                  pl.BlockSpec((1, 1, 1, QK_PAD, tk), lambda b, h, i: (b, h, i, 0, 0)),
                  pl.BlockSpec((tk, HEAD), lambda b, h, i: (b * nq + i, h)),
                  pl.BlockSpec((lp, HEAD), lambda b, h, i: (b, h)),
                  pl.BlockSpec((lp, HEAD), lambda b, h, i: (b, h)),
                  pl.BlockSpec((1, 1, nq, 1, tq), lambda b, h, i: (b, h, 0, 0, 0))],
        out_specs=[pl.BlockSpec((lp, QK_PAD), lambda b, h, i: (b, h)),
                   pl.BlockSpec((tk, QK_PAD), lambda b, h, i: (b * nq + i, h)),
                   pl.BlockSpec((tk, HEAD), lambda b, h, i: (b * nq + i, h))],
        out_shape=[jax.ShapeDtypeStruct((bl * lp, nm * QK_PAD), F32),
                   jax.ShapeDtypeStruct((bl * lp, nm * QK_PAD), F32),
                   jax.ShapeDtypeStruct((bl * lp, nm * HEAD), F32)],
        scratch_shapes=[pltpu.VMEM((nq, QK_PAD, tq), F32)],
        compiler_params=pltpu.CompilerParams(dimension_semantics=("arbitrary", "arbitrary", "arbitrary")),
    )(q_cat, k_cat, k_t, v, o, do, lse)


def _place():
    return lax.axis_index("x"), lax.axis_index("y"), lax.axis_index("c")


def gather_shards(packed):
    hbm = pl.BlockSpec(memory_space=pl.ANY)

    def body(src_ref, out_ref, send_sems, recv_sems, local_sem):
        x, y, c = _place()
        me = 2 * x + y
        chips = [(1 - x, y), (x, 1 - y), (1 - x, 1 - y)]
        local = pltpu.make_async_copy(src_ref, out_ref.at[me], local_sem)
        local.start()
        sends = []
        for k, (px, py) in enumerate(chips):
            cp = pltpu.make_async_remote_copy(src_ref=src_ref, dst_ref=out_ref.at[me], send_sem=send_sems.at[k],
                                              recv_sem=recv_sems.at[k], device_id=(px, py, c), device_id_type=MESH)
            cp.start()
            sends.append(cp)
        for k, (px, py) in enumerate(chips):
            pltpu.make_async_remote_copy(src_ref=src_ref, dst_ref=out_ref.at[2 * px + py], send_sem=send_sems.at[k],
                                         recv_sem=recv_sems.at[k], device_id=(px, py, c),
                                         device_id_type=MESH).wait_recv()
        for cp in sends:
            cp.wait_send()
        local.wait()

    return pl.pallas_call(
        body, name="gather_shards", in_specs=[hbm], out_specs=hbm,
        out_shape=jax.ShapeDtypeStruct((4,) + packed.shape, packed.dtype),
        scratch_shapes=[pltpu.SemaphoreType.DMA((3,)), pltpu.SemaphoreType.DMA((3,)), pltpu.SemaphoreType.DMA],
    )(packed)


def exchange_grads(send, small):
    hbm = pl.BlockSpec(memory_space=pl.ANY)

    def body(send_ref, small_ref, recv_ref, all_ref, send_sems, recv_sems, ssend_sems, srecv_sems, local_sem):
        x, y, c = _place()
        me = 4 * x + 2 * y + c
        chips = [(1 - x, y), (x, 1 - y), (1 - x, 1 - y)]
        local = pltpu.make_async_copy(small_ref, all_ref.at[me], local_sem)
        local.start()
        sends = []
        for k, (px, py) in enumerate(chips):
            cp = pltpu.make_async_remote_copy(src_ref=send_ref.at[2 * px + py], dst_ref=recv_ref.at[k],
                                              send_sem=send_sems.at[k], recv_sem=recv_sems.at[k],
                                              device_id=(px, py, c), device_id_type=MESH)
            cp.start()
            sends.append(cp)
        others = [(x ^ ((r >> 2) & 1), y ^ ((r >> 1) & 1), c ^ (r & 1)) for r in range(1, 8)]
        for r, peer in enumerate(others):
            cp = pltpu.make_async_remote_copy(src_ref=small_ref, dst_ref=all_ref.at[me], send_sem=ssend_sems.at[r],
                                              recv_sem=srecv_sems.at[r], device_id=peer, device_id_type=MESH)
            cp.start()
            sends.append(cp)
        for k, (px, py) in enumerate(chips):
            pltpu.make_async_remote_copy(src_ref=send_ref.at[2 * px + py], dst_ref=recv_ref.at[k],
                                         send_sem=send_sems.at[k], recv_sem=recv_sems.at[k],
                                         device_id=(px, py, c), device_id_type=MESH).wait_recv()
        for r, (px, py, pc) in enumerate(others):
            pltpu.make_async_remote_copy(src_ref=small_ref, dst_ref=all_ref.at[4 * px + 2 * py + pc],
                                         send_sem=ssend_sems.at[r], recv_sem=srecv_sems.at[r],
                                         device_id=(px, py, pc), device_id_type=MESH).wait_recv()
        for cp in sends:
            cp.wait_send()
        local.wait()

    return pl.pallas_call(
        body, name="exchange_grads", in_specs=[hbm, hbm], out_specs=[hbm, hbm],
        out_shape=[jax.ShapeDtypeStruct((3,) + send.shape[1:], send.dtype),
                   jax.ShapeDtypeStruct((8,) + small.shape, small.dtype)],
        scratch_shapes=[pltpu.SemaphoreType.DMA((3,)), pltpu.SemaphoreType.DMA((3,)),
                        pltpu.SemaphoreType.DMA((7,)), pltpu.SemaphoreType.DMA((7,)), pltpu.SemaphoreType.DMA],
    )(send, small)


def swap_with_sibling(part):
    hbm = pl.BlockSpec(memory_space=pl.ANY)

    def body(src_ref, out_ref, send_sem, recv_sem):
        x, y, c = _place()
        cp = pltpu.make_async_remote_copy(src_ref=src_ref, dst_ref=out_ref, send_sem=send_sem, recv_sem=recv_sem,
                                          device_id=(x, y, 1 - c), device_id_type=MESH)
        cp.start()
        cp.wait()

    return pl.pallas_call(
        body, name="swap_with_sibling", in_specs=[hbm], out_specs=hbm,
        out_shape=jax.ShapeDtypeStruct(part.shape, part.dtype),
        scratch_shapes=[pltpu.SemaphoreType.DMA, pltpu.SemaphoreType.DMA],
    )(part)


def adamw(name, w, g, m, v):
    r, c = w.shape
    tr = r if r * c <= 65536 else _tile(r, 128, 8)

    def body(w_ref, g_ref, m_ref, v_ref, d_ref, nm_ref, nv_ref):
        gv = g_ref[...]
        m_new = ADAM_B1 * m_ref[...] + (1.0 - ADAM_B1) * gv
        v_new = ADAM_B2 * v_ref[...] + (1.0 - ADAM_B2) * (gv * gv)
        m_hat = m_new / (1.0 - ADAM_B1 ** ADAM_STEP)
        v_hat = v_new / (1.0 - ADAM_B2 ** ADAM_STEP)
        d_ref[...] = -ADAM_LR * (m_hat / (jnp.sqrt(v_hat) + ADAM_EPS) + ADAM_WD * w_ref[...])
        nm_ref[...] = m_new
        nv_ref[...] = v_new

    spec = pl.BlockSpec((tr, c), lambda i: (i, 0))
    return pl.pallas_call(
        body, name=name, grid=(r // tr,), in_specs=[spec] * 4, out_specs=[spec] * 3,
        out_shape=[jax.ShapeDtypeStruct((r, c), F32)] * 3,
        compiler_params=pltpu.CompilerParams(dimension_semantics=("arbitrary",)),
    )(w, g, m, v)


def _pack_rows(shapes):
    rows = [(r * c) // PACK_W for (r, c) in shapes]
    for (r, c) in shapes:
        assert (r * c) % PACK_W == 0
    total = sum(rows)
    return rows, -(-total // 16) * 16


def pack_shard(parts, total_rows):
    flat = jnp.concatenate([p.reshape(-1, PACK_W) for p in parts], axis=0)
    return jnp.pad(flat, ((0, total_rows - flat.shape[0]), (0, 0)))


def split_full(name, full, s):
    if name in COL_SHARDED:
        c = full.shape[1] // 4
        return full[:, s * c:(s + 1) * c]
    r = full.shape[0] // 4
    return full[s * r:(s + 1) * r]


def join_shards(name, shards):
    return jnp.concatenate(shards, axis=1 if name in COL_SHARDED else 0)


def kernel(x, meta_tokens, w_in, b_gate, lb_logits, hg_norm_g, w_hg_o, q_a_norm_g, w_q_b, kv_a_norm_g, w_kv_b, w_mla_o, w_out, mix_pre_g, mix_post_g, ffn_pre_g, ffn_post_g, w_ffn_in, w_ffn_out, loss_target, m_meta_tokens, m_w_in, m_b_gate, m_lb_logits, m_hg_norm_g, m_w_hg_o, m_q_a_norm_g, m_w_q_b, m_kv_a_norm_g, m_w_kv_b, m_w_mla_o, m_w_out, m_mix_pre_g, m_mix_post_g, m_ffn_pre_g, m_ffn_post_g, m_w_ffn_in, m_w_ffn_out, v_meta_tokens, v_w_in, v_b_gate, v_lb_logits, v_hg_norm_g, v_w_hg_o, v_q_a_norm_g, v_w_q_b, v_kv_a_norm_g, v_w_kv_b, v_w_mla_o, v_w_out, v_mix_pre_g, v_mix_post_g, v_ffn_pre_g, v_ffn_post_g, v_w_ffn_in, v_w_ffn_out):
    wts = dict(meta_tokens=meta_tokens, w_in=w_in[0], b_gate=b_gate, lb_logits=lb_logits, hg_norm_g=hg_norm_g,
               w_hg_o=w_hg_o[0], q_a_norm_g=q_a_norm_g, w_q_b=w_q_b[0], kv_a_norm_g=kv_a_norm_g, w_kv_b=w_kv_b[0],
               w_mla_o=w_mla_o[0], w_out=w_out[0], mix_pre_g=mix_pre_g, mix_post_g=mix_post_g, ffn_pre_g=ffn_pre_g,
               ffn_post_g=ffn_post_g, w_ffn_in=w_ffn_in[0], w_ffn_out=w_ffn_out[0])
    mom_m = dict(meta_tokens=m_meta_tokens, w_in=m_w_in[0], b_gate=m_b_gate, lb_logits=m_lb_logits,
                 hg_norm_g=m_hg_norm_g, w_hg_o=m_w_hg_o[0], q_a_norm_g=m_q_a_norm_g, w_q_b=m_w_q_b[0],
                 kv_a_norm_g=m_kv_a_norm_g, w_kv_b=m_w_kv_b[0], w_mla_o=m_w_mla_o[0], w_out=m_w_out[0],
                 mix_pre_g=m_mix_pre_g, mix_post_g=m_mix_post_g, ffn_pre_g=m_ffn_pre_g, ffn_post_g=m_ffn_post_g,
                 w_ffn_in=m_w_ffn_in[0], w_ffn_out=m_w_ffn_out[0])
    mom_v = dict(meta_tokens=v_meta_tokens, w_in=v_w_in[0], b_gate=v_b_gate, lb_logits=v_lb_logits,
                 hg_norm_g=v_hg_norm_g, w_hg_o=v_w_hg_o[0], q_a_norm_g=v_q_a_norm_g, w_q_b=v_w_q_b[0],
                 kv_a_norm_g=v_kv_a_norm_g, w_kv_b=v_w_kv_b[0], w_mla_o=v_w_mla_o[0], w_out=v_w_out[0],
                 mix_pre_g=v_mix_pre_g, mix_post_g=v_mix_post_g, ffn_pre_g=v_ffn_pre_g, ffn_post_g=v_ffn_post_g,
                 w_ffn_in=v_w_ffn_in[0], w_ffn_out=v_w_ffn_out[0])

    bl, seq, d = x.shape
    lp = PAD_FRONT + N_META + seq
    t_rows = bl * lp
    nh = d // HEAD
    ql, kvl = wts["w_q_b"].shape[0], wts["w_kv_b"].shape[0]
    nm = (4 * wts["w_mla_o"].shape[0]) // HEAD
    ffn = 4 * wts["w_ffn_out"].shape[0]
    mla_w = ql + kvl + HEAD
    assert ql == kvl and ql % HEAD == 0 and seq % SEQ_BLOCK == 0 and d % HEAD == 0
    scale = (HEAD + ROPE) ** -0.5
    my_chip = 2 * lax.axis_index("x") + lax.axis_index("y")

    shard_shapes = [wts[n].shape for n in BIG]
    pack_rows, pack_total = _pack_rows(shard_shapes)
    mcols = meta_tokens.shape[1]
    meta_rows = (N_META * mcols * 2) // PACK_W
    assert (N_META * mcols * 2) % PACK_W == 0
    meta_bits = lax.bitcast_convert_type(meta_tokens, BF16).reshape(meta_rows, PACK_W)
    gathered = gather_shards(pack_shard([meta_bits] + [_bf(wts[n]) for n in BIG],
                                        -(-(meta_rows + sum(pack_rows)) // 16) * 16))
    meta_full = jnp.concatenate(
        [lax.bitcast_convert_type(gathered[s, :meta_rows].reshape(N_META, mcols, 2), F32) for s in range(4)], axis=1)
    full = {}
    off = meta_rows
    for n, rows, (r, c) in zip(BIG, pack_rows, shard_shapes):
        full[n] = join_shards(n, [gathered[s, off:off + rows].reshape(r, c) for s in range(4)])
        off += rows
    w_main = jnp.concatenate([full["w_in"][:, :4 * d], full["w_in"][:, -2 * d:]], axis=1)
    w_mla = jnp.pad(full["w_in"][:, 4 * d:4 * d + ql + kvl + ROPE], ((0, 0), (0, HEAD - ROPE)))
    w_qb = jnp.pad(full["w_q_b"].reshape(ql, nm, HEAD + ROPE), ((0, 0), (0, 0), (0, QK_PAD - HEAD - ROPE))
                   ).reshape(ql, nm * QK_PAD)
    w_kvb = full["w_kv_b"]

    h0 = jnp.concatenate([jnp.zeros((bl, PAD_FRONT, d), F32), jnp.broadcast_to(meta_full[None], (bl, N_META, d)), x],
                         axis=1).reshape(t_rows, d)
    tgt = jnp.concatenate([jnp.zeros((bl, PAD_FRONT + N_META, d), F32), loss_target], axis=1).reshape(t_rows, d)
    pos = (jnp.arange(lp, dtype=jnp.int32) - PAD_FRONT).astype(F32)
    inv_freq = 1.0 / (ROPE_THETA ** (jnp.arange(0, ROPE, 2, dtype=F32) / ROPE))
    ang = pos[:, None] * inv_freq[None, :]
    zeros32 = jnp.zeros((lp, ROPE_HALF), F32)
    zeros64 = jnp.zeros((lp, HEAD - ROPE), F32)
    t_cos = jnp.concatenate([jnp.cos(ang), jnp.cos(ang), zeros64], axis=1)
    t_up = jnp.concatenate([zeros32, jnp.sin(ang), zeros64], axis=1)
    t_dn = jnp.concatenate([-jnp.sin(ang), zeros32, zeros64], axis=1)
    real = jnp.broadcast_to((jnp.arange(lp) >= PAD_FRONT + N_META).astype(F32)[:, None], (lp, d))
    lb_soft = jax.nn.softmax(lb_logits.astype(F32), axis=0)
    lb = lb_soft[0:1]

    (u1,) = rowwise("norm_mix_pre", lambda h, g: _rms(h, g), [(h0, d, 0)], [], [mix_pre_g], [(d, BF16)])
    proj_main = matmul("proj_main", u1, w_main, "nn")
    proj_mla = matmul("proj_mla", u1, w_mla, "nn")
    hg_consts = _hg_constants()
    o_scan, states, a_mats = hgrn_fwd(proj_main, lb, hg_consts, bl, lp, d)

    def hg_out_fn(o, hg, g):
        return jnp.concatenate([_rms(o[:, h * HEAD:(h + 1) * HEAD], g) for h in range(nh)], axis=1) * _silu(hg)

    (o_hg,) = rowwise("hgrn_out", hg_out_fn, [(o_scan, d, 0), (proj_main, d, 3)], [], [hg_norm_g], [(d, BF16)])
    y_a = matmul("y_a", o_hg, _bf(full["w_hg_o"]), "nn")

    qn, kvn = rowwise("mla_norms", lambda cq, ckv, gq, gk: (_rms(cq, gq), _rms(ckv, gk)),
                      [(proj_mla, ql, 0), (proj_mla, kvl, 1)], [], [q_a_norm_g, kv_a_norm_g],
                      [(ql, BF16), (kvl, BF16)])
    q_full = matmul("q_up", qn, w_qb, "nn")
    kv_full = matmul("kv_up", kvn, w_kvb, "nn")

    def mla_prep_fn(qf, kvf, kpe, cos, s_up, s_dn):
        kpe_r = _rope(kpe, cos, s_up, s_dn)
        qs, ks, vs = [], [], []
        for h in range(nm):
            qs += [qf[:, h * QK_PAD:h * QK_PAD + HEAD], _rope(qf[:, h * QK_PAD + HEAD:(h + 1) * QK_PAD], cos, s_up, s_dn)]
            ks += [kvf[:, h * QK_PAD:h * QK_PAD + HEAD], kpe_r]
            vs += [kvf[:, h * QK_PAD + HEAD:(h + 1) * QK_PAD]]
        return jnp.concatenate(qs, axis=1), jnp.concatenate(ks, axis=1), jnp.concatenate(vs, axis=1)

    kpe_blk = (ql + kvl) // HEAD
    q_cat, k_cat, v_att = rowwise("mla_prep", mla_prep_fn,
                                  [(q_full, nm * QK_PAD, 0), (kv_full, nm * QK_PAD, 0), (proj_mla, HEAD, kpe_blk)],
                                  [t_cos, t_up, t_dn], [], [(nm * QK_PAD, BF16), (nm * QK_PAD, BF16), (nm * HEAD, BF16)])
    nq = lp // SEQ_BLOCK
    v_t = v_att.reshape(bl, nq, SEQ_BLOCK, nm, HEAD).transpose(0, 3, 1, 4, 2)
    k_t = k_cat.reshape(bl, nq, SEQ_BLOCK, nm, QK_PAD).transpose(0, 3, 1, 4, 2)
    o_mla, lse = attn_fwd_t(q_cat, k_cat, v_t, bl, lp, nm, scale)
    y_b = matmul("y_b", o_mla, _bf(full["w_mla_o"]), "nn")

    def gate_fn(ya, yb, ga, gb, bias):
        return _sigmoid(ga + bias[:, :d]) * ya + _sigmoid(gb + bias[:, d:]) * yb

    (z,) = rowwise("gate_mix", gate_fn, [(y_a, d, 0), (y_b, d, 0), (proj_main, d, 4), (proj_main, d, 5)], [],
                   [b_gate], [(d, BF16)])
    mixed = matmul("mixed", z, _bf(full["w_out"]), "nn")

    def mid_fn(h, mx, g_post, g_pre):
        h1 = h + _rms(mx, g_post)
        return h1, _rms(h1, g_pre)

    h1, u2 = rowwise("norm_mid", mid_fn, [(h0, d, 0), (mixed, d, 0)], [], [mix_post_g, ffn_pre_g],
                     [(d, F32), (d, BF16)])
    gu = matmul("ffn_in", u2, _bf(full["w_ffn_in"]), "nn")
    (act,) = rowwise("swiglu", lambda gt, up: _silu(gt) * up, [(gu, ffn, 0), (gu, ffn, 1)], [], [], [(ffn, BF16)])
    f_out = matmul("ffn_out", act, _bf(full["w_ffn_out"]), "nn")

    def loss_fn(h1v, fv, tg, realv, g_post):
        h2 = h1v + _rms(fv, g_post)
        diff = (h2 - tg) * realv
        part = jnp.broadcast_to(0.5 * jnp.sum(diff * diff, keepdims=True) / d, (1, HEAD))
        dy = diff / d
        df, dg = _rms_bwd(fv, g_post, dy)
        return dy, df, part, dg

    dy, df, loss_part, g_ffn_post = rowwise("loss_head", loss_fn, [(h1, d, 0), (f_out, d, 0), (tgt, d, 0)], [real],
                                            [ffn_post_g], [(d, F32), (d, BF16)], [(1, HEAD), (1, d)])
    grads = {}
    d_act = matmul("d_act", df, _bf(full["w_ffn_out"]), "nt")
    grads["w_ffn_out"] = matmul("gw_ffn_out", act, df, "tn")

    def swiglu_bwd_fn(gt, up, da):
        return jnp.concatenate([da * up * _silu_grad(gt), da * _silu(gt)], axis=1)

    (dgu,) = rowwise("swiglu_bwd", swiglu_bwd_fn, [(gu, ffn, 0), (gu, ffn, 1), (d_act, ffn, 0)], [], [],
                     [(2 * ffn, BF16)])
    du2 = matmul("d_u2", dgu, _bf(full["w_ffn_in"]), "nt")
    grads["w_ffn_in"] = matmul("gw_ffn_in", u2, dgu, "tn")

    def mid_bwd_fn(dyv, h1v, du2v, mx, g_pre, g_post):
        dx, dg_pre = _rms_bwd(h1v, g_pre, du2v)
        dh1 = dyv + dx
        dmx, dg_post = _rms_bwd(mx, g_post, dh1)
        return dh1, dmx, dg_pre, dg_post

    dh1, dmixed, g_ffn_pre, g_mix_post = rowwise("norm_mid_bwd", mid_bwd_fn,
                                                 [(dy, d, 0), (h1, d, 0), (du2, d, 0), (mixed, d, 0)], [],
                                                 [ffn_pre_g, mix_post_g], [(d, F32), (d, BF16)], [(1, d), (1, d)])
    dz = matmul("d_z", dmixed, _bf(full["w_out"]), "nt")
    grads["w_out"] = matmul("gw_out", z, dmixed, "tn")

    def gate_bwd_fn(dzv, ya, yb, ga, gb, bias):
        sa, sb = _sigmoid(ga + bias[:, :d]), _sigmoid(gb + bias[:, d:])
        dga = dzv * ya * sa * (1.0 - sa)
        dgb = dzv * yb * sb * (1.0 - sb)
        dgates = jnp.concatenate([dga, dgb], axis=1)
        return dzv * sa, dzv * sb, dgates, jnp.sum(dgates, axis=0, keepdims=True)

    dy_a, dy_b, dgates, g_b_gate = rowwise("gate_mix_bwd", gate_bwd_fn,
                                           [(dz, d, 0), (y_a, d, 0), (y_b, d, 0), (proj_main, d, 4), (proj_main, d, 5)],
                                           [], [b_gate], [(d, BF16), (d, BF16), (2 * d, BF16)], [(1, 2 * d)])
    do_hg = matmul("d_o_hg", dy_a, _bf(full["w_hg_o"]), "nt")
    grads["w_hg_o"] = matmul("gw_hg_o", o_hg, dy_a, "tn")
    do_mla = matmul("d_o_mla", dy_b, _bf(full["w_mla_o"]), "nt", out_dtype=BF16)
    grads["w_mla_o"] = matmul("gw_mla_o", o_mla, dy_b, "tn")

    def hg_out_bwd_fn(do, o, hg, g):
        sg = _silu(hg)
        dn = do * sg
        dos, dgs, ons = [], 0.0, []
        for h in range(nh):
            sl = slice(h * HEAD, (h + 1) * HEAD)
            dx, dg = _rms_bwd(o[:, sl], g, dn[:, sl])
            dos.append(dx)
            dgs = dgs + dg
            ons.append(_rms(o[:, sl], g))
        dhg = do * jnp.concatenate(ons, axis=1) * _silu_grad(hg)
        return jnp.concatenate(dos, axis=1), dhg, dgs

    do_scan, dhg, g_hg_norm = rowwise("hgrn_out_bwd", hg_out_bwd_fn, [(do_hg, d, 0), (o_scan, d, 0), (proj_main, d, 3)],
                                      [], [hg_norm_g], [(d, F32), (d, BF16)], [(1, HEAD)])
    dhq, dhf, dhi, g_lb = hgrn_bwd(proj_main, lb, hg_consts, states, a_mats, do_scan, bl, lp, d)

    dq_cat, dk_cat, dv_att = attn_bwd_t(q_cat, k_cat, k_t, v_att, o_mla, do_mla, lse, bl, lp, nm, scale)

    def mla_prep_bwd_fn(dqc, dkc, dvv, cos, s_up, s_dn):
        dqs, dkvs, dkpe = [], [], 0.0
        for h in range(nm):
            dqs += [dqc[:, h * QK_PAD:h * QK_PAD + HEAD],
                    _rope_bwd(dqc[:, h * QK_PAD + HEAD:(h + 1) * QK_PAD], cos, s_up, s_dn)]
            dkvs += [dkc[:, h * QK_PAD:h * QK_PAD + HEAD], dvv[:, h * HEAD:(h + 1) * HEAD]]
            dkpe = dkpe + dkc[:, h * QK_PAD + HEAD:(h + 1) * QK_PAD]
        return jnp.concatenate(dqs, axis=1), jnp.concatenate(dkvs, axis=1), _rope_bwd(dkpe, cos, s_up, s_dn)

    dq_full, dkv_full, dkpe = rowwise("mla_prep_bwd", mla_prep_bwd_fn,
                                      [(dq_cat, nm * QK_PAD, 0), (dk_cat, nm * QK_PAD, 0), (dv_att, nm * HEAD, 0)],
                                      [t_cos, t_up, t_dn], [],
                                      [(nm * QK_PAD, BF16), (nm * QK_PAD, BF16), (HEAD, F32)])
    dqn = matmul("d_qn", dq_full, w_qb, "nt")
    g_wqb = matmul("gw_q_b", qn, dq_full, "tn")
    grads["w_q_b"] = g_wqb.reshape(ql, nm, QK_PAD)[:, :, :HEAD + ROPE].reshape(ql, nm * (HEAD + ROPE))
    dkvn = matmul("d_kvn", dkv_full, w_kvb, "nt")
    grads["w_kv_b"] = matmul("gw_kv_b", kvn, dkv_full, "tn")

    def mla_norms_bwd_fn(dqnv, dkvnv, cq, ckv, dkpev, gq, gk):
        dcq, dgq = _rms_bwd(cq, gq, dqnv)
        dckv, dgk = _rms_bwd(ckv, gk, dkvnv)
        return jnp.concatenate([dcq, dckv, dkpev], axis=1), dgq, dgk

    dmla, g_q_norm, g_kv_norm = rowwise("mla_norms_bwd", mla_norms_bwd_fn,
                                        [(dqn, ql, 0), (dkvn, kvl, 0), (proj_mla, ql, 0), (proj_mla, kvl, 1),
                                         (dkpe, HEAD, 0)], [], [q_a_norm_g, kv_a_norm_g],
                                        [(mla_w, BF16)], [(1, ql), (1, kvl)])

    w_main_bf = w_main
    pieces = [(dhq, w_main_bf[:, 0:d]), (dhf, w_main_bf[:, d:2 * d]), (dhi, w_main_bf[:, 2 * d:3 * d]),
              (dhg, w_main_bf[:, 3 * d:4 * d]), (dgates, w_main_bf[:, 4 * d:6 * d]), (dmla, w_mla)]
    du1 = None
    gw_parts = []
    for k, (dp, wp) in enumerate(pieces):
        du1 = matmul(f"d_u1_{k}", dp, wp, "nt", addend=du1)
        gw_parts.append(matmul(f"gw_in_{k}", u1, dp, "tn"))
    grads["w_in"] = jnp.concatenate(gw_parts[:4] + [gw_parts[5][:, :ql + kvl + ROPE], gw_parts[4]], axis=1)

    def first_bwd_fn(dh1v, h, du1v, g):
        dx, dg = _rms_bwd(h, g, du1v)
        return dh1v + dx, dg

    dh0, g_mix_pre = rowwise("norm_mix_pre_bwd", first_bwd_fn, [(dh1, d, 0), (h0, d, 0), (du1, d, 0)], [],
                             [mix_pre_g], [(d, F32)], [(1, d)])
    dh0 = dh0.reshape(bl, lp, d)
    grad_x = dh0[:, PAD_FRONT + N_META:]

    send = jnp.stack([pack_shard([split_full(n, grads[n], s) for n in BIG], pack_total) for s in range(4)])
    mine = lax.dynamic_index_in_dim(send, my_chip, axis=0, keepdims=False)
    p0 = lb_soft[0:1]
    g_lb_logits = jnp.concatenate([g_lb * p0 * (1.0 - p0), -g_lb * p0 * (1.0 - p0)], axis=0)

    def row_of(vec):
        return vec.reshape(-1, d) if vec.size >= d else jnp.pad(vec.reshape(1, -1), ((0, 0), (0, d - vec.size)))

    small_parts = dict(b_gate=g_b_gate, lb_logits=g_lb_logits, hg_norm_g=g_hg_norm, q_a_norm_g=g_q_norm,
                       kv_a_norm_g=g_kv_norm, mix_pre_g=g_mix_pre, mix_post_g=g_mix_post, ffn_pre_g=g_ffn_pre,
                       ffn_post_g=g_ffn_post)
    g_meta = jnp.sum(dh0[:, PAD_FRONT:PAD_FRONT + N_META], axis=0)
    small_rows = [row_of(small_parts[n]) for n in SMALL] + [row_of(g_meta)]
    n_small = sum(r.shape[0] for r in small_rows)
    small = jnp.pad(jnp.concatenate(small_rows, axis=0), ((0, -(-n_small // 8) * 8 - n_small), (0, 0)))
    recv, all_small = exchange_grads(_bf(send), small)

    rt = _tile(pack_total, 512, 16)
    recv2 = recv.reshape(3 * pack_total, PACK_W)
    (part_sum,) = rowwise("sum_chips", lambda a, r0, r1, r2: a + r0.astype(F32) + r1.astype(F32) + r2.astype(F32),
                          [(mine, PACK_W, 0)] + [(recv2, PACK_W, 0, k * (pack_total // rt)) for k in range(3)],
                          [], [], [(PACK_W, F32)], tm=rt)
    sib_sum = swap_with_sibling(part_sum)
    (g_packed,) = rowwise("sum_cores", lambda a, b: a + b, [(part_sum, PACK_W, 0), (sib_sum, PACK_W, 0)], [], [],
                          [(PACK_W, F32)], tm=rt)
    small_t = small.shape[0]

    def sum8_fn(*slabs):
        acc = slabs[0]
        for s in slabs[1:]:
            acc = acc + s
        return acc

    (g_small,) = rowwise("sum_small", sum8_fn, [(all_small.reshape(8 * small_t, d), d, 0, k) for k in range(8)],
                         [], [], [(d, F32)], tm=small_t, n_rows=small_t)

    g_final = {}
    off = 0
    for n, rows, (r, c) in zip(BIG, pack_rows, shard_shapes):
        g_final[n] = g_packed[off:off + rows].reshape(r, c)
        off += rows
    off = 0
    for n, part in zip(SMALL, small_rows[:-1]):
        rows = part.shape[0]
        g_final[n] = g_small[off:off + rows, :d].reshape(-1)[:wts[n].size].reshape(wts[n].shape)
        off += rows
    mcols = meta_tokens.shape[1]
    g_final["meta_tokens"] = lax.dynamic_slice_in_dim(g_small[off:off + N_META, :d], my_chip * mcols, mcols, axis=1)

    delta, new_m, new_v = {}, {}, {}
    for n in WEIGHTS:
        w2 = wts[n].reshape(-1, wts[n].shape[-1])
        dl, mn, vn = adamw("adamw_" + n, w2, g_final[n].reshape(w2.shape), mom_m[n].reshape(w2.shape),
                           mom_v[n].reshape(w2.shape))
        delta[n], new_m[n], new_v[n] = dl, mn, vn

    loss = lax.psum(loss_part[0, 0], ("x", "y", "c"))

    def shaped(n, a):
        return a.reshape((1,) + wts[n].shape) if n in BIG else a.reshape(wts[n].shape)

    return (loss, grad_x, *[shaped(n, g_final[n]) for n in WEIGHTS], *[shaped(n, delta[n]) for n in WEIGHTS],
            *[shaped(n, new_m[n]) for n in WEIGHTS], *[shaped(n, new_v[n]) for n in WEIGHTS])
```

```python
import functools
import math

import jax
import jax.numpy as jnp
from jax import lax
from jax.experimental import pallas as pl
from jax.experimental.pallas import tpu as pltpu

F32 = jnp.float32
BF16 = jnp.bfloat16
MESH = pl.DeviceIdType.MESH

N_META = 16
NORM_EPS = 1e-6
HEAD = 128
ROPE = 64
ROPE_HALF = ROPE // 2
QK_PAD = 2 * HEAD
CHUNK = 16
ROPE_THETA = 10000.0
SEQ_BLOCK = 256
PAD_FRONT = SEQ_BLOCK - N_META
PACK_W = 1024
NEG = -1e30
VMEM_LIMIT = 56 * 1024 * 1024
ATTN_HEADS_PER_STEP = 1
ATTN_TILE_MAX = 768

ADAM_LR, ADAM_B1, ADAM_B2, ADAM_EPS, ADAM_WD, ADAM_STEP = 0.001, 0.9, 0.999, 1e-08, 0.01, 10

BIG = ("w_in", "w_hg_o", "w_q_b", "w_kv_b", "w_mla_o", "w_out", "w_ffn_in", "w_ffn_out")
COL_SHARDED = ("w_in", "w_q_b", "w_kv_b", "w_ffn_in")
SMALL = ("b_gate", "lb_logits", "hg_norm_g", "q_a_norm_g", "kv_a_norm_g", "mix_pre_g", "mix_post_g",
         "ffn_pre_g", "ffn_post_g")
WEIGHTS = ("meta_tokens", "w_in", "b_gate", "lb_logits", "hg_norm_g", "w_hg_o", "q_a_norm_g", "w_q_b",
           "kv_a_norm_g", "w_kv_b", "w_mla_o", "w_out", "mix_pre_g", "mix_post_g", "ffn_pre_g", "ffn_post_g",
           "w_ffn_in", "w_ffn_out")


def _tile(n, cap, unit=128):
    if n <= cap:
        return n
    best = None
    for t in range(unit, cap + 1, unit):
        if n % t == 0:
            best = t
    assert best is not None, (n, cap, unit)
    return best


def _sigmoid(x):
    return 1.0 / (1.0 + jnp.exp(-x))


def _bf(x):
    return x.astype(BF16)


def rowwise(name, fn, row_ins, seq_tabs, consts, row_outs, acc_outs=(), tm=SEQ_BLOCK, n_rows=None):
    t_rows = row_ins[0][0].shape[0] if n_rows is None else n_rows
    nt = t_rows // tm
    assert t_rows % tm == 0
    n_in = len(row_ins) + len(seq_tabs) + len(consts)
    n_row = len(row_outs)

    def body(*refs):
        vals = [r[...].astype(F32) for r in refs[:n_in]]
        res = fn(*vals)
        if not isinstance(res, (tuple, list)):
            res = (res,)
        outs = refs[n_in:]
        for k in range(n_row):
            outs[k][...] = res[k].astype(outs[k].dtype)
        if acc_outs:
            @pl.when(pl.program_id(0) == 0)
            def _():
                for k in range(len(acc_outs)):
                    outs[n_row + k][...] = jnp.zeros_like(outs[n_row + k])

            for k in range(len(acc_outs)):
                outs[n_row + k][...] += res[n_row + k]

    row_ins = [tuple(e) + (0,) * (4 - len(e)) for e in row_ins]
    in_specs = [pl.BlockSpec((tm, w), functools.partial(lambda i, j, ro: (i + ro, j), j=j, ro=ro))
                for (_, w, j, ro) in row_ins]
    for tab in seq_tabs:
        per = tab.shape[0] // tm
        in_specs.append(pl.BlockSpec((tm, tab.shape[1]), functools.partial(lambda i, per: (i % per, 0), per=per)))
    for c in consts:
        in_specs.append(pl.BlockSpec(c.shape, lambda i: (0, 0)))
    out_specs = [pl.BlockSpec((tm, w), lambda i: (i, 0)) for (w, _) in row_outs]
    out_specs += [pl.BlockSpec(s, lambda i: (0, 0)) for s in acc_outs]
    out_shape = [jax.ShapeDtypeStruct((t_rows, w), dt) for (w, dt) in row_outs]
    out_shape += [jax.ShapeDtypeStruct(s, F32) for s in acc_outs]
    res = pl.pallas_call(
        body, name=name, grid=(nt,), in_specs=in_specs, out_specs=out_specs, out_shape=out_shape,
        compiler_params=pltpu.CompilerParams(dimension_semantics=("arbitrary",)),
    )(*[e[0] for e in row_ins], *seq_tabs, *consts)
    return res


def matmul(name, a, b, mode, out_dtype=F32, addend=None):
    if mode == "tn":
        kdim, m = a.shape
        n = b.shape[1]
        tm, tn, tk = _tile(m, 1408), _tile(n, 1024), _tile(kdim, 512)
        a_spec = pl.BlockSpec((tk, tm), lambda i, j, k: (k, i))
        b_spec = pl.BlockSpec((tk, tn), lambda i, j, k: (k, j))
        dims = (((0,), (0,)), ((), ()))
    else:
        m, kdim = a.shape
        n = b.shape[1] if mode == "nn" else b.shape[0]
        tn, tk = _tile(n, 1536), _tile(kdim, 1536)
        tm = _tile(m, 1024 if tn <= 1024 else 512)
        a_spec = pl.BlockSpec((tm, tk), lambda i, j, k: (i, k))
        if mode == "nn":
            b_spec = pl.BlockSpec((tk, tn), lambda i, j, k: (k, j))
            dims = (((1,), (0,)), ((), ()))
        else:
            b_spec = pl.BlockSpec((tn, tk), lambda i, j, k: (j, k))
            dims = (((1,), (1,)), ((), ()))
    nk = kdim // tk
    has_add = addend is not None

    def body(*refs):
        a_ref, b_ref = refs[0], refs[1]
        add_ref = refs[2] if has_add else None
        o_ref, acc_ref = refs[-2], refs[-1]
        k = pl.program_id(2)

        @pl.when(k == 0)
        def _():
            acc_ref[...] = jnp.zeros_like(acc_ref)

        acc_ref[...] += lax.dot_general(a_ref[...], b_ref[...], dims, preferred_element_type=F32)

        @pl.when(k == nk - 1)
        def _():
            r = acc_ref[...]
            if has_add:
                r = r + add_ref[...]
            o_ref[...] = r.astype(o_ref.dtype)

    in_specs = [a_spec, b_spec]
    args = [a, b]
    if has_add:
        in_specs.append(pl.BlockSpec((tm, tn), lambda i, j, k: (i, j)))
        args.append(addend)
    return pl.pallas_call(
        body, name=name, grid=(m // tm, n // tn, nk), in_specs=in_specs,
        out_specs=pl.BlockSpec((tm, tn), lambda i, j, k: (i, j)),
        out_shape=jax.ShapeDtypeStruct((m, n), out_dtype),
        scratch_shapes=[pltpu.VMEM((tm, tn), F32)],
        compiler_params=pltpu.CompilerParams(dimension_semantics=("arbitrary", "arbitrary", "arbitrary"),
                                             vmem_limit_bytes=VMEM_LIMIT),
    )(*args)


def _rms(x, g):
    r = lax.rsqrt(jnp.mean(x * x, axis=-1, keepdims=True) + NORM_EPS)
    return x * r * g


def _rms_bwd(x, g, dy):
    r = lax.rsqrt(jnp.mean(x * x, axis=-1, keepdims=True) + NORM_EPS)
    xh = x * r
    dyg = dy * g
    dx = r * (dyg - xh * jnp.mean(dyg * xh, axis=-1, keepdims=True))
    return dx, jnp.sum(dy * xh, axis=0, keepdims=True)


def _silu(x):
    return x * _sigmoid(x)


def _silu_grad(x):
    s = _sigmoid(x)
    return s * (1.0 + x * (1.0 - s))


def _rope(xs, cos, s_up, s_dn):
    return xs * cos + pltpu.roll(xs, ROPE_HALF, 1) * s_up + pltpu.roll(xs, HEAD - ROPE_HALF, 1) * s_dn


def _rope_bwd(dy, cos, s_up, s_dn):
    return dy * cos + pltpu.roll(dy * s_up, HEAD - ROPE_HALF, 1) + pltpu.roll(dy * s_dn, ROPE_HALF, 1)


HG_SUB = 128
HG_LEVELS = 7
HG_E_ROWS = (HG_LEVELS + 1) * HG_SUB
TN_DIMS = (((0,), (0,)), ((), ()))
NT_DIMS = (((1,), (1,)), ((), ()))


def _hg_constants():
    import numpy as np
    n = HG_SUB
    r = np.arange(n)[:, None]
    c = np.arange(n)[None, :]
    cs, ps = [], []
    for lvl in range(HG_LEVELS):
        m = (n // 2) >> lvl
        upper = (r % (2 * m)) >= m
        mid = (r // (2 * m)) * (2 * m) + m - 1
        cs.append(np.where(upper, (c > mid) & (c <= r), (c > r) & (c <= mid)))
        ps.append(((r // (2 * m)) == (c // (2 * m))) & upper & ((c % (2 * m)) < m))
    cs.append(c <= r)
    cs.append(np.ones((8, n), bool))
    cstack = np.concatenate(cs, 0).astype(np.float32)
    pstack = np.concatenate(ps, 0).astype(np.float32)
    pstack_t = np.concatenate([p.T for p in ps], 0).astype(np.float32)
    return (jnp.asarray(cstack, BF16), jnp.asarray(cstack[:HG_E_ROWS].T, BF16), jnp.asarray(pstack, F32),
            jnp.asarray(pstack_t, F32))


def _split_dot(c_bf, x):
    hi = _bf(x)
    lo = _bf(x - hi.astype(F32))
    r2 = jnp.dot(c_bf, jnp.concatenate([hi, lo], axis=1), preferred_element_type=F32)
    return r2[:, :HEAD] + r2[:, HEAD:]


def _hg_gates(hq, hf, lb):
    sq = _sigmoid(hq)
    sg = _sigmoid(hf)
    fg = lb + (1.0 - lb) * sg
    return sq, hq * sq, sg, fg, 1.0 - fg, jnp.log(fg)


def _hg_block_fwd(st, hq, hf, hi, lb, cstack, p_ref):
    _, q, _, _, k, g = _hg_gates(hq, hf, lb)
    v = hi
    e = _split_dot(cstack, g)
    bc = e[HG_LEVELS * HG_SUB:HG_E_ROWS]
    b_last = jnp.tile(e[HG_E_ROWS:], (HG_SUB // 8, 1))
    a = jnp.zeros((HG_SUB, HG_SUB), F32)
    for lvl in range(HG_LEVELS):
        x = jnp.exp(e[lvl * HG_SUB:(lvl + 1) * HG_SUB])
        a = a + p_ref[pl.ds(lvl * HG_SUB, HG_SUB), :] * lax.dot_general(_bf(q * x), _bf(k * x), NT_DIMS,
                                                                          preferred_element_type=F32)
    a_bf = _bf(a)
    diag = jnp.sum(q * k, axis=1, keepdims=True)
    o = (jnp.dot(a_bf, _bf(v), preferred_element_type=F32) + diag * v
         + lax.dot_general(_bf(q * jnp.exp(bc)), _bf(st), NT_DIMS, preferred_element_type=F32))
    kd = k * jnp.exp(b_last - bc)
    st_out = st * jnp.exp(b_last) + lax.dot_general(_bf(v), _bf(kd), TN_DIMS, preferred_element_type=F32)
    return st_out, o, a_bf


def _hg_block_bwd(st, dst_out, do, hq, hf, hi, lb, a_bf, cstack, cstack_t, p_ref, pt_ref):
    sq, q, sg, fg, k, g = _hg_gates(hq, hf, lb)
    v = hi
    e = _split_dot(cstack, g)
    bc = e[HG_LEVELS * HG_SUB:HG_E_ROWS]
    b_last = jnp.tile(e[HG_E_ROWS:], (HG_SUB // 8, 1))
    eb = jnp.exp(bc)
    qb = q * eb
    er = jnp.exp(b_last - bc)
    kd = k * er
    e_last = jnp.exp(b_last)
    do_bf, v_bf, dst_bf = _bf(do), _bf(v), _bf(dst_out)
    da = lax.dot_general(do_bf, v_bf, NT_DIMS, preferred_element_type=F32)
    dat = lax.dot_general(v_bf, do_bf, NT_DIMS, preferred_element_type=F32)
    d_diag = jnp.sum(do * v, axis=1, keepdims=True)
    dv = (lax.dot_general(a_bf, do_bf, TN_DIMS, preferred_element_type=F32)
          + jnp.sum(q * k, axis=1, keepdims=True) * do
          + lax.dot_general(_bf(kd), dst_bf, NT_DIMS, preferred_element_type=F32))
    dqb = jnp.dot(do_bf, _bf(st), preferred_element_type=F32)
    dst = dst_out * e_last + lax.dot_general(do_bf, _bf(qb), TN_DIMS, preferred_element_type=F32)
    dkd = jnp.dot(v_bf, dst_bf, preferred_element_type=F32)
    dq = dqb * eb + d_diag * k
    dk = dkd * er + d_diag * q
    d_last = (jnp.sum(dst_out * st * e_last, axis=0, keepdims=True)
              + jnp.sum(dkd * kd, axis=0, keepdims=True))
    des = []
    for lvl in range(HG_LEVELS):
        x = jnp.exp(e[lvl * HG_SUB:(lvl + 1) * HG_SUB])
        qh, kh = q * x, k * x
        dm = _bf(p_ref[pl.ds(lvl * HG_SUB, HG_SUB), :] * da)
        dmt = _bf(pt_ref[pl.ds(lvl * HG_SUB, HG_SUB), :] * dat)
        dqh = jnp.dot(dm, _bf(kh), preferred_element_type=F32)
        dkh = jnp.dot(dmt, _bf(qh), preferred_element_type=F32)
        dq = dq + dqh * x
        dk = dk + dkh * x
        des.append(dqh * qh + dkh * kh)
    des.append(dqb * qb - dkd * kd)
    dg = _split_dot(cstack_t, jnp.concatenate(des, axis=0)) + d_last
    dfg = dg / fg - dk
    dhq = dq * (sq * (1.0 + hq * (1.0 - sq)))
    dhf = dfg * (1.0 - lb) * sg * (1.0 - sg)
    return dst, dhq, dhf, dv, jnp.sum(dfg * (1.0 - sg), axis=0, keepdims=True)


def hgrn_fwd(proj_main, lb, consts, bl, lp, d):
    nh = d // HEAD
    rows_blk = _tile(lp, 768, SEQ_BLOCK)
    nb = lp // rows_blk
    spb = rows_blk // HG_SUB
    cstack, _, pstack, _ = consts

    def body(hq_ref, hf_ref, hi_ref, lb_ref, c_ref, p_ref, o_ref, st_ref, a_ref, s_ref):
        j = pl.program_id(2)

        @pl.when(j == 0)
        def _():
            s_ref[...] = jnp.zeros_like(s_ref)

        lbv = lb_ref[...]
        cs = c_ref[...]

        def sub(n, carry):
            r = pl.multiple_of(n * HG_SUB, HG_SUB)
            st = s_ref[...]
            st_ref[0, 0, pl.ds(n, 1)] = st[None]
            st_out, o, a_bf = _hg_block_fwd(st, hq_ref[pl.ds(r, HG_SUB), :].astype(F32),
                                            hf_ref[pl.ds(r, HG_SUB), :].astype(F32),
                                            hi_ref[pl.ds(r, HG_SUB), :].astype(F32), lbv, cs, p_ref)
            s_ref[...] = st_out
            o_ref[pl.ds(r, HG_SUB), :] = o
            a_ref[0, 0, pl.ds(n, 1)] = a_bf[None]
            return carry

        lax.fori_loop(0, spb, sub, 0, unroll=2)

    def colspec(off):
        return pl.BlockSpec((rows_blk, HEAD), functools.partial(lambda h, b, j, off: (b * nb + j, off + h), off=off))

    whole = lambda arr: pl.BlockSpec(arr.shape, lambda h, b, j: (0, 0))
    return pl.pallas_call(
        body, name="hgrn_fwd", grid=(nh, bl, nb),
        in_specs=[colspec(0), colspec(nh), colspec(2 * nh), pl.BlockSpec((1, HEAD), lambda h, b, j: (0, h)),
                  whole(cstack), whole(pstack)],
        out_specs=[pl.BlockSpec((rows_blk, HEAD), lambda h, b, j: (b * nb + j, h)),
                   pl.BlockSpec((1, 1, spb, HEAD, HEAD), lambda h, b, j: (b, h, j, 0, 0)),
                   pl.BlockSpec((1, 1, spb, HG_SUB, HG_SUB), lambda h, b, j: (b, h, j, 0, 0))],
        out_shape=[jax.ShapeDtypeStruct((bl * lp, d), F32),
                   jax.ShapeDtypeStruct((bl, nh, lp // HG_SUB, HEAD, HEAD), F32),
                   jax.ShapeDtypeStruct((bl, nh, lp // HG_SUB, HG_SUB, HG_SUB), BF16)],
        scratch_shapes=[pltpu.VMEM((HEAD, HEAD), F32)],
        compiler_params=pltpu.CompilerParams(dimension_semantics=("arbitrary", "arbitrary", "arbitrary")),
    )(proj_main, proj_main, proj_main, lb, cstack, pstack)


def hgrn_bwd(proj_main, lb, consts, states, a_mats, do_scan, bl, lp, d):
    nh = d // HEAD
    rows_blk = _tile(lp, 768, SEQ_BLOCK)
    nb = lp // rows_blk
    spb = rows_blk // HG_SUB
    cstack, cstack_t, pstack, pstack_t = consts

    def body(hq_ref, hf_ref, hi_ref, lb_ref, c_ref, ct_ref, p_ref, pt_ref, st_ref, a_ref, do_ref,
             dq_ref, df_ref, di_ref, dlb_ref, ds_ref):
        b_id, j = pl.program_id(1), pl.program_id(2)
        blk = nb - 1 - j

        @pl.when(j == 0)
        def _():
            ds_ref[...] = jnp.zeros_like(ds_ref)

        @pl.when((j == 0) & (b_id == 0))
        def _():
            dlb_ref[...] = jnp.zeros_like(dlb_ref)

        lbv = lb_ref[...]
        cs = c_ref[...]
        cst = ct_ref[...]

        def sub(i, carry):
            n = spb - 1 - i
            r = pl.multiple_of(n * HG_SUB, HG_SUB)
            dst, dhq, dhf, dhi, dlb = _hg_block_bwd(
                st_ref[0, 0, pl.ds(n, 1)][0], ds_ref[...], do_ref[pl.ds(r, HG_SUB), :],
                hq_ref[pl.ds(r, HG_SUB), :].astype(F32), hf_ref[pl.ds(r, HG_SUB), :].astype(F32),
                hi_ref[pl.ds(r, HG_SUB), :].astype(F32), lbv,
                a_ref[0, 0, pl.ds(n, 1)][0], cs, cst, p_ref, pt_ref)
            ds_ref[...] = dst
            dq_ref[pl.ds(r, HG_SUB), :] = dhq.astype(dq_ref.dtype)
            df_ref[pl.ds(r, HG_SUB), :] = dhf.astype(df_ref.dtype)
            di_ref[pl.ds(r, HG_SUB), :] = dhi.astype(di_ref.dtype)
            dlb_ref[...] += dlb
            return carry

        lax.fori_loop(0, spb, sub, 0, unroll=2)

    def colspec(off):
        return pl.BlockSpec((rows_blk, HEAD),
                            functools.partial(lambda h, b, j, off: (b * nb + nb - 1 - j, off + h), off=off))

    whole = lambda arr: pl.BlockSpec(arr.shape, lambda h, b, j: (0, 0))
    mats = lambda: pl.BlockSpec((1, 1, spb, HEAD, HEAD), lambda h, b, j: (b, h, nb - 1 - j, 0, 0))
    t_rows = bl * lp
    return pl.pallas_call(
        body, name="hgrn_bwd", grid=(nh, bl, nb),
        in_specs=[colspec(0), colspec(nh), colspec(2 * nh), pl.BlockSpec((1, HEAD), lambda h, b, j: (0, h)),
                  whole(cstack), whole(cstack_t), whole(pstack), whole(pstack_t), mats(), mats(), colspec(0)],
        out_specs=[colspec(0), colspec(0), colspec(0), pl.BlockSpec((1, HEAD), lambda h, b, j: (0, h))],
        out_shape=[jax.ShapeDtypeStruct((t_rows, d), BF16)] * 3 + [jax.ShapeDtypeStruct((1, d), F32)],
        scratch_shapes=[pltpu.VMEM((HEAD, HEAD), F32)],
        compiler_params=pltpu.CompilerParams(dimension_semantics=("arbitrary", "arbitrary", "arbitrary")),
    )(proj_main, proj_main, proj_main, lb, cstack, cstack_t, pstack, pstack_t, states, a_mats, do_scan)


def _allowed(row0, col0, nr, nc, transposed=False):
    if transposed:
        col = col0 + lax.broadcasted_iota(jnp.int32, (nc, 1), 0)
        row = row0 + lax.broadcasted_iota(jnp.int32, (1, nr), 1)
    else:
        row = row0 + lax.broadcasted_iota(jnp.int32, (nr, 1), 0)
        col = col0 + lax.broadcasted_iota(jnp.int32, (1, nc), 1)
    return (col <= row) & ((col >= PAD_FRONT) | (row < PAD_FRONT))


def attn_fwd(q_cat, k_cat, v, bl, lp, nm, scale):
    tq = tk = SEQ_BLOCK
    nq = lp // tq

    def body(q_ref, k_ref, v_ref, o_ref, lse_ref, m_ref, l_ref, acc_ref):
        i = pl.program_id(2)
        q = q_ref[...]
        m_ref[...] = jnp.full_like(m_ref, NEG)
        l_ref[...] = jnp.zeros_like(l_ref)
        acc_ref[...] = jnp.zeros_like(acc_ref)

        def kstep(c, carry):
            c0 = pl.multiple_of(c * tk, tk)
            s = lax.dot_general(q, k_ref[pl.ds(c0, tk), :], NT_DIMS, preferred_element_type=F32) * scale
            s = jnp.where(_allowed(i * tq, c * tk, tq, tk), s, NEG)
            m_old = m_ref[...]
            m_new = jnp.maximum(m_old, jnp.max(s, axis=1, keepdims=True))
            alpha = jnp.exp(m_old - m_new)
            p = jnp.exp(s - m_new)
            l_ref[...] = alpha * l_ref[...] + jnp.sum(p, axis=1, keepdims=True)
            acc_ref[...] = alpha * acc_ref[...] + jnp.dot(_bf(p), v_ref[pl.ds(c0, tk), :],
                                                          preferred_element_type=F32)
            m_ref[...] = m_new
            return carry

        lax.fori_loop(0, i + 1, kstep, 0)
        o_ref[...] = (acc_ref[...] / l_ref[...]).astype(o_ref.dtype)
        lse_ref[0, 0] = m_ref[...] + jnp.log(l_ref[...])

    return pl.pallas_call(
        body, name="attn_fwd", grid=(bl, nm, nq),
        in_specs=[pl.BlockSpec((tq, QK_PAD), lambda b, h, i: (b * nq + i, h)),
                  pl.BlockSpec((lp, QK_PAD), lambda b, h, i: (b, h)),
                  pl.BlockSpec((lp, HEAD), lambda b, h, i: (b, h))],
        out_specs=[pl.BlockSpec((tq, HEAD), lambda b, h, i: (b * nq + i, h)),
                   pl.BlockSpec((1, 1, tq, 1), lambda b, h, i: (b, h, i, 0))],
        out_shape=[jax.ShapeDtypeStruct((bl * lp, nm * HEAD), BF16),
                   jax.ShapeDtypeStruct((bl, nm, lp, 1), F32)],
        scratch_shapes=[pltpu.VMEM((tq, 1), F32), pltpu.VMEM((tq, 1), F32), pltpu.VMEM((tq, HEAD), F32)],
        compiler_params=pltpu.CompilerParams(dimension_semantics=("arbitrary", "arbitrary", "arbitrary")),
    )(q_cat, k_cat, v)


def attn_bwd_dq(q_cat, k_cat, v, o, do, lse, bl, lp, nm, scale):
    tq = tk = SEQ_BLOCK
    nq = lp // tq

    def body(q_ref, k_ref, v_ref, o_ref, do_ref, lse_ref, dq_ref, dl_ref, acc_ref):
        i = pl.program_id(2)
        q = q_ref[...]
        do_b = do_ref[...]
        delta = jnp.sum(o_ref[...].astype(F32) * do_b.astype(F32), axis=1, keepdims=True)
        lse_b = lse_ref[0, 0]
        acc_ref[...] = jnp.zeros_like(acc_ref)

        def kstep(c, carry):
            c0 = pl.multiple_of(c * tk, tk)
            ks = k_ref[pl.ds(c0, tk), :]
            s = lax.dot_general(q, ks, NT_DIMS, preferred_element_type=F32) * scale
            p = jnp.where(_allowed(i * tq, c * tk, tq, tk), jnp.exp(s - lse_b), 0.0)
            dp = lax.dot_general(do_b, v_ref[pl.ds(c0, tk), :], NT_DIMS, preferred_element_type=F32)
            ds = p * (dp - delta)
            acc_ref[...] += jnp.dot(_bf(ds), ks, preferred_element_type=F32)
            return carry

        lax.fori_loop(0, i + 1, kstep, 0)
        dq_ref[...] = acc_ref[...] * scale
        dl_ref[0, 0] = delta

    return pl.pallas_call(
        body, name="attn_bwd_dq", grid=(bl, nm, nq),
        in_specs=[pl.BlockSpec((tq, QK_PAD), lambda b, h, i: (b * nq + i, h)),
                  pl.BlockSpec((lp, QK_PAD), lambda b, h, i: (b, h)),
                  pl.BlockSpec((lp, HEAD), lambda b, h, i: (b, h)),
                  pl.BlockSpec((tq, HEAD), lambda b, h, i: (b * nq + i, h)),
                  pl.BlockSpec((tq, HEAD), lambda b, h, i: (b * nq + i, h)),
                  pl.BlockSpec((1, 1, tq, 1), lambda b, h, i: (b, h, i, 0))],
        out_specs=[pl.BlockSpec((tq, QK_PAD), lambda b, h, i: (b * nq + i, h)),
                   pl.BlockSpec((1, 1, tq, 1), lambda b, h, i: (b, h, i, 0))],
        out_shape=[jax.ShapeDtypeStruct((bl * lp, nm * QK_PAD), F32),
                   jax.ShapeDtypeStruct((bl, nm, lp, 1), F32)],
        scratch_shapes=[pltpu.VMEM((tq, QK_PAD), F32)],
        compiler_params=pltpu.CompilerParams(dimension_semantics=("arbitrary", "arbitrary", "arbitrary")),
    )(q_cat, k_cat, v, o, do, lse)


def attn_bwd_dkv(q_cat, k_cat, v, do, lse_row, delta_row, bl, lp, nm, scale):
    tq = tk = SEQ_BLOCK
    nq = lp // tq

    def body(q_ref, k_ref, v_ref, do_ref, lse_ref, dl_ref, dk_ref, dv_ref):
        i = pl.program_id(2)
        kt = k_ref[...]
        vt = v_ref[...]
        dk_ref[...] = jnp.zeros_like(dk_ref)
        dv_ref[...] = jnp.zeros_like(dv_ref)

        def qstep(c, carry):
            c0 = pl.multiple_of(c * tq, tq)
            qs = q_ref[pl.ds(c0, tq), :]
            dos = do_ref[pl.ds(c0, tq), :]
            st = lax.dot_general(kt, qs, NT_DIMS, preferred_element_type=F32) * scale
            pt = jnp.where(_allowed(c * tq, i * tk, tq, tk, transposed=True),
                           jnp.exp(st - lse_ref[0, 0, pl.ds(c, 1)][0]), 0.0)
            dv_ref[...] += jnp.dot(_bf(pt), dos, preferred_element_type=F32)
            dpt = lax.dot_general(vt, dos, NT_DIMS, preferred_element_type=F32)
            dst = pt * (dpt - dl_ref[0, 0, pl.ds(c, 1)][0])
            dk_ref[...] += jnp.dot(_bf(dst), qs, preferred_element_type=F32)
            return carry

        lax.fori_loop(i, nq, qstep, 0)
        dk_ref[...] = dk_ref[...] * scale

    return pl.pallas_call(
        body, name="attn_bwd_dkv", grid=(bl, nm, nq),
        in_specs=[pl.BlockSpec((lp, QK_PAD), lambda b, h, i: (b, h)),
                  pl.BlockSpec((tk, QK_PAD), lambda b, h, i: (b * nq + i, h)),
                  pl.BlockSpec((tk, HEAD), lambda b, h, i: (b * nq + i, h)),
                  pl.BlockSpec((lp, HEAD), lambda b, h, i: (b, h)),
                  pl.BlockSpec((1, 1, nq, 1, tq), lambda b, h, i: (b, h, 0, 0, 0)),
                  pl.BlockSpec((1, 1, nq, 1, tq), lambda b, h, i: (b, h, 0, 0, 0))],
        out_specs=[pl.BlockSpec((tk, QK_PAD), lambda b, h, i: (b * nq + i, h)),
                   pl.BlockSpec((tk, HEAD), lambda b, h, i: (b * nq + i, h))],
        out_shape=[jax.ShapeDtypeStruct((bl * lp, nm * QK_PAD), F32),
                   jax.ShapeDtypeStruct((bl * lp, nm * HEAD), F32)],
        compiler_params=pltpu.CompilerParams(dimension_semantics=("arbitrary", "arbitrary", "arbitrary")),
    )(q_cat, k_cat, v, do, lse_row, delta_row)


def _key_query_mask(key0, qry0, nk, nq_, causal):
    key = key0 + lax.broadcasted_iota(jnp.int32, (nk, 1), 0)
    if not causal:
        return key >= PAD_FRONT
    qry = qry0 + lax.broadcasted_iota(jnp.int32, (1, nq_), 1)
    return (key <= qry) & (key >= PAD_FRONT)


def _attn_tile(lp):
    return _tile(lp, ATTN_TILE_MAX, SEQ_BLOCK)


def attn_fwd_t(q_cat, k_cat, v_t, bl, lp, nm, scale):
    tq = tk = _attn_tile(lp)
    nq = lp // tq
    hp = ATTN_HEADS_PER_STEP
    assert nm % hp == 0

    def body(q_ref, k_ref, vt_ref, o_ref, lse_ref, m_ref, l_ref, acc_ref):
        i = pl.program_id(2)
        m_ref[...] = jnp.full_like(m_ref, NEG)
        l_ref[...] = jnp.zeros_like(l_ref)
        acc_ref[...] = jnp.zeros_like(acc_ref)

        def step(c, mask):
            c0 = pl.multiple_of(c * tk, tk)
            for hh in range(hp):
                cols = pl.ds(hh * QK_PAD, QK_PAD)
                st = lax.dot_general(k_ref[pl.ds(c0, tk), cols], q_ref[:, cols], NT_DIMS,
                                     preferred_element_type=F32) * scale
                if mask is not None:
                    st = jnp.where(_key_query_mask(c * tk, i * tq, tk, tq, mask == "causal"), st, NEG)
                m_old = m_ref[hh]
                m_new = jnp.maximum(m_old, jnp.max(st, axis=0, keepdims=True))
                alpha = jnp.exp(m_old - m_new)
                pt = jnp.exp(st - m_new)
                l_ref[hh] = alpha * l_ref[hh] + jnp.sum(pt, axis=0, keepdims=True)
                acc_ref[hh] = alpha * acc_ref[hh] + jnp.dot(vt_ref[0, hh, pl.ds(c, 1)][0], _bf(pt),
                                                            preferred_element_type=F32)
                m_ref[hh] = m_new

        def mid(c, carry):
            step(c, None)
            return carry

        @pl.when(i == 0)
        def _():
            step(0, "causal")

        @pl.when(i > 0)
        def _():
            step(0, "pad")
            lax.fori_loop(1, i, mid, 0)
            step(i, "causal")

        for hh in range(hp):
            o_ref[:, pl.ds(hh * HEAD, HEAD)] = jnp.transpose(acc_ref[hh] / l_ref[hh]).astype(o_ref.dtype)
            lse_ref[0, hh, 0] = m_ref[hh] + jnp.log(l_ref[hh])

    return pl.pallas_call(
        body, name="attn_fwd", grid=(bl, nm // hp, nq),
        in_specs=[pl.BlockSpec((tq, hp * QK_PAD), lambda b, h, i: (b * nq + i, h)),
                  pl.BlockSpec((lp, hp * QK_PAD), lambda b, h, i: (b, h)),
                  pl.BlockSpec((1, hp, nq, HEAD, tk), lambda b, h, i: (b, h, 0, 0, 0))],
        out_specs=[pl.BlockSpec((tq, hp * HEAD), lambda b, h, i: (b * nq + i, h)),
                   pl.BlockSpec((1, hp, 1, 1, tq), lambda b, h, i: (b, h, i, 0, 0))],
        out_shape=[jax.ShapeDtypeStruct((bl * lp, nm * HEAD), BF16),
                   jax.ShapeDtypeStruct((bl, nm, nq, 1, tq), F32)],
        scratch_shapes=[pltpu.VMEM((hp, 1, tq), F32), pltpu.VMEM((hp, 1, tq), F32), pltpu.VMEM((hp, HEAD, tq), F32)],
        compiler_params=pltpu.CompilerParams(dimension_semantics=("arbitrary", "arbitrary", "arbitrary")),
    )(q_cat, k_cat, v_t)


def attn_bwd_t(q_cat, k_cat, k_t, v, o, do, lse, bl, lp, nm, scale):
    tq = tk = _attn_tile(lp)
    nq = lp // tq
    hp = ATTN_HEADS_PER_STEP
    assert nm % hp == 0

    def body(q_ref, k_ref, kt_ref, v_ref, o_ref, do_ref, lse_ref, dq_ref, dk_ref, dv_ref, dqt_ref, dka_ref, dva_ref):
        i = pl.program_id(2)

        @pl.when(i == 0)
        def _():
            dqt_ref[...] = jnp.zeros_like(dqt_ref)

        dka_ref[...] = jnp.zeros_like(dka_ref)
        dva_ref[...] = jnp.zeros_like(dva_ref)
        ones8 = jnp.ones((8, HEAD), BF16)

        def step(c, mask):
            c0 = pl.multiple_of(c * tq, tq)
            for hh in range(hp):
                qcols, vcols = pl.ds(hh * QK_PAD, QK_PAD), pl.ds(hh * HEAD, HEAD)
                qs = q_ref[pl.ds(c0, tq), qcols]
                dos = do_ref[pl.ds(c0, tq), vcols]
                prod = dos.astype(F32) * o_ref[pl.ds(c0, tq), vcols].astype(F32)
                hi = _bf(prod)
                lo = _bf(prod - hi.astype(F32))
                delta8 = (lax.dot_general(ones8, hi, NT_DIMS, preferred_element_type=F32)
                          + lax.dot_general(ones8, lo, NT_DIMS, preferred_element_type=F32))
                st = lax.dot_general(k_ref[:, qcols], qs, NT_DIMS, preferred_element_type=F32) * scale
                pt = jnp.exp(st - lse_ref[0, hh, pl.ds(c, 1)][0])
                if mask is not None:
                    pt = jnp.where(_key_query_mask(i * tk, c * tq, tk, tq, mask == "causal"), pt, 0.0)
                dva_ref[hh] += jnp.dot(_bf(pt), dos, preferred_element_type=F32)
                dpt = lax.dot_general(v_ref[:, vcols], dos, NT_DIMS, preferred_element_type=F32)
                dst = _bf(pt * (dpt - jnp.tile(delta8, (tk // 8, 1))))
                dka_ref[hh] += jnp.dot(dst, qs, preferred_element_type=F32)
                dqt_ref[hh, pl.ds(c, 1)] += jnp.dot(kt_ref[0, hh, 0], dst, preferred_element_type=F32)[None]

        step(i, "causal")

        def rest_masked(c, carry):
            step(c, "pad")
            return carry

        def rest(c, carry):
            step(c, None)
            return carry

        @pl.when(i == 0)
        def _():
            lax.fori_loop(1, nq, rest_masked, 0)

        @pl.when(i > 0)
        def _():
            lax.fori_loop(i + 1, nq, rest, 0)

        for hh in range(hp):
            dk_ref[:, pl.ds(hh * QK_PAD, QK_PAD)] = (dka_ref[hh] * scale).astype(dk_ref.dtype)
            dv_ref[:, pl.ds(hh * HEAD, HEAD)] = dva_ref[hh].astype(dv_ref.dtype)

        @pl.when(i == nq - 1)
        def _():
            for hh in range(hp):
                for c in range(nq):
                    dq_ref[pl.ds(c * tq, tq), pl.ds(hh * QK_PAD, QK_PAD)] = (
                        jnp.transpose(dqt_ref[hh, c]) * scale).astype(dq_ref.dtype)

    return pl.pallas_call(
        body, name="attn_bwd", grid=(bl, nm // hp, nq),
        in_specs=[pl.BlockSpec((lp, hp * QK_PAD), lambda b, h, i: (b, h)),
                  pl.BlockSpec((tk, hp * QK_PAD), lambda b, h, i: (b * nq + i, h)),
                  pl.BlockSpec((1, hp, 1, QK_PAD, tk), lambda b, h, i: (b, h, i, 0, 0)),
                  pl.BlockSpec((tk, hp * HEAD), lambda b, h, i: (b * nq + i, h)),
                  pl.BlockSpec((lp, hp * HEAD), lambda b, h, i: (b, h)),
                  pl.BlockSpec((lp, hp * HEAD), lambda b, h, i: (b, h)),
                  pl.BlockSpec((1, hp, nq, 1, tq), lambda b, h, i: (b, h, 0, 0, 0))],
        out_specs=[pl.BlockSpec((lp, hp * QK_PAD), lambda b, h, i: (b, h)),
                   pl.BlockSpec((tk, hp * QK_PAD), lambda b, h, i: (b * nq + i, h)),
                   pl.BlockSpec((tk, hp * HEAD), lambda b, h, i: (b * nq + i, h))],
        out_shape=[jax.ShapeDtypeStruct((bl * lp, nm * QK_PAD), BF16),
                   jax.ShapeDtypeStruct((bl * lp, nm * QK_PAD), BF16),
                   jax.ShapeDtypeStruct((bl * lp, nm * HEAD), BF16)],
        scratch_shapes=[pltpu.VMEM((hp, nq, QK_PAD, tq), F32), pltpu.VMEM((hp, tk, QK_PAD), F32),
                        pltpu.VMEM((hp, tk, HEAD), F32)],
        compiler_params=pltpu.CompilerParams(dimension_semantics=("arbitrary", "arbitrary", "arbitrary")),
    )(q_cat, k_cat, k_t, v, o, do, lse)


def _place():
    return lax.axis_index("x"), lax.axis_index("y"), lax.axis_index("c")


def gather_shards(packed):
    hbm = pl.BlockSpec(memory_space=pl.ANY)

    def body(src_ref, out_ref, send_sems, recv_sems, local_sem):
        x, y, c = _place()
        me = 2 * x + y
        chips = [(1 - x, y), (x, 1 - y), (1 - x, 1 - y)]
        local = pltpu.make_async_copy(src_ref, out_ref.at[me], local_sem)
        local.start()
        sends = []
        for k, (px, py) in enumerate(chips):
            cp = pltpu.make_async_remote_copy(src_ref=src_ref, dst_ref=out_ref.at[me], send_sem=send_sems.at[k],
                                              recv_sem=recv_sems.at[k], device_id=(px, py, c), device_id_type=MESH)
            cp.start()
            sends.append(cp)
        for k, (px, py) in enumerate(chips):
            pltpu.make_async_remote_copy(src_ref=src_ref, dst_ref=out_ref.at[2 * px + py], send_sem=send_sems.at[k],
                                         recv_sem=recv_sems.at[k], device_id=(px, py, c),
                                         device_id_type=MESH).wait_recv()
        for cp in sends:
            cp.wait_send()
        local.wait()

    return pl.pallas_call(
        body, name="gather_shards", in_specs=[hbm], out_specs=hbm,
        out_shape=jax.ShapeDtypeStruct((4,) + packed.shape, packed.dtype),
        scratch_shapes=[pltpu.SemaphoreType.DMA((3,)), pltpu.SemaphoreType.DMA((3,)), pltpu.SemaphoreType.DMA],
    )(packed)


def exchange_grads(send, small):
    hbm = pl.BlockSpec(memory_space=pl.ANY)

    def body(send_ref, small_ref, recv_ref, all_ref, send_sems, recv_sems, ssend_sems, srecv_sems, local_sem):
        x, y, c = _place()
        me = 4 * x + 2 * y + c
        chips = [(1 - x, y), (x, 1 - y), (1 - x, 1 - y)]
        local = pltpu.make_async_copy(small_ref, all_ref.at[me], local_sem)
        local.start()
        sends = []
        for k, (px, py) in enumerate(chips):
            cp = pltpu.make_async_remote_copy(src_ref=send_ref.at[2 * px + py], dst_ref=recv_ref.at[k],
                                              send_sem=send_sems.at[k], recv_sem=recv_sems.at[k],
                                              device_id=(px, py, c), device_id_type=MESH)
            cp.start()
            sends.append(cp)
        others = [(x ^ ((r >> 2) & 1), y ^ ((r >> 1) & 1), c ^ (r & 1)) for r in range(1, 8)]
        for r, peer in enumerate(others):
            cp = pltpu.make_async_remote_copy(src_ref=small_ref, dst_ref=all_ref.at[me], send_sem=ssend_sems.at[r],
                                              recv_sem=srecv_sems.at[r], device_id=peer, device_id_type=MESH)
            cp.start()
            sends.append(cp)
        for k, (px, py) in enumerate(chips):
            pltpu.make_async_remote_copy(src_ref=send_ref.at[2 * px + py], dst_ref=recv_ref.at[k],
                                         send_sem=send_sems.at[k], recv_sem=recv_sems.at[k],
                                         device_id=(px, py, c), device_id_type=MESH).wait_recv()
        for r, (px, py, pc) in enumerate(others):
            pltpu.make_async_remote_copy(src_ref=small_ref, dst_ref=all_ref.at[4 * px + 2 * py + pc],
                                         send_sem=ssend_sems.at[r], recv_sem=srecv_sems.at[r],
                                         device_id=(px, py, pc), device_id_type=MESH).wait_recv()
        for cp in sends:
            cp.wait_send()
        local.wait()

    return pl.pallas_call(
        body, name="exchange_grads", in_specs=[hbm, hbm], out_specs=[hbm, hbm],
        out_shape=[jax.ShapeDtypeStruct((3,) + send.shape[1:], send.dtype),
                   jax.ShapeDtypeStruct((8,) + small.shape, small.dtype)],
        scratch_shapes=[pltpu.SemaphoreType.DMA((3,)), pltpu.SemaphoreType.DMA((3,)),
                        pltpu.SemaphoreType.DMA((7,)), pltpu.SemaphoreType.DMA((7,)), pltpu.SemaphoreType.DMA],
    )(send, small)


def swap_with_sibling(part):
    hbm = pl.BlockSpec(memory_space=pl.ANY)

    def body(src_ref, out_ref, send_sem, recv_sem):
        x, y, c = _place()
        cp = pltpu.make_async_remote_copy(src_ref=src_ref, dst_ref=out_ref, send_sem=send_sem, recv_sem=recv_sem,
                                          device_id=(x, y, 1 - c), device_id_type=MESH)
        cp.start()
        cp.wait()

    return pl.pallas_call(
        body, name="swap_with_sibling", in_specs=[hbm], out_specs=hbm,
        out_shape=jax.ShapeDtypeStruct(part.shape, part.dtype),
        scratch_shapes=[pltpu.SemaphoreType.DMA, pltpu.SemaphoreType.DMA],
    )(part)


def adamw(name, w, g, m, v):
    r, c = w.shape
    tr = r if r * c <= 65536 else _tile(r, 128, 8)

    def body(w_ref, g_ref, m_ref, v_ref, d_ref, nm_ref, nv_ref):
        gv = g_ref[...]
        m_new = ADAM_B1 * m_ref[...] + (1.0 - ADAM_B1) * gv
        v_new = ADAM_B2 * v_ref[...] + (1.0 - ADAM_B2) * (gv * gv)
        m_hat = m_new / (1.0 - ADAM_B1 ** ADAM_STEP)
        v_hat = v_new / (1.0 - ADAM_B2 ** ADAM_STEP)
        d_ref[...] = -ADAM_LR * (m_hat / (jnp.sqrt(v_hat) + ADAM_EPS) + ADAM_WD * w_ref[...])
        nm_ref[...] = m_new
        nv_ref[...] = v_new

    spec = pl.BlockSpec((tr, c), lambda i: (i, 0))
    return pl.pallas_call(
        body, name=name, grid=(r // tr,), in_specs=[spec] * 4, out_specs=[spec] * 3,
        out_shape=[jax.ShapeDtypeStruct((r, c), F32)] * 3,
        compiler_params=pltpu.CompilerParams(dimension_semantics=("arbitrary",)),
    )(w, g, m, v)


def _pack_rows(shapes):
    rows = [(r * c) // PACK_W for (r, c) in shapes]
    for (r, c) in shapes:
        assert (r * c) % PACK_W == 0
    total = sum(rows)
    return rows, -(-total // 16) * 16


def pack_shard(parts, total_rows):
    flat = jnp.concatenate([p.reshape(-1, PACK_W) for p in parts], axis=0)
    return jnp.pad(flat, ((0, total_rows - flat.shape[0]), (0, 0)))


def split_full(name, full, s):
    if name in COL_SHARDED:
        c = full.shape[1] // 4
        return full[:, s * c:(s + 1) * c]
    r = full.shape[0] // 4
    return full[s * r:(s + 1) * r]


def join_shards(name, shards):
    return jnp.concatenate(shards, axis=1 if name in COL_SHARDED else 0)


def kernel(x, meta_tokens, w_in, b_gate, lb_logits, hg_norm_g, w_hg_o, q_a_norm_g, w_q_b, kv_a_norm_g, w_kv_b, w_mla_o, w_out, mix_pre_g, mix_post_g, ffn_pre_g, ffn_post_g, w_ffn_in, w_ffn_out, loss_target, m_meta_tokens, m_w_in, m_b_gate, m_lb_logits, m_hg_norm_g, m_w_hg_o, m_q_a_norm_g, m_w_q_b, m_kv_a_norm_g, m_w_kv_b, m_w_mla_o, m_w_out, m_mix_pre_g, m_mix_post_g, m_ffn_pre_g, m_ffn_post_g, m_w_ffn_in, m_w_ffn_out, v_meta_tokens, v_w_in, v_b_gate, v_lb_logits, v_hg_norm_g, v_w_hg_o, v_q_a_norm_g, v_w_q_b, v_kv_a_norm_g, v_w_kv_b, v_w_mla_o, v_w_out, v_mix_pre_g, v_mix_post_g, v_ffn_pre_g, v_ffn_post_g, v_w_ffn_in, v_w_ffn_out):
    wts = dict(meta_tokens=meta_tokens, w_in=w_in[0], b_gate=b_gate, lb_logits=lb_logits, hg_norm_g=hg_norm_g,
               w_hg_o=w_hg_o[0], q_a_norm_g=q_a_norm_g, w_q_b=w_q_b[0], kv_a_norm_g=kv_a_norm_g, w_kv_b=w_kv_b[0],
               w_mla_o=w_mla_o[0], w_out=w_out[0], mix_pre_g=mix_pre_g, mix_post_g=mix_post_g, ffn_pre_g=ffn_pre_g,
               ffn_post_g=ffn_post_g, w_ffn_in=w_ffn_in[0], w_ffn_out=w_ffn_out[0])
    mom_m = dict(meta_tokens=m_meta_tokens, w_in=m_w_in[0], b_gate=m_b_gate, lb_logits=m_lb_logits,
                 hg_norm_g=m_hg_norm_g, w_hg_o=m_w_hg_o[0], q_a_norm_g=m_q_a_norm_g, w_q_b=m_w_q_b[0],
                 kv_a_norm_g=m_kv_a_norm_g, w_kv_b=m_w_kv_b[0], w_mla_o=m_w_mla_o[0], w_out=m_w_out[0],
                 mix_pre_g=m_mix_pre_g, mix_post_g=m_mix_post_g, ffn_pre_g=m_ffn_pre_g, ffn_post_g=m_ffn_post_g,
                 w_ffn_in=m_w_ffn_in[0], w_ffn_out=m_w_ffn_out[0])
    mom_v = dict(meta_tokens=v_meta_tokens, w_in=v_w_in[0], b_gate=v_b_gate, lb_logits=v_lb_logits,
                 hg_norm_g=v_hg_norm_g, w_hg_o=v_w_hg_o[0], q_a_norm_g=v_q_a_norm_g, w_q_b=v_w_q_b[0],
                 kv_a_norm_g=v_kv_a_norm_g, w_kv_b=v_w_kv_b[0], w_mla_o=v_w_mla_o[0], w_out=v_w_out[0],
                 mix_pre_g=v_mix_pre_g, mix_post_g=v_mix_post_g, ffn_pre_g=v_ffn_pre_g, ffn_post_g=v_ffn_post_g,
                 w_ffn_in=v_w_ffn_in[0], w_ffn_out=v_w_ffn_out[0])

    bl, seq, d = x.shape
    lp = PAD_FRONT + N_META + seq
    t_rows = bl * lp
    nh = d // HEAD
    ql, kvl = wts["w_q_b"].shape[0], wts["w_kv_b"].shape[0]
    nm = (4 * wts["w_mla_o"].shape[0]) // HEAD
    ffn = 4 * wts["w_ffn_out"].shape[0]
    mla_w = ql + kvl + HEAD
    assert ql == kvl and ql % HEAD == 0 and seq % SEQ_BLOCK == 0 and d % HEAD == 0
    scale = (HEAD + ROPE) ** -0.5
    my_chip = 2 * lax.axis_index("x") + lax.axis_index("y")

    shard_shapes = [wts[n].shape for n in BIG]
    pack_rows, pack_total = _pack_rows(shard_shapes)
    mcols = meta_tokens.shape[1]
    meta_rows = (N_META * mcols * 2) // PACK_W
    assert (N_META * mcols * 2) % PACK_W == 0
    meta_bits = lax.bitcast_convert_type(meta_tokens, BF16).reshape(meta_rows, PACK_W)
    gathered = gather_shards(pack_shard([meta_bits] + [_bf(wts[n]) for n in BIG],
                                        -(-(meta_rows + sum(pack_rows)) // 16) * 16))
    meta_full = jnp.concatenate(
        [lax.bitcast_convert_type(gathered[s, :meta_rows].reshape(N_META, mcols, 2), F32) for s in range(4)], axis=1)
    full = {}
    off = meta_rows
    for n, rows, (r, c) in zip(BIG, pack_rows, shard_shapes):
        full[n] = join_shards(n, [gathered[s, off:off + rows].reshape(r, c) for s in range(4)])
        off += rows
    w_main = jnp.concatenate([full["w_in"][:, :4 * d], full["w_in"][:, -2 * d:]], axis=1)
    w_mla = jnp.pad(full["w_in"][:, 4 * d:4 * d + ql + kvl + ROPE], ((0, 0), (0, HEAD - ROPE)))
    w_qb = jnp.pad(full["w_q_b"].reshape(ql, nm, HEAD + ROPE), ((0, 0), (0, 0), (0, QK_PAD - HEAD - ROPE))
                   ).reshape(ql, nm * QK_PAD)
    w_kvb = full["w_kv_b"]

    h0 = jnp.concatenate([jnp.zeros((bl, PAD_FRONT, d), F32), jnp.broadcast_to(meta_full[None], (bl, N_META, d)), x],
                         axis=1).reshape(t_rows, d)
    tgt = jnp.concatenate([jnp.zeros((bl, PAD_FRONT + N_META, d), F32), loss_target], axis=1).reshape(t_rows, d)
    pos = (jnp.arange(lp, dtype=jnp.int32) - PAD_FRONT).astype(F32)
    inv_freq = 1.0 / (ROPE_THETA ** (jnp.arange(0, ROPE, 2, dtype=F32) / ROPE))
    ang = pos[:, None] * inv_freq[None, :]
    zeros32 = jnp.zeros((lp, ROPE_HALF), F32)
    zeros64 = jnp.zeros((lp, HEAD - ROPE), F32)
    t_cos = jnp.concatenate([jnp.cos(ang), jnp.cos(ang), zeros64], axis=1)
    t_up = jnp.concatenate([zeros32, jnp.sin(ang), zeros64], axis=1)
    t_dn = jnp.concatenate([-jnp.sin(ang), zeros32, zeros64], axis=1)
    real = jnp.broadcast_to((jnp.arange(lp) >= PAD_FRONT + N_META).astype(F32)[:, None], (lp, d))
    lb_soft = jax.nn.softmax(lb_logits.astype(F32), axis=0)
    lb = lb_soft[0:1]

    (u1,) = rowwise("norm_mix_pre", lambda h, g: _rms(h, g), [(h0, d, 0)], [], [mix_pre_g], [(d, BF16)])
    proj_main = matmul("proj_main", u1, w_main, "nn", out_dtype=BF16)
    proj_mla = matmul("proj_mla", u1, w_mla, "nn", out_dtype=BF16)
    hg_consts = _hg_constants()
    o_scan, states, a_mats = hgrn_fwd(proj_main, lb, hg_consts, bl, lp, d)

    def hg_out_fn(o, hg, g):
        return jnp.concatenate([_rms(o[:, h * HEAD:(h + 1) * HEAD], g) for h in range(nh)], axis=1) * _silu(hg)

    (o_hg,) = rowwise("hgrn_out", hg_out_fn, [(o_scan, d, 0), (proj_main, d, 3)], [], [hg_norm_g], [(d, BF16)])
    y_a = matmul("y_a", o_hg, _bf(full["w_hg_o"]), "nn", out_dtype=BF16)

    qn, kvn = rowwise("mla_norms", lambda cq, ckv, gq, gk: (_rms(cq, gq), _rms(ckv, gk)),
                      [(proj_mla, ql, 0), (proj_mla, kvl, 1)], [], [q_a_norm_g, kv_a_norm_g],
                      [(ql, BF16), (kvl, BF16)])
    q_full = matmul("q_up", qn, w_qb, "nn", out_dtype=BF16)
    kv_full = matmul("kv_up", kvn, w_kvb, "nn", out_dtype=BF16)

    def mla_prep_fn(qf, kvf, kpe, cos, s_up, s_dn):
        kpe_r = _rope(kpe, cos, s_up, s_dn)
        qs, ks, vs = [], [], []
        for h in range(nm):
            qs += [qf[:, h * QK_PAD:h * QK_PAD + HEAD], _rope(qf[:, h * QK_PAD + HEAD:(h + 1) * QK_PAD], cos, s_up, s_dn)]
            ks += [kvf[:, h * QK_PAD:h * QK_PAD + HEAD], kpe_r]
            vs += [kvf[:, h * QK_PAD + HEAD:(h + 1) * QK_PAD]]
        return jnp.concatenate(qs, axis=1), jnp.concatenate(ks, axis=1), jnp.concatenate(vs, axis=1)

    kpe_blk = (ql + kvl) // HEAD
    q_cat, k_cat, v_att = rowwise("mla_prep", mla_prep_fn,
                                  [(q_full, nm * QK_PAD, 0), (kv_full, nm * QK_PAD, 0), (proj_mla, HEAD, kpe_blk)],
                                  [t_cos, t_up, t_dn], [], [(nm * QK_PAD, BF16), (nm * QK_PAD, BF16), (nm * HEAD, BF16)])
    at = _attn_tile(lp)
    v_t = v_att.reshape(bl, lp // at, at, nm, HEAD).transpose(0, 3, 1, 4, 2)
    k_t = k_cat.reshape(bl, lp // at, at, nm, QK_PAD).transpose(0, 3, 1, 4, 2)
    o_mla, lse = attn_fwd_t(q_cat, k_cat, v_t, bl, lp, nm, scale)
    y_b = matmul("y_b", o_mla, _bf(full["w_mla_o"]), "nn", out_dtype=BF16)

    def gate_fn(ya, yb, ga, gb, bias):
        return _sigmoid(ga + bias[:, :d]) * ya + _sigmoid(gb + bias[:, d:]) * yb

    (z,) = rowwise("gate_mix", gate_fn, [(y_a, d, 0), (y_b, d, 0), (proj_main, d, 4), (proj_main, d, 5)], [],
                   [b_gate], [(d, BF16)])
    mixed = matmul("mixed", z, _bf(full["w_out"]), "nn")

    def mid_fn(h, mx, g_post, g_pre):
        h1 = h + _rms(mx, g_post)
        return h1, _rms(h1, g_pre)

    h1, u2 = rowwise("norm_mid", mid_fn, [(h0, d, 0), (mixed, d, 0)], [], [mix_post_g, ffn_pre_g],
                     [(d, F32), (d, BF16)])
    gu = matmul("ffn_in", u2, _bf(full["w_ffn_in"]), "nn", out_dtype=BF16)
    (act,) = rowwise("swiglu", lambda gt, up: _silu(gt) * up, [(gu, ffn, 0), (gu, ffn, 1)], [], [], [(ffn, BF16)])
    f_out = matmul("ffn_out", act, _bf(full["w_ffn_out"]), "nn")

    def loss_fn(h1v, fv, tg, realv, g_post):
        h2 = h1v + _rms(fv, g_post)
        diff = (h2 - tg) * realv
        part = jnp.broadcast_to(0.5 * jnp.sum(diff * diff, keepdims=True) / d, (1, HEAD))
        dy = diff / d
        df, dg = _rms_bwd(fv, g_post, dy)
        return dy, df, part, dg

    dy, df, loss_part, g_ffn_post = rowwise("loss_head", loss_fn, [(h1, d, 0), (f_out, d, 0), (tgt, d, 0)], [real],
                                            [ffn_post_g], [(d, F32), (d, BF16)], [(1, HEAD), (1, d)])
    grads = {}
    d_act = matmul("d_act", df, _bf(full["w_ffn_out"]), "nt", out_dtype=BF16)
    grads["w_ffn_out"] = matmul("gw_ffn_out", act, df, "tn")

    def swiglu_bwd_fn(gt, up, da):
        return jnp.concatenate([da * up * _silu_grad(gt), da * _silu(gt)], axis=1)

    (dgu,) = rowwise("swiglu_bwd", swiglu_bwd_fn, [(gu, ffn, 0), (gu, ffn, 1), (d_act, ffn, 0)], [], [],
                     [(2 * ffn, BF16)])
    du2 = matmul("d_u2", dgu, _bf(full["w_ffn_in"]), "nt", out_dtype=BF16)
    grads["w_ffn_in"] = matmul("gw_ffn_in", u2, dgu, "tn")

    def mid_bwd_fn(dyv, h1v, du2v, mx, g_pre, g_post):
        dx, dg_pre = _rms_bwd(h1v, g_pre, du2v)
        dh1 = dyv + dx
        dmx, dg_post = _rms_bwd(mx, g_post, dh1)
        return dh1, dmx, dg_pre, dg_post

    dh1, dmixed, g_ffn_pre, g_mix_post = rowwise("norm_mid_bwd", mid_bwd_fn,
                                                 [(dy, d, 0), (h1, d, 0), (du2, d, 0), (mixed, d, 0)], [],
                                                 [ffn_pre_g, mix_post_g], [(d, F32), (d, BF16)], [(1, d), (1, d)])
    dz = matmul("d_z", dmixed, _bf(full["w_out"]), "nt", out_dtype=BF16)
    grads["w_out"] = matmul("gw_out", z, dmixed, "tn")

    def gate_bwd_fn(dzv, ya, yb, ga, gb, bias):
        sa, sb = _sigmoid(ga + bias[:, :d]), _sigmoid(gb + bias[:, d:])
        dga = dzv * ya * sa * (1.0 - sa)
        dgb = dzv * yb * sb * (1.0 - sb)
        dgates = jnp.concatenate([dga, dgb], axis=1)
        return dzv * sa, dzv * sb, dgates, jnp.sum(dgates, axis=0, keepdims=True)

    dy_a, dy_b, dgates, g_b_gate = rowwise("gate_mix_bwd", gate_bwd_fn,
                                           [(dz, d, 0), (y_a, d, 0), (y_b, d, 0), (proj_main, d, 4), (proj_main, d, 5)],
                                           [], [b_gate], [(d, BF16), (d, BF16), (2 * d, BF16)], [(1, 2 * d)])
    do_hg = matmul("d_o_hg", dy_a, _bf(full["w_hg_o"]), "nt", out_dtype=BF16)
    grads["w_hg_o"] = matmul("gw_hg_o", o_hg, dy_a, "tn")
    do_mla = matmul("d_o_mla", dy_b, _bf(full["w_mla_o"]), "nt", out_dtype=BF16)
    grads["w_mla_o"] = matmul("gw_mla_o", o_mla, dy_b, "tn")

    def hg_out_bwd_fn(do, o, hg, g):
        sg = _silu(hg)
        dn = do * sg
        dos, dgs, ons = [], 0.0, []
        for h in range(nh):
            sl = slice(h * HEAD, (h + 1) * HEAD)
            dx, dg = _rms_bwd(o[:, sl], g, dn[:, sl])
            dos.append(dx)
            dgs = dgs + dg
            ons.append(_rms(o[:, sl], g))
        dhg = do * jnp.concatenate(ons, axis=1) * _silu_grad(hg)
        return jnp.concatenate(dos, axis=1), dhg, dgs

    do_scan, dhg, g_hg_norm = rowwise("hgrn_out_bwd", hg_out_bwd_fn, [(do_hg, d, 0), (o_scan, d, 0), (proj_main, d, 3)],
                                      [], [hg_norm_g], [(d, F32), (d, BF16)], [(1, HEAD)])
    dhq, dhf, dhi, g_lb = hgrn_bwd(proj_main, lb, hg_consts, states, a_mats, do_scan, bl, lp, d)

    dq_cat, dk_cat, dv_att = attn_bwd_t(q_cat, k_cat, k_t, v_att, o_mla, do_mla, lse, bl, lp, nm, scale)

    def mla_prep_bwd_fn(dqc, dkc, dvv, cos, s_up, s_dn):
        dqs, dkvs, dkpe = [], [], 0.0
        for h in range(nm):
            dqs += [dqc[:, h * QK_PAD:h * QK_PAD + HEAD],
                    _rope_bwd(dqc[:, h * QK_PAD + HEAD:(h + 1) * QK_PAD], cos, s_up, s_dn)]
            dkvs += [dkc[:, h * QK_PAD:h * QK_PAD + HEAD], dvv[:, h * HEAD:(h + 1) * HEAD]]
            dkpe = dkpe + dkc[:, h * QK_PAD + HEAD:(h + 1) * QK_PAD]
        return jnp.concatenate(dqs, axis=1), jnp.concatenate(dkvs, axis=1), _rope_bwd(dkpe, cos, s_up, s_dn)

    dq_full, dkv_full, dkpe = rowwise("mla_prep_bwd", mla_prep_bwd_fn,
                                      [(dq_cat, nm * QK_PAD, 0), (dk_cat, nm * QK_PAD, 0), (dv_att, nm * HEAD, 0)],
                                      [t_cos, t_up, t_dn], [],
                                      [(nm * QK_PAD, BF16), (nm * QK_PAD, BF16), (HEAD, F32)])
    dqn = matmul("d_qn", dq_full, w_qb, "nt", out_dtype=BF16)
    g_wqb = matmul("gw_q_b", qn, dq_full, "tn")
    grads["w_q_b"] = g_wqb.reshape(ql, nm, QK_PAD)[:, :, :HEAD + ROPE].reshape(ql, nm * (HEAD + ROPE))
    dkvn = matmul("d_kvn", dkv_full, w_kvb, "nt", out_dtype=BF16)
    grads["w_kv_b"] = matmul("gw_kv_b", kvn, dkv_full, "tn")

    def mla_norms_bwd_fn(dqnv, dkvnv, cq, ckv, dkpev, gq, gk):
        dcq, dgq = _rms_bwd(cq, gq, dqnv)
        dckv, dgk = _rms_bwd(ckv, gk, dkvnv)
        return jnp.concatenate([dcq, dckv, dkpev], axis=1), dgq, dgk

    dmla, g_q_norm, g_kv_norm = rowwise("mla_norms_bwd", mla_norms_bwd_fn,
                                        [(dqn, ql, 0), (dkvn, kvl, 0), (proj_mla, ql, 0), (proj_mla, kvl, 1),
                                         (dkpe, HEAD, 0)], [], [q_a_norm_g, kv_a_norm_g],
                                        [(mla_w, BF16)], [(1, ql), (1, kvl)])

    w_main_bf = w_main
    pieces = [(dhq, w_main_bf[:, 0:d]), (dhf, w_main_bf[:, d:2 * d]), (dhi, w_main_bf[:, 2 * d:3 * d]),
              (dhg, w_main_bf[:, 3 * d:4 * d]), (dgates, w_main_bf[:, 4 * d:6 * d]), (dmla, w_mla)]
    du1 = None
    gw_parts = []
    for k, (dp, wp) in enumerate(pieces):
        du1 = matmul(f"d_u1_{k}", dp, wp, "nt", addend=du1)
        gw_parts.append(matmul(f"gw_in_{k}", u1, dp, "tn"))
    grads["w_in"] = jnp.concatenate(gw_parts[:4] + [gw_parts[5][:, :ql + kvl + ROPE], gw_parts[4]], axis=1)

    def first_bwd_fn(dh1v, h, du1v, g):
        dx, dg = _rms_bwd(h, g, du1v)
        return dh1v + dx, dg

    dh0, g_mix_pre = rowwise("norm_mix_pre_bwd", first_bwd_fn, [(dh1, d, 0), (h0, d, 0), (du1, d, 0)], [],
                             [mix_pre_g], [(d, F32)], [(1, d)])
    dh0 = dh0.reshape(bl, lp, d)
    grad_x = dh0[:, PAD_FRONT + N_META:]

    send = jnp.stack([pack_shard([split_full(n, grads[n], s) for n in BIG], pack_total) for s in range(4)])
    mine = lax.dynamic_index_in_dim(send, my_chip, axis=0, keepdims=False)
    p0 = lb_soft[0:1]
    g_lb_logits = jnp.concatenate([g_lb * p0 * (1.0 - p0), -g_lb * p0 * (1.0 - p0)], axis=0)

    def row_of(vec):
        return vec.reshape(-1, d) if vec.size >= d else jnp.pad(vec.reshape(1, -1), ((0, 0), (0, d - vec.size)))

    small_parts = dict(b_gate=g_b_gate, lb_logits=g_lb_logits, hg_norm_g=g_hg_norm, q_a_norm_g=g_q_norm,
                       kv_a_norm_g=g_kv_norm, mix_pre_g=g_mix_pre, mix_post_g=g_mix_post, ffn_pre_g=g_ffn_pre,
                       ffn_post_g=g_ffn_post)
    g_meta = jnp.sum(dh0[:, PAD_FRONT:PAD_FRONT + N_META], axis=0)
    small_rows = [row_of(small_parts[n]) for n in SMALL] + [row_of(g_meta)]
    n_small = sum(r.shape[0] for r in small_rows)
    small = jnp.pad(jnp.concatenate(small_rows, axis=0), ((0, -(-n_small // 8) * 8 - n_small), (0, 0)))
    recv, all_small = exchange_grads(_bf(send), small)

    rt = _tile(pack_total, 512, 16)
    recv2 = recv.reshape(3 * pack_total, PACK_W)
    (part_sum,) = rowwise("sum_chips", lambda a, r0, r1, r2: a + r0.astype(F32) + r1.astype(F32) + r2.astype(F32),
                          [(mine, PACK_W, 0)] + [(recv2, PACK_W, 0, k * (pack_total // rt)) for k in range(3)],
                          [], [], [(PACK_W, F32)], tm=rt)
    sib_sum = swap_with_sibling(part_sum)
    (g_packed,) = rowwise("sum_cores", lambda a, b: a + b, [(part_sum, PACK_W, 0), (sib_sum, PACK_W, 0)], [], [],
                          [(PACK_W, F32)], tm=rt)
    small_t = small.shape[0]

    def sum8_fn(*slabs):
        acc = slabs[0]
        for s in slabs[1:]:
            acc = acc + s
        return acc

    (g_small,) = rowwise("sum_small", sum8_fn, [(all_small.reshape(8 * small_t, d), d, 0, k) for k in range(8)],
                         [], [], [(d, F32)], tm=small_t, n_rows=small_t)

    g_final = {}
    off = 0
    for n, rows, (r, c) in zip(BIG, pack_rows, shard_shapes):
        g_final[n] = g_packed[off:off + rows].reshape(r, c)
        off += rows
    off = 0
    for n, part in zip(SMALL, small_rows[:-1]):
        rows = part.shape[0]
        g_final[n] = g_small[off:off + rows, :d].reshape(-1)[:wts[n].size].reshape(wts[n].shape)
        off += rows
    mcols = meta_tokens.shape[1]
    g_final["meta_tokens"] = lax.dynamic_slice_in_dim(g_small[off:off + N_META, :d], my_chip * mcols, mcols, axis=1)

    delta, new_m, new_v = {}, {}, {}
    for n in WEIGHTS:
        w2 = wts[n].reshape(-1, wts[n].shape[-1])
        dl, mn, vn = adamw("adamw_" + n, w2, g_final[n].reshape(w2.shape), mom_m[n].reshape(w2.shape),
                           mom_v[n].reshape(w2.shape))
        delta[n], new_m[n], new_v[n] = dl, mn, vn

    loss = lax.psum(loss_part[0, 0], ("x", "y", "c"))

    def shaped(n, a):
        return a.reshape((1,) + wts[n].shape) if n in BIG else a.reshape(wts[n].shape)

    return (loss, grad_x, *[shaped(n, g_final[n]) for n in WEIGHTS], *[shaped(n, delta[n]) for n in WEIGHTS],
            *[shaped(n, new_m[n]) for n in WEIGHTS], *[shaped(n, new_v[n]) for n in WEIGHTS])
```

```python
import functools
import math

import jax
import jax.numpy as jnp
from jax import lax
from jax.experimental import pallas as pl
from jax.experimental.pallas import tpu as pltpu

F32 = jnp.float32
BF16 = jnp.bfloat16
MESH = pl.DeviceIdType.MESH

N_META = 16
NORM_EPS = 1e-6
HEAD = 128
ROPE = 64
ROPE_HALF = ROPE // 2
QK_PAD = 2 * HEAD
CHUNK = 16
ROPE_THETA = 10000.0
SEQ_BLOCK = 256
PAD_FRONT = SEQ_BLOCK - N_META
PACK_W = 1024
NEG = -1e30
VMEM_LIMIT = 56 * 1024 * 1024
ATTN_HEADS_PER_STEP = 1
ATTN_TILE_MAX = 768

ADAM_LR, ADAM_B1, ADAM_B2, ADAM_EPS, ADAM_WD, ADAM_STEP = 0.001, 0.9, 0.999, 1e-08, 0.01, 10

BIG = ("w_in", "w_hg_o", "w_q_b", "w_kv_b", "w_mla_o", "w_out", "w_ffn_in", "w_ffn_out")
COL_SHARDED = ("w_in", "w_q_b", "w_kv_b", "w_ffn_in")
SMALL = ("b_gate", "lb_logits", "hg_norm_g", "q_a_norm_g", "kv_a_norm_g", "mix_pre_g", "mix_post_g",
         "ffn_pre_g", "ffn_post_g")
WEIGHTS = ("meta_tokens", "w_in", "b_gate", "lb_logits", "hg_norm_g", "w_hg_o", "q_a_norm_g", "w_q_b",
           "kv_a_norm_g", "w_kv_b", "w_mla_o", "w_out", "mix_pre_g", "mix_post_g", "ffn_pre_g", "ffn_post_g",
           "w_ffn_in", "w_ffn_out")


def _tile(n, cap, unit=128):
    if n <= cap:
        return n
    best = None
    for t in range(unit, cap + 1, unit):
        if n % t == 0:
            best = t
    assert best is not None, (n, cap, unit)
    return best


def _sigmoid(x):
    return 1.0 / (1.0 + jnp.exp(-x))


def _bf(x):
    return x.astype(BF16)


def rowwise(name, fn, row_ins, seq_tabs, consts, row_outs, acc_outs=(), tm=SEQ_BLOCK, n_rows=None):
    t_rows = row_ins[0][0].shape[0] if n_rows is None else n_rows
    nt = t_rows // tm
    assert t_rows % tm == 0
    n_in = len(row_ins) + len(seq_tabs) + len(consts)
    n_row = len(row_outs)

    def body(*refs):
        vals = [r[...].astype(F32) for r in refs[:n_in]]
        res = fn(*vals)
        if not isinstance(res, (tuple, list)):
            res = (res,)
        outs = refs[n_in:]
        for k in range(n_row):
            outs[k][...] = res[k].astype(outs[k].dtype)
        if acc_outs:
            @pl.when(pl.program_id(0) == 0)
            def _():
                for k in range(len(acc_outs)):
                    outs[n_row + k][...] = jnp.zeros_like(outs[n_row + k])

            for k in range(len(acc_outs)):
                outs[n_row + k][...] += res[n_row + k]

    row_ins = [tuple(e) + (0,) * (4 - len(e)) for e in row_ins]
    in_specs = [pl.BlockSpec((tm, w), functools.partial(lambda i, j, ro: (i + ro, j), j=j, ro=ro))
                for (_, w, j, ro) in row_ins]
    for tab in seq_tabs:
        per = tab.shape[0] // tm
        in_specs.append(pl.BlockSpec((tm, tab.shape[1]), functools.partial(lambda i, per: (i % per, 0), per=per)))
    for c in consts:
        in_specs.append(pl.BlockSpec(c.shape, lambda i: (0, 0)))
    out_specs = [pl.BlockSpec((tm, w), lambda i: (i, 0)) for (w, _) in row_outs]
    out_specs += [pl.BlockSpec(s, lambda i: (0, 0)) for s in acc_outs]
    out_shape = [jax.ShapeDtypeStruct((t_rows, w), dt) for (w, dt) in row_outs]
    out_shape += [jax.ShapeDtypeStruct(s, F32) for s in acc_outs]
    res = pl.pallas_call(
        body, name=name, grid=(nt,), in_specs=in_specs, out_specs=out_specs, out_shape=out_shape,
        compiler_params=pltpu.CompilerParams(dimension_semantics=("arbitrary",)),
    )(*[e[0] for e in row_ins], *seq_tabs, *consts)
    return res


def matmul(name, a, b, mode, out_dtype=F32, addend=None):
    if mode == "tn":
        kdim, m = a.shape
        n = b.shape[1]
        tn = _tile(n, 1536)
        tm, tk = _tile(m, 1408 if tn <= 1024 else 1024), _tile(kdim, 512)
        a_spec = pl.BlockSpec((tk, tm), lambda i, j, k: (k, i))
        b_spec = pl.BlockSpec((tk, tn), lambda i, j, k: (k, j))
        dims = (((0,), (0,)), ((), ()))
    else:
        m, kdim = a.shape
        n = b.shape[1] if mode == "nn" else b.shape[0]
        tn, tk = _tile(n, 1536), _tile(kdim, 1536)
        tm = _tile(m, 1024 if tn <= 1024 else 512)
        a_spec = pl.BlockSpec((tm, tk), lambda i, j, k: (i, k))
        if mode == "nn":
            b_spec = pl.BlockSpec((tk, tn), lambda i, j, k: (k, j))
            dims = (((1,), (0,)), ((), ()))
        else:
            b_spec = pl.BlockSpec((tn, tk), lambda i, j, k: (j, k))
            dims = (((1,), (1,)), ((), ()))
    nk = kdim // tk
    has_add = addend is not None

    def body(*refs):
        a_ref, b_ref = refs[0], refs[1]
        add_ref = refs[2] if has_add else None
        o_ref, acc_ref = refs[-2], refs[-1]
        k = pl.program_id(2)

        @pl.when(k == 0)
        def _():
            acc_ref[...] = jnp.zeros_like(acc_ref)

        acc_ref[...] += lax.dot_general(a_ref[...], b_ref[...], dims, preferred_element_type=F32)

        @pl.when(k == nk - 1)
        def _():
            r = acc_ref[...]
            if has_add:
                r = r + add_ref[...]
            o_ref[...] = r.astype(o_ref.dtype)

    in_specs = [a_spec, b_spec]
    args = [a, b]
    if has_add:
        in_specs.append(pl.BlockSpec((tm, tn), lambda i, j, k: (i, j)))
        args.append(addend)
    return pl.pallas_call(
        body, name=name, grid=(m // tm, n // tn, nk), in_specs=in_specs,
        out_specs=pl.BlockSpec((tm, tn), lambda i, j, k: (i, j)),
        out_shape=jax.ShapeDtypeStruct((m, n), out_dtype),
        scratch_shapes=[pltpu.VMEM((tm, tn), F32)],
        compiler_params=pltpu.CompilerParams(dimension_semantics=("arbitrary", "arbitrary", "arbitrary"),
                                             vmem_limit_bytes=VMEM_LIMIT),
    )(*args)


def _rms(x, g):
    r = lax.rsqrt(jnp.mean(x * x, axis=-1, keepdims=True) + NORM_EPS)
    return x * r * g


def _rms_bwd(x, g, dy):
    r = lax.rsqrt(jnp.mean(x * x, axis=-1, keepdims=True) + NORM_EPS)
    xh = x * r
    dyg = dy * g
    dx = r * (dyg - xh * jnp.mean(dyg * xh, axis=-1, keepdims=True))
    return dx, jnp.sum(dy * xh, axis=0, keepdims=True)


def _silu(x):
    return x * _sigmoid(x)


def _silu_grad(x):
    s = _sigmoid(x)
    return s * (1.0 + x * (1.0 - s))


def _rope(xs, cos, s_up, s_dn):
    return xs * cos + pltpu.roll(xs, ROPE_HALF, 1) * s_up + pltpu.roll(xs, HEAD - ROPE_HALF, 1) * s_dn


def _rope_bwd(dy, cos, s_up, s_dn):
    return dy * cos + pltpu.roll(dy * s_up, HEAD - ROPE_HALF, 1) + pltpu.roll(dy * s_dn, ROPE_HALF, 1)


HG_SUB = 128
HG_LEVELS = 7
HG_E_ROWS = (HG_LEVELS + 1) * HG_SUB
TN_DIMS = (((0,), (0,)), ((), ()))
NT_DIMS = (((1,), (1,)), ((), ()))


def _hg_constants():
    import numpy as np
    n = HG_SUB
    r = np.arange(n)[:, None]
    c = np.arange(n)[None, :]
    cs, ps = [], []
    for lvl in range(HG_LEVELS):
        m = (n // 2) >> lvl
        upper = (r % (2 * m)) >= m
        mid = (r // (2 * m)) * (2 * m) + m - 1
        cs.append(np.where(upper, (c > mid) & (c <= r), (c > r) & (c <= mid)))
        ps.append(((r // (2 * m)) == (c // (2 * m))) & upper & ((c % (2 * m)) < m))
    cs.append(c <= r)
    cs.append(np.ones((8, n), bool))
    cstack = np.concatenate(cs, 0).astype(np.float32)
    pstack = np.concatenate(ps, 0).astype(np.float32)
    pstack_t = np.concatenate([p.T for p in ps], 0).astype(np.float32)
    return (jnp.asarray(cstack, BF16), jnp.asarray(cstack[:HG_E_ROWS].T, BF16), jnp.asarray(pstack, F32),
            jnp.asarray(pstack_t, F32))


def _split_dot(c_bf, x):
    hi = _bf(x)
    lo = _bf(x - hi.astype(F32))
    r2 = jnp.dot(c_bf, jnp.concatenate([hi, lo], axis=1), preferred_element_type=F32)
    return r2[:, :HEAD] + r2[:, HEAD:]


def _hg_gates(hq, hf, lb):
    sq = _sigmoid(hq)
    sg = _sigmoid(hf)
    fg = lb + (1.0 - lb) * sg
    return sq, hq * sq, sg, fg, 1.0 - fg, jnp.log(fg)


def _hg_block_fwd(st, hq, hf, hi, lb, cstack, p_ref):
    _, q, _, _, k, g = _hg_gates(hq, hf, lb)
    v = hi
    e = _split_dot(cstack, g)
    bc = e[HG_LEVELS * HG_SUB:HG_E_ROWS]
    b_last = jnp.tile(e[HG_E_ROWS:], (HG_SUB // 8, 1))
    a = jnp.zeros((HG_SUB, HG_SUB), F32)
    for lvl in range(HG_LEVELS):
        x = jnp.exp(e[lvl * HG_SUB:(lvl + 1) * HG_SUB])
        a = a + p_ref[pl.ds(lvl * HG_SUB, HG_SUB), :] * lax.dot_general(_bf(q * x), _bf(k * x), NT_DIMS,
                                                                          preferred_element_type=F32)
    a_bf = _bf(a)
    diag = jnp.sum(q * k, axis=1, keepdims=True)
    o = (jnp.dot(a_bf, _bf(v), preferred_element_type=F32) + diag * v
         + lax.dot_general(_bf(q * jnp.exp(bc)), _bf(st), NT_DIMS, preferred_element_type=F32))
    kd = k * jnp.exp(b_last - bc)
    st_out = st * jnp.exp(b_last) + lax.dot_general(_bf(v), _bf(kd), TN_DIMS, preferred_element_type=F32)
    return st_out, o, a_bf


def _hg_block_bwd(st, dst_out, do, hq, hf, hi, lb, a_bf, cstack, cstack_t, p_ref, pt_ref):
    sq, q, sg, fg, k, g = _hg_gates(hq, hf, lb)
    v = hi
    e = _split_dot(cstack, g)
    bc = e[HG_LEVELS * HG_SUB:HG_E_ROWS]
    b_last = jnp.tile(e[HG_E_ROWS:], (HG_SUB // 8, 1))
    eb = jnp.exp(bc)
    qb = q * eb
    er = jnp.exp(b_last - bc)
    kd = k * er
    e_last = jnp.exp(b_last)
    do_bf, v_bf, dst_bf = _bf(do), _bf(v), _bf(dst_out)
    da = lax.dot_general(do_bf, v_bf, NT_DIMS, preferred_element_type=F32)
    dat = lax.dot_general(v_bf, do_bf, NT_DIMS, preferred_element_type=F32)
    d_diag = jnp.sum(do * v, axis=1, keepdims=True)
    dv = (lax.dot_general(a_bf, do_bf, TN_DIMS, preferred_element_type=F32)
          + jnp.sum(q * k, axis=1, keepdims=True) * do
          + lax.dot_general(_bf(kd), dst_bf, NT_DIMS, preferred_element_type=F32))
    dqb = jnp.dot(do_bf, _bf(st), preferred_element_type=F32)
    dst = dst_out * e_last + lax.dot_general(do_bf, _bf(qb), TN_DIMS, preferred_element_type=F32)
    dkd = jnp.dot(v_bf, dst_bf, preferred_element_type=F32)
    dq = dqb * eb + d_diag * k
    dk = dkd * er + d_diag * q
    d_last = (jnp.sum(dst_out * st * e_last, axis=0, keepdims=True)
              + jnp.sum(dkd * kd, axis=0, keepdims=True))
    des = []
    for lvl in range(HG_LEVELS):
        x = jnp.exp(e[lvl * HG_SUB:(lvl + 1) * HG_SUB])
        qh, kh = q * x, k * x
        dm = _bf(p_ref[pl.ds(lvl * HG_SUB, HG_SUB), :] * da)
        dmt = _bf(pt_ref[pl.ds(lvl * HG_SUB, HG_SUB), :] * dat)
        dqh = jnp.dot(dm, _bf(kh), preferred_element_type=F32)
        dkh = jnp.dot(dmt, _bf(qh), preferred_element_type=F32)
        dq = dq + dqh * x
        dk = dk + dkh * x
        des.append(dqh * qh + dkh * kh)
    des.append(dqb * qb - dkd * kd)
    dg = _split_dot(cstack_t, jnp.concatenate(des, axis=0)) + d_last
    dfg = dg / fg - dk
    dhq = dq * (sq * (1.0 + hq * (1.0 - sq)))
    dhf = dfg * (1.0 - lb) * sg * (1.0 - sg)
    return dst, dhq, dhf, dv, jnp.sum(dfg * (1.0 - sg), axis=0, keepdims=True)


def hgrn_fwd(proj_main, lb, consts, bl, lp, d):
    nh = d // HEAD
    rows_blk = _tile(lp, 768, SEQ_BLOCK)
    nb = lp // rows_blk
    spb = rows_blk // HG_SUB
    cstack, _, pstack, _ = consts

    def body(hq_ref, hf_ref, hi_ref, lb_ref, c_ref, p_ref, o_ref, st_ref, a_ref, s_ref):
        j = pl.program_id(2)

        @pl.when(j == 0)
        def _():
            s_ref[...] = jnp.zeros_like(s_ref)

        lbv = lb_ref[...]
        cs = c_ref[...]

        def sub(n, carry):
            r = pl.multiple_of(n * HG_SUB, HG_SUB)
            st = s_ref[...]
            st_ref[0, 0, pl.ds(n, 1)] = st[None]
            st_out, o, a_bf = _hg_block_fwd(st, hq_ref[pl.ds(r, HG_SUB), :].astype(F32),
                                            hf_ref[pl.ds(r, HG_SUB), :].astype(F32),
                                            hi_ref[pl.ds(r, HG_SUB), :].astype(F32), lbv, cs, p_ref)
            s_ref[...] = st_out
            o_ref[pl.ds(r, HG_SUB), :] = o
            a_ref[0, 0, pl.ds(n, 1)] = a_bf[None]
            return carry

        lax.fori_loop(0, spb, sub, 0, unroll=2)

    def colspec(off):
        return pl.BlockSpec((rows_blk, HEAD), functools.partial(lambda h, b, j, off: (b * nb + j, off + h), off=off))

    whole = lambda arr: pl.BlockSpec(arr.shape, lambda h, b, j: (0, 0))
    return pl.pallas_call(
        body, name="hgrn_fwd", grid=(nh, bl, nb),
        in_specs=[colspec(0), colspec(nh), colspec(2 * nh), pl.BlockSpec((1, HEAD), lambda h, b, j: (0, h)),
                  whole(cstack), whole(pstack)],
        out_specs=[pl.BlockSpec((rows_blk, HEAD), lambda h, b, j: (b * nb + j, h)),
                   pl.BlockSpec((1, 1, spb, HEAD, HEAD), lambda h, b, j: (b, h, j, 0, 0)),
                   pl.BlockSpec((1, 1, spb, HG_SUB, HG_SUB), lambda h, b, j: (b, h, j, 0, 0))],
        out_shape=[jax.ShapeDtypeStruct((bl * lp, d), F32),
                   jax.ShapeDtypeStruct((bl, nh, lp // HG_SUB, HEAD, HEAD), F32),
                   jax.ShapeDtypeStruct((bl, nh, lp // HG_SUB, HG_SUB, HG_SUB), BF16)],
        scratch_shapes=[pltpu.VMEM((HEAD, HEAD), F32)],
        compiler_params=pltpu.CompilerParams(dimension_semantics=("arbitrary", "arbitrary", "arbitrary")),
    )(proj_main, proj_main, proj_main, lb, cstack, pstack)


def hgrn_bwd(proj_main, lb, consts, states, a_mats, do_scan, bl, lp, d):
    nh = d // HEAD
    rows_blk = _tile(lp, 768, SEQ_BLOCK)
    nb = lp // rows_blk
    spb = rows_blk // HG_SUB
    cstack, cstack_t, pstack, pstack_t = consts

    def body(hq_ref, hf_ref, hi_ref, lb_ref, c_ref, ct_ref, p_ref, pt_ref, st_ref, a_ref, do_ref,
             dq_ref, df_ref, di_ref, dlb_ref, ds_ref):
        b_id, j = pl.program_id(1), pl.program_id(2)
        blk = nb - 1 - j

        @pl.when(j == 0)
        def _():
            ds_ref[...] = jnp.zeros_like(ds_ref)

        @pl.when((j == 0) & (b_id == 0))
        def _():
            dlb_ref[...] = jnp.zeros_like(dlb_ref)

        lbv = lb_ref[...]
        cs = c_ref[...]
        cst = ct_ref[...]

        def sub(i, carry):
            n = spb - 1 - i
            r = pl.multiple_of(n * HG_SUB, HG_SUB)
            dst, dhq, dhf, dhi, dlb = _hg_block_bwd(
                st_ref[0, 0, pl.ds(n, 1)][0], ds_ref[...], do_ref[pl.ds(r, HG_SUB), :],
                hq_ref[pl.ds(r, HG_SUB), :].astype(F32), hf_ref[pl.ds(r, HG_SUB), :].astype(F32),
                hi_ref[pl.ds(r, HG_SUB), :].astype(F32), lbv,
                a_ref[0, 0, pl.ds(n, 1)][0], cs, cst, p_ref, pt_ref)
            ds_ref[...] = dst
            dq_ref[pl.ds(r, HG_SUB), :] = dhq.astype(dq_ref.dtype)
            df_ref[pl.ds(r, HG_SUB), :] = dhf.astype(df_ref.dtype)
            di_ref[pl.ds(r, HG_SUB), :] = dhi.astype(di_ref.dtype)
            dlb_ref[...] += dlb
            return carry

        lax.fori_loop(0, spb, sub, 0, unroll=2)

    def colspec(off):
        return pl.BlockSpec((rows_blk, HEAD),
                            functools.partial(lambda h, b, j, off: (b * nb + nb - 1 - j, off + h), off=off))

    whole = lambda arr: pl.BlockSpec(arr.shape, lambda h, b, j: (0, 0))
    mats = lambda: pl.BlockSpec((1, 1, spb, HEAD, HEAD), lambda h, b, j: (b, h, nb - 1 - j, 0, 0))
    t_rows = bl * lp
    return pl.pallas_call(
        body, name="hgrn_bwd", grid=(nh, bl, nb),
        in_specs=[colspec(0), colspec(nh), colspec(2 * nh), pl.BlockSpec((1, HEAD), lambda h, b, j: (0, h)),
                  whole(cstack), whole(cstack_t), whole(pstack), whole(pstack_t), mats(), mats(), colspec(0)],
        out_specs=[colspec(0), colspec(0), colspec(0), pl.BlockSpec((1, HEAD), lambda h, b, j: (0, h))],
        out_shape=[jax.ShapeDtypeStruct((t_rows, d), BF16)] * 3 + [jax.ShapeDtypeStruct((1, d), F32)],
        scratch_shapes=[pltpu.VMEM((HEAD, HEAD), F32)],
        compiler_params=pltpu.CompilerParams(dimension_semantics=("arbitrary", "arbitrary", "arbitrary")),
    )(proj_main, proj_main, proj_main, lb, cstack, cstack_t, pstack, pstack_t, states, a_mats, do_scan)


def _allowed(row0, col0, nr, nc, transposed=False):
    if transposed:
        col = col0 + lax.broadcasted_iota(jnp.int32, (nc, 1), 0)
        row = row0 + lax.broadcasted_iota(jnp.int32, (1, nr), 1)
    else:
        row = row0 + lax.broadcasted_iota(jnp.int32, (nr, 1), 0)
        col = col0 + lax.broadcasted_iota(jnp.int32, (1, nc), 1)
    return (col <= row) & ((col >= PAD_FRONT) | (row < PAD_FRONT))


def attn_fwd(q_cat, k_cat, v, bl, lp, nm, scale):
    tq = tk = SEQ_BLOCK
    nq = lp // tq

    def body(q_ref, k_ref, v_ref, o_ref, lse_ref, m_ref, l_ref, acc_ref):
        i = pl.program_id(2)
        q = q_ref[...]
        m_ref[...] = jnp.full_like(m_ref, NEG)
        l_ref[...] = jnp.zeros_like(l_ref)
        acc_ref[...] = jnp.zeros_like(acc_ref)

        def kstep(c, carry):
            c0 = pl.multiple_of(c * tk, tk)
            s = lax.dot_general(q, k_ref[pl.ds(c0, tk), :], NT_DIMS, preferred_element_type=F32) * scale
            s = jnp.where(_allowed(i * tq, c * tk, tq, tk), s, NEG)
            m_old = m_ref[...]
            m_new = jnp.maximum(m_old, jnp.max(s, axis=1, keepdims=True))
            alpha = jnp.exp(m_old - m_new)
            p = jnp.exp(s - m_new)
            l_ref[...] = alpha * l_ref[...] + jnp.sum(p, axis=1, keepdims=True)
            acc_ref[...] = alpha * acc_ref[...] + jnp.dot(_bf(p), v_ref[pl.ds(c0, tk), :],
                                                          preferred_element_type=F32)
            m_ref[...] = m_new
            return carry

        lax.fori_loop(0, i + 1, kstep, 0)
        o_ref[...] = (acc_ref[...] / l_ref[...]).astype(o_ref.dtype)
        lse_ref[0, 0] = m_ref[...] + jnp.log(l_ref[...])

    return pl.pallas_call(
        body, name="attn_fwd", grid=(bl, nm, nq),
        in_specs=[pl.BlockSpec((tq, QK_PAD), lambda b, h, i: (b * nq + i, h)),
                  pl.BlockSpec((lp, QK_PAD), lambda b, h, i: (b, h)),
                  pl.BlockSpec((lp, HEAD), lambda b, h, i: (b, h))],
        out_specs=[pl.BlockSpec((tq, HEAD), lambda b, h, i: (b * nq + i, h)),
                   pl.BlockSpec((1, 1, tq, 1), lambda b, h, i: (b, h, i, 0))],
        out_shape=[jax.ShapeDtypeStruct((bl * lp, nm * HEAD), BF16),
                   jax.ShapeDtypeStruct((bl, nm, lp, 1), F32)],
        scratch_shapes=[pltpu.VMEM((tq, 1), F32), pltpu.VMEM((tq, 1), F32), pltpu.VMEM((tq, HEAD), F32)],
        compiler_params=pltpu.CompilerParams(dimension_semantics=("arbitrary", "arbitrary", "arbitrary")),
    )(q_cat, k_cat, v)


def attn_bwd_dq(q_cat, k_cat, v, o, do, lse, bl, lp, nm, scale):
    tq = tk = SEQ_BLOCK
    nq = lp // tq

    def body(q_ref, k_ref, v_ref, o_ref, do_ref, lse_ref, dq_ref, dl_ref, acc_ref):
        i = pl.program_id(2)
        q = q_ref[...]
        do_b = do_ref[...]
        delta = jnp.sum(o_ref[...].astype(F32) * do_b.astype(F32), axis=1, keepdims=True)
        lse_b = lse_ref[0, 0]
        acc_ref[...] = jnp.zeros_like(acc_ref)

        def kstep(c, carry):
            c0 = pl.multiple_of(c * tk, tk)
            ks = k_ref[pl.ds(c0, tk), :]
            s = lax.dot_general(q, ks, NT_DIMS, preferred_element_type=F32) * scale
            p = jnp.where(_allowed(i * tq, c * tk, tq, tk), jnp.exp(s - lse_b), 0.0)
            dp = lax.dot_general(do_b, v_ref[pl.ds(c0, tk), :], NT_DIMS, preferred_element_type=F32)
            ds = p * (dp - delta)
            acc_ref[...] += jnp.dot(_bf(ds), ks, preferred_element_type=F32)
            return carry

        lax.fori_loop(0, i + 1, kstep, 0)
        dq_ref[...] = acc_ref[...] * scale
        dl_ref[0, 0] = delta

    return pl.pallas_call(
        body, name="attn_bwd_dq", grid=(bl, nm, nq),
        in_specs=[pl.BlockSpec((tq, QK_PAD), lambda b, h, i: (b * nq + i, h)),
                  pl.BlockSpec((lp, QK_PAD), lambda b, h, i: (b, h)),
                  pl.BlockSpec((lp, HEAD), lambda b, h, i: (b, h)),
                  pl.BlockSpec((tq, HEAD), lambda b, h, i: (b * nq + i, h)),
                  pl.BlockSpec((tq, HEAD), lambda b, h, i: (b * nq + i, h)),
                  pl.BlockSpec((1, 1, tq, 1), lambda b, h, i: (b, h, i, 0))],
        out_specs=[pl.BlockSpec((tq, QK_PAD), lambda b, h, i: (b * nq + i, h)),
                   pl.BlockSpec((1, 1, tq, 1), lambda b, h, i: (b, h, i, 0))],
        out_shape=[jax.ShapeDtypeStruct((bl * lp, nm * QK_PAD), F32),
                   jax.ShapeDtypeStruct((bl, nm, lp, 1), F32)],
        scratch_shapes=[pltpu.VMEM((tq, QK_PAD), F32)],
        compiler_params=pltpu.CompilerParams(dimension_semantics=("arbitrary", "arbitrary", "arbitrary")),
    )(q_cat, k_cat, v, o, do, lse)


def attn_bwd_dkv(q_cat, k_cat, v, do, lse_row, delta_row, bl, lp, nm, scale):
    tq = tk = SEQ_BLOCK
    nq = lp // tq

    def body(q_ref, k_ref, v_ref, do_ref, lse_ref, dl_ref, dk_ref, dv_ref):
        i = pl.program_id(2)
        kt = k_ref[...]
        vt = v_ref[...]
        dk_ref[...] = jnp.zeros_like(dk_ref)
        dv_ref[...] = jnp.zeros_like(dv_ref)

        def qstep(c, carry):
            c0 = pl.multiple_of(c * tq, tq)
            qs = q_ref[pl.ds(c0, tq), :]
            dos = do_ref[pl.ds(c0, tq), :]
            st = lax.dot_general(kt, qs, NT_DIMS, preferred_element_type=F32) * scale
            pt = jnp.where(_allowed(c * tq, i * tk, tq, tk, transposed=True),
                           jnp.exp(st - lse_ref[0, 0, pl.ds(c, 1)][0]), 0.0)
            dv_ref[...] += jnp.dot(_bf(pt), dos, preferred_element_type=F32)
            dpt = lax.dot_general(vt, dos, NT_DIMS, preferred_element_type=F32)
            dst = pt * (dpt - dl_ref[0, 0, pl.ds(c, 1)][0])
            dk_ref[...] += jnp.dot(_bf(dst), qs, preferred_element_type=F32)
            return carry

        lax.fori_loop(i, nq, qstep, 0)
        dk_ref[...] = dk_ref[...] * scale

    return pl.pallas_call(
        body, name="attn_bwd_dkv", grid=(bl, nm, nq),
        in_specs=[pl.BlockSpec((lp, QK_PAD), lambda b, h, i: (b, h)),
                  pl.BlockSpec((tk, QK_PAD), lambda b, h, i: (b * nq + i, h)),
                  pl.BlockSpec((tk, HEAD), lambda b, h, i: (b * nq + i, h)),
                  pl.BlockSpec((lp, HEAD), lambda b, h, i: (b, h)),
                  pl.BlockSpec((1, 1, nq, 1, tq), lambda b, h, i: (b, h, 0, 0, 0)),
                  pl.BlockSpec((1, 1, nq, 1, tq), lambda b, h, i: (b, h, 0, 0, 0))],
        out_specs=[pl.BlockSpec((tk, QK_PAD), lambda b, h, i: (b * nq + i, h)),
                   pl.BlockSpec((tk, HEAD), lambda b, h, i: (b * nq + i, h))],
        out_shape=[jax.ShapeDtypeStruct((bl * lp, nm * QK_PAD), F32),
                   jax.ShapeDtypeStruct((bl * lp, nm * HEAD), F32)],
        compiler_params=pltpu.CompilerParams(dimension_semantics=("arbitrary", "arbitrary", "arbitrary")),
    )(q_cat, k_cat, v, do, lse_row, delta_row)


def _key_query_mask(key0, qry0, nk, nq_, causal):
    key = key0 + lax.broadcasted_iota(jnp.int32, (nk, 1), 0)
    if not causal:
        return key >= PAD_FRONT
    qry = qry0 + lax.broadcasted_iota(jnp.int32, (1, nq_), 1)
    return (key <= qry) & (key >= PAD_FRONT)


def _attn_tile(lp):
    return _tile(lp, ATTN_TILE_MAX, SEQ_BLOCK)


def attn_fwd_t(q_cat, k_cat, v_t, bl, lp, nm, scale):
    tq = tk = _attn_tile(lp)
    nq = lp // tq
    hp = ATTN_HEADS_PER_STEP
    assert nm % hp == 0

    def body(q_ref, k_ref, vt_ref, o_ref, lse_ref, m_ref, l_ref, acc_ref):
        i = pl.program_id(2)
        m_ref[...] = jnp.full_like(m_ref, NEG)
        l_ref[...] = jnp.zeros_like(l_ref)
        acc_ref[...] = jnp.zeros_like(acc_ref)

        def step(c, mask):
            c0 = pl.multiple_of(c * tk, tk)
            for hh in range(hp):
                cols = pl.ds(hh * QK_PAD, QK_PAD)
                st = lax.dot_general(k_ref[pl.ds(c0, tk), cols], q_ref[:, cols], NT_DIMS,
                                     preferred_element_type=F32) * scale
                if mask is not None:
                    st = jnp.where(_key_query_mask(c * tk, i * tq, tk, tq, mask == "causal"), st, NEG)
                m_old = m_ref[hh]
                m_new = jnp.maximum(m_old, jnp.max(st, axis=0, keepdims=True))
                alpha = jnp.exp(m_old - m_new)
                pt = jnp.exp(st - m_new)
                l_ref[hh] = alpha * l_ref[hh] + jnp.sum(pt, axis=0, keepdims=True)
                acc_ref[hh] = alpha * acc_ref[hh] + jnp.dot(vt_ref[0, hh, pl.ds(c, 1)][0], _bf(pt),
                                                            preferred_element_type=F32)
                m_ref[hh] = m_new

        def mid(c, carry):
            step(c, None)
            return carry

        @pl.when(i == 0)
        def _():
            step(0, "causal")

        @pl.when(i > 0)
        def _():
            step(0, "pad")
            lax.fori_loop(1, i, mid, 0)
            step(i, "causal")

        for hh in range(hp):
            o_ref[:, pl.ds(hh * HEAD, HEAD)] = jnp.transpose(acc_ref[hh] / l_ref[hh]).astype(o_ref.dtype)
            lse_ref[0, hh, 0] = m_ref[hh] + jnp.log(l_ref[hh])

    return pl.pallas_call(
        body, name="attn_fwd", grid=(bl, nm // hp, nq),
        in_specs=[pl.BlockSpec((tq, hp * QK_PAD), lambda b, h, i: (b * nq + i, h)),
                  pl.BlockSpec((lp, hp * QK_PAD), lambda b, h, i: (b, h)),
                  pl.BlockSpec((1, hp, nq, HEAD, tk), lambda b, h, i: (b, h, 0, 0, 0))],
        out_specs=[pl.BlockSpec((tq, hp * HEAD), lambda b, h, i: (b * nq + i, h)),
                   pl.BlockSpec((1, hp, 1, 1, tq), lambda b, h, i: (b, h, i, 0, 0))],
        out_shape=[jax.ShapeDtypeStruct((bl * lp, nm * HEAD), BF16),
                   jax.ShapeDtypeStruct((bl, nm, nq, 1, tq), F32)],
        scratch_shapes=[pltpu.VMEM((hp, 1, tq), F32), pltpu.VMEM((hp, 1, tq), F32), pltpu.VMEM((hp, HEAD, tq), F32)],
        compiler_params=pltpu.CompilerParams(dimension_semantics=("arbitrary", "arbitrary", "arbitrary")),
    )(q_cat, k_cat, v_t)


def attn_bwd_t(q_cat, k_cat, k_t, v, o, do, lse, bl, lp, nm, scale):
    tq = tk = _attn_tile(lp)
    nq = lp // tq
    hp = ATTN_HEADS_PER_STEP
    assert nm % hp == 0

    def body(q_ref, k_ref, kt_ref, v_ref, o_ref, do_ref, lse_ref, dq_ref, dk_ref, dv_ref, dqt_ref, dka_ref, dva_ref):
        i = pl.program_id(2)

        @pl.when(i == 0)
        def _():
            dqt_ref[...] = jnp.zeros_like(dqt_ref)

        dka_ref[...] = jnp.zeros_like(dka_ref)
        dva_ref[...] = jnp.zeros_like(dva_ref)
        ones8 = jnp.ones((8, HEAD), BF16)

        def step(c, mask):
            c0 = pl.multiple_of(c * tq, tq)
            for hh in range(hp):
                qcols, vcols = pl.ds(hh * QK_PAD, QK_PAD), pl.ds(hh * HEAD, HEAD)
                qs = q_ref[pl.ds(c0, tq), qcols]
                dos = do_ref[pl.ds(c0, tq), vcols]
                prod = dos.astype(F32) * o_ref[pl.ds(c0, tq), vcols].astype(F32)
                hi = _bf(prod)
                lo = _bf(prod - hi.astype(F32))
                delta8 = (lax.dot_general(ones8, hi, NT_DIMS, preferred_element_type=F32)
                          + lax.dot_general(ones8, lo, NT_DIMS, preferred_element_type=F32))
                st = lax.dot_general(k_ref[:, qcols], qs, NT_DIMS, preferred_element_type=F32) * scale
                pt = jnp.exp(st - lse_ref[0, hh, pl.ds(c, 1)][0])
                if mask is not None:
                    pt = jnp.where(_key_query_mask(i * tk, c * tq, tk, tq, mask == "causal"), pt, 0.0)
                dva_ref[hh] += jnp.dot(_bf(pt), dos, preferred_element_type=F32)
                dpt = lax.dot_general(v_ref[:, vcols], dos, NT_DIMS, preferred_element_type=F32)
                dst = _bf(pt * (dpt - jnp.tile(delta8, (tk // 8, 1))))
                dka_ref[hh] += jnp.dot(dst, qs, preferred_element_type=F32)
                dqt_ref[hh, pl.ds(c, 1)] += jnp.dot(kt_ref[0, hh, 0], dst, preferred_element_type=F32)[None]

        step(i, "causal")

        def rest_masked(c, carry):
            step(c, "pad")
            return carry

        def rest(c, carry):
            step(c, None)
            return carry

        @pl.when(i == 0)
        def _():
            lax.fori_loop(1, nq, rest_masked, 0)

        @pl.when(i > 0)
        def _():
            lax.fori_loop(i + 1, nq, rest, 0)

        for hh in range(hp):
            dk_ref[:, pl.ds(hh * QK_PAD, QK_PAD)] = (dka_ref[hh] * scale).astype(dk_ref.dtype)
            dv_ref[:, pl.ds(hh * HEAD, HEAD)] = dva_ref[hh].astype(dv_ref.dtype)

        @pl.when(i == nq - 1)
        def _():
            for hh in range(hp):
                for c in range(nq):
                    dq_ref[pl.ds(c * tq, tq), pl.ds(hh * QK_PAD, QK_PAD)] = (
                        jnp.transpose(dqt_ref[hh, c]) * scale).astype(dq_ref.dtype)

    return pl.pallas_call(
        body, name="attn_bwd", grid=(bl, nm // hp, nq),
        in_specs=[pl.BlockSpec((lp, hp * QK_PAD), lambda b, h, i: (b, h)),
                  pl.BlockSpec((tk, hp * QK_PAD), lambda b, h, i: (b * nq + i, h)),
                  pl.BlockSpec((1, hp, 1, QK_PAD, tk), lambda b, h, i: (b, h, i, 0, 0)),
                  pl.BlockSpec((tk, hp * HEAD), lambda b, h, i: (b * nq + i, h)),
                  pl.BlockSpec((lp, hp * HEAD), lambda b, h, i: (b, h)),
                  pl.BlockSpec((lp, hp * HEAD), lambda b, h, i: (b, h)),
                  pl.BlockSpec((1, hp, nq, 1, tq), lambda b, h, i: (b, h, 0, 0, 0))],
        out_specs=[pl.BlockSpec((lp, hp * QK_PAD), lambda b, h, i: (b, h)),
                   pl.BlockSpec((tk, hp * QK_PAD), lambda b, h, i: (b * nq + i, h)),
                   pl.BlockSpec((tk, hp * HEAD), lambda b, h, i: (b * nq + i, h))],
        out_shape=[jax.ShapeDtypeStruct((bl * lp, nm * QK_PAD), BF16),
                   jax.ShapeDtypeStruct((bl * lp, nm * QK_PAD), BF16),
                   jax.ShapeDtypeStruct((bl * lp, nm * HEAD), BF16)],
        scratch_shapes=[pltpu.VMEM((hp, nq, QK_PAD, tq), F32), pltpu.VMEM((hp, tk, QK_PAD), F32),
                        pltpu.VMEM((hp, tk, HEAD), F32)],
        compiler_params=pltpu.CompilerParams(dimension_semantics=("arbitrary", "arbitrary", "arbitrary")),
    )(q_cat, k_cat, k_t, v, o, do, lse)


def _place():
    return lax.axis_index("x"), lax.axis_index("y"), lax.axis_index("c")


def gather_shards(packed):
    hbm = pl.BlockSpec(memory_space=pl.ANY)

    def body(src_ref, out_ref, send_sems, recv_sems, local_sem):
        x, y, c = _place()
        me = 2 * x + y
        chips = [(1 - x, y), (x, 1 - y), (1 - x, 1 - y)]
        local = pltpu.make_async_copy(src_ref, out_ref.at[me], local_sem)
        local.start()
        sends = []
        for k, (px, py) in enumerate(chips):
            cp = pltpu.make_async_remote_copy(src_ref=src_ref, dst_ref=out_ref.at[me], send_sem=send_sems.at[k],
                                              recv_sem=recv_sems.at[k], device_id=(px, py, c), device_id_type=MESH)
            cp.start()
            sends.append(cp)
        for k, (px, py) in enumerate(chips):
            pltpu.make_async_remote_copy(src_ref=src_ref, dst_ref=out_ref.at[2 * px + py], send_sem=send_sems.at[k],
                                         recv_sem=recv_sems.at[k], device_id=(px, py, c),
                                         device_id_type=MESH).wait_recv()
        for cp in sends:
            cp.wait_send()
        local.wait()

    return pl.pallas_call(
        body, name="gather_shards", in_specs=[hbm], out_specs=hbm,
        out_shape=jax.ShapeDtypeStruct((4,) + packed.shape, packed.dtype),
        scratch_shapes=[pltpu.SemaphoreType.DMA((3,)), pltpu.SemaphoreType.DMA((3,)), pltpu.SemaphoreType.DMA],
    )(packed)


def exchange_grads(send, small):
    hbm = pl.BlockSpec(memory_space=pl.ANY)

    def body(send_ref, small_ref, recv_ref, all_ref, send_sems, recv_sems, ssend_sems, srecv_sems, local_sem):
        x, y, c = _place()
        me = 4 * x + 2 * y + c
        chips = [(1 - x, y), (x, 1 - y), (1 - x, 1 - y)]
        local = pltpu.make_async_copy(small_ref, all_ref.at[me], local_sem)
        local.start()
        sends = []
        for k, (px, py) in enumerate(chips):
            cp = pltpu.make_async_remote_copy(src_ref=send_ref.at[2 * px + py], dst_ref=recv_ref.at[k],
                                              send_sem=send_sems.at[k], recv_sem=recv_sems.at[k],
                                              device_id=(px, py, c), device_id_type=MESH)
            cp.start()
            sends.append(cp)
        others = [(x ^ ((r >> 2) & 1), y ^ ((r >> 1) & 1), c ^ (r & 1)) for r in range(1, 8)]
        for r, peer in enumerate(others):
            cp = pltpu.make_async_remote_copy(src_ref=small_ref, dst_ref=all_ref.at[me], send_sem=ssend_sems.at[r],
                                              recv_sem=srecv_sems.at[r], device_id=peer, device_id_type=MESH)
            cp.start()
            sends.append(cp)
        for k, (px, py) in enumerate(chips):
            pltpu.make_async_remote_copy(src_ref=send_ref.at[2 * px + py], dst_ref=recv_ref.at[k],
                                         send_sem=send_sems.at[k], recv_sem=recv_sems.at[k],
                                         device_id=(px, py, c), device_id_type=MESH).wait_recv()
        for r, (px, py, pc) in enumerate(others):
            pltpu.make_async_remote_copy(src_ref=small_ref, dst_ref=all_ref.at[4 * px + 2 * py + pc],
                                         send_sem=ssend_sems.at[r], recv_sem=srecv_sems.at[r],
                                         device_id=(px, py, pc), device_id_type=MESH).wait_recv()
        for cp in sends:
            cp.wait_send()
        local.wait()

    return pl.pallas_call(
        body, name="exchange_grads", in_specs=[hbm, hbm], out_specs=[hbm, hbm],
        out_shape=[jax.ShapeDtypeStruct((3,) + send.shape[1:], send.dtype),
                   jax.ShapeDtypeStruct((8,) + small.shape, small.dtype)],
        scratch_shapes=[pltpu.SemaphoreType.DMA((3,)), pltpu.SemaphoreType.DMA((3,)),
                        pltpu.SemaphoreType.DMA((7,)), pltpu.SemaphoreType.DMA((7,)), pltpu.SemaphoreType.DMA],
    )(send, small)


def swap_with_sibling(part):
    hbm = pl.BlockSpec(memory_space=pl.ANY)

    def body(src_ref, out_ref, send_sem, recv_sem):
        x, y, c = _place()
        cp = pltpu.make_async_remote_copy(src_ref=src_ref, dst_ref=out_ref, send_sem=send_sem, recv_sem=recv_sem,
                                          device_id=(x, y, 1 - c), device_id_type=MESH)
        cp.start()
        cp.wait()

    return pl.pallas_call(
        body, name="swap_with_sibling", in_specs=[hbm], out_specs=hbm,
        out_shape=jax.ShapeDtypeStruct(part.shape, part.dtype),
        scratch_shapes=[pltpu.SemaphoreType.DMA, pltpu.SemaphoreType.DMA],
    )(part)


def _chips3():
    x, y, c = _place()
    return [(1 - x, y, c), (x, 1 - y, c), (1 - x, 1 - y, c)]


def push_start(name, src, per_chip):
    hbm = pl.BlockSpec(memory_space=pltpu.HBM)
    sem = pl.BlockSpec(memory_space=pltpu.SEMAPHORE)
    rows = src.shape[-2:]

    def body(src_ref, land_ref, send_sems, recv_sems, src_thru, land_thru, token):
        for k, (px, py, pc) in enumerate(_chips3()):
            part = src_ref.at[2 * px + py] if per_chip else src_ref
            pltpu.make_async_remote_copy(src_ref=part, dst_ref=land_ref.at[k], send_sem=send_sems.at[k],
                                         recv_sem=recv_sems.at[k], device_id=(px, py, pc),
                                         device_id_type=MESH).start()
        token[...] = jnp.zeros_like(token)

    send_sems, recv_sems, src_thru, land_thru, token = pl.pallas_call(
        body, name=name,
        out_shape=(pltpu.SemaphoreType.DMA((3,)), pltpu.SemaphoreType.DMA((3,)), pltpu.HBM(src.shape, src.dtype),
                   pltpu.HBM((3,) + rows, src.dtype), jax.ShapeDtypeStruct((8, HEAD), F32)),
        in_specs=(hbm, hbm), out_specs=(sem, sem, hbm, hbm, pl.BlockSpec(memory_space=pltpu.VMEM)),
        input_output_aliases={0: 2, 1: 3},
        compiler_params=pltpu.CompilerParams(has_side_effects=pltpu.SideEffectType.DATAFLOW_SIDE_EFFECTING),
    )(pltpu.with_memory_space_constraint(src, pltpu.HBM),
      pltpu.with_memory_space_constraint(lax.empty((3,) + rows, src.dtype), pltpu.HBM))
    return (send_sems, recv_sems, src_thru, land_thru), token


def push_wait(name, handle, after, per_chip):
    send_sems, recv_sems, src_thru, land_thru = handle
    hbm = pl.BlockSpec(memory_space=pltpu.HBM)
    sem = pl.BlockSpec(memory_space=pltpu.SEMAPHORE)

    def body(src_ref, land_ref, send_sems_ref, recv_sems_ref, after_ref, src_dead, got_ref):
        for k, (px, py, pc) in enumerate(_chips3()):
            part = src_ref.at[2 * px + py] if per_chip else src_ref
            cp = pltpu.make_async_remote_copy(src_ref=part, dst_ref=land_ref.at[k], send_sem=send_sems_ref.at[k],
                                              recv_sem=recv_sems_ref.at[k], device_id=(px, py, pc),
                                              device_id_type=MESH)
            cp.wait_send()
            cp.wait_recv()

    return pl.pallas_call(
        body, name=name,
        out_shape=(pltpu.HBM(src_thru.shape, src_thru.dtype), pltpu.HBM(land_thru.shape, land_thru.dtype)),
        in_specs=(hbm, hbm, sem, sem, pl.BlockSpec(memory_space=pl.ANY)), out_specs=(hbm, hbm),
        input_output_aliases={0: 0, 1: 1},
        compiler_params=pltpu.CompilerParams(has_side_effects=pltpu.SideEffectType.DATAFLOW_SIDE_EFFECTING),
    )(src_thru, land_thru, send_sems, recv_sems, after)


def by_chip(own, landed, my_chip):
    by_rel = jnp.stack([own, landed[1], landed[0], landed[2]])
    return [lax.dynamic_index_in_dim(by_rel, jnp.bitwise_xor(s, my_chip), axis=0, keepdims=False) for s in range(4)]


def adamw(name, w, g, m, v):
    r, c = w.shape
    tr = r if r * c <= 65536 else _tile(r, 128, 8)

    def body(w_ref, g_ref, m_ref, v_ref, d_ref, nm_ref, nv_ref):
        gv = g_ref[...]
        m_new = ADAM_B1 * m_ref[...] + (1.0 - ADAM_B1) * gv
        v_new = ADAM_B2 * v_ref[...] + (1.0 - ADAM_B2) * (gv * gv)
        m_hat = m_new / (1.0 - ADAM_B1 ** ADAM_STEP)
        v_hat = v_new / (1.0 - ADAM_B2 ** ADAM_STEP)
        d_ref[...] = -ADAM_LR * (m_hat / (jnp.sqrt(v_hat) + ADAM_EPS) + ADAM_WD * w_ref[...])
        nm_ref[...] = m_new
        nv_ref[...] = v_new

    spec = pl.BlockSpec((tr, c), lambda i: (i, 0))
    return pl.pallas_call(
        body, name=name, grid=(r // tr,), in_specs=[spec] * 4, out_specs=[spec] * 3,
        out_shape=[jax.ShapeDtypeStruct((r, c), F32)] * 3,
        compiler_params=pltpu.CompilerParams(dimension_semantics=("arbitrary",)),
    )(w, g, m, v)


def _pack_rows(shapes):
    rows = [(r * c) // PACK_W for (r, c) in shapes]
    for (r, c) in shapes:
        assert (r * c) % PACK_W == 0
    total = sum(rows)
    return rows, -(-total // 16) * 16


def pack_shard(parts, total_rows):
    flat = jnp.concatenate([p.reshape(-1, PACK_W) for p in parts], axis=0)
    return jnp.pad(flat, ((0, total_rows - flat.shape[0]), (0, 0)))


def split_full(name, full, s):
    if name in COL_SHARDED:
        c = full.shape[1] // 4
        return full[:, s * c:(s + 1) * c]
    r = full.shape[0] // 4
    return full[s * r:(s + 1) * r]


def join_shards(name, shards):
    return jnp.concatenate(shards, axis=1 if name in COL_SHARDED else 0)


def kernel(x, meta_tokens, w_in, b_gate, lb_logits, hg_norm_g, w_hg_o, q_a_norm_g, w_q_b, kv_a_norm_g, w_kv_b, w_mla_o, w_out, mix_pre_g, mix_post_g, ffn_pre_g, ffn_post_g, w_ffn_in, w_ffn_out, loss_target, m_meta_tokens, m_w_in, m_b_gate, m_lb_logits, m_hg_norm_g, m_w_hg_o, m_q_a_norm_g, m_w_q_b, m_kv_a_norm_g, m_w_kv_b, m_w_mla_o, m_w_out, m_mix_pre_g, m_mix_post_g, m_ffn_pre_g, m_ffn_post_g, m_w_ffn_in, m_w_ffn_out, v_meta_tokens, v_w_in, v_b_gate, v_lb_logits, v_hg_norm_g, v_w_hg_o, v_q_a_norm_g, v_w_q_b, v_kv_a_norm_g, v_w_kv_b, v_w_mla_o, v_w_out, v_mix_pre_g, v_mix_post_g, v_ffn_pre_g, v_ffn_post_g, v_w_ffn_in, v_w_ffn_out):
    wts = dict(meta_tokens=meta_tokens, w_in=w_in[0], b_gate=b_gate, lb_logits=lb_logits, hg_norm_g=hg_norm_g,
               w_hg_o=w_hg_o[0], q_a_norm_g=q_a_norm_g, w_q_b=w_q_b[0], kv_a_norm_g=kv_a_norm_g, w_kv_b=w_kv_b[0],
               w_mla_o=w_mla_o[0], w_out=w_out[0], mix_pre_g=mix_pre_g, mix_post_g=mix_post_g, ffn_pre_g=ffn_pre_g,
               ffn_post_g=ffn_post_g, w_ffn_in=w_ffn_in[0], w_ffn_out=w_ffn_out[0])
    mom_m = dict(meta_tokens=m_meta_tokens, w_in=m_w_in[0], b_gate=m_b_gate, lb_logits=m_lb_logits,
                 hg_norm_g=m_hg_norm_g, w_hg_o=m_w_hg_o[0], q_a_norm_g=m_q_a_norm_g, w_q_b=m_w_q_b[0],
                 kv_a_norm_g=m_kv_a_norm_g, w_kv_b=m_w_kv_b[0], w_mla_o=m_w_mla_o[0], w_out=m_w_out[0],
                 mix_pre_g=m_mix_pre_g, mix_post_g=m_mix_post_g, ffn_pre_g=m_ffn_pre_g, ffn_post_g=m_ffn_post_g,
                 w_ffn_in=m_w_ffn_in[0], w_ffn_out=m_w_ffn_out[0])
    mom_v = dict(meta_tokens=v_meta_tokens, w_in=v_w_in[0], b_gate=v_b_gate, lb_logits=v_lb_logits,
                 hg_norm_g=v_hg_norm_g, w_hg_o=v_w_hg_o[0], q_a_norm_g=v_q_a_norm_g, w_q_b=v_w_q_b[0],
                 kv_a_norm_g=v_kv_a_norm_g, w_kv_b=v_w_kv_b[0], w_mla_o=v_w_mla_o[0], w_out=v_w_out[0],
                 mix_pre_g=v_mix_pre_g, mix_post_g=v_mix_post_g, ffn_pre_g=v_ffn_pre_g, ffn_post_g=v_ffn_post_g,
                 w_ffn_in=v_w_ffn_in[0], w_ffn_out=v_w_ffn_out[0])

    bl, seq, d = x.shape
    lp = PAD_FRONT + N_META + seq
    t_rows = bl * lp
    nh = d // HEAD
    ql, kvl = wts["w_q_b"].shape[0], wts["w_kv_b"].shape[0]
    nm = (4 * wts["w_mla_o"].shape[0]) // HEAD
    ffn = 4 * wts["w_ffn_out"].shape[0]
    mla_w = ql + kvl + HEAD
    assert ql == kvl and ql % HEAD == 0 and seq % SEQ_BLOCK == 0 and d % HEAD == 0
    scale = (HEAD + ROPE) ** -0.5
    my_chip = 2 * lax.axis_index("x") + lax.axis_index("y")

    mcols = meta_tokens.shape[1]
    meta_all = gather_shards(meta_tokens)
    meta_full = jnp.concatenate([meta_all[s] for s in range(4)], axis=1)

    def start_gather(name, names, order_after):
        shapes = [wts[n].shape for n in names]
        rows, total = _pack_rows(shapes)
        own = pack_shard([_bf(wts[n]) for n in names], total)
        if order_after is not None:
            own = own + order_after[0, 0].astype(BF16)
        handle, token = push_start(name, own, per_chip=False)
        return handle, token, rows, shapes

    def finish_gather(name, names, started, after):
        handle, _, rows, shapes = started
        own, landed = push_wait(name, handle, after, per_chip=False)
        slots = by_chip(own, landed, my_chip)
        out, off = {}, 0
        for n, nrows, (r, c) in zip(names, rows, shapes):
            out[n] = join_shards(n, [slots[s][off:off + nrows].reshape(r, c) for s in range(4)])
            off += nrows
        return out

    rest_names = tuple(n for n in BIG if n != "w_in")
    gather_1 = start_gather("gather_w_in_start", ("w_in",), None)
    gather_2 = start_gather("gather_rest_start", rest_names, gather_1[1])

    h0 = jnp.concatenate([jnp.zeros((bl, PAD_FRONT, d), F32), jnp.broadcast_to(meta_full[None], (bl, N_META, d)), x],
                         axis=1).reshape(t_rows, d)
    tgt = jnp.concatenate([jnp.zeros((bl, PAD_FRONT + N_META, d), F32), loss_target], axis=1).reshape(t_rows, d)
    pos = (jnp.arange(lp, dtype=jnp.int32) - PAD_FRONT).astype(F32)
    inv_freq = 1.0 / (ROPE_THETA ** (jnp.arange(0, ROPE, 2, dtype=F32) / ROPE))
    ang = pos[:, None] * inv_freq[None, :]
    zeros32 = jnp.zeros((lp, ROPE_HALF), F32)
    zeros64 = jnp.zeros((lp, HEAD - ROPE), F32)
    t_cos = jnp.concatenate([jnp.cos(ang), jnp.cos(ang), zeros64], axis=1)
    t_up = jnp.concatenate([zeros32, jnp.sin(ang), zeros64], axis=1)
    t_dn = jnp.concatenate([-jnp.sin(ang), zeros32, zeros64], axis=1)
    real = jnp.broadcast_to((jnp.arange(lp) >= PAD_FRONT + N_META).astype(F32)[:, None], (lp, d))
    lb_soft = jax.nn.softmax(lb_logits.astype(F32), axis=0)
    lb = lb_soft[0:1]

    (u1,) = rowwise("norm_mix_pre", lambda h, g: _rms(h, g), [(h0, d, 0)], [], [mix_pre_g + gather_2[1][0, 0]],
                    [(d, BF16)])
    full = finish_gather("gather_w_in_wait", ("w_in",), gather_1, u1)
    w_main = jnp.concatenate([full["w_in"][:, :4 * d], full["w_in"][:, -2 * d:]], axis=1)
    w_mla = jnp.pad(full["w_in"][:, 4 * d:4 * d + ql + kvl + ROPE], ((0, 0), (0, HEAD - ROPE)))
    proj_main = matmul("proj_main", u1, w_main, "nn", out_dtype=BF16)
    proj_mla = matmul("proj_mla", u1, w_mla, "nn", out_dtype=BF16)
    hg_consts = _hg_constants()
    o_scan, states, a_mats = hgrn_fwd(proj_main, lb, hg_consts, bl, lp, d)

    def hg_out_fn(o, hg, g):
        return jnp.concatenate([_rms(o[:, h * HEAD:(h + 1) * HEAD], g) for h in range(nh)], axis=1) * _silu(hg)

    (o_hg,) = rowwise("hgrn_out", hg_out_fn, [(o_scan, d, 0), (proj_main, d, 3)], [], [hg_norm_g], [(d, BF16)])
    full.update(finish_gather("gather_rest_wait", rest_names, gather_2, o_hg))
    w_qb = jnp.pad(full["w_q_b"].reshape(ql, nm, HEAD + ROPE), ((0, 0), (0, 0), (0, QK_PAD - HEAD - ROPE))
                   ).reshape(ql, nm * QK_PAD)
    w_kvb = full["w_kv_b"]
    y_a = matmul("y_a", o_hg, _bf(full["w_hg_o"]), "nn", out_dtype=BF16)

    qn, kvn = rowwise("mla_norms", lambda cq, ckv, gq, gk: (_rms(cq, gq), _rms(ckv, gk)),
                      [(proj_mla, ql, 0), (proj_mla, kvl, 1)], [], [q_a_norm_g, kv_a_norm_g],
                      [(ql, BF16), (kvl, BF16)])
    q_full = matmul("q_up", qn, w_qb, "nn", out_dtype=BF16)
    kv_full = matmul("kv_up", kvn, w_kvb, "nn", out_dtype=BF16)

    def mla_prep_fn(qf, kvf, kpe, cos, s_up, s_dn):
        kpe_r = _rope(kpe, cos, s_up, s_dn)
        qs, ks, vs = [], [], []
        for h in range(nm):
            qs += [qf[:, h * QK_PAD:h * QK_PAD + HEAD], _rope(qf[:, h * QK_PAD + HEAD:(h + 1) * QK_PAD], cos, s_up, s_dn)]
            ks += [kvf[:, h * QK_PAD:h * QK_PAD + HEAD], kpe_r]
            vs += [kvf[:, h * QK_PAD + HEAD:(h + 1) * QK_PAD]]
        return jnp.concatenate(qs, axis=1), jnp.concatenate(ks, axis=1), jnp.concatenate(vs, axis=1)

    kpe_blk = (ql + kvl) // HEAD
    q_cat, k_cat, v_att = rowwise("mla_prep", mla_prep_fn,
                                  [(q_full, nm * QK_PAD, 0), (kv_full, nm * QK_PAD, 0), (proj_mla, HEAD, kpe_blk)],
                                  [t_cos, t_up, t_dn], [], [(nm * QK_PAD, BF16), (nm * QK_PAD, BF16), (nm * HEAD, BF16)])
    at = _attn_tile(lp)
    v_t = v_att.reshape(bl, lp // at, at, nm, HEAD).transpose(0, 3, 1, 4, 2)
    k_t = k_cat.reshape(bl, lp // at, at, nm, QK_PAD).transpose(0, 3, 1, 4, 2)
    o_mla, lse = attn_fwd_t(q_cat, k_cat, v_t, bl, lp, nm, scale)
    y_b = matmul("y_b", o_mla, _bf(full["w_mla_o"]), "nn", out_dtype=BF16)

    def gate_fn(ya, yb, ga, gb, bias):
        return _sigmoid(ga + bias[:, :d]) * ya + _sigmoid(gb + bias[:, d:]) * yb

    (z,) = rowwise("gate_mix", gate_fn, [(y_a, d, 0), (y_b, d, 0), (proj_main, d, 4), (proj_main, d, 5)], [],
                   [b_gate], [(d, BF16)])
    mixed = matmul("mixed", z, _bf(full["w_out"]), "nn")

    def mid_fn(h, mx, g_post, g_pre):
        h1 = h + _rms(mx, g_post)
        return h1, _rms(h1, g_pre)

    h1, u2 = rowwise("norm_mid", mid_fn, [(h0, d, 0), (mixed, d, 0)], [], [mix_post_g, ffn_pre_g],
                     [(d, F32), (d, BF16)])
    gu = matmul("ffn_in", u2, _bf(full["w_ffn_in"]), "nn", out_dtype=BF16)
    (act,) = rowwise("swiglu", lambda gt, up: _silu(gt) * up, [(gu, ffn, 0), (gu, ffn, 1)], [], [], [(ffn, BF16)])
    f_out = matmul("ffn_out", act, _bf(full["w_ffn_out"]), "nn")

    def loss_fn(h1v, fv, tg, realv, g_post):
        h2 = h1v + _rms(fv, g_post)
        diff = (h2 - tg) * realv
        part = jnp.broadcast_to(0.5 * jnp.sum(diff * diff, keepdims=True) / d, (1, HEAD))
        dy = diff / d
        df, dg = _rms_bwd(fv, g_post, dy)
        return dy, df, part, dg

    dy, df, loss_part, g_ffn_post = rowwise("loss_head", loss_fn, [(h1, d, 0), (f_out, d, 0), (tgt, d, 0)], [real],
                                            [ffn_post_g], [(d, F32), (d, BF16)], [(1, HEAD), (1, d)])
    grads = {}
    d_act = matmul("d_act", df, _bf(full["w_ffn_out"]), "nt", out_dtype=BF16)
    grads["w_ffn_out"] = matmul("gw_ffn_out", act, df, "tn")

    def swiglu_bwd_fn(gt, up, da):
        return jnp.concatenate([da * up * _silu_grad(gt), da * _silu(gt)], axis=1)

    (dgu,) = rowwise("swiglu_bwd", swiglu_bwd_fn, [(gu, ffn, 0), (gu, ffn, 1), (d_act, ffn, 0)], [], [],
                     [(2 * ffn, BF16)])
    du2 = matmul("d_u2", dgu, _bf(full["w_ffn_in"]), "nt", out_dtype=BF16)
    grads["w_ffn_in"] = matmul("gw_ffn_in", u2, dgu, "tn")

    def mid_bwd_fn(dyv, h1v, du2v, mx, g_pre, g_post):
        dx, dg_pre = _rms_bwd(h1v, g_pre, du2v)
        dh1 = dyv + dx
        dmx, dg_post = _rms_bwd(mx, g_post, dh1)
        return dh1, dmx, dg_pre, dg_post

    dh1, dmixed, g_ffn_pre, g_mix_post = rowwise("norm_mid_bwd", mid_bwd_fn,
                                                 [(dy, d, 0), (h1, d, 0), (du2, d, 0), (mixed, d, 0)], [],
                                                 [ffn_pre_g, mix_post_g], [(d, F32), (d, BF16)], [(1, d), (1, d)])
    dz = matmul("d_z", dmixed, _bf(full["w_out"]), "nt", out_dtype=BF16)
    grads["w_out"] = matmul("gw_out", z, dmixed, "tn")

    def gate_bwd_fn(dzv, ya, yb, ga, gb, bias):
        sa, sb = _sigmoid(ga + bias[:, :d]), _sigmoid(gb + bias[:, d:])
        dga = dzv * ya * sa * (1.0 - sa)
        dgb = dzv * yb * sb * (1.0 - sb)
        dgates = jnp.concatenate([dga, dgb], axis=1)
        return dzv * sa, dzv * sb, dgates, jnp.sum(dgates, axis=0, keepdims=True)

    dy_a, dy_b, dgates, g_b_gate = rowwise("gate_mix_bwd", gate_bwd_fn,
                                           [(dz, d, 0), (y_a, d, 0), (y_b, d, 0), (proj_main, d, 4), (proj_main, d, 5)],
                                           [], [b_gate], [(d, BF16), (d, BF16), (2 * d, BF16)], [(1, 2 * d)])
    do_hg = matmul("d_o_hg", dy_a, _bf(full["w_hg_o"]), "nt", out_dtype=BF16)
    grads["w_hg_o"] = matmul("gw_hg_o", o_hg, dy_a, "tn")
    do_mla = matmul("d_o_mla", dy_b, _bf(full["w_mla_o"]), "nt", out_dtype=BF16)
    grads["w_mla_o"] = matmul("gw_mla_o", o_mla, dy_b, "tn")

    early = ("w_hg_o", "w_mla_o", "w_out", "w_ffn_in", "w_ffn_out")
    late = ("w_in", "w_q_b", "w_kv_b")

    def pack_grads(names):
        shapes = [wts[n].shape for n in names]
        rows, total = _pack_rows(shapes)
        send = jnp.stack([pack_shard([split_full(n, grads[n], s) for n in names], total) for s in range(4)])
        return send, lax.dynamic_index_in_dim(send, my_chip, axis=0, keepdims=False), rows, shapes, total

    send_a, mine_a, rows_a, shapes_a, total_a = pack_grads(early)
    grads_a, token_a = push_start("grads_early_start", _bf(send_a), per_chip=True)

    def hg_out_bwd_fn(do, o, hg, g):
        sg = _silu(hg)
        dn = do * sg
        dos, dgs, ons = [], 0.0, []
        for h in range(nh):
            sl = slice(h * HEAD, (h + 1) * HEAD)
            dx, dg = _rms_bwd(o[:, sl], g, dn[:, sl])
            dos.append(dx)
            dgs = dgs + dg
            ons.append(_rms(o[:, sl], g))
        dhg = do * jnp.concatenate(ons, axis=1) * _silu_grad(hg)
        return jnp.concatenate(dos, axis=1), dhg, dgs

    do_scan, dhg, g_hg_norm = rowwise("hgrn_out_bwd", hg_out_bwd_fn, [(do_hg, d, 0), (o_scan, d, 0), (proj_main, d, 3)],
                                      [], [hg_norm_g], [(d, F32), (d, BF16)], [(1, HEAD)])
    dhq, dhf, dhi, g_lb = hgrn_bwd(proj_main, lb + token_a[0, 0], hg_consts, states, a_mats, do_scan, bl, lp, d)

    dq_cat, dk_cat, dv_att = attn_bwd_t(q_cat, k_cat, k_t, v_att, o_mla, do_mla, lse, bl, lp, nm, scale)

    def mla_prep_bwd_fn(dqc, dkc, dvv, cos, s_up, s_dn):
        dqs, dkvs, dkpe = [], [], 0.0
        for h in range(nm):
            dqs += [dqc[:, h * QK_PAD:h * QK_PAD + HEAD],
                    _rope_bwd(dqc[:, h * QK_PAD + HEAD:(h + 1) * QK_PAD], cos, s_up, s_dn)]
            dkvs += [dkc[:, h * QK_PAD:h * QK_PAD + HEAD], dvv[:, h * HEAD:(h + 1) * HEAD]]
            dkpe = dkpe + dkc[:, h * QK_PAD + HEAD:(h + 1) * QK_PAD]
        return jnp.concatenate(dqs, axis=1), jnp.concatenate(dkvs, axis=1), _rope_bwd(dkpe, cos, s_up, s_dn)

    dq_full, dkv_full, dkpe = rowwise("mla_prep_bwd", mla_prep_bwd_fn,
                                      [(dq_cat, nm * QK_PAD, 0), (dk_cat, nm * QK_PAD, 0), (dv_att, nm * HEAD, 0)],
                                      [t_cos, t_up, t_dn], [],
                                      [(nm * QK_PAD, BF16), (nm * QK_PAD, BF16), (HEAD, F32)])
    dqn = matmul("d_qn", dq_full, w_qb, "nt", out_dtype=BF16)
    g_wqb = matmul("gw_q_b", qn, dq_full, "tn")
    grads["w_q_b"] = g_wqb.reshape(ql, nm, QK_PAD)[:, :, :HEAD + ROPE].reshape(ql, nm * (HEAD + ROPE))
    dkvn = matmul("d_kvn", dkv_full, w_kvb, "nt", out_dtype=BF16)
    grads["w_kv_b"] = matmul("gw_kv_b", kvn, dkv_full, "tn")

    def mla_norms_bwd_fn(dqnv, dkvnv, cq, ckv, dkpev, gq, gk):
        dcq, dgq = _rms_bwd(cq, gq, dqnv)
        dckv, dgk = _rms_bwd(ckv, gk, dkvnv)
        return jnp.concatenate([dcq, dckv, dkpev], axis=1), dgq, dgk

    dmla, g_q_norm, g_kv_norm = rowwise("mla_norms_bwd", mla_norms_bwd_fn,
                                        [(dqn, ql, 0), (dkvn, kvl, 0), (proj_mla, ql, 0), (proj_mla, kvl, 1),
                                         (dkpe, HEAD, 0)], [], [q_a_norm_g, kv_a_norm_g],
                                        [(mla_w, BF16)], [(1, ql), (1, kvl)])

    w_main_bf = w_main
    pieces = [(dhq, w_main_bf[:, 0:d]), (dhf, w_main_bf[:, d:2 * d]), (dhi, w_main_bf[:, 2 * d:3 * d]),
              (dhg, w_main_bf[:, 3 * d:4 * d]), (dgates, w_main_bf[:, 4 * d:6 * d]), (dmla, w_mla)]
    du1 = None
    gw_parts = []
    for k, (dp, wp) in enumerate(pieces):
        du1 = matmul(f"d_u1_{k}", dp, wp, "nt", addend=du1)
        gw_parts.append(matmul(f"gw_in_{k}", u1, dp, "tn"))
    grads["w_in"] = jnp.concatenate(gw_parts[:4] + [gw_parts[5][:, :ql + kvl + ROPE], gw_parts[4]], axis=1)

    def first_bwd_fn(dh1v, h, du1v, g):
        dx, dg = _rms_bwd(h, g, du1v)
        return dh1v + dx, dg

    dh0, g_mix_pre = rowwise("norm_mix_pre_bwd", first_bwd_fn, [(dh1, d, 0), (h0, d, 0), (du1, d, 0)], [],
                             [mix_pre_g], [(d, F32)], [(1, d)])
    dh0 = dh0.reshape(bl, lp, d)
    grad_x = dh0[:, PAD_FRONT + N_META:]

    send_b, mine_b, rows_b, shapes_b, total_b = pack_grads(late)
    p0 = lb_soft[0:1]
    g_lb_logits = jnp.concatenate([g_lb * p0 * (1.0 - p0), -g_lb * p0 * (1.0 - p0)], axis=0)

    def row_of(vec):
        return vec.reshape(-1, d) if vec.size >= d else jnp.pad(vec.reshape(1, -1), ((0, 0), (0, d - vec.size)))

    small_parts = dict(b_gate=g_b_gate, lb_logits=g_lb_logits, hg_norm_g=g_hg_norm, q_a_norm_g=g_q_norm,
                       kv_a_norm_g=g_kv_norm, mix_pre_g=g_mix_pre, mix_post_g=g_mix_post, ffn_pre_g=g_ffn_pre,
                       ffn_post_g=g_ffn_post)
    g_meta = jnp.sum(dh0[:, PAD_FRONT:PAD_FRONT + N_META], axis=0)
    small_rows = [row_of(small_parts[n]) for n in SMALL] + [row_of(g_meta)]
    n_small = sum(r.shape[0] for r in small_rows)
    small = jnp.pad(jnp.concatenate(small_rows, axis=0), ((0, -(-n_small // 8) * 8 - n_small), (0, 0)))
    recv_b, all_small = exchange_grads(_bf(send_b), small)
    _, recv_a = push_wait("grads_early_wait", grads_a, all_small, per_chip=True)

    def sum_chips(name, mine, recv, total):
        tr = _tile(total, 512, 16)
        recv2 = recv.reshape(3 * total, PACK_W)
        return rowwise(name, lambda a, r0, r1, r2: a + r0 + r1 + r2,
                       [(mine, PACK_W, 0)] + [(recv2, PACK_W, 0, k * (total // tr)) for k in range(3)],
                       [], [], [(PACK_W, F32)], tm=tr)[0]

    part_sum = jnp.concatenate([sum_chips("sum_chips_early", mine_a, recv_a, total_a),
                                sum_chips("sum_chips_late", mine_b, recv_b, total_b)], axis=0)
    rt = _tile(total_a + total_b, 512, 16)
    sib_sum = swap_with_sibling(part_sum)
    (g_packed,) = rowwise("sum_cores", lambda a, b: a + b, [(part_sum, PACK_W, 0), (sib_sum, PACK_W, 0)], [], [],
                          [(PACK_W, F32)], tm=rt)
    small_t = small.shape[0]

    def sum8_fn(*slabs):
        acc = slabs[0]
        for s in slabs[1:]:
            acc = acc + s
        return acc

    (g_small,) = rowwise("sum_small", sum8_fn, [(all_small.reshape(8 * small_t, d), d, 0, k) for k in range(8)],
                         [], [], [(d, F32)], tm=small_t, n_rows=small_t)

    g_final = {}
    off = 0
    for names, rows_g, shapes_g, total in ((early, rows_a, shapes_a, total_a), (late, rows_b, shapes_b, total_b)):
        start = off
        for n, rows, (r, c) in zip(names, rows_g, shapes_g):
            g_final[n] = g_packed[off:off + rows].reshape(r, c)
            off += rows
        off = start + total
    off = 0
    for n, part in zip(SMALL, small_rows[:-1]):
        rows = part.shape[0]
        g_final[n] = g_small[off:off + rows, :d].reshape(-1)[:wts[n].size].reshape(wts[n].shape)
        off += rows
    mcols = meta_tokens.shape[1]
    g_final["meta_tokens"] = lax.dynamic_slice_in_dim(g_small[off:off + N_META, :d], my_chip * mcols, mcols, axis=1)

    delta, new_m, new_v = {}, {}, {}
    for n in WEIGHTS:
        w2 = wts[n].reshape(-1, wts[n].shape[-1])
        dl, mn, vn = adamw("adamw_" + n, w2, g_final[n].reshape(w2.shape), mom_m[n].reshape(w2.shape),
                           mom_v[n].reshape(w2.shape))
        delta[n], new_m[n], new_v[n] = dl, mn, vn

    loss = lax.psum(loss_part[0, 0], ("x", "y", "c"))

    def shaped(n, a):
        return a.reshape((1,) + wts[n].shape) if n in BIG else a.reshape(wts[n].shape)

    return (loss, grad_x, *[shaped(n, g_final[n]) for n in WEIGHTS], *[shaped(n, delta[n]) for n in WEIGHTS],
            *[shaped(n, new_m[n]) for n in WEIGHTS], *[shaped(n, new_v[n]) for n in WEIGHTS])
```

```python
import functools
import math

import jax
import jax.numpy as jnp
from jax import lax
from jax.experimental import pallas as pl
from jax.experimental.pallas import tpu as pltpu

F32 = jnp.float32
BF16 = jnp.bfloat16
MESH = pl.DeviceIdType.MESH

N_META = 16
NORM_EPS = 1e-6
HEAD = 128
ROPE = 64
ROPE_HALF = ROPE // 2
QK_PAD = 2 * HEAD
CHUNK = 16
ROPE_THETA = 10000.0
SEQ_BLOCK = 256
PAD_FRONT = SEQ_BLOCK - N_META
PACK_W = 1024
NEG = -1e30
VMEM_LIMIT = 56 * 1024 * 1024
ATTN_HEADS_PER_STEP = 1
ATTN_TILE_MAX = 768

ADAM_LR, ADAM_B1, ADAM_B2, ADAM_EPS, ADAM_WD, ADAM_STEP = 0.001, 0.9, 0.999, 1e-08, 0.01, 10

BIG = ("w_in", "w_hg_o", "w_q_b", "w_kv_b", "w_mla_o", "w_out", "w_ffn_in", "w_ffn_out")
COL_SHARDED = ("w_in", "w_q_b", "w_kv_b", "w_ffn_in")
SMALL = ("b_gate", "lb_logits", "hg_norm_g", "q_a_norm_g", "kv_a_norm_g", "mix_pre_g", "mix_post_g",
         "ffn_pre_g", "ffn_post_g")
WEIGHTS = ("meta_tokens", "w_in", "b_gate", "lb_logits", "hg_norm_g", "w_hg_o", "q_a_norm_g", "w_q_b",
           "kv_a_norm_g", "w_kv_b", "w_mla_o", "w_out", "mix_pre_g", "mix_post_g", "ffn_pre_g", "ffn_post_g",
           "w_ffn_in", "w_ffn_out")


def _tile(n, cap, unit=128):
    if n <= cap:
        return n
    best = None
    for t in range(unit, cap + 1, unit):
        if n % t == 0:
            best = t
    assert best is not None, (n, cap, unit)
    return best


def _sigmoid(x):
    return 1.0 / (1.0 + jnp.exp(-x))


def _bf(x):
    return x.astype(BF16)


def rowwise(name, fn, row_ins, seq_tabs, consts, row_outs, acc_outs=(), tm=SEQ_BLOCK, n_rows=None):
    t_rows = row_ins[0][0].shape[0] if n_rows is None else n_rows
    nt = t_rows // tm
    assert t_rows % tm == 0
    n_in = len(row_ins) + len(seq_tabs) + len(consts)
    n_row = len(row_outs)

    def body(*refs):
        vals = [r[...].astype(F32) for r in refs[:n_in]]
        res = fn(*vals)
        if not isinstance(res, (tuple, list)):
            res = (res,)
        outs = refs[n_in:]
        for k in range(n_row):
            outs[k][...] = res[k].astype(outs[k].dtype)
        if acc_outs:
            @pl.when(pl.program_id(0) == 0)
            def _():
                for k in range(len(acc_outs)):
                    outs[n_row + k][...] = jnp.zeros_like(outs[n_row + k])

            for k in range(len(acc_outs)):
                outs[n_row + k][...] += res[n_row + k]

    row_ins = [tuple(e) + (0,) * (4 - len(e)) for e in row_ins]
    in_specs = [pl.BlockSpec((tm, w), functools.partial(lambda i, j, ro: (i + ro, j), j=j, ro=ro))
                for (_, w, j, ro) in row_ins]
    for tab in seq_tabs:
        per = tab.shape[0] // tm
        in_specs.append(pl.BlockSpec((tm, tab.shape[1]), functools.partial(lambda i, per: (i % per, 0), per=per)))
    for c in consts:
        in_specs.append(pl.BlockSpec(c.shape, lambda i: (0, 0)))
    out_specs = [pl.BlockSpec((tm, w), lambda i: (i, 0)) for (w, _) in row_outs]
    out_specs += [pl.BlockSpec(s, lambda i: (0, 0)) for s in acc_outs]
    out_shape = [jax.ShapeDtypeStruct((t_rows, w), dt) for (w, dt) in row_outs]
    out_shape += [jax.ShapeDtypeStruct(s, F32) for s in acc_outs]
    res = pl.pallas_call(
        body, name=name, grid=(nt,), in_specs=in_specs, out_specs=out_specs, out_shape=out_shape,
        compiler_params=pltpu.CompilerParams(dimension_semantics=("arbitrary",)),
    )(*[e[0] for e in row_ins], *seq_tabs, *consts)
    return res


def matmul(name, a, b, mode, out_dtype=F32):
    if mode != "tn":
        return _matmul_resident(name, a if isinstance(a, (list, tuple)) else [a],
                                b if isinstance(b, (list, tuple)) else [b], mode, out_dtype)
    kdim, m = a.shape
    n = b.shape[1]
    tn = _tile(n, 1536)
    tm, tk = _tile(m, 1408 if tn <= 1024 else 1024), _tile(kdim, 512)
    nk = kdim // tk

    def body(a_ref, b_ref, o_ref, acc_ref):
        k = pl.program_id(2)

        @pl.when(k == 0)
        def _():
            acc_ref[...] = jnp.zeros_like(acc_ref)

        acc_ref[...] += lax.dot_general(a_ref[...], b_ref[...], TN_DIMS, preferred_element_type=F32)

        @pl.when(k == nk - 1)
        def _():
            o_ref[...] = acc_ref[...].astype(o_ref.dtype)

    return pl.pallas_call(
        body, name=name, grid=(m // tm, n // tn, nk),
        in_specs=[pl.BlockSpec((tk, tm), lambda i, j, k: (k, i)), pl.BlockSpec((tk, tn), lambda i, j, k: (k, j))],
        out_specs=pl.BlockSpec((tm, tn), lambda i, j, k: (i, j)),
        out_shape=jax.ShapeDtypeStruct((m, n), out_dtype),
        scratch_shapes=[pltpu.VMEM((tm, tn), F32)],
        compiler_params=pltpu.CompilerParams(dimension_semantics=("arbitrary", "arbitrary", "arbitrary"),
                                             vmem_limit_bytes=VMEM_LIMIT),
    )(a, b)


def _matmul_resident(name, a_list, b_list, mode, out_dtype):
    m = a_list[0].shape[0]
    n = b_list[0].shape[1] if mode == "nn" else b_list[0].shape[0]
    k_total = sum(a.shape[1] for a in a_list)
    out_bytes = 2 if out_dtype == BF16 else 4
    budget = VMEM_LIMIT - 4 * k_total * n - (6 << 20)
    tm = 512
    while tm > 128 and 2 * tm * (2 * k_total + out_bytes * n) > budget:
        tm //= 2
    tm = _tile(m, tm)
    cn = _tile(n, 1024)
    npairs = len(a_list)

    def body(*refs):
        a_refs, b_refs, o_ref = refs[:npairs], refs[npairs:2 * npairs], refs[2 * npairs]
        for c in range(n // cn):
            acc = None
            for a_ref, b_ref in zip(a_refs, b_refs):
                if mode == "nn":
                    part = jnp.dot(a_ref[...], b_ref[:, pl.ds(c * cn, cn)], preferred_element_type=F32)
                else:
                    part = lax.dot_general(a_ref[...], b_ref[pl.ds(c * cn, cn), :], NT_DIMS,
                                           preferred_element_type=F32)
                acc = part if acc is None else acc + part
            o_ref[:, pl.ds(c * cn, cn)] = acc.astype(o_ref.dtype)

    in_specs = [pl.BlockSpec((tm, a.shape[1]), lambda i: (i, 0)) for a in a_list]
    in_specs += [pl.BlockSpec(b.shape, lambda i: (0, 0)) for b in b_list]
    return pl.pallas_call(
        body, name=name, grid=(m // tm,), in_specs=in_specs,
        out_specs=pl.BlockSpec((tm, n), lambda i: (i, 0)),
        out_shape=jax.ShapeDtypeStruct((m, n), out_dtype),
        compiler_params=pltpu.CompilerParams(dimension_semantics=("arbitrary",), vmem_limit_bytes=VMEM_LIMIT),
    )(*a_list, *b_list)


def _rms(x, g):
    r = lax.rsqrt(jnp.mean(x * x, axis=-1, keepdims=True) + NORM_EPS)
    return x * r * g


def _rms_bwd(x, g, dy):
    r = lax.rsqrt(jnp.mean(x * x, axis=-1, keepdims=True) + NORM_EPS)
    xh = x * r
    dyg = dy * g
    dx = r * (dyg - xh * jnp.mean(dyg * xh, axis=-1, keepdims=True))
    return dx, jnp.sum(dy * xh, axis=0, keepdims=True)


def _silu(x):
    return x * _sigmoid(x)


def _silu_grad(x):
    s = _sigmoid(x)
    return s * (1.0 + x * (1.0 - s))


def _rope(xs, cos, s_up, s_dn):
    return xs * cos + pltpu.roll(xs, ROPE_HALF, 1) * s_up + pltpu.roll(xs, HEAD - ROPE_HALF, 1) * s_dn


def _rope_bwd(dy, cos, s_up, s_dn):
    return dy * cos + pltpu.roll(dy * s_up, HEAD - ROPE_HALF, 1) + pltpu.roll(dy * s_dn, ROPE_HALF, 1)


HG_SUB = 128
HG_LEVELS = 7
HG_E_ROWS = (HG_LEVELS + 1) * HG_SUB
TN_DIMS = (((0,), (0,)), ((), ()))
NT_DIMS = (((1,), (1,)), ((), ()))


def _hg_constants():
    import numpy as np
    n = HG_SUB
    r = np.arange(n)[:, None]
    c = np.arange(n)[None, :]
    cs, ps = [], []
    for lvl in range(HG_LEVELS):
        m = (n // 2) >> lvl
        upper = (r % (2 * m)) >= m
        mid = (r // (2 * m)) * (2 * m) + m - 1
        cs.append(np.where(upper, (c > mid) & (c <= r), (c > r) & (c <= mid)))
        ps.append(((r // (2 * m)) == (c // (2 * m))) & upper & ((c % (2 * m)) < m))
    cs.append(c <= r)
    cs.append(np.ones((8, n), bool))
    cstack = np.concatenate(cs, 0).astype(np.float32)
    pstack = np.concatenate(ps, 0).astype(np.float32)
    pstack_t = np.concatenate([p.T for p in ps], 0).astype(np.float32)
    return (jnp.asarray(cstack, BF16), jnp.asarray(cstack[:HG_E_ROWS].T, BF16), jnp.asarray(pstack, F32),
            jnp.asarray(pstack_t, F32))


def _split_dot(c_bf, x):
    hi = _bf(x)
    lo = _bf(x - hi.astype(F32))
    r2 = jnp.dot(c_bf, jnp.concatenate([hi, lo], axis=1), preferred_element_type=F32)
    return r2[:, :HEAD] + r2[:, HEAD:]


def _hg_gates(hq, hf, lb):
    sq = _sigmoid(hq)
    sg = _sigmoid(hf)
    fg = lb + (1.0 - lb) * sg
    return sq, hq * sq, sg, fg, 1.0 - fg, jnp.log(fg)


def _hg_block_fwd(st, hq, hf, hi, lb, cstack, p_ref):
    _, q, _, _, k, g = _hg_gates(hq, hf, lb)
    v = hi
    e = _split_dot(cstack, g)
    bc = e[HG_LEVELS * HG_SUB:HG_E_ROWS]
    b_last = jnp.tile(e[HG_E_ROWS:], (HG_SUB // 8, 1))
    a = jnp.zeros((HG_SUB, HG_SUB), F32)
    for lvl in range(HG_LEVELS):
        x = jnp.exp(e[lvl * HG_SUB:(lvl + 1) * HG_SUB])
        a = a + p_ref[pl.ds(lvl * HG_SUB, HG_SUB), :] * lax.dot_general(_bf(q * x), _bf(k * x), NT_DIMS,
                                                                          preferred_element_type=F32)
    a_bf = _bf(a)
    diag = jnp.sum(q * k, axis=1, keepdims=True)
    o = (jnp.dot(a_bf, _bf(v), preferred_element_type=F32) + diag * v
         + lax.dot_general(_bf(q * jnp.exp(bc)), _bf(st), NT_DIMS, preferred_element_type=F32))
    kd = k * jnp.exp(b_last - bc)
    st_out = st * jnp.exp(b_last) + lax.dot_general(_bf(v), _bf(kd), TN_DIMS, preferred_element_type=F32)
    return st_out, o, a_bf


def _hg_block_bwd(st, dst_out, do, hq, hf, hi, lb, a_bf, cstack, cstack_t, p_ref, pt_ref):
    sq, q, sg, fg, k, g = _hg_gates(hq, hf, lb)
    v = hi
    e = _split_dot(cstack, g)
    bc = e[HG_LEVELS * HG_SUB:HG_E_ROWS]
    b_last = jnp.tile(e[HG_E_ROWS:], (HG_SUB // 8, 1))
    eb = jnp.exp(bc)
    qb = q * eb
    er = jnp.exp(b_last - bc)
    kd = k * er
    e_last = jnp.exp(b_last)
    do_bf, v_bf, dst_bf = _bf(do), _bf(v), _bf(dst_out)
    da = lax.dot_general(do_bf, v_bf, NT_DIMS, preferred_element_type=F32)
    dat = lax.dot_general(v_bf, do_bf, NT_DIMS, preferred_element_type=F32)
    d_diag = jnp.sum(do * v, axis=1, keepdims=True)
    dv = (lax.dot_general(a_bf, do_bf, TN_DIMS, preferred_element_type=F32)
          + jnp.sum(q * k, axis=1, keepdims=True) * do
          + lax.dot_general(_bf(kd), dst_bf, NT_DIMS, preferred_element_type=F32))
    dqb = jnp.dot(do_bf, _bf(st), preferred_element_type=F32)
    dst = dst_out * e_last + lax.dot_general(do_bf, _bf(qb), TN_DIMS, preferred_element_type=F32)
    dkd = jnp.dot(v_bf, dst_bf, preferred_element_type=F32)
    dq = dqb * eb + d_diag * k
    dk = dkd * er + d_diag * q
    d_last = (jnp.sum(dst_out * st * e_last, axis=0, keepdims=True)
              + jnp.sum(dkd * kd, axis=0, keepdims=True))
    des = []
    for lvl in range(HG_LEVELS):
        x = jnp.exp(e[lvl * HG_SUB:(lvl + 1) * HG_SUB])
        qh, kh = q * x, k * x
        dm = _bf(p_ref[pl.ds(lvl * HG_SUB, HG_SUB), :] * da)
        dmt = _bf(pt_ref[pl.ds(lvl * HG_SUB, HG_SUB), :] * dat)
        dqh = jnp.dot(dm, _bf(kh), preferred_element_type=F32)
        dkh = jnp.dot(dmt, _bf(qh), preferred_element_type=F32)
        dq = dq + dqh * x
        dk = dk + dkh * x
        des.append(dqh * qh + dkh * kh)
    des.append(dqb * qb - dkd * kd)
    dg = _split_dot(cstack_t, jnp.concatenate(des, axis=0)) + d_last
    dfg = dg / fg - dk
    dhq = dq * (sq * (1.0 + hq * (1.0 - sq)))
    dhf = dfg * (1.0 - lb) * sg * (1.0 - sg)
    return dst, dhq, dhf, dv, jnp.sum(dfg * (1.0 - sg), axis=0, keepdims=True)


def hgrn_fwd(proj_main, lb, consts, bl, lp, d):
    nh = d // HEAD
    rows_blk = _tile(lp, 768, SEQ_BLOCK)
    nb = lp // rows_blk
    spb = rows_blk // HG_SUB
    cstack, _, pstack, _ = consts

    def body(hq_ref, hf_ref, hi_ref, lb_ref, c_ref, p_ref, o_ref, st_ref, a_ref, s_ref):
        j = pl.program_id(2)

        @pl.when(j == 0)
        def _():
            s_ref[...] = jnp.zeros_like(s_ref)

        lbv = lb_ref[...]
        cs = c_ref[...]

        def sub(n, carry):
            r = pl.multiple_of(n * HG_SUB, HG_SUB)
            st = s_ref[...]
            st_ref[0, 0, pl.ds(n, 1)] = st[None]
            st_out, o, a_bf = _hg_block_fwd(st, hq_ref[pl.ds(r, HG_SUB), :].astype(F32),
                                            hf_ref[pl.ds(r, HG_SUB), :].astype(F32),
                                            hi_ref[pl.ds(r, HG_SUB), :].astype(F32), lbv, cs, p_ref)
            s_ref[...] = st_out
            o_ref[pl.ds(r, HG_SUB), :] = o
            a_ref[0, 0, pl.ds(n, 1)] = a_bf[None]
            return carry

        lax.fori_loop(0, spb, sub, 0, unroll=2)

    def colspec(off):
        return pl.BlockSpec((rows_blk, HEAD), functools.partial(lambda h, b, j, off: (b * nb + j, off + h), off=off))

    whole = lambda arr: pl.BlockSpec(arr.shape, lambda h, b, j: (0, 0))
    return pl.pallas_call(
        body, name="hgrn_fwd", grid=(nh, bl, nb),
        in_specs=[colspec(0), colspec(nh), colspec(2 * nh), pl.BlockSpec((1, HEAD), lambda h, b, j: (0, h)),
                  whole(cstack), whole(pstack)],
        out_specs=[pl.BlockSpec((rows_blk, HEAD), lambda h, b, j: (b * nb + j, h)),
                   pl.BlockSpec((1, 1, spb, HEAD, HEAD), lambda h, b, j: (b, h, j, 0, 0)),
                   pl.BlockSpec((1, 1, spb, HG_SUB, HG_SUB), lambda h, b, j: (b, h, j, 0, 0))],
        out_shape=[jax.ShapeDtypeStruct((bl * lp, d), F32),
                   jax.ShapeDtypeStruct((bl, nh, lp // HG_SUB, HEAD, HEAD), F32),
                   jax.ShapeDtypeStruct((bl, nh, lp // HG_SUB, HG_SUB, HG_SUB), BF16)],
        scratch_shapes=[pltpu.VMEM((HEAD, HEAD), F32)],
        compiler_params=pltpu.CompilerParams(dimension_semantics=("arbitrary", "arbitrary", "arbitrary")),
    )(proj_main, proj_main, proj_main, lb, cstack, pstack)


def hgrn_bwd(proj_main, lb, consts, states, a_mats, do_scan, bl, lp, d):
    nh = d // HEAD
    rows_blk = _tile(lp, 768, SEQ_BLOCK)
    nb = lp // rows_blk
    spb = rows_blk // HG_SUB
    cstack, cstack_t, pstack, pstack_t = consts

    def body(hq_ref, hf_ref, hi_ref, lb_ref, c_ref, ct_ref, p_ref, pt_ref, st_ref, a_ref, do_ref,
             dq_ref, df_ref, di_ref, dlb_ref, ds_ref):
        b_id, j = pl.program_id(1), pl.program_id(2)
        blk = nb - 1 - j

        @pl.when(j == 0)
        def _():
            ds_ref[...] = jnp.zeros_like(ds_ref)

        @pl.when((j == 0) & (b_id == 0))
        def _():
            dlb_ref[...] = jnp.zeros_like(dlb_ref)

        lbv = lb_ref[...]
        cs = c_ref[...]
        cst = ct_ref[...]

        def sub(i, carry):
            n = spb - 1 - i
            r = pl.multiple_of(n * HG_SUB, HG_SUB)
            dst, dhq, dhf, dhi, dlb = _hg_block_bwd(
                st_ref[0, 0, pl.ds(n, 1)][0], ds_ref[...], do_ref[pl.ds(r, HG_SUB), :],
                hq_ref[pl.ds(r, HG_SUB), :].astype(F32), hf_ref[pl.ds(r, HG_SUB), :].astype(F32),
                hi_ref[pl.ds(r, HG_SUB), :].astype(F32), lbv,
                a_ref[0, 0, pl.ds(n, 1)][0], cs, cst, p_ref, pt_ref)
            ds_ref[...] = dst
            dq_ref[pl.ds(r, HG_SUB), :] = dhq.astype(dq_ref.dtype)
            df_ref[pl.ds(r, HG_SUB), :] = dhf.astype(df_ref.dtype)
            di_ref[pl.ds(r, HG_SUB), :] = dhi.astype(di_ref.dtype)
            dlb_ref[...] += dlb
            return carry

        lax.fori_loop(0, spb, sub, 0, unroll=2)

    def colspec(off):
        return pl.BlockSpec((rows_blk, HEAD),
                            functools.partial(lambda h, b, j, off: (b * nb + nb - 1 - j, off + h), off=off))

    whole = lambda arr: pl.BlockSpec(arr.shape, lambda h, b, j: (0, 0))
    mats = lambda: pl.BlockSpec((1, 1, spb, HEAD, HEAD), lambda h, b, j: (b, h, nb - 1 - j, 0, 0))
    t_rows = bl * lp
    return pl.pallas_call(
        body, name="hgrn_bwd", grid=(nh, bl, nb),
        in_specs=[colspec(0), colspec(nh), colspec(2 * nh), pl.BlockSpec((1, HEAD), lambda h, b, j: (0, h)),
                  whole(cstack), whole(cstack_t), whole(pstack), whole(pstack_t), mats(), mats(), colspec(0)],
        out_specs=[colspec(0), colspec(0), colspec(0), pl.BlockSpec((1, HEAD), lambda h, b, j: (0, h))],
        out_shape=[jax.ShapeDtypeStruct((t_rows, d), BF16)] * 3 + [jax.ShapeDtypeStruct((1, d), F32)],
        scratch_shapes=[pltpu.VMEM((HEAD, HEAD), F32)],
        compiler_params=pltpu.CompilerParams(dimension_semantics=("arbitrary", "arbitrary", "arbitrary")),
    )(proj_main, proj_main, proj_main, lb, cstack, cstack_t, pstack, pstack_t, states, a_mats, do_scan)


def _allowed(row0, col0, nr, nc, transposed=False):
    if transposed:
        col = col0 + lax.broadcasted_iota(jnp.int32, (nc, 1), 0)
        row = row0 + lax.broadcasted_iota(jnp.int32, (1, nr), 1)
    else:
        row = row0 + lax.broadcasted_iota(jnp.int32, (nr, 1), 0)
        col = col0 + lax.broadcasted_iota(jnp.int32, (1, nc), 1)
    return (col <= row) & ((col >= PAD_FRONT) | (row < PAD_FRONT))


def attn_fwd(q_cat, k_cat, v, bl, lp, nm, scale):
    tq = tk = SEQ_BLOCK
    nq = lp // tq

    def body(q_ref, k_ref, v_ref, o_ref, lse_ref, m_ref, l_ref, acc_ref):
        i = pl.program_id(2)
        q = q_ref[...]
        m_ref[...] = jnp.full_like(m_ref, NEG)
        l_ref[...] = jnp.zeros_like(l_ref)
        acc_ref[...] = jnp.zeros_like(acc_ref)

        def kstep(c, carry):
            c0 = pl.multiple_of(c * tk, tk)
            s = lax.dot_general(q, k_ref[pl.ds(c0, tk), :], NT_DIMS, preferred_element_type=F32) * scale
            s = jnp.where(_allowed(i * tq, c * tk, tq, tk), s, NEG)
            m_old = m_ref[...]
            m_new = jnp.maximum(m_old, jnp.max(s, axis=1, keepdims=True))
            alpha = jnp.exp(m_old - m_new)
            p = jnp.exp(s - m_new)
            l_ref[...] = alpha * l_ref[...] + jnp.sum(p, axis=1, keepdims=True)
            acc_ref[...] = alpha * acc_ref[...] + jnp.dot(_bf(p), v_ref[pl.ds(c0, tk), :],
                                                          preferred_element_type=F32)
            m_ref[...] = m_new
            return carry

        lax.fori_loop(0, i + 1, kstep, 0)
        o_ref[...] = (acc_ref[...] / l_ref[...]).astype(o_ref.dtype)
        lse_ref[0, 0] = m_ref[...] + jnp.log(l_ref[...])

    return pl.pallas_call(
        body, name="attn_fwd", grid=(bl, nm, nq),
        in_specs=[pl.BlockSpec((tq, QK_PAD), lambda b, h, i: (b * nq + i, h)),
                  pl.BlockSpec((lp, QK_PAD), lambda b, h, i: (b, h)),
                  pl.BlockSpec((lp, HEAD), lambda b, h, i: (b, h))],
        out_specs=[pl.BlockSpec((tq, HEAD), lambda b, h, i: (b * nq + i, h)),
                   pl.BlockSpec((1, 1, tq, 1), lambda b, h, i: (b, h, i, 0))],
        out_shape=[jax.ShapeDtypeStruct((bl * lp, nm * HEAD), BF16),
                   jax.ShapeDtypeStruct((bl, nm, lp, 1), F32)],
        scratch_shapes=[pltpu.VMEM((tq, 1), F32), pltpu.VMEM((tq, 1), F32), pltpu.VMEM((tq, HEAD), F32)],
        compiler_params=pltpu.CompilerParams(dimension_semantics=("arbitrary", "arbitrary", "arbitrary")),
    )(q_cat, k_cat, v)


def attn_bwd_dq(q_cat, k_cat, v, o, do, lse, bl, lp, nm, scale):
    tq = tk = SEQ_BLOCK
    nq = lp // tq

    def body(q_ref, k_ref, v_ref, o_ref, do_ref, lse_ref, dq_ref, dl_ref, acc_ref):
        i = pl.program_id(2)
        q = q_ref[...]
        do_b = do_ref[...]
        delta = jnp.sum(o_ref[...].astype(F32) * do_b.astype(F32), axis=1, keepdims=True)
        lse_b = lse_ref[0, 0]
        acc_ref[...] = jnp.zeros_like(acc_ref)

        def kstep(c, carry):
            c0 = pl.multiple_of(c * tk, tk)
            ks = k_ref[pl.ds(c0, tk), :]
            s = lax.dot_general(q, ks, NT_DIMS, preferred_element_type=F32) * scale
            p = jnp.where(_allowed(i * tq, c * tk, tq, tk), jnp.exp(s - lse_b), 0.0)
            dp = lax.dot_general(do_b, v_ref[pl.ds(c0, tk), :], NT_DIMS, preferred_element_type=F32)
            ds = p * (dp - delta)
            acc_ref[...] += jnp.dot(_bf(ds), ks, preferred_element_type=F32)
            return carry

        lax.fori_loop(0, i + 1, kstep, 0)
        dq_ref[...] = acc_ref[...] * scale
        dl_ref[0, 0] = delta

    return pl.pallas_call(
        body, name="attn_bwd_dq", grid=(bl, nm, nq),
        in_specs=[pl.BlockSpec((tq, QK_PAD), lambda b, h, i: (b * nq + i, h)),
                  pl.BlockSpec((lp, QK_PAD), lambda b, h, i: (b, h)),
                  pl.BlockSpec((lp, HEAD), lambda b, h, i: (b, h)),
                  pl.BlockSpec((tq, HEAD), lambda b, h, i: (b * nq + i, h)),
                  pl.BlockSpec((tq, HEAD), lambda b, h, i: (b * nq + i, h)),
                  pl.BlockSpec((1, 1, tq, 1), lambda b, h, i: (b, h, i, 0))],
        out_specs=[pl.BlockSpec((tq, QK_PAD), lambda b, h, i: (b * nq + i, h)),
                   pl.BlockSpec((1, 1, tq, 1), lambda b, h, i: (b, h, i, 0))],
        out_shape=[jax.ShapeDtypeStruct((bl * lp, nm * QK_PAD), F32),
                   jax.ShapeDtypeStruct((bl, nm, lp, 1), F32)],
        scratch_shapes=[pltpu.VMEM((tq, QK_PAD), F32)],
        compiler_params=pltpu.CompilerParams(dimension_semantics=("arbitrary", "arbitrary", "arbitrary")),
    )(q_cat, k_cat, v, o, do, lse)


def attn_bwd_dkv(q_cat, k_cat, v, do, lse_row, delta_row, bl, lp, nm, scale):
    tq = tk = SEQ_BLOCK
    nq = lp // tq

    def body(q_ref, k_ref, v_ref, do_ref, lse_ref, dl_ref, dk_ref, dv_ref):
        i = pl.program_id(2)
        kt = k_ref[...]
        vt = v_ref[...]
        dk_ref[...] = jnp.zeros_like(dk_ref)
        dv_ref[...] = jnp.zeros_like(dv_ref)

        def qstep(c, carry):
            c0 = pl.multiple_of(c * tq, tq)
            qs = q_ref[pl.ds(c0, tq), :]
            dos = do_ref[pl.ds(c0, tq), :]
            st = lax.dot_general(kt, qs, NT_DIMS, preferred_element_type=F32) * scale
            pt = jnp.where(_allowed(c * tq, i * tk, tq, tk, transposed=True),
                           jnp.exp(st - lse_ref[0, 0, pl.ds(c, 1)][0]), 0.0)
            dv_ref[...] += jnp.dot(_bf(pt), dos, preferred_element_type=F32)
            dpt = lax.dot_general(vt, dos, NT_DIMS, preferred_element_type=F32)
            dst = pt * (dpt - dl_ref[0, 0, pl.ds(c, 1)][0])
            dk_ref[...] += jnp.dot(_bf(dst), qs, preferred_element_type=F32)
            return carry

        lax.fori_loop(i, nq, qstep, 0)
        dk_ref[...] = dk_ref[...] * scale

    return pl.pallas_call(
        body, name="attn_bwd_dkv", grid=(bl, nm, nq),
        in_specs=[pl.BlockSpec((lp, QK_PAD), lambda b, h, i: (b, h)),
                  pl.BlockSpec((tk, QK_PAD), lambda b, h, i: (b * nq + i, h)),
                  pl.BlockSpec((tk, HEAD), lambda b, h, i: (b * nq + i, h)),
                  pl.BlockSpec((lp, HEAD), lambda b, h, i: (b, h)),
                  pl.BlockSpec((1, 1, nq, 1, tq), lambda b, h, i: (b, h, 0, 0, 0)),
                  pl.BlockSpec((1, 1, nq, 1, tq), lambda b, h, i: (b, h, 0, 0, 0))],
        out_specs=[pl.BlockSpec((tk, QK_PAD), lambda b, h, i: (b * nq + i, h)),
                   pl.BlockSpec((tk, HEAD), lambda b, h, i: (b * nq + i, h))],
        out_shape=[jax.ShapeDtypeStruct((bl * lp, nm * QK_PAD), F32),
                   jax.ShapeDtypeStruct((bl * lp, nm * HEAD), F32)],
        compiler_params=pltpu.CompilerParams(dimension_semantics=("arbitrary", "arbitrary", "arbitrary")),
    )(q_cat, k_cat, v, do, lse_row, delta_row)


def _key_query_mask(key0, qry0, nk, nq_, causal):
    key = key0 + lax.broadcasted_iota(jnp.int32, (nk, 1), 0)
    if not causal:
        return key >= PAD_FRONT
    qry = qry0 + lax.broadcasted_iota(jnp.int32, (1, nq_), 1)
    return (key <= qry) & (key >= PAD_FRONT)


def _attn_tile(lp):
    return _tile(lp, ATTN_TILE_MAX, SEQ_BLOCK)


def attn_fwd_t(q_cat, k_cat, v_t, bl, lp, nm, scale):
    tq = tk = _attn_tile(lp)
    nq = lp // tq
    hp = ATTN_HEADS_PER_STEP
    assert nm % hp == 0

    def body(q_ref, k_ref, vt_ref, o_ref, lse_ref, m_ref, l_ref, acc_ref):
        i = pl.program_id(2)
        m_ref[...] = jnp.full_like(m_ref, NEG)
        l_ref[...] = jnp.zeros_like(l_ref)
        acc_ref[...] = jnp.zeros_like(acc_ref)

        def step(c, mask):
            c0 = pl.multiple_of(c * tk, tk)
            for hh in range(hp):
                cols = pl.ds(hh * QK_PAD, QK_PAD)
                st = lax.dot_general(k_ref[pl.ds(c0, tk), cols], q_ref[:, cols], NT_DIMS,
                                     preferred_element_type=F32) * scale
                if mask is not None:
                    st = jnp.where(_key_query_mask(c * tk, i * tq, tk, tq, mask == "causal"), st, NEG)
                m_old = m_ref[hh]
                m_new = jnp.maximum(m_old, jnp.max(st, axis=0, keepdims=True))
                alpha = jnp.exp(m_old - m_new)
                pt = jnp.exp(st - m_new)
                l_ref[hh] = alpha * l_ref[hh] + jnp.sum(pt, axis=0, keepdims=True)
                acc_ref[hh] = alpha * acc_ref[hh] + jnp.dot(vt_ref[0, hh, pl.ds(c, 1)][0], _bf(pt),
                                                            preferred_element_type=F32)
                m_ref[hh] = m_new

        def mid(c, carry):
            step(c, None)
            return carry

        @pl.when(i == 0)
        def _():
            step(0, "causal")

        @pl.when(i > 0)
        def _():
            step(0, "pad")
            lax.fori_loop(1, i, mid, 0)
            step(i, "causal")

        for hh in range(hp):
            o_ref[:, pl.ds(hh * HEAD, HEAD)] = jnp.transpose(acc_ref[hh] / l_ref[hh]).astype(o_ref.dtype)
            lse_ref[0, hh, 0] = m_ref[hh] + jnp.log(l_ref[hh])

    return pl.pallas_call(
        body, name="attn_fwd", grid=(bl, nm // hp, nq),
        in_specs=[pl.BlockSpec((tq, hp * QK_PAD), lambda b, h, i: (b * nq + i, h)),
                  pl.BlockSpec((lp, hp * QK_PAD), lambda b, h, i: (b, h)),
                  pl.BlockSpec((1, hp, nq, HEAD, tk), lambda b, h, i: (b, h, 0, 0, 0))],
        out_specs=[pl.BlockSpec((tq, hp * HEAD), lambda b, h, i: (b * nq + i, h)),
                   pl.BlockSpec((1, hp, 1, 1, tq), lambda b, h, i: (b, h, i, 0, 0))],
        out_shape=[jax.ShapeDtypeStruct((bl * lp, nm * HEAD), BF16),
                   jax.ShapeDtypeStruct((bl, nm, nq, 1, tq), F32)],
        scratch_shapes=[pltpu.VMEM((hp, 1, tq), F32), pltpu.VMEM((hp, 1, tq), F32), pltpu.VMEM((hp, HEAD, tq), F32)],
        compiler_params=pltpu.CompilerParams(dimension_semantics=("arbitrary", "arbitrary", "arbitrary")),
    )(q_cat, k_cat, v_t)


def attn_bwd_t(q_cat, k_cat, k_t, v, o, do, lse, bl, lp, nm, scale):
    tq = tk = _attn_tile(lp)
    nq = lp // tq
    hp = ATTN_HEADS_PER_STEP
    assert nm % hp == 0

    def body(q_ref, k_ref, kt_ref, v_ref, o_ref, do_ref, lse_ref, dq_ref, dk_ref, dv_ref, dqt_ref, dka_ref, dva_ref):
        i = pl.program_id(2)

        @pl.when(i == 0)
        def _():
            dqt_ref[...] = jnp.zeros_like(dqt_ref)

        dka_ref[...] = jnp.zeros_like(dka_ref)
        dva_ref[...] = jnp.zeros_like(dva_ref)
        ones8 = jnp.ones((8, HEAD), BF16)

        def step(c, mask):
            c0 = pl.multiple_of(c * tq, tq)
            for hh in range(hp):
                qcols, vcols = pl.ds(hh * QK_PAD, QK_PAD), pl.ds(hh * HEAD, HEAD)
                qs = q_ref[pl.ds(c0, tq), qcols]
                dos = do_ref[pl.ds(c0, tq), vcols]
                prod = dos.astype(F32) * o_ref[pl.ds(c0, tq), vcols].astype(F32)
                hi = _bf(prod)
                lo = _bf(prod - hi.astype(F32))
                delta8 = (lax.dot_general(ones8, hi, NT_DIMS, preferred_element_type=F32)
                          + lax.dot_general(ones8, lo, NT_DIMS, preferred_element_type=F32))
                st = lax.dot_general(k_ref[:, qcols], qs, NT_DIMS, preferred_element_type=F32) * scale
                pt = jnp.exp(st - lse_ref[0, hh, pl.ds(c, 1)][0])
                if mask is not None:
                    pt = jnp.where(_key_query_mask(i * tk, c * tq, tk, tq, mask == "causal"), pt, 0.0)
                dva_ref[hh] += jnp.dot(_bf(pt), dos, preferred_element_type=F32)
                dpt = lax.dot_general(v_ref[:, vcols], dos, NT_DIMS, preferred_element_type=F32)
                dst = _bf(pt * (dpt - jnp.tile(delta8, (tk // 8, 1))))
                dka_ref[hh] += jnp.dot(dst, qs, preferred_element_type=F32)
                dqt_ref[hh, pl.ds(c, 1)] += jnp.dot(kt_ref[0, hh, 0], dst, preferred_element_type=F32)[None]

        step(i, "causal")

        def rest_masked(c, carry):
            step(c, "pad")
            return carry

        def rest(c, carry):
            step(c, None)
            return carry

        @pl.when(i == 0)
        def _():
            lax.fori_loop(1, nq, rest_masked, 0)

        @pl.when(i > 0)
        def _():
            lax.fori_loop(i + 1, nq, rest, 0)

        for hh in range(hp):
            dk_ref[:, pl.ds(hh * QK_PAD, QK_PAD)] = (dka_ref[hh] * scale).astype(dk_ref.dtype)
            dv_ref[:, pl.ds(hh * HEAD, HEAD)] = dva_ref[hh].astype(dv_ref.dtype)

        @pl.when(i == nq - 1)
        def _():
            for hh in range(hp):
                for c in range(nq):
                    dq_ref[pl.ds(c * tq, tq), pl.ds(hh * QK_PAD, QK_PAD)] = (
                        jnp.transpose(dqt_ref[hh, c]) * scale).astype(dq_ref.dtype)

    return pl.pallas_call(
        body, name="attn_bwd", grid=(bl, nm // hp, nq),
        in_specs=[pl.BlockSpec((lp, hp * QK_PAD), lambda b, h, i: (b, h)),
                  pl.BlockSpec((tk, hp * QK_PAD), lambda b, h, i: (b * nq + i, h)),
                  pl.BlockSpec((1, hp, 1, QK_PAD, tk), lambda b, h, i: (b, h, i, 0, 0)),
                  pl.BlockSpec((tk, hp * HEAD), lambda b, h, i: (b * nq + i, h)),
                  pl.BlockSpec((lp, hp * HEAD), lambda b, h, i: (b, h)),
                  pl.BlockSpec((lp, hp * HEAD), lambda b, h, i: (b, h)),
                  pl.BlockSpec((1, hp, nq, 1, tq), lambda b, h, i: (b, h, 0, 0, 0))],
        out_specs=[pl.BlockSpec((lp, hp * QK_PAD), lambda b, h, i: (b, h)),
                   pl.BlockSpec((tk, hp * QK_PAD), lambda b, h, i: (b * nq + i, h)),
                   pl.BlockSpec((tk, hp * HEAD), lambda b, h, i: (b * nq + i, h))],
        out_shape=[jax.ShapeDtypeStruct((bl * lp, nm * QK_PAD), BF16),
                   jax.ShapeDtypeStruct((bl * lp, nm * QK_PAD), BF16),
                   jax.ShapeDtypeStruct((bl * lp, nm * HEAD), BF16)],
        scratch_shapes=[pltpu.VMEM((hp, nq, QK_PAD, tq), F32), pltpu.VMEM((hp, tk, QK_PAD), F32),
                        pltpu.VMEM((hp, tk, HEAD), F32)],
        compiler_params=pltpu.CompilerParams(dimension_semantics=("arbitrary", "arbitrary", "arbitrary")),
    )(q_cat, k_cat, k_t, v, o, do, lse)


def _place():
    return lax.axis_index("x"), lax.axis_index("y"), lax.axis_index("c")


def gather_shards(packed):
    hbm = pl.BlockSpec(memory_space=pl.ANY)

    def body(src_ref, out_ref, send_sems, recv_sems, local_sem):
        x, y, c = _place()
        me = 2 * x + y
        chips = [(1 - x, y), (x, 1 - y), (1 - x, 1 - y)]
        local = pltpu.make_async_copy(src_ref, out_ref.at[me], local_sem)
        local.start()
        sends = []
        for k, (px, py) in enumerate(chips):
            cp = pltpu.make_async_remote_copy(src_ref=src_ref, dst_ref=out_ref.at[me], send_sem=send_sems.at[k],
                                              recv_sem=recv_sems.at[k], device_id=(px, py, c), device_id_type=MESH)
            cp.start()
            sends.append(cp)
        for k, (px, py) in enumerate(chips):
            pltpu.make_async_remote_copy(src_ref=src_ref, dst_ref=out_ref.at[2 * px + py], send_sem=send_sems.at[k],
                                         recv_sem=recv_sems.at[k], device_id=(px, py, c),
                                         device_id_type=MESH).wait_recv()
        for cp in sends:
            cp.wait_send()
        local.wait()

    return pl.pallas_call(
        body, name="gather_shards", in_specs=[hbm], out_specs=hbm,
        out_shape=jax.ShapeDtypeStruct((4,) + packed.shape, packed.dtype),
        scratch_shapes=[pltpu.SemaphoreType.DMA((3,)), pltpu.SemaphoreType.DMA((3,)), pltpu.SemaphoreType.DMA],
    )(packed)


def exchange_grads(send, small):
    hbm = pl.BlockSpec(memory_space=pl.ANY)

    def body(send_ref, small_ref, recv_ref, all_ref, send_sems, recv_sems, ssend_sems, srecv_sems, local_sem):
        x, y, c = _place()
        me = 4 * x + 2 * y + c
        chips = [(1 - x, y), (x, 1 - y), (1 - x, 1 - y)]
        local = pltpu.make_async_copy(small_ref, all_ref.at[me], local_sem)
        local.start()
        sends = []
        for k, (px, py) in enumerate(chips):
            cp = pltpu.make_async_remote_copy(src_ref=send_ref.at[2 * px + py], dst_ref=recv_ref.at[k],
                                              send_sem=send_sems.at[k], recv_sem=recv_sems.at[k],
                                              device_id=(px, py, c), device_id_type=MESH)
            cp.start()
            sends.append(cp)
        others = [(x ^ ((r >> 2) & 1), y ^ ((r >> 1) & 1), c ^ (r & 1)) for r in range(1, 8)]
        for r, peer in enumerate(others):
            cp = pltpu.make_async_remote_copy(src_ref=small_ref, dst_ref=all_ref.at[me], send_sem=ssend_sems.at[r],
                                              recv_sem=srecv_sems.at[r], device_id=peer, device_id_type=MESH)
            cp.start()
            sends.append(cp)
        for k, (px, py) in enumerate(chips):
            pltpu.make_async_remote_copy(src_ref=send_ref.at[2 * px + py], dst_ref=recv_ref.at[k],
                                         send_sem=send_sems.at[k], recv_sem=recv_sems.at[k],
                                         device_id=(px, py, c), device_id_type=MESH).wait_recv()
        for r, (px, py, pc) in enumerate(others):
            pltpu.make_async_remote_copy(src_ref=small_ref, dst_ref=all_ref.at[4 * px + 2 * py + pc],
                                         send_sem=ssend_sems.at[r], recv_sem=srecv_sems.at[r],
                                         device_id=(px, py, pc), device_id_type=MESH).wait_recv()
        for cp in sends:
            cp.wait_send()
        local.wait()

    return pl.pallas_call(
        body, name="exchange_grads", in_specs=[hbm, hbm], out_specs=[hbm, hbm],
        out_shape=[jax.ShapeDtypeStruct((3,) + send.shape[1:], send.dtype),
                   jax.ShapeDtypeStruct((8,) + small.shape, small.dtype)],
        scratch_shapes=[pltpu.SemaphoreType.DMA((3,)), pltpu.SemaphoreType.DMA((3,)),
                        pltpu.SemaphoreType.DMA((7,)), pltpu.SemaphoreType.DMA((7,)), pltpu.SemaphoreType.DMA],
    )(send, small)


def swap_with_sibling(part):
    hbm = pl.BlockSpec(memory_space=pl.ANY)

    def body(src_ref, out_ref, send_sem, recv_sem):
        x, y, c = _place()
        cp = pltpu.make_async_remote_copy(src_ref=src_ref, dst_ref=out_ref, send_sem=send_sem, recv_sem=recv_sem,
                                          device_id=(x, y, 1 - c), device_id_type=MESH)
        cp.start()
        cp.wait()

    return pl.pallas_call(
        body, name="swap_with_sibling", in_specs=[hbm], out_specs=hbm,
        out_shape=jax.ShapeDtypeStruct(part.shape, part.dtype),
        scratch_shapes=[pltpu.SemaphoreType.DMA, pltpu.SemaphoreType.DMA],
    )(part)


def _chips3():
    x, y, c = _place()
    return [(1 - x, y, c), (x, 1 - y, c), (1 - x, 1 - y, c)]


def push_start(name, src, per_chip):
    hbm = pl.BlockSpec(memory_space=pltpu.HBM)
    sem = pl.BlockSpec(memory_space=pltpu.SEMAPHORE)
    rows = src.shape[-2:]

    def body(src_ref, land_ref, send_sems, recv_sems, src_thru, land_thru, token):
        for k, (px, py, pc) in enumerate(_chips3()):
            part = src_ref.at[2 * px + py] if per_chip else src_ref
            pltpu.make_async_remote_copy(src_ref=part, dst_ref=land_ref.at[k], send_sem=send_sems.at[k],
                                         recv_sem=recv_sems.at[k], device_id=(px, py, pc),
                                         device_id_type=MESH).start()
        token[...] = jnp.zeros_like(token)

    send_sems, recv_sems, src_thru, land_thru, token = pl.pallas_call(
        body, name=name,
        out_shape=(pltpu.SemaphoreType.DMA((3,)), pltpu.SemaphoreType.DMA((3,)), pltpu.HBM(src.shape, src.dtype),
                   pltpu.HBM((3,) + rows, src.dtype), jax.ShapeDtypeStruct((8, HEAD), F32)),
        in_specs=(hbm, hbm), out_specs=(sem, sem, hbm, hbm, pl.BlockSpec(memory_space=pltpu.VMEM)),
        input_output_aliases={0: 2, 1: 3},
        compiler_params=pltpu.CompilerParams(has_side_effects=pltpu.SideEffectType.DATAFLOW_SIDE_EFFECTING),
    )(pltpu.with_memory_space_constraint(src, pltpu.HBM),
      pltpu.with_memory_space_constraint(lax.empty((3,) + rows, src.dtype), pltpu.HBM))
    return (send_sems, recv_sems, src_thru, land_thru), token


def push_wait(name, handle, after, per_chip):
    send_sems, recv_sems, src_thru, land_thru = handle
    hbm = pl.BlockSpec(memory_space=pltpu.HBM)
    sem = pl.BlockSpec(memory_space=pltpu.SEMAPHORE)

    def body(src_ref, land_ref, send_sems_ref, recv_sems_ref, after_ref, src_dead, got_ref):
        for k, (px, py, pc) in enumerate(_chips3()):
            part = src_ref.at[2 * px + py] if per_chip else src_ref
            cp = pltpu.make_async_remote_copy(src_ref=part, dst_ref=land_ref.at[k], send_sem=send_sems_ref.at[k],
                                              recv_sem=recv_sems_ref.at[k], device_id=(px, py, pc),
                                              device_id_type=MESH)
            cp.wait_send()
            cp.wait_recv()

    return pl.pallas_call(
        body, name=name,
        out_shape=(pltpu.HBM(src_thru.shape, src_thru.dtype), pltpu.HBM(land_thru.shape, land_thru.dtype)),
        in_specs=(hbm, hbm, sem, sem, pl.BlockSpec(memory_space=pl.ANY)), out_specs=(hbm, hbm),
        input_output_aliases={0: 0, 1: 1},
        compiler_params=pltpu.CompilerParams(has_side_effects=pltpu.SideEffectType.DATAFLOW_SIDE_EFFECTING),
    )(src_thru, land_thru, send_sems, recv_sems, after)


def by_chip(own, landed, my_chip):
    by_rel = jnp.stack([own, landed[1], landed[0], landed[2]])
    return [lax.dynamic_index_in_dim(by_rel, jnp.bitwise_xor(s, my_chip), axis=0, keepdims=False) for s in range(4)]


def adamw(name, w, g, m, v):
    r, c = w.shape
    tr = r if r * c <= 65536 else _tile(r, 128, 8)

    def body(w_ref, g_ref, m_ref, v_ref, d_ref, nm_ref, nv_ref):
        gv = g_ref[...]
        m_new = ADAM_B1 * m_ref[...] + (1.0 - ADAM_B1) * gv
        v_new = ADAM_B2 * v_ref[...] + (1.0 - ADAM_B2) * (gv * gv)
        m_hat = m_new / (1.0 - ADAM_B1 ** ADAM_STEP)
        v_hat = v_new / (1.0 - ADAM_B2 ** ADAM_STEP)
        d_ref[...] = -ADAM_LR * (m_hat / (jnp.sqrt(v_hat) + ADAM_EPS) + ADAM_WD * w_ref[...])
        nm_ref[...] = m_new
        nv_ref[...] = v_new

    spec = pl.BlockSpec((tr, c), lambda i: (i, 0))
    return pl.pallas_call(
        body, name=name, grid=(r // tr,), in_specs=[spec] * 4, out_specs=[spec] * 3,
        out_shape=[jax.ShapeDtypeStruct((r, c), F32)] * 3,
        compiler_params=pltpu.CompilerParams(dimension_semantics=("arbitrary",)),
    )(w, g, m, v)


def _pack_rows(shapes):
    rows = [(r * c) // PACK_W for (r, c) in shapes]
    for (r, c) in shapes:
        assert (r * c) % PACK_W == 0
    total = sum(rows)
    return rows, -(-total // 16) * 16


def pack_shard(parts, total_rows):
    flat = jnp.concatenate([p.reshape(-1, PACK_W) for p in parts], axis=0)
    return jnp.pad(flat, ((0, total_rows - flat.shape[0]), (0, 0)))


def split_full(name, full, s):
    if name in COL_SHARDED:
        c = full.shape[1] // 4
        return full[:, s * c:(s + 1) * c]
    r = full.shape[0] // 4
    return full[s * r:(s + 1) * r]


def join_shards(name, shards):
    return jnp.concatenate(shards, axis=1 if name in COL_SHARDED else 0)


def kernel(x, meta_tokens, w_in, b_gate, lb_logits, hg_norm_g, w_hg_o, q_a_norm_g, w_q_b, kv_a_norm_g, w_kv_b, w_mla_o, w_out, mix_pre_g, mix_post_g, ffn_pre_g, ffn_post_g, w_ffn_in, w_ffn_out, loss_target, m_meta_tokens, m_w_in, m_b_gate, m_lb_logits, m_hg_norm_g, m_w_hg_o, m_q_a_norm_g, m_w_q_b, m_kv_a_norm_g, m_w_kv_b, m_w_mla_o, m_w_out, m_mix_pre_g, m_mix_post_g, m_ffn_pre_g, m_ffn_post_g, m_w_ffn_in, m_w_ffn_out, v_meta_tokens, v_w_in, v_b_gate, v_lb_logits, v_hg_norm_g, v_w_hg_o, v_q_a_norm_g, v_w_q_b, v_kv_a_norm_g, v_w_kv_b, v_w_mla_o, v_w_out, v_mix_pre_g, v_mix_post_g, v_ffn_pre_g, v_ffn_post_g, v_w_ffn_in, v_w_ffn_out):
    wts = dict(meta_tokens=meta_tokens, w_in=w_in[0], b_gate=b_gate, lb_logits=lb_logits, hg_norm_g=hg_norm_g,
               w_hg_o=w_hg_o[0], q_a_norm_g=q_a_norm_g, w_q_b=w_q_b[0], kv_a_norm_g=kv_a_norm_g, w_kv_b=w_kv_b[0],
               w_mla_o=w_mla_o[0], w_out=w_out[0], mix_pre_g=mix_pre_g, mix_post_g=mix_post_g, ffn_pre_g=ffn_pre_g,
               ffn_post_g=ffn_post_g, w_ffn_in=w_ffn_in[0], w_ffn_out=w_ffn_out[0])
    mom_m = dict(meta_tokens=m_meta_tokens, w_in=m_w_in[0], b_gate=m_b_gate, lb_logits=m_lb_logits,
                 hg_norm_g=m_hg_norm_g, w_hg_o=m_w_hg_o[0], q_a_norm_g=m_q_a_norm_g, w_q_b=m_w_q_b[0],
                 kv_a_norm_g=m_kv_a_norm_g, w_kv_b=m_w_kv_b[0], w_mla_o=m_w_mla_o[0], w_out=m_w_out[0],
                 mix_pre_g=m_mix_pre_g, mix_post_g=m_mix_post_g, ffn_pre_g=m_ffn_pre_g, ffn_post_g=m_ffn_post_g,
                 w_ffn_in=m_w_ffn_in[0], w_ffn_out=m_w_ffn_out[0])
    mom_v = dict(meta_tokens=v_meta_tokens, w_in=v_w_in[0], b_gate=v_b_gate, lb_logits=v_lb_logits,
                 hg_norm_g=v_hg_norm_g, w_hg_o=v_w_hg_o[0], q_a_norm_g=v_q_a_norm_g, w_q_b=v_w_q_b[0],
                 kv_a_norm_g=v_kv_a_norm_g, w_kv_b=v_w_kv_b[0], w_mla_o=v_w_mla_o[0], w_out=v_w_out[0],
                 mix_pre_g=v_mix_pre_g, mix_post_g=v_mix_post_g, ffn_pre_g=v_ffn_pre_g, ffn_post_g=v_ffn_post_g,
                 w_ffn_in=v_w_ffn_in[0], w_ffn_out=v_w_ffn_out[0])

    bl, seq, d = x.shape
    lp = PAD_FRONT + N_META + seq
    t_rows = bl * lp
    nh = d // HEAD
    ql, kvl = wts["w_q_b"].shape[0], wts["w_kv_b"].shape[0]
    nm = (4 * wts["w_mla_o"].shape[0]) // HEAD
    ffn = 4 * wts["w_ffn_out"].shape[0]
    mla_w = ql + kvl + HEAD
    assert ql == kvl and ql % HEAD == 0 and seq % SEQ_BLOCK == 0 and d % HEAD == 0
    scale = (HEAD + ROPE) ** -0.5
    my_chip = 2 * lax.axis_index("x") + lax.axis_index("y")

    mcols = meta_tokens.shape[1]
    meta_all = gather_shards(meta_tokens)
    meta_full = jnp.concatenate([meta_all[s] for s in range(4)], axis=1)

    def start_gather(name, names, order_after):
        shapes = [wts[n].shape for n in names]
        rows, total = _pack_rows(shapes)
        own = pack_shard([_bf(wts[n]) for n in names], total)
        if order_after is not None:
            own = own + order_after[0, 0].astype(BF16)
        handle, token = push_start(name, own, per_chip=False)
        return handle, token, rows, shapes

    def finish_gather(name, names, started, after):
        handle, _, rows, shapes = started
        own, landed = push_wait(name, handle, after, per_chip=False)
        slots = by_chip(own, landed, my_chip)
        out, off = {}, 0
        for n, nrows, (r, c) in zip(names, rows, shapes):
            out[n] = join_shards(n, [slots[s][off:off + nrows].reshape(r, c) for s in range(4)])
            off += nrows
        return out

    rest_names = tuple(n for n in BIG if n != "w_in")
    gather_1 = start_gather("gather_w_in_start", ("w_in",), None)
    gather_2 = start_gather("gather_rest_start", rest_names, gather_1[1])

    h0 = jnp.concatenate([jnp.zeros((bl, PAD_FRONT, d), F32), jnp.broadcast_to(meta_full[None], (bl, N_META, d)), x],
                         axis=1).reshape(t_rows, d)
    tgt = jnp.concatenate([jnp.zeros((bl, PAD_FRONT + N_META, d), F32), loss_target], axis=1).reshape(t_rows, d)
    pos = (jnp.arange(lp, dtype=jnp.int32) - PAD_FRONT).astype(F32)
    inv_freq = 1.0 / (ROPE_THETA ** (jnp.arange(0, ROPE, 2, dtype=F32) / ROPE))
    ang = pos[:, None] * inv_freq[None, :]
    zeros32 = jnp.zeros((lp, ROPE_HALF), F32)
    zeros64 = jnp.zeros((lp, HEAD - ROPE), F32)
    t_cos = jnp.concatenate([jnp.cos(ang), jnp.cos(ang), zeros64], axis=1)
    t_up = jnp.concatenate([zeros32, jnp.sin(ang), zeros64], axis=1)
    t_dn = jnp.concatenate([-jnp.sin(ang), zeros32, zeros64], axis=1)
    real = jnp.broadcast_to((jnp.arange(lp) >= PAD_FRONT + N_META).astype(F32)[:, None], (lp, d))
    lb_soft = jax.nn.softmax(lb_logits.astype(F32), axis=0)
    lb = lb_soft[0:1]

    (u1,) = rowwise("norm_mix_pre", lambda h, g: _rms(h, g), [(h0, d, 0)], [], [mix_pre_g + gather_2[1][0, 0]],
                    [(d, BF16)])
    full = finish_gather("gather_w_in_wait", ("w_in",), gather_1, u1)
    w_main = jnp.concatenate([full["w_in"][:, :4 * d], full["w_in"][:, -2 * d:]], axis=1)
    w_mla = jnp.pad(full["w_in"][:, 4 * d:4 * d + ql + kvl + ROPE], ((0, 0), (0, HEAD - ROPE)))
    proj_main = matmul("proj_main", u1, w_main, "nn", out_dtype=BF16)
    proj_mla = matmul("proj_mla", u1, w_mla, "nn", out_dtype=BF16)
    hg_consts = _hg_constants()
    o_scan, states, a_mats = hgrn_fwd(proj_main, lb, hg_consts, bl, lp, d)

    def hg_out_fn(o, hg, g):
        return jnp.concatenate([_rms(o[:, h * HEAD:(h + 1) * HEAD], g) for h in range(nh)], axis=1) * _silu(hg)

    (o_hg,) = rowwise("hgrn_out", hg_out_fn, [(o_scan, d, 0), (proj_main, d, 3)], [], [hg_norm_g], [(d, BF16)])
    full.update(finish_gather("gather_rest_wait", rest_names, gather_2, o_hg))
    w_qb = jnp.pad(full["w_q_b"].reshape(ql, nm, HEAD + ROPE), ((0, 0), (0, 0), (0, QK_PAD - HEAD - ROPE))
                   ).reshape(ql, nm * QK_PAD)
    w_kvb = full["w_kv_b"]
    y_a = matmul("y_a", o_hg, _bf(full["w_hg_o"]), "nn", out_dtype=BF16)

    qn, kvn = rowwise("mla_norms", lambda cq, ckv, gq, gk: (_rms(cq, gq), _rms(ckv, gk)),
                      [(proj_mla, ql, 0), (proj_mla, kvl, 1)], [], [q_a_norm_g, kv_a_norm_g],
                      [(ql, BF16), (kvl, BF16)])
    q_full = matmul("q_up", qn, w_qb, "nn", out_dtype=BF16)
    kv_full = matmul("kv_up", kvn, w_kvb, "nn", out_dtype=BF16)

    def mla_prep_fn(qf, kvf, kpe, cos, s_up, s_dn):
        kpe_r = _rope(kpe, cos, s_up, s_dn)
        qs, ks, vs = [], [], []
        for h in range(nm):
            qs += [qf[:, h * QK_PAD:h * QK_PAD + HEAD], _rope(qf[:, h * QK_PAD + HEAD:(h + 1) * QK_PAD], cos, s_up, s_dn)]
            ks += [kvf[:, h * QK_PAD:h * QK_PAD + HEAD], kpe_r]
            vs += [kvf[:, h * QK_PAD + HEAD:(h + 1) * QK_PAD]]
        return jnp.concatenate(qs, axis=1), jnp.concatenate(ks, axis=1), jnp.concatenate(vs, axis=1)

    kpe_blk = (ql + kvl) // HEAD
    q_cat, k_cat, v_att = rowwise("mla_prep", mla_prep_fn,
                                  [(q_full, nm * QK_PAD, 0), (kv_full, nm * QK_PAD, 0), (proj_mla, HEAD, kpe_blk)],
                                  [t_cos, t_up, t_dn], [], [(nm * QK_PAD, BF16), (nm * QK_PAD, BF16), (nm * HEAD, BF16)])
    at = _attn_tile(lp)
    v_t = v_att.reshape(bl, lp // at, at, nm, HEAD).transpose(0, 3, 1, 4, 2)
    k_t = k_cat.reshape(bl, lp // at, at, nm, QK_PAD).transpose(0, 3, 1, 4, 2)
    o_mla, lse = attn_fwd_t(q_cat, k_cat, v_t, bl, lp, nm, scale)
    y_b = matmul("y_b", o_mla, _bf(full["w_mla_o"]), "nn", out_dtype=BF16)

    def gate_fn(ya, yb, ga, gb, bias):
        return _sigmoid(ga + bias[:, :d]) * ya + _sigmoid(gb + bias[:, d:]) * yb

    (z,) = rowwise("gate_mix", gate_fn, [(y_a, d, 0), (y_b, d, 0), (proj_main, d, 4), (proj_main, d, 5)], [],
                   [b_gate], [(d, BF16)])
    mixed = matmul("mixed", z, _bf(full["w_out"]), "nn")

    def mid_fn(h, mx, g_post, g_pre):
        h1 = h + _rms(mx, g_post)
        return h1, _rms(h1, g_pre)

    h1, u2 = rowwise("norm_mid", mid_fn, [(h0, d, 0), (mixed, d, 0)], [], [mix_post_g, ffn_pre_g],
                     [(d, F32), (d, BF16)])
    gu = matmul("ffn_in", u2, _bf(full["w_ffn_in"]), "nn", out_dtype=BF16)
    (act,) = rowwise("swiglu", lambda gt, up: _silu(gt) * up, [(gu, ffn, 0), (gu, ffn, 1)], [], [], [(ffn, BF16)])
    f_out = matmul("ffn_out", act, _bf(full["w_ffn_out"]), "nn")

    def loss_fn(h1v, fv, tg, realv, g_post):
        h2 = h1v + _rms(fv, g_post)
        diff = (h2 - tg) * realv
        part = jnp.broadcast_to(0.5 * jnp.sum(diff * diff, keepdims=True) / d, (1, HEAD))
        dy = diff / d
        df, dg = _rms_bwd(fv, g_post, dy)
        return dy, df, part, dg

    dy, df, loss_part, g_ffn_post = rowwise("loss_head", loss_fn, [(h1, d, 0), (f_out, d, 0), (tgt, d, 0)], [real],
                                            [ffn_post_g], [(d, F32), (d, BF16)], [(1, HEAD), (1, d)])
    grads = {}
    d_act = matmul("d_act", df, _bf(full["w_ffn_out"]), "nt", out_dtype=BF16)
    grads["w_ffn_out"] = matmul("gw_ffn_out", act, df, "tn")

    def swiglu_bwd_fn(gt, up, da):
        return jnp.concatenate([da * up * _silu_grad(gt), da * _silu(gt)], axis=1)

    (dgu,) = rowwise("swiglu_bwd", swiglu_bwd_fn, [(gu, ffn, 0), (gu, ffn, 1), (d_act, ffn, 0)], [], [],
                     [(2 * ffn, BF16)])
    du2 = matmul("d_u2", dgu, _bf(full["w_ffn_in"]), "nt", out_dtype=BF16)
    grads["w_ffn_in"] = matmul("gw_ffn_in", u2, dgu, "tn")

    def mid_bwd_fn(dyv, h1v, du2v, mx, g_pre, g_post):
        dx, dg_pre = _rms_bwd(h1v, g_pre, du2v)
        dh1 = dyv + dx
        dmx, dg_post = _rms_bwd(mx, g_post, dh1)
        return dh1, dmx, dg_pre, dg_post

    dh1, dmixed, g_ffn_pre, g_mix_post = rowwise("norm_mid_bwd", mid_bwd_fn,
                                                 [(dy, d, 0), (h1, d, 0), (du2, d, 0), (mixed, d, 0)], [],
                                                 [ffn_pre_g, mix_post_g], [(d, F32), (d, BF16)], [(1, d), (1, d)])
    dz = matmul("d_z", dmixed, _bf(full["w_out"]), "nt", out_dtype=BF16)
    grads["w_out"] = matmul("gw_out", z, dmixed, "tn")

    def gate_bwd_fn(dzv, ya, yb, ga, gb, bias):
        sa, sb = _sigmoid(ga + bias[:, :d]), _sigmoid(gb + bias[:, d:])
        dga = dzv * ya * sa * (1.0 - sa)
        dgb = dzv * yb * sb * (1.0 - sb)
        dgates = jnp.concatenate([dga, dgb], axis=1)
        return dzv * sa, dzv * sb, dgates, jnp.sum(dgates, axis=0, keepdims=True)

    dy_a, dy_b, dgates, g_b_gate = rowwise("gate_mix_bwd", gate_bwd_fn,
                                           [(dz, d, 0), (y_a, d, 0), (y_b, d, 0), (proj_main, d, 4), (proj_main, d, 5)],
                                           [], [b_gate], [(d, BF16), (d, BF16), (2 * d, BF16)], [(1, 2 * d)])
    do_hg = matmul("d_o_hg", dy_a, _bf(full["w_hg_o"]), "nt", out_dtype=BF16)
    grads["w_hg_o"] = matmul("gw_hg_o", o_hg, dy_a, "tn")
    do_mla = matmul("d_o_mla", dy_b, _bf(full["w_mla_o"]), "nt", out_dtype=BF16)
    grads["w_mla_o"] = matmul("gw_mla_o", o_mla, dy_b, "tn")

    early = ("w_hg_o", "w_mla_o", "w_out", "w_ffn_in", "w_ffn_out")
    late = ("w_in", "w_q_b", "w_kv_b")

    def pack_grads(names):
        shapes = [wts[n].shape for n in names]
        rows, total = _pack_rows(shapes)
        send = jnp.stack([pack_shard([split_full(n, grads[n], s) for n in names], total) for s in range(4)])
        return send, lax.dynamic_index_in_dim(send, my_chip, axis=0, keepdims=False), rows, shapes, total

    send_a, mine_a, rows_a, shapes_a, total_a = pack_grads(early)
    grads_a, token_a = push_start("grads_early_start", _bf(send_a), per_chip=True)

    def hg_out_bwd_fn(do, o, hg, g):
        sg = _silu(hg)
        dn = do * sg
        dos, dgs, ons = [], 0.0, []
        for h in range(nh):
            sl = slice(h * HEAD, (h + 1) * HEAD)
            dx, dg = _rms_bwd(o[:, sl], g, dn[:, sl])
            dos.append(dx)
            dgs = dgs + dg
            ons.append(_rms(o[:, sl], g))
        dhg = do * jnp.concatenate(ons, axis=1) * _silu_grad(hg)
        return jnp.concatenate(dos, axis=1), dhg, dgs

    do_scan, dhg, g_hg_norm = rowwise("hgrn_out_bwd", hg_out_bwd_fn, [(do_hg, d, 0), (o_scan, d, 0), (proj_main, d, 3)],
                                      [], [hg_norm_g], [(d, F32), (d, BF16)], [(1, HEAD)])
    dhq, dhf, dhi, g_lb = hgrn_bwd(proj_main, lb + token_a[0, 0], hg_consts, states, a_mats, do_scan, bl, lp, d)

    dq_cat, dk_cat, dv_att = attn_bwd_t(q_cat, k_cat, k_t, v_att, o_mla, do_mla, lse, bl, lp, nm, scale)

    def mla_prep_bwd_fn(dqc, dkc, dvv, cos, s_up, s_dn):
        dqs, dkvs, dkpe = [], [], 0.0
        for h in range(nm):
            dqs += [dqc[:, h * QK_PAD:h * QK_PAD + HEAD],
                    _rope_bwd(dqc[:, h * QK_PAD + HEAD:(h + 1) * QK_PAD], cos, s_up, s_dn)]
            dkvs += [dkc[:, h * QK_PAD:h * QK_PAD + HEAD], dvv[:, h * HEAD:(h + 1) * HEAD]]
            dkpe = dkpe + dkc[:, h * QK_PAD + HEAD:(h + 1) * QK_PAD]
        return jnp.concatenate(dqs, axis=1), jnp.concatenate(dkvs, axis=1), _rope_bwd(dkpe, cos, s_up, s_dn)

    dq_full, dkv_full, dkpe = rowwise("mla_prep_bwd", mla_prep_bwd_fn,
                                      [(dq_cat, nm * QK_PAD, 0), (dk_cat, nm * QK_PAD, 0), (dv_att, nm * HEAD, 0)],
                                      [t_cos, t_up, t_dn], [],
                                      [(nm * QK_PAD, BF16), (nm * QK_PAD, BF16), (HEAD, F32)])
    dqn = matmul("d_qn", dq_full, w_qb, "nt", out_dtype=BF16)
    g_wqb = matmul("gw_q_b", qn, dq_full, "tn")
    grads["w_q_b"] = g_wqb.reshape(ql, nm, QK_PAD)[:, :, :HEAD + ROPE].reshape(ql, nm * (HEAD + ROPE))
    dkvn = matmul("d_kvn", dkv_full, w_kvb, "nt", out_dtype=BF16)
    grads["w_kv_b"] = matmul("gw_kv_b", kvn, dkv_full, "tn")

    def mla_norms_bwd_fn(dqnv, dkvnv, cq, ckv, dkpev, gq, gk):
        dcq, dgq = _rms_bwd(cq, gq, dqnv)
        dckv, dgk = _rms_bwd(ckv, gk, dkvnv)
        return jnp.concatenate([dcq, dckv, dkpev], axis=1), dgq, dgk

    dmla, g_q_norm, g_kv_norm = rowwise("mla_norms_bwd", mla_norms_bwd_fn,
                                        [(dqn, ql, 0), (dkvn, kvl, 0), (proj_mla, ql, 0), (proj_mla, kvl, 1),
                                         (dkpe, HEAD, 0)], [], [q_a_norm_g, kv_a_norm_g],
                                        [(mla_w, BF16)], [(1, ql), (1, kvl)])

    w_main_bf = w_main
    pieces = [(dhq, w_main_bf[:, 0:d]), (dhf, w_main_bf[:, d:2 * d]), (dhi, w_main_bf[:, 2 * d:3 * d]),
              (dhg, w_main_bf[:, 3 * d:4 * d]), (dgates, w_main_bf[:, 4 * d:6 * d]), (dmla, w_mla)]
    du1 = matmul("d_u1", [dp for dp, _ in pieces], [wp for _, wp in pieces], "nt")
    gw_parts = [matmul(f"gw_in_{k}", u1, dp, "tn") for k, (dp, _) in enumerate(pieces)]
    grads["w_in"] = jnp.concatenate(gw_parts[:4] + [gw_parts[5][:, :ql + kvl + ROPE], gw_parts[4]], axis=1)

    def first_bwd_fn(dh1v, h, du1v, g):
        dx, dg = _rms_bwd(h, g, du1v)
        return dh1v + dx, dg

    dh0, g_mix_pre = rowwise("norm_mix_pre_bwd", first_bwd_fn, [(dh1, d, 0), (h0, d, 0), (du1, d, 0)], [],
                             [mix_pre_g], [(d, F32)], [(1, d)])
    dh0 = dh0.reshape(bl, lp, d)
    grad_x = dh0[:, PAD_FRONT + N_META:]

    send_b, mine_b, rows_b, shapes_b, total_b = pack_grads(late)
    p0 = lb_soft[0:1]
    g_lb_logits = jnp.concatenate([g_lb * p0 * (1.0 - p0), -g_lb * p0 * (1.0 - p0)], axis=0)

    def row_of(vec):
        return vec.reshape(-1, d) if vec.size >= d else jnp.pad(vec.reshape(1, -1), ((0, 0), (0, d - vec.size)))

    small_parts = dict(b_gate=g_b_gate, lb_logits=g_lb_logits, hg_norm_g=g_hg_norm, q_a_norm_g=g_q_norm,
                       kv_a_norm_g=g_kv_norm, mix_pre_g=g_mix_pre, mix_post_g=g_mix_post, ffn_pre_g=g_ffn_pre,
                       ffn_post_g=g_ffn_post)
    g_meta = jnp.sum(dh0[:, PAD_FRONT:PAD_FRONT + N_META], axis=0)
    small_rows = [row_of(small_parts[n]) for n in SMALL] + [row_of(g_meta)]
    n_small = sum(r.shape[0] for r in small_rows)
    small = jnp.pad(jnp.concatenate(small_rows, axis=0), ((0, -(-n_small // 8) * 8 - n_small), (0, 0)))
    recv_b, all_small = exchange_grads(_bf(send_b), small)
    _, recv_a = push_wait("grads_early_wait", grads_a, all_small, per_chip=True)

    def sum_chips(name, mine, recv, total):
        tr = _tile(total, 512, 16)
        recv2 = recv.reshape(3 * total, PACK_W)
        return rowwise(name, lambda a, r0, r1, r2: a + r0 + r1 + r2,
                       [(mine, PACK_W, 0)] + [(recv2, PACK_W, 0, k * (total // tr)) for k in range(3)],
                       [], [], [(PACK_W, F32)], tm=tr)[0]

    part_sum = jnp.concatenate([sum_chips("sum_chips_early", mine_a, recv_a, total_a),
                                sum_chips("sum_chips_late", mine_b, recv_b, total_b)], axis=0)
    rt = _tile(total_a + total_b, 512, 16)
    sib_sum = swap_with_sibling(part_sum)
    (g_packed,) = rowwise("sum_cores", lambda a, b: a + b, [(part_sum, PACK_W, 0), (sib_sum, PACK_W, 0)], [], [],
                          [(PACK_W, F32)], tm=rt)
    small_t = small.shape[0]

    def sum8_fn(*slabs):
        acc = slabs[0]
        for s in slabs[1:]:
            acc = acc + s
        return acc

    (g_small,) = rowwise("sum_small", sum8_fn, [(all_small.reshape(8 * small_t, d), d, 0, k) for k in range(8)],
                         [], [], [(d, F32)], tm=small_t, n_rows=small_t)

    g_final = {}
    off = 0
    for names, rows_g, shapes_g, total in ((early, rows_a, shapes_a, total_a), (late, rows_b, shapes_b, total_b)):
        start = off
        for n, rows, (r, c) in zip(names, rows_g, shapes_g):
            g_final[n] = g_packed[off:off + rows].reshape(r, c)
            off += rows
        off = start + total
    off = 0
    for n, part in zip(SMALL, small_rows[:-1]):
        rows = part.shape[0]
        g_final[n] = g_small[off:off + rows, :d].reshape(-1)[:wts[n].size].reshape(wts[n].shape)
        off += rows
    mcols = meta_tokens.shape[1]
    g_final["meta_tokens"] = lax.dynamic_slice_in_dim(g_small[off:off + N_META, :d], my_chip * mcols, mcols, axis=1)

    delta, new_m, new_v = {}, {}, {}
    for n in WEIGHTS:
        w2 = wts[n].reshape(-1, wts[n].shape[-1])
        dl, mn, vn = adamw("adamw_" + n, w2, g_final[n].reshape(w2.shape), mom_m[n].reshape(w2.shape),
                           mom_v[n].reshape(w2.shape))
        delta[n], new_m[n], new_v[n] = dl, mn, vn

    loss = lax.psum(loss_part[0, 0], ("x", "y", "c"))

    def shaped(n, a):
        return a.reshape((1,) + wts[n].shape) if n in BIG else a.reshape(wts[n].shape)

    return (loss, grad_x, *[shaped(n, g_final[n]) for n in WEIGHTS], *[shaped(n, delta[n]) for n in WEIGHTS],
            *[shaped(n, new_m[n]) for n in WEIGHTS], *[shaped(n, new_v[n]) for n in WEIGHTS])
```

```python
import functools
import math

import jax
import jax.numpy as jnp
from jax import lax
from jax.experimental import pallas as pl
from jax.experimental.pallas import tpu as pltpu

F32 = jnp.float32
BF16 = jnp.bfloat16
MESH = pl.DeviceIdType.MESH

N_META = 16
NORM_EPS = 1e-6
HEAD = 128
ROPE = 64
ROPE_HALF = ROPE // 2
QK_PAD = 2 * HEAD
ROPE_THETA = 10000.0
SEQ_BLOCK = 256
PAD_FRONT = SEQ_BLOCK - N_META
NEG = -1e30
VMEM_LIMIT = 56 * 1024 * 1024
ATTN_HEADS_PER_STEP = 1
ATTN_TILE_MAX = 768

ADAM_LR, ADAM_B1, ADAM_B2, ADAM_EPS, ADAM_WD, ADAM_STEP = 0.001, 0.9, 0.999, 1e-08, 0.01, 10

BIG = ("w_in", "w_hg_o", "w_q_b", "w_kv_b", "w_mla_o", "w_out", "w_ffn_in", "w_ffn_out")
COL_SHARDED = ("w_in", "w_q_b", "w_kv_b", "w_ffn_in")
SMALL = ("b_gate", "lb_logits", "hg_norm_g", "q_a_norm_g", "kv_a_norm_g", "mix_pre_g", "mix_post_g",
         "ffn_pre_g", "ffn_post_g")
WEIGHTS = ("meta_tokens", "w_in", "b_gate", "lb_logits", "hg_norm_g", "w_hg_o", "q_a_norm_g", "w_q_b",
           "kv_a_norm_g", "w_kv_b", "w_mla_o", "w_out", "mix_pre_g", "mix_post_g", "ffn_pre_g", "ffn_post_g",
           "w_ffn_in", "w_ffn_out")


def _tile(n, cap, unit=128):
    if n <= cap:
        return n
    best = None
    for t in range(unit, cap + 1, unit):
        if n % t == 0:
            best = t
    assert best is not None, (n, cap, unit)
    return best


def _sigmoid(x):
    return 1.0 / (1.0 + jnp.exp(-x))


def _bf(x):
    return x.astype(BF16)


def rowwise(name, fn, row_ins, seq_tabs, consts, row_outs, acc_outs=(), tm=SEQ_BLOCK, n_rows=None):
    t_rows = row_ins[0][0].shape[0] if n_rows is None else n_rows
    nt = t_rows // tm
    assert t_rows % tm == 0
    n_in = len(row_ins) + len(seq_tabs) + len(consts)
    n_row = len(row_outs)

    def body(*refs):
        vals = [r[...].astype(F32) for r in refs[:n_in]]
        res = fn(*vals)
        if not isinstance(res, (tuple, list)):
            res = (res,)
        outs = refs[n_in:]
        for k in range(n_row):
            outs[k][...] = res[k].astype(outs[k].dtype)
        if acc_outs:
            @pl.when(pl.program_id(0) == 0)
            def _():
                for k in range(len(acc_outs)):
                    outs[n_row + k][...] = jnp.zeros_like(outs[n_row + k])

            for k in range(len(acc_outs)):
                outs[n_row + k][...] += res[n_row + k]

    row_ins = [tuple(e) + (0,) * (4 - len(e)) for e in row_ins]
    in_specs = [pl.BlockSpec((tm, w), functools.partial(lambda i, j, ro: (i + ro, j), j=j, ro=ro))
                for (_, w, j, ro) in row_ins]
    for tab in seq_tabs:
        per = tab.shape[0] // tm
        in_specs.append(pl.BlockSpec((tm, tab.shape[1]), functools.partial(lambda i, per: (i % per, 0), per=per)))
    for c in consts:
        in_specs.append(pl.BlockSpec(c.shape, lambda i: (0, 0)))
    out_specs = [pl.BlockSpec((tm, w), lambda i: (i, 0)) for (w, _) in row_outs]
    out_specs += [pl.BlockSpec(s, lambda i: (0, 0)) for s in acc_outs]
    out_shape = [jax.ShapeDtypeStruct((t_rows, w), dt) for (w, dt) in row_outs]
    out_shape += [jax.ShapeDtypeStruct(s, F32) for s in acc_outs]
    res = pl.pallas_call(
        body, name=name, grid=(nt,), in_specs=in_specs, out_specs=out_specs, out_shape=out_shape,
        compiler_params=pltpu.CompilerParams(dimension_semantics=("arbitrary",)),
    )(*[e[0] for e in row_ins], *seq_tabs, *consts)
    return res


def matmul(name, a, b, mode, out_dtype=F32):
    if mode != "tn":
        return _matmul_resident(name, a if isinstance(a, (list, tuple)) else [a],
                                b if isinstance(b, (list, tuple)) else [b], mode, out_dtype)
    kdim, m = a.shape
    n = b.shape[1]
    tn = _tile(n, 1536)
    tm, tk = _tile(m, 1408 if tn <= 1024 else 1024), _tile(kdim, 1024)
    nk = kdim // tk

    def body(a_ref, b_ref, o_ref, acc_ref):
        k = pl.program_id(2)

        @pl.when(k == 0)
        def _():
            acc_ref[...] = jnp.zeros_like(acc_ref)

        acc_ref[...] += lax.dot_general(a_ref[...], b_ref[...], TN_DIMS, preferred_element_type=F32)

        @pl.when(k == nk - 1)
        def _():
            o_ref[...] = acc_ref[...].astype(o_ref.dtype)

    return pl.pallas_call(
        body, name=name, grid=(m // tm, n // tn, nk),
        in_specs=[pl.BlockSpec((tk, tm), lambda i, j, k: (k, i)), pl.BlockSpec((tk, tn), lambda i, j, k: (k, j))],
        out_specs=pl.BlockSpec((tm, tn), lambda i, j, k: (i, j)),
        out_shape=jax.ShapeDtypeStruct((m, n), out_dtype),
        scratch_shapes=[pltpu.VMEM((tm, tn), F32)],
        compiler_params=pltpu.CompilerParams(dimension_semantics=("arbitrary", "arbitrary", "arbitrary"),
                                             vmem_limit_bytes=VMEM_LIMIT),
    )(a, b)


def _matmul_resident(name, a_list, b_list, mode, out_dtype):
    m = a_list[0].shape[0]
    n = b_list[0].shape[1] if mode == "nn" else b_list[0].shape[0]
    k_total = sum(a.shape[1] for a in a_list)
    out_bytes = 2 if out_dtype == BF16 else 4
    budget = VMEM_LIMIT - 4 * k_total * n - (6 << 20)
    tm = 512
    while tm > 128 and 2 * tm * (2 * k_total + out_bytes * n) > budget:
        tm //= 2
    tm = _tile(m, tm)
    cn = _tile(n, 1024)
    npairs = len(a_list)

    def body(*refs):
        a_refs, b_refs, o_ref = refs[:npairs], refs[npairs:2 * npairs], refs[2 * npairs]
        for c in range(n // cn):
            acc = None
            for a_ref, b_ref in zip(a_refs, b_refs):
                if mode == "nn":
                    part = jnp.dot(a_ref[...], b_ref[:, pl.ds(c * cn, cn)], preferred_element_type=F32)
                else:
                    part = lax.dot_general(a_ref[...], b_ref[pl.ds(c * cn, cn), :], NT_DIMS,
                                           preferred_element_type=F32)
                acc = part if acc is None else acc + part
            o_ref[:, pl.ds(c * cn, cn)] = acc.astype(o_ref.dtype)

    in_specs = [pl.BlockSpec((tm, a.shape[1]), lambda i: (i, 0)) for a in a_list]
    in_specs += [pl.BlockSpec(b.shape, lambda i: (0, 0)) for b in b_list]
    return pl.pallas_call(
        body, name=name, grid=(m // tm,), in_specs=in_specs,
        out_specs=pl.BlockSpec((tm, n), lambda i: (i, 0)),
        out_shape=jax.ShapeDtypeStruct((m, n), out_dtype),
        compiler_params=pltpu.CompilerParams(dimension_semantics=("arbitrary",), vmem_limit_bytes=VMEM_LIMIT),
    )(*a_list, *b_list)


def _rms(x, g):
    r = lax.rsqrt(jnp.mean(x * x, axis=-1, keepdims=True) + NORM_EPS)
    return x * r * g


def _rms_bwd(x, g, dy):
    r = lax.rsqrt(jnp.mean(x * x, axis=-1, keepdims=True) + NORM_EPS)
    xh = x * r
    dyg = dy * g
    dx = r * (dyg - xh * jnp.mean(dyg * xh, axis=-1, keepdims=True))
    return dx, jnp.sum(dy * xh, axis=0, keepdims=True)


def _silu(x):
    return x * _sigmoid(x)


def _silu_grad(x):
    s = _sigmoid(x)
    return s * (1.0 + x * (1.0 - s))


def _rope(xs, cos, s_up, s_dn):
    return xs * cos + pltpu.roll(xs, ROPE_HALF, 1) * s_up + pltpu.roll(xs, HEAD - ROPE_HALF, 1) * s_dn


def _rope_bwd(dy, cos, s_up, s_dn):
    return dy * cos + pltpu.roll(dy * s_up, HEAD - ROPE_HALF, 1) + pltpu.roll(dy * s_dn, ROPE_HALF, 1)


HG_SUB = 128
HG_LEVELS = 7
HG_E_ROWS = (HG_LEVELS + 1) * HG_SUB
TN_DIMS = (((0,), (0,)), ((), ()))
NT_DIMS = (((1,), (1,)), ((), ()))


def _hg_constants():
    import numpy as np
    n = HG_SUB
    r = np.arange(n)[:, None]
    c = np.arange(n)[None, :]
    cs, ps = [], []
    for lvl in range(HG_LEVELS):
        m = (n // 2) >> lvl
        upper = (r % (2 * m)) >= m
        mid = (r // (2 * m)) * (2 * m) + m - 1
        cs.append(np.where(upper, (c > mid) & (c <= r), (c > r) & (c <= mid)))
        ps.append(((r // (2 * m)) == (c // (2 * m))) & upper & ((c % (2 * m)) < m))
    cs.append(c <= r)
    cs.append(np.ones((8, n), bool))
    cstack = np.concatenate(cs, 0).astype(np.float32)
    pstack = np.concatenate(ps, 0).astype(np.float32)
    pstack_t = np.concatenate([p.T for p in ps], 0).astype(np.float32)
    return (jnp.asarray(cstack, BF16), jnp.asarray(cstack[:HG_E_ROWS].T, BF16), jnp.asarray(pstack, F32),
            jnp.asarray(pstack_t, F32))


def _split_dot(c_bf, x):
    hi = _bf(x)
    lo = _bf(x - hi.astype(F32))
    r2 = jnp.dot(c_bf, jnp.concatenate([hi, lo], axis=1), preferred_element_type=F32)
    return r2[:, :HEAD] + r2[:, HEAD:]


def _hg_gates(hq, hf, lb):
    sq = _sigmoid(hq)
    sg = _sigmoid(hf)
    fg = lb + (1.0 - lb) * sg
    return sq, hq * sq, sg, fg, 1.0 - fg, jnp.log(fg)


def _hg_block_fwd(st, hq, hf, hi, lb, cstack, p_ref):
    _, q, _, _, k, g = _hg_gates(hq, hf, lb)
    v = hi
    e = _split_dot(cstack, g)
    bc = e[HG_LEVELS * HG_SUB:HG_E_ROWS]
    b_last = jnp.tile(e[HG_E_ROWS:], (HG_SUB // 8, 1))
    a = jnp.zeros((HG_SUB, HG_SUB), F32)
    for lvl in range(HG_LEVELS):
        x = jnp.exp(e[lvl * HG_SUB:(lvl + 1) * HG_SUB])
        a = a + p_ref[pl.ds(lvl * HG_SUB, HG_SUB), :] * lax.dot_general(_bf(q * x), _bf(k * x), NT_DIMS,
                                                                          preferred_element_type=F32)
    a_bf = _bf(a)
    diag = jnp.sum(q * k, axis=1, keepdims=True)
    o = (jnp.dot(a_bf, _bf(v), preferred_element_type=F32) + diag * v
         + lax.dot_general(_bf(q * jnp.exp(bc)), _bf(st), NT_DIMS, preferred_element_type=F32))
    kd = k * jnp.exp(b_last - bc)
    st_out = st * jnp.exp(b_last) + lax.dot_general(_bf(v), _bf(kd), TN_DIMS, preferred_element_type=F32)
    return st_out, o, a_bf


def _hg_block_bwd(st, dst_out, do, hq, hf, hi, lb, a_bf, cstack, cstack_t, p_ref, pt_ref):
    sq, q, sg, fg, k, g = _hg_gates(hq, hf, lb)
    v = hi
    e = _split_dot(cstack, g)
    bc = e[HG_LEVELS * HG_SUB:HG_E_ROWS]
    b_last = jnp.tile(e[HG_E_ROWS:], (HG_SUB // 8, 1))
    eb = jnp.exp(bc)
    qb = q * eb
    er = jnp.exp(b_last - bc)
    kd = k * er
    e_last = jnp.exp(b_last)
    do_bf, v_bf, dst_bf = _bf(do), _bf(v), _bf(dst_out)
    da = lax.dot_general(do_bf, v_bf, NT_DIMS, preferred_element_type=F32)
    dat = lax.dot_general(v_bf, do_bf, NT_DIMS, preferred_element_type=F32)
    d_diag = jnp.sum(do * v, axis=1, keepdims=True)
    dv = (lax.dot_general(a_bf, do_bf, TN_DIMS, preferred_element_type=F32)
          + jnp.sum(q * k, axis=1, keepdims=True) * do
          + lax.dot_general(_bf(kd), dst_bf, NT_DIMS, preferred_element_type=F32))
    dqb = jnp.dot(do_bf, _bf(st), preferred_element_type=F32)
    dst = dst_out * e_last + lax.dot_general(do_bf, _bf(qb), TN_DIMS, preferred_element_type=F32)
    dkd = jnp.dot(v_bf, dst_bf, preferred_element_type=F32)
    dq = dqb * eb + d_diag * k
    dk = dkd * er + d_diag * q
    d_last = (jnp.sum(dst_out * st * e_last, axis=0, keepdims=True)
              + jnp.sum(dkd * kd, axis=0, keepdims=True))
    des = []
    for lvl in range(HG_LEVELS):
        x = jnp.exp(e[lvl * HG_SUB:(lvl + 1) * HG_SUB])
        qh, kh = q * x, k * x
        dm = _bf(p_ref[pl.ds(lvl * HG_SUB, HG_SUB), :] * da)
        dmt = _bf(pt_ref[pl.ds(lvl * HG_SUB, HG_SUB), :] * dat)
        dqh = jnp.dot(dm, _bf(kh), preferred_element_type=F32)
        dkh = jnp.dot(dmt, _bf(qh), preferred_element_type=F32)
        dq = dq + dqh * x
        dk = dk + dkh * x
        des.append(dqh * qh + dkh * kh)
    des.append(dqb * qb - dkd * kd)
    dg = _split_dot(cstack_t, jnp.concatenate(des, axis=0)) + d_last
    dfg = dg / fg - dk
    dhq = dq * (sq * (1.0 + hq * (1.0 - sq)))
    dhf = dfg * (1.0 - lb) * sg * (1.0 - sg)
    return dst, dhq, dhf, dv, jnp.sum(dfg * (1.0 - sg), axis=0, keepdims=True)


def hgrn_fwd(proj_main, lb, consts, bl, lp, d):
    nh = d // HEAD
    rows_blk = _tile(lp, 768, SEQ_BLOCK)
    nb = lp // rows_blk
    spb = rows_blk // HG_SUB
    cstack, _, pstack, _ = consts

    def body(hq_ref, hf_ref, hi_ref, lb_ref, c_ref, p_ref, o_ref, st_ref, a_ref, s_ref):
        j = pl.program_id(2)

        @pl.when(j == 0)
        def _():
            s_ref[...] = jnp.zeros_like(s_ref)

        lbv = lb_ref[...]
        cs = c_ref[...]

        def sub(n, carry):
            r = pl.multiple_of(n * HG_SUB, HG_SUB)
            st = s_ref[...]
            st_ref[0, 0, pl.ds(n, 1)] = st[None]
            st_out, o, a_bf = _hg_block_fwd(st, hq_ref[pl.ds(r, HG_SUB), :].astype(F32),
                                            hf_ref[pl.ds(r, HG_SUB), :].astype(F32),
                                            hi_ref[pl.ds(r, HG_SUB), :].astype(F32), lbv, cs, p_ref)
            s_ref[...] = st_out
            o_ref[pl.ds(r, HG_SUB), :] = o
            a_ref[0, 0, pl.ds(n, 1)] = a_bf[None]
            return carry

        lax.fori_loop(0, spb, sub, 0, unroll=2)

    def colspec(off):
        return pl.BlockSpec((rows_blk, HEAD), functools.partial(lambda h, b, j, off: (b * nb + j, off + h), off=off))

    whole = lambda arr: pl.BlockSpec(arr.shape, lambda h, b, j: (0, 0))
    return pl.pallas_call(
        body, name="hgrn_fwd", grid=(nh, bl, nb),
        in_specs=[colspec(0), colspec(nh), colspec(2 * nh), pl.BlockSpec((1, HEAD), lambda h, b, j: (0, h)),
                  whole(cstack), whole(pstack)],
        out_specs=[pl.BlockSpec((rows_blk, HEAD), lambda h, b, j: (b * nb + j, h)),
                   pl.BlockSpec((1, 1, spb, HEAD, HEAD), lambda h, b, j: (b, h, j, 0, 0)),
                   pl.BlockSpec((1, 1, spb, HG_SUB, HG_SUB), lambda h, b, j: (b, h, j, 0, 0))],
        out_shape=[jax.ShapeDtypeStruct((bl * lp, d), F32),
                   jax.ShapeDtypeStruct((bl, nh, lp // HG_SUB, HEAD, HEAD), F32),
                   jax.ShapeDtypeStruct((bl, nh, lp // HG_SUB, HG_SUB, HG_SUB), BF16)],
        scratch_shapes=[pltpu.VMEM((HEAD, HEAD), F32)],
        compiler_params=pltpu.CompilerParams(dimension_semantics=("arbitrary", "arbitrary", "arbitrary")),
    )(proj_main, proj_main, proj_main, lb, cstack, pstack)


def hgrn_bwd(proj_main, lb, consts, states, a_mats, do_scan, bl, lp, d):
    nh = d // HEAD
    rows_blk = _tile(lp, 768, SEQ_BLOCK)
    nb = lp // rows_blk
    spb = rows_blk // HG_SUB
    cstack, cstack_t, pstack, pstack_t = consts

    def body(hq_ref, hf_ref, hi_ref, lb_ref, c_ref, ct_ref, p_ref, pt_ref, st_ref, a_ref, do_ref,
             dq_ref, df_ref, di_ref, dlb_ref, ds_ref):
        b_id, j = pl.program_id(1), pl.program_id(2)
        blk = nb - 1 - j

        @pl.when(j == 0)
        def _():
            ds_ref[...] = jnp.zeros_like(ds_ref)

        @pl.when((j == 0) & (b_id == 0))
        def _():
            dlb_ref[...] = jnp.zeros_like(dlb_ref)

        lbv = lb_ref[...]
        cs = c_ref[...]
        cst = ct_ref[...]

        def sub(i, carry):
            n = spb - 1 - i
            r = pl.multiple_of(n * HG_SUB, HG_SUB)
            dst, dhq, dhf, dhi, dlb = _hg_block_bwd(
                st_ref[0, 0, pl.ds(n, 1)][0], ds_ref[...], do_ref[pl.ds(r, HG_SUB), :],
                hq_ref[pl.ds(r, HG_SUB), :].astype(F32), hf_ref[pl.ds(r, HG_SUB), :].astype(F32),
                hi_ref[pl.ds(r, HG_SUB), :].astype(F32), lbv,
                a_ref[0, 0, pl.ds(n, 1)][0], cs, cst, p_ref, pt_ref)
            ds_ref[...] = dst
            dq_ref[pl.ds(r, HG_SUB), :] = dhq.astype(dq_ref.dtype)
            df_ref[pl.ds(r, HG_SUB), :] = dhf.astype(df_ref.dtype)
            di_ref[pl.ds(r, HG_SUB), :] = dhi.astype(di_ref.dtype)
            dlb_ref[...] += dlb
            return carry

        lax.fori_loop(0, spb, sub, 0, unroll=2)

    def colspec(off):
        return pl.BlockSpec((rows_blk, HEAD),
                            functools.partial(lambda h, b, j, off: (b * nb + nb - 1 - j, off + h), off=off))

    whole = lambda arr: pl.BlockSpec(arr.shape, lambda h, b, j: (0, 0))
    mats = lambda: pl.BlockSpec((1, 1, spb, HEAD, HEAD), lambda h, b, j: (b, h, nb - 1 - j, 0, 0))
    t_rows = bl * lp
    return pl.pallas_call(
        body, name="hgrn_bwd", grid=(nh, bl, nb),
        in_specs=[colspec(0), colspec(nh), colspec(2 * nh), pl.BlockSpec((1, HEAD), lambda h, b, j: (0, h)),
                  whole(cstack), whole(cstack_t), whole(pstack), whole(pstack_t), mats(), mats(), colspec(0)],
        out_specs=[colspec(0), colspec(0), colspec(0), pl.BlockSpec((1, HEAD), lambda h, b, j: (0, h))],
        out_shape=[jax.ShapeDtypeStruct((t_rows, d), BF16)] * 3 + [jax.ShapeDtypeStruct((1, d), F32)],
        scratch_shapes=[pltpu.VMEM((HEAD, HEAD), F32)],
        compiler_params=pltpu.CompilerParams(dimension_semantics=("arbitrary", "arbitrary", "arbitrary")),
    )(proj_main, proj_main, proj_main, lb, cstack, cstack_t, pstack, pstack_t, states, a_mats, do_scan)


def _allowed(row0, col0, nr, nc, transposed=False):
    if transposed:
        col = col0 + lax.broadcasted_iota(jnp.int32, (nc, 1), 0)
        row = row0 + lax.broadcasted_iota(jnp.int32, (1, nr), 1)
    else:
        row = row0 + lax.broadcasted_iota(jnp.int32, (nr, 1), 0)
        col = col0 + lax.broadcasted_iota(jnp.int32, (1, nc), 1)
    return (col <= row) & ((col >= PAD_FRONT) | (row < PAD_FRONT))


def attn_fwd(q_cat, k_cat, v, bl, lp, nm, scale):
    tq = tk = SEQ_BLOCK
    nq = lp // tq

    def body(q_ref, k_ref, v_ref, o_ref, lse_ref, m_ref, l_ref, acc_ref):
        i = pl.program_id(2)
        q = q_ref[...]
        m_ref[...] = jnp.full_like(m_ref, NEG)
        l_ref[...] = jnp.zeros_like(l_ref)
        acc_ref[...] = jnp.zeros_like(acc_ref)

        def kstep(c, carry):
            c0 = pl.multiple_of(c * tk, tk)
            s = lax.dot_general(q, k_ref[pl.ds(c0, tk), :], NT_DIMS, preferred_element_type=F32) * scale
            s = jnp.where(_allowed(i * tq, c * tk, tq, tk), s, NEG)
            m_old = m_ref[...]
            m_new = jnp.maximum(m_old, jnp.max(s, axis=1, keepdims=True))
            alpha = jnp.exp(m_old - m_new)
            p = jnp.exp(s - m_new)
            l_ref[...] = alpha * l_ref[...] + jnp.sum(p, axis=1, keepdims=True)
            acc_ref[...] = alpha * acc_ref[...] + jnp.dot(_bf(p), v_ref[pl.ds(c0, tk), :],
                                                          preferred_element_type=F32)
            m_ref[...] = m_new
            return carry

        lax.fori_loop(0, i + 1, kstep, 0)
        o_ref[...] = (acc_ref[...] / l_ref[...]).astype(o_ref.dtype)
        lse_ref[0, 0] = m_ref[...] + jnp.log(l_ref[...])

    return pl.pallas_call(
        body, name="attn_fwd", grid=(bl, nm, nq),
        in_specs=[pl.BlockSpec((tq, QK_PAD), lambda b, h, i: (b * nq + i, h)),
                  pl.BlockSpec((lp, QK_PAD), lambda b, h, i: (b, h)),
                  pl.BlockSpec((lp, HEAD), lambda b, h, i: (b, h))],
        out_specs=[pl.BlockSpec((tq, HEAD), lambda b, h, i: (b * nq + i, h)),
                   pl.BlockSpec((1, 1, tq, 1), lambda b, h, i: (b, h, i, 0))],
        out_shape=[jax.ShapeDtypeStruct((bl * lp, nm * HEAD), BF16),
                   jax.ShapeDtypeStruct((bl, nm, lp, 1), F32)],
        scratch_shapes=[pltpu.VMEM((tq, 1), F32), pltpu.VMEM((tq, 1), F32), pltpu.VMEM((tq, HEAD), F32)],
        compiler_params=pltpu.CompilerParams(dimension_semantics=("arbitrary", "arbitrary", "arbitrary")),
    )(q_cat, k_cat, v)


def attn_bwd_dq(q_cat, k_cat, v, o, do, lse, bl, lp, nm, scale):
    tq = tk = SEQ_BLOCK
    nq = lp // tq

    def body(q_ref, k_ref, v_ref, o_ref, do_ref, lse_ref, dq_ref, dl_ref, acc_ref):
        i = pl.program_id(2)
        q = q_ref[...]
        do_b = do_ref[...]
        delta = jnp.sum(o_ref[...].astype(F32) * do_b.astype(F32), axis=1, keepdims=True)
        lse_b = lse_ref[0, 0]
        acc_ref[...] = jnp.zeros_like(acc_ref)

        def kstep(c, carry):
            c0 = pl.multiple_of(c * tk, tk)
            ks = k_ref[pl.ds(c0, tk), :]
            s = lax.dot_general(q, ks, NT_DIMS, preferred_element_type=F32) * scale
            p = jnp.where(_allowed(i * tq, c * tk, tq, tk), jnp.exp(s - lse_b), 0.0)
            dp = lax.dot_general(do_b, v_ref[pl.ds(c0, tk), :], NT_DIMS, preferred_element_type=F32)
            ds = p * (dp - delta)
            acc_ref[...] += jnp.dot(_bf(ds), ks, preferred_element_type=F32)
            return carry

        lax.fori_loop(0, i + 1, kstep, 0)
        dq_ref[...] = acc_ref[...] * scale
        dl_ref[0, 0] = delta

    return pl.pallas_call(
        body, name="attn_bwd_dq", grid=(bl, nm, nq),
        in_specs=[pl.BlockSpec((tq, QK_PAD), lambda b, h, i: (b * nq + i, h)),
                  pl.BlockSpec((lp, QK_PAD), lambda b, h, i: (b, h)),
                  pl.BlockSpec((lp, HEAD), lambda b, h, i: (b, h)),
                  pl.BlockSpec((tq, HEAD), lambda b, h, i: (b * nq + i, h)),
                  pl.BlockSpec((tq, HEAD), lambda b, h, i: (b * nq + i, h)),
                  pl.BlockSpec((1, 1, tq, 1), lambda b, h, i: (b, h, i, 0))],
        out_specs=[pl.BlockSpec((tq, QK_PAD), lambda b, h, i: (b * nq + i, h)),
                   pl.BlockSpec((1, 1, tq, 1), lambda b, h, i: (b, h, i, 0))],
        out_shape=[jax.ShapeDtypeStruct((bl * lp, nm * QK_PAD), F32),
                   jax.ShapeDtypeStruct((bl, nm, lp, 1), F32)],
        scratch_shapes=[pltpu.VMEM((tq, QK_PAD), F32)],
        compiler_params=pltpu.CompilerParams(dimension_semantics=("arbitrary", "arbitrary", "arbitrary")),
    )(q_cat, k_cat, v, o, do, lse)


def attn_bwd_dkv(q_cat, k_cat, v, do, lse_row, delta_row, bl, lp, nm, scale):
    tq = tk = SEQ_BLOCK
    nq = lp // tq

    def body(q_ref, k_ref, v_ref, do_ref, lse_ref, dl_ref, dk_ref, dv_ref):
        i = pl.program_id(2)
        kt = k_ref[...]
        vt = v_ref[...]
        dk_ref[...] = jnp.zeros_like(dk_ref)
        dv_ref[...] = jnp.zeros_like(dv_ref)

        def qstep(c, carry):
            c0 = pl.multiple_of(c * tq, tq)
            qs = q_ref[pl.ds(c0, tq), :]
            dos = do_ref[pl.ds(c0, tq), :]
            st = lax.dot_general(kt, qs, NT_DIMS, preferred_element_type=F32) * scale
            pt = jnp.where(_allowed(c * tq, i * tk, tq, tk, transposed=True),
                           jnp.exp(st - lse_ref[0, 0, pl.ds(c, 1)][0]), 0.0)
            dv_ref[...] += jnp.dot(_bf(pt), dos, preferred_element_type=F32)
            dpt = lax.dot_general(vt, dos, NT_DIMS, preferred_element_type=F32)
            dst = pt * (dpt - dl_ref[0, 0, pl.ds(c, 1)][0])
            dk_ref[...] += jnp.dot(_bf(dst), qs, preferred_element_type=F32)
            return carry

        lax.fori_loop(i, nq, qstep, 0)
        dk_ref[...] = dk_ref[...] * scale

    return pl.pallas_call(
        body, name="attn_bwd_dkv", grid=(bl, nm, nq),
        in_specs=[pl.BlockSpec((lp, QK_PAD), lambda b, h, i: (b, h)),
                  pl.BlockSpec((tk, QK_PAD), lambda b, h, i: (b * nq + i, h)),
                  pl.BlockSpec((tk, HEAD), lambda b, h, i: (b * nq + i, h)),
                  pl.BlockSpec((lp, HEAD), lambda b, h, i: (b, h)),
                  pl.BlockSpec((1, 1, nq, 1, tq), lambda b, h, i: (b, h, 0, 0, 0)),
                  pl.BlockSpec((1, 1, nq, 1, tq), lambda b, h, i: (b, h, 0, 0, 0))],
        out_specs=[pl.BlockSpec((tk, QK_PAD), lambda b, h, i: (b * nq + i, h)),
                   pl.BlockSpec((tk, HEAD), lambda b, h, i: (b * nq + i, h))],
        out_shape=[jax.ShapeDtypeStruct((bl * lp, nm * QK_PAD), F32),
                   jax.ShapeDtypeStruct((bl * lp, nm * HEAD), F32)],
        compiler_params=pltpu.CompilerParams(dimension_semantics=("arbitrary", "arbitrary", "arbitrary")),
    )(q_cat, k_cat, v, do, lse_row, delta_row)


def _key_query_mask(key0, qry0, nk, nq_, causal):
    key = key0 + lax.broadcasted_iota(jnp.int32, (nk, 1), 0)
    if not causal:
        return key >= PAD_FRONT
    qry = qry0 + lax.broadcasted_iota(jnp.int32, (1, nq_), 1)
    return (key <= qry) & (key >= PAD_FRONT)


def _attn_tile(lp):
    return _tile(lp, ATTN_TILE_MAX, SEQ_BLOCK)


def attn_fwd_t(q_cat, k_cat, v_t, bl, lp, nm, scale):
    tq = tk = _attn_tile(lp)
    nq = lp // tq
    hp = ATTN_HEADS_PER_STEP
    assert nm % hp == 0

    def body(q_ref, k_ref, vt_ref, o_ref, lse_ref, m_ref, l_ref, acc_ref):
        i = pl.program_id(2)
        m_ref[...] = jnp.full_like(m_ref, NEG)
        l_ref[...] = jnp.zeros_like(l_ref)
        acc_ref[...] = jnp.zeros_like(acc_ref)

        def step(c, mask):
            c0 = pl.multiple_of(c * tk, tk)
            for hh in range(hp):
                cols = pl.ds(hh * QK_PAD, QK_PAD)
                st = lax.dot_general(k_ref[pl.ds(c0, tk), cols], q_ref[:, cols], NT_DIMS,
                                     preferred_element_type=F32) * scale
                if mask is not None:
                    st = jnp.where(_key_query_mask(c * tk, i * tq, tk, tq, mask == "causal"), st, NEG)
                m_old = m_ref[hh]
                m_new = jnp.maximum(m_old, jnp.max(st, axis=0, keepdims=True))
                alpha = jnp.exp(m_old - m_new)
                pt = jnp.exp(st - m_new)
                l_ref[hh] = alpha * l_ref[hh] + jnp.sum(pt, axis=0, keepdims=True)
                acc_ref[hh] = alpha * acc_ref[hh] + jnp.dot(vt_ref[0, hh, pl.ds(c, 1)][0], _bf(pt),
                                                            preferred_element_type=F32)
                m_ref[hh] = m_new

        def mid(c, carry):
            step(c, None)
            return carry

        @pl.when(i == 0)
        def _():
            step(0, "causal")

        @pl.when(i > 0)
        def _():
            step(0, "pad")
            lax.fori_loop(1, i, mid, 0)
            step(i, "causal")

        for hh in range(hp):
            o_ref[:, pl.ds(hh * HEAD, HEAD)] = jnp.transpose(acc_ref[hh] / l_ref[hh]).astype(o_ref.dtype)
            lse_ref[0, hh, 0] = m_ref[hh] + jnp.log(l_ref[hh])

    return pl.pallas_call(
        body, name="attn_fwd", grid=(bl, nm // hp, nq),
        in_specs=[pl.BlockSpec((tq, hp * QK_PAD), lambda b, h, i: (b * nq + i, h)),
                  pl.BlockSpec((lp, hp * QK_PAD), lambda b, h, i: (b, h)),
                  pl.BlockSpec((1, hp, nq, HEAD, tk), lambda b, h, i: (b, h, 0, 0, 0))],
        out_specs=[pl.BlockSpec((tq, hp * HEAD), lambda b, h, i: (b * nq + i, h)),
                   pl.BlockSpec((1, hp, 1, 1, tq), lambda b, h, i: (b, h, i, 0, 0))],
        out_shape=[jax.ShapeDtypeStruct((bl * lp, nm * HEAD), BF16),
                   jax.ShapeDtypeStruct((bl, nm, nq, 1, tq), F32)],
        scratch_shapes=[pltpu.VMEM((hp, 1, tq), F32), pltpu.VMEM((hp, 1, tq), F32), pltpu.VMEM((hp, HEAD, tq), F32)],
        compiler_params=pltpu.CompilerParams(dimension_semantics=("arbitrary", "arbitrary", "arbitrary")),
    )(q_cat, k_cat, v_t)


def attn_bwd_t(q_cat, k_cat, k_t, v, o, do, lse, bl, lp, nm, scale):
    tq = tk = _attn_tile(lp)
    nq = lp // tq
    hp = ATTN_HEADS_PER_STEP
    assert nm % hp == 0

    def body(q_ref, k_ref, kt_ref, v_ref, o_ref, do_ref, lse_ref, dq_ref, dk_ref, dv_ref, dqt_ref, dka_ref, dva_ref):
        i = pl.program_id(2)

        @pl.when(i == 0)
        def _():
            dqt_ref[...] = jnp.zeros_like(dqt_ref)

        dka_ref[...] = jnp.zeros_like(dka_ref)
        dva_ref[...] = jnp.zeros_like(dva_ref)
        ones8 = jnp.ones((8, HEAD), BF16)

        def step(c, mask):
            c0 = pl.multiple_of(c * tq, tq)
            for hh in range(hp):
                qcols, vcols = pl.ds(hh * QK_PAD, QK_PAD), pl.ds(hh * HEAD, HEAD)
                qs = q_ref[pl.ds(c0, tq), qcols]
                dos = do_ref[pl.ds(c0, tq), vcols]
                prod = dos.astype(F32) * o_ref[pl.ds(c0, tq), vcols].astype(F32)
                hi = _bf(prod)
                lo = _bf(prod - hi.astype(F32))
                delta8 = (lax.dot_general(ones8, hi, NT_DIMS, preferred_element_type=F32)
                          + lax.dot_general(ones8, lo, NT_DIMS, preferred_element_type=F32))
                st = lax.dot_general(k_ref[:, qcols], qs, NT_DIMS, preferred_element_type=F32) * scale
                pt = jnp.exp(st - lse_ref[0, hh, pl.ds(c, 1)][0])
                if mask is not None:
                    pt = jnp.where(_key_query_mask(i * tk, c * tq, tk, tq, mask == "causal"), pt, 0.0)
                dva_ref[hh] += jnp.dot(_bf(pt), dos, preferred_element_type=F32)
                dpt = lax.dot_general(v_ref[:, vcols], dos, NT_DIMS, preferred_element_type=F32)
                dst = _bf(pt * (dpt - jnp.tile(delta8, (tk // 8, 1))))
                dka_ref[hh] += jnp.dot(dst, qs, preferred_element_type=F32)
                dqt_ref[hh, pl.ds(c, 1)] += jnp.dot(kt_ref[0, hh, 0], dst, preferred_element_type=F32)[None]

        step(i, "causal")

        def rest_masked(c, carry):
            step(c, "pad")
            return carry

        def rest(c, carry):
            step(c, None)
            return carry

        @pl.when(i == 0)
        def _():
            lax.fori_loop(1, nq, rest_masked, 0)

        @pl.when(i > 0)
        def _():
            lax.fori_loop(i + 1, nq, rest, 0)

        for hh in range(hp):
            dk_ref[:, pl.ds(hh * QK_PAD, QK_PAD)] = (dka_ref[hh] * scale).astype(dk_ref.dtype)
            dv_ref[:, pl.ds(hh * HEAD, HEAD)] = dva_ref[hh].astype(dv_ref.dtype)

        @pl.when(i == nq - 1)
        def _():
            for hh in range(hp):
                for c in range(nq):
                    dq_ref[pl.ds(c * tq, tq), pl.ds(hh * QK_PAD, QK_PAD)] = (
                        jnp.transpose(dqt_ref[hh, c]) * scale).astype(dq_ref.dtype)

    return pl.pallas_call(
        body, name="attn_bwd", grid=(bl, nm // hp, nq),
        in_specs=[pl.BlockSpec((lp, hp * QK_PAD), lambda b, h, i: (b, h)),
                  pl.BlockSpec((tk, hp * QK_PAD), lambda b, h, i: (b * nq + i, h)),
                  pl.BlockSpec((1, hp, 1, QK_PAD, tk), lambda b, h, i: (b, h, i, 0, 0)),
                  pl.BlockSpec((tk, hp * HEAD), lambda b, h, i: (b * nq + i, h)),
                  pl.BlockSpec((lp, hp * HEAD), lambda b, h, i: (b, h)),
                  pl.BlockSpec((lp, hp * HEAD), lambda b, h, i: (b, h)),
                  pl.BlockSpec((1, hp, nq, 1, tq), lambda b, h, i: (b, h, 0, 0, 0))],
        out_specs=[pl.BlockSpec((lp, hp * QK_PAD), lambda b, h, i: (b, h)),
                   pl.BlockSpec((tk, hp * QK_PAD), lambda b, h, i: (b * nq + i, h)),
                   pl.BlockSpec((tk, hp * HEAD), lambda b, h, i: (b * nq + i, h))],
        out_shape=[jax.ShapeDtypeStruct((bl * lp, nm * QK_PAD), BF16),
                   jax.ShapeDtypeStruct((bl * lp, nm * QK_PAD), BF16),
                   jax.ShapeDtypeStruct((bl * lp, nm * HEAD), BF16)],
        scratch_shapes=[pltpu.VMEM((hp, nq, QK_PAD, tq), F32), pltpu.VMEM((hp, tk, QK_PAD), F32),
                        pltpu.VMEM((hp, tk, HEAD), F32)],
        compiler_params=pltpu.CompilerParams(dimension_semantics=("arbitrary", "arbitrary", "arbitrary")),
    )(q_cat, k_cat, k_t, v, o, do, lse)


def _place():
    return lax.axis_index("x"), lax.axis_index("y"), lax.axis_index("c")


def gather_shards(packed):
    hbm = pl.BlockSpec(memory_space=pl.ANY)

    def body(src_ref, out_ref, send_sems, recv_sems, local_sem):
        x, y, c = _place()
        me = 2 * x + y
        chips = [(1 - x, y), (x, 1 - y), (1 - x, 1 - y)]
        local = pltpu.make_async_copy(src_ref, out_ref.at[me], local_sem)
        local.start()
        sends = []
        for k, (px, py) in enumerate(chips):
            cp = pltpu.make_async_remote_copy(src_ref=src_ref, dst_ref=out_ref.at[me], send_sem=send_sems.at[k],
                                              recv_sem=recv_sems.at[k], device_id=(px, py, c), device_id_type=MESH)
            cp.start()
            sends.append(cp)
        for k, (px, py) in enumerate(chips):
            pltpu.make_async_remote_copy(src_ref=src_ref, dst_ref=out_ref.at[2 * px + py], send_sem=send_sems.at[k],
                                         recv_sem=recv_sems.at[k], device_id=(px, py, c),
                                         device_id_type=MESH).wait_recv()
        for cp in sends:
            cp.wait_send()
        local.wait()

    return pl.pallas_call(
        body, name="gather_shards", in_specs=[hbm], out_specs=hbm,
        out_shape=jax.ShapeDtypeStruct((4,) + packed.shape, packed.dtype),
        scratch_shapes=[pltpu.SemaphoreType.DMA((3,)), pltpu.SemaphoreType.DMA((3,)), pltpu.SemaphoreType.DMA],
    )(packed)


def gather_small(small):
    hbm = pl.BlockSpec(memory_space=pl.ANY)

    def body(small_ref, all_ref, send_sems, recv_sems, local_sem):
        x, y, c = _place()
        me = 4 * x + 2 * y + c
        local = pltpu.make_async_copy(small_ref, all_ref.at[me], local_sem)
        local.start()
        others = [(x ^ ((r >> 2) & 1), y ^ ((r >> 1) & 1), c ^ (r & 1)) for r in range(1, 8)]
        sends = []
        for r, peer in enumerate(others):
            cp = pltpu.make_async_remote_copy(src_ref=small_ref, dst_ref=all_ref.at[me], send_sem=send_sems.at[r],
                                              recv_sem=recv_sems.at[r], device_id=peer, device_id_type=MESH)
            cp.start()
            sends.append(cp)
        for r, (px, py, pc) in enumerate(others):
            pltpu.make_async_remote_copy(src_ref=small_ref, dst_ref=all_ref.at[4 * px + 2 * py + pc],
                                         send_sem=send_sems.at[r], recv_sem=recv_sems.at[r],
                                         device_id=(px, py, pc), device_id_type=MESH).wait_recv()
        for cp in sends:
            cp.wait_send()
        local.wait()

    return pl.pallas_call(
        body, name="gather_small", in_specs=[hbm], out_specs=hbm,
        out_shape=jax.ShapeDtypeStruct((8,) + small.shape, small.dtype),
        scratch_shapes=[pltpu.SemaphoreType.DMA((7,)), pltpu.SemaphoreType.DMA((7,)), pltpu.SemaphoreType.DMA],
    )(small)


def swap_with_sibling(name, parts):
    n = len(parts)
    hbm = pl.BlockSpec(memory_space=pl.ANY)

    def body(*refs):
        x, y, c = _place()
        cps = [pltpu.make_async_remote_copy(src_ref=refs[j], dst_ref=refs[n + j], send_sem=refs[2 * n].at[j],
                                            recv_sem=refs[2 * n + 1].at[j], device_id=(x, y, 1 - c),
                                            device_id_type=MESH) for j in range(n)]
        for cp in cps:
            cp.start()
        for cp in cps:
            cp.wait()

    return pl.pallas_call(
        body, name=name, in_specs=[hbm] * n, out_specs=[hbm] * n,
        out_shape=[jax.ShapeDtypeStruct(p.shape, p.dtype) for p in parts],
        scratch_shapes=[pltpu.SemaphoreType.DMA((n,)), pltpu.SemaphoreType.DMA((n,))],
    )(*parts)


def _chips3():
    x, y, c = _place()
    return [(1 - x, y, c), (x, 1 - y, c), (1 - x, 1 - y, c)]


def _push_copies(src_refs, land_refs, send_sems, recv_sems, per_chip):
    cps = []
    for j, (src_ref, land_ref) in enumerate(zip(src_refs, land_refs)):
        for k, (px, py, pc) in enumerate(_chips3()):
            part = src_ref.at[2 * px + py] if per_chip else src_ref
            cps.append(pltpu.make_async_remote_copy(
                src_ref=part, dst_ref=land_ref.at[k], send_sem=send_sems.at[3 * j + k],
                recv_sem=recv_sems.at[3 * j + k], device_id=(px, py, pc), device_id_type=MESH))
    return cps


def push_start(name, srcs, per_chip):
    n = len(srcs)
    hbm = pl.BlockSpec(memory_space=pltpu.HBM)
    sem = pl.BlockSpec(memory_space=pltpu.SEMAPHORE)
    lands = [lax.empty((3,) + s.shape[-2:], s.dtype) for s in srcs]

    def body(*refs):
        src_refs, land_refs = refs[:n], refs[n:2 * n]
        send_sems, recv_sems = refs[2 * n], refs[2 * n + 1]
        for cp in _push_copies(src_refs, land_refs, send_sems, recv_sems, per_chip):
            cp.start()
        refs[-1][...] = jnp.zeros_like(refs[-1])

    outs = pl.pallas_call(
        body, name=name,
        out_shape=(pltpu.SemaphoreType.DMA((3 * n,)), pltpu.SemaphoreType.DMA((3 * n,)),
                   *[pltpu.HBM(a.shape, a.dtype) for a in list(srcs) + lands], jax.ShapeDtypeStruct((8, HEAD), F32)),
        in_specs=(hbm,) * (2 * n),
        out_specs=(sem, sem) + (hbm,) * (2 * n) + (pl.BlockSpec(memory_space=pltpu.VMEM),),
        input_output_aliases={j: 2 + j for j in range(2 * n)},
        compiler_params=pltpu.CompilerParams(has_side_effects=pltpu.SideEffectType.DATAFLOW_SIDE_EFFECTING),
    )(*[pltpu.with_memory_space_constraint(a, pltpu.HBM) for a in list(srcs) + lands])
    return tuple(outs[:-1]), outs[-1]


def push_wait(name, handle, after, per_chip):
    send_sems, recv_sems = handle[0], handle[1]
    thru = handle[2:]
    n = len(thru) // 2
    hbm = pl.BlockSpec(memory_space=pltpu.HBM)
    sem = pl.BlockSpec(memory_space=pltpu.SEMAPHORE)

    def body(*refs):
        src_refs, land_refs = refs[:n], refs[n:2 * n]
        for cp in _push_copies(src_refs, land_refs, refs[2 * n], refs[2 * n + 1], per_chip):
            cp.wait_send()
            cp.wait_recv()

    outs = pl.pallas_call(
        body, name=name,
        out_shape=tuple(pltpu.HBM(a.shape, a.dtype) for a in thru),
        in_specs=(hbm,) * (2 * n) + (sem, sem, pl.BlockSpec(memory_space=pl.ANY)), out_specs=(hbm,) * (2 * n),
        input_output_aliases={j: j for j in range(2 * n)},
        compiler_params=pltpu.CompilerParams(has_side_effects=pltpu.SideEffectType.DATAFLOW_SIDE_EFFECTING),
    )(*thru, send_sems, recv_sems, after)
    return outs[:n], outs[n:]


def by_chip(own, landed, my_chip):
    by_rel = jnp.stack([own, landed[1], landed[0], landed[2]])
    return [lax.dynamic_index_in_dim(by_rel, jnp.bitwise_xor(s, my_chip), axis=0, keepdims=False) for s in range(4)]


def adamw(name, w, g_parts, m, v):
    r, c = w.shape
    tr = r if r * c <= 65536 else _tile(r, 128, 8)
    ng = len(g_parts)

    def body(*refs):
        w_ref, m_ref, v_ref = refs[0], refs[1 + ng], refs[2 + ng]
        g_ref, d_ref, nm_ref, nv_ref = refs[3 + ng:]
        gv = refs[1][...]
        for k in range(1, ng):
            gv = gv + refs[1 + k][...]
        m_new = ADAM_B1 * m_ref[...] + (1.0 - ADAM_B1) * gv
        v_new = ADAM_B2 * v_ref[...] + (1.0 - ADAM_B2) * (gv * gv)
        m_hat = m_new / (1.0 - ADAM_B1 ** ADAM_STEP)
        v_hat = v_new / (1.0 - ADAM_B2 ** ADAM_STEP)
        g_ref[...] = gv
        d_ref[...] = -ADAM_LR * (m_hat / (jnp.sqrt(v_hat) + ADAM_EPS) + ADAM_WD * w_ref[...])
        nm_ref[...] = m_new
        nv_ref[...] = v_new

    spec = pl.BlockSpec((tr, c), lambda i: (i, 0))
    return pl.pallas_call(
        body, name=name, grid=(r // tr,), in_specs=[spec] * (3 + ng), out_specs=[spec] * 4,
        out_shape=[jax.ShapeDtypeStruct((r, c), F32)] * 4,
        compiler_params=pltpu.CompilerParams(dimension_semantics=("arbitrary",)),
    )(w, *g_parts, m, v)


def split_full(name, full, s):
    if name in COL_SHARDED:
        c = full.shape[1] // 4
        return full[:, s * c:(s + 1) * c]
    r = full.shape[0] // 4
    return full[s * r:(s + 1) * r]


def join_shards(name, shards):
    return jnp.concatenate(shards, axis=1 if name in COL_SHARDED else 0)


def kernel(x, meta_tokens, w_in, b_gate, lb_logits, hg_norm_g, w_hg_o, q_a_norm_g, w_q_b, kv_a_norm_g, w_kv_b, w_mla_o, w_out, mix_pre_g, mix_post_g, ffn_pre_g, ffn_post_g, w_ffn_in, w_ffn_out, loss_target, m_meta_tokens, m_w_in, m_b_gate, m_lb_logits, m_hg_norm_g, m_w_hg_o, m_q_a_norm_g, m_w_q_b, m_kv_a_norm_g, m_w_kv_b, m_w_mla_o, m_w_out, m_mix_pre_g, m_mix_post_g, m_ffn_pre_g, m_ffn_post_g, m_w_ffn_in, m_w_ffn_out, v_meta_tokens, v_w_in, v_b_gate, v_lb_logits, v_hg_norm_g, v_w_hg_o, v_q_a_norm_g, v_w_q_b, v_kv_a_norm_g, v_w_kv_b, v_w_mla_o, v_w_out, v_mix_pre_g, v_mix_post_g, v_ffn_pre_g, v_ffn_post_g, v_w_ffn_in, v_w_ffn_out):
    wts = dict(meta_tokens=meta_tokens, w_in=w_in[0], b_gate=b_gate, lb_logits=lb_logits, hg_norm_g=hg_norm_g,
               w_hg_o=w_hg_o[0], q_a_norm_g=q_a_norm_g, w_q_b=w_q_b[0], kv_a_norm_g=kv_a_norm_g, w_kv_b=w_kv_b[0],
               w_mla_o=w_mla_o[0], w_out=w_out[0], mix_pre_g=mix_pre_g, mix_post_g=mix_post_g, ffn_pre_g=ffn_pre_g,
               ffn_post_g=ffn_post_g, w_ffn_in=w_ffn_in[0], w_ffn_out=w_ffn_out[0])
    mom_m = dict(meta_tokens=m_meta_tokens, w_in=m_w_in[0], b_gate=m_b_gate, lb_logits=m_lb_logits,
                 hg_norm_g=m_hg_norm_g, w_hg_o=m_w_hg_o[0], q_a_norm_g=m_q_a_norm_g, w_q_b=m_w_q_b[0],
                 kv_a_norm_g=m_kv_a_norm_g, w_kv_b=m_w_kv_b[0], w_mla_o=m_w_mla_o[0], w_out=m_w_out[0],
                 mix_pre_g=m_mix_pre_g, mix_post_g=m_mix_post_g, ffn_pre_g=m_ffn_pre_g, ffn_post_g=m_ffn_post_g,
                 w_ffn_in=m_w_ffn_in[0], w_ffn_out=m_w_ffn_out[0])
    mom_v = dict(meta_tokens=v_meta_tokens, w_in=v_w_in[0], b_gate=v_b_gate, lb_logits=v_lb_logits,
                 hg_norm_g=v_hg_norm_g, w_hg_o=v_w_hg_o[0], q_a_norm_g=v_q_a_norm_g, w_q_b=v_w_q_b[0],
                 kv_a_norm_g=v_kv_a_norm_g, w_kv_b=v_w_kv_b[0], w_mla_o=v_w_mla_o[0], w_out=v_w_out[0],
                 mix_pre_g=v_mix_pre_g, mix_post_g=v_mix_post_g, ffn_pre_g=v_ffn_pre_g, ffn_post_g=v_ffn_post_g,
                 w_ffn_in=v_w_ffn_in[0], w_ffn_out=v_w_ffn_out[0])

    bl, seq, d = x.shape
    lp = PAD_FRONT + N_META + seq
    t_rows = bl * lp
    nh = d // HEAD
    ql, kvl = wts["w_q_b"].shape[0], wts["w_kv_b"].shape[0]
    nm = (4 * wts["w_mla_o"].shape[0]) // HEAD
    ffn = 4 * wts["w_ffn_out"].shape[0]
    mla_w = ql + kvl + HEAD
    assert ql == kvl and ql % HEAD == 0 and seq % SEQ_BLOCK == 0 and d % HEAD == 0
    scale = (HEAD + ROPE) ** -0.5
    my_chip = 2 * lax.axis_index("x") + lax.axis_index("y")

    mcols = meta_tokens.shape[1]
    meta_all = gather_shards(meta_tokens)
    meta_full = jnp.concatenate([meta_all[s] for s in range(4)], axis=1)

    def start_gather(name, names, order_after):
        srcs = [_bf(wts[n]) for n in names]
        if order_after is not None:
            srcs[0] = srcs[0] + order_after[0, 0].astype(BF16)
        return push_start(name, srcs, per_chip=False)

    def finish_gather(name, names, started, after):
        owns, landed = push_wait(name, started[0], after, per_chip=False)
        return {n: join_shards(n, by_chip(own, land, my_chip)) for n, own, land in zip(names, owns, landed)}

    rest_names = tuple(n for n in BIG if n != "w_in")
    gather_1 = start_gather("gather_w_in_start", ("w_in",), None)
    gather_2 = start_gather("gather_rest_start", rest_names, gather_1[1])

    h0 = jnp.concatenate([jnp.zeros((bl, PAD_FRONT, d), F32), jnp.broadcast_to(meta_full[None], (bl, N_META, d)), x],
                         axis=1).reshape(t_rows, d)
    tgt = jnp.concatenate([jnp.zeros((bl, PAD_FRONT + N_META, d), F32), loss_target], axis=1).reshape(t_rows, d)
    pos = (jnp.arange(lp, dtype=jnp.int32) - PAD_FRONT).astype(F32)
    inv_freq = 1.0 / (ROPE_THETA ** (jnp.arange(0, ROPE, 2, dtype=F32) / ROPE))
    ang = pos[:, None] * inv_freq[None, :]
    zeros32 = jnp.zeros((lp, ROPE_HALF), F32)
    zeros64 = jnp.zeros((lp, HEAD - ROPE), F32)
    t_cos = jnp.concatenate([jnp.cos(ang), jnp.cos(ang), zeros64], axis=1)
    t_up = jnp.concatenate([zeros32, jnp.sin(ang), zeros64], axis=1)
    t_dn = jnp.concatenate([-jnp.sin(ang), zeros32, zeros64], axis=1)
    real = jnp.broadcast_to((jnp.arange(lp) >= PAD_FRONT + N_META).astype(F32)[:, None], (lp, d))
    lb_soft = jax.nn.softmax(lb_logits.astype(F32), axis=0)
    lb = lb_soft[0:1]

    (u1,) = rowwise("norm_mix_pre", lambda h, g: _rms(h, g), [(h0, d, 0)], [], [mix_pre_g + gather_2[1][0, 0]],
                    [(d, BF16)])
    full = finish_gather("gather_w_in_wait", ("w_in",), gather_1, u1)
    w_main = jnp.concatenate([full["w_in"][:, :4 * d], full["w_in"][:, -2 * d:]], axis=1)
    w_mla = jnp.pad(full["w_in"][:, 4 * d:4 * d + ql + kvl + ROPE], ((0, 0), (0, HEAD - ROPE)))
    proj_main = matmul("proj_main", u1, w_main, "nn", out_dtype=BF16)
    proj_mla = matmul("proj_mla", u1, w_mla, "nn", out_dtype=BF16)
    hg_consts = _hg_constants()
    o_scan, states, a_mats = hgrn_fwd(proj_main, lb, hg_consts, bl, lp, d)

    def hg_out_fn(o, hg, g):
        return jnp.concatenate([_rms(o[:, h * HEAD:(h + 1) * HEAD], g) for h in range(nh)], axis=1) * _silu(hg)

    (o_hg,) = rowwise("hgrn_out", hg_out_fn, [(o_scan, d, 0), (proj_main, d, 3)], [], [hg_norm_g], [(d, BF16)])
    full.update(finish_gather("gather_rest_wait", rest_names, gather_2, o_hg))
    w_qb = jnp.pad(full["w_q_b"].reshape(ql, nm, HEAD + ROPE), ((0, 0), (0, 0), (0, QK_PAD - HEAD - ROPE))
                   ).reshape(ql, nm * QK_PAD)
    w_kvb = full["w_kv_b"]
    y_a = matmul("y_a", o_hg, _bf(full["w_hg_o"]), "nn", out_dtype=BF16)

    qn, kvn = rowwise("mla_norms", lambda cq, ckv, gq, gk: (_rms(cq, gq), _rms(ckv, gk)),
                      [(proj_mla, ql, 0), (proj_mla, kvl, 1)], [], [q_a_norm_g, kv_a_norm_g],
                      [(ql, BF16), (kvl, BF16)])
    q_full = matmul("q_up", qn, w_qb, "nn", out_dtype=BF16)
    kv_full = matmul("kv_up", kvn, w_kvb, "nn", out_dtype=BF16)

    def mla_prep_fn(qf, kvf, kpe, cos, s_up, s_dn):
        kpe_r = _rope(kpe, cos, s_up, s_dn)
        qs, ks, vs = [], [], []
        for h in range(nm):
            qs += [qf[:, h * QK_PAD:h * QK_PAD + HEAD], _rope(qf[:, h * QK_PAD + HEAD:(h + 1) * QK_PAD], cos, s_up, s_dn)]
            ks += [kvf[:, h * QK_PAD:h * QK_PAD + HEAD], kpe_r]
            vs += [kvf[:, h * QK_PAD + HEAD:(h + 1) * QK_PAD]]
        return jnp.concatenate(qs, axis=1), jnp.concatenate(ks, axis=1), jnp.concatenate(vs, axis=1)

    kpe_blk = (ql + kvl) // HEAD
    q_cat, k_cat, v_att = rowwise("mla_prep", mla_prep_fn,
                                  [(q_full, nm * QK_PAD, 0), (kv_full, nm * QK_PAD, 0), (proj_mla, HEAD, kpe_blk)],
                                  [t_cos, t_up, t_dn], [], [(nm * QK_PAD, BF16), (nm * QK_PAD, BF16), (nm * HEAD, BF16)])
    at = _attn_tile(lp)
    v_t = v_att.reshape(bl, lp // at, at, nm, HEAD).transpose(0, 3, 1, 4, 2)
    k_t = k_cat.reshape(bl, lp // at, at, nm, QK_PAD).transpose(0, 3, 1, 4, 2)
    o_mla, lse = attn_fwd_t(q_cat, k_cat, v_t, bl, lp, nm, scale)
    y_b = matmul("y_b", o_mla, _bf(full["w_mla_o"]), "nn", out_dtype=BF16)

    def gate_fn(ya, yb, ga, gb, bias):
        return _sigmoid(ga + bias[:, :d]) * ya + _sigmoid(gb + bias[:, d:]) * yb

    (z,) = rowwise("gate_mix", gate_fn, [(y_a, d, 0), (y_b, d, 0), (proj_main, d, 4), (proj_main, d, 5)], [],
                   [b_gate], [(d, BF16)])
    mixed = matmul("mixed", z, _bf(full["w_out"]), "nn")

    def mid_fn(h, mx, g_post, g_pre):
        h1 = h + _rms(mx, g_post)
        return h1, _rms(h1, g_pre)

    h1, u2 = rowwise("norm_mid", mid_fn, [(h0, d, 0), (mixed, d, 0)], [], [mix_post_g, ffn_pre_g],
                     [(d, F32), (d, BF16)])
    gu = matmul("ffn_in", u2, _bf(full["w_ffn_in"]), "nn", out_dtype=BF16)
    (act,) = rowwise("swiglu", lambda gt, up: _silu(gt) * up, [(gu, ffn, 0), (gu, ffn, 1)], [], [], [(ffn, BF16)])
    f_out = matmul("ffn_out", act, _bf(full["w_ffn_out"]), "nn")

    def loss_fn(h1v, fv, tg, realv, g_post):
        h2 = h1v + _rms(fv, g_post)
        diff = (h2 - tg) * realv
        part = jnp.broadcast_to(0.5 * jnp.sum(diff * diff, keepdims=True) / d, (1, HEAD))
        dy = diff / d
        df, dg = _rms_bwd(fv, g_post, dy)
        return dy, df, part, dg

    dy, df, loss_part, g_ffn_post = rowwise("loss_head", loss_fn, [(h1, d, 0), (f_out, d, 0), (tgt, d, 0)], [real],
                                            [ffn_post_g], [(d, F32), (d, BF16)], [(1, HEAD), (1, d)])
    grads = {}
    d_act = matmul("d_act", df, _bf(full["w_ffn_out"]), "nt", out_dtype=BF16)
    grads["w_ffn_out"] = matmul("gw_ffn_out", act, df, "tn")

    def swiglu_bwd_fn(gt, up, da):
        return jnp.concatenate([da * up * _silu_grad(gt), da * _silu(gt)], axis=1)

    (dgu,) = rowwise("swiglu_bwd", swiglu_bwd_fn, [(gu, ffn, 0), (gu, ffn, 1), (d_act, ffn, 0)], [], [],
                     [(2 * ffn, BF16)])
    du2 = matmul("d_u2", dgu, _bf(full["w_ffn_in"]), "nt", out_dtype=BF16)
    grads["w_ffn_in"] = matmul("gw_ffn_in", u2, dgu, "tn")

    def mid_bwd_fn(dyv, h1v, du2v, mx, g_pre, g_post):
        dx, dg_pre = _rms_bwd(h1v, g_pre, du2v)
        dh1 = dyv + dx
        dmx, dg_post = _rms_bwd(mx, g_post, dh1)
        return dh1, dmx, dg_pre, dg_post

    dh1, dmixed, g_ffn_pre, g_mix_post = rowwise("norm_mid_bwd", mid_bwd_fn,
                                                 [(dy, d, 0), (h1, d, 0), (du2, d, 0), (mixed, d, 0)], [],
                                                 [ffn_pre_g, mix_post_g], [(d, F32), (d, BF16)], [(1, d), (1, d)])
    dz = matmul("d_z", dmixed, _bf(full["w_out"]), "nt", out_dtype=BF16)
    grads["w_out"] = matmul("gw_out", z, dmixed, "tn")

    def gate_bwd_fn(dzv, ya, yb, ga, gb, bias):
        sa, sb = _sigmoid(ga + bias[:, :d]), _sigmoid(gb + bias[:, d:])
        dga = dzv * ya * sa * (1.0 - sa)
        dgb = dzv * yb * sb * (1.0 - sb)
        dgates = jnp.concatenate([dga, dgb], axis=1)
        return dzv * sa, dzv * sb, dgates, jnp.sum(dgates, axis=0, keepdims=True)

    dy_a, dy_b, dgates, g_b_gate = rowwise("gate_mix_bwd", gate_bwd_fn,
                                           [(dz, d, 0), (y_a, d, 0), (y_b, d, 0), (proj_main, d, 4), (proj_main, d, 5)],
                                           [], [b_gate], [(d, BF16), (d, BF16), (2 * d, BF16)], [(1, 2 * d)])
    do_hg = matmul("d_o_hg", dy_a, _bf(full["w_hg_o"]), "nt", out_dtype=BF16)
    grads["w_hg_o"] = matmul("gw_hg_o", o_hg, dy_a, "tn")
    do_mla = matmul("d_o_mla", dy_b, _bf(full["w_mla_o"]), "nt", out_dtype=BF16)
    grads["w_mla_o"] = matmul("gw_mla_o", o_mla, dy_b, "tn")

    early = ("w_hg_o", "w_mla_o", "w_out", "w_ffn_in", "w_ffn_out")
    late = ("w_in", "w_q_b", "w_kv_b")

    def start_grads(name, names):
        sends = [_bf(jnp.stack([split_full(n, grads[n], s) for s in range(4)])) for n in names]
        mines = []
        for n in names:
            r, c = wts[n].shape
            axis, size = (1, c) if n in COL_SHARDED else (0, r)
            mines.append(lax.dynamic_slice_in_dim(grads[n], my_chip * size, size, axis=axis))
        handle, token = push_start(name, sends, per_chip=True)
        return handle, token, mines

    def finish_grads(tag, names, started, after):
        handle, _, mines = started
        _, landed = push_wait(f"grads_{tag}_wait", handle, after, per_chip=True)
        parts = []
        for n, mine, land in zip(names, mines, landed):
            r, c = mine.shape
            tr = _tile(r, 256, 16)
            land2 = land.reshape(3 * r, c)
            parts.append(rowwise(f"sum_chips_{n}", lambda a, r0, r1, r2: a + r0 + r1 + r2,
                                 [(mine, c, 0)] + [(land2, c, 0, k * (r // tr)) for k in range(3)],
                                 [], [], [(c, F32)], tm=tr)[0])
        sibs = swap_with_sibling(f"swap_{tag}", parts)
        return {n: [p, s] for n, p, s in zip(names, parts, sibs)}

    grads_early = start_grads("grads_early_start", early)
    token_a = grads_early[1]

    def hg_out_bwd_fn(do, o, hg, g):
        sg = _silu(hg)
        dn = do * sg
        dos, dgs, ons = [], 0.0, []
        for h in range(nh):
            sl = slice(h * HEAD, (h + 1) * HEAD)
            dx, dg = _rms_bwd(o[:, sl], g, dn[:, sl])
            dos.append(dx)
            dgs = dgs + dg
            ons.append(_rms(o[:, sl], g))
        dhg = do * jnp.concatenate(ons, axis=1) * _silu_grad(hg)
        return jnp.concatenate(dos, axis=1), dhg, dgs

    do_scan, dhg, g_hg_norm = rowwise("hgrn_out_bwd", hg_out_bwd_fn, [(do_hg, d, 0), (o_scan, d, 0), (proj_main, d, 3)],
                                      [], [hg_norm_g], [(d, F32), (d, BF16)], [(1, HEAD)])
    dhq, dhf, dhi, g_lb = hgrn_bwd(proj_main, lb + token_a[0, 0], hg_consts, states, a_mats, do_scan, bl, lp, d)

    dq_cat, dk_cat, dv_att = attn_bwd_t(q_cat, k_cat, k_t, v_att, o_mla, do_mla, lse, bl, lp, nm, scale)

    def mla_prep_bwd_fn(dqc, dkc, dvv, cos, s_up, s_dn):
        dqs, dkvs, dkpe = [], [], 0.0
        for h in range(nm):
            dqs += [dqc[:, h * QK_PAD:h * QK_PAD + HEAD],
                    _rope_bwd(dqc[:, h * QK_PAD + HEAD:(h + 1) * QK_PAD], cos, s_up, s_dn)]
            dkvs += [dkc[:, h * QK_PAD:h * QK_PAD + HEAD], dvv[:, h * HEAD:(h + 1) * HEAD]]
            dkpe = dkpe + dkc[:, h * QK_PAD + HEAD:(h + 1) * QK_PAD]
        return jnp.concatenate(dqs, axis=1), jnp.concatenate(dkvs, axis=1), _rope_bwd(dkpe, cos, s_up, s_dn)

    dq_full, dkv_full, dkpe = rowwise("mla_prep_bwd", mla_prep_bwd_fn,
                                      [(dq_cat, nm * QK_PAD, 0), (dk_cat, nm * QK_PAD, 0), (dv_att, nm * HEAD, 0)],
                                      [t_cos, t_up, t_dn], [],
                                      [(nm * QK_PAD, BF16), (nm * QK_PAD, BF16), (HEAD, F32)])
    dqn = matmul("d_qn", dq_full, w_qb, "nt", out_dtype=BF16)
    g_wqb = matmul("gw_q_b", qn, dq_full, "tn")
    grads["w_q_b"] = g_wqb.reshape(ql, nm, QK_PAD)[:, :, :HEAD + ROPE].reshape(ql, nm * (HEAD + ROPE))
    dkvn = matmul("d_kvn", dkv_full, w_kvb, "nt", out_dtype=BF16)
    grads["w_kv_b"] = matmul("gw_kv_b", kvn, dkv_full, "tn")

    def mla_norms_bwd_fn(dqnv, dkvnv, cq, ckv, dkpev, gq, gk):
        dcq, dgq = _rms_bwd(cq, gq, dqnv)
        dckv, dgk = _rms_bwd(ckv, gk, dkvnv)
        return jnp.concatenate([dcq, dckv, dkpev], axis=1), dgq, dgk

    dmla, g_q_norm, g_kv_norm = rowwise("mla_norms_bwd", mla_norms_bwd_fn,
                                        [(dqn, ql, 0), (dkvn, kvl, 0), (proj_mla, ql, 0), (proj_mla, kvl, 1),
                                         (dkpe, HEAD, 0)], [], [q_a_norm_g, kv_a_norm_g],
                                        [(mla_w, BF16)], [(1, ql), (1, kvl)])

    d_pieces = [dhq, dhf, dhi, dhg, dgates, dmla]
    gw_parts = [matmul(f"gw_in_{k}", u1, dp, "tn") for k, dp in enumerate(d_pieces)]
    grads["w_in"] = jnp.concatenate(gw_parts[:4] + [gw_parts[5][:, :ql + kvl + ROPE], gw_parts[4]], axis=1)
    grads_late = start_grads("grads_late_start", late)
    w_mla_after = w_mla + grads_late[1][0, 0].astype(BF16)
    w_pieces = [w_main[:, 0:d], w_main[:, d:2 * d], w_main[:, 2 * d:3 * d], w_main[:, 3 * d:4 * d],
                w_main[:, 4 * d:6 * d], w_mla_after]
    du1 = matmul("d_u1", d_pieces, w_pieces, "nt")

    def first_bwd_fn(dh1v, h, du1v, g):
        dx, dg = _rms_bwd(h, g, du1v)
        return dh1v + dx, dg

    dh0, g_mix_pre = rowwise("norm_mix_pre_bwd", first_bwd_fn, [(dh1, d, 0), (h0, d, 0), (du1, d, 0)], [],
                             [mix_pre_g], [(d, F32)], [(1, d)])
    dh0 = dh0.reshape(bl, lp, d)
    grad_x = dh0[:, PAD_FRONT + N_META:]

    g_parts = finish_grads("early", early, grads_early, g_mix_pre)
    updates = {}

    def update(n, parts):
        w2 = wts[n].reshape(-1, wts[n].shape[-1])
        updates[n] = adamw("adamw_" + n, w2, [p.reshape(w2.shape) for p in parts], mom_m[n].reshape(w2.shape),
                           mom_v[n].reshape(w2.shape))

    for n in early:
        update(n, g_parts[n])
    g_parts = finish_grads("late", late, grads_late, updates[early[-1]][0])
    for n in late:
        update(n, g_parts[n])
    p0 = lb_soft[0:1]
    g_lb_logits = jnp.concatenate([g_lb * p0 * (1.0 - p0), -g_lb * p0 * (1.0 - p0)], axis=0)

    def row_of(vec):
        return vec.reshape(-1, d) if vec.size >= d else jnp.pad(vec.reshape(1, -1), ((0, 0), (0, d - vec.size)))

    small_parts = dict(b_gate=g_b_gate, lb_logits=g_lb_logits, hg_norm_g=g_hg_norm, q_a_norm_g=g_q_norm,
                       kv_a_norm_g=g_kv_norm, mix_pre_g=g_mix_pre, mix_post_g=g_mix_post, ffn_pre_g=g_ffn_pre,
                       ffn_post_g=g_ffn_post)
    g_meta = jnp.sum(dh0[:, PAD_FRONT:PAD_FRONT + N_META], axis=0)
    small_rows = [row_of(small_parts[n]) for n in SMALL] + [row_of(g_meta)]
    n_small = sum(r.shape[0] for r in small_rows)
    small = jnp.pad(jnp.concatenate(small_rows, axis=0), ((0, -(-n_small // 8) * 8 - n_small), (0, 0)))
    all_small = gather_small(small)
    small_t = small.shape[0]

    def sum8_fn(*slabs):
        acc = slabs[0]
        for s in slabs[1:]:
            acc = acc + s
        return acc

    (g_small,) = rowwise("sum_small", sum8_fn, [(all_small.reshape(8 * small_t, d), d, 0, k) for k in range(8)],
                         [], [], [(d, F32)], tm=small_t, n_rows=small_t)

    off = 0
    for n, part in zip(SMALL, small_rows[:-1]):
        rows = part.shape[0]
        update(n, [g_small[off:off + rows, :d].reshape(-1)[:wts[n].size]])
        off += rows
    update("meta_tokens", [lax.dynamic_slice_in_dim(g_small[off:off + N_META, :d], my_chip * mcols, mcols, axis=1)])

    loss = lax.psum(loss_part[0, 0], ("x", "y", "c"))

    def shaped(n, a):
        return a.reshape((1,) + wts[n].shape) if n in BIG else a.reshape(wts[n].shape)

    return (loss, grad_x, *[shaped(n, updates[n][k]) for k in range(4) for n in WEIGHTS])
```

```python
import functools
import math

import jax
import jax.numpy as jnp
from jax import lax
from jax.experimental import pallas as pl
from jax.experimental.pallas import tpu as pltpu

F32 = jnp.float32
BF16 = jnp.bfloat16
MESH = pl.DeviceIdType.MESH

N_META = 16
NORM_EPS = 1e-6
HEAD = 128
ROPE = 64
ROPE_HALF = ROPE // 2
QK_PAD = 2 * HEAD
ROPE_THETA = 10000.0
SEQ_BLOCK = 256
PAD_FRONT = SEQ_BLOCK - N_META
NEG = -1e30
VMEM_LIMIT = 56 * 1024 * 1024
ATTN_HEADS_PER_STEP = 1
ATTN_TILE_MAX = 768

ADAM_LR, ADAM_B1, ADAM_B2, ADAM_EPS, ADAM_WD, ADAM_STEP = 0.001, 0.9, 0.999, 1e-08, 0.01, 10

BIG = ("w_in", "w_hg_o", "w_q_b", "w_kv_b", "w_mla_o", "w_out", "w_ffn_in", "w_ffn_out")
COL_SHARDED = ("w_in", "w_q_b", "w_kv_b", "w_ffn_in")
SMALL = ("b_gate", "lb_logits", "hg_norm_g", "q_a_norm_g", "kv_a_norm_g", "mix_pre_g", "mix_post_g",
         "ffn_pre_g", "ffn_post_g")
WEIGHTS = ("meta_tokens", "w_in", "b_gate", "lb_logits", "hg_norm_g", "w_hg_o", "q_a_norm_g", "w_q_b",
           "kv_a_norm_g", "w_kv_b", "w_mla_o", "w_out", "mix_pre_g", "mix_post_g", "ffn_pre_g", "ffn_post_g",
           "w_ffn_in", "w_ffn_out")


def _tile(n, cap, unit=128):
    if n <= cap:
        return n
    best = None
    for t in range(unit, cap + 1, unit):
        if n % t == 0:
            best = t
    assert best is not None, (n, cap, unit)
    return best


def _sigmoid(x):
    return 1.0 / (1.0 + jnp.exp(-x))


def _bf(x):
    return x.astype(BF16)


def rowwise(name, fn, row_ins, seq_tabs, consts, row_outs, acc_outs=(), tm=SEQ_BLOCK, n_rows=None):
    t_rows = row_ins[0][0].shape[0] if n_rows is None else n_rows
    nt = t_rows // tm
    assert t_rows % tm == 0
    n_in = len(row_ins) + len(seq_tabs) + len(consts)
    n_row = len(row_outs)

    def body(*refs):
        vals = [r[...].astype(F32) for r in refs[:n_in]]
        res = fn(*vals)
        if not isinstance(res, (tuple, list)):
            res = (res,)
        outs = refs[n_in:]
        for k in range(n_row):
            outs[k][...] = res[k].astype(outs[k].dtype)
        if acc_outs:
            @pl.when(pl.program_id(0) == 0)
            def _():
                for k in range(len(acc_outs)):
                    outs[n_row + k][...] = jnp.zeros_like(outs[n_row + k])

            for k in range(len(acc_outs)):
                outs[n_row + k][...] += res[n_row + k]

    row_ins = [tuple(e) + (0,) * (4 - len(e)) for e in row_ins]
    in_specs = [pl.BlockSpec((tm, w), functools.partial(lambda i, j, ro: (i + ro, j), j=j, ro=ro))
                for (_, w, j, ro) in row_ins]
    for tab in seq_tabs:
        per = tab.shape[0] // tm
        in_specs.append(pl.BlockSpec((tm, tab.shape[1]), functools.partial(lambda i, per: (i % per, 0), per=per)))
    for c in consts:
        in_specs.append(pl.BlockSpec(c.shape, lambda i: (0, 0)))
    out_specs = [pl.BlockSpec((tm, w), lambda i: (i, 0)) for (w, _) in row_outs]
    out_specs += [pl.BlockSpec(s, lambda i: (0, 0)) for s in acc_outs]
    out_shape = [jax.ShapeDtypeStruct((t_rows, w), dt) for (w, dt) in row_outs]
    out_shape += [jax.ShapeDtypeStruct(s, F32) for s in acc_outs]
    res = pl.pallas_call(
        body, name=name, grid=(nt,), in_specs=in_specs, out_specs=out_specs, out_shape=out_shape,
        compiler_params=pltpu.CompilerParams(dimension_semantics=("arbitrary",)),
    )(*[e[0] for e in row_ins], *seq_tabs, *consts)
    return res


def matmul(name, a, b, mode, out_dtype=F32):
    if mode != "tn":
        return _matmul_resident(name, a if isinstance(a, (list, tuple)) else [a],
                                b if isinstance(b, (list, tuple)) else [b], mode, out_dtype)
    kdim, m = a.shape
    n = b.shape[1]
    tn = _tile(n, 1536)
    tm, tk = _tile(m, 1408 if tn <= 1024 else 1024), _tile(kdim, 1024)
    nk = kdim // tk

    def body(a_ref, b_ref, o_ref, acc_ref):
        k = pl.program_id(2)

        @pl.when(k == 0)
        def _():
            acc_ref[...] = jnp.zeros_like(acc_ref)

        acc_ref[...] += lax.dot_general(a_ref[...], b_ref[...], TN_DIMS, preferred_element_type=F32)

        @pl.when(k == nk - 1)
        def _():
            o_ref[...] = acc_ref[...].astype(o_ref.dtype)

    return pl.pallas_call(
        body, name=name, grid=(m // tm, n // tn, nk),
        in_specs=[pl.BlockSpec((tk, tm), lambda i, j, k: (k, i)), pl.BlockSpec((tk, tn), lambda i, j, k: (k, j))],
        out_specs=pl.BlockSpec((tm, tn), lambda i, j, k: (i, j)),
        out_shape=jax.ShapeDtypeStruct((m, n), out_dtype),
        scratch_shapes=[pltpu.VMEM((tm, tn), F32)],
        compiler_params=pltpu.CompilerParams(dimension_semantics=("arbitrary", "arbitrary", "arbitrary"),
                                             vmem_limit_bytes=VMEM_LIMIT),
    )(a, b)


def _matmul_resident(name, a_list, b_list, mode, out_dtype):
    m = a_list[0].shape[0]
    n = b_list[0].shape[1] if mode == "nn" else b_list[0].shape[0]
    k_total = sum(a.shape[1] for a in a_list)
    out_bytes = 2 if out_dtype == BF16 else 4
    budget = VMEM_LIMIT - 4 * k_total * n - (6 << 20)
    tm = 512
    while tm > 128 and 2 * tm * (2 * k_total + out_bytes * n) > budget:
        tm //= 2
    tm = _tile(m, tm)
    cn = _tile(n, 1024)
    npairs = len(a_list)

    def body(*refs):
        a_refs, b_refs, o_ref = refs[:npairs], refs[npairs:2 * npairs], refs[2 * npairs]
        for c in range(n // cn):
            acc = None
            for a_ref, b_ref in zip(a_refs, b_refs):
                if mode == "nn":
                    part = jnp.dot(a_ref[...], b_ref[:, pl.ds(c * cn, cn)], preferred_element_type=F32)
                else:
                    part = lax.dot_general(a_ref[...], b_ref[pl.ds(c * cn, cn), :], NT_DIMS,
                                           preferred_element_type=F32)
                acc = part if acc is None else acc + part
            o_ref[:, pl.ds(c * cn, cn)] = acc.astype(o_ref.dtype)

    in_specs = [pl.BlockSpec((tm, a.shape[1]), lambda i: (i, 0)) for a in a_list]
    in_specs += [pl.BlockSpec(b.shape, lambda i: (0, 0)) for b in b_list]
    return pl.pallas_call(
        body, name=name, grid=(m // tm,), in_specs=in_specs,
        out_specs=pl.BlockSpec((tm, n), lambda i: (i, 0)),
        out_shape=jax.ShapeDtypeStruct((m, n), out_dtype),
        compiler_params=pltpu.CompilerParams(dimension_semantics=("arbitrary",), vmem_limit_bytes=VMEM_LIMIT),
    )(*a_list, *b_list)


def _rms(x, g):
    r = lax.rsqrt(jnp.mean(x * x, axis=-1, keepdims=True) + NORM_EPS)
    return x * r * g


def _rms_bwd(x, g, dy):
    r = lax.rsqrt(jnp.mean(x * x, axis=-1, keepdims=True) + NORM_EPS)
    xh = x * r
    dyg = dy * g
    dx = r * (dyg - xh * jnp.mean(dyg * xh, axis=-1, keepdims=True))
    return dx, jnp.sum(dy * xh, axis=0, keepdims=True)


def _silu(x):
    return x * _sigmoid(x)


def _silu_grad(x):
    s = _sigmoid(x)
    return s * (1.0 + x * (1.0 - s))


def _rope(xs, cos, s_up, s_dn):
    return xs * cos + pltpu.roll(xs, ROPE_HALF, 1) * s_up + pltpu.roll(xs, HEAD - ROPE_HALF, 1) * s_dn


def _rope_bwd(dy, cos, s_up, s_dn):
    return dy * cos + pltpu.roll(dy * s_up, HEAD - ROPE_HALF, 1) + pltpu.roll(dy * s_dn, ROPE_HALF, 1)


HG_SUB = 128
HG_LEVELS = 7
HG_E_ROWS = (HG_LEVELS + 1) * HG_SUB
TN_DIMS = (((0,), (0,)), ((), ()))
NT_DIMS = (((1,), (1,)), ((), ()))


def _hg_constants():
    import numpy as np
    n = HG_SUB
    r = np.arange(n)[:, None]
    c = np.arange(n)[None, :]
    cs, ps = [], []
    for lvl in range(HG_LEVELS):
        m = (n // 2) >> lvl
        upper = (r % (2 * m)) >= m
        mid = (r // (2 * m)) * (2 * m) + m - 1
        cs.append(np.where(upper, (c > mid) & (c <= r), (c > r) & (c <= mid)))
        ps.append(((r // (2 * m)) == (c // (2 * m))) & upper & ((c % (2 * m)) < m))
    cs.append(c <= r)
    cs.append(np.ones((8, n), bool))
    cstack = np.concatenate(cs, 0).astype(np.float32)
    pstack = np.concatenate(ps, 0).astype(np.float32)
    pstack_t = np.concatenate([p.T for p in ps], 0).astype(np.float32)
    return (jnp.asarray(cstack, BF16), jnp.asarray(cstack[:HG_E_ROWS].T, BF16), jnp.asarray(pstack, F32),
            jnp.asarray(pstack_t, F32))


def _split_dot(c_bf, x):
    hi = _bf(x)
    lo = _bf(x - hi.astype(F32))
    r2 = jnp.dot(c_bf, jnp.concatenate([hi, lo], axis=1), preferred_element_type=F32)
    return r2[:, :HEAD] + r2[:, HEAD:]


def _hg_gates(hq, hf, lb):
    sq = _sigmoid(hq)
    sg = _sigmoid(hf)
    fg = lb + (1.0 - lb) * sg
    return sq, hq * sq, sg, fg, 1.0 - fg, jnp.log(fg)


def _hg_block_fwd(st, hq, hf, hi, lb, cstack, p_ref):
    _, q, _, _, k, g = _hg_gates(hq, hf, lb)
    v = hi
    e = _split_dot(cstack, g)
    bc = e[HG_LEVELS * HG_SUB:HG_E_ROWS]
    b_last = jnp.tile(e[HG_E_ROWS:], (HG_SUB // 8, 1))
    a = jnp.zeros((HG_SUB, HG_SUB), F32)
    for lvl in range(HG_LEVELS):
        x = jnp.exp(e[lvl * HG_SUB:(lvl + 1) * HG_SUB])
        a = a + p_ref[pl.ds(lvl * HG_SUB, HG_SUB), :] * lax.dot_general(_bf(q * x), _bf(k * x), NT_DIMS,
                                                                          preferred_element_type=F32)
    a_bf = _bf(a)
    diag = jnp.sum(q * k, axis=1, keepdims=True)
    o = (jnp.dot(a_bf, _bf(v), preferred_element_type=F32) + diag * v
         + lax.dot_general(_bf(q * jnp.exp(bc)), _bf(st), NT_DIMS, preferred_element_type=F32))
    kd = k * jnp.exp(b_last - bc)
    st_out = st * jnp.exp(b_last) + lax.dot_general(_bf(v), _bf(kd), TN_DIMS, preferred_element_type=F32)
    return st_out, o, a_bf


def _hg_block_bwd(st, dst_out, do, hq, hf, hi, lb, a_bf, cstack, cstack_t, p_ref, pt_ref):
    sq, q, sg, fg, k, g = _hg_gates(hq, hf, lb)
    v = hi
    e = _split_dot(cstack, g)
    bc = e[HG_LEVELS * HG_SUB:HG_E_ROWS]
    b_last = jnp.tile(e[HG_E_ROWS:], (HG_SUB // 8, 1))
    eb = jnp.exp(bc)
    qb = q * eb
    er = jnp.exp(b_last - bc)
    kd = k * er
    e_last = jnp.exp(b_last)
    do_bf, v_bf, dst_bf = _bf(do), _bf(v), _bf(dst_out)
    da = lax.dot_general(do_bf, v_bf, NT_DIMS, preferred_element_type=F32)
    dat = lax.dot_general(v_bf, do_bf, NT_DIMS, preferred_element_type=F32)
    d_diag = jnp.sum(do * v, axis=1, keepdims=True)
    dv = (lax.dot_general(a_bf, do_bf, TN_DIMS, preferred_element_type=F32)
          + jnp.sum(q * k, axis=1, keepdims=True) * do
          + lax.dot_general(_bf(kd), dst_bf, NT_DIMS, preferred_element_type=F32))
    dqb = jnp.dot(do_bf, _bf(st), preferred_element_type=F32)
    dst = dst_out * e_last + lax.dot_general(do_bf, _bf(qb), TN_DIMS, preferred_element_type=F32)
    dkd = jnp.dot(v_bf, dst_bf, preferred_element_type=F32)
    dq = dqb * eb + d_diag * k
    dk = dkd * er + d_diag * q
    d_last = (jnp.sum(dst_out * st * e_last, axis=0, keepdims=True)
              + jnp.sum(dkd * kd, axis=0, keepdims=True))
    des = []
    for lvl in range(HG_LEVELS):
        x = jnp.exp(e[lvl * HG_SUB:(lvl + 1) * HG_SUB])
        qh, kh = q * x, k * x
        dm = _bf(p_ref[pl.ds(lvl * HG_SUB, HG_SUB), :] * da)
        dmt = _bf(pt_ref[pl.ds(lvl * HG_SUB, HG_SUB), :] * dat)
        dqh = jnp.dot(dm, _bf(kh), preferred_element_type=F32)
        dkh = jnp.dot(dmt, _bf(qh), preferred_element_type=F32)
        dq = dq + dqh * x
        dk = dk + dkh * x
        des.append(dqh * qh + dkh * kh)
    des.append(dqb * qb - dkd * kd)
    dg = _split_dot(cstack_t, jnp.concatenate(des, axis=0)) + d_last
    dfg = dg / fg - dk
    dhq = dq * (sq * (1.0 + hq * (1.0 - sq)))
    dhf = dfg * (1.0 - lb) * sg * (1.0 - sg)
    return dst, dhq, dhf, dv, jnp.sum(dfg * (1.0 - sg), axis=0, keepdims=True)


def hgrn_fwd(proj_main, lb, consts, bl, lp, d):
    nh = d // HEAD
    rows_blk = _tile(lp, 768, SEQ_BLOCK)
    nb = lp // rows_blk
    spb = rows_blk // HG_SUB
    cstack, _, pstack, _ = consts

    def body(hq_ref, hf_ref, hi_ref, lb_ref, c_ref, p_ref, o_ref, st_ref, a_ref, s_ref):
        j = pl.program_id(2)

        @pl.when(j == 0)
        def _():
            s_ref[...] = jnp.zeros_like(s_ref)

        lbv = lb_ref[...]
        cs = c_ref[...]

        def sub(n, carry):
            r = pl.multiple_of(n * HG_SUB, HG_SUB)
            st = s_ref[...]
            st_ref[0, 0, pl.ds(n, 1)] = st[None]
            st_out, o, a_bf = _hg_block_fwd(st, hq_ref[pl.ds(r, HG_SUB), :].astype(F32),
                                            hf_ref[pl.ds(r, HG_SUB), :].astype(F32),
                                            hi_ref[pl.ds(r, HG_SUB), :].astype(F32), lbv, cs, p_ref)
            s_ref[...] = st_out
            o_ref[pl.ds(r, HG_SUB), :] = o
            a_ref[0, 0, pl.ds(n, 1)] = a_bf[None]
            return carry

        lax.fori_loop(0, spb, sub, 0, unroll=3 if spb % 3 == 0 else 2)

    def colspec(off):
        return pl.BlockSpec((rows_blk, HEAD), functools.partial(lambda h, b, j, off: (b * nb + j, off + h), off=off))

    whole = lambda arr: pl.BlockSpec(arr.shape, lambda h, b, j: (0, 0))
    return pl.pallas_call(
        body, name="hgrn_fwd", grid=(nh, bl, nb),
        in_specs=[colspec(0), colspec(nh), colspec(2 * nh), pl.BlockSpec((1, HEAD), lambda h, b, j: (0, h)),
                  whole(cstack), whole(pstack)],
        out_specs=[pl.BlockSpec((rows_blk, HEAD), lambda h, b, j: (b * nb + j, h)),
                   pl.BlockSpec((1, 1, spb, HEAD, HEAD), lambda h, b, j: (b, h, j, 0, 0)),
                   pl.BlockSpec((1, 1, spb, HG_SUB, HG_SUB), lambda h, b, j: (b, h, j, 0, 0))],
        out_shape=[jax.ShapeDtypeStruct((bl * lp, d), F32),
                   jax.ShapeDtypeStruct((bl, nh, lp // HG_SUB, HEAD, HEAD), F32),
                   jax.ShapeDtypeStruct((bl, nh, lp // HG_SUB, HG_SUB, HG_SUB), BF16)],
        scratch_shapes=[pltpu.VMEM((HEAD, HEAD), F32)],
        compiler_params=pltpu.CompilerParams(dimension_semantics=("arbitrary", "arbitrary", "arbitrary")),
    )(proj_main, proj_main, proj_main, lb, cstack, pstack)


def hgrn_bwd(proj_main, lb, consts, states, a_mats, do_scan, bl, lp, d):
    nh = d // HEAD
    rows_blk = _tile(lp, 768, SEQ_BLOCK)
    nb = lp // rows_blk
    spb = rows_blk // HG_SUB
    cstack, cstack_t, pstack, pstack_t = consts

    def body(hq_ref, hf_ref, hi_ref, lb_ref, c_ref, ct_ref, p_ref, pt_ref, st_ref, a_ref, do_ref,
             dq_ref, df_ref, di_ref, dlb_ref, ds_ref):
        b_id, j = pl.program_id(1), pl.program_id(2)
        blk = nb - 1 - j

        @pl.when(j == 0)
        def _():
            ds_ref[...] = jnp.zeros_like(ds_ref)

        @pl.when((j == 0) & (b_id == 0))
        def _():
            dlb_ref[...] = jnp.zeros_like(dlb_ref)

        lbv = lb_ref[...]
        cs = c_ref[...]
        cst = ct_ref[...]

        def sub(i, carry):
            n = spb - 1 - i
            r = pl.multiple_of(n * HG_SUB, HG_SUB)
            dst, dhq, dhf, dhi, dlb = _hg_block_bwd(
                st_ref[0, 0, pl.ds(n, 1)][0], ds_ref[...], do_ref[pl.ds(r, HG_SUB), :],
                hq_ref[pl.ds(r, HG_SUB), :].astype(F32), hf_ref[pl.ds(r, HG_SUB), :].astype(F32),
                hi_ref[pl.ds(r, HG_SUB), :].astype(F32), lbv,
                a_ref[0, 0, pl.ds(n, 1)][0], cs, cst, p_ref, pt_ref)
            ds_ref[...] = dst
            dq_ref[pl.ds(r, HG_SUB), :] = dhq.astype(dq_ref.dtype)
            df_ref[pl.ds(r, HG_SUB), :] = dhf.astype(df_ref.dtype)
            di_ref[pl.ds(r, HG_SUB), :] = dhi.astype(di_ref.dtype)
            dlb_ref[...] += dlb
            return carry

        lax.fori_loop(0, spb, sub, 0, unroll=2)

    def colspec(off):
        return pl.BlockSpec((rows_blk, HEAD),
                            functools.partial(lambda h, b, j, off: (b * nb + nb - 1 - j, off + h), off=off))

    whole = lambda arr: pl.BlockSpec(arr.shape, lambda h, b, j: (0, 0))
    mats = lambda: pl.BlockSpec((1, 1, spb, HEAD, HEAD), lambda h, b, j: (b, h, nb - 1 - j, 0, 0))
    t_rows = bl * lp
    return pl.pallas_call(
        body, name="hgrn_bwd", grid=(nh, bl, nb),
        in_specs=[colspec(0), colspec(nh), colspec(2 * nh), pl.BlockSpec((1, HEAD), lambda h, b, j: (0, h)),
                  whole(cstack), whole(cstack_t), whole(pstack), whole(pstack_t), mats(), mats(), colspec(0)],
        out_specs=[colspec(0), colspec(0), colspec(0), pl.BlockSpec((1, HEAD), lambda h, b, j: (0, h))],
        out_shape=[jax.ShapeDtypeStruct((t_rows, d), BF16)] * 3 + [jax.ShapeDtypeStruct((1, d), F32)],
        scratch_shapes=[pltpu.VMEM((HEAD, HEAD), F32)],
        compiler_params=pltpu.CompilerParams(dimension_semantics=("arbitrary", "arbitrary", "arbitrary")),
    )(proj_main, proj_main, proj_main, lb, cstack, cstack_t, pstack, pstack_t, states, a_mats, do_scan)


def _allowed(row0, col0, nr, nc, transposed=False):
    if transposed:
        col = col0 + lax.broadcasted_iota(jnp.int32, (nc, 1), 0)
        row = row0 + lax.broadcasted_iota(jnp.int32, (1, nr), 1)
    else:
        row = row0 + lax.broadcasted_iota(jnp.int32, (nr, 1), 0)
        col = col0 + lax.broadcasted_iota(jnp.int32, (1, nc), 1)
    return (col <= row) & ((col >= PAD_FRONT) | (row < PAD_FRONT))


def attn_fwd(q_cat, k_cat, v, bl, lp, nm, scale):
    tq = tk = SEQ_BLOCK
    nq = lp // tq

    def body(q_ref, k_ref, v_ref, o_ref, lse_ref, m_ref, l_ref, acc_ref):
        i = pl.program_id(2)
        q = q_ref[...]
        m_ref[...] = jnp.full_like(m_ref, NEG)
        l_ref[...] = jnp.zeros_like(l_ref)
        acc_ref[...] = jnp.zeros_like(acc_ref)

        def kstep(c, carry):
            c0 = pl.multiple_of(c * tk, tk)
            s = lax.dot_general(q, k_ref[pl.ds(c0, tk), :], NT_DIMS, preferred_element_type=F32) * scale
            s = jnp.where(_allowed(i * tq, c * tk, tq, tk), s, NEG)
            m_old = m_ref[...]
            m_new = jnp.maximum(m_old, jnp.max(s, axis=1, keepdims=True))
            alpha = jnp.exp(m_old - m_new)
            p = jnp.exp(s - m_new)
            l_ref[...] = alpha * l_ref[...] + jnp.sum(p, axis=1, keepdims=True)
            acc_ref[...] = alpha * acc_ref[...] + jnp.dot(_bf(p), v_ref[pl.ds(c0, tk), :],
                                                          preferred_element_type=F32)
            m_ref[...] = m_new
            return carry

        lax.fori_loop(0, i + 1, kstep, 0)
        o_ref[...] = (acc_ref[...] / l_ref[...]).astype(o_ref.dtype)
        lse_ref[0, 0] = m_ref[...] + jnp.log(l_ref[...])

    return pl.pallas_call(
        body, name="attn_fwd", grid=(bl, nm, nq),
        in_specs=[pl.BlockSpec((tq, QK_PAD), lambda b, h, i: (b * nq + i, h)),
                  pl.BlockSpec((lp, QK_PAD), lambda b, h, i: (b, h)),
                  pl.BlockSpec((lp, HEAD), lambda b, h, i: (b, h))],
        out_specs=[pl.BlockSpec((tq, HEAD), lambda b, h, i: (b * nq + i, h)),
                   pl.BlockSpec((1, 1, tq, 1), lambda b, h, i: (b, h, i, 0))],
        out_shape=[jax.ShapeDtypeStruct((bl * lp, nm * HEAD), BF16),
                   jax.ShapeDtypeStruct((bl, nm, lp, 1), F32)],
        scratch_shapes=[pltpu.VMEM((tq, 1), F32), pltpu.VMEM((tq, 1), F32), pltpu.VMEM((tq, HEAD), F32)],
        compiler_params=pltpu.CompilerParams(dimension_semantics=("arbitrary", "arbitrary", "arbitrary")),
    )(q_cat, k_cat, v)


def attn_bwd_dq(q_cat, k_cat, v, o, do, lse, bl, lp, nm, scale):
    tq = tk = SEQ_BLOCK
    nq = lp // tq

    def body(q_ref, k_ref, v_ref, o_ref, do_ref, lse_ref, dq_ref, dl_ref, acc_ref):
        i = pl.program_id(2)
        q = q_ref[...]
        do_b = do_ref[...]
        delta = jnp.sum(o_ref[...].astype(F32) * do_b.astype(F32), axis=1, keepdims=True)
        lse_b = lse_ref[0, 0]
        acc_ref[...] = jnp.zeros_like(acc_ref)

        def kstep(c, carry):
            c0 = pl.multiple_of(c * tk, tk)
            ks = k_ref[pl.ds(c0, tk), :]
            s = lax.dot_general(q, ks, NT_DIMS, preferred_element_type=F32) * scale
            p = jnp.where(_allowed(i * tq, c * tk, tq, tk), jnp.exp(s - lse_b), 0.0)
            dp = lax.dot_general(do_b, v_ref[pl.ds(c0, tk), :], NT_DIMS, preferred_element_type=F32)
            ds = p * (dp - delta)
            acc_ref[...] += jnp.dot(_bf(ds), ks, preferred_element_type=F32)
            return carry

        lax.fori_loop(0, i + 1, kstep, 0)
        dq_ref[...] = acc_ref[...] * scale
        dl_ref[0, 0] = delta

    return pl.pallas_call(
        body, name="attn_bwd_dq", grid=(bl, nm, nq),
        in_specs=[pl.BlockSpec((tq, QK_PAD), lambda b, h, i: (b * nq + i, h)),
                  pl.BlockSpec((lp, QK_PAD), lambda b, h, i: (b, h)),
                  pl.BlockSpec((lp, HEAD), lambda b, h, i: (b, h)),
                  pl.BlockSpec((tq, HEAD), lambda b, h, i: (b * nq + i, h)),
                  pl.BlockSpec((tq, HEAD), lambda b, h, i: (b * nq + i, h)),
                  pl.BlockSpec((1, 1, tq, 1), lambda b, h, i: (b, h, i, 0))],
        out_specs=[pl.BlockSpec((tq, QK_PAD), lambda b, h, i: (b * nq + i, h)),
                   pl.BlockSpec((1, 1, tq, 1), lambda b, h, i: (b, h, i, 0))],
        out_shape=[jax.ShapeDtypeStruct((bl * lp, nm * QK_PAD), F32),
                   jax.ShapeDtypeStruct((bl, nm, lp, 1), F32)],
        scratch_shapes=[pltpu.VMEM((tq, QK_PAD), F32)],
        compiler_params=pltpu.CompilerParams(dimension_semantics=("arbitrary", "arbitrary", "arbitrary")),
    )(q_cat, k_cat, v, o, do, lse)


def attn_bwd_dkv(q_cat, k_cat, v, do, lse_row, delta_row, bl, lp, nm, scale):
    tq = tk = SEQ_BLOCK
    nq = lp // tq

    def body(q_ref, k_ref, v_ref, do_ref, lse_ref, dl_ref, dk_ref, dv_ref):
        i = pl.program_id(2)
        kt = k_ref[...]
        vt = v_ref[...]
        dk_ref[...] = jnp.zeros_like(dk_ref)
        dv_ref[...] = jnp.zeros_like(dv_ref)

        def qstep(c, carry):
            c0 = pl.multiple_of(c * tq, tq)
            qs = q_ref[pl.ds(c0, tq), :]
            dos = do_ref[pl.ds(c0, tq), :]
            st = lax.dot_general(kt, qs, NT_DIMS, preferred_element_type=F32) * scale
            pt = jnp.where(_allowed(c * tq, i * tk, tq, tk, transposed=True),
                           jnp.exp(st - lse_ref[0, 0, pl.ds(c, 1)][0]), 0.0)
            dv_ref[...] += jnp.dot(_bf(pt), dos, preferred_element_type=F32)
            dpt = lax.dot_general(vt, dos, NT_DIMS, preferred_element_type=F32)
            dst = pt * (dpt - dl_ref[0, 0, pl.ds(c, 1)][0])
            dk_ref[...] += jnp.dot(_bf(dst), qs, preferred_element_type=F32)
            return carry

        lax.fori_loop(i, nq, qstep, 0)
        dk_ref[...] = dk_ref[...] * scale

    return pl.pallas_call(
        body, name="attn_bwd_dkv", grid=(bl, nm, nq),
        in_specs=[pl.BlockSpec((lp, QK_PAD), lambda b, h, i: (b, h)),
                  pl.BlockSpec((tk, QK_PAD), lambda b, h, i: (b * nq + i, h)),
                  pl.BlockSpec((tk, HEAD), lambda b, h, i: (b * nq + i, h)),
                  pl.BlockSpec((lp, HEAD), lambda b, h, i: (b, h)),
                  pl.BlockSpec((1, 1, nq, 1, tq), lambda b, h, i: (b, h, 0, 0, 0)),
                  pl.BlockSpec((1, 1, nq, 1, tq), lambda b, h, i: (b, h, 0, 0, 0))],
        out_specs=[pl.BlockSpec((tk, QK_PAD), lambda b, h, i: (b * nq + i, h)),
                   pl.BlockSpec((tk, HEAD), lambda b, h, i: (b * nq + i, h))],
        out_shape=[jax.ShapeDtypeStruct((bl * lp, nm * QK_PAD), F32),
                   jax.ShapeDtypeStruct((bl * lp, nm * HEAD), F32)],
        compiler_params=pltpu.CompilerParams(dimension_semantics=("arbitrary", "arbitrary", "arbitrary")),
    )(q_cat, k_cat, v, do, lse_row, delta_row)


def _key_query_mask(key0, qry0, nk, nq_, causal):
    key = key0 + lax.broadcasted_iota(jnp.int32, (nk, 1), 0)
    if not causal:
        return key >= PAD_FRONT
    qry = qry0 + lax.broadcasted_iota(jnp.int32, (1, nq_), 1)
    return (key <= qry) & (key >= PAD_FRONT)


def _attn_tile(lp):
    return _tile(lp, ATTN_TILE_MAX, SEQ_BLOCK)


def attn_fwd_t(q_cat, k_cat, v_t, bl, lp, nm, scale):
    tq = tk = _attn_tile(lp)
    nq = lp // tq
    hp = ATTN_HEADS_PER_STEP
    assert nm % hp == 0

    def body(q_ref, k_ref, vt_ref, o_ref, lse_ref, m_ref, l_ref, acc_ref):
        i = pl.program_id(2)
        m_ref[...] = jnp.full_like(m_ref, NEG)
        l_ref[...] = jnp.zeros_like(l_ref)
        acc_ref[...] = jnp.zeros_like(acc_ref)

        def step(c, mask):
            c0 = pl.multiple_of(c * tk, tk)
            for hh in range(hp):
                cols = pl.ds(hh * QK_PAD, QK_PAD)
                st = lax.dot_general(k_ref[pl.ds(c0, tk), cols], q_ref[:, cols], NT_DIMS,
                                     preferred_element_type=F32) * scale
                if mask is not None:
                    st = jnp.where(_key_query_mask(c * tk, i * tq, tk, tq, mask == "causal"), st, NEG)
                m_old = m_ref[hh]
                m_new = jnp.maximum(m_old, jnp.max(st, axis=0, keepdims=True))
                alpha = jnp.exp(m_old - m_new)
                pt = jnp.exp(st - m_new)
                l_ref[hh] = alpha * l_ref[hh] + jnp.sum(pt, axis=0, keepdims=True)
                acc_ref[hh] = alpha * acc_ref[hh] + jnp.dot(vt_ref[0, hh, pl.ds(c, 1)][0], _bf(pt),
                                                            preferred_element_type=F32)
                m_ref[hh] = m_new

        def mid(c, carry):
            step(c, None)
            return carry

        @pl.when(i == 0)
        def _():
            step(0, "causal")

        @pl.when(i > 0)
        def _():
            step(0, "pad")
            lax.fori_loop(1, i, mid, 0)
            step(i, "causal")

        for hh in range(hp):
            o_ref[:, pl.ds(hh * HEAD, HEAD)] = jnp.transpose(acc_ref[hh] / l_ref[hh]).astype(o_ref.dtype)
            lse_ref[0, hh, 0] = m_ref[hh] + jnp.log(l_ref[hh])

    return pl.pallas_call(
        body, name="attn_fwd", grid=(bl, nm // hp, nq),
        in_specs=[pl.BlockSpec((tq, hp * QK_PAD), lambda b, h, i: (b * nq + i, h)),
                  pl.BlockSpec((lp, hp * QK_PAD), lambda b, h, i: (b, h)),
                  pl.BlockSpec((1, hp, nq, HEAD, tk), lambda b, h, i: (b, h, 0, 0, 0))],
        out_specs=[pl.BlockSpec((tq, hp * HEAD), lambda b, h, i: (b * nq + i, h)),
                   pl.BlockSpec((1, hp, 1, 1, tq), lambda b, h, i: (b, h, i, 0, 0))],
        out_shape=[jax.ShapeDtypeStruct((bl * lp, nm * HEAD), BF16),
                   jax.ShapeDtypeStruct((bl, nm, nq, 1, tq), F32)],
        scratch_shapes=[pltpu.VMEM((hp, 1, tq), F32), pltpu.VMEM((hp, 1, tq), F32), pltpu.VMEM((hp, HEAD, tq), F32)],
        compiler_params=pltpu.CompilerParams(dimension_semantics=("arbitrary", "arbitrary", "arbitrary")),
    )(q_cat, k_cat, v_t)


def attn_bwd_t(q_cat, k_cat, k_t, v, o, do, lse, bl, lp, nm, scale):
    tq = tk = _attn_tile(lp)
    nq = lp // tq
    hp = ATTN_HEADS_PER_STEP
    assert nm % hp == 0

    def body(q_ref, k_ref, kt_ref, v_ref, o_ref, do_ref, lse_ref, dq_ref, dk_ref, dv_ref, dqt_ref, dka_ref, dva_ref):
        i = pl.program_id(2)

        @pl.when(i == 0)
        def _():
            dqt_ref[...] = jnp.zeros_like(dqt_ref)

        dka_ref[...] = jnp.zeros_like(dka_ref)
        dva_ref[...] = jnp.zeros_like(dva_ref)
        ones8 = jnp.ones((8, HEAD), BF16)

        def step(c, mask):
            c0 = pl.multiple_of(c * tq, tq)
            for hh in range(hp):
                qcols, vcols = pl.ds(hh * QK_PAD, QK_PAD), pl.ds(hh * HEAD, HEAD)
                qs = q_ref[pl.ds(c0, tq), qcols]
                dos = do_ref[pl.ds(c0, tq), vcols]
                prod = dos.astype(F32) * o_ref[pl.ds(c0, tq), vcols].astype(F32)
                hi = _bf(prod)
                lo = _bf(prod - hi.astype(F32))
                delta8 = (lax.dot_general(ones8, hi, NT_DIMS, preferred_element_type=F32)
                          + lax.dot_general(ones8, lo, NT_DIMS, preferred_element_type=F32))
                st = lax.dot_general(k_ref[:, qcols], qs, NT_DIMS, preferred_element_type=F32) * scale
                pt = jnp.exp(st - lse_ref[0, hh, pl.ds(c, 1)][0])
                if mask is not None:
                    pt = jnp.where(_key_query_mask(i * tk, c * tq, tk, tq, mask == "causal"), pt, 0.0)
                dva_ref[hh] += jnp.dot(_bf(pt), dos, preferred_element_type=F32)
                dpt = lax.dot_general(v_ref[:, vcols], dos, NT_DIMS, preferred_element_type=F32)
                dst = _bf(pt * (dpt - jnp.tile(delta8, (tk // 8, 1))))
                dka_ref[hh] += jnp.dot(dst, qs, preferred_element_type=F32)
                dqt_ref[hh, pl.ds(c, 1)] += jnp.dot(kt_ref[0, hh, 0], dst, preferred_element_type=F32)[None]

        step(i, "causal")

        def rest_masked(c, carry):
            step(c, "pad")
            return carry

        def rest(c, carry):
            step(c, None)
            return carry

        @pl.when(i == 0)
        def _():
            lax.fori_loop(1, nq, rest_masked, 0)

        @pl.when(i > 0)
        def _():
            lax.fori_loop(i + 1, nq, rest, 0)

        for hh in range(hp):
            dk_ref[:, pl.ds(hh * QK_PAD, QK_PAD)] = (dka_ref[hh] * scale).astype(dk_ref.dtype)
            dv_ref[:, pl.ds(hh * HEAD, HEAD)] = dva_ref[hh].astype(dv_ref.dtype)

        @pl.when(i == nq - 1)
        def _():
            for hh in range(hp):
                for c in range(nq):
                    dq_ref[pl.ds(c * tq, tq), pl.ds(hh * QK_PAD, QK_PAD)] = (
                        jnp.transpose(dqt_ref[hh, c]) * scale).astype(dq_ref.dtype)

    return pl.pallas_call(
        body, name="attn_bwd", grid=(bl, nm // hp, nq),
        in_specs=[pl.BlockSpec((lp, hp * QK_PAD), lambda b, h, i: (b, h)),
                  pl.BlockSpec((tk, hp * QK_PAD), lambda b, h, i: (b * nq + i, h)),
                  pl.BlockSpec((1, hp, 1, QK_PAD, tk), lambda b, h, i: (b, h, i, 0, 0)),
                  pl.BlockSpec((tk, hp * HEAD), lambda b, h, i: (b * nq + i, h)),
                  pl.BlockSpec((lp, hp * HEAD), lambda b, h, i: (b, h)),
                  pl.BlockSpec((lp, hp * HEAD), lambda b, h, i: (b, h)),
                  pl.BlockSpec((1, hp, nq, 1, tq), lambda b, h, i: (b, h, 0, 0, 0))],
        out_specs=[pl.BlockSpec((lp, hp * QK_PAD), lambda b, h, i: (b, h)),
                   pl.BlockSpec((tk, hp * QK_PAD), lambda b, h, i: (b * nq + i, h)),
                   pl.BlockSpec((tk, hp * HEAD), lambda b, h, i: (b * nq + i, h))],
        out_shape=[jax.ShapeDtypeStruct((bl * lp, nm * QK_PAD), BF16),
                   jax.ShapeDtypeStruct((bl * lp, nm * QK_PAD), BF16),
                   jax.ShapeDtypeStruct((bl * lp, nm * HEAD), BF16)],
        scratch_shapes=[pltpu.VMEM((hp, nq, QK_PAD, tq), F32), pltpu.VMEM((hp, tk, QK_PAD), F32),
                        pltpu.VMEM((hp, tk, HEAD), F32)],
        compiler_params=pltpu.CompilerParams(dimension_semantics=("arbitrary", "arbitrary", "arbitrary")),
    )(q_cat, k_cat, k_t, v, o, do, lse)


def _place():
    return lax.axis_index("x"), lax.axis_index("y"), lax.axis_index("c")


def gather_shards(packed):
    hbm = pl.BlockSpec(memory_space=pl.ANY)

    def body(src_ref, out_ref, send_sems, recv_sems, local_sem):
        x, y, c = _place()
        me = 2 * x + y
        chips = [(1 - x, y), (x, 1 - y), (1 - x, 1 - y)]
        local = pltpu.make_async_copy(src_ref, out_ref.at[me], local_sem)
        local.start()
        sends = []
        for k, (px, py) in enumerate(chips):
            cp = pltpu.make_async_remote_copy(src_ref=src_ref, dst_ref=out_ref.at[me], send_sem=send_sems.at[k],
                                              recv_sem=recv_sems.at[k], device_id=(px, py, c), device_id_type=MESH)
            cp.start()
            sends.append(cp)
        for k, (px, py) in enumerate(chips):
            pltpu.make_async_remote_copy(src_ref=src_ref, dst_ref=out_ref.at[2 * px + py], send_sem=send_sems.at[k],
                                         recv_sem=recv_sems.at[k], device_id=(px, py, c),
                                         device_id_type=MESH).wait_recv()
        for cp in sends:
            cp.wait_send()
        local.wait()

    return pl.pallas_call(
        body, name="gather_shards", in_specs=[hbm], out_specs=hbm,
        out_shape=jax.ShapeDtypeStruct((4,) + packed.shape, packed.dtype),
        scratch_shapes=[pltpu.SemaphoreType.DMA((3,)), pltpu.SemaphoreType.DMA((3,)), pltpu.SemaphoreType.DMA],
    )(packed)


def gather_small(small):
    hbm = pl.BlockSpec(memory_space=pl.ANY)

    def body(small_ref, all_ref, send_sems, recv_sems, local_sem):
        x, y, c = _place()
        me = 4 * x + 2 * y + c
        local = pltpu.make_async_copy(small_ref, all_ref.at[me], local_sem)
        local.start()
        others = [(x ^ ((r >> 2) & 1), y ^ ((r >> 1) & 1), c ^ (r & 1)) for r in range(1, 8)]
        sends = []
        for r, peer in enumerate(others):
            cp = pltpu.make_async_remote_copy(src_ref=small_ref, dst_ref=all_ref.at[me], send_sem=send_sems.at[r],
                                              recv_sem=recv_sems.at[r], device_id=peer, device_id_type=MESH)
            cp.start()
            sends.append(cp)
        for r, (px, py, pc) in enumerate(others):
            pltpu.make_async_remote_copy(src_ref=small_ref, dst_ref=all_ref.at[4 * px + 2 * py + pc],
                                         send_sem=send_sems.at[r], recv_sem=recv_sems.at[r],
                                         device_id=(px, py, pc), device_id_type=MESH).wait_recv()
        for cp in sends:
            cp.wait_send()
        local.wait()

    return pl.pallas_call(
        body, name="gather_small", in_specs=[hbm], out_specs=hbm,
        out_shape=jax.ShapeDtypeStruct((8,) + small.shape, small.dtype),
        scratch_shapes=[pltpu.SemaphoreType.DMA((7,)), pltpu.SemaphoreType.DMA((7,)), pltpu.SemaphoreType.DMA],
    )(small)


def swap_with_sibling(name, parts):
    n = len(parts)
    hbm = pl.BlockSpec(memory_space=pl.ANY)

    def body(*refs):
        x, y, c = _place()
        cps = [pltpu.make_async_remote_copy(src_ref=refs[j], dst_ref=refs[n + j], send_sem=refs[2 * n].at[j],
                                            recv_sem=refs[2 * n + 1].at[j], device_id=(x, y, 1 - c),
                                            device_id_type=MESH) for j in range(n)]
        for cp in cps:
            cp.start()
        for cp in cps:
            cp.wait()

    return pl.pallas_call(
        body, name=name, in_specs=[hbm] * n, out_specs=[hbm] * n,
        out_shape=[jax.ShapeDtypeStruct(p.shape, p.dtype) for p in parts],
        scratch_shapes=[pltpu.SemaphoreType.DMA((n,)), pltpu.SemaphoreType.DMA((n,))],
    )(*parts)


def _chips3():
    x, y, c = _place()
    return [(1 - x, y, c), (x, 1 - y, c), (1 - x, 1 - y, c)]


def _push_copies(src_refs, land_refs, send_sems, recv_sems, per_chip):
    cps = []
    for j, (src_ref, land_ref) in enumerate(zip(src_refs, land_refs)):
        for k, (px, py, pc) in enumerate(_chips3()):
            part = src_ref.at[2 * px + py] if per_chip else src_ref
            cps.append(pltpu.make_async_remote_copy(
                src_ref=part, dst_ref=land_ref.at[k], send_sem=send_sems.at[3 * j + k],
                recv_sem=recv_sems.at[3 * j + k], device_id=(px, py, pc), device_id_type=MESH))
    return cps


def push_start(name, srcs, per_chip):
    n = len(srcs)
    hbm = pl.BlockSpec(memory_space=pltpu.HBM)
    sem = pl.BlockSpec(memory_space=pltpu.SEMAPHORE)
    lands = [lax.empty((3,) + s.shape[-2:], s.dtype) for s in srcs]

    def body(*refs):
        src_refs, land_refs = refs[:n], refs[n:2 * n]
        send_sems, recv_sems = refs[2 * n], refs[2 * n + 1]
        for cp in _push_copies(src_refs, land_refs, send_sems, recv_sems, per_chip):
            cp.start()
        refs[-1][...] = jnp.zeros_like(refs[-1])

    outs = pl.pallas_call(
        body, name=name,
        out_shape=(pltpu.SemaphoreType.DMA((3 * n,)), pltpu.SemaphoreType.DMA((3 * n,)),
                   *[pltpu.HBM(a.shape, a.dtype) for a in list(srcs) + lands], jax.ShapeDtypeStruct((8, HEAD), F32)),
        in_specs=(hbm,) * (2 * n),
        out_specs=(sem, sem) + (hbm,) * (2 * n) + (pl.BlockSpec(memory_space=pltpu.VMEM),),
        input_output_aliases={j: 2 + j for j in range(2 * n)},
        compiler_params=pltpu.CompilerParams(has_side_effects=pltpu.SideEffectType.DATAFLOW_SIDE_EFFECTING),
    )(*[pltpu.with_memory_space_constraint(a, pltpu.HBM) for a in list(srcs) + lands])
    return tuple(outs[:-1]), outs[-1]


def push_wait(name, handle, after, per_chip):
    send_sems, recv_sems = handle[0], handle[1]
    thru = handle[2:]
    n = len(thru) // 2
    hbm = pl.BlockSpec(memory_space=pltpu.HBM)
    sem = pl.BlockSpec(memory_space=pltpu.SEMAPHORE)

    def body(*refs):
        src_refs, land_refs = refs[:n], refs[n:2 * n]
        for cp in _push_copies(src_refs, land_refs, refs[2 * n], refs[2 * n + 1], per_chip):
            cp.wait_send()
            cp.wait_recv()

    outs = pl.pallas_call(
        body, name=name,
        out_shape=tuple(pltpu.HBM(a.shape, a.dtype) for a in thru),
        in_specs=(hbm,) * (2 * n) + (sem, sem, pl.BlockSpec(memory_space=pl.ANY)), out_specs=(hbm,) * (2 * n),
        input_output_aliases={j: j for j in range(2 * n)},
        compiler_params=pltpu.CompilerParams(has_side_effects=pltpu.SideEffectType.DATAFLOW_SIDE_EFFECTING),
    )(*thru, send_sems, recv_sems, after)
    return outs[:n], outs[n:]


def by_chip(own, landed, my_chip):
    by_rel = jnp.stack([own, landed[1], landed[0], landed[2]])
    return [lax.dynamic_index_in_dim(by_rel, jnp.bitwise_xor(s, my_chip), axis=0, keepdims=False) for s in range(4)]


def adamw(name, w, g_parts, m, v):
    r, c = w.shape
    tr = r if r * c <= 65536 else _tile(r, 128, 8)
    ng = len(g_parts)

    def body(*refs):
        w_ref, m_ref, v_ref = refs[0], refs[1 + ng], refs[2 + ng]
        g_ref, d_ref, nm_ref, nv_ref = refs[3 + ng:]
        gv = refs[1][...]
        for k in range(1, ng):
            gv = gv + refs[1 + k][...]
        m_new = ADAM_B1 * m_ref[...] + (1.0 - ADAM_B1) * gv
        v_new = ADAM_B2 * v_ref[...] + (1.0 - ADAM_B2) * (gv * gv)
        m_hat = m_new / (1.0 - ADAM_B1 ** ADAM_STEP)
        v_hat = v_new / (1.0 - ADAM_B2 ** ADAM_STEP)
        g_ref[...] = gv
        d_ref[...] = -ADAM_LR * (m_hat / (jnp.sqrt(v_hat) + ADAM_EPS) + ADAM_WD * w_ref[...])
        nm_ref[...] = m_new
        nv_ref[...] = v_new

    spec = pl.BlockSpec((tr, c), lambda i: (i, 0))
    return pl.pallas_call(
        body, name=name, grid=(r // tr,), in_specs=[spec] * (3 + ng), out_specs=[spec] * 4,
        out_shape=[jax.ShapeDtypeStruct((r, c), F32)] * 4,
        compiler_params=pltpu.CompilerParams(dimension_semantics=("arbitrary",)),
    )(w, *g_parts, m, v)


def split_full(name, full, s):
    if name in COL_SHARDED:
        c = full.shape[1] // 4
        return full[:, s * c:(s + 1) * c]
    r = full.shape[0] // 4
    return full[s * r:(s + 1) * r]


def join_shards(name, shards):
    return jnp.concatenate(shards, axis=1 if name in COL_SHARDED else 0)


def kernel(x, meta_tokens, w_in, b_gate, lb_logits, hg_norm_g, w_hg_o, q_a_norm_g, w_q_b, kv_a_norm_g, w_kv_b, w_mla_o, w_out, mix_pre_g, mix_post_g, ffn_pre_g, ffn_post_g, w_ffn_in, w_ffn_out, loss_target, m_meta_tokens, m_w_in, m_b_gate, m_lb_logits, m_hg_norm_g, m_w_hg_o, m_q_a_norm_g, m_w_q_b, m_kv_a_norm_g, m_w_kv_b, m_w_mla_o, m_w_out, m_mix_pre_g, m_mix_post_g, m_ffn_pre_g, m_ffn_post_g, m_w_ffn_in, m_w_ffn_out, v_meta_tokens, v_w_in, v_b_gate, v_lb_logits, v_hg_norm_g, v_w_hg_o, v_q_a_norm_g, v_w_q_b, v_kv_a_norm_g, v_w_kv_b, v_w_mla_o, v_w_out, v_mix_pre_g, v_mix_post_g, v_ffn_pre_g, v_ffn_post_g, v_w_ffn_in, v_w_ffn_out):
    wts = dict(meta_tokens=meta_tokens, w_in=w_in[0], b_gate=b_gate, lb_logits=lb_logits, hg_norm_g=hg_norm_g,
               w_hg_o=w_hg_o[0], q_a_norm_g=q_a_norm_g, w_q_b=w_q_b[0], kv_a_norm_g=kv_a_norm_g, w_kv_b=w_kv_b[0],
               w_mla_o=w_mla_o[0], w_out=w_out[0], mix_pre_g=mix_pre_g, mix_post_g=mix_post_g, ffn_pre_g=ffn_pre_g,
               ffn_post_g=ffn_post_g, w_ffn_in=w_ffn_in[0], w_ffn_out=w_ffn_out[0])
    mom_m = dict(meta_tokens=m_meta_tokens, w_in=m_w_in[0], b_gate=m_b_gate, lb_logits=m_lb_logits,
                 hg_norm_g=m_hg_norm_g, w_hg_o=m_w_hg_o[0], q_a_norm_g=m_q_a_norm_g, w_q_b=m_w_q_b[0],
                 kv_a_norm_g=m_kv_a_norm_g, w_kv_b=m_w_kv_b[0], w_mla_o=m_w_mla_o[0], w_out=m_w_out[0],
                 mix_pre_g=m_mix_pre_g, mix_post_g=m_mix_post_g, ffn_pre_g=m_ffn_pre_g, ffn_post_g=m_ffn_post_g,
                 w_ffn_in=m_w_ffn_in[0], w_ffn_out=m_w_ffn_out[0])
    mom_v = dict(meta_tokens=v_meta_tokens, w_in=v_w_in[0], b_gate=v_b_gate, lb_logits=v_lb_logits,
                 hg_norm_g=v_hg_norm_g, w_hg_o=v_w_hg_o[0], q_a_norm_g=v_q_a_norm_g, w_q_b=v_w_q_b[0],
                 kv_a_norm_g=v_kv_a_norm_g, w_kv_b=v_w_kv_b[0], w_mla_o=v_w_mla_o[0], w_out=v_w_out[0],
                 mix_pre_g=v_mix_pre_g, mix_post_g=v_mix_post_g, ffn_pre_g=v_ffn_pre_g, ffn_post_g=v_ffn_post_g,
                 w_ffn_in=v_w_ffn_in[0], w_ffn_out=v_w_ffn_out[0])

    bl, seq, d = x.shape
    lp = PAD_FRONT + N_META + seq
    t_rows = bl * lp
    nh = d // HEAD
    ql, kvl = wts["w_q_b"].shape[0], wts["w_kv_b"].shape[0]
    nm = (4 * wts["w_mla_o"].shape[0]) // HEAD
    ffn = 4 * wts["w_ffn_out"].shape[0]
    mla_w = ql + kvl + HEAD
    assert ql == kvl and ql % HEAD == 0 and seq % SEQ_BLOCK == 0 and d % HEAD == 0
    scale = (HEAD + ROPE) ** -0.5
    my_chip = 2 * lax.axis_index("x") + lax.axis_index("y")

    mcols = meta_tokens.shape[1]
    meta_all = gather_shards(meta_tokens)
    meta_full = jnp.concatenate([meta_all[s] for s in range(4)], axis=1)

    def start_gather(name, names, order_after):
        srcs = [_bf(wts[n]) for n in names]
        if order_after is not None:
            srcs[0] = srcs[0] + order_after[0, 0].astype(BF16)
        return push_start(name, srcs, per_chip=False)

    def finish_gather(name, names, started, after):
        owns, landed = push_wait(name, started[0], after, per_chip=False)
        return {n: join_shards(n, by_chip(own, land, my_chip)) for n, own, land in zip(names, owns, landed)}

    rest_names = tuple(n for n in BIG if n != "w_in")
    gather_1 = start_gather("gather_w_in_start", ("w_in",), meta_all[0, :1, :1] * 0.0)
    gather_2 = start_gather("gather_rest_start", rest_names, gather_1[1])

    h0 = jnp.concatenate([jnp.zeros((bl, PAD_FRONT, d), F32), jnp.broadcast_to(meta_full[None], (bl, N_META, d)), x],
                         axis=1).reshape(t_rows, d)
    tgt = jnp.concatenate([jnp.zeros((bl, PAD_FRONT + N_META, d), F32), loss_target], axis=1).reshape(t_rows, d)
    pos = (jnp.arange(lp, dtype=jnp.int32) - PAD_FRONT).astype(F32)
    inv_freq = 1.0 / (ROPE_THETA ** (jnp.arange(0, ROPE, 2, dtype=F32) / ROPE))
    ang = pos[:, None] * inv_freq[None, :]
    zeros32 = jnp.zeros((lp, ROPE_HALF), F32)
    zeros64 = jnp.zeros((lp, HEAD - ROPE), F32)
    t_cos = jnp.concatenate([jnp.cos(ang), jnp.cos(ang), zeros64], axis=1)
    t_up = jnp.concatenate([zeros32, jnp.sin(ang), zeros64], axis=1)
    t_dn = jnp.concatenate([-jnp.sin(ang), zeros32, zeros64], axis=1)
    real = jnp.broadcast_to((jnp.arange(lp) >= PAD_FRONT + N_META).astype(F32)[:, None], (lp, d))
    lb_soft = jax.nn.softmax(lb_logits.astype(F32), axis=0)
    lb = lb_soft[0:1]

    (u1,) = rowwise("norm_mix_pre", lambda h, g: _rms(h, g), [(h0, d, 0)], [], [mix_pre_g + gather_2[1][0, 0]],
                    [(d, BF16)])
    full = finish_gather("gather_w_in_wait", ("w_in",), gather_1, u1)
    w_main = jnp.concatenate([full["w_in"][:, :4 * d], full["w_in"][:, -2 * d:]], axis=1)
    w_mla = jnp.pad(full["w_in"][:, 4 * d:4 * d + ql + kvl + ROPE], ((0, 0), (0, HEAD - ROPE)))
    proj_main = matmul("proj_main", u1, w_main, "nn", out_dtype=BF16)
    proj_mla = matmul("proj_mla", u1, w_mla, "nn", out_dtype=BF16)
    hg_consts = _hg_constants()
    o_scan, states, a_mats = hgrn_fwd(proj_main, lb, hg_consts, bl, lp, d)

    def hg_out_fn(o, hg, g):
        return jnp.concatenate([_rms(o[:, h * HEAD:(h + 1) * HEAD], g) for h in range(nh)], axis=1) * _silu(hg)

    (o_hg,) = rowwise("hgrn_out", hg_out_fn, [(o_scan, d, 0), (proj_main, d, 3)], [], [hg_norm_g], [(d, BF16)])
    full.update(finish_gather("gather_rest_wait", rest_names, gather_2, o_hg))
    w_qb = jnp.pad(full["w_q_b"].reshape(ql, nm, HEAD + ROPE), ((0, 0), (0, 0), (0, QK_PAD - HEAD - ROPE))
                   ).reshape(ql, nm * QK_PAD)
    w_kvb = full["w_kv_b"]
    y_a = matmul("y_a", o_hg, _bf(full["w_hg_o"]), "nn", out_dtype=BF16)

    qn, kvn = rowwise("mla_norms", lambda cq, ckv, gq, gk: (_rms(cq, gq), _rms(ckv, gk)),
                      [(proj_mla, ql, 0), (proj_mla, kvl, 1)], [], [q_a_norm_g, kv_a_norm_g],
                      [(ql, BF16), (kvl, BF16)])
    q_full = matmul("q_up", qn, w_qb, "nn", out_dtype=BF16)
    kv_full = matmul("kv_up", kvn, w_kvb, "nn", out_dtype=BF16)

    def mla_prep_fn(qf, kvf, kpe, cos, s_up, s_dn):
        kpe_r = _rope(kpe, cos, s_up, s_dn)
        qs, ks, vs = [], [], []
        for h in range(nm):
            qs += [qf[:, h * QK_PAD:h * QK_PAD + HEAD], _rope(qf[:, h * QK_PAD + HEAD:(h + 1) * QK_PAD], cos, s_up, s_dn)]
            ks += [kvf[:, h * QK_PAD:h * QK_PAD + HEAD], kpe_r]
            vs += [kvf[:, h * QK_PAD + HEAD:(h + 1) * QK_PAD]]
        return jnp.concatenate(qs, axis=1), jnp.concatenate(ks, axis=1), jnp.concatenate(vs, axis=1)

    kpe_blk = (ql + kvl) // HEAD
    q_cat, k_cat, v_att = rowwise("mla_prep", mla_prep_fn,
                                  [(q_full, nm * QK_PAD, 0), (kv_full, nm * QK_PAD, 0), (proj_mla, HEAD, kpe_blk)],
                                  [t_cos, t_up, t_dn], [], [(nm * QK_PAD, BF16), (nm * QK_PAD, BF16), (nm * HEAD, BF16)])
    at = _attn_tile(lp)
    v_t = v_att.reshape(bl, lp // at, at, nm, HEAD).transpose(0, 3, 1, 4, 2)
    k_t = k_cat.reshape(bl, lp // at, at, nm, QK_PAD).transpose(0, 3, 1, 4, 2)
    o_mla, lse = attn_fwd_t(q_cat, k_cat, v_t, bl, lp, nm, scale)
    y_b = matmul("y_b", o_mla, _bf(full["w_mla_o"]), "nn", out_dtype=BF16)

    def gate_fn(ya, yb, ga, gb, bias):
        return _sigmoid(ga + bias[:, :d]) * ya + _sigmoid(gb + bias[:, d:]) * yb

    (z,) = rowwise("gate_mix", gate_fn, [(y_a, d, 0), (y_b, d, 0), (proj_main, d, 4), (proj_main, d, 5)], [],
                   [b_gate], [(d, BF16)])
    mixed = matmul("mixed", z, _bf(full["w_out"]), "nn")

    def mid_fn(h, mx, g_post, g_pre):
        h1 = h + _rms(mx, g_post)
        return h1, _rms(h1, g_pre)

    h1, u2 = rowwise("norm_mid", mid_fn, [(h0, d, 0), (mixed, d, 0)], [], [mix_post_g, ffn_pre_g],
                     [(d, F32), (d, BF16)])
    gu = matmul("ffn_in", u2, _bf(full["w_ffn_in"]), "nn", out_dtype=BF16)
    (act,) = rowwise("swiglu", lambda gt, up: _silu(gt) * up, [(gu, ffn, 0), (gu, ffn, 1)], [], [], [(ffn, BF16)])
    f_out = matmul("ffn_out", act, _bf(full["w_ffn_out"]), "nn")

    def loss_fn(h1v, fv, tg, realv, g_post):
        h2 = h1v + _rms(fv, g_post)
        diff = (h2 - tg) * realv
        part = jnp.broadcast_to(0.5 * jnp.sum(diff * diff, keepdims=True) / d, (1, HEAD))
        dy = diff / d
        df, dg = _rms_bwd(fv, g_post, dy)
        return dy, df, part, dg

    dy, df, loss_part, g_ffn_post = rowwise("loss_head", loss_fn, [(h1, d, 0), (f_out, d, 0), (tgt, d, 0)], [real],
                                            [ffn_post_g], [(d, F32), (d, BF16)], [(1, HEAD), (1, d)])
    grads = {}
    d_act = matmul("d_act", df, _bf(full["w_ffn_out"]), "nt", out_dtype=BF16)
    grads["w_ffn_out"] = matmul("gw_ffn_out", act, df, "tn")

    def swiglu_bwd_fn(gt, up, da):
        return jnp.concatenate([da * up * _silu_grad(gt), da * _silu(gt)], axis=1)

    (dgu,) = rowwise("swiglu_bwd", swiglu_bwd_fn, [(gu, ffn, 0), (gu, ffn, 1), (d_act, ffn, 0)], [], [],
                     [(2 * ffn, BF16)])
    du2 = matmul("d_u2", dgu, _bf(full["w_ffn_in"]), "nt", out_dtype=BF16)
    grads["w_ffn_in"] = matmul("gw_ffn_in", u2, dgu, "tn")

    def mid_bwd_fn(dyv, h1v, du2v, mx, g_pre, g_post):
        dx, dg_pre = _rms_bwd(h1v, g_pre, du2v)
        dh1 = dyv + dx
        dmx, dg_post = _rms_bwd(mx, g_post, dh1)
        return dh1, dmx, dg_pre, dg_post

    dh1, dmixed, g_ffn_pre, g_mix_post = rowwise("norm_mid_bwd", mid_bwd_fn,
                                                 [(dy, d, 0), (h1, d, 0), (du2, d, 0), (mixed, d, 0)], [],
                                                 [ffn_pre_g, mix_post_g], [(d, F32), (d, BF16)], [(1, d), (1, d)])
    dz = matmul("d_z", dmixed, _bf(full["w_out"]), "nt", out_dtype=BF16)
    grads["w_out"] = matmul("gw_out", z, dmixed, "tn")

    def gate_bwd_fn(dzv, ya, yb, ga, gb, bias):
        sa, sb = _sigmoid(ga + bias[:, :d]), _sigmoid(gb + bias[:, d:])
        dga = dzv * ya * sa * (1.0 - sa)
        dgb = dzv * yb * sb * (1.0 - sb)
        dgates = jnp.concatenate([dga, dgb], axis=1)
        return dzv * sa, dzv * sb, dgates, jnp.sum(dgates, axis=0, keepdims=True)

    dy_a, dy_b, dgates, g_b_gate = rowwise("gate_mix_bwd", gate_bwd_fn,
                                           [(dz, d, 0), (y_a, d, 0), (y_b, d, 0), (proj_main, d, 4), (proj_main, d, 5)],
                                           [], [b_gate], [(d, BF16), (d, BF16), (2 * d, BF16)], [(1, 2 * d)])
    do_hg = matmul("d_o_hg", dy_a, _bf(full["w_hg_o"]), "nt", out_dtype=BF16)
    grads["w_hg_o"] = matmul("gw_hg_o", o_hg, dy_a, "tn")
    do_mla = matmul("d_o_mla", dy_b, _bf(full["w_mla_o"]), "nt", out_dtype=BF16)
    grads["w_mla_o"] = matmul("gw_mla_o", o_mla, dy_b, "tn")

    early = ("w_hg_o", "w_mla_o", "w_out", "w_ffn_in", "w_ffn_out")
    late = ("w_in", "w_q_b", "w_kv_b")

    def start_grads(name, names):
        sends = [_bf(jnp.stack([split_full(n, grads[n], s) for s in range(4)])) for n in names]
        mines = []
        for n in names:
            r, c = wts[n].shape
            axis, size = (1, c) if n in COL_SHARDED else (0, r)
            mines.append(lax.dynamic_slice_in_dim(grads[n], my_chip * size, size, axis=axis))
        handle, token = push_start(name, sends, per_chip=True)
        return handle, token, mines

    def finish_grads(tag, names, started, after):
        handle, _, mines = started
        _, landed = push_wait(f"grads_{tag}_wait", handle, after, per_chip=True)
        parts = []
        for n, mine, land in zip(names, mines, landed):
            r, c = mine.shape
            tr = _tile(r, 256, 16)
            land2 = land.reshape(3 * r, c)
            parts.append(rowwise(f"sum_chips_{n}", lambda a, r0, r1, r2: a + r0 + r1 + r2,
                                 [(mine, c, 0)] + [(land2, c, 0, k * (r // tr)) for k in range(3)],
                                 [], [], [(c, F32)], tm=tr)[0])
        sibs = swap_with_sibling(f"swap_{tag}", parts)
        return {n: [p, s] for n, p, s in zip(names, parts, sibs)}

    grads_early = start_grads("grads_early_start", early)
    token_a = grads_early[1]

    def hg_out_bwd_fn(do, o, hg, g):
        sg = _silu(hg)
        dn = do * sg
        dos, dgs, ons = [], 0.0, []
        for h in range(nh):
            sl = slice(h * HEAD, (h + 1) * HEAD)
            dx, dg = _rms_bwd(o[:, sl], g, dn[:, sl])
            dos.append(dx)
            dgs = dgs + dg
            ons.append(_rms(o[:, sl], g))
        dhg = do * jnp.concatenate(ons, axis=1) * _silu_grad(hg)
        return jnp.concatenate(dos, axis=1), dhg, dgs

    do_scan, dhg, g_hg_norm = rowwise("hgrn_out_bwd", hg_out_bwd_fn, [(do_hg, d, 0), (o_scan, d, 0), (proj_main, d, 3)],
                                      [], [hg_norm_g], [(d, F32), (d, BF16)], [(1, HEAD)])
    dhq, dhf, dhi, g_lb = hgrn_bwd(proj_main, lb + token_a[0, 0], hg_consts, states, a_mats, do_scan, bl, lp, d)

    dq_cat, dk_cat, dv_att = attn_bwd_t(q_cat, k_cat, k_t, v_att, o_mla, do_mla, lse, bl, lp, nm, scale)

    def mla_prep_bwd_fn(dqc, dkc, dvv, cos, s_up, s_dn):
        dqs, dkvs, dkpe = [], [], 0.0
        for h in range(nm):
            dqs += [dqc[:, h * QK_PAD:h * QK_PAD + HEAD],
                    _rope_bwd(dqc[:, h * QK_PAD + HEAD:(h + 1) * QK_PAD], cos, s_up, s_dn)]
            dkvs += [dkc[:, h * QK_PAD:h * QK_PAD + HEAD], dvv[:, h * HEAD:(h + 1) * HEAD]]
            dkpe = dkpe + dkc[:, h * QK_PAD + HEAD:(h + 1) * QK_PAD]
        return jnp.concatenate(dqs, axis=1), jnp.concatenate(dkvs, axis=1), _rope_bwd(dkpe, cos, s_up, s_dn)

    dq_full, dkv_full, dkpe = rowwise("mla_prep_bwd", mla_prep_bwd_fn,
                                      [(dq_cat, nm * QK_PAD, 0), (dk_cat, nm * QK_PAD, 0), (dv_att, nm * HEAD, 0)],
                                      [t_cos, t_up, t_dn], [],
                                      [(nm * QK_PAD, BF16), (nm * QK_PAD, BF16), (HEAD, F32)])
    dqn = matmul("d_qn", dq_full, w_qb, "nt", out_dtype=BF16)
    g_wqb = matmul("gw_q_b", qn, dq_full, "tn")
    grads["w_q_b"] = g_wqb.reshape(ql, nm, QK_PAD)[:, :, :HEAD + ROPE].reshape(ql, nm * (HEAD + ROPE))
    dkvn = matmul("d_kvn", dkv_full, w_kvb, "nt", out_dtype=BF16)
    grads["w_kv_b"] = matmul("gw_kv_b", kvn, dkv_full, "tn")

    def mla_norms_bwd_fn(dqnv, dkvnv, cq, ckv, dkpev, gq, gk):
        dcq, dgq = _rms_bwd(cq, gq, dqnv)
        dckv, dgk = _rms_bwd(ckv, gk, dkvnv)
        return jnp.concatenate([dcq, dckv, dkpev], axis=1), dgq, dgk

    dmla, g_q_norm, g_kv_norm = rowwise("mla_norms_bwd", mla_norms_bwd_fn,
                                        [(dqn, ql, 0), (dkvn, kvl, 0), (proj_mla, ql, 0), (proj_mla, kvl, 1),
                                         (dkpe, HEAD, 0)], [], [q_a_norm_g, kv_a_norm_g],
                                        [(mla_w, BF16)], [(1, ql), (1, kvl)])

    d_pieces = [dhq, dhf, dhi, dhg, dgates, dmla]
    gw_parts = [matmul(f"gw_in_{k}", u1, dp, "tn") for k, dp in enumerate(d_pieces)]
    grads["w_in"] = jnp.concatenate(gw_parts[:4] + [gw_parts[5][:, :ql + kvl + ROPE], gw_parts[4]], axis=1)
    grads_late = start_grads("grads_late_start", late)
    w_mla_after = w_mla + grads_late[1][0, 0].astype(BF16)
    w_pieces = [w_main[:, 0:d], w_main[:, d:2 * d], w_main[:, 2 * d:3 * d], w_main[:, 3 * d:4 * d],
                w_main[:, 4 * d:6 * d], w_mla_after]
    du1 = matmul("d_u1", d_pieces, w_pieces, "nt")

    def first_bwd_fn(dh1v, h, du1v, g):
        dx, dg = _rms_bwd(h, g, du1v)
        return dh1v + dx, dg

    dh0, g_mix_pre = rowwise("norm_mix_pre_bwd", first_bwd_fn, [(dh1, d, 0), (h0, d, 0), (du1, d, 0)], [],
                             [mix_pre_g], [(d, F32)], [(1, d)])
    dh0 = dh0.reshape(bl, lp, d)
    grad_x = dh0[:, PAD_FRONT + N_META:]

    g_parts = finish_grads("early", early, grads_early, g_mix_pre)
    updates = {}

    def update(n, parts):
        w2 = wts[n].reshape(-1, wts[n].shape[-1])
        updates[n] = adamw("adamw_" + n, w2, [p.reshape(w2.shape) for p in parts], mom_m[n].reshape(w2.shape),
                           mom_v[n].reshape(w2.shape))

    for n in early:
        update(n, g_parts[n])
    g_parts = finish_grads("late", late, grads_late, updates[early[-1]][0])
    for n in late:
        update(n, g_parts[n])
    p0 = lb_soft[0:1]
    g_lb_logits = jnp.concatenate([g_lb * p0 * (1.0 - p0), -g_lb * p0 * (1.0 - p0)], axis=0)

    def row_of(vec):
        return vec.reshape(-1, d) if vec.size >= d else jnp.pad(vec.reshape(1, -1), ((0, 0), (0, d - vec.size)))

    small_parts = dict(b_gate=g_b_gate, lb_logits=g_lb_logits, hg_norm_g=g_hg_norm, q_a_norm_g=g_q_norm,
                       kv_a_norm_g=g_kv_norm, mix_pre_g=g_mix_pre, mix_post_g=g_mix_post, ffn_pre_g=g_ffn_pre,
                       ffn_post_g=g_ffn_post)
    g_meta = jnp.sum(dh0[:, PAD_FRONT:PAD_FRONT + N_META], axis=0)
    small_rows = [row_of(small_parts[n]) for n in SMALL] + [row_of(g_meta)]
    n_small = sum(r.shape[0] for r in small_rows)
    small = jnp.pad(jnp.concatenate(small_rows, axis=0), ((0, -(-n_small // 8) * 8 - n_small), (0, 0)))
    all_small = gather_small(small)
    small_t = small.shape[0]

    def sum8_fn(*slabs):
        acc = slabs[0]
        for s in slabs[1:]:
            acc = acc + s
        return acc

    (g_small,) = rowwise("sum_small", sum8_fn, [(all_small.reshape(8 * small_t, d), d, 0, k) for k in range(8)],
                         [], [], [(d, F32)], tm=small_t, n_rows=small_t)

    off = 0
    for n, part in zip(SMALL, small_rows[:-1]):
        rows = part.shape[0]
        update(n, [g_small[off:off + rows, :d].reshape(-1)[:wts[n].size]])
        off += rows
    update("meta_tokens", [lax.dynamic_slice_in_dim(g_small[off:off + N_META, :d], my_chip * mcols, mcols, axis=1)])

    loss = lax.psum(loss_part[0, 0], ("x", "y", "c"))

    def shaped(n, a):
        return a.reshape((1,) + wts[n].shape) if n in BIG else a.reshape(wts[n].shape)

    return (loss, grad_x, *[shaped(n, updates[n][k]) for k in range(4) for n in WEIGHTS])
```

```python
import functools
import math

import jax
import jax.numpy as jnp
from jax import lax
from jax.experimental import pallas as pl
from jax.experimental.pallas import tpu as pltpu

F32 = jnp.float32
BF16 = jnp.bfloat16
MESH = pl.DeviceIdType.MESH

N_META = 16
NORM_EPS = 1e-6
HEAD = 128
ROPE = 64
ROPE_HALF = ROPE // 2
QK_PAD = 2 * HEAD
ROPE_THETA = 10000.0
SEQ_BLOCK = 256
PAD_FRONT = SEQ_BLOCK - N_META
NEG = -1e30
VMEM_LIMIT = 56 * 1024 * 1024
ATTN_HEADS_PER_STEP = 1
ATTN_TILE_MAX = 768

ADAM_LR, ADAM_B1, ADAM_B2, ADAM_EPS, ADAM_WD, ADAM_STEP = 0.001, 0.9, 0.999, 1e-08, 0.01, 10

BIG = ("w_in", "w_hg_o", "w_q_b", "w_kv_b", "w_mla_o", "w_out", "w_ffn_in", "w_ffn_out")
COL_SHARDED = ("w_in", "w_q_b", "w_kv_b", "w_ffn_in")
SMALL = ("b_gate", "lb_logits", "hg_norm_g", "q_a_norm_g", "kv_a_norm_g", "mix_pre_g", "mix_post_g",
         "ffn_pre_g", "ffn_post_g")
WEIGHTS = ("meta_tokens", "w_in", "b_gate", "lb_logits", "hg_norm_g", "w_hg_o", "q_a_norm_g", "w_q_b",
           "kv_a_norm_g", "w_kv_b", "w_mla_o", "w_out", "mix_pre_g", "mix_post_g", "ffn_pre_g", "ffn_post_g",
           "w_ffn_in", "w_ffn_out")


def _tile(n, cap, unit=128):
    if n <= cap:
        return n
    best = None
    for t in range(unit, cap + 1, unit):
        if n % t == 0:
            best = t
    assert best is not None, (n, cap, unit)
    return best


def _sigmoid(x):
    return 1.0 / (1.0 + jnp.exp(-x))


def _bf(x):
    return x.astype(BF16)


def rowwise(name, fn, row_ins, seq_tabs, consts, row_outs, acc_outs=(), tm=SEQ_BLOCK, n_rows=None):
    t_rows = row_ins[0][0].shape[0] if n_rows is None else n_rows
    nt = t_rows // tm
    assert t_rows % tm == 0
    n_in = len(row_ins) + len(seq_tabs) + len(consts)
    n_row = len(row_outs)

    def body(*refs):
        vals = [r[...].astype(F32) for r in refs[:n_in]]
        res = fn(*vals)
        if not isinstance(res, (tuple, list)):
            res = (res,)
        outs = refs[n_in:]
        for k in range(n_row):
            outs[k][...] = res[k].astype(outs[k].dtype)
        if acc_outs:
            @pl.when(pl.program_id(0) == 0)
            def _():
                for k in range(len(acc_outs)):
                    outs[n_row + k][...] = jnp.zeros_like(outs[n_row + k])

            for k in range(len(acc_outs)):
                outs[n_row + k][...] += res[n_row + k]

    row_ins = [tuple(e) + (0,) * (4 - len(e)) for e in row_ins]
    in_specs = [pl.BlockSpec((tm, w), functools.partial(lambda i, j, ro: (ro(i) if callable(ro) else i + ro, j),
                                                        j=j, ro=ro)) for (_, w, j, ro) in row_ins]
    for tab in seq_tabs:
        per = tab.shape[0] // tm
        in_specs.append(pl.BlockSpec((tm, tab.shape[1]), functools.partial(lambda i, per: (i % per, 0), per=per)))
    for c in consts:
        in_specs.append(pl.BlockSpec(c.shape, lambda i: (0, 0)))
    row_outs = [tuple(e) + (t_rows, None)[len(e) - 2:] for e in row_outs]
    out_specs = [pl.BlockSpec((tm, w), functools.partial(lambda i, rm: (i if rm is None else rm(i), 0), rm=rm))
                 for (w, _, _, rm) in row_outs]
    out_specs += [pl.BlockSpec(s, lambda i: (0, 0)) for s in acc_outs]
    out_shape = [jax.ShapeDtypeStruct((rows, w), dt) for (w, dt, rows, _) in row_outs]
    out_shape += [jax.ShapeDtypeStruct(s, F32) for s in acc_outs]
    res = pl.pallas_call(
        body, name=name, grid=(nt,), in_specs=in_specs, out_specs=out_specs, out_shape=out_shape,
        compiler_params=pltpu.CompilerParams(dimension_semantics=("arbitrary",)),
    )(*[e[0] for e in row_ins], *seq_tabs, *consts)
    return res


def matmul(name, a, b, mode, out_dtype=F32):
    if mode != "tn":
        return _matmul_resident(name, a if isinstance(a, (list, tuple)) else [a],
                                b if isinstance(b, (list, tuple)) else [b], mode, out_dtype)
    kdim, m = a.shape
    n = b.shape[1]
    tn = _tile(n, 1536)
    tm, tk = _tile(m, 1408 if tn <= 1024 else 1024), _tile(kdim, 1024)
    nk = kdim // tk

    def body(a_ref, b_ref, o_ref, acc_ref):
        k = pl.program_id(2)

        @pl.when(k == 0)
        def _():
            acc_ref[...] = jnp.zeros_like(acc_ref)

        acc_ref[...] += lax.dot_general(a_ref[...], b_ref[...], TN_DIMS, preferred_element_type=F32)

        @pl.when(k == nk - 1)
        def _():
            o_ref[...] = acc_ref[...].astype(o_ref.dtype)

    return pl.pallas_call(
        body, name=name, grid=(m // tm, n // tn, nk),
        in_specs=[pl.BlockSpec((tk, tm), lambda i, j, k: (k, i)), pl.BlockSpec((tk, tn), lambda i, j, k: (k, j))],
        out_specs=pl.BlockSpec((tm, tn), lambda i, j, k: (i, j)),
        out_shape=jax.ShapeDtypeStruct((m, n), out_dtype),
        scratch_shapes=[pltpu.VMEM((tm, tn), F32)],
        compiler_params=pltpu.CompilerParams(dimension_semantics=("arbitrary", "arbitrary", "arbitrary"),
                                             vmem_limit_bytes=VMEM_LIMIT),
    )(a, b)


def _matmul_resident(name, a_list, b_list, mode, out_dtype):
    m = a_list[0].shape[0]
    n = b_list[0].shape[1] if mode == "nn" else b_list[0].shape[0]
    k_total = sum(a.shape[1] for a in a_list)
    out_bytes = 2 if out_dtype == BF16 else 4
    budget = VMEM_LIMIT - 4 * k_total * n - (6 << 20)
    tm = 512
    while tm > 128 and 2 * tm * (2 * k_total + out_bytes * n) > budget:
        tm //= 2
    tm = _tile(m, tm)
    cn = _tile(n, 1024)
    npairs = len(a_list)

    def body(*refs):
        a_refs, b_refs, o_ref = refs[:npairs], refs[npairs:2 * npairs], refs[2 * npairs]
        for c in range(n // cn):
            acc = None
            for a_ref, b_ref in zip(a_refs, b_refs):
                if mode == "nn":
                    part = jnp.dot(a_ref[...], b_ref[:, pl.ds(c * cn, cn)], preferred_element_type=F32)
                else:
                    part = lax.dot_general(a_ref[...], b_ref[pl.ds(c * cn, cn), :], NT_DIMS,
                                           preferred_element_type=F32)
                acc = part if acc is None else acc + part
            o_ref[:, pl.ds(c * cn, cn)] = acc.astype(o_ref.dtype)

    in_specs = [pl.BlockSpec((tm, a.shape[1]), lambda i: (i, 0)) for a in a_list]
    in_specs += [pl.BlockSpec(b.shape, lambda i: (0, 0)) for b in b_list]
    return pl.pallas_call(
        body, name=name, grid=(m // tm,), in_specs=in_specs,
        out_specs=pl.BlockSpec((tm, n), lambda i: (i, 0)),
        out_shape=jax.ShapeDtypeStruct((m, n), out_dtype),
        compiler_params=pltpu.CompilerParams(dimension_semantics=("arbitrary",), vmem_limit_bytes=VMEM_LIMIT),
    )(*a_list, *b_list)


def _rms(x, g):
    r = lax.rsqrt(jnp.mean(x * x, axis=-1, keepdims=True) + NORM_EPS)
    return x * r * g


def _rms_bwd(x, g, dy):
    r = lax.rsqrt(jnp.mean(x * x, axis=-1, keepdims=True) + NORM_EPS)
    xh = x * r
    dyg = dy * g
    dx = r * (dyg - xh * jnp.mean(dyg * xh, axis=-1, keepdims=True))
    return dx, jnp.sum(dy * xh, axis=0, keepdims=True)


def _silu(x):
    return x * _sigmoid(x)


def _silu_grad(x):
    s = _sigmoid(x)
    return s * (1.0 + x * (1.0 - s))


def _rope(xs, cos, s_up, s_dn):
    return xs * cos + pltpu.roll(xs, ROPE_HALF, 1) * s_up + pltpu.roll(xs, HEAD - ROPE_HALF, 1) * s_dn


def _rope_bwd(dy, cos, s_up, s_dn):
    return dy * cos + pltpu.roll(dy * s_up, HEAD - ROPE_HALF, 1) + pltpu.roll(dy * s_dn, ROPE_HALF, 1)


HG_SUB = 128
HG_LEVELS = 7
HG_E_ROWS = (HG_LEVELS + 1) * HG_SUB
TN_DIMS = (((0,), (0,)), ((), ()))
NT_DIMS = (((1,), (1,)), ((), ()))


def _hg_constants():
    import numpy as np
    n = HG_SUB
    r = np.arange(n)[:, None]
    c = np.arange(n)[None, :]
    cs, ps = [], []
    for lvl in range(HG_LEVELS):
        m = (n // 2) >> lvl
        upper = (r % (2 * m)) >= m
        mid = (r // (2 * m)) * (2 * m) + m - 1
        cs.append(np.where(upper, (c > mid) & (c <= r), (c > r) & (c <= mid)))
        ps.append(((r // (2 * m)) == (c // (2 * m))) & upper & ((c % (2 * m)) < m))
    cs.append(c <= r)
    cs.append(np.ones((8, n), bool))
    cstack = np.concatenate(cs, 0).astype(np.float32)
    pstack = np.concatenate(ps, 0).astype(np.float32)
    pstack_t = np.concatenate([p.T for p in ps], 0).astype(np.float32)
    return (jnp.asarray(cstack, BF16), jnp.asarray(cstack[:HG_E_ROWS].T, BF16), jnp.asarray(pstack, F32),
            jnp.asarray(pstack_t, F32))


def _split_dot(c_bf, x):
    hi = _bf(x)
    lo = _bf(x - hi.astype(F32))
    r2 = jnp.dot(c_bf, jnp.concatenate([hi, lo], axis=1), preferred_element_type=F32)
    return r2[:, :HEAD] + r2[:, HEAD:]


def _hg_gates(hq, hf, lb):
    sq = _sigmoid(hq)
    sg = _sigmoid(hf)
    fg = lb + (1.0 - lb) * sg
    return sq, hq * sq, sg, fg, 1.0 - fg, jnp.log(fg)


def _hg_block_fwd(st, hq, hf, hi, lb, cstack, p_ref):
    _, q, _, _, k, g = _hg_gates(hq, hf, lb)
    v = hi
    e = _split_dot(cstack, g)
    bc = e[HG_LEVELS * HG_SUB:HG_E_ROWS]
    b_last = jnp.tile(e[HG_E_ROWS:], (HG_SUB // 8, 1))
    a = jnp.zeros((HG_SUB, HG_SUB), F32)
    for lvl in range(HG_LEVELS):
        x = jnp.exp(e[lvl * HG_SUB:(lvl + 1) * HG_SUB])
        a = a + p_ref[pl.ds(lvl * HG_SUB, HG_SUB), :] * lax.dot_general(_bf(q * x), _bf(k * x), NT_DIMS,
                                                                          preferred_element_type=F32)
    a_bf = _bf(a)
    diag = jnp.sum(q * k, axis=1, keepdims=True)
    o = (jnp.dot(a_bf, _bf(v), preferred_element_type=F32) + diag * v
         + lax.dot_general(_bf(q * jnp.exp(bc)), _bf(st), NT_DIMS, preferred_element_type=F32))
    kd = k * jnp.exp(b_last - bc)
    st_out = st * jnp.exp(b_last) + lax.dot_general(_bf(v), _bf(kd), TN_DIMS, preferred_element_type=F32)
    return st_out, o, a_bf


def _hg_block_bwd(st, dst_out, do, hq, hf, hi, lb, a_bf, cstack, cstack_t, p_ref, pt_ref):
    sq, q, sg, fg, k, g = _hg_gates(hq, hf, lb)
    v = hi
    e = _split_dot(cstack, g)
    bc = e[HG_LEVELS * HG_SUB:HG_E_ROWS]
    b_last = jnp.tile(e[HG_E_ROWS:], (HG_SUB // 8, 1))
    eb = jnp.exp(bc)
    qb = q * eb
    er = jnp.exp(b_last - bc)
    kd = k * er
    e_last = jnp.exp(b_last)
    do_bf, v_bf, dst_bf = _bf(do), _bf(v), _bf(dst_out)
    da = lax.dot_general(do_bf, v_bf, NT_DIMS, preferred_element_type=F32)
    dat = lax.dot_general(v_bf, do_bf, NT_DIMS, preferred_element_type=F32)
    d_diag = jnp.sum(do * v, axis=1, keepdims=True)
    dv = (lax.dot_general(a_bf, do_bf, TN_DIMS, preferred_element_type=F32)
          + jnp.sum(q * k, axis=1, keepdims=True) * do
          + lax.dot_general(_bf(kd), dst_bf, NT_DIMS, preferred_element_type=F32))
    dqb = jnp.dot(do_bf, _bf(st), preferred_element_type=F32)
    dst = dst_out * e_last + lax.dot_general(do_bf, _bf(qb), TN_DIMS, preferred_element_type=F32)
    dkd = jnp.dot(v_bf, dst_bf, preferred_element_type=F32)
    dq = dqb * eb + d_diag * k
    dk = dkd * er + d_diag * q
    d_last = (jnp.sum(dst_out * st * e_last, axis=0, keepdims=True)
              + jnp.sum(dkd * kd, axis=0, keepdims=True))
    des = []
    for lvl in range(HG_LEVELS):
        x = jnp.exp(e[lvl * HG_SUB:(lvl + 1) * HG_SUB])
        qh, kh = q * x, k * x
        dm = _bf(p_ref[pl.ds(lvl * HG_SUB, HG_SUB), :] * da)
        dmt = _bf(pt_ref[pl.ds(lvl * HG_SUB, HG_SUB), :] * dat)
        dqh = jnp.dot(dm, _bf(kh), preferred_element_type=F32)
        dkh = jnp.dot(dmt, _bf(qh), preferred_element_type=F32)
        dq = dq + dqh * x
        dk = dk + dkh * x
        des.append(dqh * qh + dkh * kh)
    des.append(dqb * qb - dkd * kd)
    dg = _split_dot(cstack_t, jnp.concatenate(des, axis=0)) + d_last
    dfg = dg / fg - dk
    dhq = dq * (sq * (1.0 + hq * (1.0 - sq)))
    dhf = dfg * (1.0 - lb) * sg * (1.0 - sg)
    return dst, dhq, dhf, dv, jnp.sum(dfg * (1.0 - sg), axis=0, keepdims=True)


def hgrn_fwd(proj_main, lb, consts, bl, lp, d):
    nh = d // HEAD
    rows_blk = _tile(lp, 768, SEQ_BLOCK)
    nb = lp // rows_blk
    spb = rows_blk // HG_SUB
    cstack, _, pstack, _ = consts

    def body(hq_ref, hf_ref, hi_ref, lb_ref, c_ref, p_ref, o_ref, st_ref, a_ref, s_ref):
        j = pl.program_id(2)

        @pl.when(j == 0)
        def _():
            s_ref[...] = jnp.zeros_like(s_ref)

        lbv = lb_ref[...]
        cs = c_ref[...]

        def sub(n, carry):
            r = pl.multiple_of(n * HG_SUB, HG_SUB)
            st = s_ref[...]
            st_ref[0, 0, pl.ds(n, 1)] = st[None]
            st_out, o, a_bf = _hg_block_fwd(st, hq_ref[pl.ds(r, HG_SUB), :].astype(F32),
                                            hf_ref[pl.ds(r, HG_SUB), :].astype(F32),
                                            hi_ref[pl.ds(r, HG_SUB), :].astype(F32), lbv, cs, p_ref)
            s_ref[...] = st_out
            o_ref[pl.ds(r, HG_SUB), :] = o.astype(o_ref.dtype)
            a_ref[0, 0, pl.ds(n, 1)] = a_bf[None]
            return carry

        lax.fori_loop(0, spb, sub, 0, unroll=3 if spb % 3 == 0 else 2)

    def colspec(off):
        return pl.BlockSpec((rows_blk, HEAD), functools.partial(lambda h, b, j, off: (b * nb + j, off + h), off=off))

    whole = lambda arr: pl.BlockSpec(arr.shape, lambda h, b, j: (0, 0))
    return pl.pallas_call(
        body, name="hgrn_fwd", grid=(nh, bl, nb),
        in_specs=[colspec(0), colspec(nh), colspec(2 * nh), pl.BlockSpec((1, HEAD), lambda h, b, j: (0, h)),
                  whole(cstack), whole(pstack)],
        out_specs=[pl.BlockSpec((rows_blk, HEAD), lambda h, b, j: (b * nb + j, h)),
                   pl.BlockSpec((1, 1, spb, HEAD, HEAD), lambda h, b, j: (b, h, j, 0, 0)),
                   pl.BlockSpec((1, 1, spb, HG_SUB, HG_SUB), lambda h, b, j: (b, h, j, 0, 0))],
        out_shape=[jax.ShapeDtypeStruct((bl * lp, d), BF16),
                   jax.ShapeDtypeStruct((bl, nh, lp // HG_SUB, HEAD, HEAD), F32),
                   jax.ShapeDtypeStruct((bl, nh, lp // HG_SUB, HG_SUB, HG_SUB), BF16)],
        scratch_shapes=[pltpu.VMEM((HEAD, HEAD), F32)],
        compiler_params=pltpu.CompilerParams(dimension_semantics=("arbitrary", "arbitrary", "arbitrary")),
    )(proj_main, proj_main, proj_main, lb, cstack, pstack)


def hgrn_bwd(proj_main, lb, consts, states, a_mats, do_scan, bl, lp, d):
    nh = d // HEAD
    rows_blk = _tile(lp, 768, SEQ_BLOCK)
    nb = lp // rows_blk
    spb = rows_blk // HG_SUB
    cstack, cstack_t, pstack, pstack_t = consts

    def body(hq_ref, hf_ref, hi_ref, lb_ref, c_ref, ct_ref, p_ref, pt_ref, st_ref, a_ref, do_ref,
             dq_ref, df_ref, di_ref, dlb_ref, ds_ref):
        b_id, j = pl.program_id(1), pl.program_id(2)
        blk = nb - 1 - j

        @pl.when(j == 0)
        def _():
            ds_ref[...] = jnp.zeros_like(ds_ref)

        @pl.when((j == 0) & (b_id == 0))
        def _():
            dlb_ref[...] = jnp.zeros_like(dlb_ref)

        lbv = lb_ref[...]
        cs = c_ref[...]
        cst = ct_ref[...]

        def sub(i, carry):
            n = spb - 1 - i
            r = pl.multiple_of(n * HG_SUB, HG_SUB)
            dst, dhq, dhf, dhi, dlb = _hg_block_bwd(
                st_ref[0, 0, pl.ds(n, 1)][0], ds_ref[...], do_ref[pl.ds(r, HG_SUB), :].astype(F32),
                hq_ref[pl.ds(r, HG_SUB), :].astype(F32), hf_ref[pl.ds(r, HG_SUB), :].astype(F32),
                hi_ref[pl.ds(r, HG_SUB), :].astype(F32), lbv,
                a_ref[0, 0, pl.ds(n, 1)][0], cs, cst, p_ref, pt_ref)
            ds_ref[...] = dst
            dq_ref[pl.ds(r, HG_SUB), :] = dhq.astype(dq_ref.dtype)
            df_ref[pl.ds(r, HG_SUB), :] = dhf.astype(df_ref.dtype)
            di_ref[pl.ds(r, HG_SUB), :] = dhi.astype(di_ref.dtype)
            dlb_ref[...] += dlb
            return carry

        lax.fori_loop(0, spb, sub, 0, unroll=2)

    def colspec(off):
        return pl.BlockSpec((rows_blk, HEAD),
                            functools.partial(lambda h, b, j, off: (b * nb + nb - 1 - j, off + h), off=off))

    whole = lambda arr: pl.BlockSpec(arr.shape, lambda h, b, j: (0, 0))
    mats = lambda: pl.BlockSpec((1, 1, spb, HEAD, HEAD), lambda h, b, j: (b, h, nb - 1 - j, 0, 0))
    t_rows = bl * lp
    return pl.pallas_call(
        body, name="hgrn_bwd", grid=(nh, bl, nb),
        in_specs=[colspec(0), colspec(nh), colspec(2 * nh), pl.BlockSpec((1, HEAD), lambda h, b, j: (0, h)),
                  whole(cstack), whole(cstack_t), whole(pstack), whole(pstack_t), mats(), mats(), colspec(0)],
        out_specs=[colspec(0), colspec(0), colspec(0), pl.BlockSpec((1, HEAD), lambda h, b, j: (0, h))],
        out_shape=[jax.ShapeDtypeStruct((t_rows, d), BF16)] * 3 + [jax.ShapeDtypeStruct((1, d), F32)],
        scratch_shapes=[pltpu.VMEM((HEAD, HEAD), F32)],
        compiler_params=pltpu.CompilerParams(dimension_semantics=("arbitrary", "arbitrary", "arbitrary")),
    )(proj_main, proj_main, proj_main, lb, cstack, cstack_t, pstack, pstack_t, states, a_mats, do_scan)


def _allowed(row0, col0, nr, nc, transposed=False):
    if transposed:
        col = col0 + lax.broadcasted_iota(jnp.int32, (nc, 1), 0)
        row = row0 + lax.broadcasted_iota(jnp.int32, (1, nr), 1)
    else:
        row = row0 + lax.broadcasted_iota(jnp.int32, (nr, 1), 0)
        col = col0 + lax.broadcasted_iota(jnp.int32, (1, nc), 1)
    return (col <= row) & ((col >= PAD_FRONT) | (row < PAD_FRONT))


def attn_fwd(q_cat, k_cat, v, bl, lp, nm, scale):
    tq = tk = SEQ_BLOCK
    nq = lp // tq

    def body(q_ref, k_ref, v_ref, o_ref, lse_ref, m_ref, l_ref, acc_ref):
        i = pl.program_id(2)
        q = q_ref[...]
        m_ref[...] = jnp.full_like(m_ref, NEG)
        l_ref[...] = jnp.zeros_like(l_ref)
        acc_ref[...] = jnp.zeros_like(acc_ref)

        def kstep(c, carry):
            c0 = pl.multiple_of(c * tk, tk)
            s = lax.dot_general(q, k_ref[pl.ds(c0, tk), :], NT_DIMS, preferred_element_type=F32) * scale
            s = jnp.where(_allowed(i * tq, c * tk, tq, tk), s, NEG)
            m_old = m_ref[...]
            m_new = jnp.maximum(m_old, jnp.max(s, axis=1, keepdims=True))
            alpha = jnp.exp(m_old - m_new)
            p = jnp.exp(s - m_new)
            l_ref[...] = alpha * l_ref[...] + jnp.sum(p, axis=1, keepdims=True)
            acc_ref[...] = alpha * acc_ref[...] + jnp.dot(_bf(p), v_ref[pl.ds(c0, tk), :],
                                                          preferred_element_type=F32)
            m_ref[...] = m_new
            return carry

        lax.fori_loop(0, i + 1, kstep, 0)
        o_ref[...] = (acc_ref[...] / l_ref[...]).astype(o_ref.dtype)
        lse_ref[0, 0] = m_ref[...] + jnp.log(l_ref[...])

    return pl.pallas_call(
        body, name="attn_fwd", grid=(bl, nm, nq),
        in_specs=[pl.BlockSpec((tq, QK_PAD), lambda b, h, i: (b * nq + i, h)),
                  pl.BlockSpec((lp, QK_PAD), lambda b, h, i: (b, h)),
                  pl.BlockSpec((lp, HEAD), lambda b, h, i: (b, h))],
        out_specs=[pl.BlockSpec((tq, HEAD), lambda b, h, i: (b * nq + i, h)),
                   pl.BlockSpec((1, 1, tq, 1), lambda b, h, i: (b, h, i, 0))],
        out_shape=[jax.ShapeDtypeStruct((bl * lp, nm * HEAD), BF16),
                   jax.ShapeDtypeStruct((bl, nm, lp, 1), F32)],
        scratch_shapes=[pltpu.VMEM((tq, 1), F32), pltpu.VMEM((tq, 1), F32), pltpu.VMEM((tq, HEAD), F32)],
        compiler_params=pltpu.CompilerParams(dimension_semantics=("arbitrary", "arbitrary", "arbitrary")),
    )(q_cat, k_cat, v)


def attn_bwd_dq(q_cat, k_cat, v, o, do, lse, bl, lp, nm, scale):
    tq = tk = SEQ_BLOCK
    nq = lp // tq

    def body(q_ref, k_ref, v_ref, o_ref, do_ref, lse_ref, dq_ref, dl_ref, acc_ref):
        i = pl.program_id(2)
        q = q_ref[...]
        do_b = do_ref[...]
        delta = jnp.sum(o_ref[...].astype(F32) * do_b.astype(F32), axis=1, keepdims=True)
        lse_b = lse_ref[0, 0]
        acc_ref[...] = jnp.zeros_like(acc_ref)

        def kstep(c, carry):
            c0 = pl.multiple_of(c * tk, tk)
            ks = k_ref[pl.ds(c0, tk), :]
            s = lax.dot_general(q, ks, NT_DIMS, preferred_element_type=F32) * scale
            p = jnp.where(_allowed(i * tq, c * tk, tq, tk), jnp.exp(s - lse_b), 0.0)
            dp = lax.dot_general(do_b, v_ref[pl.ds(c0, tk), :], NT_DIMS, preferred_element_type=F32)
            ds = p * (dp - delta)
            acc_ref[...] += jnp.dot(_bf(ds), ks, preferred_element_type=F32)
            return carry

        lax.fori_loop(0, i + 1, kstep, 0)
        dq_ref[...] = acc_ref[...] * scale
        dl_ref[0, 0] = delta

    return pl.pallas_call(
        body, name="attn_bwd_dq", grid=(bl, nm, nq),
        in_specs=[pl.BlockSpec((tq, QK_PAD), lambda b, h, i: (b * nq + i, h)),
                  pl.BlockSpec((lp, QK_PAD), lambda b, h, i: (b, h)),
                  pl.BlockSpec((lp, HEAD), lambda b, h, i: (b, h)),
                  pl.BlockSpec((tq, HEAD), lambda b, h, i: (b * nq + i, h)),
                  pl.BlockSpec((tq, HEAD), lambda b, h, i: (b * nq + i, h)),
                  pl.BlockSpec((1, 1, tq, 1), lambda b, h, i: (b, h, i, 0))],
        out_specs=[pl.BlockSpec((tq, QK_PAD), lambda b, h, i: (b * nq + i, h)),
                   pl.BlockSpec((1, 1, tq, 1), lambda b, h, i: (b, h, i, 0))],
        out_shape=[jax.ShapeDtypeStruct((bl * lp, nm * QK_PAD), F32),
                   jax.ShapeDtypeStruct((bl, nm, lp, 1), F32)],
        scratch_shapes=[pltpu.VMEM((tq, QK_PAD), F32)],
        compiler_params=pltpu.CompilerParams(dimension_semantics=("arbitrary", "arbitrary", "arbitrary")),
    )(q_cat, k_cat, v, o, do, lse)


def attn_bwd_dkv(q_cat, k_cat, v, do, lse_row, delta_row, bl, lp, nm, scale):
    tq = tk = SEQ_BLOCK
    nq = lp // tq

    def body(q_ref, k_ref, v_ref, do_ref, lse_ref, dl_ref, dk_ref, dv_ref):
        i = pl.program_id(2)
        kt = k_ref[...]
        vt = v_ref[...]
        dk_ref[...] = jnp.zeros_like(dk_ref)
        dv_ref[...] = jnp.zeros_like(dv_ref)

        def qstep(c, carry):
            c0 = pl.multiple_of(c * tq, tq)
            qs = q_ref[pl.ds(c0, tq), :]
            dos = do_ref[pl.ds(c0, tq), :]
            st = lax.dot_general(kt, qs, NT_DIMS, preferred_element_type=F32) * scale
            pt = jnp.where(_allowed(c * tq, i * tk, tq, tk, transposed=True),
                           jnp.exp(st - lse_ref[0, 0, pl.ds(c, 1)][0]), 0.0)
            dv_ref[...] += jnp.dot(_bf(pt), dos, preferred_element_type=F32)
            dpt = lax.dot_general(vt, dos, NT_DIMS, preferred_element_type=F32)
            dst = pt * (dpt - dl_ref[0, 0, pl.ds(c, 1)][0])
            dk_ref[...] += jnp.dot(_bf(dst), qs, preferred_element_type=F32)
            return carry

        lax.fori_loop(i, nq, qstep, 0)
        dk_ref[...] = dk_ref[...] * scale

    return pl.pallas_call(
        body, name="attn_bwd_dkv", grid=(bl, nm, nq),
        in_specs=[pl.BlockSpec((lp, QK_PAD), lambda b, h, i: (b, h)),
                  pl.BlockSpec((tk, QK_PAD), lambda b, h, i: (b * nq + i, h)),
                  pl.BlockSpec((tk, HEAD), lambda b, h, i: (b * nq + i, h)),
                  pl.BlockSpec((lp, HEAD), lambda b, h, i: (b, h)),
                  pl.BlockSpec((1, 1, nq, 1, tq), lambda b, h, i: (b, h, 0, 0, 0)),
                  pl.BlockSpec((1, 1, nq, 1, tq), lambda b, h, i: (b, h, 0, 0, 0))],
        out_specs=[pl.BlockSpec((tk, QK_PAD), lambda b, h, i: (b * nq + i, h)),
                   pl.BlockSpec((tk, HEAD), lambda b, h, i: (b * nq + i, h))],
        out_shape=[jax.ShapeDtypeStruct((bl * lp, nm * QK_PAD), F32),
                   jax.ShapeDtypeStruct((bl * lp, nm * HEAD), F32)],
        compiler_params=pltpu.CompilerParams(dimension_semantics=("arbitrary", "arbitrary", "arbitrary")),
    )(q_cat, k_cat, v, do, lse_row, delta_row)


def _key_query_mask(key0, qry0, nk, nq_, causal):
    key = key0 + lax.broadcasted_iota(jnp.int32, (nk, 1), 0)
    if not causal:
        return key >= PAD_FRONT
    qry = qry0 + lax.broadcasted_iota(jnp.int32, (1, nq_), 1)
    return (key <= qry) & (key >= PAD_FRONT)


def _attn_tile(lp):
    return _tile(lp, ATTN_TILE_MAX, SEQ_BLOCK)


def attn_fwd_t(q_cat, k_cat, v_t, bl, lp, nm, scale):
    tq = tk = _attn_tile(lp)
    nq = lp // tq
    hp = ATTN_HEADS_PER_STEP
    assert nm % hp == 0

    def body(q_ref, k_ref, vt_ref, o_ref, lse_ref, m_ref, l_ref, acc_ref):
        i = pl.program_id(2)
        m_ref[...] = jnp.full_like(m_ref, NEG)
        l_ref[...] = jnp.zeros_like(l_ref)
        acc_ref[...] = jnp.zeros_like(acc_ref)

        def step(c, mask):
            c0 = pl.multiple_of(c * tk, tk)
            for hh in range(hp):
                cols = pl.ds(hh * QK_PAD, QK_PAD)
                st = lax.dot_general(k_ref[pl.ds(c0, tk), cols], q_ref[:, cols], NT_DIMS,
                                     preferred_element_type=F32) * scale
                if mask is not None:
                    st = jnp.where(_key_query_mask(c * tk, i * tq, tk, tq, mask == "causal"), st, NEG)
                m_old = m_ref[hh]
                m_new = jnp.maximum(m_old, jnp.max(st, axis=0, keepdims=True))
                alpha = jnp.exp(m_old - m_new)
                pt = jnp.exp(st - m_new)
                l_ref[hh] = alpha * l_ref[hh] + jnp.sum(pt, axis=0, keepdims=True)
                acc_ref[hh] = alpha * acc_ref[hh] + jnp.dot(vt_ref[0, hh, pl.ds(c, 1)][0], _bf(pt),
                                                            preferred_element_type=F32)
                m_ref[hh] = m_new

        def mid(c, carry):
            step(c, None)
            return carry

        @pl.when(i == 0)
        def _():
            step(0, "causal")

        @pl.when(i > 0)
        def _():
            step(0, "pad")
            lax.fori_loop(1, i, mid, 0)
            step(i, "causal")

        for hh in range(hp):
            o_ref[:, pl.ds(hh * HEAD, HEAD)] = jnp.transpose(acc_ref[hh] / l_ref[hh]).astype(o_ref.dtype)
            lse_ref[0, hh, 0] = m_ref[hh] + jnp.log(l_ref[hh])

    return pl.pallas_call(
        body, name="attn_fwd", grid=(bl, nm // hp, nq),
        in_specs=[pl.BlockSpec((tq, hp * QK_PAD), lambda b, h, i: (b * nq + i, h)),
                  pl.BlockSpec((lp, hp * QK_PAD), lambda b, h, i: (b, h)),
                  pl.BlockSpec((1, hp, nq, HEAD, tk), lambda b, h, i: (b, h, 0, 0, 0))],
        out_specs=[pl.BlockSpec((tq, hp * HEAD), lambda b, h, i: (b * nq + i, h)),
                   pl.BlockSpec((1, hp, 1, 1, tq), lambda b, h, i: (b, h, i, 0, 0))],
        out_shape=[jax.ShapeDtypeStruct((bl * lp, nm * HEAD), BF16),
                   jax.ShapeDtypeStruct((bl, nm, nq, 1, tq), F32)],
        scratch_shapes=[pltpu.VMEM((hp, 1, tq), F32), pltpu.VMEM((hp, 1, tq), F32), pltpu.VMEM((hp, HEAD, tq), F32)],
        compiler_params=pltpu.CompilerParams(dimension_semantics=("arbitrary", "arbitrary", "arbitrary")),
    )(q_cat, k_cat, v_t)


def attn_bwd_t(q_cat, k_cat, k_t, v, o, do, lse, bl, lp, nm, scale):
    tq = tk = _attn_tile(lp)
    nq = lp // tq
    hp = ATTN_HEADS_PER_STEP
    assert nm % hp == 0

    def body(q_ref, k_ref, kt_ref, v_ref, o_ref, do_ref, lse_ref, dq_ref, dk_ref, dv_ref, dqt_ref, dka_ref, dva_ref):
        i = pl.program_id(2)

        @pl.when(i == 0)
        def _():
            dqt_ref[...] = jnp.zeros_like(dqt_ref)

        dka_ref[...] = jnp.zeros_like(dka_ref)
        dva_ref[...] = jnp.zeros_like(dva_ref)
        ones8 = jnp.ones((8, HEAD), BF16)

        def step(c, mask):
            c0 = pl.multiple_of(c * tq, tq)
            for hh in range(hp):
                qcols, vcols = pl.ds(hh * QK_PAD, QK_PAD), pl.ds(hh * HEAD, HEAD)
                qs = q_ref[pl.ds(c0, tq), qcols]
                dos = do_ref[pl.ds(c0, tq), vcols]
                prod = dos.astype(F32) * o_ref[pl.ds(c0, tq), vcols].astype(F32)
                hi = _bf(prod)
                lo = _bf(prod - hi.astype(F32))
                delta8 = (lax.dot_general(ones8, hi, NT_DIMS, preferred_element_type=F32)
                          + lax.dot_general(ones8, lo, NT_DIMS, preferred_element_type=F32))
                st = lax.dot_general(k_ref[:, qcols], qs, NT_DIMS, preferred_element_type=F32) * scale
                pt = jnp.exp(st - lse_ref[0, hh, pl.ds(c, 1)][0])
                if mask is not None:
                    pt = jnp.where(_key_query_mask(i * tk, c * tq, tk, tq, mask == "causal"), pt, 0.0)
                dva_ref[hh] += jnp.dot(_bf(pt), dos, preferred_element_type=F32)
                dpt = lax.dot_general(v_ref[:, vcols], dos, NT_DIMS, preferred_element_type=F32)
                dst = _bf(pt * (dpt - jnp.tile(delta8, (tk // 8, 1))))
                dka_ref[hh] += jnp.dot(dst, qs, preferred_element_type=F32)
                dqt_ref[hh, pl.ds(c, 1)] += jnp.dot(kt_ref[0, hh, 0], dst, preferred_element_type=F32)[None]

        step(i, "causal")

        def rest_masked(c, carry):
            step(c, "pad")
            return carry

        def rest(c, carry):
            step(c, None)
            return carry

        @pl.when(i == 0)
        def _():
            lax.fori_loop(1, nq, rest_masked, 0)

        @pl.when(i > 0)
        def _():
            lax.fori_loop(i + 1, nq, rest, 0)

        for hh in range(hp):
            dk_ref[:, pl.ds(hh * QK_PAD, QK_PAD)] = (dka_ref[hh] * scale).astype(dk_ref.dtype)
            dv_ref[:, pl.ds(hh * HEAD, HEAD)] = dva_ref[hh].astype(dv_ref.dtype)

        @pl.when(i == nq - 1)
        def _():
            for hh in range(hp):
                for c in range(nq):
                    dq_ref[pl.ds(c * tq, tq), pl.ds(hh * QK_PAD, QK_PAD)] = (
                        jnp.transpose(dqt_ref[hh, c]) * scale).astype(dq_ref.dtype)

    return pl.pallas_call(
        body, name="attn_bwd", grid=(bl, nm // hp, nq),
        in_specs=[pl.BlockSpec((lp, hp * QK_PAD), lambda b, h, i: (b, h)),
                  pl.BlockSpec((tk, hp * QK_PAD), lambda b, h, i: (b * nq + i, h)),
                  pl.BlockSpec((1, hp, 1, QK_PAD, tk), lambda b, h, i: (b, h, i, 0, 0)),
                  pl.BlockSpec((tk, hp * HEAD), lambda b, h, i: (b * nq + i, h)),
                  pl.BlockSpec((lp, hp * HEAD), lambda b, h, i: (b, h)),
                  pl.BlockSpec((lp, hp * HEAD), lambda b, h, i: (b, h)),
                  pl.BlockSpec((1, hp, nq, 1, tq), lambda b, h, i: (b, h, 0, 0, 0))],
        out_specs=[pl.BlockSpec((lp, hp * QK_PAD), lambda b, h, i: (b, h)),
                   pl.BlockSpec((tk, hp * QK_PAD), lambda b, h, i: (b * nq + i, h)),
                   pl.BlockSpec((tk, hp * HEAD), lambda b, h, i: (b * nq + i, h))],
        out_shape=[jax.ShapeDtypeStruct((bl * lp, nm * QK_PAD), BF16),
                   jax.ShapeDtypeStruct((bl * lp, nm * QK_PAD), BF16),
                   jax.ShapeDtypeStruct((bl * lp, nm * HEAD), BF16)],
        scratch_shapes=[pltpu.VMEM((hp, nq, QK_PAD, tq), F32), pltpu.VMEM((hp, tk, QK_PAD), F32),
                        pltpu.VMEM((hp, tk, HEAD), F32)],
        compiler_params=pltpu.CompilerParams(dimension_semantics=("arbitrary", "arbitrary", "arbitrary")),
    )(q_cat, k_cat, k_t, v, o, do, lse)


def _place():
    return lax.axis_index("x"), lax.axis_index("y"), lax.axis_index("c")


def gather_shards(packed):
    hbm = pl.BlockSpec(memory_space=pl.ANY)

    def body(src_ref, out_ref, send_sems, recv_sems, local_sem):
        x, y, c = _place()
        me = 2 * x + y
        chips = [(1 - x, y), (x, 1 - y), (1 - x, 1 - y)]
        local = pltpu.make_async_copy(src_ref, out_ref.at[me], local_sem)
        local.start()
        sends = []
        for k, (px, py) in enumerate(chips):
            cp = pltpu.make_async_remote_copy(src_ref=src_ref, dst_ref=out_ref.at[me], send_sem=send_sems.at[k],
                                              recv_sem=recv_sems.at[k], device_id=(px, py, c), device_id_type=MESH)
            cp.start()
            sends.append(cp)
        for k, (px, py) in enumerate(chips):
            pltpu.make_async_remote_copy(src_ref=src_ref, dst_ref=out_ref.at[2 * px + py], send_sem=send_sems.at[k],
                                         recv_sem=recv_sems.at[k], device_id=(px, py, c),
                                         device_id_type=MESH).wait_recv()
        for cp in sends:
            cp.wait_send()
        local.wait()

    return pl.pallas_call(
        body, name="gather_shards", in_specs=[hbm], out_specs=hbm,
        out_shape=jax.ShapeDtypeStruct((4,) + packed.shape, packed.dtype),
        scratch_shapes=[pltpu.SemaphoreType.DMA((3,)), pltpu.SemaphoreType.DMA((3,)), pltpu.SemaphoreType.DMA],
    )(packed)


def gather_small(small):
    hbm = pl.BlockSpec(memory_space=pl.ANY)

    def body(small_ref, all_ref, send_sems, recv_sems, local_sem):
        x, y, c = _place()
        me = 4 * x + 2 * y + c
        local = pltpu.make_async_copy(small_ref, all_ref.at[me], local_sem)
        local.start()
        others = [(x ^ ((r >> 2) & 1), y ^ ((r >> 1) & 1), c ^ (r & 1)) for r in range(1, 8)]
        sends = []
        for r, peer in enumerate(others):
            cp = pltpu.make_async_remote_copy(src_ref=small_ref, dst_ref=all_ref.at[me], send_sem=send_sems.at[r],
                                              recv_sem=recv_sems.at[r], device_id=peer, device_id_type=MESH)
            cp.start()
            sends.append(cp)
        for r, (px, py, pc) in enumerate(others):
            pltpu.make_async_remote_copy(src_ref=small_ref, dst_ref=all_ref.at[4 * px + 2 * py + pc],
                                         send_sem=send_sems.at[r], recv_sem=recv_sems.at[r],
                                         device_id=(px, py, pc), device_id_type=MESH).wait_recv()
        for cp in sends:
            cp.wait_send()
        local.wait()

    return pl.pallas_call(
        body, name="gather_small", in_specs=[hbm], out_specs=hbm,
        out_shape=jax.ShapeDtypeStruct((8,) + small.shape, small.dtype),
        scratch_shapes=[pltpu.SemaphoreType.DMA((7,)), pltpu.SemaphoreType.DMA((7,)), pltpu.SemaphoreType.DMA],
    )(small)


def swap_with_sibling(name, parts):
    n = len(parts)
    hbm = pl.BlockSpec(memory_space=pl.ANY)

    def body(*refs):
        x, y, c = _place()
        cps = [pltpu.make_async_remote_copy(src_ref=refs[j], dst_ref=refs[n + j], send_sem=refs[2 * n].at[j],
                                            recv_sem=refs[2 * n + 1].at[j], device_id=(x, y, 1 - c),
                                            device_id_type=MESH) for j in range(n)]
        for cp in cps:
            cp.start()
        for cp in cps:
            cp.wait()

    return pl.pallas_call(
        body, name=name, in_specs=[hbm] * n, out_specs=[hbm] * n,
        out_shape=[jax.ShapeDtypeStruct(p.shape, p.dtype) for p in parts],
        scratch_shapes=[pltpu.SemaphoreType.DMA((n,)), pltpu.SemaphoreType.DMA((n,))],
    )(*parts)


def _chips3():
    x, y, c = _place()
    return [(1 - x, y, c), (x, 1 - y, c), (1 - x, 1 - y, c)]


def _push_copies(src_refs, land_refs, send_sems, recv_sems, per_chip):
    cps = []
    for j, (src_ref, land_ref) in enumerate(zip(src_refs, land_refs)):
        for k, (px, py, pc) in enumerate(_chips3()):
            part = src_ref.at[2 * px + py] if per_chip else src_ref
            cps.append(pltpu.make_async_remote_copy(
                src_ref=part, dst_ref=land_ref.at[k], send_sem=send_sems.at[3 * j + k],
                recv_sem=recv_sems.at[3 * j + k], device_id=(px, py, pc), device_id_type=MESH))
    return cps


def push_start(name, srcs, per_chip):
    n = len(srcs)
    hbm = pl.BlockSpec(memory_space=pltpu.HBM)
    sem = pl.BlockSpec(memory_space=pltpu.SEMAPHORE)
    lands = [lax.empty((3,) + s.shape[-2:], s.dtype) for s in srcs]

    def body(*refs):
        src_refs, land_refs = refs[:n], refs[n:2 * n]
        send_sems, recv_sems = refs[2 * n], refs[2 * n + 1]
        for cp in _push_copies(src_refs, land_refs, send_sems, recv_sems, per_chip):
            cp.start()
        refs[-1][...] = jnp.zeros_like(refs[-1])

    outs = pl.pallas_call(
        body, name=name,
        out_shape=(pltpu.SemaphoreType.DMA((3 * n,)), pltpu.SemaphoreType.DMA((3 * n,)),
                   *[pltpu.HBM(a.shape, a.dtype) for a in list(srcs) + lands], jax.ShapeDtypeStruct((8, HEAD), F32)),
        in_specs=(hbm,) * (2 * n),
        out_specs=(sem, sem) + (hbm,) * (2 * n) + (pl.BlockSpec(memory_space=pltpu.VMEM),),
        input_output_aliases={j: 2 + j for j in range(2 * n)},
        compiler_params=pltpu.CompilerParams(has_side_effects=pltpu.SideEffectType.DATAFLOW_SIDE_EFFECTING),
    )(*[pltpu.with_memory_space_constraint(a, pltpu.HBM) for a in list(srcs) + lands])
    return tuple(outs[:-1]), outs[-1]


def push_wait(name, handle, after, per_chip):
    send_sems, recv_sems = handle[0], handle[1]
    thru = handle[2:]
    n = len(thru) // 2
    hbm = pl.BlockSpec(memory_space=pltpu.HBM)
    sem = pl.BlockSpec(memory_space=pltpu.SEMAPHORE)

    def body(*refs):
        src_refs, land_refs = refs[:n], refs[n:2 * n]
        for cp in _push_copies(src_refs, land_refs, refs[2 * n], refs[2 * n + 1], per_chip):
            cp.wait_send()
            cp.wait_recv()

    outs = pl.pallas_call(
        body, name=name,
        out_shape=tuple(pltpu.HBM(a.shape, a.dtype) for a in thru),
        in_specs=(hbm,) * (2 * n) + (sem, sem, pl.BlockSpec(memory_space=pl.ANY)), out_specs=(hbm,) * (2 * n),
        input_output_aliases={j: j for j in range(2 * n)},
        compiler_params=pltpu.CompilerParams(has_side_effects=pltpu.SideEffectType.DATAFLOW_SIDE_EFFECTING),
    )(*thru, send_sems, recv_sems, after)
    return outs[:n], outs[n:]


def by_chip(own, landed, my_chip):
    by_rel = jnp.stack([own, landed[1], landed[0], landed[2]])
    return [lax.dynamic_index_in_dim(by_rel, jnp.bitwise_xor(s, my_chip), axis=0, keepdims=False) for s in range(4)]


def adamw(name, w, g_parts, m, v):
    r, c = w.shape
    tr = r if r * c <= 65536 else _tile(r, 128, 8)
    ng = len(g_parts)

    def body(*refs):
        w_ref, m_ref, v_ref = refs[0], refs[1 + ng], refs[2 + ng]
        g_ref, d_ref, nm_ref, nv_ref = refs[3 + ng:]
        gv = refs[1][...]
        for k in range(1, ng):
            gv = gv + refs[1 + k][...]
        m_new = ADAM_B1 * m_ref[...] + (1.0 - ADAM_B1) * gv
        v_new = ADAM_B2 * v_ref[...] + (1.0 - ADAM_B2) * (gv * gv)
        m_hat = m_new / (1.0 - ADAM_B1 ** ADAM_STEP)
        v_hat = v_new / (1.0 - ADAM_B2 ** ADAM_STEP)
        g_ref[...] = gv
        d_ref[...] = -ADAM_LR * (m_hat / (jnp.sqrt(v_hat) + ADAM_EPS) + ADAM_WD * w_ref[...])
        nm_ref[...] = m_new
        nv_ref[...] = v_new

    spec = pl.BlockSpec((tr, c), lambda i: (i, 0))
    return pl.pallas_call(
        body, name=name, grid=(r // tr,), in_specs=[spec] * (3 + ng), out_specs=[spec] * 4,
        out_shape=[jax.ShapeDtypeStruct((r, c), F32)] * 4,
        compiler_params=pltpu.CompilerParams(dimension_semantics=("arbitrary",)),
    )(w, *g_parts, m, v)


def split_full(name, full, s):
    if name in COL_SHARDED:
        c = full.shape[1] // 4
        return full[:, s * c:(s + 1) * c]
    r = full.shape[0] // 4
    return full[s * r:(s + 1) * r]


def join_shards(name, shards):
    return jnp.concatenate(shards, axis=1 if name in COL_SHARDED else 0)


def kernel(x, meta_tokens, w_in, b_gate, lb_logits, hg_norm_g, w_hg_o, q_a_norm_g, w_q_b, kv_a_norm_g, w_kv_b, w_mla_o, w_out, mix_pre_g, mix_post_g, ffn_pre_g, ffn_post_g, w_ffn_in, w_ffn_out, loss_target, m_meta_tokens, m_w_in, m_b_gate, m_lb_logits, m_hg_norm_g, m_w_hg_o, m_q_a_norm_g, m_w_q_b, m_kv_a_norm_g, m_w_kv_b, m_w_mla_o, m_w_out, m_mix_pre_g, m_mix_post_g, m_ffn_pre_g, m_ffn_post_g, m_w_ffn_in, m_w_ffn_out, v_meta_tokens, v_w_in, v_b_gate, v_lb_logits, v_hg_norm_g, v_w_hg_o, v_q_a_norm_g, v_w_q_b, v_kv_a_norm_g, v_w_kv_b, v_w_mla_o, v_w_out, v_mix_pre_g, v_mix_post_g, v_ffn_pre_g, v_ffn_post_g, v_w_ffn_in, v_w_ffn_out):
    wts = dict(meta_tokens=meta_tokens, w_in=w_in[0], b_gate=b_gate, lb_logits=lb_logits, hg_norm_g=hg_norm_g,
               w_hg_o=w_hg_o[0], q_a_norm_g=q_a_norm_g, w_q_b=w_q_b[0], kv_a_norm_g=kv_a_norm_g, w_kv_b=w_kv_b[0],
               w_mla_o=w_mla_o[0], w_out=w_out[0], mix_pre_g=mix_pre_g, mix_post_g=mix_post_g, ffn_pre_g=ffn_pre_g,
               ffn_post_g=ffn_post_g, w_ffn_in=w_ffn_in[0], w_ffn_out=w_ffn_out[0])
    mom_m = dict(meta_tokens=m_meta_tokens, w_in=m_w_in[0], b_gate=m_b_gate, lb_logits=m_lb_logits,
                 hg_norm_g=m_hg_norm_g, w_hg_o=m_w_hg_o[0], q_a_norm_g=m_q_a_norm_g, w_q_b=m_w_q_b[0],
                 kv_a_norm_g=m_kv_a_norm_g, w_kv_b=m_w_kv_b[0], w_mla_o=m_w_mla_o[0], w_out=m_w_out[0],
                 mix_pre_g=m_mix_pre_g, mix_post_g=m_mix_post_g, ffn_pre_g=m_ffn_pre_g, ffn_post_g=m_ffn_post_g,
                 w_ffn_in=m_w_ffn_in[0], w_ffn_out=m_w_ffn_out[0])
    mom_v = dict(meta_tokens=v_meta_tokens, w_in=v_w_in[0], b_gate=v_b_gate, lb_logits=v_lb_logits,
                 hg_norm_g=v_hg_norm_g, w_hg_o=v_w_hg_o[0], q_a_norm_g=v_q_a_norm_g, w_q_b=v_w_q_b[0],
                 kv_a_norm_g=v_kv_a_norm_g, w_kv_b=v_w_kv_b[0], w_mla_o=v_w_mla_o[0], w_out=v_w_out[0],
                 mix_pre_g=v_mix_pre_g, mix_post_g=v_mix_post_g, ffn_pre_g=v_ffn_pre_g, ffn_post_g=v_ffn_post_g,
                 w_ffn_in=v_w_ffn_in[0], w_ffn_out=v_w_ffn_out[0])

    bl, seq, d = x.shape
    lp = PAD_FRONT + N_META + seq
    t_rows = bl * lp
    nh = d // HEAD
    ql, kvl = wts["w_q_b"].shape[0], wts["w_kv_b"].shape[0]
    nm = (4 * wts["w_mla_o"].shape[0]) // HEAD
    ffn = 4 * wts["w_ffn_out"].shape[0]
    mla_w = ql + kvl + HEAD
    assert ql == kvl and ql % HEAD == 0 and seq % SEQ_BLOCK == 0 and d % HEAD == 0
    scale = (HEAD + ROPE) ** -0.5
    my_chip = 2 * lax.axis_index("x") + lax.axis_index("y")

    mcols = meta_tokens.shape[1]
    meta_all = gather_shards(meta_tokens)
    meta_full = jnp.concatenate([meta_all[s] for s in range(4)], axis=1)

    def start_gather(name, names, order_after):
        srcs = [_bf(wts[n]) for n in names]
        if order_after is not None:
            srcs[0] = srcs[0] + order_after[0, 0].astype(BF16)
        return push_start(name, srcs, per_chip=False)

    def finish_gather(name, names, started, after):
        owns, landed = push_wait(name, started[0], after, per_chip=False)
        return {n: join_shards(n, by_chip(own, land, my_chip)) for n, own, land in zip(names, owns, landed)}

    rest_names = tuple(n for n in BIG if n != "w_in")
    gather_1 = start_gather("gather_w_in_start", ("w_in",), meta_all[0, :1, :1] * 0.0)
    gather_2 = start_gather("gather_rest_start", rest_names, gather_1[1])

    h0 = jnp.concatenate([jnp.zeros((bl, PAD_FRONT, d), F32), jnp.broadcast_to(meta_full[None], (bl, N_META, d)), x],
                         axis=1).reshape(t_rows, d)
    tiles_seq, tiles_real = lp // SEQ_BLOCK, seq // SEQ_BLOCK
    assert PAD_FRONT + N_META == SEQ_BLOCK

    def real_block(i):
        return (i // tiles_seq) * tiles_real + jnp.maximum(i % tiles_seq - 1, 0)

    meta_rows = jnp.broadcast_to(((jnp.arange(lp) >= PAD_FRONT) & (jnp.arange(lp) < PAD_FRONT + N_META)
                                  ).astype(F32)[:, None], (lp, HEAD))
    pos = (jnp.arange(lp, dtype=jnp.int32) - PAD_FRONT).astype(F32)
    inv_freq = 1.0 / (ROPE_THETA ** (jnp.arange(0, ROPE, 2, dtype=F32) / ROPE))
    ang = pos[:, None] * inv_freq[None, :]
    zeros32 = jnp.zeros((lp, ROPE_HALF), F32)
    zeros64 = jnp.zeros((lp, HEAD - ROPE), F32)
    t_cos = jnp.concatenate([jnp.cos(ang), jnp.cos(ang), zeros64], axis=1)
    t_up = jnp.concatenate([zeros32, jnp.sin(ang), zeros64], axis=1)
    t_dn = jnp.concatenate([-jnp.sin(ang), zeros32, zeros64], axis=1)
    real = jnp.broadcast_to((jnp.arange(lp) >= PAD_FRONT + N_META).astype(F32)[:, None], (lp, d))
    lb_soft = jax.nn.softmax(lb_logits.astype(F32), axis=0)
    lb = lb_soft[0:1]

    (u1,) = rowwise("norm_mix_pre", lambda h, g: _rms(h, g), [(h0, d, 0)], [], [mix_pre_g + gather_2[1][0, 0]],
                    [(d, BF16)])
    full = finish_gather("gather_w_in_wait", ("w_in",), gather_1, u1)
    w_main = jnp.concatenate([full["w_in"][:, :4 * d], full["w_in"][:, -2 * d:]], axis=1)
    w_mla = jnp.pad(full["w_in"][:, 4 * d:4 * d + ql + kvl + ROPE], ((0, 0), (0, HEAD - ROPE)))
    proj_main = matmul("proj_main", u1, w_main, "nn", out_dtype=BF16)
    proj_mla = matmul("proj_mla", u1, w_mla, "nn", out_dtype=BF16)
    hg_consts = _hg_constants()
    o_scan, states, a_mats = hgrn_fwd(proj_main, lb, hg_consts, bl, lp, d)

    def hg_out_fn(o, hg, g):
        return jnp.concatenate([_rms(o[:, h * HEAD:(h + 1) * HEAD], g) for h in range(nh)], axis=1) * _silu(hg)

    (o_hg,) = rowwise("hgrn_out", hg_out_fn, [(o_scan, d, 0), (proj_main, d, 3)], [], [hg_norm_g], [(d, BF16)])
    full.update(finish_gather("gather_rest_wait", rest_names, gather_2, o_hg))
    w_qb = jnp.pad(full["w_q_b"].reshape(ql, nm, HEAD + ROPE), ((0, 0), (0, 0), (0, QK_PAD - HEAD - ROPE))
                   ).reshape(ql, nm * QK_PAD)
    w_kvb = full["w_kv_b"]
    y_a = matmul("y_a", o_hg, _bf(full["w_hg_o"]), "nn", out_dtype=BF16)

    qn, kvn = rowwise("mla_norms", lambda cq, ckv, gq, gk: (_rms(cq, gq), _rms(ckv, gk)),
                      [(proj_mla, ql, 0), (proj_mla, kvl, 1)], [], [q_a_norm_g, kv_a_norm_g],
                      [(ql, BF16), (kvl, BF16)])
    q_full = matmul("q_up", qn, w_qb, "nn", out_dtype=BF16)
    kv_full = matmul("kv_up", kvn, w_kvb, "nn", out_dtype=BF16)

    def mla_prep_fn(qf, kvf, kpe, cos, s_up, s_dn):
        kpe_r = _rope(kpe, cos, s_up, s_dn)
        qs, ks, vs = [], [], []
        for h in range(nm):
            qs += [qf[:, h * QK_PAD:h * QK_PAD + HEAD], _rope(qf[:, h * QK_PAD + HEAD:(h + 1) * QK_PAD], cos, s_up, s_dn)]
            ks += [kvf[:, h * QK_PAD:h * QK_PAD + HEAD], kpe_r]
            vs += [kvf[:, h * QK_PAD + HEAD:(h + 1) * QK_PAD]]
        return jnp.concatenate(qs, axis=1), jnp.concatenate(ks, axis=1), jnp.concatenate(vs, axis=1)

    kpe_blk = (ql + kvl) // HEAD
    q_cat, k_cat, v_att = rowwise("mla_prep", mla_prep_fn,
                                  [(q_full, nm * QK_PAD, 0), (kv_full, nm * QK_PAD, 0), (proj_mla, HEAD, kpe_blk)],
                                  [t_cos, t_up, t_dn], [], [(nm * QK_PAD, BF16), (nm * QK_PAD, BF16), (nm * HEAD, BF16)])
    at = _attn_tile(lp)
    v_t = v_att.reshape(bl, lp // at, at, nm, HEAD).transpose(0, 3, 1, 4, 2)
    k_t = k_cat.reshape(bl, lp // at, at, nm, QK_PAD).transpose(0, 3, 1, 4, 2)
    o_mla, lse = attn_fwd_t(q_cat, k_cat, v_t, bl, lp, nm, scale)
    y_b = matmul("y_b", o_mla, _bf(full["w_mla_o"]), "nn", out_dtype=BF16)

    def gate_fn(ya, yb, ga, gb, bias):
        return _sigmoid(ga + bias[:, :d]) * ya + _sigmoid(gb + bias[:, d:]) * yb

    (z,) = rowwise("gate_mix", gate_fn, [(y_a, d, 0), (y_b, d, 0), (proj_main, d, 4), (proj_main, d, 5)], [],
                   [b_gate], [(d, BF16)])
    mixed = matmul("mixed", z, _bf(full["w_out"]), "nn", out_dtype=BF16)

    def mid_fn(h, mx, g_post, g_pre):
        h1 = h + _rms(mx, g_post)
        return h1, _rms(h1, g_pre)

    h1, u2 = rowwise("norm_mid", mid_fn, [(h0, d, 0), (mixed, d, 0)], [], [mix_post_g, ffn_pre_g],
                     [(d, F32), (d, BF16)])
    gu = matmul("ffn_in", u2, _bf(full["w_ffn_in"]), "nn", out_dtype=BF16)
    (act,) = rowwise("swiglu", lambda gt, up: _silu(gt) * up, [(gu, ffn, 0), (gu, ffn, 1)], [], [], [(ffn, BF16)])
    f_out = matmul("ffn_out", act, _bf(full["w_ffn_out"]), "nn", out_dtype=BF16)

    def loss_fn(h1v, fv, tg, realv, g_post):
        h2 = h1v + _rms(fv, g_post)
        diff = (h2 - tg) * realv
        part = jnp.broadcast_to(0.5 * jnp.sum(diff * diff, keepdims=True) / d, (1, HEAD))
        dy = diff / d
        df, dg = _rms_bwd(fv, g_post, dy)
        return dy, df, part, dg

    dy, df, loss_part, g_ffn_post = rowwise(
        "loss_head", loss_fn,
        [(h1, d, 0), (f_out, d, 0), (loss_target.reshape(bl * seq, d), d, 0, real_block)], [real],
        [ffn_post_g], [(d, BF16), (d, BF16)], [(1, HEAD), (1, d)])
    grads = {}
    d_act = matmul("d_act", df, _bf(full["w_ffn_out"]), "nt", out_dtype=BF16)
    grads["w_ffn_out"] = matmul("gw_ffn_out", act, df, "tn")

    def swiglu_bwd_fn(gt, up, da):
        return jnp.concatenate([da * up * _silu_grad(gt), da * _silu(gt)], axis=1)

    (dgu,) = rowwise("swiglu_bwd", swiglu_bwd_fn, [(gu, ffn, 0), (gu, ffn, 1), (d_act, ffn, 0)], [], [],
                     [(2 * ffn, BF16)])
    du2 = matmul("d_u2", dgu, _bf(full["w_ffn_in"]), "nt", out_dtype=BF16)
    grads["w_ffn_in"] = matmul("gw_ffn_in", u2, dgu, "tn")

    def mid_bwd_fn(dyv, h1v, du2v, mx, g_pre, g_post):
        dx, dg_pre = _rms_bwd(h1v, g_pre, du2v)
        dh1 = dyv + dx
        dmx, dg_post = _rms_bwd(mx, g_post, dh1)
        return dh1, dmx, dg_pre, dg_post

    dh1, dmixed, g_ffn_pre, g_mix_post = rowwise("norm_mid_bwd", mid_bwd_fn,
                                                 [(dy, d, 0), (h1, d, 0), (du2, d, 0), (mixed, d, 0)], [],
                                                 [ffn_pre_g, mix_post_g], [(d, BF16), (d, BF16)], [(1, d), (1, d)])
    dz = matmul("d_z", dmixed, _bf(full["w_out"]), "nt", out_dtype=BF16)
    grads["w_out"] = matmul("gw_out", z, dmixed, "tn")

    def gate_bwd_fn(dzv, ya, yb, ga, gb, bias):
        sa, sb = _sigmoid(ga + bias[:, :d]), _sigmoid(gb + bias[:, d:])
        dga = dzv * ya * sa * (1.0 - sa)
        dgb = dzv * yb * sb * (1.0 - sb)
        dgates = jnp.concatenate([dga, dgb], axis=1)
        return dzv * sa, dzv * sb, dgates, jnp.sum(dgates, axis=0, keepdims=True)

    dy_a, dy_b, dgates, g_b_gate = rowwise("gate_mix_bwd", gate_bwd_fn,
                                           [(dz, d, 0), (y_a, d, 0), (y_b, d, 0), (proj_main, d, 4), (proj_main, d, 5)],
                                           [], [b_gate], [(d, BF16), (d, BF16), (2 * d, BF16)], [(1, 2 * d)])
    do_hg = matmul("d_o_hg", dy_a, _bf(full["w_hg_o"]), "nt", out_dtype=BF16)
    grads["w_hg_o"] = matmul("gw_hg_o", o_hg, dy_a, "tn")
    do_mla = matmul("d_o_mla", dy_b, _bf(full["w_mla_o"]), "nt", out_dtype=BF16)
    grads["w_mla_o"] = matmul("gw_mla_o", o_mla, dy_b, "tn")

    early = ("w_hg_o", "w_mla_o", "w_out", "w_ffn_in", "w_ffn_out")
    late = ("w_in", "w_q_b", "w_kv_b")

    def start_grads(name, names):
        sends = [_bf(jnp.stack([split_full(n, grads[n], s) for s in range(4)])) for n in names]
        mines = []
        for n in names:
            r, c = wts[n].shape
            axis, size = (1, c) if n in COL_SHARDED else (0, r)
            mines.append(lax.dynamic_slice_in_dim(grads[n], my_chip * size, size, axis=axis))
        handle, token = push_start(name, sends, per_chip=True)
        return handle, token, mines

    def finish_grads(tag, names, started, after):
        handle, _, mines = started
        _, landed = push_wait(f"grads_{tag}_wait", handle, after, per_chip=True)
        parts = []
        for n, mine, land in zip(names, mines, landed):
            r, c = mine.shape
            tr = _tile(r, 256, 16)
            land2 = land.reshape(3 * r, c)
            parts.append(rowwise(f"sum_chips_{n}", lambda a, r0, r1, r2: a + r0 + r1 + r2,
                                 [(mine, c, 0)] + [(land2, c, 0, k * (r // tr)) for k in range(3)],
                                 [], [], [(c, F32)], tm=tr)[0])
        sibs = swap_with_sibling(f"swap_{tag}", parts)
        return {n: [p, s] for n, p, s in zip(names, parts, sibs)}

    grads_early = start_grads("grads_early_start", early)
    token_a = grads_early[1]

    def hg_out_bwd_fn(do, o, hg, g):
        sg = _silu(hg)
        dn = do * sg
        dos, dgs, ons = [], 0.0, []
        for h in range(nh):
            sl = slice(h * HEAD, (h + 1) * HEAD)
            dx, dg = _rms_bwd(o[:, sl], g, dn[:, sl])
            dos.append(dx)
            dgs = dgs + dg
            ons.append(_rms(o[:, sl], g))
        dhg = do * jnp.concatenate(ons, axis=1) * _silu_grad(hg)
        return jnp.concatenate(dos, axis=1), dhg, dgs

    do_scan, dhg, g_hg_norm = rowwise("hgrn_out_bwd", hg_out_bwd_fn, [(do_hg, d, 0), (o_scan, d, 0), (proj_main, d, 3)],
                                      [], [hg_norm_g], [(d, BF16), (d, BF16)], [(1, HEAD)])
    dhq, dhf, dhi, g_lb = hgrn_bwd(proj_main, lb + token_a[0, 0], hg_consts, states, a_mats, do_scan, bl, lp, d)

    dq_cat, dk_cat, dv_att = attn_bwd_t(q_cat, k_cat, k_t, v_att, o_mla, do_mla, lse, bl, lp, nm, scale)

    def mla_prep_bwd_fn(dqc, dkc, dvv, cos, s_up, s_dn):
        dqs, dkvs, dkpe = [], [], 0.0
        for h in range(nm):
            dqs += [dqc[:, h * QK_PAD:h * QK_PAD + HEAD],
                    _rope_bwd(dqc[:, h * QK_PAD + HEAD:(h + 1) * QK_PAD], cos, s_up, s_dn)]
            dkvs += [dkc[:, h * QK_PAD:h * QK_PAD + HEAD], dvv[:, h * HEAD:(h + 1) * HEAD]]
            dkpe = dkpe + dkc[:, h * QK_PAD + HEAD:(h + 1) * QK_PAD]
        return jnp.concatenate(dqs, axis=1), jnp.concatenate(dkvs, axis=1), _rope_bwd(dkpe, cos, s_up, s_dn)

    dq_full, dkv_full, dkpe = rowwise("mla_prep_bwd", mla_prep_bwd_fn,
                                      [(dq_cat, nm * QK_PAD, 0), (dk_cat, nm * QK_PAD, 0), (dv_att, nm * HEAD, 0)],
                                      [t_cos, t_up, t_dn], [],
                                      [(nm * QK_PAD, BF16), (nm * QK_PAD, BF16), (HEAD, F32)])
    dqn = matmul("d_qn", dq_full, w_qb, "nt", out_dtype=BF16)
    g_wqb = matmul("gw_q_b", qn, dq_full, "tn")
    grads["w_q_b"] = g_wqb.reshape(ql, nm, QK_PAD)[:, :, :HEAD + ROPE].reshape(ql, nm * (HEAD + ROPE))
    dkvn = matmul("d_kvn", dkv_full, w_kvb, "nt", out_dtype=BF16)
    grads["w_kv_b"] = matmul("gw_kv_b", kvn, dkv_full, "tn")

    def mla_norms_bwd_fn(dqnv, dkvnv, cq, ckv, dkpev, gq, gk):
        dcq, dgq = _rms_bwd(cq, gq, dqnv)
        dckv, dgk = _rms_bwd(ckv, gk, dkvnv)
        return jnp.concatenate([dcq, dckv, dkpev], axis=1), dgq, dgk

    dmla, g_q_norm, g_kv_norm = rowwise("mla_norms_bwd", mla_norms_bwd_fn,
                                        [(dqn, ql, 0), (dkvn, kvl, 0), (proj_mla, ql, 0), (proj_mla, kvl, 1),
                                         (dkpe, HEAD, 0)], [], [q_a_norm_g, kv_a_norm_g],
                                        [(mla_w, BF16)], [(1, ql), (1, kvl)])

    d_pieces = [dhq, dhf, dhi, dhg, dgates, dmla]
    gw_parts = [matmul(f"gw_in_{k}", u1, dp, "tn") for k, dp in enumerate(d_pieces)]
    grads["w_in"] = jnp.concatenate(gw_parts[:4] + [gw_parts[5][:, :ql + kvl + ROPE], gw_parts[4]], axis=1)
    grads_late = start_grads("grads_late_start", late)
    w_mla_after = w_mla + grads_late[1][0, 0].astype(BF16)
    w_pieces = [w_main[:, 0:d], w_main[:, d:2 * d], w_main[:, 2 * d:3 * d], w_main[:, 3 * d:4 * d],
                w_main[:, 4 * d:6 * d], w_mla_after]
    du1 = matmul("d_u1", d_pieces, w_pieces, "nt", out_dtype=BF16)

    def first_bwd_fn(dh1v, h, du1v, is_meta, g):
        dx, dg = _rms_bwd(h, g, du1v)
        dh0v = dh1v + dx
        return dh0v, dg, dh0v * jnp.tile(is_meta, (1, d // HEAD))

    grad_x, g_mix_pre, meta_tile = rowwise(
        "norm_mix_pre_bwd", first_bwd_fn, [(dh1, d, 0), (h0, d, 0), (du1, d, 0)], [meta_rows], [mix_pre_g],
        [(d, F32, bl * seq, real_block)], [(1, d), (SEQ_BLOCK, d)])
    grad_x = grad_x.reshape(bl, seq, d)

    g_parts = finish_grads("early", early, grads_early, g_mix_pre)
    updates = {}

    def update(n, parts):
        w2 = wts[n].reshape(-1, wts[n].shape[-1])
        updates[n] = adamw("adamw_" + n, w2, [p.reshape(w2.shape) for p in parts], mom_m[n].reshape(w2.shape),
                           mom_v[n].reshape(w2.shape))

    for n in early:
        update(n, g_parts[n])
    g_parts = finish_grads("late", late, grads_late, updates[early[-1]][0])
    for n in late:
        update(n, g_parts[n])
    p0 = lb_soft[0:1]
    g_lb_logits = jnp.concatenate([g_lb * p0 * (1.0 - p0), -g_lb * p0 * (1.0 - p0)], axis=0)

    def row_of(vec):
        return vec.reshape(-1, d) if vec.size >= d else jnp.pad(vec.reshape(1, -1), ((0, 0), (0, d - vec.size)))

    small_parts = dict(b_gate=g_b_gate, lb_logits=g_lb_logits, hg_norm_g=g_hg_norm, q_a_norm_g=g_q_norm,
                       kv_a_norm_g=g_kv_norm, mix_pre_g=g_mix_pre, mix_post_g=g_mix_post, ffn_pre_g=g_ffn_pre,
                       ffn_post_g=g_ffn_post)
    g_meta = meta_tile[PAD_FRONT:PAD_FRONT + N_META]
    small_rows = [row_of(small_parts[n]) for n in SMALL] + [row_of(g_meta)]
    n_small = sum(r.shape[0] for r in small_rows)
    small = jnp.pad(jnp.concatenate(small_rows, axis=0), ((0, -(-n_small // 8) * 8 - n_small), (0, 0)))
    all_small = gather_small(small)
    small_t = small.shape[0]

    def sum8_fn(*slabs):
        acc = slabs[0]
        for s in slabs[1:]:
            acc = acc + s
        return acc

    (g_small,) = rowwise("sum_small", sum8_fn, [(all_small.reshape(8 * small_t, d), d, 0, k) for k in range(8)],
                         [], [], [(d, F32)], tm=small_t, n_rows=small_t)

    off = 0
    for n, part in zip(SMALL, small_rows[:-1]):
        rows = part.shape[0]
        update(n, [g_small[off:off + rows, :d].reshape(-1)[:wts[n].size]])
        off += rows
    update("meta_tokens", [lax.dynamic_slice_in_dim(g_small[off:off + N_META, :d], my_chip * mcols, mcols, axis=1)])

    loss = lax.psum(loss_part[0, 0], ("x", "y", "c"))

    def shaped(n, a):
        return a.reshape((1,) + wts[n].shape) if n in BIG else a.reshape(wts[n].shape)

    return (loss, grad_x, *[shaped(n, updates[n][k]) for k in range(4) for n in WEIGHTS])
```

```python
import functools
import math

import jax
import jax.numpy as jnp
from jax import lax
from jax.experimental import pallas as pl
from jax.experimental.pallas import tpu as pltpu

F32 = jnp.float32
BF16 = jnp.bfloat16
MESH = pl.DeviceIdType.MESH

N_META = 16
NORM_EPS = 1e-6
HEAD = 128
ROPE = 64
ROPE_HALF = ROPE // 2
QK_PAD = 2 * HEAD
ROPE_THETA = 10000.0
SEQ_BLOCK = 256
PAD_FRONT = SEQ_BLOCK - N_META
NEG = -1e30
VMEM_LIMIT = 56 * 1024 * 1024
ATTN_HEADS_PER_STEP = 1
ATTN_TILE_MAX = 768

ADAM_LR, ADAM_B1, ADAM_B2, ADAM_EPS, ADAM_WD, ADAM_STEP = 0.001, 0.9, 0.999, 1e-08, 0.01, 10

BIG = ("w_in", "w_hg_o", "w_q_b", "w_kv_b", "w_mla_o", "w_out", "w_ffn_in", "w_ffn_out")
COL_SHARDED = ("w_in", "w_q_b", "w_kv_b", "w_ffn_in")
SMALL = ("b_gate", "lb_logits", "hg_norm_g", "q_a_norm_g", "kv_a_norm_g", "mix_pre_g", "mix_post_g",
         "ffn_pre_g", "ffn_post_g")
WEIGHTS = ("meta_tokens", "w_in", "b_gate", "lb_logits", "hg_norm_g", "w_hg_o", "q_a_norm_g", "w_q_b",
           "kv_a_norm_g", "w_kv_b", "w_mla_o", "w_out", "mix_pre_g", "mix_post_g", "ffn_pre_g", "ffn_post_g",
           "w_ffn_in", "w_ffn_out")


def _tile(n, cap, unit=128):
    if n <= cap:
        return n
    best = None
    for t in range(unit, cap + 1, unit):
        if n % t == 0:
            best = t
    assert best is not None, (n, cap, unit)
    return best


def _sigmoid(x):
    return 1.0 / (1.0 + jnp.exp(-x))


def _bf(x):
    return x.astype(BF16)


def rowwise(name, fn, row_ins, seq_tabs, consts, row_outs, acc_outs=(), tm=SEQ_BLOCK, n_rows=None):
    t_rows = row_ins[0][0].shape[0] if n_rows is None else n_rows
    nt = t_rows // tm
    assert t_rows % tm == 0
    n_in = len(row_ins) + len(seq_tabs) + len(consts)
    n_row = len(row_outs)

    def body(*refs):
        vals = [r[...].astype(F32) for r in refs[:n_in]]
        res = fn(*vals)
        if not isinstance(res, (tuple, list)):
            res = (res,)
        outs = refs[n_in:]
        for k in range(n_row):
            outs[k][...] = res[k].astype(outs[k].dtype)
        if acc_outs:
            @pl.when(pl.program_id(0) == 0)
            def _():
                for k in range(len(acc_outs)):
                    outs[n_row + k][...] = jnp.zeros_like(outs[n_row + k])

            for k in range(len(acc_outs)):
                outs[n_row + k][...] += res[n_row + k]

    row_ins = [tuple(e) + (0,) * (4 - len(e)) for e in row_ins]
    in_specs = [pl.BlockSpec((tm, w), functools.partial(lambda i, j, ro: (ro(i) if callable(ro) else i + ro, j),
                                                        j=j, ro=ro)) for (_, w, j, ro) in row_ins]
    for tab in seq_tabs:
        per = tab.shape[0] // tm
        in_specs.append(pl.BlockSpec((tm, tab.shape[1]), functools.partial(lambda i, per: (i % per, 0), per=per)))
    for c in consts:
        in_specs.append(pl.BlockSpec(c.shape, lambda i: (0, 0)))
    row_outs = [tuple(e) + (t_rows, None)[len(e) - 2:] for e in row_outs]
    out_specs = [pl.BlockSpec((tm, w), functools.partial(lambda i, rm: (i if rm is None else rm(i), 0), rm=rm))
                 for (w, _, _, rm) in row_outs]
    out_specs += [pl.BlockSpec(s, lambda i: (0, 0)) for s in acc_outs]
    out_shape = [jax.ShapeDtypeStruct((rows, w), dt) for (w, dt, rows, _) in row_outs]
    out_shape += [jax.ShapeDtypeStruct(s, F32) for s in acc_outs]
    res = pl.pallas_call(
        body, name=name, grid=(nt,), in_specs=in_specs, out_specs=out_specs, out_shape=out_shape,
        compiler_params=pltpu.CompilerParams(dimension_semantics=("arbitrary",)),
    )(*[e[0] for e in row_ins], *seq_tabs, *consts)
    return res


def matmul(name, a, b, mode, out_dtype=F32):
    if mode != "tn":
        return _matmul_resident(name, a if isinstance(a, (list, tuple)) else [a],
                                b if isinstance(b, (list, tuple)) else [b], mode, out_dtype)
    kdim, m = a.shape
    n = b.shape[1]
    tn = _tile(n, 1536)
    tm, tk = _tile(m, 1408 if tn <= 1024 else 1024), _tile(kdim, 1024)
    nk = kdim // tk

    def body(a_ref, b_ref, o_ref, acc_ref):
        k = pl.program_id(2)

        @pl.when(k == 0)
        def _():
            acc_ref[...] = jnp.zeros_like(acc_ref)

        acc_ref[...] += lax.dot_general(a_ref[...], b_ref[...], TN_DIMS, preferred_element_type=F32)

        @pl.when(k == nk - 1)
        def _():
            o_ref[...] = acc_ref[...].astype(o_ref.dtype)

    return pl.pallas_call(
        body, name=name, grid=(m // tm, n // tn, nk),
        in_specs=[pl.BlockSpec((tk, tm), lambda i, j, k: (k, i)), pl.BlockSpec((tk, tn), lambda i, j, k: (k, j))],
        out_specs=pl.BlockSpec((tm, tn), lambda i, j, k: (i, j)),
        out_shape=jax.ShapeDtypeStruct((m, n), out_dtype),
        scratch_shapes=[pltpu.VMEM((tm, tn), F32)],
        compiler_params=pltpu.CompilerParams(dimension_semantics=("arbitrary", "arbitrary", "arbitrary"),
                                             vmem_limit_bytes=VMEM_LIMIT),
    )(a, b)


def _matmul_resident(name, a_list, b_list, mode, out_dtype):
    m = a_list[0].shape[0]
    n = b_list[0].shape[1] if mode == "nn" else b_list[0].shape[0]
    k_total = sum(a.shape[1] for a in a_list)
    out_bytes = 2 if out_dtype == BF16 else 4
    budget = VMEM_LIMIT - 4 * k_total * n - (6 << 20)
    tm = 1024
    while tm > 128 and 2 * tm * (2 * k_total + out_bytes * n) > budget:
        tm //= 2
    tm = _tile(m, tm)
    cn = _tile(n, 1024)
    npairs = len(a_list)

    def body(*refs):
        a_refs, b_refs, o_ref = refs[:npairs], refs[npairs:2 * npairs], refs[2 * npairs]
        for c in range(n // cn):
            acc = None
            for a_ref, b_ref in zip(a_refs, b_refs):
                if mode == "nn":
                    part = jnp.dot(a_ref[...], b_ref[:, pl.ds(c * cn, cn)], preferred_element_type=F32)
                else:
                    part = lax.dot_general(a_ref[...], b_ref[pl.ds(c * cn, cn), :], NT_DIMS,
                                           preferred_element_type=F32)
                acc = part if acc is None else acc + part
            o_ref[:, pl.ds(c * cn, cn)] = acc.astype(o_ref.dtype)

    in_specs = [pl.BlockSpec((tm, a.shape[1]), lambda i: (i, 0)) for a in a_list]
    in_specs += [pl.BlockSpec(b.shape, lambda i: (0, 0)) for b in b_list]
    return pl.pallas_call(
        body, name=name, grid=(m // tm,), in_specs=in_specs,
        out_specs=pl.BlockSpec((tm, n), lambda i: (i, 0)),
        out_shape=jax.ShapeDtypeStruct((m, n), out_dtype),
        compiler_params=pltpu.CompilerParams(dimension_semantics=("arbitrary",), vmem_limit_bytes=VMEM_LIMIT),
    )(*a_list, *b_list)


def matmul_fused(name, fn, row_ins, consts, weight, pieces, mode, extra_outs, out_dtype=BF16, tm=256):
    row_ins = [tuple(e) + (0,) * (4 - len(e)) for e in row_ins]
    t_rows = row_ins[0][0].shape[0]
    tm = _tile(t_rows, tm)
    n = weight.shape[1] if mode == "nn" else weight.shape[0]
    n_in = len(row_ins) + len(consts)
    n_parts = len(pieces)

    def body(*refs):
        w_hbm = refs[n_in]
        outs = refs[n_in + 1:n_in + 2 + len(extra_outs)]
        w_ref, sem = refs[-2], refs[-1]

        @pl.when(pl.program_id(0) == 0)
        def _():
            cp = pltpu.make_async_copy(w_hbm, w_ref, sem)
            cp.start()
            cp.wait()

        res = fn(*[r[...].astype(F32) for r in refs[:n_in]])
        acc = None
        for a_p, (k0, k1) in zip(res[:n_parts], pieces):
            if mode == "nn":
                part = jnp.dot(_bf(a_p), w_ref[pl.ds(k0, k1 - k0), :], preferred_element_type=F32)
            else:
                part = lax.dot_general(_bf(a_p), w_ref[:, pl.ds(k0, k1 - k0)], NT_DIMS, preferred_element_type=F32)
            acc = part if acc is None else acc + part
        outs[0][...] = acc.astype(outs[0].dtype)
        for o_ref, val in zip(outs[1:], res[n_parts:]):
            o_ref[...] = val.astype(o_ref.dtype)

    in_specs = [pl.BlockSpec((tm, w), functools.partial(lambda i, j, ro: (i + ro, j), j=j, ro=ro))
                for (_, w, j, ro) in row_ins]
    in_specs += [pl.BlockSpec(c.shape, lambda i: (0, 0)) for c in consts]
    in_specs.append(pl.BlockSpec(memory_space=pl.ANY))
    widths = [(n, out_dtype)] + list(extra_outs)
    return pl.pallas_call(
        body, name=name, grid=(t_rows // tm,), in_specs=in_specs,
        out_specs=[pl.BlockSpec((tm, w), lambda i: (i, 0)) for (w, _) in widths],
        out_shape=[jax.ShapeDtypeStruct((t_rows, w), dt) for (w, dt) in widths],
        scratch_shapes=[pltpu.VMEM(weight.shape, weight.dtype), pltpu.SemaphoreType.DMA],
        compiler_params=pltpu.CompilerParams(dimension_semantics=("arbitrary",), vmem_limit_bytes=VMEM_LIMIT),
    )(*[e[0] for e in row_ins], *consts, weight)


def _rms(x, g):
    r = lax.rsqrt(jnp.mean(x * x, axis=-1, keepdims=True) + NORM_EPS)
    return x * r * g


def _rms_bwd(x, g, dy):
    r = lax.rsqrt(jnp.mean(x * x, axis=-1, keepdims=True) + NORM_EPS)
    xh = x * r
    dyg = dy * g
    dx = r * (dyg - xh * jnp.mean(dyg * xh, axis=-1, keepdims=True))
    return dx, jnp.sum(dy * xh, axis=0, keepdims=True)


def _silu(x):
    return x * _sigmoid(x)


def _silu_grad(x):
    s = _sigmoid(x)
    return s * (1.0 + x * (1.0 - s))


def _rope(xs, cos, s_up, s_dn):
    return xs * cos + pltpu.roll(xs, ROPE_HALF, 1) * s_up + pltpu.roll(xs, HEAD - ROPE_HALF, 1) * s_dn


def _rope_bwd(dy, cos, s_up, s_dn):
    return dy * cos + pltpu.roll(dy * s_up, HEAD - ROPE_HALF, 1) + pltpu.roll(dy * s_dn, ROPE_HALF, 1)


HG_SUB = 128
HG_LEVELS = 7
HG_E_ROWS = (HG_LEVELS + 1) * HG_SUB
TN_DIMS = (((0,), (0,)), ((), ()))
NT_DIMS = (((1,), (1,)), ((), ()))


def _hg_constants():
    import numpy as np
    n = HG_SUB
    r = np.arange(n)[:, None]
    c = np.arange(n)[None, :]
    cs, ps = [], []
    for lvl in range(HG_LEVELS):
        m = (n // 2) >> lvl
        upper = (r % (2 * m)) >= m
        mid = (r // (2 * m)) * (2 * m) + m - 1
        cs.append(np.where(upper, (c > mid) & (c <= r), (c > r) & (c <= mid)))
        ps.append(((r // (2 * m)) == (c // (2 * m))) & upper & ((c % (2 * m)) < m))
    cs.append(c <= r)
    cs.append(np.ones((8, n), bool))
    cstack = np.concatenate(cs, 0).astype(np.float32)
    pstack = np.concatenate(ps, 0).astype(np.float32)
    pstack_t = np.concatenate([p.T for p in ps], 0).astype(np.float32)
    return (jnp.asarray(cstack, BF16), jnp.asarray(cstack[:HG_E_ROWS].T, BF16), jnp.asarray(pstack, F32),
            jnp.asarray(pstack_t, F32))


def _split_dot(c_bf, x):
    hi = _bf(x)
    lo = _bf(x - hi.astype(F32))
    r2 = jnp.dot(c_bf, jnp.concatenate([hi, lo], axis=1), preferred_element_type=F32)
    return r2[:, :HEAD] + r2[:, HEAD:]


def _hg_gates(hq, hf, lb):
    sq = _sigmoid(hq)
    sg = _sigmoid(hf)
    fg = lb + (1.0 - lb) * sg
    return sq, hq * sq, sg, fg, 1.0 - fg, jnp.log(fg)


def _hg_block_fwd(st, hq, hf, hi, lb, cstack, p_ref):
    _, q, _, _, k, g = _hg_gates(hq, hf, lb)
    v = hi
    e = _split_dot(cstack, g)
    bc = e[HG_LEVELS * HG_SUB:HG_E_ROWS]
    b_last = jnp.tile(e[HG_E_ROWS:], (HG_SUB // 8, 1))
    a = jnp.zeros((HG_SUB, HG_SUB), F32)
    for lvl in range(HG_LEVELS):
        x = jnp.exp(e[lvl * HG_SUB:(lvl + 1) * HG_SUB])
        a = a + p_ref[pl.ds(lvl * HG_SUB, HG_SUB), :] * lax.dot_general(_bf(q * x), _bf(k * x), NT_DIMS,
                                                                          preferred_element_type=F32)
    a_bf = _bf(a)
    diag = jnp.sum(q * k, axis=1, keepdims=True)
    o = (jnp.dot(a_bf, _bf(v), preferred_element_type=F32) + diag * v
         + lax.dot_general(_bf(q * jnp.exp(bc)), _bf(st), NT_DIMS, preferred_element_type=F32))
    kd = k * jnp.exp(b_last - bc)
    st_out = st * jnp.exp(b_last) + lax.dot_general(_bf(v), _bf(kd), TN_DIMS, preferred_element_type=F32)
    return st_out, o, a_bf


def _hg_block_bwd(st, dst_out, do, hq, hf, hi, lb, a_bf, cstack, cstack_t, p_ref, pt_ref):
    sq, q, sg, fg, k, g = _hg_gates(hq, hf, lb)
    v = hi
    e = _split_dot(cstack, g)
    bc = e[HG_LEVELS * HG_SUB:HG_E_ROWS]
    b_last = jnp.tile(e[HG_E_ROWS:], (HG_SUB // 8, 1))
    eb = jnp.exp(bc)
    qb = q * eb
    er = jnp.exp(b_last - bc)
    kd = k * er
    e_last = jnp.exp(b_last)
    do_bf, v_bf, dst_bf = _bf(do), _bf(v), _bf(dst_out)
    da = lax.dot_general(do_bf, v_bf, NT_DIMS, preferred_element_type=F32)
    dat = lax.dot_general(v_bf, do_bf, NT_DIMS, preferred_element_type=F32)
    d_diag = jnp.sum(do * v, axis=1, keepdims=True)
    dv = (lax.dot_general(a_bf, do_bf, TN_DIMS, preferred_element_type=F32)
          + jnp.sum(q * k, axis=1, keepdims=True) * do
          + lax.dot_general(_bf(kd), dst_bf, NT_DIMS, preferred_element_type=F32))
    dqb = jnp.dot(do_bf, _bf(st), preferred_element_type=F32)
    dst = dst_out * e_last + lax.dot_general(do_bf, _bf(qb), TN_DIMS, preferred_element_type=F32)
    dkd = jnp.dot(v_bf, dst_bf, preferred_element_type=F32)
    dq = dqb * eb + d_diag * k
    dk = dkd * er + d_diag * q
    d_last = (jnp.sum(dst_out * st * e_last, axis=0, keepdims=True)
              + jnp.sum(dkd * kd, axis=0, keepdims=True))
    des = []
    for lvl in range(HG_LEVELS):
        x = jnp.exp(e[lvl * HG_SUB:(lvl + 1) * HG_SUB])
        qh, kh = q * x, k * x
        dm = _bf(p_ref[pl.ds(lvl * HG_SUB, HG_SUB), :] * da)
        dmt = _bf(pt_ref[pl.ds(lvl * HG_SUB, HG_SUB), :] * dat)
        dqh = jnp.dot(dm, _bf(kh), preferred_element_type=F32)
        dkh = jnp.dot(dmt, _bf(qh), preferred_element_type=F32)
        dq = dq + dqh * x
        dk = dk + dkh * x
        des.append(dqh * qh + dkh * kh)
    des.append(dqb * qb - dkd * kd)
    dg = _split_dot(cstack_t, jnp.concatenate(des, axis=0)) + d_last
    dfg = dg / fg - dk
    dhq = dq * (sq * (1.0 + hq * (1.0 - sq)))
    dhf = dfg * (1.0 - lb) * sg * (1.0 - sg)
    return dst, dhq, dhf, dv, jnp.sum(dfg * (1.0 - sg), axis=0, keepdims=True)


def hgrn_fwd(proj_main, lb, consts, bl, lp, d):
    nh = d // HEAD
    rows_blk = _tile(lp, 768, SEQ_BLOCK)
    nb = lp // rows_blk
    spb = rows_blk // HG_SUB
    cstack, _, pstack, _ = consts

    def body(hq_ref, hf_ref, hi_ref, lb_ref, c_ref, p_ref, o_ref, st_ref, a_ref, s_ref):
        j = pl.program_id(2)

        @pl.when(j == 0)
        def _():
            s_ref[...] = jnp.zeros_like(s_ref)

        lbv = lb_ref[...]
        cs = c_ref[...]

        def sub(n, carry):
            r = pl.multiple_of(n * HG_SUB, HG_SUB)
            st = s_ref[...]
            st_ref[0, 0, pl.ds(n, 1)] = st[None]
            st_out, o, a_bf = _hg_block_fwd(st, hq_ref[pl.ds(r, HG_SUB), :].astype(F32),
                                            hf_ref[pl.ds(r, HG_SUB), :].astype(F32),
                                            hi_ref[pl.ds(r, HG_SUB), :].astype(F32), lbv, cs, p_ref)
            s_ref[...] = st_out
            o_ref[pl.ds(r, HG_SUB), :] = o.astype(o_ref.dtype)
            a_ref[0, 0, pl.ds(n, 1)] = a_bf[None]
            return carry

        lax.fori_loop(0, spb, sub, 0, unroll=3 if spb % 3 == 0 else 2)

    def colspec(off):
        return pl.BlockSpec((rows_blk, HEAD), functools.partial(lambda h, b, j, off: (b * nb + j, off + h), off=off))

    whole = lambda arr: pl.BlockSpec(arr.shape, lambda h, b, j: (0, 0))
    return pl.pallas_call(
        body, name="hgrn_fwd", grid=(nh, bl, nb),
        in_specs=[colspec(0), colspec(nh), colspec(2 * nh), pl.BlockSpec((1, HEAD), lambda h, b, j: (0, h)),
                  whole(cstack), whole(pstack)],
        out_specs=[pl.BlockSpec((rows_blk, HEAD), lambda h, b, j: (b * nb + j, h)),
                   pl.BlockSpec((1, 1, spb, HEAD, HEAD), lambda h, b, j: (b, h, j, 0, 0)),
                   pl.BlockSpec((1, 1, spb, HG_SUB, HG_SUB), lambda h, b, j: (b, h, j, 0, 0))],
        out_shape=[jax.ShapeDtypeStruct((bl * lp, d), BF16),
                   jax.ShapeDtypeStruct((bl, nh, lp // HG_SUB, HEAD, HEAD), F32),
                   jax.ShapeDtypeStruct((bl, nh, lp // HG_SUB, HG_SUB, HG_SUB), BF16)],
        scratch_shapes=[pltpu.VMEM((HEAD, HEAD), F32)],
        compiler_params=pltpu.CompilerParams(dimension_semantics=("arbitrary", "arbitrary", "arbitrary")),
    )(proj_main, proj_main, proj_main, lb, cstack, pstack)


def hgrn_bwd(proj_main, lb, consts, states, a_mats, do_scan, bl, lp, d):
    nh = d // HEAD
    rows_blk = _tile(lp, 768, SEQ_BLOCK)
    nb = lp // rows_blk
    spb = rows_blk // HG_SUB
    cstack, cstack_t, pstack, pstack_t = consts

    def body(hq_ref, hf_ref, hi_ref, lb_ref, c_ref, ct_ref, p_ref, pt_ref, st_ref, a_ref, do_ref,
             dq_ref, df_ref, di_ref, dlb_ref, ds_ref):
        b_id, j = pl.program_id(1), pl.program_id(2)
        blk = nb - 1 - j

        @pl.when(j == 0)
        def _():
            ds_ref[...] = jnp.zeros_like(ds_ref)

        @pl.when((j == 0) & (b_id == 0))
        def _():
            dlb_ref[...] = jnp.zeros_like(dlb_ref)

        lbv = lb_ref[...]
        cs = c_ref[...]
        cst = ct_ref[...]

        def sub(i, carry):
            n = spb - 1 - i
            r = pl.multiple_of(n * HG_SUB, HG_SUB)
            dst, dhq, dhf, dhi, dlb = _hg_block_bwd(
                st_ref[0, 0, pl.ds(n, 1)][0], ds_ref[...], do_ref[pl.ds(r, HG_SUB), :].astype(F32),
                hq_ref[pl.ds(r, HG_SUB), :].astype(F32), hf_ref[pl.ds(r, HG_SUB), :].astype(F32),
                hi_ref[pl.ds(r, HG_SUB), :].astype(F32), lbv,
                a_ref[0, 0, pl.ds(n, 1)][0], cs, cst, p_ref, pt_ref)
            ds_ref[...] = dst
            dq_ref[pl.ds(r, HG_SUB), :] = dhq.astype(dq_ref.dtype)
            df_ref[pl.ds(r, HG_SUB), :] = dhf.astype(df_ref.dtype)
            di_ref[pl.ds(r, HG_SUB), :] = dhi.astype(di_ref.dtype)
            dlb_ref[...] += dlb
            return carry

        lax.fori_loop(0, spb, sub, 0, unroll=2)

    def colspec(off):
        return pl.BlockSpec((rows_blk, HEAD),
                            functools.partial(lambda h, b, j, off: (b * nb + nb - 1 - j, off + h), off=off))

    whole = lambda arr: pl.BlockSpec(arr.shape, lambda h, b, j: (0, 0))
    mats = lambda: pl.BlockSpec((1, 1, spb, HEAD, HEAD), lambda h, b, j: (b, h, nb - 1 - j, 0, 0))
    t_rows = bl * lp
    return pl.pallas_call(
        body, name="hgrn_bwd", grid=(nh, bl, nb),
        in_specs=[colspec(0), colspec(nh), colspec(2 * nh), pl.BlockSpec((1, HEAD), lambda h, b, j: (0, h)),
                  whole(cstack), whole(cstack_t), whole(pstack), whole(pstack_t), mats(), mats(), colspec(0)],
        out_specs=[colspec(0), colspec(0), colspec(0), pl.BlockSpec((1, HEAD), lambda h, b, j: (0, h))],
        out_shape=[jax.ShapeDtypeStruct((t_rows, d), BF16)] * 3 + [jax.ShapeDtypeStruct((1, d), F32)],
        scratch_shapes=[pltpu.VMEM((HEAD, HEAD), F32)],
        compiler_params=pltpu.CompilerParams(dimension_semantics=("arbitrary", "arbitrary", "arbitrary")),
    )(proj_main, proj_main, proj_main, lb, cstack, cstack_t, pstack, pstack_t, states, a_mats, do_scan)


def _allowed(row0, col0, nr, nc, transposed=False):
    if transposed:
        col = col0 + lax.broadcasted_iota(jnp.int32, (nc, 1), 0)
        row = row0 + lax.broadcasted_iota(jnp.int32, (1, nr), 1)
    else:
        row = row0 + lax.broadcasted_iota(jnp.int32, (nr, 1), 0)
        col = col0 + lax.broadcasted_iota(jnp.int32, (1, nc), 1)
    return (col <= row) & ((col >= PAD_FRONT) | (row < PAD_FRONT))


def attn_fwd(q_cat, k_cat, v, bl, lp, nm, scale):
    tq = tk = SEQ_BLOCK
    nq = lp // tq

    def body(q_ref, k_ref, v_ref, o_ref, lse_ref, m_ref, l_ref, acc_ref):
        i = pl.program_id(2)
        q = q_ref[...]
        m_ref[...] = jnp.full_like(m_ref, NEG)
        l_ref[...] = jnp.zeros_like(l_ref)
        acc_ref[...] = jnp.zeros_like(acc_ref)

        def kstep(c, carry):
            c0 = pl.multiple_of(c * tk, tk)
            s = lax.dot_general(q, k_ref[pl.ds(c0, tk), :], NT_DIMS, preferred_element_type=F32) * scale
            s = jnp.where(_allowed(i * tq, c * tk, tq, tk), s, NEG)
            m_old = m_ref[...]
            m_new = jnp.maximum(m_old, jnp.max(s, axis=1, keepdims=True))
            alpha = jnp.exp(m_old - m_new)
            p = jnp.exp(s - m_new)
            l_ref[...] = alpha * l_ref[...] + jnp.sum(p, axis=1, keepdims=True)
            acc_ref[...] = alpha * acc_ref[...] + jnp.dot(_bf(p), v_ref[pl.ds(c0, tk), :],
                                                          preferred_element_type=F32)
            m_ref[...] = m_new
            return carry

        lax.fori_loop(0, i + 1, kstep, 0)
        o_ref[...] = (acc_ref[...] / l_ref[...]).astype(o_ref.dtype)
        lse_ref[0, 0] = m_ref[...] + jnp.log(l_ref[...])

    return pl.pallas_call(
        body, name="attn_fwd", grid=(bl, nm, nq),
        in_specs=[pl.BlockSpec((tq, QK_PAD), lambda b, h, i: (b * nq + i, h)),
                  pl.BlockSpec((lp, QK_PAD), lambda b, h, i: (b, h)),
                  pl.BlockSpec((lp, HEAD), lambda b, h, i: (b, h))],
        out_specs=[pl.BlockSpec((tq, HEAD), lambda b, h, i: (b * nq + i, h)),
                   pl.BlockSpec((1, 1, tq, 1), lambda b, h, i: (b, h, i, 0))],
        out_shape=[jax.ShapeDtypeStruct((bl * lp, nm * HEAD), BF16),
                   jax.ShapeDtypeStruct((bl, nm, lp, 1), F32)],
        scratch_shapes=[pltpu.VMEM((tq, 1), F32), pltpu.VMEM((tq, 1), F32), pltpu.VMEM((tq, HEAD), F32)],
        compiler_params=pltpu.CompilerParams(dimension_semantics=("arbitrary", "arbitrary", "arbitrary")),
    )(q_cat, k_cat, v)


def attn_bwd_dq(q_cat, k_cat, v, o, do, lse, bl, lp, nm, scale):
    tq = tk = SEQ_BLOCK
    nq = lp // tq

    def body(q_ref, k_ref, v_ref, o_ref, do_ref, lse_ref, dq_ref, dl_ref, acc_ref):
        i = pl.program_id(2)
        q = q_ref[...]
        do_b = do_ref[...]
        delta = jnp.sum(o_ref[...].astype(F32) * do_b.astype(F32), axis=1, keepdims=True)
        lse_b = lse_ref[0, 0]
        acc_ref[...] = jnp.zeros_like(acc_ref)

        def kstep(c, carry):
            c0 = pl.multiple_of(c * tk, tk)
            ks = k_ref[pl.ds(c0, tk), :]
            s = lax.dot_general(q, ks, NT_DIMS, preferred_element_type=F32) * scale
            p = jnp.where(_allowed(i * tq, c * tk, tq, tk), jnp.exp(s - lse_b), 0.0)
            dp = lax.dot_general(do_b, v_ref[pl.ds(c0, tk), :], NT_DIMS, preferred_element_type=F32)
            ds = p * (dp - delta)
            acc_ref[...] += jnp.dot(_bf(ds), ks, preferred_element_type=F32)
            return carry

        lax.fori_loop(0, i + 1, kstep, 0)
        dq_ref[...] = acc_ref[...] * scale
        dl_ref[0, 0] = delta

    return pl.pallas_call(
        body, name="attn_bwd_dq", grid=(bl, nm, nq),
        in_specs=[pl.BlockSpec((tq, QK_PAD), lambda b, h, i: (b * nq + i, h)),
                  pl.BlockSpec((lp, QK_PAD), lambda b, h, i: (b, h)),
                  pl.BlockSpec((lp, HEAD), lambda b, h, i: (b, h)),
                  pl.BlockSpec((tq, HEAD), lambda b, h, i: (b * nq + i, h)),
                  pl.BlockSpec((tq, HEAD), lambda b, h, i: (b * nq + i, h)),
                  pl.BlockSpec((1, 1, tq, 1), lambda b, h, i: (b, h, i, 0))],
        out_specs=[pl.BlockSpec((tq, QK_PAD), lambda b, h, i: (b * nq + i, h)),
                   pl.BlockSpec((1, 1, tq, 1), lambda b, h, i: (b, h, i, 0))],
        out_shape=[jax.ShapeDtypeStruct((bl * lp, nm * QK_PAD), F32),
                   jax.ShapeDtypeStruct((bl, nm, lp, 1), F32)],
        scratch_shapes=[pltpu.VMEM((tq, QK_PAD), F32)],
        compiler_params=pltpu.CompilerParams(dimension_semantics=("arbitrary", "arbitrary", "arbitrary")),
    )(q_cat, k_cat, v, o, do, lse)


def attn_bwd_dkv(q_cat, k_cat, v, do, lse_row, delta_row, bl, lp, nm, scale):
    tq = tk = SEQ_BLOCK
    nq = lp // tq

    def body(q_ref, k_ref, v_ref, do_ref, lse_ref, dl_ref, dk_ref, dv_ref):
        i = pl.program_id(2)
        kt = k_ref[...]
        vt = v_ref[...]
        dk_ref[...] = jnp.zeros_like(dk_ref)
        dv_ref[...] = jnp.zeros_like(dv_ref)

        def qstep(c, carry):
            c0 = pl.multiple_of(c * tq, tq)
            qs = q_ref[pl.ds(c0, tq), :]
            dos = do_ref[pl.ds(c0, tq), :]
            st = lax.dot_general(kt, qs, NT_DIMS, preferred_element_type=F32) * scale
            pt = jnp.where(_allowed(c * tq, i * tk, tq, tk, transposed=True),
                           jnp.exp(st - lse_ref[0, 0, pl.ds(c, 1)][0]), 0.0)
            dv_ref[...] += jnp.dot(_bf(pt), dos, preferred_element_type=F32)
            dpt = lax.dot_general(vt, dos, NT_DIMS, preferred_element_type=F32)
            dst = pt * (dpt - dl_ref[0, 0, pl.ds(c, 1)][0])
            dk_ref[...] += jnp.dot(_bf(dst), qs, preferred_element_type=F32)
            return carry

        lax.fori_loop(i, nq, qstep, 0)
        dk_ref[...] = dk_ref[...] * scale

    return pl.pallas_call(
        body, name="attn_bwd_dkv", grid=(bl, nm, nq),
        in_specs=[pl.BlockSpec((lp, QK_PAD), lambda b, h, i: (b, h)),
                  pl.BlockSpec((tk, QK_PAD), lambda b, h, i: (b * nq + i, h)),
                  pl.BlockSpec((tk, HEAD), lambda b, h, i: (b * nq + i, h)),
                  pl.BlockSpec((lp, HEAD), lambda b, h, i: (b, h)),
                  pl.BlockSpec((1, 1, nq, 1, tq), lambda b, h, i: (b, h, 0, 0, 0)),
                  pl.BlockSpec((1, 1, nq, 1, tq), lambda b, h, i: (b, h, 0, 0, 0))],
        out_specs=[pl.BlockSpec((tk, QK_PAD), lambda b, h, i: (b * nq + i, h)),
                   pl.BlockSpec((tk, HEAD), lambda b, h, i: (b * nq + i, h))],
        out_shape=[jax.ShapeDtypeStruct((bl * lp, nm * QK_PAD), F32),
                   jax.ShapeDtypeStruct((bl * lp, nm * HEAD), F32)],
        compiler_params=pltpu.CompilerParams(dimension_semantics=("arbitrary", "arbitrary", "arbitrary")),
    )(q_cat, k_cat, v, do, lse_row, delta_row)


def _key_query_mask(key0, qry0, nk, nq_, causal):
    key = key0 + lax.broadcasted_iota(jnp.int32, (nk, 1), 0)
    if not causal:
        return key >= PAD_FRONT
    qry = qry0 + lax.broadcasted_iota(jnp.int32, (1, nq_), 1)
    return (key <= qry) & (key >= PAD_FRONT)


def _attn_tile(lp):
    return _tile(lp, ATTN_TILE_MAX, SEQ_BLOCK)


def attn_fwd_t(q_cat, k_cat, v_t, bl, lp, nm, scale):
    tq = tk = _attn_tile(lp)
    nq = lp // tq
    hp = ATTN_HEADS_PER_STEP
    assert nm % hp == 0

    def body(q_ref, k_ref, vt_ref, o_ref, lse_ref, m_ref, l_ref, acc_ref):
        i = pl.program_id(2)
        m_ref[...] = jnp.full_like(m_ref, NEG)
        l_ref[...] = jnp.zeros_like(l_ref)
        acc_ref[...] = jnp.zeros_like(acc_ref)

        def step(c, mask):
            c0 = pl.multiple_of(c * tk, tk)
            for hh in range(hp):
                cols = pl.ds(hh * QK_PAD, QK_PAD)
                st = lax.dot_general(k_ref[pl.ds(c0, tk), cols], q_ref[:, cols], NT_DIMS,
                                     preferred_element_type=F32) * scale
                if mask is not None:
                    st = jnp.where(_key_query_mask(c * tk, i * tq, tk, tq, mask == "causal"), st, NEG)
                m_old = m_ref[hh]
                m_new = jnp.maximum(m_old, jnp.max(st, axis=0, keepdims=True))
                alpha = jnp.exp(m_old - m_new)
                pt = jnp.exp(st - m_new)
                l_ref[hh] = alpha * l_ref[hh] + jnp.sum(pt, axis=0, keepdims=True)
                acc_ref[hh] = alpha * acc_ref[hh] + jnp.dot(vt_ref[0, hh, pl.ds(c, 1)][0], _bf(pt),
                                                            preferred_element_type=F32)
                m_ref[hh] = m_new

        def mid(c, carry):
            step(c, None)
            return carry

        @pl.when(i == 0)
        def _():
            step(0, "causal")

        @pl.when(i > 0)
        def _():
            step(0, "pad")
            lax.fori_loop(1, i, mid, 0)
            step(i, "causal")

        for hh in range(hp):
            o_ref[:, pl.ds(hh * HEAD, HEAD)] = jnp.transpose(acc_ref[hh] / l_ref[hh]).astype(o_ref.dtype)
            lse_ref[0, hh, 0] = m_ref[hh] + jnp.log(l_ref[hh])

    return pl.pallas_call(
        body, name="attn_fwd", grid=(bl, nm // hp, nq),
        in_specs=[pl.BlockSpec((tq, hp * QK_PAD), lambda b, h, i: (b * nq + i, h)),
                  pl.BlockSpec((lp, hp * QK_PAD), lambda b, h, i: (b, h)),
                  pl.BlockSpec((1, hp, nq, HEAD, tk), lambda b, h, i: (b, h, 0, 0, 0))],
        out_specs=[pl.BlockSpec((tq, hp * HEAD), lambda b, h, i: (b * nq + i, h)),
                   pl.BlockSpec((1, hp, 1, 1, tq), lambda b, h, i: (b, h, i, 0, 0))],
        out_shape=[jax.ShapeDtypeStruct((bl * lp, nm * HEAD), BF16),
                   jax.ShapeDtypeStruct((bl, nm, nq, 1, tq), F32)],
        scratch_shapes=[pltpu.VMEM((hp, 1, tq), F32), pltpu.VMEM((hp, 1, tq), F32), pltpu.VMEM((hp, HEAD, tq), F32)],
        compiler_params=pltpu.CompilerParams(dimension_semantics=("arbitrary", "arbitrary", "arbitrary")),
    )(q_cat, k_cat, v_t)


def attn_bwd_t(q_cat, k_cat, k_t, v, o, do, lse, bl, lp, nm, scale):
    tq = tk = _attn_tile(lp)
    nq = lp // tq
    hp = ATTN_HEADS_PER_STEP
    assert nm % hp == 0

    def body(q_ref, k_ref, kt_ref, v_ref, o_ref, do_ref, lse_ref, dq_ref, dk_ref, dv_ref, dqt_ref, dka_ref, dva_ref):
        i = pl.program_id(2)

        @pl.when(i == 0)
        def _():
            dqt_ref[...] = jnp.zeros_like(dqt_ref)

        dka_ref[...] = jnp.zeros_like(dka_ref)
        dva_ref[...] = jnp.zeros_like(dva_ref)
        ones8 = jnp.ones((8, HEAD), BF16)

        def step(c, mask):
            c0 = pl.multiple_of(c * tq, tq)
            for hh in range(hp):
                qcols, vcols = pl.ds(hh * QK_PAD, QK_PAD), pl.ds(hh * HEAD, HEAD)
                qs = q_ref[pl.ds(c0, tq), qcols]
                dos = do_ref[pl.ds(c0, tq), vcols]
                prod = dos.astype(F32) * o_ref[pl.ds(c0, tq), vcols].astype(F32)
                hi = _bf(prod)
                lo = _bf(prod - hi.astype(F32))
                delta8 = (lax.dot_general(ones8, hi, NT_DIMS, preferred_element_type=F32)
                          + lax.dot_general(ones8, lo, NT_DIMS, preferred_element_type=F32))
                st = lax.dot_general(k_ref[:, qcols], qs, NT_DIMS, preferred_element_type=F32) * scale
                pt = jnp.exp(st - lse_ref[0, hh, pl.ds(c, 1)][0])
                if mask is not None:
                    pt = jnp.where(_key_query_mask(i * tk, c * tq, tk, tq, mask == "causal"), pt, 0.0)
                dva_ref[hh] += jnp.dot(_bf(pt), dos, preferred_element_type=F32)
                dpt = lax.dot_general(v_ref[:, vcols], dos, NT_DIMS, preferred_element_type=F32)
                dst = _bf(pt * (dpt - jnp.tile(delta8, (tk // 8, 1))))
                dka_ref[hh] += jnp.dot(dst, qs, preferred_element_type=F32)
                dqt_ref[hh, pl.ds(c, 1)] += jnp.dot(kt_ref[0, hh, 0], dst, preferred_element_type=F32)[None]

        step(i, "causal")

        def rest_masked(c, carry):
            step(c, "pad")
            return carry

        def rest(c, carry):
            step(c, None)
            return carry

        @pl.when(i == 0)
        def _():
            lax.fori_loop(1, nq, rest_masked, 0)

        @pl.when(i > 0)
        def _():
            lax.fori_loop(i + 1, nq, rest, 0)

        for hh in range(hp):
            dk_ref[:, pl.ds(hh * QK_PAD, QK_PAD)] = (dka_ref[hh] * scale).astype(dk_ref.dtype)
            dv_ref[:, pl.ds(hh * HEAD, HEAD)] = dva_ref[hh].astype(dv_ref.dtype)

        @pl.when(i == nq - 1)
        def _():
            for hh in range(hp):
                for c in range(nq):
                    dq_ref[pl.ds(c * tq, tq), pl.ds(hh * QK_PAD, QK_PAD)] = (
                        jnp.transpose(dqt_ref[hh, c]) * scale).astype(dq_ref.dtype)

    return pl.pallas_call(
        body, name="attn_bwd", grid=(bl, nm // hp, nq),
        in_specs=[pl.BlockSpec((lp, hp * QK_PAD), lambda b, h, i: (b, h)),
                  pl.BlockSpec((tk, hp * QK_PAD), lambda b, h, i: (b * nq + i, h)),
                  pl.BlockSpec((1, hp, 1, QK_PAD, tk), lambda b, h, i: (b, h, i, 0, 0)),
                  pl.BlockSpec((tk, hp * HEAD), lambda b, h, i: (b * nq + i, h)),
                  pl.BlockSpec((lp, hp * HEAD), lambda b, h, i: (b, h)),
                  pl.BlockSpec((lp, hp * HEAD), lambda b, h, i: (b, h)),
                  pl.BlockSpec((1, hp, nq, 1, tq), lambda b, h, i: (b, h, 0, 0, 0))],
        out_specs=[pl.BlockSpec((lp, hp * QK_PAD), lambda b, h, i: (b, h)),
                   pl.BlockSpec((tk, hp * QK_PAD), lambda b, h, i: (b * nq + i, h)),
                   pl.BlockSpec((tk, hp * HEAD), lambda b, h, i: (b * nq + i, h))],
        out_shape=[jax.ShapeDtypeStruct((bl * lp, nm * QK_PAD), BF16),
                   jax.ShapeDtypeStruct((bl * lp, nm * QK_PAD), BF16),
                   jax.ShapeDtypeStruct((bl * lp, nm * HEAD), BF16)],
        scratch_shapes=[pltpu.VMEM((hp, nq, QK_PAD, tq), F32), pltpu.VMEM((hp, tk, QK_PAD), F32),
                        pltpu.VMEM((hp, tk, HEAD), F32)],
        compiler_params=pltpu.CompilerParams(dimension_semantics=("arbitrary", "arbitrary", "arbitrary")),
    )(q_cat, k_cat, k_t, v, o, do, lse)


def _place():
    return lax.axis_index("x"), lax.axis_index("y"), lax.axis_index("c")


def gather_shards(packed):
    hbm = pl.BlockSpec(memory_space=pl.ANY)

    def body(src_ref, out_ref, send_sems, recv_sems, local_sem):
        x, y, c = _place()
        me = 2 * x + y
        chips = [(1 - x, y), (x, 1 - y), (1 - x, 1 - y)]
        local = pltpu.make_async_copy(src_ref, out_ref.at[me], local_sem)
        local.start()
        sends = []
        for k, (px, py) in enumerate(chips):
            cp = pltpu.make_async_remote_copy(src_ref=src_ref, dst_ref=out_ref.at[me], send_sem=send_sems.at[k],
                                              recv_sem=recv_sems.at[k], device_id=(px, py, c), device_id_type=MESH)
            cp.start()
            sends.append(cp)
        for k, (px, py) in enumerate(chips):
            pltpu.make_async_remote_copy(src_ref=src_ref, dst_ref=out_ref.at[2 * px + py], send_sem=send_sems.at[k],
                                         recv_sem=recv_sems.at[k], device_id=(px, py, c),
                                         device_id_type=MESH).wait_recv()
        for cp in sends:
            cp.wait_send()
        local.wait()

    return pl.pallas_call(
        body, name="gather_shards", in_specs=[hbm], out_specs=hbm,
        out_shape=jax.ShapeDtypeStruct((4,) + packed.shape, packed.dtype),
        scratch_shapes=[pltpu.SemaphoreType.DMA((3,)), pltpu.SemaphoreType.DMA((3,)), pltpu.SemaphoreType.DMA],
    )(packed)


def gather_small(small):
    hbm = pl.BlockSpec(memory_space=pl.ANY)

    def body(small_ref, all_ref, send_sems, recv_sems, local_sem):
        x, y, c = _place()
        me = 4 * x + 2 * y + c
        local = pltpu.make_async_copy(small_ref, all_ref.at[me], local_sem)
        local.start()
        others = [(x ^ ((r >> 2) & 1), y ^ ((r >> 1) & 1), c ^ (r & 1)) for r in range(1, 8)]
        sends = []
        for r, peer in enumerate(others):
            cp = pltpu.make_async_remote_copy(src_ref=small_ref, dst_ref=all_ref.at[me], send_sem=send_sems.at[r],
                                              recv_sem=recv_sems.at[r], device_id=peer, device_id_type=MESH)
            cp.start()
            sends.append(cp)
        for r, (px, py, pc) in enumerate(others):
            pltpu.make_async_remote_copy(src_ref=small_ref, dst_ref=all_ref.at[4 * px + 2 * py + pc],
                                         send_sem=send_sems.at[r], recv_sem=recv_sems.at[r],
                                         device_id=(px, py, pc), device_id_type=MESH).wait_recv()
        for cp in sends:
            cp.wait_send()
        local.wait()

    return pl.pallas_call(
        body, name="gather_small", in_specs=[hbm], out_specs=hbm,
        out_shape=jax.ShapeDtypeStruct((8,) + small.shape, small.dtype),
        scratch_shapes=[pltpu.SemaphoreType.DMA((7,)), pltpu.SemaphoreType.DMA((7,)), pltpu.SemaphoreType.DMA],
    )(small)


def swap_with_sibling(name, parts):
    n = len(parts)
    hbm = pl.BlockSpec(memory_space=pl.ANY)

    def body(*refs):
        x, y, c = _place()
        cps = [pltpu.make_async_remote_copy(src_ref=refs[j], dst_ref=refs[n + j], send_sem=refs[2 * n].at[j],
                                            recv_sem=refs[2 * n + 1].at[j], device_id=(x, y, 1 - c),
                                            device_id_type=MESH) for j in range(n)]
        for cp in cps:
            cp.start()
        for cp in cps:
            cp.wait()

    return pl.pallas_call(
        body, name=name, in_specs=[hbm] * n, out_specs=[hbm] * n,
        out_shape=[jax.ShapeDtypeStruct(p.shape, p.dtype) for p in parts],
        scratch_shapes=[pltpu.SemaphoreType.DMA((n,)), pltpu.SemaphoreType.DMA((n,))],
    )(*parts)


def _chips3():
    x, y, c = _place()
    return [(1 - x, y, c), (x, 1 - y, c), (1 - x, 1 - y, c)]


def _push_copies(src_refs, land_refs, send_sems, recv_sems, per_chip):
    cps = []
    for j, (src_ref, land_ref) in enumerate(zip(src_refs, land_refs)):
        for k, (px, py, pc) in enumerate(_chips3()):
            part = src_ref.at[2 * px + py] if per_chip else src_ref
            cps.append(pltpu.make_async_remote_copy(
                src_ref=part, dst_ref=land_ref.at[k], send_sem=send_sems.at[3 * j + k],
                recv_sem=recv_sems.at[3 * j + k], device_id=(px, py, pc), device_id_type=MESH))
    return cps


def push_start(name, srcs, per_chip):
    n = len(srcs)
    hbm = pl.BlockSpec(memory_space=pltpu.HBM)
    sem = pl.BlockSpec(memory_space=pltpu.SEMAPHORE)
    lands = [lax.empty((3,) + s.shape[-2:], s.dtype) for s in srcs]

    def body(*refs):
        src_refs, land_refs = refs[:n], refs[n:2 * n]
        send_sems, recv_sems = refs[2 * n], refs[2 * n + 1]
        for cp in _push_copies(src_refs, land_refs, send_sems, recv_sems, per_chip):
            cp.start()
        refs[-1][...] = jnp.zeros_like(refs[-1])

    outs = pl.pallas_call(
        body, name=name,
        out_shape=(pltpu.SemaphoreType.DMA((3 * n,)), pltpu.SemaphoreType.DMA((3 * n,)),
                   *[pltpu.HBM(a.shape, a.dtype) for a in list(srcs) + lands], jax.ShapeDtypeStruct((8, HEAD), F32)),
        in_specs=(hbm,) * (2 * n),
        out_specs=(sem, sem) + (hbm,) * (2 * n) + (pl.BlockSpec(memory_space=pltpu.VMEM),),
        input_output_aliases={j: 2 + j for j in range(2 * n)},
        compiler_params=pltpu.CompilerParams(has_side_effects=pltpu.SideEffectType.DATAFLOW_SIDE_EFFECTING),
    )(*[pltpu.with_memory_space_constraint(a, pltpu.HBM) for a in list(srcs) + lands])
    return tuple(outs[:-1]), outs[-1]


def push_wait(name, handle, after, per_chip):
    send_sems, recv_sems = handle[0], handle[1]
    thru = handle[2:]
    n = len(thru) // 2
    hbm = pl.BlockSpec(memory_space=pltpu.HBM)
    sem = pl.BlockSpec(memory_space=pltpu.SEMAPHORE)

    def body(*refs):
        src_refs, land_refs = refs[:n], refs[n:2 * n]
        for cp in _push_copies(src_refs, land_refs, refs[2 * n], refs[2 * n + 1], per_chip):
            cp.wait_send()
            cp.wait_recv()

    outs = pl.pallas_call(
        body, name=name,
        out_shape=tuple(pltpu.HBM(a.shape, a.dtype) for a in thru),
        in_specs=(hbm,) * (2 * n) + (sem, sem, pl.BlockSpec(memory_space=pl.ANY)), out_specs=(hbm,) * (2 * n),
        input_output_aliases={j: j for j in range(2 * n)},
        compiler_params=pltpu.CompilerParams(has_side_effects=pltpu.SideEffectType.DATAFLOW_SIDE_EFFECTING),
    )(*thru, send_sems, recv_sems, after)
    return outs[:n], outs[n:]


def by_chip(own, landed, my_chip):
    by_rel = jnp.stack([own, landed[1], landed[0], landed[2]])
    return [lax.dynamic_index_in_dim(by_rel, jnp.bitwise_xor(s, my_chip), axis=0, keepdims=False) for s in range(4)]


def adamw(name, w, g_parts, m, v):
    r, c = w.shape
    tr = r if r * c <= 65536 else _tile(r, 128, 8)
    ng = len(g_parts)

    def body(*refs):
        w_ref, m_ref, v_ref = refs[0], refs[1 + ng], refs[2 + ng]
        g_ref, d_ref, nm_ref, nv_ref = refs[3 + ng:]
        gv = refs[1][...]
        for k in range(1, ng):
            gv = gv + refs[1 + k][...]
        m_new = ADAM_B1 * m_ref[...] + (1.0 - ADAM_B1) * gv
        v_new = ADAM_B2 * v_ref[...] + (1.0 - ADAM_B2) * (gv * gv)
        m_hat = m_new / (1.0 - ADAM_B1 ** ADAM_STEP)
        v_hat = v_new / (1.0 - ADAM_B2 ** ADAM_STEP)
        g_ref[...] = gv
        d_ref[...] = -ADAM_LR * (m_hat / (jnp.sqrt(v_hat) + ADAM_EPS) + ADAM_WD * w_ref[...])
        nm_ref[...] = m_new
        nv_ref[...] = v_new

    spec = pl.BlockSpec((tr, c), lambda i: (i, 0))
    return pl.pallas_call(
        body, name=name, grid=(r // tr,), in_specs=[spec] * (3 + ng), out_specs=[spec] * 4,
        out_shape=[jax.ShapeDtypeStruct((r, c), F32)] * 4,
        compiler_params=pltpu.CompilerParams(dimension_semantics=("arbitrary",)),
    )(w, *g_parts, m, v)


def split_full(name, full, s):
    if name in COL_SHARDED:
        c = full.shape[1] // 4
        return full[:, s * c:(s + 1) * c]
    r = full.shape[0] // 4
    return full[s * r:(s + 1) * r]


def join_shards(name, shards):
    return jnp.concatenate(shards, axis=1 if name in COL_SHARDED else 0)


def kernel(x, meta_tokens, w_in, b_gate, lb_logits, hg_norm_g, w_hg_o, q_a_norm_g, w_q_b, kv_a_norm_g, w_kv_b, w_mla_o, w_out, mix_pre_g, mix_post_g, ffn_pre_g, ffn_post_g, w_ffn_in, w_ffn_out, loss_target, m_meta_tokens, m_w_in, m_b_gate, m_lb_logits, m_hg_norm_g, m_w_hg_o, m_q_a_norm_g, m_w_q_b, m_kv_a_norm_g, m_w_kv_b, m_w_mla_o, m_w_out, m_mix_pre_g, m_mix_post_g, m_ffn_pre_g, m_ffn_post_g, m_w_ffn_in, m_w_ffn_out, v_meta_tokens, v_w_in, v_b_gate, v_lb_logits, v_hg_norm_g, v_w_hg_o, v_q_a_norm_g, v_w_q_b, v_kv_a_norm_g, v_w_kv_b, v_w_mla_o, v_w_out, v_mix_pre_g, v_mix_post_g, v_ffn_pre_g, v_ffn_post_g, v_w_ffn_in, v_w_ffn_out):
    wts = dict(meta_tokens=meta_tokens, w_in=w_in[0], b_gate=b_gate, lb_logits=lb_logits, hg_norm_g=hg_norm_g,
               w_hg_o=w_hg_o[0], q_a_norm_g=q_a_norm_g, w_q_b=w_q_b[0], kv_a_norm_g=kv_a_norm_g, w_kv_b=w_kv_b[0],
               w_mla_o=w_mla_o[0], w_out=w_out[0], mix_pre_g=mix_pre_g, mix_post_g=mix_post_g, ffn_pre_g=ffn_pre_g,
               ffn_post_g=ffn_post_g, w_ffn_in=w_ffn_in[0], w_ffn_out=w_ffn_out[0])
    mom_m = dict(meta_tokens=m_meta_tokens, w_in=m_w_in[0], b_gate=m_b_gate, lb_logits=m_lb_logits,
                 hg_norm_g=m_hg_norm_g, w_hg_o=m_w_hg_o[0], q_a_norm_g=m_q_a_norm_g, w_q_b=m_w_q_b[0],
                 kv_a_norm_g=m_kv_a_norm_g, w_kv_b=m_w_kv_b[0], w_mla_o=m_w_mla_o[0], w_out=m_w_out[0],
                 mix_pre_g=m_mix_pre_g, mix_post_g=m_mix_post_g, ffn_pre_g=m_ffn_pre_g, ffn_post_g=m_ffn_post_g,
                 w_ffn_in=m_w_ffn_in[0], w_ffn_out=m_w_ffn_out[0])
    mom_v = dict(meta_tokens=v_meta_tokens, w_in=v_w_in[0], b_gate=v_b_gate, lb_logits=v_lb_logits,
                 hg_norm_g=v_hg_norm_g, w_hg_o=v_w_hg_o[0], q_a_norm_g=v_q_a_norm_g, w_q_b=v_w_q_b[0],
                 kv_a_norm_g=v_kv_a_norm_g, w_kv_b=v_w_kv_b[0], w_mla_o=v_w_mla_o[0], w_out=v_w_out[0],
                 mix_pre_g=v_mix_pre_g, mix_post_g=v_mix_post_g, ffn_pre_g=v_ffn_pre_g, ffn_post_g=v_ffn_post_g,
                 w_ffn_in=v_w_ffn_in[0], w_ffn_out=v_w_ffn_out[0])

    bl, seq, d = x.shape
    lp = PAD_FRONT + N_META + seq
    t_rows = bl * lp
    nh = d // HEAD
    ql, kvl = wts["w_q_b"].shape[0], wts["w_kv_b"].shape[0]
    nm = (4 * wts["w_mla_o"].shape[0]) // HEAD
    ffn = 4 * wts["w_ffn_out"].shape[0]
    mla_w = ql + kvl + HEAD
    assert ql == kvl and ql % HEAD == 0 and seq % SEQ_BLOCK == 0 and d % HEAD == 0
    scale = (HEAD + ROPE) ** -0.5
    my_chip = 2 * lax.axis_index("x") + lax.axis_index("y")

    mcols = meta_tokens.shape[1]
    meta_all = gather_shards(meta_tokens)
    meta_full = jnp.concatenate([meta_all[s] for s in range(4)], axis=1)

    def start_gather(name, names, order_after):
        srcs = [_bf(wts[n]) for n in names]
        if order_after is not None:
            srcs[0] = srcs[0] + order_after[0, 0].astype(BF16)
        return push_start(name, srcs, per_chip=False)

    def finish_gather(name, names, started, after):
        owns, landed = push_wait(name, started[0], after, per_chip=False)
        return {n: join_shards(n, by_chip(own, land, my_chip)) for n, own, land in zip(names, owns, landed)}

    rest_names = tuple(n for n in BIG if n != "w_in")
    my_c = lax.axis_index("c")
    w_in_bf = _bf(wts["w_in"])
    half = w_in_bf.shape[0] // 2
    own_half = (lax.dynamic_slice_in_dim(w_in_bf, my_c * half, half, axis=0)
                + (meta_all[0, :1, :1] * 0.0)[0, 0].astype(BF16))
    gather_1 = push_start("gather_w_in_start", [own_half], per_chip=False)
    gather_2 = start_gather("gather_rest_start", rest_names, gather_1[1])

    h0 = jnp.concatenate([jnp.zeros((bl, PAD_FRONT, d), F32), jnp.broadcast_to(meta_full[None], (bl, N_META, d)), x],
                         axis=1).reshape(t_rows, d)
    tiles_seq, tiles_real = lp // SEQ_BLOCK, seq // SEQ_BLOCK
    assert PAD_FRONT + N_META == SEQ_BLOCK

    def real_block(i):
        return (i // tiles_seq) * tiles_real + jnp.maximum(i % tiles_seq - 1, 0)

    meta_rows = jnp.broadcast_to(((jnp.arange(lp) >= PAD_FRONT) & (jnp.arange(lp) < PAD_FRONT + N_META)
                                  ).astype(F32)[:, None], (lp, HEAD))
    pos = (jnp.arange(lp, dtype=jnp.int32) - PAD_FRONT).astype(F32)
    inv_freq = 1.0 / (ROPE_THETA ** (jnp.arange(0, ROPE, 2, dtype=F32) / ROPE))
    ang = pos[:, None] * inv_freq[None, :]
    zeros32 = jnp.zeros((lp, ROPE_HALF), F32)
    zeros64 = jnp.zeros((lp, HEAD - ROPE), F32)
    t_cos = jnp.concatenate([jnp.cos(ang), jnp.cos(ang), zeros64], axis=1)
    t_up = jnp.concatenate([zeros32, jnp.sin(ang), zeros64], axis=1)
    t_dn = jnp.concatenate([-jnp.sin(ang), zeros32, zeros64], axis=1)
    real = jnp.broadcast_to((jnp.arange(lp) >= PAD_FRONT + N_META).astype(F32)[:, None], (lp, d))
    lb_soft = jax.nn.softmax(lb_logits.astype(F32), axis=0)
    lb = lb_soft[0:1]

    (u1,) = rowwise("norm_mix_pre", lambda h, g: _rms(h, g), [(h0, d, 0)], [], [mix_pre_g + gather_2[1][0, 0]],
                    [(d, BF16)])
    _, (fetched,) = push_wait("gather_w_in_wait", gather_1[0], u1, per_chip=False)
    (handed,) = swap_with_sibling("swap_w_in", [fetched])
    halves = jnp.stack([fetched, handed])
    remote = jnp.concatenate([lax.dynamic_index_in_dim(halves, my_c, 0, keepdims=False),
                              lax.dynamic_index_in_dim(halves, 1 - my_c, 0, keepdims=False)], axis=1)
    full = {"w_in": join_shards("w_in", by_chip(w_in_bf, remote, my_chip))}
    w_main = jnp.concatenate([full["w_in"][:, :4 * d], full["w_in"][:, -2 * d:]], axis=1)
    w_mla = jnp.pad(full["w_in"][:, 4 * d:4 * d + ql + kvl + ROPE], ((0, 0), (0, HEAD - ROPE)))
    proj_main = matmul("proj_main", u1, w_main, "nn", out_dtype=BF16)
    proj_mla = matmul("proj_mla", u1, w_mla, "nn", out_dtype=BF16)
    hg_consts = _hg_constants()
    o_scan, states, a_mats = hgrn_fwd(proj_main, lb, hg_consts, bl, lp, d)

    def hg_out_fn(o, hg, g):
        return jnp.concatenate([_rms(o[:, h * HEAD:(h + 1) * HEAD], g) for h in range(nh)], axis=1) * _silu(hg)

    (o_hg,) = rowwise("hgrn_out", hg_out_fn, [(o_scan, d, 0), (proj_main, d, 3)], [], [hg_norm_g], [(d, BF16)])
    full.update(finish_gather("gather_rest_wait", rest_names, gather_2, o_hg))
    w_qb = jnp.pad(full["w_q_b"].reshape(ql, nm, HEAD + ROPE), ((0, 0), (0, 0), (0, QK_PAD - HEAD - ROPE))
                   ).reshape(ql, nm * QK_PAD)
    w_kvb = full["w_kv_b"]
    y_a = matmul("y_a", o_hg, _bf(full["w_hg_o"]), "nn", out_dtype=BF16)

    qn, kvn = rowwise("mla_norms", lambda cq, ckv, gq, gk: (_rms(cq, gq), _rms(ckv, gk)),
                      [(proj_mla, ql, 0), (proj_mla, kvl, 1)], [], [q_a_norm_g, kv_a_norm_g],
                      [(ql, BF16), (kvl, BF16)])
    q_full = matmul("q_up", qn, w_qb, "nn", out_dtype=BF16)
    kv_full = matmul("kv_up", kvn, w_kvb, "nn", out_dtype=BF16)

    def mla_prep_fn(qf, kvf, kpe, cos, s_up, s_dn):
        kpe_r = _rope(kpe, cos, s_up, s_dn)
        qs, ks, vs = [], [], []
        for h in range(nm):
            qs += [qf[:, h * QK_PAD:h * QK_PAD + HEAD], _rope(qf[:, h * QK_PAD + HEAD:(h + 1) * QK_PAD], cos, s_up, s_dn)]
            ks += [kvf[:, h * QK_PAD:h * QK_PAD + HEAD], kpe_r]
            vs += [kvf[:, h * QK_PAD + HEAD:(h + 1) * QK_PAD]]
        return jnp.concatenate(qs, axis=1), jnp.concatenate(ks, axis=1), jnp.concatenate(vs, axis=1)

    kpe_blk = (ql + kvl) // HEAD
    q_cat, k_cat, v_att = rowwise("mla_prep", mla_prep_fn,
                                  [(q_full, nm * QK_PAD, 0), (kv_full, nm * QK_PAD, 0), (proj_mla, HEAD, kpe_blk)],
                                  [t_cos, t_up, t_dn], [], [(nm * QK_PAD, BF16), (nm * QK_PAD, BF16), (nm * HEAD, BF16)])
    at = _attn_tile(lp)
    v_t = v_att.reshape(bl, lp // at, at, nm, HEAD).transpose(0, 3, 1, 4, 2)
    k_t = k_cat.reshape(bl, lp // at, at, nm, QK_PAD).transpose(0, 3, 1, 4, 2)
    o_mla, lse = attn_fwd_t(q_cat, k_cat, v_t, bl, lp, nm, scale)
    y_b = matmul("y_b", o_mla, _bf(full["w_mla_o"]), "nn", out_dtype=BF16)

    def gate_fn(ya, yb, ga, gb, bias):
        zv = _sigmoid(ga + bias[:, :d]) * ya + _sigmoid(gb + bias[:, d:]) * yb
        return zv, zv

    mixed, z = matmul_fused("gate_mix_out", gate_fn,
                            [(y_a, d, 0), (y_b, d, 0), (proj_main, d, 4), (proj_main, d, 5)], [b_gate],
                            _bf(full["w_out"]), [(0, d)], "nn", [(d, BF16)], tm=512)

    def mid_fn(h, mx, g_post, g_pre):
        h1 = h + _rms(mx, g_post)
        return h1, _rms(h1, g_pre)

    h1, u2 = rowwise("norm_mid", mid_fn, [(h0, d, 0), (mixed, d, 0)], [], [mix_post_g, ffn_pre_g],
                     [(d, F32), (d, BF16)])
    gu = matmul("ffn_in", u2, _bf(full["w_ffn_in"]), "nn", out_dtype=BF16)
    def swiglu_fn(gt, up):
        a = _silu(gt) * up
        return a, a

    f_out, act = matmul_fused("swiglu_ffn_out", swiglu_fn, [(gu, ffn, 0), (gu, ffn, 1)], [],
                              _bf(full["w_ffn_out"]), [(0, ffn)], "nn", [(ffn, BF16)])

    def loss_fn(h1v, fv, tg, realv, g_post):
        h2 = h1v + _rms(fv, g_post)
        diff = (h2 - tg) * realv
        part = jnp.broadcast_to(0.5 * jnp.sum(diff * diff, keepdims=True) / d, (1, HEAD))
        dy = diff / d
        df, dg = _rms_bwd(fv, g_post, dy)
        return dy, df, part, dg

    dy, df, loss_part, g_ffn_post = rowwise(
        "loss_head", loss_fn,
        [(h1, d, 0), (f_out, d, 0), (loss_target.reshape(bl * seq, d), d, 0, real_block)], [real],
        [ffn_post_g], [(d, BF16), (d, BF16)], [(1, HEAD), (1, d)])
    grads = {}
    d_act = matmul("d_act", df, _bf(full["w_ffn_out"]), "nt", out_dtype=BF16)
    grads["w_ffn_out"] = matmul("gw_ffn_out", act, df, "tn")

    def swiglu_bwd_fn(gt, up, da):
        dgt, dup = da * up * _silu_grad(gt), da * _silu(gt)
        return dgt, dup, jnp.concatenate([dgt, dup], axis=1)

    du2, dgu = matmul_fused("swiglu_bwd_d_u2", swiglu_bwd_fn, [(gu, ffn, 0), (gu, ffn, 1), (d_act, ffn, 0)], [],
                            _bf(full["w_ffn_in"]), [(0, ffn), (ffn, 2 * ffn)], "nt", [(2 * ffn, BF16)])
    grads["w_ffn_in"] = matmul("gw_ffn_in", u2, dgu, "tn")

    def mid_bwd_fn(dyv, h1v, du2v, mx, g_pre, g_post):
        dx, dg_pre = _rms_bwd(h1v, g_pre, du2v)
        dh1 = dyv + dx
        dmx, dg_post = _rms_bwd(mx, g_post, dh1)
        return dh1, dmx, dg_pre, dg_post

    dh1, dmixed, g_ffn_pre, g_mix_post = rowwise("norm_mid_bwd", mid_bwd_fn,
                                                 [(dy, d, 0), (h1, d, 0), (du2, d, 0), (mixed, d, 0)], [],
                                                 [ffn_pre_g, mix_post_g], [(d, BF16), (d, BF16)], [(1, d), (1, d)])
    dz = matmul("d_z", dmixed, _bf(full["w_out"]), "nt", out_dtype=BF16)
    grads["w_out"] = matmul("gw_out", z, dmixed, "tn")

    def gate_bwd_fn(dzv, ya, yb, ga, gb, bias):
        sa, sb = _sigmoid(ga + bias[:, :d]), _sigmoid(gb + bias[:, d:])
        dga = dzv * ya * sa * (1.0 - sa)
        dgb = dzv * yb * sb * (1.0 - sb)
        dgates = jnp.concatenate([dga, dgb], axis=1)
        return dzv * sa, dzv * sb, dgates, jnp.sum(dgates, axis=0, keepdims=True)

    dy_a, dy_b, dgates, g_b_gate = rowwise("gate_mix_bwd", gate_bwd_fn,
                                           [(dz, d, 0), (y_a, d, 0), (y_b, d, 0), (proj_main, d, 4), (proj_main, d, 5)],
                                           [], [b_gate], [(d, BF16), (d, BF16), (2 * d, BF16)], [(1, 2 * d)])
    do_hg = matmul("d_o_hg", dy_a, _bf(full["w_hg_o"]), "nt", out_dtype=BF16)
    grads["w_hg_o"] = matmul("gw_hg_o", o_hg, dy_a, "tn")
    do_mla = matmul("d_o_mla", dy_b, _bf(full["w_mla_o"]), "nt", out_dtype=BF16)
    grads["w_mla_o"] = matmul("gw_mla_o", o_mla, dy_b, "tn")

    early = ("w_hg_o", "w_mla_o", "w_out", "w_ffn_in", "w_ffn_out")
    late = ("w_in", "w_q_b", "w_kv_b")

    def start_grads(name, names):
        sends = [_bf(jnp.stack([split_full(n, grads[n], s) for s in range(4)])) for n in names]
        mines = []
        for n in names:
            r, c = wts[n].shape
            axis, size = (1, c) if n in COL_SHARDED else (0, r)
            mines.append(lax.dynamic_slice_in_dim(grads[n], my_chip * size, size, axis=axis))
        handle, token = push_start(name, sends, per_chip=True)
        return handle, token, mines

    def finish_grads(tag, names, started, after):
        handle, _, mines = started
        _, landed = push_wait(f"grads_{tag}_wait", handle, after, per_chip=True)
        parts = []
        for n, mine, land in zip(names, mines, landed):
            r, c = mine.shape
            tr = _tile(r, 256, 16)
            land2 = land.reshape(3 * r, c)
            parts.append(rowwise(f"sum_chips_{n}", lambda a, r0, r1, r2: a + r0 + r1 + r2,
                                 [(mine, c, 0)] + [(land2, c, 0, k * (r // tr)) for k in range(3)],
                                 [], [], [(c, F32)], tm=tr)[0])
        sibs = swap_with_sibling(f"swap_{tag}", parts)
        return {n: [p, s] for n, p, s in zip(names, parts, sibs)}

    grads_early = start_grads("grads_early_start", early)
    token_a = grads_early[1]

    def hg_out_bwd_fn(do, o, hg, g):
        sg = _silu(hg)
        dn = do * sg
        dos, dgs, ons = [], 0.0, []
        for h in range(nh):
            sl = slice(h * HEAD, (h + 1) * HEAD)
            dx, dg = _rms_bwd(o[:, sl], g, dn[:, sl])
            dos.append(dx)
            dgs = dgs + dg
            ons.append(_rms(o[:, sl], g))
        dhg = do * jnp.concatenate(ons, axis=1) * _silu_grad(hg)
        return jnp.concatenate(dos, axis=1), dhg, dgs

    do_scan, dhg, g_hg_norm = rowwise("hgrn_out_bwd", hg_out_bwd_fn, [(do_hg, d, 0), (o_scan, d, 0), (proj_main, d, 3)],
                                      [], [hg_norm_g], [(d, BF16), (d, BF16)], [(1, HEAD)])
    dhq, dhf, dhi, g_lb = hgrn_bwd(proj_main, lb + token_a[0, 0], hg_consts, states, a_mats, do_scan, bl, lp, d)

    dq_cat, dk_cat, dv_att = attn_bwd_t(q_cat, k_cat, k_t, v_att, o_mla, do_mla, lse, bl, lp, nm, scale)

    def mla_prep_bwd_fn(dqc, dkc, dvv, cos, s_up, s_dn):
        dqs, dkvs, dkpe = [], [], 0.0
        for h in range(nm):
            dqs += [dqc[:, h * QK_PAD:h * QK_PAD + HEAD],
                    _rope_bwd(dqc[:, h * QK_PAD + HEAD:(h + 1) * QK_PAD], cos, s_up, s_dn)]
            dkvs += [dkc[:, h * QK_PAD:h * QK_PAD + HEAD], dvv[:, h * HEAD:(h + 1) * HEAD]]
            dkpe = dkpe + dkc[:, h * QK_PAD + HEAD:(h + 1) * QK_PAD]
        return jnp.concatenate(dqs, axis=1), jnp.concatenate(dkvs, axis=1), _rope_bwd(dkpe, cos, s_up, s_dn)

    dq_full, dkv_full, dkpe = rowwise("mla_prep_bwd", mla_prep_bwd_fn,
                                      [(dq_cat, nm * QK_PAD, 0), (dk_cat, nm * QK_PAD, 0), (dv_att, nm * HEAD, 0)],
                                      [t_cos, t_up, t_dn], [],
                                      [(nm * QK_PAD, BF16), (nm * QK_PAD, BF16), (HEAD, F32)])
    dqn = matmul("d_qn", dq_full, w_qb, "nt", out_dtype=BF16)
    g_wqb = matmul("gw_q_b", qn, dq_full, "tn")
    grads["w_q_b"] = g_wqb.reshape(ql, nm, QK_PAD)[:, :, :HEAD + ROPE].reshape(ql, nm * (HEAD + ROPE))
    dkvn = matmul("d_kvn", dkv_full, w_kvb, "nt", out_dtype=BF16)
    grads["w_kv_b"] = matmul("gw_kv_b", kvn, dkv_full, "tn")

    def mla_norms_bwd_fn(dqnv, dkvnv, cq, ckv, dkpev, gq, gk):
        dcq, dgq = _rms_bwd(cq, gq, dqnv)
        dckv, dgk = _rms_bwd(ckv, gk, dkvnv)
        return jnp.concatenate([dcq, dckv, dkpev], axis=1), dgq, dgk

    dmla, g_q_norm, g_kv_norm = rowwise("mla_norms_bwd", mla_norms_bwd_fn,
                                        [(dqn, ql, 0), (dkvn, kvl, 0), (proj_mla, ql, 0), (proj_mla, kvl, 1),
                                         (dkpe, HEAD, 0)], [], [q_a_norm_g, kv_a_norm_g],
                                        [(mla_w, BF16)], [(1, ql), (1, kvl)])

    d_pieces = [dhq, dhf, dhi, dhg, dgates, dmla]
    gw_parts = [matmul(f"gw_in_{k}", u1, dp, "tn") for k, dp in enumerate(d_pieces)]
    grads["w_in"] = jnp.concatenate(gw_parts[:4] + [gw_parts[5][:, :ql + kvl + ROPE], gw_parts[4]], axis=1)
    grads_late = start_grads("grads_late_start", late)
    w_mla_after = w_mla + grads_late[1][0, 0].astype(BF16)
    w_pieces = [w_main[:, 0:d], w_main[:, d:2 * d], w_main[:, 2 * d:3 * d], w_main[:, 3 * d:4 * d],
                w_main[:, 4 * d:6 * d], w_mla_after]
    du1 = matmul("d_u1", d_pieces, w_pieces, "nt", out_dtype=BF16)

    def first_bwd_fn(dh1v, h, du1v, is_meta, g):
        dx, dg = _rms_bwd(h, g, du1v)
        dh0v = dh1v + dx
        return dh0v, dg, dh0v * jnp.tile(is_meta, (1, d // HEAD))

    grad_x, g_mix_pre, meta_tile = rowwise(
        "norm_mix_pre_bwd", first_bwd_fn, [(dh1, d, 0), (h0, d, 0), (du1, d, 0)], [meta_rows], [mix_pre_g],
        [(d, F32, bl * seq, real_block)], [(1, d), (SEQ_BLOCK, d)])
    grad_x = grad_x.reshape(bl, seq, d)

    g_parts = finish_grads("early", early, grads_early, g_mix_pre)
    updates = {}

    def update(n, parts):
        w2 = wts[n].reshape(-1, wts[n].shape[-1])
        updates[n] = adamw("adamw_" + n, w2, [p.reshape(w2.shape) for p in parts], mom_m[n].reshape(w2.shape),
                           mom_v[n].reshape(w2.shape))

    for n in early:
        update(n, g_parts[n])
    g_parts = finish_grads("late", late, grads_late, updates[early[-1]][0])
    for n in late:
        update(n, g_parts[n])
    p0 = lb_soft[0:1]
    g_lb_logits = jnp.concatenate([g_lb * p0 * (1.0 - p0), -g_lb * p0 * (1.0 - p0)], axis=0)

    def row_of(vec):
        return vec.reshape(-1, d) if vec.size >= d else jnp.pad(vec.reshape(1, -1), ((0, 0), (0, d - vec.size)))

    small_parts = dict(b_gate=g_b_gate, lb_logits=g_lb_logits, hg_norm_g=g_hg_norm, q_a_norm_g=g_q_norm,
                       kv_a_norm_g=g_kv_norm, mix_pre_g=g_mix_pre, mix_post_g=g_mix_post, ffn_pre_g=g_ffn_pre,
                       ffn_post_g=g_ffn_post)
    g_meta = meta_tile[PAD_FRONT:PAD_FRONT + N_META]
    small_rows = [row_of(small_parts[n]) for n in SMALL] + [row_of(g_meta)]
    n_small = sum(r.shape[0] for r in small_rows)
    small = jnp.pad(jnp.concatenate(small_rows, axis=0), ((0, -(-n_small // 8) * 8 - n_small), (0, 0)))
    all_small = gather_small(small)
    small_t = small.shape[0]

    def sum8_fn(*slabs):
        acc = slabs[0]
        for s in slabs[1:]:
            acc = acc + s
        return acc

    (g_small,) = rowwise("sum_small", sum8_fn, [(all_small.reshape(8 * small_t, d), d, 0, k) for k in range(8)],
                         [], [], [(d, F32)], tm=small_t, n_rows=small_t)

    off = 0
    for n, part in zip(SMALL, small_rows[:-1]):
        rows = part.shape[0]
        update(n, [g_small[off:off + rows, :d].reshape(-1)[:wts[n].size]])
        off += rows
    update("meta_tokens", [lax.dynamic_slice_in_dim(g_small[off:off + N_META, :d], my_chip * mcols, mcols, axis=1)])

    loss = lax.psum(loss_part[0, 0], ("x", "y", "c"))

    def shaped(n, a):
        return a.reshape((1,) + wts[n].shape) if n in BIG else a.reshape(wts[n].shape)

    return (loss, grad_x, *[shaped(n, updates[n][k]) for k in range(4) for n in WEIGHTS])
```

```python
import functools
import math

import jax
import jax.numpy as jnp
from jax import lax
from jax.experimental import pallas as pl
from jax.experimental.pallas import tpu as pltpu

F32 = jnp.float32
BF16 = jnp.bfloat16
MESH = pl.DeviceIdType.MESH

N_META = 16
NORM_EPS = 1e-6
HEAD = 128
ROPE = 64
ROPE_HALF = ROPE // 2
QK_PAD = 2 * HEAD
ROPE_THETA = 10000.0
SEQ_BLOCK = 256
PAD_FRONT = SEQ_BLOCK - N_META
NEG = -1e30
VMEM_LIMIT = 56 * 1024 * 1024
ATTN_HEADS_PER_STEP = 1
ATTN_TILE_MAX = 768

ADAM_LR, ADAM_B1, ADAM_B2, ADAM_EPS, ADAM_WD, ADAM_STEP = 0.001, 0.9, 0.999, 1e-08, 0.01, 10

BIG = ("w_in", "w_hg_o", "w_q_b", "w_kv_b", "w_mla_o", "w_out", "w_ffn_in", "w_ffn_out")
COL_SHARDED = ("w_in", "w_q_b", "w_kv_b", "w_ffn_in")
SMALL = ("b_gate", "lb_logits", "hg_norm_g", "q_a_norm_g", "kv_a_norm_g", "mix_pre_g", "mix_post_g",
         "ffn_pre_g", "ffn_post_g")
WEIGHTS = ("meta_tokens", "w_in", "b_gate", "lb_logits", "hg_norm_g", "w_hg_o", "q_a_norm_g", "w_q_b",
           "kv_a_norm_g", "w_kv_b", "w_mla_o", "w_out", "mix_pre_g", "mix_post_g", "ffn_pre_g", "ffn_post_g",
           "w_ffn_in", "w_ffn_out")


def _tile(n, cap, unit=128):
    if n <= cap:
        return n
    best = None
    for t in range(unit, cap + 1, unit):
        if n % t == 0:
            best = t
    assert best is not None, (n, cap, unit)
    return best


def _sigmoid(x):
    return 1.0 / (1.0 + jnp.exp(-x))


def _bf(x):
    return x.astype(BF16)


def rowwise(name, fn, row_ins, seq_tabs, consts, row_outs, acc_outs=(), tm=SEQ_BLOCK, n_rows=None):
    t_rows = row_ins[0][0].shape[0] if n_rows is None else n_rows
    nt = t_rows // tm
    assert t_rows % tm == 0
    n_in = len(row_ins) + len(seq_tabs) + len(consts)
    n_row = len(row_outs)

    def body(*refs):
        vals = [r[...].astype(F32) for r in refs[:n_in]]
        res = fn(*vals)
        if not isinstance(res, (tuple, list)):
            res = (res,)
        outs = refs[n_in:]
        for k in range(n_row):
            outs[k][...] = res[k].astype(outs[k].dtype)
        if acc_outs:
            @pl.when(pl.program_id(0) == 0)
            def _():
                for k in range(len(acc_outs)):
                    outs[n_row + k][...] = jnp.zeros_like(outs[n_row + k])

            for k in range(len(acc_outs)):
                outs[n_row + k][...] += res[n_row + k]

    row_ins = [tuple(e) + (0,) * (4 - len(e)) for e in row_ins]
    in_specs = [pl.BlockSpec((tm, w), functools.partial(lambda i, j, ro: (ro(i) if callable(ro) else i + ro, j),
                                                        j=j, ro=ro)) for (_, w, j, ro) in row_ins]
    for tab in seq_tabs:
        per = tab.shape[0] // tm
        in_specs.append(pl.BlockSpec((tm, tab.shape[1]), functools.partial(lambda i, per: (i % per, 0), per=per)))
    for c in consts:
        in_specs.append(pl.BlockSpec(c.shape, lambda i: (0, 0)))
    row_outs = [tuple(e) + (t_rows, None)[len(e) - 2:] for e in row_outs]
    out_specs = [pl.BlockSpec((tm, w), functools.partial(lambda i, rm: (i if rm is None else rm(i), 0), rm=rm))
                 for (w, _, _, rm) in row_outs]
    out_specs += [pl.BlockSpec(s, lambda i: (0, 0)) for s in acc_outs]
    out_shape = [jax.ShapeDtypeStruct((rows, w), dt) for (w, dt, rows, _) in row_outs]
    out_shape += [jax.ShapeDtypeStruct(s, F32) for s in acc_outs]
    res = pl.pallas_call(
        body, name=name, grid=(nt,), in_specs=in_specs, out_specs=out_specs, out_shape=out_shape,
        compiler_params=pltpu.CompilerParams(dimension_semantics=("arbitrary",)),
    )(*[e[0] for e in row_ins], *seq_tabs, *consts)
    return res


def matmul(name, a, b, mode, out_dtype=F32):
    if mode != "tn":
        return _matmul_resident(name, a if isinstance(a, (list, tuple)) else [a],
                                b if isinstance(b, (list, tuple)) else [b], mode, out_dtype)
    kdim, m = a.shape
    n = b.shape[1]
    tn = _tile(n, 1536)
    tm, tk = _tile(m, 1408 if tn <= 1024 else 1024), _tile(kdim, 1024)
    nk = kdim // tk

    def body(a_ref, b_ref, o_ref, acc_ref):
        k = pl.program_id(2)

        @pl.when(k == 0)
        def _():
            acc_ref[...] = jnp.zeros_like(acc_ref)

        acc_ref[...] += lax.dot_general(a_ref[...], b_ref[...], TN_DIMS, preferred_element_type=F32)

        @pl.when(k == nk - 1)
        def _():
            o_ref[...] = acc_ref[...].astype(o_ref.dtype)

    return pl.pallas_call(
        body, name=name, grid=(m // tm, n // tn, nk),
        in_specs=[pl.BlockSpec((tk, tm), lambda i, j, k: (k, i)), pl.BlockSpec((tk, tn), lambda i, j, k: (k, j))],
        out_specs=pl.BlockSpec((tm, tn), lambda i, j, k: (i, j)),
        out_shape=jax.ShapeDtypeStruct((m, n), out_dtype),
        scratch_shapes=[pltpu.VMEM((tm, tn), F32)],
        compiler_params=pltpu.CompilerParams(dimension_semantics=("arbitrary", "arbitrary", "arbitrary"),
                                             vmem_limit_bytes=VMEM_LIMIT),
    )(a, b)


def _matmul_resident(name, a_list, b_list, mode, out_dtype):
    m = a_list[0].shape[0]
    n = b_list[0].shape[1] if mode == "nn" else b_list[0].shape[0]
    k_total = sum(a.shape[1] for a in a_list)
    out_bytes = 2 if out_dtype == BF16 else 4
    budget = VMEM_LIMIT - 4 * k_total * n - (6 << 20)
    tm = 1024
    while tm > 128 and 2 * tm * (2 * k_total + out_bytes * n) > budget:
        tm //= 2
    tm = _tile(m, tm)
    cn = _tile(n, 1024)
    npairs = len(a_list)

    def body(*refs):
        a_refs, b_refs, o_ref = refs[:npairs], refs[npairs:2 * npairs], refs[2 * npairs]
        for c in range(n // cn):
            acc = None
            for a_ref, b_ref in zip(a_refs, b_refs):
                if mode == "nn":
                    part = jnp.dot(a_ref[...], b_ref[:, pl.ds(c * cn, cn)], preferred_element_type=F32)
                else:
                    part = lax.dot_general(a_ref[...], b_ref[pl.ds(c * cn, cn), :], NT_DIMS,
                                           preferred_element_type=F32)
                acc = part if acc is None else acc + part
            o_ref[:, pl.ds(c * cn, cn)] = acc.astype(o_ref.dtype)

    in_specs = [pl.BlockSpec((tm, a.shape[1]), lambda i: (i, 0)) for a in a_list]
    in_specs += [pl.BlockSpec(b.shape, lambda i: (0, 0)) for b in b_list]
    return pl.pallas_call(
        body, name=name, grid=(m // tm,), in_specs=in_specs,
        out_specs=pl.BlockSpec((tm, n), lambda i: (i, 0)),
        out_shape=jax.ShapeDtypeStruct((m, n), out_dtype),
        compiler_params=pltpu.CompilerParams(dimension_semantics=("arbitrary",), vmem_limit_bytes=VMEM_LIMIT),
    )(*a_list, *b_list)


def matmul_fused(name, fn, row_ins, consts, weight, pieces, mode, extra_outs, out_dtype=BF16, tm=256):
    row_ins = [tuple(e) + (0,) * (4 - len(e)) for e in row_ins]
    t_rows = row_ins[0][0].shape[0]
    tm = _tile(t_rows, tm)
    n = weight.shape[1] if mode == "nn" else weight.shape[0]
    n_in = len(row_ins) + len(consts)
    n_parts = len(pieces)

    def body(*refs):
        w_hbm = refs[n_in]
        outs = refs[n_in + 1:n_in + 2 + len(extra_outs)]
        w_ref, sem = refs[-2], refs[-1]

        @pl.when(pl.program_id(0) == 0)
        def _():
            cp = pltpu.make_async_copy(w_hbm, w_ref, sem)
            cp.start()
            cp.wait()

        res = fn(*[r[...].astype(F32) for r in refs[:n_in]])
        acc = None
        for a_p, (k0, k1) in zip(res[:n_parts], pieces):
            if mode == "nn":
                part = jnp.dot(_bf(a_p), w_ref[pl.ds(k0, k1 - k0), :], preferred_element_type=F32)
            else:
                part = lax.dot_general(_bf(a_p), w_ref[:, pl.ds(k0, k1 - k0)], NT_DIMS, preferred_element_type=F32)
            acc = part if acc is None else acc + part
        outs[0][...] = acc.astype(outs[0].dtype)
        for o_ref, val in zip(outs[1:], res[n_parts:]):
            o_ref[...] = val.astype(o_ref.dtype)

    in_specs = [pl.BlockSpec((tm, w), functools.partial(lambda i, j, ro: (i + ro, j), j=j, ro=ro))
                for (_, w, j, ro) in row_ins]
    in_specs += [pl.BlockSpec(c.shape, lambda i: (0, 0)) for c in consts]
    in_specs.append(pl.BlockSpec(memory_space=pl.ANY))
    widths = [(n, out_dtype)] + list(extra_outs)
    return pl.pallas_call(
        body, name=name, grid=(t_rows // tm,), in_specs=in_specs,
        out_specs=[pl.BlockSpec((tm, w), lambda i: (i, 0)) for (w, _) in widths],
        out_shape=[jax.ShapeDtypeStruct((t_rows, w), dt) for (w, dt) in widths],
        scratch_shapes=[pltpu.VMEM(weight.shape, weight.dtype), pltpu.SemaphoreType.DMA],
        compiler_params=pltpu.CompilerParams(dimension_semantics=("arbitrary",), vmem_limit_bytes=VMEM_LIMIT),
    )(*[e[0] for e in row_ins], *consts, weight)


def _rms(x, g):
    r = lax.rsqrt(jnp.mean(x * x, axis=-1, keepdims=True) + NORM_EPS)
    return x * r * g


def _rms_bwd(x, g, dy):
    r = lax.rsqrt(jnp.mean(x * x, axis=-1, keepdims=True) + NORM_EPS)
    xh = x * r
    dyg = dy * g
    dx = r * (dyg - xh * jnp.mean(dyg * xh, axis=-1, keepdims=True))
    return dx, jnp.sum(dy * xh, axis=0, keepdims=True)


def _silu(x):
    return x * _sigmoid(x)


def _silu_grad(x):
    s = _sigmoid(x)
    return s * (1.0 + x * (1.0 - s))


def _rope(xs, cos, s_up, s_dn):
    return xs * cos + pltpu.roll(xs, ROPE_HALF, 1) * s_up + pltpu.roll(xs, HEAD - ROPE_HALF, 1) * s_dn


def _rope_bwd(dy, cos, s_up, s_dn):
    return dy * cos + pltpu.roll(dy * s_up, HEAD - ROPE_HALF, 1) + pltpu.roll(dy * s_dn, ROPE_HALF, 1)


HG_SUB = 128
HG_LEVELS = 7
HG_E_ROWS = (HG_LEVELS + 1) * HG_SUB
HG_BWD_GROUP = 6
TN_DIMS = (((0,), (0,)), ((), ()))
NT_DIMS = (((1,), (1,)), ((), ()))


def _hg_constants():
    import numpy as np
    n = HG_SUB
    r = np.arange(n)[:, None]
    c = np.arange(n)[None, :]
    cs, ps = [], []
    for lvl in range(HG_LEVELS):
        m = (n // 2) >> lvl
        upper = (r % (2 * m)) >= m
        mid = (r // (2 * m)) * (2 * m) + m - 1
        cs.append(np.where(upper, (c > mid) & (c <= r), (c > r) & (c <= mid)))
        ps.append(((r // (2 * m)) == (c // (2 * m))) & upper & ((c % (2 * m)) < m))
    cs.append(c <= r)
    cs.append(np.ones((8, n), bool))
    cstack = np.concatenate(cs, 0).astype(np.float32)
    pstack = np.concatenate(ps, 0).astype(np.float32)
    pstack_t = np.concatenate([p.T for p in ps], 0).astype(np.float32)
    return (jnp.asarray(cstack, BF16), jnp.asarray(cstack[:HG_E_ROWS].T, BF16), jnp.asarray(pstack, F32),
            jnp.asarray(pstack_t, F32))


def _split_dot(c_bf, x):
    hi = _bf(x)
    lo = _bf(x - hi.astype(F32))
    r2 = jnp.dot(c_bf, jnp.concatenate([hi, lo], axis=1), preferred_element_type=F32)
    return r2[:, :HEAD] + r2[:, HEAD:]


def _hg_gates(hq, hf, lb):
    sq = _sigmoid(hq)
    sg = _sigmoid(hf)
    fg = lb + (1.0 - lb) * sg
    return sq, hq * sq, sg, fg, 1.0 - fg, jnp.log(fg)


def _hg_block_fwd(st, hq, hf, hi, lb, cstack, p_ref):
    _, q, _, _, k, g = _hg_gates(hq, hf, lb)
    v = hi
    e = _split_dot(cstack, g)
    bc = e[HG_LEVELS * HG_SUB:HG_E_ROWS]
    b_last = jnp.tile(e[HG_E_ROWS:], (HG_SUB // 8, 1))
    a = jnp.zeros((HG_SUB, HG_SUB), F32)
    for lvl in range(HG_LEVELS):
        x = jnp.exp(e[lvl * HG_SUB:(lvl + 1) * HG_SUB])
        a = a + p_ref[pl.ds(lvl * HG_SUB, HG_SUB), :] * lax.dot_general(_bf(q * x), _bf(k * x), NT_DIMS,
                                                                          preferred_element_type=F32)
    a_bf = _bf(a)
    diag = jnp.sum(q * k, axis=1, keepdims=True)
    o = (jnp.dot(a_bf, _bf(v), preferred_element_type=F32) + diag * v
         + lax.dot_general(_bf(q * jnp.exp(bc)), _bf(st), NT_DIMS, preferred_element_type=F32))
    kd = k * jnp.exp(b_last - bc)
    st_out = st * jnp.exp(b_last) + lax.dot_general(_bf(v), _bf(kd), TN_DIMS, preferred_element_type=F32)
    return st_out, o, a_bf


def _hg_block_bwd(st, dst_out, do, hq, hf, hi, lb, a_bf, cstack, cstack_t, p_ref, pt_ref):
    sq, q, sg, fg, k, g = _hg_gates(hq, hf, lb)
    v = hi
    e = _split_dot(cstack, g)
    bc = e[HG_LEVELS * HG_SUB:HG_E_ROWS]
    b_last = jnp.tile(e[HG_E_ROWS:], (HG_SUB // 8, 1))
    eb = jnp.exp(bc)
    qb = q * eb
    er = jnp.exp(b_last - bc)
    kd = k * er
    e_last = jnp.exp(b_last)
    do_bf, v_bf, dst_bf = _bf(do), _bf(v), _bf(dst_out)
    da = lax.dot_general(do_bf, v_bf, NT_DIMS, preferred_element_type=F32)
    dat = lax.dot_general(v_bf, do_bf, NT_DIMS, preferred_element_type=F32)
    d_diag = jnp.sum(do * v, axis=1, keepdims=True)
    dv = (lax.dot_general(a_bf, do_bf, TN_DIMS, preferred_element_type=F32)
          + jnp.sum(q * k, axis=1, keepdims=True) * do
          + lax.dot_general(_bf(kd), dst_bf, NT_DIMS, preferred_element_type=F32))
    dqb = jnp.dot(do_bf, _bf(st), preferred_element_type=F32)
    dst = dst_out * e_last + lax.dot_general(do_bf, _bf(qb), TN_DIMS, preferred_element_type=F32)
    dkd = jnp.dot(v_bf, dst_bf, preferred_element_type=F32)
    dq = dqb * eb + d_diag * k
    dk = dkd * er + d_diag * q
    d_last = (jnp.sum(dst_out * st * e_last, axis=0, keepdims=True)
              + jnp.sum(dkd * kd, axis=0, keepdims=True))
    des = []
    for lvl in range(HG_LEVELS):
        x = jnp.exp(e[lvl * HG_SUB:(lvl + 1) * HG_SUB])
        qh, kh = q * x, k * x
        dm = _bf(p_ref[pl.ds(lvl * HG_SUB, HG_SUB), :] * da)
        dmt = _bf(pt_ref[pl.ds(lvl * HG_SUB, HG_SUB), :] * dat)
        dqh = jnp.dot(dm, _bf(kh), preferred_element_type=F32)
        dkh = jnp.dot(dmt, _bf(qh), preferred_element_type=F32)
        dq = dq + dqh * x
        dk = dk + dkh * x
        des.append(dqh * qh + dkh * kh)
    des.append(dqb * qb - dkd * kd)
    dg = _split_dot(cstack_t, jnp.concatenate(des, axis=0)) + d_last
    dfg = dg / fg - dk
    dhq = dq * (sq * (1.0 + hq * (1.0 - sq)))
    dhf = dfg * (1.0 - lb) * sg * (1.0 - sg)
    return dst, dhq, dhf, dv, jnp.sum(dfg * (1.0 - sg), axis=0, keepdims=True)


def hgrn_fwd(proj_main, lb, consts, bl, lp, d):
    nh = d // HEAD
    rows_blk = _tile(lp, 768, SEQ_BLOCK)
    nb = lp // rows_blk
    spb = rows_blk // HG_SUB
    cstack, _, pstack, _ = consts

    def body(hq_ref, hf_ref, hi_ref, lb_ref, c_ref, p_ref, o_ref, st_ref, a_ref, s_ref):
        j = pl.program_id(2)

        @pl.when(j == 0)
        def _():
            s_ref[...] = jnp.zeros_like(s_ref)

        lbv = lb_ref[...]
        cs = c_ref[...]
        rows = [pl.ds(s * HG_SUB, HG_SUB) for s in range(spb)]
        gates = [_hg_gates(hq_ref[r, :].astype(F32), hf_ref[r, :].astype(F32), lbv) for r in rows]
        qs, ks = [g_[1] for g_ in gates], [g_[4] for g_ in gates]
        vs = [hi_ref[r, :].astype(F32) for r in rows]
        es = [_split_dot(cs, g_[5]) for g_ in gates]
        a_acc = [jnp.zeros((HG_SUB, HG_SUB), F32) for _ in rows]
        for lvl in range(HG_LEVELS):
            for s in range(spb):
                x = jnp.exp(es[s][lvl * HG_SUB:(lvl + 1) * HG_SUB])
                a_acc[s] = a_acc[s] + p_ref[pl.ds(lvl * HG_SUB, HG_SUB), :] * lax.dot_general(
                    _bf(qs[s] * x), _bf(ks[s] * x), NT_DIMS, preferred_element_type=F32)
        o_intra, qbs, kds, e_lasts = [], [], [], []
        for s in range(spb):
            a_bf = _bf(a_acc[s])
            a_ref[0, 0, s] = a_bf
            bc = es[s][HG_LEVELS * HG_SUB:HG_E_ROWS]
            b_last = jnp.tile(es[s][HG_E_ROWS:], (HG_SUB // 8, 1))
            o_intra.append(jnp.dot(a_bf, _bf(vs[s]), preferred_element_type=F32)
                           + jnp.sum(qs[s] * ks[s], axis=1, keepdims=True) * vs[s])
            qbs.append(_bf(qs[s] * jnp.exp(bc)))
            kds.append(_bf(ks[s] * jnp.exp(b_last - bc)))
            e_lasts.append(jnp.exp(b_last))
        st = s_ref[...]
        for s in range(spb):
            st_ref[0, 0, s] = st
            o_ref[rows[s], :] = (o_intra[s] + lax.dot_general(qbs[s], _bf(st), NT_DIMS, preferred_element_type=F32)
                                 ).astype(o_ref.dtype)
            st = st * e_lasts[s] + lax.dot_general(_bf(vs[s]), kds[s], TN_DIMS, preferred_element_type=F32)
        s_ref[...] = st

    def colspec(off):
        return pl.BlockSpec((rows_blk, HEAD), functools.partial(lambda h, b, j, off: (b * nb + j, off + h), off=off))

    whole = lambda arr: pl.BlockSpec(arr.shape, lambda h, b, j: (0, 0))
    return pl.pallas_call(
        body, name="hgrn_fwd", grid=(nh, bl, nb),
        in_specs=[colspec(0), colspec(nh), colspec(2 * nh), pl.BlockSpec((1, HEAD), lambda h, b, j: (0, h)),
                  whole(cstack), whole(pstack)],
        out_specs=[pl.BlockSpec((rows_blk, HEAD), lambda h, b, j: (b * nb + j, h)),
                   pl.BlockSpec((1, 1, spb, HEAD, HEAD), lambda h, b, j: (b, h, j, 0, 0)),
                   pl.BlockSpec((1, 1, spb, HG_SUB, HG_SUB), lambda h, b, j: (b, h, j, 0, 0))],
        out_shape=[jax.ShapeDtypeStruct((bl * lp, d), BF16),
                   jax.ShapeDtypeStruct((bl, nh, lp // HG_SUB, HEAD, HEAD), F32),
                   jax.ShapeDtypeStruct((bl, nh, lp // HG_SUB, HG_SUB, HG_SUB), BF16)],
        scratch_shapes=[pltpu.VMEM((HEAD, HEAD), F32)],
        compiler_params=pltpu.CompilerParams(dimension_semantics=("arbitrary", "arbitrary", "arbitrary")),
    )(proj_main, proj_main, proj_main, lb, cstack, pstack)


def hgrn_bwd(proj_main, lb, consts, states, a_mats, do_scan, bl, lp, d):
    nh = d // HEAD
    rows_blk = _tile(lp, 768, SEQ_BLOCK)
    nb = lp // rows_blk
    spb = rows_blk // HG_SUB
    cstack, cstack_t, pstack, pstack_t = consts

    def body(hq_ref, hf_ref, hi_ref, lb_ref, c_ref, ct_ref, p_ref, pt_ref, st_ref, a_ref, do_ref,
             dq_ref, df_ref, di_ref, dlb_ref, ds_ref):
        b_id, j = pl.program_id(1), pl.program_id(2)
        blk = nb - 1 - j

        @pl.when(j == 0)
        def _():
            ds_ref[...] = jnp.zeros_like(ds_ref)

        @pl.when((j == 0) & (b_id == 0))
        def _():
            dlb_ref[...] = jnp.zeros_like(dlb_ref)

        lbv = lb_ref[...]
        cs = c_ref[...]
        cst = ct_ref[...]

        dlb = jnp.zeros((1, HEAD), F32)
        for first in reversed(range(0, spb, HG_BWD_GROUP)):
            dlb = dlb + _hg_group_bwd(list(range(first, min(first + HG_BWD_GROUP, spb))), lbv, cs, cst, hq_ref,
                                      hf_ref, hi_ref, st_ref, a_ref, do_ref, p_ref, pt_ref, dq_ref, df_ref, di_ref,
                                      ds_ref)
        dlb_ref[...] += dlb

    def _hg_group_bwd(ids, lbv, cs, cst, hq_ref, hf_ref, hi_ref, st_ref, a_ref, do_ref, p_ref, pt_ref, dq_ref,
                      df_ref, di_ref, ds_ref):
        rng = range(len(ids))
        rows = [pl.ds(s * HG_SUB, HG_SUB) for s in ids]
        hqs = [hq_ref[r, :].astype(F32) for r in rows]
        gates = [_hg_gates(hqs[s], hf_ref[rows[s], :].astype(F32), lbv) for s in rng]
        sqs, qs, sgs, fgs, ks = ([g_[i] for g_ in gates] for i in range(5))
        vs = [hi_ref[r, :].astype(F32) for r in rows]
        dos = [do_ref[r, :].astype(F32) for r in rows]
        sts = [st_ref[0, 0, s] for s in ids]
        es = [_split_dot(cs, g_[5]) for g_ in gates]
        bcs = [e[HG_LEVELS * HG_SUB:HG_E_ROWS] for e in es]
        b_lasts = [jnp.tile(e[HG_E_ROWS:], (HG_SUB // 8, 1)) for e in es]
        ebs = [jnp.exp(bc) for bc in bcs]
        qbs = [qs[s] * ebs[s] for s in rng]
        ers = [jnp.exp(b_lasts[s] - bcs[s]) for s in rng]
        kds = [ks[s] * ers[s] for s in rng]
        e_lasts = [jnp.exp(b) for b in b_lasts]
        do_bfs, v_bfs = [_bf(x) for x in dos], [_bf(x) for x in vs]
        das = [lax.dot_general(do_bfs[s], v_bfs[s], NT_DIMS, preferred_element_type=F32) for s in rng]
        dats = [lax.dot_general(v_bfs[s], do_bfs[s], NT_DIMS, preferred_element_type=F32) for s in rng]
        dqbs = [jnp.dot(do_bfs[s], _bf(sts[s]), preferred_element_type=F32) for s in rng]
        m_s = [lax.dot_general(do_bfs[s], _bf(qbs[s]), TN_DIMS, preferred_element_type=F32) for s in rng]
        dst_outs = [None] * len(ids)
        dst = ds_ref[...]
        for s in reversed(rng):
            dst_outs[s] = dst
            dst = dst * e_lasts[s] + m_s[s]
        ds_ref[...] = dst
        dst_bfs = [_bf(x) for x in dst_outs]
        d_diags = [jnp.sum(dos[s] * vs[s], axis=1, keepdims=True) for s in rng]
        dvs = [lax.dot_general(a_ref[0, 0, ids[s]], do_bfs[s], TN_DIMS, preferred_element_type=F32)
               + jnp.sum(qs[s] * ks[s], axis=1, keepdims=True) * dos[s]
               + lax.dot_general(_bf(kds[s]), dst_bfs[s], NT_DIMS, preferred_element_type=F32) for s in rng]
        dkds = [jnp.dot(v_bfs[s], dst_bfs[s], preferred_element_type=F32) for s in rng]
        dqs = [dqbs[s] * ebs[s] + d_diags[s] * ks[s] for s in rng]
        dks = [dkds[s] * ers[s] + d_diags[s] * qs[s] for s in rng]
        d_lasts = [jnp.sum(dst_outs[s] * sts[s] * e_lasts[s], axis=0, keepdims=True)
                   + jnp.sum(dkds[s] * kds[s], axis=0, keepdims=True) for s in rng]
        des = [[] for _ in rng]
        for lvl in range(HG_LEVELS):
            for s in rng:
                x = jnp.exp(es[s][lvl * HG_SUB:(lvl + 1) * HG_SUB])
                qh, kh = qs[s] * x, ks[s] * x
                dm = _bf(p_ref[pl.ds(lvl * HG_SUB, HG_SUB), :] * das[s])
                dmt = _bf(pt_ref[pl.ds(lvl * HG_SUB, HG_SUB), :] * dats[s])
                dqh = jnp.dot(dm, _bf(kh), preferred_element_type=F32)
                dkh = jnp.dot(dmt, _bf(qh), preferred_element_type=F32)
                dqs[s] = dqs[s] + dqh * x
                dks[s] = dks[s] + dkh * x
                des[s].append(dqh * qh + dkh * kh)
        dlb = jnp.zeros((1, HEAD), F32)
        for s in rng:
            des[s].append(dqbs[s] * qbs[s] - dkds[s] * kds[s])
            dg = _split_dot(cst, jnp.concatenate(des[s], axis=0)) + d_lasts[s]
            dfg = dg / fgs[s] - dks[s]
            dq_ref[rows[s], :] = (dqs[s] * (sqs[s] * (1.0 + hqs[s] * (1.0 - sqs[s])))).astype(dq_ref.dtype)
            df_ref[rows[s], :] = (dfg * (1.0 - lbv) * sgs[s] * (1.0 - sgs[s])).astype(df_ref.dtype)
            di_ref[rows[s], :] = dvs[s].astype(di_ref.dtype)
            dlb = dlb + jnp.sum(dfg * (1.0 - sgs[s]), axis=0, keepdims=True)
        return dlb

    def colspec(off):
        return pl.BlockSpec((rows_blk, HEAD),
                            functools.partial(lambda h, b, j, off: (b * nb + nb - 1 - j, off + h), off=off))

    whole = lambda arr: pl.BlockSpec(arr.shape, lambda h, b, j: (0, 0))
    mats = lambda: pl.BlockSpec((1, 1, spb, HEAD, HEAD), lambda h, b, j: (b, h, nb - 1 - j, 0, 0))
    t_rows = bl * lp
    return pl.pallas_call(
        body, name="hgrn_bwd", grid=(nh, bl, nb),
        in_specs=[colspec(0), colspec(nh), colspec(2 * nh), pl.BlockSpec((1, HEAD), lambda h, b, j: (0, h)),
                  whole(cstack), whole(cstack_t), whole(pstack), whole(pstack_t), mats(), mats(), colspec(0)],
        out_specs=[colspec(0), colspec(0), colspec(0), pl.BlockSpec((1, HEAD), lambda h, b, j: (0, h))],
        out_shape=[jax.ShapeDtypeStruct((t_rows, d), BF16)] * 3 + [jax.ShapeDtypeStruct((1, d), F32)],
        scratch_shapes=[pltpu.VMEM((HEAD, HEAD), F32)],
        compiler_params=pltpu.CompilerParams(dimension_semantics=("arbitrary", "arbitrary", "arbitrary")),
    )(proj_main, proj_main, proj_main, lb, cstack, cstack_t, pstack, pstack_t, states, a_mats, do_scan)


def _allowed(row0, col0, nr, nc, transposed=False):
    if transposed:
        col = col0 + lax.broadcasted_iota(jnp.int32, (nc, 1), 0)
        row = row0 + lax.broadcasted_iota(jnp.int32, (1, nr), 1)
    else:
        row = row0 + lax.broadcasted_iota(jnp.int32, (nr, 1), 0)
        col = col0 + lax.broadcasted_iota(jnp.int32, (1, nc), 1)
    return (col <= row) & ((col >= PAD_FRONT) | (row < PAD_FRONT))


def attn_fwd(q_cat, k_cat, v, bl, lp, nm, scale):
    tq = tk = SEQ_BLOCK
    nq = lp // tq

    def body(q_ref, k_ref, v_ref, o_ref, lse_ref, m_ref, l_ref, acc_ref):
        i = pl.program_id(2)
        q = q_ref[...]
        m_ref[...] = jnp.full_like(m_ref, NEG)
        l_ref[...] = jnp.zeros_like(l_ref)
        acc_ref[...] = jnp.zeros_like(acc_ref)

        def kstep(c, carry):
            c0 = pl.multiple_of(c * tk, tk)
            s = lax.dot_general(q, k_ref[pl.ds(c0, tk), :], NT_DIMS, preferred_element_type=F32) * scale
            s = jnp.where(_allowed(i * tq, c * tk, tq, tk), s, NEG)
            m_old = m_ref[...]
            m_new = jnp.maximum(m_old, jnp.max(s, axis=1, keepdims=True))
            alpha = jnp.exp(m_old - m_new)
            p = jnp.exp(s - m_new)
            l_ref[...] = alpha * l_ref[...] + jnp.sum(p, axis=1, keepdims=True)
            acc_ref[...] = alpha * acc_ref[...] + jnp.dot(_bf(p), v_ref[pl.ds(c0, tk), :],
                                                          preferred_element_type=F32)
            m_ref[...] = m_new
            return carry

        lax.fori_loop(0, i + 1, kstep, 0)
        o_ref[...] = (acc_ref[...] / l_ref[...]).astype(o_ref.dtype)
        lse_ref[0, 0] = m_ref[...] + jnp.log(l_ref[...])

    return pl.pallas_call(
        body, name="attn_fwd", grid=(bl, nm, nq),
        in_specs=[pl.BlockSpec((tq, QK_PAD), lambda b, h, i: (b * nq + i, h)),
                  pl.BlockSpec((lp, QK_PAD), lambda b, h, i: (b, h)),
                  pl.BlockSpec((lp, HEAD), lambda b, h, i: (b, h))],
        out_specs=[pl.BlockSpec((tq, HEAD), lambda b, h, i: (b * nq + i, h)),
                   pl.BlockSpec((1, 1, tq, 1), lambda b, h, i: (b, h, i, 0))],
        out_shape=[jax.ShapeDtypeStruct((bl * lp, nm * HEAD), BF16),
                   jax.ShapeDtypeStruct((bl, nm, lp, 1), F32)],
        scratch_shapes=[pltpu.VMEM((tq, 1), F32), pltpu.VMEM((tq, 1), F32), pltpu.VMEM((tq, HEAD), F32)],
        compiler_params=pltpu.CompilerParams(dimension_semantics=("arbitrary", "arbitrary", "arbitrary")),
    )(q_cat, k_cat, v)


def attn_bwd_dq(q_cat, k_cat, v, o, do, lse, bl, lp, nm, scale):
    tq = tk = SEQ_BLOCK
    nq = lp // tq

    def body(q_ref, k_ref, v_ref, o_ref, do_ref, lse_ref, dq_ref, dl_ref, acc_ref):
        i = pl.program_id(2)
        q = q_ref[...]
        do_b = do_ref[...]
        delta = jnp.sum(o_ref[...].astype(F32) * do_b.astype(F32), axis=1, keepdims=True)
        lse_b = lse_ref[0, 0]
        acc_ref[...] = jnp.zeros_like(acc_ref)

        def kstep(c, carry):
            c0 = pl.multiple_of(c * tk, tk)
            ks = k_ref[pl.ds(c0, tk), :]
            s = lax.dot_general(q, ks, NT_DIMS, preferred_element_type=F32) * scale
            p = jnp.where(_allowed(i * tq, c * tk, tq, tk), jnp.exp(s - lse_b), 0.0)
            dp = lax.dot_general(do_b, v_ref[pl.ds(c0, tk), :], NT_DIMS, preferred_element_type=F32)
            ds = p * (dp - delta)
            acc_ref[...] += jnp.dot(_bf(ds), ks, preferred_element_type=F32)
            return carry

        lax.fori_loop(0, i + 1, kstep, 0)
        dq_ref[...] = acc_ref[...] * scale
        dl_ref[0, 0] = delta

    return pl.pallas_call(
        body, name="attn_bwd_dq", grid=(bl, nm, nq),
        in_specs=[pl.BlockSpec((tq, QK_PAD), lambda b, h, i: (b * nq + i, h)),
                  pl.BlockSpec((lp, QK_PAD), lambda b, h, i: (b, h)),
                  pl.BlockSpec((lp, HEAD), lambda b, h, i: (b, h)),
                  pl.BlockSpec((tq, HEAD), lambda b, h, i: (b * nq + i, h)),
                  pl.BlockSpec((tq, HEAD), lambda b, h, i: (b * nq + i, h)),
                  pl.BlockSpec((1, 1, tq, 1), lambda b, h, i: (b, h, i, 0))],
        out_specs=[pl.BlockSpec((tq, QK_PAD), lambda b, h, i: (b * nq + i, h)),
                   pl.BlockSpec((1, 1, tq, 1), lambda b, h, i: (b, h, i, 0))],
        out_shape=[jax.ShapeDtypeStruct((bl * lp, nm * QK_PAD), F32),
                   jax.ShapeDtypeStruct((bl, nm, lp, 1), F32)],
        scratch_shapes=[pltpu.VMEM((tq, QK_PAD), F32)],
        compiler_params=pltpu.CompilerParams(dimension_semantics=("arbitrary", "arbitrary", "arbitrary")),
    )(q_cat, k_cat, v, o, do, lse)


def attn_bwd_dkv(q_cat, k_cat, v, do, lse_row, delta_row, bl, lp, nm, scale):
    tq = tk = SEQ_BLOCK
    nq = lp // tq

    def body(q_ref, k_ref, v_ref, do_ref, lse_ref, dl_ref, dk_ref, dv_ref):
        i = pl.program_id(2)
        kt = k_ref[...]
        vt = v_ref[...]
        dk_ref[...] = jnp.zeros_like(dk_ref)
        dv_ref[...] = jnp.zeros_like(dv_ref)

        def qstep(c, carry):
            c0 = pl.multiple_of(c * tq, tq)
            qs = q_ref[pl.ds(c0, tq), :]
            dos = do_ref[pl.ds(c0, tq), :]
            st = lax.dot_general(kt, qs, NT_DIMS, preferred_element_type=F32) * scale
            pt = jnp.where(_allowed(c * tq, i * tk, tq, tk, transposed=True),
                           jnp.exp(st - lse_ref[0, 0, pl.ds(c, 1)][0]), 0.0)
            dv_ref[...] += jnp.dot(_bf(pt), dos, preferred_element_type=F32)
            dpt = lax.dot_general(vt, dos, NT_DIMS, preferred_element_type=F32)
            dst = pt * (dpt - dl_ref[0, 0, pl.ds(c, 1)][0])
            dk_ref[...] += jnp.dot(_bf(dst), qs, preferred_element_type=F32)
            return carry

        lax.fori_loop(i, nq, qstep, 0)
        dk_ref[...] = dk_ref[...] * scale

    return pl.pallas_call(
        body, name="attn_bwd_dkv", grid=(bl, nm, nq),
        in_specs=[pl.BlockSpec((lp, QK_PAD), lambda b, h, i: (b, h)),
                  pl.BlockSpec((tk, QK_PAD), lambda b, h, i: (b * nq + i, h)),
                  pl.BlockSpec((tk, HEAD), lambda b, h, i: (b * nq + i, h)),
                  pl.BlockSpec((lp, HEAD), lambda b, h, i: (b, h)),
                  pl.BlockSpec((1, 1, nq, 1, tq), lambda b, h, i: (b, h, 0, 0, 0)),
                  pl.BlockSpec((1, 1, nq, 1, tq), lambda b, h, i: (b, h, 0, 0, 0))],
        out_specs=[pl.BlockSpec((tk, QK_PAD), lambda b, h, i: (b * nq + i, h)),
                   pl.BlockSpec((tk, HEAD), lambda b, h, i: (b * nq + i, h))],
        out_shape=[jax.ShapeDtypeStruct((bl * lp, nm * QK_PAD), F32),
                   jax.ShapeDtypeStruct((bl * lp, nm * HEAD), F32)],
        compiler_params=pltpu.CompilerParams(dimension_semantics=("arbitrary", "arbitrary", "arbitrary")),
    )(q_cat, k_cat, v, do, lse_row, delta_row)


def _key_query_mask(key0, qry0, nk, nq_, causal):
    key = key0 + lax.broadcasted_iota(jnp.int32, (nk, 1), 0)
    if not causal:
        return key >= PAD_FRONT
    qry = qry0 + lax.broadcasted_iota(jnp.int32, (1, nq_), 1)
    return (key <= qry) & (key >= PAD_FRONT)


def _attn_tile(lp):
    return _tile(lp, ATTN_TILE_MAX, SEQ_BLOCK)


def attn_fwd_t(q_cat, k_cat, v_t, bl, lp, nm, scale):
    tq = tk = _attn_tile(lp)
    nq = lp // tq
    hp = ATTN_HEADS_PER_STEP
    assert nm % hp == 0

    def body(q_ref, k_ref, vt_ref, o_ref, lse_ref, m_ref, l_ref, acc_ref):
        i = pl.program_id(2)
        m_ref[...] = jnp.full_like(m_ref, NEG)
        l_ref[...] = jnp.zeros_like(l_ref)
        acc_ref[...] = jnp.zeros_like(acc_ref)

        def step(c, mask):
            c0 = pl.multiple_of(c * tk, tk)
            for hh in range(hp):
                cols = pl.ds(hh * QK_PAD, QK_PAD)
                st = lax.dot_general(k_ref[pl.ds(c0, tk), cols], q_ref[:, cols], NT_DIMS,
                                     preferred_element_type=F32) * scale
                if mask is not None:
                    st = jnp.where(_key_query_mask(c * tk, i * tq, tk, tq, mask == "causal"), st, NEG)
                m_old = m_ref[hh]
                m_new = jnp.maximum(m_old, jnp.max(st, axis=0, keepdims=True))
                alpha = jnp.exp(m_old - m_new)
                pt = jnp.exp(st - m_new)
                l_ref[hh] = alpha * l_ref[hh] + jnp.sum(pt, axis=0, keepdims=True)
                acc_ref[hh] = alpha * acc_ref[hh] + jnp.dot(vt_ref[0, hh, pl.ds(c, 1)][0], _bf(pt),
                                                            preferred_element_type=F32)
                m_ref[hh] = m_new

        def mid(c, carry):
            step(c, None)
            return carry

        @pl.when(i == 0)
        def _():
            step(0, "causal")

        @pl.when(i > 0)
        def _():
            step(0, "pad")
            lax.fori_loop(1, i, mid, 0)
            step(i, "causal")

        for hh in range(hp):
            o_ref[:, pl.ds(hh * HEAD, HEAD)] = jnp.transpose(acc_ref[hh] / l_ref[hh]).astype(o_ref.dtype)
            lse_ref[0, hh, 0] = m_ref[hh] + jnp.log(l_ref[hh])

    return pl.pallas_call(
        body, name="attn_fwd", grid=(bl, nm // hp, nq),
        in_specs=[pl.BlockSpec((tq, hp * QK_PAD), lambda b, h, i: (b * nq + i, h)),
                  pl.BlockSpec((lp, hp * QK_PAD), lambda b, h, i: (b, h)),
                  pl.BlockSpec((1, hp, nq, HEAD, tk), lambda b, h, i: (b, h, 0, 0, 0))],
        out_specs=[pl.BlockSpec((tq, hp * HEAD), lambda b, h, i: (b * nq + i, h)),
                   pl.BlockSpec((1, hp, 1, 1, tq), lambda b, h, i: (b, h, i, 0, 0))],
        out_shape=[jax.ShapeDtypeStruct((bl * lp, nm * HEAD), BF16),
                   jax.ShapeDtypeStruct((bl, nm, nq, 1, tq), F32)],
        scratch_shapes=[pltpu.VMEM((hp, 1, tq), F32), pltpu.VMEM((hp, 1, tq), F32), pltpu.VMEM((hp, HEAD, tq), F32)],
        compiler_params=pltpu.CompilerParams(dimension_semantics=("arbitrary", "arbitrary", "arbitrary")),
    )(q_cat, k_cat, v_t)


def attn_bwd_t(q_cat, k_cat, k_t, v, o, do, lse, bl, lp, nm, scale):
    tq = tk = _attn_tile(lp)
    nq = lp // tq
    hp = ATTN_HEADS_PER_STEP
    assert nm % hp == 0

    def body(q_ref, k_ref, kt_ref, v_ref, o_ref, do_ref, lse_ref, dq_ref, dk_ref, dv_ref, dqt_ref, dka_ref, dva_ref):
        i = pl.program_id(2)

        @pl.when(i == 0)
        def _():
            dqt_ref[...] = jnp.zeros_like(dqt_ref)

        dka_ref[...] = jnp.zeros_like(dka_ref)
        dva_ref[...] = jnp.zeros_like(dva_ref)
        ones8 = jnp.ones((8, HEAD), BF16)

        def step(c, mask):
            c0 = pl.multiple_of(c * tq, tq)
            for hh in range(hp):
                qcols, vcols = pl.ds(hh * QK_PAD, QK_PAD), pl.ds(hh * HEAD, HEAD)
                qs = q_ref[pl.ds(c0, tq), qcols]
                dos = do_ref[pl.ds(c0, tq), vcols]
                prod = dos.astype(F32) * o_ref[pl.ds(c0, tq), vcols].astype(F32)
                hi = _bf(prod)
                lo = _bf(prod - hi.astype(F32))
                delta8 = (lax.dot_general(ones8, hi, NT_DIMS, preferred_element_type=F32)
                          + lax.dot_general(ones8, lo, NT_DIMS, preferred_element_type=F32))
                st = lax.dot_general(k_ref[:, qcols], qs, NT_DIMS, preferred_element_type=F32) * scale
                pt = jnp.exp(st - lse_ref[0, hh, pl.ds(c, 1)][0])
                if mask is not None:
                    pt = jnp.where(_key_query_mask(i * tk, c * tq, tk, tq, mask == "causal"), pt, 0.0)
                dva_ref[hh] += jnp.dot(_bf(pt), dos, preferred_element_type=F32)
                dpt = lax.dot_general(v_ref[:, vcols], dos, NT_DIMS, preferred_element_type=F32)
                dst = _bf(pt * (dpt - jnp.tile(delta8, (tk // 8, 1))))
                dka_ref[hh] += jnp.dot(dst, qs, preferred_element_type=F32)
                dqt_ref[hh, pl.ds(c, 1)] += jnp.dot(kt_ref[0, hh, 0], dst, preferred_element_type=F32)[None]

        step(i, "causal")

        def rest_masked(c, carry):
            step(c, "pad")
            return carry

        def rest(c, carry):
            step(c, None)
            return carry

        @pl.when(i == 0)
        def _():
            lax.fori_loop(1, nq, rest_masked, 0)

        @pl.when(i > 0)
        def _():
            lax.fori_loop(i + 1, nq, rest, 0)

        for hh in range(hp):
            dk_ref[:, pl.ds(hh * QK_PAD, QK_PAD)] = (dka_ref[hh] * scale).astype(dk_ref.dtype)
            dv_ref[:, pl.ds(hh * HEAD, HEAD)] = dva_ref[hh].astype(dv_ref.dtype)

        @pl.when(i == nq - 1)
        def _():
            for hh in range(hp):
                for c in range(nq):
                    dq_ref[pl.ds(c * tq, tq), pl.ds(hh * QK_PAD, QK_PAD)] = (
                        jnp.transpose(dqt_ref[hh, c]) * scale).astype(dq_ref.dtype)

    return pl.pallas_call(
        body, name="attn_bwd", grid=(bl, nm // hp, nq),
        in_specs=[pl.BlockSpec((lp, hp * QK_PAD), lambda b, h, i: (b, h)),
                  pl.BlockSpec((tk, hp * QK_PAD), lambda b, h, i: (b * nq + i, h)),
                  pl.BlockSpec((1, hp, 1, QK_PAD, tk), lambda b, h, i: (b, h, i, 0, 0)),
                  pl.BlockSpec((tk, hp * HEAD), lambda b, h, i: (b * nq + i, h)),
                  pl.BlockSpec((lp, hp * HEAD), lambda b, h, i: (b, h)),
                  pl.BlockSpec((lp, hp * HEAD), lambda b, h, i: (b, h)),
                  pl.BlockSpec((1, hp, nq, 1, tq), lambda b, h, i: (b, h, 0, 0, 0))],
        out_specs=[pl.BlockSpec((lp, hp * QK_PAD), lambda b, h, i: (b, h)),
                   pl.BlockSpec((tk, hp * QK_PAD), lambda b, h, i: (b * nq + i, h)),
                   pl.BlockSpec((tk, hp * HEAD), lambda b, h, i: (b * nq + i, h))],
        out_shape=[jax.ShapeDtypeStruct((bl * lp, nm * QK_PAD), BF16),
                   jax.ShapeDtypeStruct((bl * lp, nm * QK_PAD), BF16),
                   jax.ShapeDtypeStruct((bl * lp, nm * HEAD), BF16)],
        scratch_shapes=[pltpu.VMEM((hp, nq, QK_PAD, tq), F32), pltpu.VMEM((hp, tk, QK_PAD), F32),
                        pltpu.VMEM((hp, tk, HEAD), F32)],
        compiler_params=pltpu.CompilerParams(dimension_semantics=("arbitrary", "arbitrary", "arbitrary")),
    )(q_cat, k_cat, k_t, v, o, do, lse)


def _place():
    return lax.axis_index("x"), lax.axis_index("y"), lax.axis_index("c")


def gather_shards(packed):
    hbm = pl.BlockSpec(memory_space=pl.ANY)

    def body(src_ref, out_ref, send_sems, recv_sems, local_sem):
        x, y, c = _place()
        me = 2 * x + y
        chips = [(1 - x, y), (x, 1 - y), (1 - x, 1 - y)]
        local = pltpu.make_async_copy(src_ref, out_ref.at[me], local_sem)
        local.start()
        sends = []
        for k, (px, py) in enumerate(chips):
            cp = pltpu.make_async_remote_copy(src_ref=src_ref, dst_ref=out_ref.at[me], send_sem=send_sems.at[k],
                                              recv_sem=recv_sems.at[k], device_id=(px, py, c), device_id_type=MESH)
            cp.start()
            sends.append(cp)
        for k, (px, py) in enumerate(chips):
            pltpu.make_async_remote_copy(src_ref=src_ref, dst_ref=out_ref.at[2 * px + py], send_sem=send_sems.at[k],
                                         recv_sem=recv_sems.at[k], device_id=(px, py, c),
                                         device_id_type=MESH).wait_recv()
        for cp in sends:
            cp.wait_send()
        local.wait()

    return pl.pallas_call(
        body, name="gather_shards", in_specs=[hbm], out_specs=hbm,
        out_shape=jax.ShapeDtypeStruct((4,) + packed.shape, packed.dtype),
        scratch_shapes=[pltpu.SemaphoreType.DMA((3,)), pltpu.SemaphoreType.DMA((3,)), pltpu.SemaphoreType.DMA],
    )(packed)


def gather_small(small):
    hbm = pl.BlockSpec(memory_space=pl.ANY)

    def body(small_ref, all_ref, send_sems, recv_sems, local_sem):
        x, y, c = _place()
        me = 4 * x + 2 * y + c
        local = pltpu.make_async_copy(small_ref, all_ref.at[me], local_sem)
        local.start()
        others = [(x ^ ((r >> 2) & 1), y ^ ((r >> 1) & 1), c ^ (r & 1)) for r in range(1, 8)]
        sends = []
        for r, peer in enumerate(others):
            cp = pltpu.make_async_remote_copy(src_ref=small_ref, dst_ref=all_ref.at[me], send_sem=send_sems.at[r],
                                              recv_sem=recv_sems.at[r], device_id=peer, device_id_type=MESH)
            cp.start()
            sends.append(cp)
        for r, (px, py, pc) in enumerate(others):
            pltpu.make_async_remote_copy(src_ref=small_ref, dst_ref=all_ref.at[4 * px + 2 * py + pc],
                                         send_sem=send_sems.at[r], recv_sem=recv_sems.at[r],
                                         device_id=(px, py, pc), device_id_type=MESH).wait_recv()
        for cp in sends:
            cp.wait_send()
        local.wait()

    return pl.pallas_call(
        body, name="gather_small", in_specs=[hbm], out_specs=hbm,
        out_shape=jax.ShapeDtypeStruct((8,) + small.shape, small.dtype),
        scratch_shapes=[pltpu.SemaphoreType.DMA((7,)), pltpu.SemaphoreType.DMA((7,)), pltpu.SemaphoreType.DMA],
    )(small)


def swap_with_sibling(name, parts):
    n = len(parts)
    hbm = pl.BlockSpec(memory_space=pl.ANY)

    def body(*refs):
        x, y, c = _place()
        cps = [pltpu.make_async_remote_copy(src_ref=refs[j], dst_ref=refs[n + j], send_sem=refs[2 * n].at[j],
                                            recv_sem=refs[2 * n + 1].at[j], device_id=(x, y, 1 - c),
                                            device_id_type=MESH) for j in range(n)]
        for cp in cps:
            cp.start()
        for cp in cps:
            cp.wait()

    return pl.pallas_call(
        body, name=name, in_specs=[hbm] * n, out_specs=[hbm] * n,
        out_shape=[jax.ShapeDtypeStruct(p.shape, p.dtype) for p in parts],
        scratch_shapes=[pltpu.SemaphoreType.DMA((n,)), pltpu.SemaphoreType.DMA((n,))],
    )(*parts)


def _chips3():
    x, y, c = _place()
    return [(1 - x, y, c), (x, 1 - y, c), (1 - x, 1 - y, c)]


def _push_copies(src_refs, land_refs, send_sems, recv_sems, per_chip):
    cps = []
    for j, (src_ref, land_ref) in enumerate(zip(src_refs, land_refs)):
        for k, (px, py, pc) in enumerate(_chips3()):
            part = src_ref.at[2 * px + py] if per_chip else src_ref
            cps.append(pltpu.make_async_remote_copy(
                src_ref=part, dst_ref=land_ref.at[k], send_sem=send_sems.at[3 * j + k],
                recv_sem=recv_sems.at[3 * j + k], device_id=(px, py, pc), device_id_type=MESH))
    return cps


def push_start(name, srcs, per_chip):
    n = len(srcs)
    hbm = pl.BlockSpec(memory_space=pltpu.HBM)
    sem = pl.BlockSpec(memory_space=pltpu.SEMAPHORE)
    lands = [lax.empty((3,) + s.shape[-2:], s.dtype) for s in srcs]

    def body(*refs):
        src_refs, land_refs = refs[:n], refs[n:2 * n]
        send_sems, recv_sems = refs[2 * n], refs[2 * n + 1]
        for cp in _push_copies(src_refs, land_refs, send_sems, recv_sems, per_chip):
            cp.start()
        refs[-1][...] = jnp.zeros_like(refs[-1])

    outs = pl.pallas_call(
        body, name=name,
        out_shape=(pltpu.SemaphoreType.DMA((3 * n,)), pltpu.SemaphoreType.DMA((3 * n,)),
                   *[pltpu.HBM(a.shape, a.dtype) for a in list(srcs) + lands], jax.ShapeDtypeStruct((8, HEAD), F32)),
        in_specs=(hbm,) * (2 * n),
        out_specs=(sem, sem) + (hbm,) * (2 * n) + (pl.BlockSpec(memory_space=pltpu.VMEM),),
        input_output_aliases={j: 2 + j for j in range(2 * n)},
        compiler_params=pltpu.CompilerParams(has_side_effects=pltpu.SideEffectType.DATAFLOW_SIDE_EFFECTING),
    )(*[pltpu.with_memory_space_constraint(a, pltpu.HBM) for a in list(srcs) + lands])
    return tuple(outs[:-1]), outs[-1]


def push_wait(name, handle, after, per_chip):
    send_sems, recv_sems = handle[0], handle[1]
    thru = handle[2:]
    n = len(thru) // 2
    hbm = pl.BlockSpec(memory_space=pltpu.HBM)
    sem = pl.BlockSpec(memory_space=pltpu.SEMAPHORE)

    def body(*refs):
        src_refs, land_refs = refs[:n], refs[n:2 * n]
        for cp in _push_copies(src_refs, land_refs, refs[2 * n], refs[2 * n + 1], per_chip):
            cp.wait_send()
            cp.wait_recv()

    outs = pl.pallas_call(
        body, name=name,
        out_shape=tuple(pltpu.HBM(a.shape, a.dtype) for a in thru),
        in_specs=(hbm,) * (2 * n) + (sem, sem, pl.BlockSpec(memory_space=pl.ANY)), out_specs=(hbm,) * (2 * n),
        input_output_aliases={j: j for j in range(2 * n)},
        compiler_params=pltpu.CompilerParams(has_side_effects=pltpu.SideEffectType.DATAFLOW_SIDE_EFFECTING),
    )(*thru, send_sems, recv_sems, after)
    return outs[:n], outs[n:]


def by_chip(own, landed, my_chip):
    by_rel = jnp.stack([own, landed[1], landed[0], landed[2]])
    return [lax.dynamic_index_in_dim(by_rel, jnp.bitwise_xor(s, my_chip), axis=0, keepdims=False) for s in range(4)]


def adamw(name, w, g_parts, m, v):
    r, c = w.shape
    tr = r if r * c <= 65536 else _tile(r, 128, 8)
    ng = len(g_parts)

    def body(*refs):
        w_ref, m_ref, v_ref = refs[0], refs[1 + ng], refs[2 + ng]
        g_ref, d_ref, nm_ref, nv_ref = refs[3 + ng:]
        gv = refs[1][...]
        for k in range(1, ng):
            gv = gv + refs[1 + k][...]
        m_new = ADAM_B1 * m_ref[...] + (1.0 - ADAM_B1) * gv
        v_new = ADAM_B2 * v_ref[...] + (1.0 - ADAM_B2) * (gv * gv)
        m_hat = m_new / (1.0 - ADAM_B1 ** ADAM_STEP)
        v_hat = v_new / (1.0 - ADAM_B2 ** ADAM_STEP)
        g_ref[...] = gv
        d_ref[...] = -ADAM_LR * (m_hat / (jnp.sqrt(v_hat) + ADAM_EPS) + ADAM_WD * w_ref[...])
        nm_ref[...] = m_new
        nv_ref[...] = v_new

    spec = pl.BlockSpec((tr, c), lambda i: (i, 0))
    return pl.pallas_call(
        body, name=name, grid=(r // tr,), in_specs=[spec] * (3 + ng), out_specs=[spec] * 4,
        out_shape=[jax.ShapeDtypeStruct((r, c), F32)] * 4,
        compiler_params=pltpu.CompilerParams(dimension_semantics=("arbitrary",)),
    )(w, *g_parts, m, v)


def split_full(name, full, s):
    if name in COL_SHARDED:
        c = full.shape[1] // 4
        return full[:, s * c:(s + 1) * c]
    r = full.shape[0] // 4
    return full[s * r:(s + 1) * r]


def join_shards(name, shards):
    return jnp.concatenate(shards, axis=1 if name in COL_SHARDED else 0)


def kernel(x, meta_tokens, w_in, b_gate, lb_logits, hg_norm_g, w_hg_o, q_a_norm_g, w_q_b, kv_a_norm_g, w_kv_b, w_mla_o, w_out, mix_pre_g, mix_post_g, ffn_pre_g, ffn_post_g, w_ffn_in, w_ffn_out, loss_target, m_meta_tokens, m_w_in, m_b_gate, m_lb_logits, m_hg_norm_g, m_w_hg_o, m_q_a_norm_g, m_w_q_b, m_kv_a_norm_g, m_w_kv_b, m_w_mla_o, m_w_out, m_mix_pre_g, m_mix_post_g, m_ffn_pre_g, m_ffn_post_g, m_w_ffn_in, m_w_ffn_out, v_meta_tokens, v_w_in, v_b_gate, v_lb_logits, v_hg_norm_g, v_w_hg_o, v_q_a_norm_g, v_w_q_b, v_kv_a_norm_g, v_w_kv_b, v_w_mla_o, v_w_out, v_mix_pre_g, v_mix_post_g, v_ffn_pre_g, v_ffn_post_g, v_w_ffn_in, v_w_ffn_out):
    wts = dict(meta_tokens=meta_tokens, w_in=w_in[0], b_gate=b_gate, lb_logits=lb_logits, hg_norm_g=hg_norm_g,
               w_hg_o=w_hg_o[0], q_a_norm_g=q_a_norm_g, w_q_b=w_q_b[0], kv_a_norm_g=kv_a_norm_g, w_kv_b=w_kv_b[0],
               w_mla_o=w_mla_o[0], w_out=w_out[0], mix_pre_g=mix_pre_g, mix_post_g=mix_post_g, ffn_pre_g=ffn_pre_g,
               ffn_post_g=ffn_post_g, w_ffn_in=w_ffn_in[0], w_ffn_out=w_ffn_out[0])
    mom_m = dict(meta_tokens=m_meta_tokens, w_in=m_w_in[0], b_gate=m_b_gate, lb_logits=m_lb_logits,
                 hg_norm_g=m_hg_norm_g, w_hg_o=m_w_hg_o[0], q_a_norm_g=m_q_a_norm_g, w_q_b=m_w_q_b[0],
                 kv_a_norm_g=m_kv_a_norm_g, w_kv_b=m_w_kv_b[0], w_mla_o=m_w_mla_o[0], w_out=m_w_out[0],
                 mix_pre_g=m_mix_pre_g, mix_post_g=m_mix_post_g, ffn_pre_g=m_ffn_pre_g, ffn_post_g=m_ffn_post_g,
                 w_ffn_in=m_w_ffn_in[0], w_ffn_out=m_w_ffn_out[0])
    mom_v = dict(meta_tokens=v_meta_tokens, w_in=v_w_in[0], b_gate=v_b_gate, lb_logits=v_lb_logits,
                 hg_norm_g=v_hg_norm_g, w_hg_o=v_w_hg_o[0], q_a_norm_g=v_q_a_norm_g, w_q_b=v_w_q_b[0],
                 kv_a_norm_g=v_kv_a_norm_g, w_kv_b=v_w_kv_b[0], w_mla_o=v_w_mla_o[0], w_out=v_w_out[0],
                 mix_pre_g=v_mix_pre_g, mix_post_g=v_mix_post_g, ffn_pre_g=v_ffn_pre_g, ffn_post_g=v_ffn_post_g,
                 w_ffn_in=v_w_ffn_in[0], w_ffn_out=v_w_ffn_out[0])

    bl, seq, d = x.shape
    lp = PAD_FRONT + N_META + seq
    t_rows = bl * lp
    nh = d // HEAD
    ql, kvl = wts["w_q_b"].shape[0], wts["w_kv_b"].shape[0]
    nm = (4 * wts["w_mla_o"].shape[0]) // HEAD
    ffn = 4 * wts["w_ffn_out"].shape[0]
    mla_w = ql + kvl + HEAD
    assert ql == kvl and ql % HEAD == 0 and seq % SEQ_BLOCK == 0 and d % HEAD == 0
    scale = (HEAD + ROPE) ** -0.5
    my_chip = 2 * lax.axis_index("x") + lax.axis_index("y")

    mcols = meta_tokens.shape[1]
    meta_all = gather_shards(meta_tokens)
    meta_full = jnp.concatenate([meta_all[s] for s in range(4)], axis=1)

    def start_gather(name, names, order_after):
        srcs = [_bf(wts[n]) for n in names]
        if order_after is not None:
            srcs[0] = srcs[0] + order_after[0, 0].astype(BF16)
        return push_start(name, srcs, per_chip=False)

    def finish_gather(name, names, started, after):
        owns, landed = push_wait(name, started[0], after, per_chip=False)
        return {n: join_shards(n, by_chip(own, land, my_chip)) for n, own, land in zip(names, owns, landed)}

    rest_names = tuple(n for n in BIG if n != "w_in")
    my_c = lax.axis_index("c")
    w_in_bf = _bf(wts["w_in"])
    half = w_in_bf.shape[0] // 2
    own_half = (lax.dynamic_slice_in_dim(w_in_bf, my_c * half, half, axis=0)
                + (meta_all[0, :1, :1] * 0.0)[0, 0].astype(BF16))
    gather_1 = push_start("gather_w_in_start", [own_half], per_chip=False)
    gather_2 = start_gather("gather_rest_start", rest_names, gather_1[1])

    h0 = jnp.concatenate([jnp.zeros((bl, PAD_FRONT, d), F32), jnp.broadcast_to(meta_full[None], (bl, N_META, d)), x],
                         axis=1).reshape(t_rows, d)
    tiles_seq, tiles_real = lp // SEQ_BLOCK, seq // SEQ_BLOCK
    assert PAD_FRONT + N_META == SEQ_BLOCK

    def real_block(i):
        return (i // tiles_seq) * tiles_real + jnp.maximum(i % tiles_seq - 1, 0)

    meta_rows = jnp.broadcast_to(((jnp.arange(lp) >= PAD_FRONT) & (jnp.arange(lp) < PAD_FRONT + N_META)
                                  ).astype(F32)[:, None], (lp, HEAD))
    pos = (jnp.arange(lp, dtype=jnp.int32) - PAD_FRONT).astype(F32)
    inv_freq = 1.0 / (ROPE_THETA ** (jnp.arange(0, ROPE, 2, dtype=F32) / ROPE))
    ang = pos[:, None] * inv_freq[None, :]
    zeros32 = jnp.zeros((lp, ROPE_HALF), F32)
    zeros64 = jnp.zeros((lp, HEAD - ROPE), F32)
    t_cos = jnp.concatenate([jnp.cos(ang), jnp.cos(ang), zeros64], axis=1)
    t_up = jnp.concatenate([zeros32, jnp.sin(ang), zeros64], axis=1)
    t_dn = jnp.concatenate([-jnp.sin(ang), zeros32, zeros64], axis=1)
    real = jnp.broadcast_to((jnp.arange(lp) >= PAD_FRONT + N_META).astype(F32)[:, None], (lp, d))
    lb_soft = jax.nn.softmax(lb_logits.astype(F32), axis=0)
    lb = lb_soft[0:1]

    (u1,) = rowwise("norm_mix_pre", lambda h, g: _rms(h, g), [(h0, d, 0)], [], [mix_pre_g + gather_2[1][0, 0]],
                    [(d, BF16)])
    _, (fetched,) = push_wait("gather_w_in_wait", gather_1[0], u1, per_chip=False)
    (handed,) = swap_with_sibling("swap_w_in", [fetched])
    halves = jnp.stack([fetched, handed])
    remote = jnp.concatenate([lax.dynamic_index_in_dim(halves, my_c, 0, keepdims=False),
                              lax.dynamic_index_in_dim(halves, 1 - my_c, 0, keepdims=False)], axis=1)
    full = {"w_in": join_shards("w_in", by_chip(w_in_bf, remote, my_chip))}
    w_main = jnp.concatenate([full["w_in"][:, :4 * d], full["w_in"][:, -2 * d:]], axis=1)
    w_mla = jnp.pad(full["w_in"][:, 4 * d:4 * d + ql + kvl + ROPE], ((0, 0), (0, HEAD - ROPE)))
    proj_main = matmul("proj_main", u1, w_main, "nn", out_dtype=BF16)
    proj_mla = matmul("proj_mla", u1, w_mla, "nn", out_dtype=BF16)
    hg_consts = _hg_constants()
    o_scan, states, a_mats = hgrn_fwd(proj_main, lb, hg_consts, bl, lp, d)

    def hg_out_fn(o, hg, g):
        return jnp.concatenate([_rms(o[:, h * HEAD:(h + 1) * HEAD], g) for h in range(nh)], axis=1) * _silu(hg)

    (o_hg,) = rowwise("hgrn_out", hg_out_fn, [(o_scan, d, 0), (proj_main, d, 3)], [], [hg_norm_g], [(d, BF16)])
    full.update(finish_gather("gather_rest_wait", rest_names, gather_2, o_hg))
    w_qb = jnp.pad(full["w_q_b"].reshape(ql, nm, HEAD + ROPE), ((0, 0), (0, 0), (0, QK_PAD - HEAD - ROPE))
                   ).reshape(ql, nm * QK_PAD)
    w_kvb = full["w_kv_b"]
    y_a = matmul("y_a", o_hg, _bf(full["w_hg_o"]), "nn", out_dtype=BF16)

    qn, kvn = rowwise("mla_norms", lambda cq, ckv, gq, gk: (_rms(cq, gq), _rms(ckv, gk)),
                      [(proj_mla, ql, 0), (proj_mla, kvl, 1)], [], [q_a_norm_g, kv_a_norm_g],
                      [(ql, BF16), (kvl, BF16)])
    q_full = matmul("q_up", qn, w_qb, "nn", out_dtype=BF16)
    kv_full = matmul("kv_up", kvn, w_kvb, "nn", out_dtype=BF16)

    def mla_prep_fn(qf, kvf, kpe, cos, s_up, s_dn):
        kpe_r = _rope(kpe, cos, s_up, s_dn)
        qs, ks, vs = [], [], []
        for h in range(nm):
            qs += [qf[:, h * QK_PAD:h * QK_PAD + HEAD], _rope(qf[:, h * QK_PAD + HEAD:(h + 1) * QK_PAD], cos, s_up, s_dn)]
            ks += [kvf[:, h * QK_PAD:h * QK_PAD + HEAD], kpe_r]
            vs += [kvf[:, h * QK_PAD + HEAD:(h + 1) * QK_PAD]]
        return jnp.concatenate(qs, axis=1), jnp.concatenate(ks, axis=1), jnp.concatenate(vs, axis=1)

    kpe_blk = (ql + kvl) // HEAD
    q_cat, k_cat, v_att = rowwise("mla_prep", mla_prep_fn,
                                  [(q_full, nm * QK_PAD, 0), (kv_full, nm * QK_PAD, 0), (proj_mla, HEAD, kpe_blk)],
                                  [t_cos, t_up, t_dn], [], [(nm * QK_PAD, BF16), (nm * QK_PAD, BF16), (nm * HEAD, BF16)])
    at = _attn_tile(lp)
    v_t = v_att.reshape(bl, lp // at, at, nm, HEAD).transpose(0, 3, 1, 4, 2)
    k_t = k_cat.reshape(bl, lp // at, at, nm, QK_PAD).transpose(0, 3, 1, 4, 2)
    o_mla, lse = attn_fwd_t(q_cat, k_cat, v_t, bl, lp, nm, scale)
    y_b = matmul("y_b", o_mla, _bf(full["w_mla_o"]), "nn", out_dtype=BF16)

    def gate_fn(ya, yb, ga, gb, bias):
        zv = _sigmoid(ga + bias[:, :d]) * ya + _sigmoid(gb + bias[:, d:]) * yb
        return zv, zv

    mixed, z = matmul_fused("gate_mix_out", gate_fn,
                            [(y_a, d, 0), (y_b, d, 0), (proj_main, d, 4), (proj_main, d, 5)], [b_gate],
                            _bf(full["w_out"]), [(0, d)], "nn", [(d, BF16)], tm=512)

    def mid_fn(h, mx, g_post, g_pre):
        h1 = h + _rms(mx, g_post)
        return h1, _rms(h1, g_pre)

    h1, u2 = rowwise("norm_mid", mid_fn, [(h0, d, 0), (mixed, d, 0)], [], [mix_post_g, ffn_pre_g],
                     [(d, F32), (d, BF16)])
    gu = matmul("ffn_in", u2, _bf(full["w_ffn_in"]), "nn", out_dtype=BF16)
    def swiglu_fn(gt, up):
        a = _silu(gt) * up
        return a, a

    f_out, act = matmul_fused("swiglu_ffn_out", swiglu_fn, [(gu, ffn, 0), (gu, ffn, 1)], [],
                              _bf(full["w_ffn_out"]), [(0, ffn)], "nn", [(ffn, BF16)])

    def loss_fn(h1v, fv, tg, realv, g_post):
        h2 = h1v + _rms(fv, g_post)
        diff = (h2 - tg) * realv
        part = jnp.broadcast_to(0.5 * jnp.sum(diff * diff, keepdims=True) / d, (1, HEAD))
        dy = diff / d
        df, dg = _rms_bwd(fv, g_post, dy)
        return dy, df, part, dg

    dy, df, loss_part, g_ffn_post = rowwise(
        "loss_head", loss_fn,
        [(h1, d, 0), (f_out, d, 0), (loss_target.reshape(bl * seq, d), d, 0, real_block)], [real],
        [ffn_post_g], [(d, BF16), (d, BF16)], [(1, HEAD), (1, d)])
    grads = {}
    d_act = matmul("d_act", df, _bf(full["w_ffn_out"]), "nt", out_dtype=BF16)
    grads["w_ffn_out"] = matmul("gw_ffn_out", act, df, "tn")

    def swiglu_bwd_fn(gt, up, da):
        dgt, dup = da * up * _silu_grad(gt), da * _silu(gt)
        return dgt, dup, jnp.concatenate([dgt, dup], axis=1)

    du2, dgu = matmul_fused("swiglu_bwd_d_u2", swiglu_bwd_fn, [(gu, ffn, 0), (gu, ffn, 1), (d_act, ffn, 0)], [],
                            _bf(full["w_ffn_in"]), [(0, ffn), (ffn, 2 * ffn)], "nt", [(2 * ffn, BF16)])
    grads["w_ffn_in"] = matmul("gw_ffn_in", u2, dgu, "tn")

    def mid_bwd_fn(dyv, h1v, du2v, mx, g_pre, g_post):
        dx, dg_pre = _rms_bwd(h1v, g_pre, du2v)
        dh1 = dyv + dx
        dmx, dg_post = _rms_bwd(mx, g_post, dh1)
        return dh1, dmx, dg_pre, dg_post

    dh1, dmixed, g_ffn_pre, g_mix_post = rowwise("norm_mid_bwd", mid_bwd_fn,
                                                 [(dy, d, 0), (h1, d, 0), (du2, d, 0), (mixed, d, 0)], [],
                                                 [ffn_pre_g, mix_post_g], [(d, BF16), (d, BF16)], [(1, d), (1, d)])
    dz = matmul("d_z", dmixed, _bf(full["w_out"]), "nt", out_dtype=BF16)
    grads["w_out"] = matmul("gw_out", z, dmixed, "tn")

    def gate_bwd_fn(dzv, ya, yb, ga, gb, bias):
        sa, sb = _sigmoid(ga + bias[:, :d]), _sigmoid(gb + bias[:, d:])
        dga = dzv * ya * sa * (1.0 - sa)
        dgb = dzv * yb * sb * (1.0 - sb)
        dgates = jnp.concatenate([dga, dgb], axis=1)
        return dzv * sa, dzv * sb, dgates, jnp.sum(dgates, axis=0, keepdims=True)

    dy_a, dy_b, dgates, g_b_gate = rowwise("gate_mix_bwd", gate_bwd_fn,
                                           [(dz, d, 0), (y_a, d, 0), (y_b, d, 0), (proj_main, d, 4), (proj_main, d, 5)],
                                           [], [b_gate], [(d, BF16), (d, BF16), (2 * d, BF16)], [(1, 2 * d)])
    do_hg = matmul("d_o_hg", dy_a, _bf(full["w_hg_o"]), "nt", out_dtype=BF16)
    grads["w_hg_o"] = matmul("gw_hg_o", o_hg, dy_a, "tn")
    do_mla = matmul("d_o_mla", dy_b, _bf(full["w_mla_o"]), "nt", out_dtype=BF16)
    grads["w_mla_o"] = matmul("gw_mla_o", o_mla, dy_b, "tn")

    early = ("w_hg_o", "w_mla_o", "w_out", "w_ffn_in", "w_ffn_out")
    late = ("w_in", "w_q_b", "w_kv_b")

    def start_grads(name, names):
        sends = [_bf(jnp.stack([split_full(n, grads[n], s) for s in range(4)])) for n in names]
        mines = []
        for n in names:
            r, c = wts[n].shape
            axis, size = (1, c) if n in COL_SHARDED else (0, r)
            mines.append(lax.dynamic_slice_in_dim(grads[n], my_chip * size, size, axis=axis))
        handle, token = push_start(name, sends, per_chip=True)
        return handle, token, mines

    def finish_grads(tag, names, started, after):
        handle, _, mines = started
        _, landed = push_wait(f"grads_{tag}_wait", handle, after, per_chip=True)
        parts = []
        for n, mine, land in zip(names, mines, landed):
            r, c = mine.shape
            tr = _tile(r, 256, 16)
            land2 = land.reshape(3 * r, c)
            parts.append(rowwise(f"sum_chips_{n}", lambda a, r0, r1, r2: a + r0 + r1 + r2,
                                 [(mine, c, 0)] + [(land2, c, 0, k * (r // tr)) for k in range(3)],
                                 [], [], [(c, F32)], tm=tr)[0])
        sibs = swap_with_sibling(f"swap_{tag}", parts)
        return {n: [p, s] for n, p, s in zip(names, parts, sibs)}

    grads_early = start_grads("grads_early_start", early)
    token_a = grads_early[1]

    def hg_out_bwd_fn(do, o, hg, g):
        sg = _silu(hg)
        dn = do * sg
        dos, dgs, ons = [], 0.0, []
        for h in range(nh):
            sl = slice(h * HEAD, (h + 1) * HEAD)
            dx, dg = _rms_bwd(o[:, sl], g, dn[:, sl])
            dos.append(dx)
            dgs = dgs + dg
            ons.append(_rms(o[:, sl], g))
        dhg = do * jnp.concatenate(ons, axis=1) * _silu_grad(hg)
        return jnp.concatenate(dos, axis=1), dhg, dgs

    do_scan, dhg, g_hg_norm = rowwise("hgrn_out_bwd", hg_out_bwd_fn, [(do_hg, d, 0), (o_scan, d, 0), (proj_main, d, 3)],
                                      [], [hg_norm_g], [(d, BF16), (d, BF16)], [(1, HEAD)])
    dhq, dhf, dhi, g_lb = hgrn_bwd(proj_main, lb + token_a[0, 0], hg_consts, states, a_mats, do_scan, bl, lp, d)

    dq_cat, dk_cat, dv_att = attn_bwd_t(q_cat, k_cat, k_t, v_att, o_mla, do_mla, lse, bl, lp, nm, scale)

    def mla_prep_bwd_fn(dqc, dkc, dvv, cos, s_up, s_dn):
        dqs, dkvs, dkpe = [], [], 0.0
        for h in range(nm):
            dqs += [dqc[:, h * QK_PAD:h * QK_PAD + HEAD],
                    _rope_bwd(dqc[:, h * QK_PAD + HEAD:(h + 1) * QK_PAD], cos, s_up, s_dn)]
            dkvs += [dkc[:, h * QK_PAD:h * QK_PAD + HEAD], dvv[:, h * HEAD:(h + 1) * HEAD]]
            dkpe = dkpe + dkc[:, h * QK_PAD + HEAD:(h + 1) * QK_PAD]
        return jnp.concatenate(dqs, axis=1), jnp.concatenate(dkvs, axis=1), _rope_bwd(dkpe, cos, s_up, s_dn)

    dq_full, dkv_full, dkpe = rowwise("mla_prep_bwd", mla_prep_bwd_fn,
                                      [(dq_cat, nm * QK_PAD, 0), (dk_cat, nm * QK_PAD, 0), (dv_att, nm * HEAD, 0)],
                                      [t_cos, t_up, t_dn], [],
                                      [(nm * QK_PAD, BF16), (nm * QK_PAD, BF16), (HEAD, F32)])
    dqn = matmul("d_qn", dq_full, w_qb, "nt", out_dtype=BF16)
    g_wqb = matmul("gw_q_b", qn, dq_full, "tn")
    grads["w_q_b"] = g_wqb.reshape(ql, nm, QK_PAD)[:, :, :HEAD + ROPE].reshape(ql, nm * (HEAD + ROPE))
    dkvn = matmul("d_kvn", dkv_full, w_kvb, "nt", out_dtype=BF16)
    grads["w_kv_b"] = matmul("gw_kv_b", kvn, dkv_full, "tn")

    def mla_norms_bwd_fn(dqnv, dkvnv, cq, ckv, dkpev, gq, gk):
        dcq, dgq = _rms_bwd(cq, gq, dqnv)
        dckv, dgk = _rms_bwd(ckv, gk, dkvnv)
        return jnp.concatenate([dcq, dckv, dkpev], axis=1), dgq, dgk

    dmla, g_q_norm, g_kv_norm = rowwise("mla_norms_bwd", mla_norms_bwd_fn,
                                        [(dqn, ql, 0), (dkvn, kvl, 0), (proj_mla, ql, 0), (proj_mla, kvl, 1),
                                         (dkpe, HEAD, 0)], [], [q_a_norm_g, kv_a_norm_g],
                                        [(mla_w, BF16)], [(1, ql), (1, kvl)])

    d_pieces = [dhq, dhf, dhi, dhg, dgates, dmla]
    gw_parts = [matmul(f"gw_in_{k}", u1, dp, "tn") for k, dp in enumerate(d_pieces)]
    grads["w_in"] = jnp.concatenate(gw_parts[:4] + [gw_parts[5][:, :ql + kvl + ROPE], gw_parts[4]], axis=1)
    grads_late = start_grads("grads_late_start", late)
    w_mla_after = w_mla + grads_late[1][0, 0].astype(BF16)
    w_pieces = [w_main[:, 0:d], w_main[:, d:2 * d], w_main[:, 2 * d:3 * d], w_main[:, 3 * d:4 * d],
                w_main[:, 4 * d:6 * d], w_mla_after]
    du1 = matmul("d_u1", d_pieces, w_pieces, "nt", out_dtype=BF16)

    def first_bwd_fn(dh1v, h, du1v, is_meta, g):
        dx, dg = _rms_bwd(h, g, du1v)
        dh0v = dh1v + dx
        return dh0v, dg, dh0v * jnp.tile(is_meta, (1, d // HEAD))

    grad_x, g_mix_pre, meta_tile = rowwise(
        "norm_mix_pre_bwd", first_bwd_fn, [(dh1, d, 0), (h0, d, 0), (du1, d, 0)], [meta_rows], [mix_pre_g],
        [(d, F32, bl * seq, real_block)], [(1, d), (SEQ_BLOCK, d)])
    grad_x = grad_x.reshape(bl, seq, d)

    g_parts = finish_grads("early", early, grads_early, g_mix_pre)
    updates = {}

    def update(n, parts):
        w2 = wts[n].reshape(-1, wts[n].shape[-1])
        updates[n] = adamw("adamw_" + n, w2, [p.reshape(w2.shape) for p in parts], mom_m[n].reshape(w2.shape),
                           mom_v[n].reshape(w2.shape))

    for n in early:
        update(n, g_parts[n])
    g_parts = finish_grads("late", late, grads_late, updates[early[-1]][0])
    for n in late:
        update(n, g_parts[n])
    p0 = lb_soft[0:1]
    g_lb_logits = jnp.concatenate([g_lb * p0 * (1.0 - p0), -g_lb * p0 * (1.0 - p0)], axis=0)

    def row_of(vec):
        return vec.reshape(-1, d) if vec.size >= d else jnp.pad(vec.reshape(1, -1), ((0, 0), (0, d - vec.size)))

    small_parts = dict(b_gate=g_b_gate, lb_logits=g_lb_logits, hg_norm_g=g_hg_norm, q_a_norm_g=g_q_norm,
                       kv_a_norm_g=g_kv_norm, mix_pre_g=g_mix_pre, mix_post_g=g_mix_post, ffn_pre_g=g_ffn_pre,
                       ffn_post_g=g_ffn_post)
    g_meta = meta_tile[PAD_FRONT:PAD_FRONT + N_META]
    small_rows = [row_of(small_parts[n]) for n in SMALL] + [row_of(g_meta)]
    n_small = sum(r.shape[0] for r in small_rows)
    small = jnp.pad(jnp.concatenate(small_rows, axis=0), ((0, -(-n_small // 8) * 8 - n_small), (0, 0)))
    all_small = gather_small(small)
    small_t = small.shape[0]

    def sum8_fn(*slabs):
        acc = slabs[0]
        for s in slabs[1:]:
            acc = acc + s
        return acc

    (g_small,) = rowwise("sum_small", sum8_fn, [(all_small.reshape(8 * small_t, d), d, 0, k) for k in range(8)],
                         [], [], [(d, F32)], tm=small_t, n_rows=small_t)

    off = 0
    for n, part in zip(SMALL, small_rows[:-1]):
        rows = part.shape[0]
        update(n, [g_small[off:off + rows, :d].reshape(-1)[:wts[n].size]])
        off += rows
    update("meta_tokens", [lax.dynamic_slice_in_dim(g_small[off:off + N_META, :d], my_chip * mcols, mcols, axis=1)])

    loss = lax.psum(loss_part[0, 0], ("x", "y", "c"))

    def shaped(n, a):
        return a.reshape((1,) + wts[n].shape) if n in BIG else a.reshape(wts[n].shape)

    return (loss, grad_x, *[shaped(n, updates[n][k]) for k in range(4) for n in WEIGHTS])
```

```python
import functools
import math

import jax
import jax.numpy as jnp
from jax import lax
from jax.experimental import pallas as pl
from jax.experimental.pallas import tpu as pltpu

F32 = jnp.float32
BF16 = jnp.bfloat16
MESH = pl.DeviceIdType.MESH

N_META = 16
NORM_EPS = 1e-6
HEAD = 128
ROPE = 64
ROPE_HALF = ROPE // 2
QK_PAD = 2 * HEAD
ROPE_THETA = 10000.0
SEQ_BLOCK = 256
PAD_FRONT = SEQ_BLOCK - N_META
NEG = -1e30
VMEM_LIMIT = 56 * 1024 * 1024
ATTN_HEADS_PER_STEP = 1
ATTN_TILE_MAX = 768

ADAM_LR, ADAM_B1, ADAM_B2, ADAM_EPS, ADAM_WD, ADAM_STEP = 0.001, 0.9, 0.999, 1e-08, 0.01, 10

BIG = ("w_in", "w_hg_o", "w_q_b", "w_kv_b", "w_mla_o", "w_out", "w_ffn_in", "w_ffn_out")
COL_SHARDED = ("w_in", "w_q_b", "w_kv_b", "w_ffn_in")
SMALL = ("b_gate", "lb_logits", "hg_norm_g", "q_a_norm_g", "kv_a_norm_g", "mix_pre_g", "mix_post_g",
         "ffn_pre_g", "ffn_post_g")
WEIGHTS = ("meta_tokens", "w_in", "b_gate", "lb_logits", "hg_norm_g", "w_hg_o", "q_a_norm_g", "w_q_b",
           "kv_a_norm_g", "w_kv_b", "w_mla_o", "w_out", "mix_pre_g", "mix_post_g", "ffn_pre_g", "ffn_post_g",
           "w_ffn_in", "w_ffn_out")


def _tile(n, cap, unit=128):
    if n <= cap:
        return n
    best = None
    for t in range(unit, cap + 1, unit):
        if n % t == 0:
            best = t
    assert best is not None, (n, cap, unit)
    return best


def _sigmoid(x):
    return 1.0 / (1.0 + jnp.exp(-x))


def _bf(x):
    return x.astype(BF16)


def rowwise(name, fn, row_ins, seq_tabs, consts, row_outs, acc_outs=(), tm=SEQ_BLOCK, n_rows=None):
    t_rows = row_ins[0][0].shape[0] if n_rows is None else n_rows
    nt = t_rows // tm
    assert t_rows % tm == 0
    n_in = len(row_ins) + len(seq_tabs) + len(consts)
    n_row = len(row_outs)

    def body(*refs):
        vals = [r[...].astype(F32) for r in refs[:n_in]]
        res = fn(*vals)
        if not isinstance(res, (tuple, list)):
            res = (res,)
        outs = refs[n_in:]
        for k in range(n_row):
            outs[k][...] = res[k].astype(outs[k].dtype)
        if acc_outs:
            @pl.when(pl.program_id(0) == 0)
            def _():
                for k in range(len(acc_outs)):
                    outs[n_row + k][...] = jnp.zeros_like(outs[n_row + k])

            for k in range(len(acc_outs)):
                outs[n_row + k][...] += res[n_row + k]

    row_ins = [tuple(e) + (0,) * (4 - len(e)) for e in row_ins]
    in_specs = [pl.BlockSpec((tm, w), functools.partial(lambda i, j, ro: (ro(i) if callable(ro) else i + ro, j),
                                                        j=j, ro=ro)) for (_, w, j, ro) in row_ins]
    for tab in seq_tabs:
        per = tab.shape[0] // tm
        in_specs.append(pl.BlockSpec((tm, tab.shape[1]), functools.partial(lambda i, per: (i % per, 0), per=per)))
    for c in consts:
        in_specs.append(pl.BlockSpec(c.shape, lambda i: (0, 0)))
    row_outs = [tuple(e) + (t_rows, None)[len(e) - 2:] for e in row_outs]
    out_specs = [pl.BlockSpec((tm, w), functools.partial(lambda i, rm: (i if rm is None else rm(i), 0), rm=rm))
                 for (w, _, _, rm) in row_outs]
    out_specs += [pl.BlockSpec(s, lambda i: (0, 0)) for s in acc_outs]
    out_shape = [jax.ShapeDtypeStruct((rows, w), dt) for (w, dt, rows, _) in row_outs]
    out_shape += [jax.ShapeDtypeStruct(s, F32) for s in acc_outs]
    res = pl.pallas_call(
        body, name=name, grid=(nt,), in_specs=in_specs, out_specs=out_specs, out_shape=out_shape,
        compiler_params=pltpu.CompilerParams(dimension_semantics=("arbitrary",)),
    )(*[e[0] for e in row_ins], *seq_tabs, *consts)
    return res


def matmul(name, a, b, mode, out_dtype=F32):
    if mode != "tn":
        return _matmul_resident(name, a if isinstance(a, (list, tuple)) else [a],
                                b if isinstance(b, (list, tuple)) else [b], mode, out_dtype)
    kdim, m = a.shape
    n = b.shape[1]
    tn = _tile(n, 1536)
    tm, tk = _tile(m, 1408 if tn <= 1024 else 1024), _tile(kdim, 1024)
    nk = kdim // tk

    def body(a_ref, b_ref, o_ref, acc_ref):
        k = pl.program_id(2)

        @pl.when(k == 0)
        def _():
            acc_ref[...] = jnp.zeros_like(acc_ref)

        acc_ref[...] += lax.dot_general(a_ref[...], b_ref[...], TN_DIMS, preferred_element_type=F32)

        @pl.when(k == nk - 1)
        def _():
            o_ref[...] = acc_ref[...].astype(o_ref.dtype)

    return pl.pallas_call(
        body, name=name, grid=(m // tm, n // tn, nk),
        in_specs=[pl.BlockSpec((tk, tm), lambda i, j, k: (k, i)), pl.BlockSpec((tk, tn), lambda i, j, k: (k, j))],
        out_specs=pl.BlockSpec((tm, tn), lambda i, j, k: (i, j)),
        out_shape=jax.ShapeDtypeStruct((m, n), out_dtype),
        scratch_shapes=[pltpu.VMEM((tm, tn), F32)],
        compiler_params=pltpu.CompilerParams(dimension_semantics=("arbitrary", "arbitrary", "arbitrary"),
                                             vmem_limit_bytes=VMEM_LIMIT),
    )(a, b)


def _matmul_resident(name, a_list, b_list, mode, out_dtype):
    m = a_list[0].shape[0]
    n = b_list[0].shape[1] if mode == "nn" else b_list[0].shape[0]
    k_total = sum(a.shape[1] for a in a_list)
    out_bytes = 2 if out_dtype == BF16 else 4
    budget = VMEM_LIMIT - 4 * k_total * n - (6 << 20)
    tm = 1024
    while tm > 128 and 2 * tm * (2 * k_total + out_bytes * n) > budget:
        tm //= 2
    tm = _tile(m, tm)
    cn = _tile(n, 1024)
    npairs = len(a_list)

    def body(*refs):
        a_refs, b_refs, o_ref = refs[:npairs], refs[npairs:2 * npairs], refs[2 * npairs]
        for c in range(n // cn):
            acc = None
            for a_ref, b_ref in zip(a_refs, b_refs):
                if mode == "nn":
                    part = jnp.dot(a_ref[...], b_ref[:, pl.ds(c * cn, cn)], preferred_element_type=F32)
                else:
                    part = lax.dot_general(a_ref[...], b_ref[pl.ds(c * cn, cn), :], NT_DIMS,
                                           preferred_element_type=F32)
                acc = part if acc is None else acc + part
            o_ref[:, pl.ds(c * cn, cn)] = acc.astype(o_ref.dtype)

    in_specs = [pl.BlockSpec((tm, a.shape[1]), lambda i: (i, 0)) for a in a_list]
    in_specs += [pl.BlockSpec(b.shape, lambda i: (0, 0)) for b in b_list]
    return pl.pallas_call(
        body, name=name, grid=(m // tm,), in_specs=in_specs,
        out_specs=pl.BlockSpec((tm, n), lambda i: (i, 0)),
        out_shape=jax.ShapeDtypeStruct((m, n), out_dtype),
        compiler_params=pltpu.CompilerParams(dimension_semantics=("arbitrary",), vmem_limit_bytes=VMEM_LIMIT),
    )(*a_list, *b_list)


def matmul_fused(name, fn, row_ins, consts, weight, pieces, mode, extra_outs, out_dtype=BF16, tm=256):
    row_ins = [tuple(e) + (0,) * (4 - len(e)) for e in row_ins]
    t_rows = row_ins[0][0].shape[0]
    tm = _tile(t_rows, tm)
    n = weight.shape[1] if mode == "nn" else weight.shape[0]
    n_in = len(row_ins) + len(consts)
    n_parts = len(pieces)

    def body(*refs):
        w_hbm = refs[n_in]
        outs = refs[n_in + 1:n_in + 2 + len(extra_outs)]
        w_ref, sem = refs[-2], refs[-1]

        @pl.when(pl.program_id(0) == 0)
        def _():
            cp = pltpu.make_async_copy(w_hbm, w_ref, sem)
            cp.start()
            cp.wait()

        res = fn(*[r[...].astype(F32) for r in refs[:n_in]])
        acc = None
        for a_p, (k0, k1) in zip(res[:n_parts], pieces):
            if mode == "nn":
                part = jnp.dot(_bf(a_p), w_ref[pl.ds(k0, k1 - k0), :], preferred_element_type=F32)
            else:
                part = lax.dot_general(_bf(a_p), w_ref[:, pl.ds(k0, k1 - k0)], NT_DIMS, preferred_element_type=F32)
            acc = part if acc is None else acc + part
        outs[0][...] = acc.astype(outs[0].dtype)
        for o_ref, val in zip(outs[1:], res[n_parts:]):
            o_ref[...] = val.astype(o_ref.dtype)

    in_specs = [pl.BlockSpec((tm, w), functools.partial(lambda i, j, ro: (i + ro, j), j=j, ro=ro))
                for (_, w, j, ro) in row_ins]
    in_specs += [pl.BlockSpec(c.shape, lambda i: (0, 0)) for c in consts]
    in_specs.append(pl.BlockSpec(memory_space=pl.ANY))
    widths = [(n, out_dtype)] + list(extra_outs)
    return pl.pallas_call(
        body, name=name, grid=(t_rows // tm,), in_specs=in_specs,
        out_specs=[pl.BlockSpec((tm, w), lambda i: (i, 0)) for (w, _) in widths],
        out_shape=[jax.ShapeDtypeStruct((t_rows, w), dt) for (w, dt) in widths],
        scratch_shapes=[pltpu.VMEM(weight.shape, weight.dtype), pltpu.SemaphoreType.DMA],
        compiler_params=pltpu.CompilerParams(dimension_semantics=("arbitrary",), vmem_limit_bytes=VMEM_LIMIT),
    )(*[e[0] for e in row_ins], *consts, weight)


def _rms(x, g):
    r = lax.rsqrt(jnp.mean(x * x, axis=-1, keepdims=True) + NORM_EPS)
    return x * r * g


def _rms_bwd(x, g, dy):
    r = lax.rsqrt(jnp.mean(x * x, axis=-1, keepdims=True) + NORM_EPS)
    xh = x * r
    dyg = dy * g
    dx = r * (dyg - xh * jnp.mean(dyg * xh, axis=-1, keepdims=True))
    return dx, jnp.sum(dy * xh, axis=0, keepdims=True)


def _silu(x):
    return x * _sigmoid(x)


def _silu_grad(x):
    s = _sigmoid(x)
    return s * (1.0 + x * (1.0 - s))


def _rope(xs, cos, s_up, s_dn):
    return xs * cos + pltpu.roll(xs, ROPE_HALF, 1) * s_up + pltpu.roll(xs, HEAD - ROPE_HALF, 1) * s_dn


def _rope_bwd(dy, cos, s_up, s_dn):
    return dy * cos + pltpu.roll(dy * s_up, HEAD - ROPE_HALF, 1) + pltpu.roll(dy * s_dn, ROPE_HALF, 1)


HG_SUB = 128
HG_LEVELS = 7
HG_E_ROWS = (HG_LEVELS + 1) * HG_SUB
HG_BWD_GROUP = 6
TN_DIMS = (((0,), (0,)), ((), ()))
NT_DIMS = (((1,), (1,)), ((), ()))


def _hg_constants():
    import numpy as np
    n = HG_SUB
    r = np.arange(n)[:, None]
    c = np.arange(n)[None, :]
    cs, ps = [], []
    for lvl in range(HG_LEVELS):
        m = (n // 2) >> lvl
        upper = (r % (2 * m)) >= m
        mid = (r // (2 * m)) * (2 * m) + m - 1
        cs.append(np.where(upper, (c > mid) & (c <= r), (c > r) & (c <= mid)))
        ps.append(((r // (2 * m)) == (c // (2 * m))) & upper & ((c % (2 * m)) < m))
    cs.append(c <= r)
    cs.append(np.ones((8, n), bool))
    cstack = np.concatenate(cs, 0).astype(np.float32)
    pstack = np.concatenate(ps, 0).astype(np.float32)
    pstack_t = np.concatenate([p.T for p in ps], 0).astype(np.float32)
    return (jnp.asarray(cstack, BF16), jnp.asarray(cstack[:HG_E_ROWS].T, BF16), jnp.asarray(pstack, F32),
            jnp.asarray(pstack_t, F32))


def _split_dot(c_bf, x):
    hi = _bf(x)
    lo = _bf(x - hi.astype(F32))
    r2 = jnp.dot(c_bf, jnp.concatenate([hi, lo], axis=1), preferred_element_type=F32)
    return r2[:, :HEAD] + r2[:, HEAD:]


def _hg_gates(hq, hf, lb):
    sq = _sigmoid(hq)
    sg = _sigmoid(hf)
    fg = lb + (1.0 - lb) * sg
    return sq, hq * sq, sg, fg, 1.0 - fg, jnp.log(fg)


def hgrn_fwd(proj_main, lb, consts, bl, lp, d):
    nh = d // HEAD
    rows_blk = _tile(lp, 768, SEQ_BLOCK)
    nb = lp // rows_blk
    spb = rows_blk // HG_SUB
    cstack, _, pstack, _ = consts

    def body(hq_ref, hf_ref, hi_ref, lb_ref, c_ref, p_ref, o_ref, st_ref, a_ref, s_ref):
        j = pl.program_id(2)

        @pl.when(j == 0)
        def _():
            s_ref[...] = jnp.zeros_like(s_ref)

        lbv = lb_ref[...]
        cs = c_ref[...]
        rows = [pl.ds(s * HG_SUB, HG_SUB) for s in range(spb)]
        gates = [_hg_gates(hq_ref[r, :].astype(F32), hf_ref[r, :].astype(F32), lbv) for r in rows]
        qs, ks = [g_[1] for g_ in gates], [g_[4] for g_ in gates]
        vs = [hi_ref[r, :].astype(F32) for r in rows]
        es = [_split_dot(cs, g_[5]) for g_ in gates]
        a_acc = [jnp.zeros((HG_SUB, HG_SUB), F32) for _ in rows]
        for lvl in range(HG_LEVELS):
            for s in range(spb):
                x = jnp.exp(es[s][lvl * HG_SUB:(lvl + 1) * HG_SUB])
                a_acc[s] = a_acc[s] + p_ref[pl.ds(lvl * HG_SUB, HG_SUB), :] * lax.dot_general(
                    _bf(qs[s] * x), _bf(ks[s] * x), NT_DIMS, preferred_element_type=F32)
        o_intra, qbs, kds, e_lasts = [], [], [], []
        for s in range(spb):
            a_bf = _bf(a_acc[s])
            a_ref[0, 0, s] = a_bf
            bc = es[s][HG_LEVELS * HG_SUB:HG_E_ROWS]
            b_last = jnp.tile(es[s][HG_E_ROWS:], (HG_SUB // 8, 1))
            o_intra.append(jnp.dot(a_bf, _bf(vs[s]), preferred_element_type=F32)
                           + jnp.sum(qs[s] * ks[s], axis=1, keepdims=True) * vs[s])
            qbs.append(_bf(qs[s] * jnp.exp(bc)))
            kds.append(_bf(ks[s] * jnp.exp(b_last - bc)))
            e_lasts.append(jnp.exp(b_last))
        st = s_ref[...]
        for s in range(spb):
            st_ref[0, 0, s] = st
            o_ref[rows[s], :] = (o_intra[s] + lax.dot_general(qbs[s], _bf(st), NT_DIMS, preferred_element_type=F32)
                                 ).astype(o_ref.dtype)
            st = st * e_lasts[s] + lax.dot_general(_bf(vs[s]), kds[s], TN_DIMS, preferred_element_type=F32)
        s_ref[...] = st

    def colspec(off):
        return pl.BlockSpec((rows_blk, HEAD), functools.partial(lambda h, b, j, off: (b * nb + j, off + h), off=off))

    whole = lambda arr: pl.BlockSpec(arr.shape, lambda h, b, j: (0, 0))
    return pl.pallas_call(
        body, name="hgrn_fwd", grid=(nh, bl, nb),
        in_specs=[colspec(0), colspec(nh), colspec(2 * nh), pl.BlockSpec((1, HEAD), lambda h, b, j: (0, h)),
                  whole(cstack), whole(pstack)],
        out_specs=[pl.BlockSpec((rows_blk, HEAD), lambda h, b, j: (b * nb + j, h)),
                   pl.BlockSpec((1, 1, spb, HEAD, HEAD), lambda h, b, j: (b, h, j, 0, 0)),
                   pl.BlockSpec((1, 1, spb, HG_SUB, HG_SUB), lambda h, b, j: (b, h, j, 0, 0))],
        out_shape=[jax.ShapeDtypeStruct((bl * lp, d), BF16),
                   jax.ShapeDtypeStruct((bl, nh, lp // HG_SUB, HEAD, HEAD), F32),
                   jax.ShapeDtypeStruct((bl, nh, lp // HG_SUB, HG_SUB, HG_SUB), BF16)],
        scratch_shapes=[pltpu.VMEM((HEAD, HEAD), F32)],
        compiler_params=pltpu.CompilerParams(dimension_semantics=("arbitrary", "arbitrary", "arbitrary")),
    )(proj_main, proj_main, proj_main, lb, cstack, pstack)


def hgrn_bwd(proj_main, lb, consts, states, a_mats, do_scan, bl, lp, d):
    nh = d // HEAD
    rows_blk = _tile(lp, 768, SEQ_BLOCK)
    nb = lp // rows_blk
    spb = rows_blk // HG_SUB
    cstack, cstack_t = consts[0], consts[1]
    pstack, pstack_t = _bf(consts[2]), _bf(consts[3])

    def body(hq_ref, hf_ref, hi_ref, lb_ref, c_ref, ct_ref, p_ref, pt_ref, st_ref, a_ref, do_ref,
             dq_ref, df_ref, di_ref, dlb_ref, ds_ref):
        b_id, j = pl.program_id(1), pl.program_id(2)
        blk = nb - 1 - j

        @pl.when(j == 0)
        def _():
            ds_ref[...] = jnp.zeros_like(ds_ref)

        @pl.when((j == 0) & (b_id == 0))
        def _():
            dlb_ref[...] = jnp.zeros_like(dlb_ref)

        lbv = lb_ref[...]
        cs = c_ref[...]
        cst = ct_ref[...]

        dlb = jnp.zeros((1, HEAD), F32)
        for first in reversed(range(0, spb, HG_BWD_GROUP)):
            dlb = dlb + _hg_group_bwd(list(range(first, min(first + HG_BWD_GROUP, spb))), lbv, cs, cst, hq_ref,
                                      hf_ref, hi_ref, st_ref, a_ref, do_ref, p_ref, pt_ref, dq_ref, df_ref, di_ref,
                                      ds_ref)
        dlb_ref[...] += dlb

    def _hg_group_bwd(ids, lbv, cs, cst, hq_ref, hf_ref, hi_ref, st_ref, a_ref, do_ref, p_ref, pt_ref, dq_ref,
                      df_ref, di_ref, ds_ref):
        rng = range(len(ids))
        rows = [pl.ds(s * HG_SUB, HG_SUB) for s in ids]
        hqs = [hq_ref[r, :].astype(F32) for r in rows]
        gates = [_hg_gates(hqs[s], hf_ref[rows[s], :].astype(F32), lbv) for s in rng]
        sqs, qs, sgs, fgs, ks = ([g_[i] for g_ in gates] for i in range(5))
        vs = [hi_ref[r, :].astype(F32) for r in rows]
        dos = [do_ref[r, :].astype(F32) for r in rows]
        sts = [st_ref[0, 0, s] for s in ids]
        es = [_split_dot(cs, g_[5]) for g_ in gates]
        bcs = [e[HG_LEVELS * HG_SUB:HG_E_ROWS] for e in es]
        b_lasts = [jnp.tile(e[HG_E_ROWS:], (HG_SUB // 8, 1)) for e in es]
        ebs = [jnp.exp(bc) for bc in bcs]
        qbs = [qs[s] * ebs[s] for s in rng]
        ers = [jnp.exp(b_lasts[s] - bcs[s]) for s in rng]
        kds = [ks[s] * ers[s] for s in rng]
        e_lasts = [jnp.exp(b) for b in b_lasts]
        do_bfs, v_bfs = [_bf(x) for x in dos], [_bf(x) for x in vs]
        das = [_bf(lax.dot_general(do_bfs[s], v_bfs[s], NT_DIMS, preferred_element_type=F32)) for s in rng]
        dats = [_bf(lax.dot_general(v_bfs[s], do_bfs[s], NT_DIMS, preferred_element_type=F32)) for s in rng]
        dqbs = [jnp.dot(do_bfs[s], _bf(sts[s]), preferred_element_type=F32) for s in rng]
        m_s = [lax.dot_general(do_bfs[s], _bf(qbs[s]), TN_DIMS, preferred_element_type=F32) for s in rng]
        dst_outs = [None] * len(ids)
        dst = ds_ref[...]
        for s in reversed(rng):
            dst_outs[s] = dst
            dst = dst * e_lasts[s] + m_s[s]
        ds_ref[...] = dst
        dst_bfs = [_bf(x) for x in dst_outs]
        d_diags = [jnp.sum(dos[s] * vs[s], axis=1, keepdims=True) for s in rng]
        dvs = [lax.dot_general(a_ref[0, 0, ids[s]], do_bfs[s], TN_DIMS, preferred_element_type=F32)
               + jnp.sum(qs[s] * ks[s], axis=1, keepdims=True) * dos[s]
               + lax.dot_general(_bf(kds[s]), dst_bfs[s], NT_DIMS, preferred_element_type=F32) for s in rng]
        dkds = [jnp.dot(v_bfs[s], dst_bfs[s], preferred_element_type=F32) for s in rng]
        dqs = [dqbs[s] * ebs[s] + d_diags[s] * ks[s] for s in rng]
        dks = [dkds[s] * ers[s] + d_diags[s] * qs[s] for s in rng]
        d_lasts = [jnp.sum(dst_outs[s] * sts[s] * e_lasts[s], axis=0, keepdims=True)
                   + jnp.sum(dkds[s] * kds[s], axis=0, keepdims=True) for s in rng]
        des = [[] for _ in rng]
        for lvl in range(HG_LEVELS):
            for s in rng:
                x = jnp.exp(es[s][lvl * HG_SUB:(lvl + 1) * HG_SUB])
                qh, kh = qs[s] * x, ks[s] * x
                dm = p_ref[pl.ds(lvl * HG_SUB, HG_SUB), :] * das[s]
                dmt = pt_ref[pl.ds(lvl * HG_SUB, HG_SUB), :] * dats[s]
                dqh = jnp.dot(dm, _bf(kh), preferred_element_type=F32)
                dkh = jnp.dot(dmt, _bf(qh), preferred_element_type=F32)
                dqs[s] = dqs[s] + dqh * x
                dks[s] = dks[s] + dkh * x
                des[s].append(dqh * qh + dkh * kh)
        dlb = jnp.zeros((1, HEAD), F32)
        for s in rng:
            des[s].append(dqbs[s] * qbs[s] - dkds[s] * kds[s])
            dg = _split_dot(cst, jnp.concatenate(des[s], axis=0)) + d_lasts[s]
            dfg = dg / fgs[s] - dks[s]
            dq_ref[rows[s], :] = (dqs[s] * (sqs[s] * (1.0 + hqs[s] * (1.0 - sqs[s])))).astype(dq_ref.dtype)
            df_ref[rows[s], :] = (dfg * (1.0 - lbv) * sgs[s] * (1.0 - sgs[s])).astype(df_ref.dtype)
            di_ref[rows[s], :] = dvs[s].astype(di_ref.dtype)
            dlb = dlb + jnp.sum(dfg * (1.0 - sgs[s]), axis=0, keepdims=True)
        return dlb

    def colspec(off):
        return pl.BlockSpec((rows_blk, HEAD),
                            functools.partial(lambda h, b, j, off: (b * nb + nb - 1 - j, off + h), off=off))

    whole = lambda arr: pl.BlockSpec(arr.shape, lambda h, b, j: (0, 0))
    mats = lambda: pl.BlockSpec((1, 1, spb, HEAD, HEAD), lambda h, b, j: (b, h, nb - 1 - j, 0, 0))
    t_rows = bl * lp
    return pl.pallas_call(
        body, name="hgrn_bwd", grid=(nh, bl, nb),
        in_specs=[colspec(0), colspec(nh), colspec(2 * nh), pl.BlockSpec((1, HEAD), lambda h, b, j: (0, h)),
                  whole(cstack), whole(cstack_t), whole(pstack), whole(pstack_t), mats(), mats(), colspec(0)],
        out_specs=[colspec(0), colspec(0), colspec(0), pl.BlockSpec((1, HEAD), lambda h, b, j: (0, h))],
        out_shape=[jax.ShapeDtypeStruct((t_rows, d), BF16)] * 3 + [jax.ShapeDtypeStruct((1, d), F32)],
        scratch_shapes=[pltpu.VMEM((HEAD, HEAD), F32)],
        compiler_params=pltpu.CompilerParams(dimension_semantics=("arbitrary", "arbitrary", "arbitrary")),
    )(proj_main, proj_main, proj_main, lb, cstack, cstack_t, pstack, pstack_t, states, a_mats, do_scan)


def _key_query_mask(key0, qry0, nk, nq_, causal):
    key = key0 + lax.broadcasted_iota(jnp.int32, (nk, 1), 0)
    if not causal:
        return key >= PAD_FRONT
    qry = qry0 + lax.broadcasted_iota(jnp.int32, (1, nq_), 1)
    return (key <= qry) & (key >= PAD_FRONT)


def _attn_tile(lp):
    return _tile(lp, ATTN_TILE_MAX, SEQ_BLOCK)


def attn_fwd_t(q_cat, k_cat, v_t, bl, lp, nm):
    tq = tk = _attn_tile(lp)
    nq = lp // tq
    hp = ATTN_HEADS_PER_STEP
    assert nm % hp == 0

    def body(q_ref, k_ref, vt_ref, o_ref, lse_ref, m_ref, l_ref, acc_ref):
        i = pl.program_id(2)
        m_ref[...] = jnp.full_like(m_ref, NEG)
        l_ref[...] = jnp.zeros_like(l_ref)
        acc_ref[...] = jnp.zeros_like(acc_ref)

        def step(c, mask):
            c0 = pl.multiple_of(c * tk, tk)
            for hh in range(hp):
                cols = pl.ds(hh * QK_PAD, QK_PAD)
                st = lax.dot_general(k_ref[pl.ds(c0, tk), cols], q_ref[:, cols], NT_DIMS,
                                     preferred_element_type=F32)
                if mask is not None:
                    st = jnp.where(_key_query_mask(c * tk, i * tq, tk, tq, mask == "causal"), st, NEG)
                m_old = m_ref[hh]
                m_new = jnp.maximum(m_old, jnp.max(st, axis=0, keepdims=True))
                alpha = jnp.exp(m_old - m_new)
                pt = jnp.exp(st - m_new)
                l_ref[hh] = alpha * l_ref[hh] + jnp.sum(pt, axis=0, keepdims=True)
                acc_ref[hh] = alpha * acc_ref[hh] + jnp.dot(vt_ref[0, hh, pl.ds(c, 1)][0], _bf(pt),
                                                            preferred_element_type=F32)
                m_ref[hh] = m_new

        def mid(c, carry):
            step(c, None)
            return carry

        @pl.when(i == 0)
        def _():
            step(0, "causal")

        @pl.when(i > 0)
        def _():
            step(0, "pad")
            lax.fori_loop(1, i, mid, 0)
            step(i, "causal")

        for hh in range(hp):
            o_ref[:, pl.ds(hh * HEAD, HEAD)] = jnp.transpose(acc_ref[hh] / l_ref[hh]).astype(o_ref.dtype)
            lse_ref[0, hh, 0] = m_ref[hh] + jnp.log(l_ref[hh])

    return pl.pallas_call(
        body, name="attn_fwd", grid=(bl, nm // hp, nq),
        in_specs=[pl.BlockSpec((tq, hp * QK_PAD), lambda b, h, i: (b * nq + i, h)),
                  pl.BlockSpec((lp, hp * QK_PAD), lambda b, h, i: (b, h)),
                  pl.BlockSpec((1, hp, nq, HEAD, tk), lambda b, h, i: (b, h, 0, 0, 0))],
        out_specs=[pl.BlockSpec((tq, hp * HEAD), lambda b, h, i: (b * nq + i, h)),
                   pl.BlockSpec((1, hp, 1, 1, tq), lambda b, h, i: (b, h, i, 0, 0))],
        out_shape=[jax.ShapeDtypeStruct((bl * lp, nm * HEAD), BF16),
                   jax.ShapeDtypeStruct((bl, nm, nq, 1, tq), F32)],
        scratch_shapes=[pltpu.VMEM((hp, 1, tq), F32), pltpu.VMEM((hp, 1, tq), F32), pltpu.VMEM((hp, HEAD, tq), F32)],
        compiler_params=pltpu.CompilerParams(dimension_semantics=("arbitrary", "arbitrary", "arbitrary")),
    )(q_cat, k_cat, v_t)


def attn_bwd_t(q_cat, k_cat, k_t, v, o, do, lse, bl, lp, nm):
    tq = tk = _attn_tile(lp)
    nq = lp // tq
    hp = ATTN_HEADS_PER_STEP
    assert nm % hp == 0

    def body(q_ref, k_ref, kt_ref, v_ref, o_ref, do_ref, lse_ref, dq_ref, dk_ref, dv_ref, dqt_ref, dka_ref, dva_ref):
        i = pl.program_id(2)

        @pl.when(i == 0)
        def _():
            dqt_ref[...] = jnp.zeros_like(dqt_ref)

        dka_ref[...] = jnp.zeros_like(dka_ref)
        dva_ref[...] = jnp.zeros_like(dva_ref)
        ones8 = jnp.ones((8, HEAD), BF16)

        def step(c, mask):
            c0 = pl.multiple_of(c * tq, tq)
            for hh in range(hp):
                qcols, vcols = pl.ds(hh * QK_PAD, QK_PAD), pl.ds(hh * HEAD, HEAD)
                qs = q_ref[pl.ds(c0, tq), qcols]
                dos = do_ref[pl.ds(c0, tq), vcols]
                prod = dos.astype(F32) * o_ref[pl.ds(c0, tq), vcols].astype(F32)
                hi = _bf(prod)
                lo = _bf(prod - hi.astype(F32))
                delta8 = (lax.dot_general(ones8, hi, NT_DIMS, preferred_element_type=F32)
                          + lax.dot_general(ones8, lo, NT_DIMS, preferred_element_type=F32))
                st = lax.dot_general(k_ref[:, qcols], qs, NT_DIMS, preferred_element_type=F32)
                pt = jnp.exp(st - lse_ref[0, hh, pl.ds(c, 1)][0])
                if mask is not None:
                    pt = jnp.where(_key_query_mask(i * tk, c * tq, tk, tq, mask == "causal"), pt, 0.0)
                dva_ref[hh] += jnp.dot(_bf(pt), dos, preferred_element_type=F32)
                dpt = lax.dot_general(v_ref[:, vcols], dos, NT_DIMS, preferred_element_type=F32)
                dst = _bf(pt * (dpt - jnp.tile(delta8, (tk // 8, 1))))
                dka_ref[hh] += jnp.dot(dst, qs, preferred_element_type=F32)
                dqt_ref[hh, pl.ds(c, 1)] += jnp.dot(kt_ref[0, hh, 0], dst, preferred_element_type=F32)[None]

        step(i, "causal")

        def rest_masked(c, carry):
            step(c, "pad")
            return carry

        def rest(c, carry):
            step(c, None)
            return carry

        @pl.when(i == 0)
        def _():
            lax.fori_loop(1, nq, rest_masked, 0)

        @pl.when(i > 0)
        def _():
            lax.fori_loop(i + 1, nq, rest, 0)

        for hh in range(hp):
            dk_ref[:, pl.ds(hh * QK_PAD, QK_PAD)] = dka_ref[hh].astype(dk_ref.dtype)
            dv_ref[:, pl.ds(hh * HEAD, HEAD)] = dva_ref[hh].astype(dv_ref.dtype)

        @pl.when(i == nq - 1)
        def _():
            for hh in range(hp):
                for c in range(nq):
                    dq_ref[pl.ds(c * tq, tq), pl.ds(hh * QK_PAD, QK_PAD)] = (
                        jnp.transpose(dqt_ref[hh, c])).astype(dq_ref.dtype)

    return pl.pallas_call(
        body, name="attn_bwd", grid=(bl, nm // hp, nq),
        in_specs=[pl.BlockSpec((lp, hp * QK_PAD), lambda b, h, i: (b, h)),
                  pl.BlockSpec((tk, hp * QK_PAD), lambda b, h, i: (b * nq + i, h)),
                  pl.BlockSpec((1, hp, 1, QK_PAD, tk), lambda b, h, i: (b, h, i, 0, 0)),
                  pl.BlockSpec((tk, hp * HEAD), lambda b, h, i: (b * nq + i, h)),
                  pl.BlockSpec((lp, hp * HEAD), lambda b, h, i: (b, h)),
                  pl.BlockSpec((lp, hp * HEAD), lambda b, h, i: (b, h)),
                  pl.BlockSpec((1, hp, nq, 1, tq), lambda b, h, i: (b, h, 0, 0, 0))],
        out_specs=[pl.BlockSpec((lp, hp * QK_PAD), lambda b, h, i: (b, h)),
                   pl.BlockSpec((tk, hp * QK_PAD), lambda b, h, i: (b * nq + i, h)),
                   pl.BlockSpec((tk, hp * HEAD), lambda b, h, i: (b * nq + i, h))],
        out_shape=[jax.ShapeDtypeStruct((bl * lp, nm * QK_PAD), BF16),
                   jax.ShapeDtypeStruct((bl * lp, nm * QK_PAD), BF16),
                   jax.ShapeDtypeStruct((bl * lp, nm * HEAD), BF16)],
        scratch_shapes=[pltpu.VMEM((hp, nq, QK_PAD, tq), F32), pltpu.VMEM((hp, tk, QK_PAD), F32),
                        pltpu.VMEM((hp, tk, HEAD), F32)],
        compiler_params=pltpu.CompilerParams(dimension_semantics=("arbitrary", "arbitrary", "arbitrary")),
    )(q_cat, k_cat, k_t, v, o, do, lse)


def _place():
    return lax.axis_index("x"), lax.axis_index("y"), lax.axis_index("c")


def gather_shards(packed):
    hbm = pl.BlockSpec(memory_space=pl.ANY)

    def body(src_ref, out_ref, send_sems, recv_sems, local_sem):
        x, y, c = _place()
        me = 2 * x + y
        chips = [(1 - x, y), (x, 1 - y), (1 - x, 1 - y)]
        local = pltpu.make_async_copy(src_ref, out_ref.at[me], local_sem)
        local.start()
        sends = []
        for k, (px, py) in enumerate(chips):
            cp = pltpu.make_async_remote_copy(src_ref=src_ref, dst_ref=out_ref.at[me], send_sem=send_sems.at[k],
                                              recv_sem=recv_sems.at[k], device_id=(px, py, c), device_id_type=MESH)
            cp.start()
            sends.append(cp)
        for k, (px, py) in enumerate(chips):
            pltpu.make_async_remote_copy(src_ref=src_ref, dst_ref=out_ref.at[2 * px + py], send_sem=send_sems.at[k],
                                         recv_sem=recv_sems.at[k], device_id=(px, py, c),
                                         device_id_type=MESH).wait_recv()
        for cp in sends:
            cp.wait_send()
        local.wait()

    return pl.pallas_call(
        body, name="gather_shards", in_specs=[hbm], out_specs=hbm,
        out_shape=jax.ShapeDtypeStruct((4,) + packed.shape, packed.dtype),
        scratch_shapes=[pltpu.SemaphoreType.DMA((3,)), pltpu.SemaphoreType.DMA((3,)), pltpu.SemaphoreType.DMA],
    )(packed)


def gather_small(small):
    hbm = pl.BlockSpec(memory_space=pl.ANY)

    def body(small_ref, all_ref, send_sems, recv_sems, local_sem):
        x, y, c = _place()
        me = 4 * x + 2 * y + c
        local = pltpu.make_async_copy(small_ref, all_ref.at[me], local_sem)
        local.start()
        others = [(x ^ ((r >> 2) & 1), y ^ ((r >> 1) & 1), c ^ (r & 1)) for r in range(1, 8)]
        sends = []
        for r, peer in enumerate(others):
            cp = pltpu.make_async_remote_copy(src_ref=small_ref, dst_ref=all_ref.at[me], send_sem=send_sems.at[r],
                                              recv_sem=recv_sems.at[r], device_id=peer, device_id_type=MESH)
            cp.start()
            sends.append(cp)
        for r, (px, py, pc) in enumerate(others):
            pltpu.make_async_remote_copy(src_ref=small_ref, dst_ref=all_ref.at[4 * px + 2 * py + pc],
                                         send_sem=send_sems.at[r], recv_sem=recv_sems.at[r],
                                         device_id=(px, py, pc), device_id_type=MESH).wait_recv()
        for cp in sends:
            cp.wait_send()
        local.wait()

    return pl.pallas_call(
        body, name="gather_small", in_specs=[hbm], out_specs=hbm,
        out_shape=jax.ShapeDtypeStruct((8,) + small.shape, small.dtype),
        scratch_shapes=[pltpu.SemaphoreType.DMA((7,)), pltpu.SemaphoreType.DMA((7,)), pltpu.SemaphoreType.DMA],
    )(small)


def swap_with_sibling(name, parts):
    n = len(parts)
    hbm = pl.BlockSpec(memory_space=pl.ANY)

    def body(*refs):
        x, y, c = _place()
        cps = [pltpu.make_async_remote_copy(src_ref=refs[j], dst_ref=refs[n + j], send_sem=refs[2 * n].at[j],
                                            recv_sem=refs[2 * n + 1].at[j], device_id=(x, y, 1 - c),
                                            device_id_type=MESH) for j in range(n)]
        for cp in cps:
            cp.start()
        for cp in cps:
            cp.wait()

    return pl.pallas_call(
        body, name=name, in_specs=[hbm] * n, out_specs=[hbm] * n,
        out_shape=[jax.ShapeDtypeStruct(p.shape, p.dtype) for p in parts],
        scratch_shapes=[pltpu.SemaphoreType.DMA((n,)), pltpu.SemaphoreType.DMA((n,))],
    )(*parts)


def _chips3():
    x, y, c = _place()
    return [(1 - x, y, c), (x, 1 - y, c), (1 - x, 1 - y, c)]


def _push_copies(src_refs, land_refs, send_sems, recv_sems, per_chip):
    cps = []
    for j, (src_ref, land_ref) in enumerate(zip(src_refs, land_refs)):
        for k, (px, py, pc) in enumerate(_chips3()):
            part = src_ref.at[2 * px + py] if per_chip else src_ref
            cps.append(pltpu.make_async_remote_copy(
                src_ref=part, dst_ref=land_ref.at[k], send_sem=send_sems.at[3 * j + k],
                recv_sem=recv_sems.at[3 * j + k], device_id=(px, py, pc), device_id_type=MESH))
    return cps


def push_start(name, srcs, per_chip):
    n = len(srcs)
    hbm = pl.BlockSpec(memory_space=pltpu.HBM)
    sem = pl.BlockSpec(memory_space=pltpu.SEMAPHORE)
    lands = [lax.empty((3,) + s.shape[-2:], s.dtype) for s in srcs]

    def body(*refs):
        src_refs, land_refs = refs[:n], refs[n:2 * n]
        send_sems, recv_sems = refs[2 * n], refs[2 * n + 1]
        for cp in _push_copies(src_refs, land_refs, send_sems, recv_sems, per_chip):
            cp.start()
        refs[-1][...] = jnp.zeros_like(refs[-1])

    outs = pl.pallas_call(
        body, name=name,
        out_shape=(pltpu.SemaphoreType.DMA((3 * n,)), pltpu.SemaphoreType.DMA((3 * n,)),
                   *[pltpu.HBM(a.shape, a.dtype) for a in list(srcs) + lands], jax.ShapeDtypeStruct((8, HEAD), F32)),
        in_specs=(hbm,) * (2 * n),
        out_specs=(sem, sem) + (hbm,) * (2 * n) + (pl.BlockSpec(memory_space=pltpu.VMEM),),
        input_output_aliases={j: 2 + j for j in range(2 * n)},
        compiler_params=pltpu.CompilerParams(has_side_effects=pltpu.SideEffectType.DATAFLOW_SIDE_EFFECTING),
    )(*[pltpu.with_memory_space_constraint(a, pltpu.HBM) for a in list(srcs) + lands])
    return tuple(outs[:-1]), outs[-1]


def push_wait(name, handle, after, per_chip):
    send_sems, recv_sems = handle[0], handle[1]
    thru = handle[2:]
    n = len(thru) // 2
    hbm = pl.BlockSpec(memory_space=pltpu.HBM)
    sem = pl.BlockSpec(memory_space=pltpu.SEMAPHORE)

    def body(*refs):
        src_refs, land_refs = refs[:n], refs[n:2 * n]
        for cp in _push_copies(src_refs, land_refs, refs[2 * n], refs[2 * n + 1], per_chip):
            cp.wait_send()
            cp.wait_recv()

    outs = pl.pallas_call(
        body, name=name,
        out_shape=tuple(pltpu.HBM(a.shape, a.dtype) for a in thru),
        in_specs=(hbm,) * (2 * n) + (sem, sem, pl.BlockSpec(memory_space=pl.ANY)), out_specs=(hbm,) * (2 * n),
        input_output_aliases={j: j for j in range(2 * n)},
        compiler_params=pltpu.CompilerParams(has_side_effects=pltpu.SideEffectType.DATAFLOW_SIDE_EFFECTING),
    )(*thru, send_sems, recv_sems, after)
    return outs[:n], outs[n:]


def by_chip(own, landed, my_chip):
    by_rel = jnp.stack([own, landed[1], landed[0], landed[2]])
    return [lax.dynamic_index_in_dim(by_rel, jnp.bitwise_xor(s, my_chip), axis=0, keepdims=False) for s in range(4)]


def adamw(name, w, g_parts, m, v):
    r, c = w.shape
    tr = r if r * c <= 65536 else _tile(r, 128, 8)
    ng = len(g_parts)

    def body(*refs):
        w_ref, m_ref, v_ref = refs[0], refs[1 + ng], refs[2 + ng]
        g_ref, d_ref, nm_ref, nv_ref = refs[3 + ng:]
        gv = refs[1][...]
        for k in range(1, ng):
            gv = gv + refs[1 + k][...]
        m_new = ADAM_B1 * m_ref[...] + (1.0 - ADAM_B1) * gv
        v_new = ADAM_B2 * v_ref[...] + (1.0 - ADAM_B2) * (gv * gv)
        m_hat = m_new / (1.0 - ADAM_B1 ** ADAM_STEP)
        v_hat = v_new / (1.0 - ADAM_B2 ** ADAM_STEP)
        g_ref[...] = gv
        d_ref[...] = -ADAM_LR * (m_hat / (jnp.sqrt(v_hat) + ADAM_EPS) + ADAM_WD * w_ref[...])
        nm_ref[...] = m_new
        nv_ref[...] = v_new

    spec = pl.BlockSpec((tr, c), lambda i: (i, 0))
    return pl.pallas_call(
        body, name=name, grid=(r // tr,), in_specs=[spec] * (3 + ng), out_specs=[spec] * 4,
        out_shape=[jax.ShapeDtypeStruct((r, c), F32)] * 4,
        compiler_params=pltpu.CompilerParams(dimension_semantics=("arbitrary",)),
    )(w, *g_parts, m, v)


def split_full(name, full, s):
    if name in COL_SHARDED:
        c = full.shape[1] // 4
        return full[:, s * c:(s + 1) * c]
    r = full.shape[0] // 4
    return full[s * r:(s + 1) * r]


def join_shards(name, shards):
    return jnp.concatenate(shards, axis=1 if name in COL_SHARDED else 0)


def kernel(x, meta_tokens, w_in, b_gate, lb_logits, hg_norm_g, w_hg_o, q_a_norm_g, w_q_b, kv_a_norm_g, w_kv_b, w_mla_o, w_out, mix_pre_g, mix_post_g, ffn_pre_g, ffn_post_g, w_ffn_in, w_ffn_out, loss_target, m_meta_tokens, m_w_in, m_b_gate, m_lb_logits, m_hg_norm_g, m_w_hg_o, m_q_a_norm_g, m_w_q_b, m_kv_a_norm_g, m_w_kv_b, m_w_mla_o, m_w_out, m_mix_pre_g, m_mix_post_g, m_ffn_pre_g, m_ffn_post_g, m_w_ffn_in, m_w_ffn_out, v_meta_tokens, v_w_in, v_b_gate, v_lb_logits, v_hg_norm_g, v_w_hg_o, v_q_a_norm_g, v_w_q_b, v_kv_a_norm_g, v_w_kv_b, v_w_mla_o, v_w_out, v_mix_pre_g, v_mix_post_g, v_ffn_pre_g, v_ffn_post_g, v_w_ffn_in, v_w_ffn_out):
    wts = dict(meta_tokens=meta_tokens, w_in=w_in[0], b_gate=b_gate, lb_logits=lb_logits, hg_norm_g=hg_norm_g,
               w_hg_o=w_hg_o[0], q_a_norm_g=q_a_norm_g, w_q_b=w_q_b[0], kv_a_norm_g=kv_a_norm_g, w_kv_b=w_kv_b[0],
               w_mla_o=w_mla_o[0], w_out=w_out[0], mix_pre_g=mix_pre_g, mix_post_g=mix_post_g, ffn_pre_g=ffn_pre_g,
               ffn_post_g=ffn_post_g, w_ffn_in=w_ffn_in[0], w_ffn_out=w_ffn_out[0])
    mom_m = dict(meta_tokens=m_meta_tokens, w_in=m_w_in[0], b_gate=m_b_gate, lb_logits=m_lb_logits,
                 hg_norm_g=m_hg_norm_g, w_hg_o=m_w_hg_o[0], q_a_norm_g=m_q_a_norm_g, w_q_b=m_w_q_b[0],
                 kv_a_norm_g=m_kv_a_norm_g, w_kv_b=m_w_kv_b[0], w_mla_o=m_w_mla_o[0], w_out=m_w_out[0],
                 mix_pre_g=m_mix_pre_g, mix_post_g=m_mix_post_g, ffn_pre_g=m_ffn_pre_g, ffn_post_g=m_ffn_post_g,
                 w_ffn_in=m_w_ffn_in[0], w_ffn_out=m_w_ffn_out[0])
    mom_v = dict(meta_tokens=v_meta_tokens, w_in=v_w_in[0], b_gate=v_b_gate, lb_logits=v_lb_logits,
                 hg_norm_g=v_hg_norm_g, w_hg_o=v_w_hg_o[0], q_a_norm_g=v_q_a_norm_g, w_q_b=v_w_q_b[0],
                 kv_a_norm_g=v_kv_a_norm_g, w_kv_b=v_w_kv_b[0], w_mla_o=v_w_mla_o[0], w_out=v_w_out[0],
                 mix_pre_g=v_mix_pre_g, mix_post_g=v_mix_post_g, ffn_pre_g=v_ffn_pre_g, ffn_post_g=v_ffn_post_g,
                 w_ffn_in=v_w_ffn_in[0], w_ffn_out=v_w_ffn_out[0])

    bl, seq, d = x.shape
    lp = PAD_FRONT + N_META + seq
    t_rows = bl * lp
    nh = d // HEAD
    ql, kvl = wts["w_q_b"].shape[0], wts["w_kv_b"].shape[0]
    nm = (4 * wts["w_mla_o"].shape[0]) // HEAD
    ffn = 4 * wts["w_ffn_out"].shape[0]
    mla_w = ql + kvl + HEAD
    assert ql == kvl and ql % HEAD == 0 and seq % SEQ_BLOCK == 0 and d % HEAD == 0
    scale = (HEAD + ROPE) ** -0.5
    my_chip = 2 * lax.axis_index("x") + lax.axis_index("y")

    mcols = meta_tokens.shape[1]
    meta_all = gather_shards(meta_tokens)
    meta_full = jnp.concatenate([meta_all[s] for s in range(4)], axis=1)

    def start_gather(name, names, order_after):
        srcs = [_bf(wts[n]) for n in names]
        if order_after is not None:
            srcs[0] = srcs[0] + order_after[0, 0].astype(BF16)
        return push_start(name, srcs, per_chip=False)

    def finish_gather(name, names, started, after):
        owns, landed = push_wait(name, started[0], after, per_chip=False)
        return {n: join_shards(n, by_chip(own, land, my_chip)) for n, own, land in zip(names, owns, landed)}

    rest_names = tuple(n for n in BIG if n != "w_in")
    my_c = lax.axis_index("c")
    w_in_bf = _bf(wts["w_in"])
    half = w_in_bf.shape[0] // 2
    own_half = (lax.dynamic_slice_in_dim(w_in_bf, my_c * half, half, axis=0)
                + (meta_all[0, :1, :1] * 0.0)[0, 0].astype(BF16))
    gather_1 = push_start("gather_w_in_start", [own_half], per_chip=False)
    gather_2 = start_gather("gather_rest_start", rest_names, gather_1[1])

    h0 = jnp.concatenate([jnp.zeros((bl, PAD_FRONT, d), F32), jnp.broadcast_to(meta_full[None], (bl, N_META, d)), x],
                         axis=1).reshape(t_rows, d)
    tiles_seq, tiles_real = lp // SEQ_BLOCK, seq // SEQ_BLOCK
    assert PAD_FRONT + N_META == SEQ_BLOCK

    def real_block(i):
        return (i // tiles_seq) * tiles_real + jnp.maximum(i % tiles_seq - 1, 0)

    meta_rows = jnp.broadcast_to(((jnp.arange(lp) >= PAD_FRONT) & (jnp.arange(lp) < PAD_FRONT + N_META)
                                  ).astype(F32)[:, None], (lp, HEAD))
    pos = (jnp.arange(lp, dtype=jnp.int32) - PAD_FRONT).astype(F32)
    inv_freq = 1.0 / (ROPE_THETA ** (jnp.arange(0, ROPE, 2, dtype=F32) / ROPE))
    ang = pos[:, None] * inv_freq[None, :]
    zeros32 = jnp.zeros((lp, ROPE_HALF), F32)
    zeros64 = jnp.zeros((lp, HEAD - ROPE), F32)
    t_cos = jnp.concatenate([jnp.cos(ang), jnp.cos(ang), zeros64], axis=1)
    t_up = jnp.concatenate([zeros32, jnp.sin(ang), zeros64], axis=1)
    t_dn = jnp.concatenate([-jnp.sin(ang), zeros32, zeros64], axis=1)
    real = jnp.broadcast_to((jnp.arange(lp) >= PAD_FRONT + N_META).astype(F32)[:, None], (lp, d))
    lb_soft = jax.nn.softmax(lb_logits.astype(F32), axis=0)
    lb = lb_soft[0:1]

    (u1,) = rowwise("norm_mix_pre", lambda h, g: _rms(h, g), [(h0, d, 0)], [], [mix_pre_g + gather_2[1][0, 0]],
                    [(d, BF16)])
    _, (fetched,) = push_wait("gather_w_in_wait", gather_1[0], u1, per_chip=False)
    (handed,) = swap_with_sibling("swap_w_in", [fetched])
    halves = jnp.stack([fetched, handed])
    remote = jnp.concatenate([lax.dynamic_index_in_dim(halves, my_c, 0, keepdims=False),
                              lax.dynamic_index_in_dim(halves, 1 - my_c, 0, keepdims=False)], axis=1)
    full = {"w_in": join_shards("w_in", by_chip(w_in_bf, remote, my_chip))}
    w_main = jnp.concatenate([full["w_in"][:, :4 * d], full["w_in"][:, -2 * d:]], axis=1)
    w_mla = jnp.pad(full["w_in"][:, 4 * d:4 * d + ql + kvl + ROPE], ((0, 0), (0, HEAD - ROPE)))
    proj_main = matmul("proj_main", u1, w_main, "nn", out_dtype=BF16)
    proj_mla = matmul("proj_mla", u1, w_mla, "nn", out_dtype=BF16)
    hg_consts = _hg_constants()
    o_scan, states, a_mats = hgrn_fwd(proj_main, lb, hg_consts, bl, lp, d)

    def hg_out_fn(o, hg, g):
        return jnp.concatenate([_rms(o[:, h * HEAD:(h + 1) * HEAD], g) for h in range(nh)], axis=1) * _silu(hg)

    (o_hg,) = rowwise("hgrn_out", hg_out_fn, [(o_scan, d, 0), (proj_main, d, 3)], [], [hg_norm_g], [(d, BF16)])
    full.update(finish_gather("gather_rest_wait", rest_names, gather_2, o_hg))
    w_qb = jnp.pad(full["w_q_b"].reshape(ql, nm, HEAD + ROPE), ((0, 0), (0, 0), (0, QK_PAD - HEAD - ROPE))
                   ).reshape(ql, nm * QK_PAD)
    w_kvb = full["w_kv_b"]
    y_a = matmul("y_a", o_hg, _bf(full["w_hg_o"]), "nn", out_dtype=BF16)

    qn, kvn = rowwise("mla_norms", lambda cq, ckv, gq, gk: (_rms(cq, gq), _rms(ckv, gk)),
                      [(proj_mla, ql, 0), (proj_mla, kvl, 1)], [], [q_a_norm_g, kv_a_norm_g],
                      [(ql, BF16), (kvl, BF16)])
    q_full = matmul("q_up", qn, w_qb, "nn", out_dtype=BF16)
    kv_full = matmul("kv_up", kvn, w_kvb, "nn", out_dtype=BF16)

    def mla_prep_fn(qf, kvf, kpe, cos, s_up, s_dn):
        qf = qf * scale
        kpe_r = _rope(kpe, cos, s_up, s_dn)
        qs, ks, vs = [], [], []
        for h in range(nm):
            qs += [qf[:, h * QK_PAD:h * QK_PAD + HEAD], _rope(qf[:, h * QK_PAD + HEAD:(h + 1) * QK_PAD], cos, s_up, s_dn)]
            ks += [kvf[:, h * QK_PAD:h * QK_PAD + HEAD], kpe_r]
            vs += [kvf[:, h * QK_PAD + HEAD:(h + 1) * QK_PAD]]
        return jnp.concatenate(qs, axis=1), jnp.concatenate(ks, axis=1), jnp.concatenate(vs, axis=1)

    kpe_blk = (ql + kvl) // HEAD
    q_cat, k_cat, v_att = rowwise("mla_prep", mla_prep_fn,
                                  [(q_full, nm * QK_PAD, 0), (kv_full, nm * QK_PAD, 0), (proj_mla, HEAD, kpe_blk)],
                                  [t_cos, t_up, t_dn], [], [(nm * QK_PAD, BF16), (nm * QK_PAD, BF16), (nm * HEAD, BF16)])
    at = _attn_tile(lp)
    v_t = v_att.reshape(bl, lp // at, at, nm, HEAD).transpose(0, 3, 1, 4, 2)
    k_t = k_cat.reshape(bl, lp // at, at, nm, QK_PAD).transpose(0, 3, 1, 4, 2)
    o_mla, lse = attn_fwd_t(q_cat, k_cat, v_t, bl, lp, nm)
    y_b = matmul("y_b", o_mla, _bf(full["w_mla_o"]), "nn", out_dtype=BF16)

    def gate_fn(ya, yb, ga, gb, bias):
        zv = _sigmoid(ga + bias[:, :d]) * ya + _sigmoid(gb + bias[:, d:]) * yb
        return zv, zv

    mixed, z = matmul_fused("gate_mix_out", gate_fn,
                            [(y_a, d, 0), (y_b, d, 0), (proj_main, d, 4), (proj_main, d, 5)], [b_gate],
                            _bf(full["w_out"]), [(0, d)], "nn", [(d, BF16)], tm=512)

    def mid_fn(h, mx, g_post, g_pre):
        h1 = h + _rms(mx, g_post)
        return h1, _rms(h1, g_pre)

    h1, u2 = rowwise("norm_mid", mid_fn, [(h0, d, 0), (mixed, d, 0)], [], [mix_post_g, ffn_pre_g],
                     [(d, F32), (d, BF16)])
    gu = matmul("ffn_in", u2, _bf(full["w_ffn_in"]), "nn", out_dtype=BF16)
    def swiglu_fn(gt, up):
        a = _silu(gt) * up
        return a, a

    f_out, act = matmul_fused("swiglu_ffn_out", swiglu_fn, [(gu, ffn, 0), (gu, ffn, 1)], [],
                              _bf(full["w_ffn_out"]), [(0, ffn)], "nn", [(ffn, BF16)])

    def loss_fn(h1v, fv, tg, realv, g_post):
        h2 = h1v + _rms(fv, g_post)
        diff = (h2 - tg) * realv
        part = jnp.broadcast_to(0.5 * jnp.sum(diff * diff, keepdims=True) / d, (1, HEAD))
        dy = diff / d
        df, dg = _rms_bwd(fv, g_post, dy)
        return dy, df, part, dg

    dy, df, loss_part, g_ffn_post = rowwise(
        "loss_head", loss_fn,
        [(h1, d, 0), (f_out, d, 0), (loss_target.reshape(bl * seq, d), d, 0, real_block)], [real],
        [ffn_post_g], [(d, BF16), (d, BF16)], [(1, HEAD), (1, d)])
    grads = {}
    d_act = matmul("d_act", df, _bf(full["w_ffn_out"]), "nt", out_dtype=BF16)
    grads["w_ffn_out"] = matmul("gw_ffn_out", act, df, "tn")

    def swiglu_bwd_fn(gt, up, da):
        dgt, dup = da * up * _silu_grad(gt), da * _silu(gt)
        return dgt, dup, jnp.concatenate([dgt, dup], axis=1)

    du2, dgu = matmul_fused("swiglu_bwd_d_u2", swiglu_bwd_fn, [(gu, ffn, 0), (gu, ffn, 1), (d_act, ffn, 0)], [],
                            _bf(full["w_ffn_in"]), [(0, ffn), (ffn, 2 * ffn)], "nt", [(2 * ffn, BF16)])
    grads["w_ffn_in"] = matmul("gw_ffn_in", u2, dgu, "tn")

    def mid_bwd_fn(dyv, h1v, du2v, mx, g_pre, g_post):
        dx, dg_pre = _rms_bwd(h1v, g_pre, du2v)
        dh1 = dyv + dx
        dmx, dg_post = _rms_bwd(mx, g_post, dh1)
        return dh1, dmx, dg_pre, dg_post

    dh1, dmixed, g_ffn_pre, g_mix_post = rowwise("norm_mid_bwd", mid_bwd_fn,
                                                 [(dy, d, 0), (h1, d, 0), (du2, d, 0), (mixed, d, 0)], [],
                                                 [ffn_pre_g, mix_post_g], [(d, BF16), (d, BF16)], [(1, d), (1, d)])
    dz = matmul("d_z", dmixed, _bf(full["w_out"]), "nt", out_dtype=BF16)
    grads["w_out"] = matmul("gw_out", z, dmixed, "tn")

    def gate_bwd_fn(dzv, ya, yb, ga, gb, bias):
        sa, sb = _sigmoid(ga + bias[:, :d]), _sigmoid(gb + bias[:, d:])
        dga = dzv * ya * sa * (1.0 - sa)
        dgb = dzv * yb * sb * (1.0 - sb)
        dgates = jnp.concatenate([dga, dgb], axis=1)
        return dzv * sa, dzv * sb, dgates, jnp.sum(dgates, axis=0, keepdims=True)

    dy_a, dy_b, dgates, g_b_gate = rowwise("gate_mix_bwd", gate_bwd_fn,
                                           [(dz, d, 0), (y_a, d, 0), (y_b, d, 0), (proj_main, d, 4), (proj_main, d, 5)],
                                           [], [b_gate], [(d, BF16), (d, BF16), (2 * d, BF16)], [(1, 2 * d)])
    do_hg = matmul("d_o_hg", dy_a, _bf(full["w_hg_o"]), "nt", out_dtype=BF16)
    grads["w_hg_o"] = matmul("gw_hg_o", o_hg, dy_a, "tn")
    do_mla = matmul("d_o_mla", dy_b, _bf(full["w_mla_o"]), "nt", out_dtype=BF16)
    grads["w_mla_o"] = matmul("gw_mla_o", o_mla, dy_b, "tn")

    early = ("w_hg_o", "w_mla_o", "w_out", "w_ffn_in", "w_ffn_out")
    late = ("w_in", "w_q_b", "w_kv_b")

    def start_grads(name, names):
        sends = [_bf(jnp.stack([split_full(n, grads[n], s) for s in range(4)])) for n in names]
        mines = []
        for n in names:
            r, c = wts[n].shape
            axis, size = (1, c) if n in COL_SHARDED else (0, r)
            mines.append(lax.dynamic_slice_in_dim(grads[n], my_chip * size, size, axis=axis))
        handle, token = push_start(name, sends, per_chip=True)
        return handle, token, mines

    def finish_grads(tag, names, started, after):
        handle, _, mines = started
        _, landed = push_wait(f"grads_{tag}_wait", handle, after, per_chip=True)
        parts = []
        for n, mine, land in zip(names, mines, landed):
            r, c = mine.shape
            tr = _tile(r, 256, 16)
            land2 = land.reshape(3 * r, c)
            parts.append(rowwise(f"sum_chips_{n}", lambda a, r0, r1, r2: a + r0 + r1 + r2,
                                 [(mine, c, 0)] + [(land2, c, 0, k * (r // tr)) for k in range(3)],
                                 [], [], [(c, F32)], tm=tr)[0])
        sibs = swap_with_sibling(f"swap_{tag}", parts)
        return {n: [p, s] for n, p, s in zip(names, parts, sibs)}

    grads_early = start_grads("grads_early_start", early)
    token_a = grads_early[1]

    def hg_out_bwd_fn(do, o, hg, g):
        sg = _silu(hg)
        dn = do * sg
        dos, dgs, ons = [], 0.0, []
        for h in range(nh):
            sl = slice(h * HEAD, (h + 1) * HEAD)
            dx, dg = _rms_bwd(o[:, sl], g, dn[:, sl])
            dos.append(dx)
            dgs = dgs + dg
            ons.append(_rms(o[:, sl], g))
        dhg = do * jnp.concatenate(ons, axis=1) * _silu_grad(hg)
        return jnp.concatenate(dos, axis=1), dhg, dgs

    do_scan, dhg, g_hg_norm = rowwise("hgrn_out_bwd", hg_out_bwd_fn, [(do_hg, d, 0), (o_scan, d, 0), (proj_main, d, 3)],
                                      [], [hg_norm_g], [(d, BF16), (d, BF16)], [(1, HEAD)])
    dhq, dhf, dhi, g_lb = hgrn_bwd(proj_main, lb + token_a[0, 0], hg_consts, states, a_mats, do_scan, bl, lp, d)

    dq_cat, dk_cat, dv_att = attn_bwd_t(q_cat, k_cat, k_t, v_att, o_mla, do_mla, lse, bl, lp, nm)

    def mla_prep_bwd_fn(dqc, dkc, dvv, cos, s_up, s_dn):
        dqc = dqc * scale
        dqs, dkvs, dkpe = [], [], 0.0
        for h in range(nm):
            dqs += [dqc[:, h * QK_PAD:h * QK_PAD + HEAD],
                    _rope_bwd(dqc[:, h * QK_PAD + HEAD:(h + 1) * QK_PAD], cos, s_up, s_dn)]
            dkvs += [dkc[:, h * QK_PAD:h * QK_PAD + HEAD], dvv[:, h * HEAD:(h + 1) * HEAD]]
            dkpe = dkpe + dkc[:, h * QK_PAD + HEAD:(h + 1) * QK_PAD]
        return jnp.concatenate(dqs, axis=1), jnp.concatenate(dkvs, axis=1), _rope_bwd(dkpe, cos, s_up, s_dn)

    dq_full, dkv_full, dkpe = rowwise("mla_prep_bwd", mla_prep_bwd_fn,
                                      [(dq_cat, nm * QK_PAD, 0), (dk_cat, nm * QK_PAD, 0), (dv_att, nm * HEAD, 0)],
                                      [t_cos, t_up, t_dn], [],
                                      [(nm * QK_PAD, BF16), (nm * QK_PAD, BF16), (HEAD, F32)])
    dqn = matmul("d_qn", dq_full, w_qb, "nt", out_dtype=BF16)
    g_wqb = matmul("gw_q_b", qn, dq_full, "tn")
    grads["w_q_b"] = g_wqb.reshape(ql, nm, QK_PAD)[:, :, :HEAD + ROPE].reshape(ql, nm * (HEAD + ROPE))
    dkvn = matmul("d_kvn", dkv_full, w_kvb, "nt", out_dtype=BF16)
    grads["w_kv_b"] = matmul("gw_kv_b", kvn, dkv_full, "tn")

    def mla_norms_bwd_fn(dqnv, dkvnv, cq, ckv, dkpev, gq, gk):
        dcq, dgq = _rms_bwd(cq, gq, dqnv)
        dckv, dgk = _rms_bwd(ckv, gk, dkvnv)
        return jnp.concatenate([dcq, dckv, dkpev], axis=1), dgq, dgk

    dmla, g_q_norm, g_kv_norm = rowwise("mla_norms_bwd", mla_norms_bwd_fn,
                                        [(dqn, ql, 0), (dkvn, kvl, 0), (proj_mla, ql, 0), (proj_mla, kvl, 1),
                                         (dkpe, HEAD, 0)], [], [q_a_norm_g, kv_a_norm_g],
                                        [(mla_w, BF16)], [(1, ql), (1, kvl)])

    d_pieces = [dhq, dhf, dhi, dhg, dgates, dmla]
    gw_parts = [matmul(f"gw_in_{k}", u1, dp, "tn") for k, dp in enumerate(d_pieces)]
    grads["w_in"] = jnp.concatenate(gw_parts[:4] + [gw_parts[5][:, :ql + kvl + ROPE], gw_parts[4]], axis=1)
    grads_late = start_grads("grads_late_start", late)
    w_mla_after = w_mla + grads_late[1][0, 0].astype(BF16)
    w_pieces = [w_main[:, 0:d], w_main[:, d:2 * d], w_main[:, 2 * d:3 * d], w_main[:, 3 * d:4 * d],
                w_main[:, 4 * d:6 * d], w_mla_after]
    du1 = matmul("d_u1", d_pieces, w_pieces, "nt", out_dtype=BF16)

    def first_bwd_fn(dh1v, h, du1v, is_meta, g):
        dx, dg = _rms_bwd(h, g, du1v)
        dh0v = dh1v + dx
        return dh0v, dg, dh0v * jnp.tile(is_meta, (1, d // HEAD))

    grad_x, g_mix_pre, meta_tile = rowwise(
        "norm_mix_pre_bwd", first_bwd_fn, [(dh1, d, 0), (h0, d, 0), (du1, d, 0)], [meta_rows], [mix_pre_g],
        [(d, F32, bl * seq, real_block)], [(1, d), (SEQ_BLOCK, d)])
    grad_x = grad_x.reshape(bl, seq, d)

    g_parts = finish_grads("early", early, grads_early, g_mix_pre)
    updates = {}

    def update(n, parts):
        w2 = wts[n].reshape(-1, wts[n].shape[-1])
        updates[n] = adamw("adamw_" + n, w2, [p.reshape(w2.shape) for p in parts], mom_m[n].reshape(w2.shape),
                           mom_v[n].reshape(w2.shape))

    for n in early:
        update(n, g_parts[n])
    g_parts = finish_grads("late", late, grads_late, updates[early[-1]][0])
    for n in late:
        update(n, g_parts[n])
    p0 = lb_soft[0:1]
    g_lb_logits = jnp.concatenate([g_lb * p0 * (1.0 - p0), -g_lb * p0 * (1.0 - p0)], axis=0)

    def row_of(vec):
        return vec.reshape(-1, d) if vec.size >= d else jnp.pad(vec.reshape(1, -1), ((0, 0), (0, d - vec.size)))

    small_parts = dict(b_gate=g_b_gate, lb_logits=g_lb_logits, hg_norm_g=g_hg_norm, q_a_norm_g=g_q_norm,
                       kv_a_norm_g=g_kv_norm, mix_pre_g=g_mix_pre, mix_post_g=g_mix_post, ffn_pre_g=g_ffn_pre,
                       ffn_post_g=g_ffn_post)
    g_meta = meta_tile[PAD_FRONT:PAD_FRONT + N_META]
    small_rows = [row_of(small_parts[n]) for n in SMALL] + [row_of(g_meta)]
    n_small = sum(r.shape[0] for r in small_rows)
    small = jnp.pad(jnp.concatenate(small_rows, axis=0), ((0, -(-n_small // 8) * 8 - n_small), (0, 0)))
    all_small = gather_small(small)
    small_t = small.shape[0]

    def sum8_fn(*slabs):
        acc = slabs[0]
        for s in slabs[1:]:
            acc = acc + s
        return acc

    (g_small,) = rowwise("sum_small", sum8_fn, [(all_small.reshape(8 * small_t, d), d, 0, k) for k in range(8)],
                         [], [], [(d, F32)], tm=small_t, n_rows=small_t)

    off = 0
    for n, part in zip(SMALL, small_rows[:-1]):
        rows = part.shape[0]
        update(n, [g_small[off:off + rows, :d].reshape(-1)[:wts[n].size]])
        off += rows
    update("meta_tokens", [lax.dynamic_slice_in_dim(g_small[off:off + N_META, :d], my_chip * mcols, mcols, axis=1)])

    loss = lax.psum(loss_part[0, 0], ("x", "y", "c"))

    def shaped(n, a):
        return a.reshape((1,) + wts[n].shape) if n in BIG else a.reshape(wts[n].shape)

    return (loss, grad_x, *[shaped(n, updates[n][k]) for k in range(4) for n in WEIGHTS])
```

```python
import functools
import math

import jax
import jax.numpy as jnp
from jax import lax
from jax.experimental import pallas as pl
from jax.experimental.pallas import tpu as pltpu

F32 = jnp.float32
BF16 = jnp.bfloat16
MESH = pl.DeviceIdType.MESH

N_META = 16
NORM_EPS = 1e-6
HEAD = 128
ROPE = 64
ROPE_HALF = ROPE // 2
QK_PAD = 2 * HEAD
ROPE_THETA = 10000.0
SEQ_BLOCK = 256
PAD_FRONT = SEQ_BLOCK - N_META
NEG = -1e30
VMEM_LIMIT = 56 * 1024 * 1024
ATTN_HEADS_PER_STEP = 1
ATTN_TILE_MAX = 768

ADAM_LR, ADAM_B1, ADAM_B2, ADAM_EPS, ADAM_WD, ADAM_STEP = 0.001, 0.9, 0.999, 1e-08, 0.01, 10

BIG = ("w_in", "w_hg_o", "w_q_b", "w_kv_b", "w_mla_o", "w_out", "w_ffn_in", "w_ffn_out")
COL_SHARDED = ("w_in", "w_q_b", "w_kv_b", "w_ffn_in")
SMALL = ("b_gate", "lb_logits", "hg_norm_g", "q_a_norm_g", "kv_a_norm_g", "mix_pre_g", "mix_post_g",
         "ffn_pre_g", "ffn_post_g")
WEIGHTS = ("meta_tokens", "w_in", "b_gate", "lb_logits", "hg_norm_g", "w_hg_o", "q_a_norm_g", "w_q_b",
           "kv_a_norm_g", "w_kv_b", "w_mla_o", "w_out", "mix_pre_g", "mix_post_g", "ffn_pre_g", "ffn_post_g",
           "w_ffn_in", "w_ffn_out")


def _tile(n, cap, unit=128):
    if n <= cap:
        return n
    best = None
    for t in range(unit, cap + 1, unit):
        if n % t == 0:
            best = t
    assert best is not None, (n, cap, unit)
    return best


def _sigmoid(x):
    return 1.0 / (1.0 + jnp.exp(-x))


def _bf(x):
    return x.astype(BF16)


def rowwise(name, fn, row_ins, seq_tabs, consts, row_outs, acc_outs=(), tm=SEQ_BLOCK, n_rows=None):
    t_rows = row_ins[0][0].shape[0] if n_rows is None else n_rows
    nt = t_rows // tm
    assert t_rows % tm == 0
    n_in = len(row_ins) + len(seq_tabs) + len(consts)
    n_row = len(row_outs)

    def body(*refs):
        vals = [r[...].astype(F32) for r in refs[:n_in]]
        res = fn(*vals)
        if not isinstance(res, (tuple, list)):
            res = (res,)
        outs = refs[n_in:]
        for k in range(n_row):
            outs[k][...] = res[k].astype(outs[k].dtype)
        if acc_outs:
            @pl.when(pl.program_id(0) == 0)
            def _():
                for k in range(len(acc_outs)):
                    outs[n_row + k][...] = jnp.zeros_like(outs[n_row + k])

            for k in range(len(acc_outs)):
                outs[n_row + k][...] += res[n_row + k]

    row_ins = [tuple(e) + (0,) * (4 - len(e)) for e in row_ins]
    in_specs = [pl.BlockSpec((tm, w), functools.partial(lambda i, j, ro: (ro(i) if callable(ro) else i + ro, j),
                                                        j=j, ro=ro)) for (_, w, j, ro) in row_ins]
    for tab in seq_tabs:
        per = tab.shape[0] // tm
        in_specs.append(pl.BlockSpec((tm, tab.shape[1]), functools.partial(lambda i, per: (i % per, 0), per=per)))
    for c in consts:
        in_specs.append(pl.BlockSpec(c.shape, lambda i: (0, 0)))
    row_outs = [tuple(e) + (t_rows, None)[len(e) - 2:] for e in row_outs]
    out_specs = [pl.BlockSpec((tm, w), functools.partial(lambda i, rm: (i if rm is None else rm(i), 0), rm=rm))
                 for (w, _, _, rm) in row_outs]
    out_specs += [pl.BlockSpec(s, lambda i: (0, 0)) for s in acc_outs]
    out_shape = [jax.ShapeDtypeStruct((rows, w), dt) for (w, dt, rows, _) in row_outs]
    out_shape += [jax.ShapeDtypeStruct(s, F32) for s in acc_outs]
    res = pl.pallas_call(
        body, name=name, grid=(nt,), in_specs=in_specs, out_specs=out_specs, out_shape=out_shape,
        compiler_params=pltpu.CompilerParams(dimension_semantics=("arbitrary",)),
    )(*[e[0] for e in row_ins], *seq_tabs, *consts)
    return res


def matmul(name, a, b, mode, out_dtype=F32):
    if mode != "tn":
        return _matmul_resident(name, a if isinstance(a, (list, tuple)) else [a],
                                b if isinstance(b, (list, tuple)) else [b], mode, out_dtype)
    kdim, m = a.shape
    n = b.shape[1]
    tn = _tile(n, 1536)
    tm, tk = _tile(m, 1408 if tn <= 1024 else 1024), _tile(kdim, 1536)
    nk = kdim // tk

    def body(a_ref, b_ref, o_ref, acc_ref):
        k = pl.program_id(2)

        @pl.when(k == 0)
        def _():
            acc_ref[...] = jnp.zeros_like(acc_ref)

        acc_ref[...] += lax.dot_general(a_ref[...], b_ref[...], TN_DIMS, preferred_element_type=F32)

        @pl.when(k == nk - 1)
        def _():
            o_ref[...] = acc_ref[...].astype(o_ref.dtype)

    return pl.pallas_call(
        body, name=name, grid=(m // tm, n // tn, nk),
        in_specs=[pl.BlockSpec((tk, tm), lambda i, j, k: (k, i)), pl.BlockSpec((tk, tn), lambda i, j, k: (k, j))],
        out_specs=pl.BlockSpec((tm, tn), lambda i, j, k: (i, j)),
        out_shape=jax.ShapeDtypeStruct((m, n), out_dtype),
        scratch_shapes=[pltpu.VMEM((tm, tn), F32)],
        compiler_params=pltpu.CompilerParams(dimension_semantics=("arbitrary", "arbitrary", "arbitrary"),
                                             vmem_limit_bytes=VMEM_LIMIT),
    )(a, b)


def _matmul_resident(name, a_list, b_list, mode, out_dtype):
    m = a_list[0].shape[0]
    n = b_list[0].shape[1] if mode == "nn" else b_list[0].shape[0]
    k_total = sum(a.shape[1] for a in a_list)
    out_bytes = 2 if out_dtype == BF16 else 4
    budget = VMEM_LIMIT - 4 * k_total * n - (6 << 20)
    tm = 1024
    while tm > 128 and 2 * tm * (2 * k_total + out_bytes * n) > budget:
        tm //= 2
    tm = _tile(m, tm)
    cn = _tile(n, 1024)
    npairs = len(a_list)

    def body(*refs):
        a_refs, b_refs, o_ref = refs[:npairs], refs[npairs:2 * npairs], refs[2 * npairs]
        for c in range(n // cn):
            acc = None
            for a_ref, b_ref in zip(a_refs, b_refs):
                if mode == "nn":
                    part = jnp.dot(a_ref[...], b_ref[:, pl.ds(c * cn, cn)], preferred_element_type=F32)
                else:
                    part = lax.dot_general(a_ref[...], b_ref[pl.ds(c * cn, cn), :], NT_DIMS,
                                           preferred_element_type=F32)
                acc = part if acc is None else acc + part
            o_ref[:, pl.ds(c * cn, cn)] = acc.astype(o_ref.dtype)

    in_specs = [pl.BlockSpec((tm, a.shape[1]), lambda i: (i, 0)) for a in a_list]
    in_specs += [pl.BlockSpec(b.shape, lambda i: (0, 0)) for b in b_list]
    return pl.pallas_call(
        body, name=name, grid=(m // tm,), in_specs=in_specs,
        out_specs=pl.BlockSpec((tm, n), lambda i: (i, 0)),
        out_shape=jax.ShapeDtypeStruct((m, n), out_dtype),
        compiler_params=pltpu.CompilerParams(dimension_semantics=("arbitrary",), vmem_limit_bytes=VMEM_LIMIT),
    )(*a_list, *b_list)


def matmul_fused(name, fn, row_ins, consts, weight, pieces, mode, extra_outs, out_dtype=BF16, tm=256):
    row_ins = [tuple(e) + (0,) * (4 - len(e)) for e in row_ins]
    t_rows = row_ins[0][0].shape[0]
    tm = _tile(t_rows, tm)
    n = weight.shape[1] if mode == "nn" else weight.shape[0]
    n_in = len(row_ins) + len(consts)
    n_parts = len(pieces)

    def body(*refs):
        w_hbm = refs[n_in]
        outs = refs[n_in + 1:n_in + 2 + len(extra_outs)]
        w_ref, sem = refs[-2], refs[-1]

        @pl.when(pl.program_id(0) == 0)
        def _():
            cp = pltpu.make_async_copy(w_hbm, w_ref, sem)
            cp.start()
            cp.wait()

        res = fn(*[r[...].astype(F32) for r in refs[:n_in]])
        acc = None
        for a_p, (k0, k1) in zip(res[:n_parts], pieces):
            if mode == "nn":
                part = jnp.dot(_bf(a_p), w_ref[pl.ds(k0, k1 - k0), :], preferred_element_type=F32)
            else:
                part = lax.dot_general(_bf(a_p), w_ref[:, pl.ds(k0, k1 - k0)], NT_DIMS, preferred_element_type=F32)
            acc = part if acc is None else acc + part
        outs[0][...] = acc.astype(outs[0].dtype)
        for o_ref, val in zip(outs[1:], res[n_parts:]):
            o_ref[...] = val.astype(o_ref.dtype)

    in_specs = [pl.BlockSpec((tm, w), functools.partial(lambda i, j, ro: (i + ro, j), j=j, ro=ro))
                for (_, w, j, ro) in row_ins]
    in_specs += [pl.BlockSpec(c.shape, lambda i: (0, 0)) for c in consts]
    in_specs.append(pl.BlockSpec(memory_space=pl.ANY))
    widths = [(n, out_dtype)] + list(extra_outs)
    return pl.pallas_call(
        body, name=name, grid=(t_rows // tm,), in_specs=in_specs,
        out_specs=[pl.BlockSpec((tm, w), lambda i: (i, 0)) for (w, _) in widths],
        out_shape=[jax.ShapeDtypeStruct((t_rows, w), dt) for (w, dt) in widths],
        scratch_shapes=[pltpu.VMEM(weight.shape, weight.dtype), pltpu.SemaphoreType.DMA],
        compiler_params=pltpu.CompilerParams(dimension_semantics=("arbitrary",), vmem_limit_bytes=VMEM_LIMIT),
    )(*[e[0] for e in row_ins], *consts, weight)


def _rms(x, g):
    r = lax.rsqrt(jnp.mean(x * x, axis=-1, keepdims=True) + NORM_EPS)
    return x * r * g


def _rms_bwd(x, g, dy):
    r = lax.rsqrt(jnp.mean(x * x, axis=-1, keepdims=True) + NORM_EPS)
    xh = x * r
    dyg = dy * g
    dx = r * (dyg - xh * jnp.mean(dyg * xh, axis=-1, keepdims=True))
    return dx, jnp.sum(dy * xh, axis=0, keepdims=True)


def _silu(x):
    return x * _sigmoid(x)


def _silu_grad(x):
    s = _sigmoid(x)
    return s * (1.0 + x * (1.0 - s))


def _rope(xs, cos, s_up, s_dn):
    return xs * cos + pltpu.roll(xs, ROPE_HALF, 1) * s_up + pltpu.roll(xs, HEAD - ROPE_HALF, 1) * s_dn


def _rope_bwd(dy, cos, s_up, s_dn):
    return dy * cos + pltpu.roll(dy * s_up, HEAD - ROPE_HALF, 1) + pltpu.roll(dy * s_dn, ROPE_HALF, 1)


HG_SUB = 128
HG_LEVELS = 7
HG_E_ROWS = (HG_LEVELS + 1) * HG_SUB
HG_BWD_GROUP = 6
TN_DIMS = (((0,), (0,)), ((), ()))
NT_DIMS = (((1,), (1,)), ((), ()))


def _hg_constants():
    import numpy as np
    n = HG_SUB
    r = np.arange(n)[:, None]
    c = np.arange(n)[None, :]
    cs, ps = [], []
    for lvl in range(HG_LEVELS):
        m = (n // 2) >> lvl
        upper = (r % (2 * m)) >= m
        mid = (r // (2 * m)) * (2 * m) + m - 1
        cs.append(np.where(upper, (c > mid) & (c <= r), (c > r) & (c <= mid)))
        ps.append(((r // (2 * m)) == (c // (2 * m))) & upper & ((c % (2 * m)) < m))
    cs.append(c <= r)
    cs.append(np.ones((8, n), bool))
    cstack = np.concatenate(cs, 0).astype(np.float32)
    pstack = np.concatenate(ps, 0).astype(np.float32)
    pstack_t = np.concatenate([p.T for p in ps], 0).astype(np.float32)
    return (jnp.asarray(cstack, BF16), jnp.asarray(cstack[:HG_E_ROWS].T, BF16), jnp.asarray(pstack, F32),
            jnp.asarray(pstack_t, F32))


def _split_dot(c_bf, x):
    hi = _bf(x)
    lo = _bf(x - hi.astype(F32))
    r2 = jnp.dot(c_bf, jnp.concatenate([hi, lo], axis=1), preferred_element_type=F32)
    return r2[:, :HEAD] + r2[:, HEAD:]


def _hg_gates(hq, hf, lb):
    sq = _sigmoid(hq)
    sg = _sigmoid(hf)
    fg = lb + (1.0 - lb) * sg
    return sq, hq * sq, sg, fg, 1.0 - fg, jnp.log(fg)


def hgrn_fwd(proj_main, lb, consts, bl, lp, d):
    nh = d // HEAD
    rows_blk = _tile(lp, 768, SEQ_BLOCK)
    nb = lp // rows_blk
    spb = rows_blk // HG_SUB
    cstack, _, pstack, _ = consts

    def body(hq_ref, hf_ref, hi_ref, lb_ref, c_ref, p_ref, o_ref, st_ref, a_ref, s_ref):
        j = pl.program_id(2)

        @pl.when(j == 0)
        def _():
            s_ref[...] = jnp.zeros_like(s_ref)

        lbv = lb_ref[...]
        cs = c_ref[...]
        rows = [pl.ds(s * HG_SUB, HG_SUB) for s in range(spb)]
        gates = [_hg_gates(hq_ref[r, :].astype(F32), hf_ref[r, :].astype(F32), lbv) for r in rows]
        qs, ks = [g_[1] for g_ in gates], [g_[4] for g_ in gates]
        vs = [hi_ref[r, :].astype(F32) for r in rows]
        es = [_split_dot(cs, g_[5]) for g_ in gates]
        a_acc = [jnp.zeros((HG_SUB, HG_SUB), F32) for _ in rows]
        for lvl in range(HG_LEVELS):
            for s in range(spb):
                x = jnp.exp(es[s][lvl * HG_SUB:(lvl + 1) * HG_SUB])
                a_acc[s] = a_acc[s] + p_ref[pl.ds(lvl * HG_SUB, HG_SUB), :] * lax.dot_general(
                    _bf(qs[s] * x), _bf(ks[s] * x), NT_DIMS, preferred_element_type=F32)
        o_intra, qbs, kds, e_lasts = [], [], [], []
        for s in range(spb):
            a_bf = _bf(a_acc[s])
            a_ref[0, 0, s] = a_bf
            bc = es[s][HG_LEVELS * HG_SUB:HG_E_ROWS]
            b_last = jnp.tile(es[s][HG_E_ROWS:], (HG_SUB // 8, 1))
            o_intra.append(jnp.dot(a_bf, _bf(vs[s]), preferred_element_type=F32)
                           + jnp.sum(qs[s] * ks[s], axis=1, keepdims=True) * vs[s])
            qbs.append(_bf(qs[s] * jnp.exp(bc)))
            kds.append(_bf(ks[s] * jnp.exp(b_last - bc)))
            e_lasts.append(jnp.exp(b_last))
        st = s_ref[...]
        for s in range(spb):
            st_ref[0, 0, s] = st
            o_ref[rows[s], :] = (o_intra[s] + lax.dot_general(qbs[s], _bf(st), NT_DIMS, preferred_element_type=F32)
                                 ).astype(o_ref.dtype)
            st = st * e_lasts[s] + lax.dot_general(_bf(vs[s]), kds[s], TN_DIMS, preferred_element_type=F32)
        s_ref[...] = st

    def colspec(off):
        return pl.BlockSpec((rows_blk, HEAD), functools.partial(lambda h, b, j, off: (b * nb + j, off + h), off=off))

    whole = lambda arr: pl.BlockSpec(arr.shape, lambda h, b, j: (0, 0))
    return pl.pallas_call(
        body, name="hgrn_fwd", grid=(nh, bl, nb),
        in_specs=[colspec(0), colspec(nh), colspec(2 * nh), pl.BlockSpec((1, HEAD), lambda h, b, j: (0, h)),
                  whole(cstack), whole(pstack)],
        out_specs=[pl.BlockSpec((rows_blk, HEAD), lambda h, b, j: (b * nb + j, h)),
                   pl.BlockSpec((1, 1, spb, HEAD, HEAD), lambda h, b, j: (b, h, j, 0, 0)),
                   pl.BlockSpec((1, 1, spb, HG_SUB, HG_SUB), lambda h, b, j: (b, h, j, 0, 0))],
        out_shape=[jax.ShapeDtypeStruct((bl * lp, d), BF16),
                   jax.ShapeDtypeStruct((bl, nh, lp // HG_SUB, HEAD, HEAD), F32),
                   jax.ShapeDtypeStruct((bl, nh, lp // HG_SUB, HG_SUB, HG_SUB), BF16)],
        scratch_shapes=[pltpu.VMEM((HEAD, HEAD), F32)],
        compiler_params=pltpu.CompilerParams(dimension_semantics=("arbitrary", "arbitrary", "arbitrary")),
    )(proj_main, proj_main, proj_main, lb, cstack, pstack)


def hgrn_bwd(proj_main, lb, consts, states, a_mats, do_scan, bl, lp, d):
    nh = d // HEAD
    rows_blk = _tile(lp, 768, SEQ_BLOCK)
    nb = lp // rows_blk
    spb = rows_blk // HG_SUB
    cstack, cstack_t = consts[0], consts[1]
    pstack, pstack_t = _bf(consts[2]), _bf(consts[3])

    def body(hq_ref, hf_ref, hi_ref, lb_ref, c_ref, ct_ref, p_ref, pt_ref, st_ref, a_ref, do_ref,
             dq_ref, df_ref, di_ref, dlb_ref, ds_ref):
        b_id, j = pl.program_id(1), pl.program_id(2)
        blk = nb - 1 - j

        @pl.when(j == 0)
        def _():
            ds_ref[...] = jnp.zeros_like(ds_ref)

        @pl.when((j == 0) & (b_id == 0))
        def _():
            dlb_ref[...] = jnp.zeros_like(dlb_ref)

        lbv = lb_ref[...]
        cs = c_ref[...]
        cst = ct_ref[...]

        dlb = jnp.zeros((1, HEAD), F32)
        for first in reversed(range(0, spb, HG_BWD_GROUP)):
            dlb = dlb + _hg_group_bwd(list(range(first, min(first + HG_BWD_GROUP, spb))), lbv, cs, cst, hq_ref,
                                      hf_ref, hi_ref, st_ref, a_ref, do_ref, p_ref, pt_ref, dq_ref, df_ref, di_ref,
                                      ds_ref)
        dlb_ref[...] += dlb

    def _hg_group_bwd(ids, lbv, cs, cst, hq_ref, hf_ref, hi_ref, st_ref, a_ref, do_ref, p_ref, pt_ref, dq_ref,
                      df_ref, di_ref, ds_ref):
        rng = range(len(ids))
        rows = [pl.ds(s * HG_SUB, HG_SUB) for s in ids]
        hqs = [hq_ref[r, :].astype(F32) for r in rows]
        gates = [_hg_gates(hqs[s], hf_ref[rows[s], :].astype(F32), lbv) for s in rng]
        sqs, qs, sgs, fgs, ks = ([g_[i] for g_ in gates] for i in range(5))
        vs = [hi_ref[r, :].astype(F32) for r in rows]
        dos = [do_ref[r, :].astype(F32) for r in rows]
        sts = [st_ref[0, 0, s] for s in ids]
        es = [_split_dot(cs, g_[5]) for g_ in gates]
        bcs = [e[HG_LEVELS * HG_SUB:HG_E_ROWS] for e in es]
        b_lasts = [jnp.tile(e[HG_E_ROWS:], (HG_SUB // 8, 1)) for e in es]
        ebs = [jnp.exp(bc) for bc in bcs]
        qbs = [qs[s] * ebs[s] for s in rng]
        ers = [jnp.exp(b_lasts[s] - bcs[s]) for s in rng]
        kds = [ks[s] * ers[s] for s in rng]
        e_lasts = [jnp.exp(b) for b in b_lasts]
        do_bfs, v_bfs = [_bf(x) for x in dos], [_bf(x) for x in vs]
        das = [_bf(lax.dot_general(do_bfs[s], v_bfs[s], NT_DIMS, preferred_element_type=F32)) for s in rng]
        dats = [_bf(lax.dot_general(v_bfs[s], do_bfs[s], NT_DIMS, preferred_element_type=F32)) for s in rng]
        dqbs = [jnp.dot(do_bfs[s], _bf(sts[s]), preferred_element_type=F32) for s in rng]
        m_s = [lax.dot_general(do_bfs[s], _bf(qbs[s]), TN_DIMS, preferred_element_type=F32) for s in rng]
        dst_outs = [None] * len(ids)
        dst = ds_ref[...]
        for s in reversed(rng):
            dst_outs[s] = dst
            dst = dst * e_lasts[s] + m_s[s]
        ds_ref[...] = dst
        dst_bfs = [_bf(x) for x in dst_outs]
        d_diags = [jnp.sum(dos[s] * vs[s], axis=1, keepdims=True) for s in rng]
        dvs = [lax.dot_general(a_ref[0, 0, ids[s]], do_bfs[s], TN_DIMS, preferred_element_type=F32)
               + jnp.sum(qs[s] * ks[s], axis=1, keepdims=True) * dos[s]
               + lax.dot_general(_bf(kds[s]), dst_bfs[s], NT_DIMS, preferred_element_type=F32) for s in rng]
        dkds = [jnp.dot(v_bfs[s], dst_bfs[s], preferred_element_type=F32) for s in rng]
        dqs = [dqbs[s] * ebs[s] + d_diags[s] * ks[s] for s in rng]
        dks = [dkds[s] * ers[s] + d_diags[s] * qs[s] for s in rng]
        d_lasts = [jnp.sum(dst_outs[s] * sts[s] * e_lasts[s], axis=0, keepdims=True)
                   + jnp.sum(dkds[s] * kds[s], axis=0, keepdims=True) for s in rng]
        des = [[] for _ in rng]
        for lvl in range(HG_LEVELS):
            for s in rng:
                x = jnp.exp(es[s][lvl * HG_SUB:(lvl + 1) * HG_SUB])
                qh, kh = qs[s] * x, ks[s] * x
                dm = p_ref[pl.ds(lvl * HG_SUB, HG_SUB), :] * das[s]
                dmt = pt_ref[pl.ds(lvl * HG_SUB, HG_SUB), :] * dats[s]
                dqh = jnp.dot(dm, _bf(kh), preferred_element_type=F32)
                dkh = jnp.dot(dmt, _bf(qh), preferred_element_type=F32)
                dqs[s] = dqs[s] + dqh * x
                dks[s] = dks[s] + dkh * x
                des[s].append(dqh * qh + dkh * kh)
        dlb = jnp.zeros((1, HEAD), F32)
        for s in rng:
            des[s].append(dqbs[s] * qbs[s] - dkds[s] * kds[s])
            dg = _split_dot(cst, jnp.concatenate(des[s], axis=0)) + d_lasts[s]
            dfg = dg / fgs[s] - dks[s]
            dq_ref[rows[s], :] = (dqs[s] * (sqs[s] * (1.0 + hqs[s] * (1.0 - sqs[s])))).astype(dq_ref.dtype)
            df_ref[rows[s], :] = (dfg * (1.0 - lbv) * sgs[s] * (1.0 - sgs[s])).astype(df_ref.dtype)
            di_ref[rows[s], :] = dvs[s].astype(di_ref.dtype)
            dlb = dlb + jnp.sum(dfg * (1.0 - sgs[s]), axis=0, keepdims=True)
        return dlb

    def colspec(off):
        return pl.BlockSpec((rows_blk, HEAD),
                            functools.partial(lambda h, b, j, off: (b * nb + nb - 1 - j, off + h), off=off))

    whole = lambda arr: pl.BlockSpec(arr.shape, lambda h, b, j: (0, 0))
    mats = lambda: pl.BlockSpec((1, 1, spb, HEAD, HEAD), lambda h, b, j: (b, h, nb - 1 - j, 0, 0))
    t_rows = bl * lp
    return pl.pallas_call(
        body, name="hgrn_bwd", grid=(nh, bl, nb),
        in_specs=[colspec(0), colspec(nh), colspec(2 * nh), pl.BlockSpec((1, HEAD), lambda h, b, j: (0, h)),
                  whole(cstack), whole(cstack_t), whole(pstack), whole(pstack_t), mats(), mats(), colspec(0)],
        out_specs=[colspec(0), colspec(0), colspec(0), pl.BlockSpec((1, HEAD), lambda h, b, j: (0, h))],
        out_shape=[jax.ShapeDtypeStruct((t_rows, d), BF16)] * 3 + [jax.ShapeDtypeStruct((1, d), F32)],
        scratch_shapes=[pltpu.VMEM((HEAD, HEAD), F32)],
        compiler_params=pltpu.CompilerParams(dimension_semantics=("arbitrary", "arbitrary", "arbitrary")),
    )(proj_main, proj_main, proj_main, lb, cstack, cstack_t, pstack, pstack_t, states, a_mats, do_scan)


def _key_query_mask(key0, qry0, nk, nq_, causal):
    key = key0 + lax.broadcasted_iota(jnp.int32, (nk, 1), 0)
    if not causal:
        return key >= PAD_FRONT
    qry = qry0 + lax.broadcasted_iota(jnp.int32, (1, nq_), 1)
    return (key <= qry) & (key >= PAD_FRONT)


def _attn_tile(lp):
    return _tile(lp, ATTN_TILE_MAX, SEQ_BLOCK)


def attn_fwd_t(q_cat, k_cat, v_t, bl, lp, nm):
    tq = tk = _attn_tile(lp)
    nq = lp // tq
    hp = ATTN_HEADS_PER_STEP
    assert nm % hp == 0

    def body(q_ref, k_ref, vt_ref, o_ref, lse_ref, m_ref, l_ref, acc_ref):
        i = pl.program_id(2)
        m_ref[...] = jnp.full_like(m_ref, NEG)
        l_ref[...] = jnp.zeros_like(l_ref)
        acc_ref[...] = jnp.zeros_like(acc_ref)

        def step(c, mask):
            c0 = pl.multiple_of(c * tk, tk)
            for hh in range(hp):
                cols = pl.ds(hh * QK_PAD, QK_PAD)
                st = lax.dot_general(k_ref[pl.ds(c0, tk), cols], q_ref[:, cols], NT_DIMS,
                                     preferred_element_type=F32)
                if mask is not None:
                    st = jnp.where(_key_query_mask(c * tk, i * tq, tk, tq, mask == "causal"), st, NEG)
                m_old = m_ref[hh]
                m_new = jnp.maximum(m_old, jnp.max(st, axis=0, keepdims=True))
                alpha = jnp.exp(m_old - m_new)
                pt = jnp.exp(st - m_new)
                l_ref[hh] = alpha * l_ref[hh] + jnp.sum(pt, axis=0, keepdims=True)
                acc_ref[hh] = alpha * acc_ref[hh] + jnp.dot(vt_ref[0, hh, pl.ds(c, 1)][0], _bf(pt),
                                                            preferred_element_type=F32)
                m_ref[hh] = m_new

        def mid(c, carry):
            step(c, None)
            return carry

        @pl.when(i == 0)
        def _():
            step(0, "causal")

        @pl.when(i > 0)
        def _():
            step(0, "pad")
            lax.fori_loop(1, i, mid, 0)
            step(i, "causal")

        for hh in range(hp):
            o_ref[:, pl.ds(hh * HEAD, HEAD)] = jnp.transpose(acc_ref[hh] / l_ref[hh]).astype(o_ref.dtype)
            lse_ref[0, hh, 0] = m_ref[hh] + jnp.log(l_ref[hh])

    return pl.pallas_call(
        body, name="attn_fwd", grid=(bl, nm // hp, nq),
        in_specs=[pl.BlockSpec((tq, hp * QK_PAD), lambda b, h, i: (b * nq + i, h)),
                  pl.BlockSpec((lp, hp * QK_PAD), lambda b, h, i: (b, h)),
                  pl.BlockSpec((1, hp, nq, HEAD, tk), lambda b, h, i: (b, h, 0, 0, 0))],
        out_specs=[pl.BlockSpec((tq, hp * HEAD), lambda b, h, i: (b * nq + i, h)),
                   pl.BlockSpec((1, hp, 1, 1, tq), lambda b, h, i: (b, h, i, 0, 0))],
        out_shape=[jax.ShapeDtypeStruct((bl * lp, nm * HEAD), BF16),
                   jax.ShapeDtypeStruct((bl, nm, nq, 1, tq), F32)],
        scratch_shapes=[pltpu.VMEM((hp, 1, tq), F32), pltpu.VMEM((hp, 1, tq), F32), pltpu.VMEM((hp, HEAD, tq), F32)],
        compiler_params=pltpu.CompilerParams(dimension_semantics=("arbitrary", "arbitrary", "arbitrary")),
    )(q_cat, k_cat, v_t)


def attn_bwd_t(q_cat, k_cat, k_t, v, o, do, lse, bl, lp, nm):
    tq = tk = _attn_tile(lp)
    nq = lp // tq
    hp = ATTN_HEADS_PER_STEP
    assert nm % hp == 0

    def body(q_ref, k_ref, kt_ref, v_ref, o_ref, do_ref, lse_ref, dq_ref, dk_ref, dv_ref, dqt_ref, dka_ref, dva_ref):
        i = pl.program_id(2)

        @pl.when(i == 0)
        def _():
            dqt_ref[...] = jnp.zeros_like(dqt_ref)

        dka_ref[...] = jnp.zeros_like(dka_ref)
        dva_ref[...] = jnp.zeros_like(dva_ref)
        ones8 = jnp.ones((8, HEAD), BF16)

        def step(c, mask):
            c0 = pl.multiple_of(c * tq, tq)
            for hh in range(hp):
                qcols, vcols = pl.ds(hh * QK_PAD, QK_PAD), pl.ds(hh * HEAD, HEAD)
                qs = q_ref[pl.ds(c0, tq), qcols]
                dos = do_ref[pl.ds(c0, tq), vcols]
                prod = dos.astype(F32) * o_ref[pl.ds(c0, tq), vcols].astype(F32)
                hi = _bf(prod)
                lo = _bf(prod - hi.astype(F32))
                delta8 = (lax.dot_general(ones8, hi, NT_DIMS, preferred_element_type=F32)
                          + lax.dot_general(ones8, lo, NT_DIMS, preferred_element_type=F32))
                st = lax.dot_general(k_ref[:, qcols], qs, NT_DIMS, preferred_element_type=F32)
                pt = jnp.exp(st - lse_ref[0, hh, pl.ds(c, 1)][0])
                if mask is not None:
                    pt = jnp.where(_key_query_mask(i * tk, c * tq, tk, tq, mask == "causal"), pt, 0.0)
                dva_ref[hh] += jnp.dot(_bf(pt), dos, preferred_element_type=F32)
                dpt = lax.dot_general(v_ref[:, vcols], dos, NT_DIMS, preferred_element_type=F32)
                dst = _bf(pt * (dpt - jnp.tile(delta8, (tk // 8, 1))))
                dka_ref[hh] += jnp.dot(dst, qs, preferred_element_type=F32)
                dqt_ref[hh, pl.ds(c, 1)] += jnp.dot(kt_ref[0, hh, 0], dst, preferred_element_type=F32)[None]

        step(i, "causal")

        def rest_masked(c, carry):
            step(c, "pad")
            return carry

        def rest(c, carry):
            step(c, None)
            return carry

        @pl.when(i == 0)
        def _():
            lax.fori_loop(1, nq, rest_masked, 0)

        @pl.when(i > 0)
        def _():
            lax.fori_loop(i + 1, nq, rest, 0)

        for hh in range(hp):
            dk_ref[:, pl.ds(hh * QK_PAD, QK_PAD)] = dka_ref[hh].astype(dk_ref.dtype)
            dv_ref[:, pl.ds(hh * HEAD, HEAD)] = dva_ref[hh].astype(dv_ref.dtype)

        @pl.when(i == nq - 1)
        def _():
            for hh in range(hp):
                for c in range(nq):
                    dq_ref[pl.ds(c * tq, tq), pl.ds(hh * QK_PAD, QK_PAD)] = (
                        jnp.transpose(dqt_ref[hh, c])).astype(dq_ref.dtype)

    return pl.pallas_call(
        body, name="attn_bwd", grid=(bl, nm // hp, nq),
        in_specs=[pl.BlockSpec((lp, hp * QK_PAD), lambda b, h, i: (b, h)),
                  pl.BlockSpec((tk, hp * QK_PAD), lambda b, h, i: (b * nq + i, h)),
                  pl.BlockSpec((1, hp, 1, QK_PAD, tk), lambda b, h, i: (b, h, i, 0, 0)),
                  pl.BlockSpec((tk, hp * HEAD), lambda b, h, i: (b * nq + i, h)),
                  pl.BlockSpec((lp, hp * HEAD), lambda b, h, i: (b, h)),
                  pl.BlockSpec((lp, hp * HEAD), lambda b, h, i: (b, h)),
                  pl.BlockSpec((1, hp, nq, 1, tq), lambda b, h, i: (b, h, 0, 0, 0))],
        out_specs=[pl.BlockSpec((lp, hp * QK_PAD), lambda b, h, i: (b, h)),
                   pl.BlockSpec((tk, hp * QK_PAD), lambda b, h, i: (b * nq + i, h)),
                   pl.BlockSpec((tk, hp * HEAD), lambda b, h, i: (b * nq + i, h))],
        out_shape=[jax.ShapeDtypeStruct((bl * lp, nm * QK_PAD), BF16),
                   jax.ShapeDtypeStruct((bl * lp, nm * QK_PAD), BF16),
                   jax.ShapeDtypeStruct((bl * lp, nm * HEAD), BF16)],
        scratch_shapes=[pltpu.VMEM((hp, nq, QK_PAD, tq), F32), pltpu.VMEM((hp, tk, QK_PAD), F32),
                        pltpu.VMEM((hp, tk, HEAD), F32)],
        compiler_params=pltpu.CompilerParams(dimension_semantics=("arbitrary", "arbitrary", "arbitrary")),
    )(q_cat, k_cat, k_t, v, o, do, lse)


def _place():
    return lax.axis_index("x"), lax.axis_index("y"), lax.axis_index("c")


def gather_shards(packed):
    hbm = pl.BlockSpec(memory_space=pl.ANY)

    def body(src_ref, out_ref, send_sems, recv_sems, local_sem):
        x, y, c = _place()
        me = 2 * x + y
        chips = [(1 - x, y), (x, 1 - y), (1 - x, 1 - y)]
        local = pltpu.make_async_copy(src_ref, out_ref.at[me], local_sem)
        local.start()
        sends = []
        for k, (px, py) in enumerate(chips):
            cp = pltpu.make_async_remote_copy(src_ref=src_ref, dst_ref=out_ref.at[me], send_sem=send_sems.at[k],
                                              recv_sem=recv_sems.at[k], device_id=(px, py, c), device_id_type=MESH)
            cp.start()
            sends.append(cp)
        for k, (px, py) in enumerate(chips):
            pltpu.make_async_remote_copy(src_ref=src_ref, dst_ref=out_ref.at[2 * px + py], send_sem=send_sems.at[k],
                                         recv_sem=recv_sems.at[k], device_id=(px, py, c),
                                         device_id_type=MESH).wait_recv()
        for cp in sends:
            cp.wait_send()
        local.wait()

    return pl.pallas_call(
        body, name="gather_shards", in_specs=[hbm], out_specs=hbm,
        out_shape=jax.ShapeDtypeStruct((4,) + packed.shape, packed.dtype),
        scratch_shapes=[pltpu.SemaphoreType.DMA((3,)), pltpu.SemaphoreType.DMA((3,)), pltpu.SemaphoreType.DMA],
    )(packed)


def gather_small(small):
    hbm = pl.BlockSpec(memory_space=pl.ANY)

    def body(small_ref, all_ref, send_sems, recv_sems, local_sem):
        x, y, c = _place()
        me = 4 * x + 2 * y + c
        local = pltpu.make_async_copy(small_ref, all_ref.at[me], local_sem)
        local.start()
        others = [(x ^ ((r >> 2) & 1), y ^ ((r >> 1) & 1), c ^ (r & 1)) for r in range(1, 8)]
        sends = []
        for r, peer in enumerate(others):
            cp = pltpu.make_async_remote_copy(src_ref=small_ref, dst_ref=all_ref.at[me], send_sem=send_sems.at[r],
                                              recv_sem=recv_sems.at[r], device_id=peer, device_id_type=MESH)
            cp.start()
            sends.append(cp)
        for r, (px, py, pc) in enumerate(others):
            pltpu.make_async_remote_copy(src_ref=small_ref, dst_ref=all_ref.at[4 * px + 2 * py + pc],
                                         send_sem=send_sems.at[r], recv_sem=recv_sems.at[r],
                                         device_id=(px, py, pc), device_id_type=MESH).wait_recv()
        for cp in sends:
            cp.wait_send()
        local.wait()

    return pl.pallas_call(
        body, name="gather_small", in_specs=[hbm], out_specs=hbm,
        out_shape=jax.ShapeDtypeStruct((8,) + small.shape, small.dtype),
        scratch_shapes=[pltpu.SemaphoreType.DMA((7,)), pltpu.SemaphoreType.DMA((7,)), pltpu.SemaphoreType.DMA],
    )(small)


def swap_with_sibling(name, parts):
    n = len(parts)
    hbm = pl.BlockSpec(memory_space=pl.ANY)

    def body(*refs):
        x, y, c = _place()
        cps = [pltpu.make_async_remote_copy(src_ref=refs[j], dst_ref=refs[n + j], send_sem=refs[2 * n].at[j],
                                            recv_sem=refs[2 * n + 1].at[j], device_id=(x, y, 1 - c),
                                            device_id_type=MESH) for j in range(n)]
        for cp in cps:
            cp.start()
        for cp in cps:
            cp.wait()

    return pl.pallas_call(
        body, name=name, in_specs=[hbm] * n, out_specs=[hbm] * n,
        out_shape=[jax.ShapeDtypeStruct(p.shape, p.dtype) for p in parts],
        scratch_shapes=[pltpu.SemaphoreType.DMA((n,)), pltpu.SemaphoreType.DMA((n,))],
    )(*parts)


def _chips3():
    x, y, c = _place()
    return [(1 - x, y, c), (x, 1 - y, c), (1 - x, 1 - y, c)]


def _push_copies(src_refs, land_refs, send_sems, recv_sems, per_chip):
    x, y, _ = _place()
    cps = []
    for j, (src_ref, land_ref) in enumerate(zip(src_refs, land_refs)):
        for k, (px, py, pc) in enumerate(_chips3()):
            part = src_ref.at[2 * px + py] if per_chip else src_ref
            slot = k if per_chip else 2 * x + y
            cps.append(pltpu.make_async_remote_copy(
                src_ref=part, dst_ref=land_ref.at[slot], send_sem=send_sems.at[3 * j + k],
                recv_sem=recv_sems.at[3 * j + k], device_id=(px, py, pc), device_id_type=MESH))
    return cps


def push_start(name, srcs, per_chip):
    n = len(srcs)
    hbm = pl.BlockSpec(memory_space=pltpu.HBM)
    sem = pl.BlockSpec(memory_space=pltpu.SEMAPHORE)
    lands = [lax.empty((3 if per_chip else 4,) + s.shape[-2:], s.dtype) for s in srcs]

    def body(*refs):
        src_refs, land_refs = refs[:n], refs[n:2 * n]
        send_sems, recv_sems = refs[2 * n], refs[2 * n + 1]
        for cp in _push_copies(src_refs, land_refs, send_sems, recv_sems, per_chip):
            cp.start()
        refs[-1][...] = jnp.zeros_like(refs[-1])

    outs = pl.pallas_call(
        body, name=name,
        out_shape=(pltpu.SemaphoreType.DMA((3 * n,)), pltpu.SemaphoreType.DMA((3 * n,)),
                   *[pltpu.HBM(a.shape, a.dtype) for a in list(srcs) + lands], jax.ShapeDtypeStruct((8, HEAD), F32)),
        in_specs=(hbm,) * (2 * n),
        out_specs=(sem, sem) + (hbm,) * (2 * n) + (pl.BlockSpec(memory_space=pltpu.VMEM),),
        input_output_aliases={j: 2 + j for j in range(2 * n)},
        compiler_params=pltpu.CompilerParams(has_side_effects=pltpu.SideEffectType.DATAFLOW_SIDE_EFFECTING),
    )(*[pltpu.with_memory_space_constraint(a, pltpu.HBM) for a in list(srcs) + lands])
    return tuple(outs[:-1]), outs[-1]


def push_wait(name, handle, after, per_chip):
    send_sems, recv_sems = handle[0], handle[1]
    thru = handle[2:]
    n = len(thru) // 2
    hbm = pl.BlockSpec(memory_space=pltpu.HBM)
    sem = pl.BlockSpec(memory_space=pltpu.SEMAPHORE)

    def body(*refs):
        src_refs, land_refs = refs[:n], refs[n:2 * n]
        for cp in _push_copies(src_refs, land_refs, refs[2 * n], refs[2 * n + 1], per_chip):
            cp.wait_send()
            cp.wait_recv()

    outs = pl.pallas_call(
        body, name=name,
        out_shape=tuple(pltpu.HBM(a.shape, a.dtype) for a in thru),
        in_specs=(hbm,) * (2 * n) + (sem, sem, pl.BlockSpec(memory_space=pl.ANY)), out_specs=(hbm,) * (2 * n),
        input_output_aliases={j: j for j in range(2 * n)},
        compiler_params=pltpu.CompilerParams(has_side_effects=pltpu.SideEffectType.DATAFLOW_SIDE_EFFECTING),
    )(*thru, send_sems, recv_sems, after)
    return outs[:n], outs[n:]


def join_gathered(name, own, landed, my_chip):
    blocks = lax.dynamic_update_index_in_dim(landed, own, my_chip, 0)
    _, r, c = blocks.shape
    if name in COL_SHARDED:
        return blocks.transpose(1, 0, 2).reshape(r, 4 * c)
    return blocks.reshape(4 * r, c)


def adamw(name, w, g_parts, m, v):
    r, c = w.shape
    tr = r if r * c <= 65536 else _tile(r, 128, 8)
    ng = len(g_parts)

    def body(*refs):
        w_ref, m_ref, v_ref = refs[0], refs[1 + ng], refs[2 + ng]
        g_ref, d_ref, nm_ref, nv_ref = refs[3 + ng:]
        gv = refs[1][...]
        for k in range(1, ng):
            gv = gv + refs[1 + k][...]
        m_new = ADAM_B1 * m_ref[...] + (1.0 - ADAM_B1) * gv
        v_new = ADAM_B2 * v_ref[...] + (1.0 - ADAM_B2) * (gv * gv)
        m_hat = m_new / (1.0 - ADAM_B1 ** ADAM_STEP)
        v_hat = v_new / (1.0 - ADAM_B2 ** ADAM_STEP)
        g_ref[...] = gv
        d_ref[...] = -ADAM_LR * (m_hat / (jnp.sqrt(v_hat) + ADAM_EPS) + ADAM_WD * w_ref[...])
        nm_ref[...] = m_new
        nv_ref[...] = v_new

    spec = pl.BlockSpec((tr, c), lambda i: (i, 0))
    return pl.pallas_call(
        body, name=name, grid=(r // tr,), in_specs=[spec] * (3 + ng), out_specs=[spec] * 4,
        out_shape=[jax.ShapeDtypeStruct((r, c), F32)] * 4,
        compiler_params=pltpu.CompilerParams(dimension_semantics=("arbitrary",)),
    )(w, *g_parts, m, v)


def split_full(name, full, s):
    if name in COL_SHARDED:
        c = full.shape[1] // 4
        return full[:, s * c:(s + 1) * c]
    r = full.shape[0] // 4
    return full[s * r:(s + 1) * r]


def kernel(x, meta_tokens, w_in, b_gate, lb_logits, hg_norm_g, w_hg_o, q_a_norm_g, w_q_b, kv_a_norm_g, w_kv_b, w_mla_o, w_out, mix_pre_g, mix_post_g, ffn_pre_g, ffn_post_g, w_ffn_in, w_ffn_out, loss_target, m_meta_tokens, m_w_in, m_b_gate, m_lb_logits, m_hg_norm_g, m_w_hg_o, m_q_a_norm_g, m_w_q_b, m_kv_a_norm_g, m_w_kv_b, m_w_mla_o, m_w_out, m_mix_pre_g, m_mix_post_g, m_ffn_pre_g, m_ffn_post_g, m_w_ffn_in, m_w_ffn_out, v_meta_tokens, v_w_in, v_b_gate, v_lb_logits, v_hg_norm_g, v_w_hg_o, v_q_a_norm_g, v_w_q_b, v_kv_a_norm_g, v_w_kv_b, v_w_mla_o, v_w_out, v_mix_pre_g, v_mix_post_g, v_ffn_pre_g, v_ffn_post_g, v_w_ffn_in, v_w_ffn_out):
    wts = dict(meta_tokens=meta_tokens, w_in=w_in[0], b_gate=b_gate, lb_logits=lb_logits, hg_norm_g=hg_norm_g,
               w_hg_o=w_hg_o[0], q_a_norm_g=q_a_norm_g, w_q_b=w_q_b[0], kv_a_norm_g=kv_a_norm_g, w_kv_b=w_kv_b[0],
               w_mla_o=w_mla_o[0], w_out=w_out[0], mix_pre_g=mix_pre_g, mix_post_g=mix_post_g, ffn_pre_g=ffn_pre_g,
               ffn_post_g=ffn_post_g, w_ffn_in=w_ffn_in[0], w_ffn_out=w_ffn_out[0])
    mom_m = dict(meta_tokens=m_meta_tokens, w_in=m_w_in[0], b_gate=m_b_gate, lb_logits=m_lb_logits,
                 hg_norm_g=m_hg_norm_g, w_hg_o=m_w_hg_o[0], q_a_norm_g=m_q_a_norm_g, w_q_b=m_w_q_b[0],
                 kv_a_norm_g=m_kv_a_norm_g, w_kv_b=m_w_kv_b[0], w_mla_o=m_w_mla_o[0], w_out=m_w_out[0],
                 mix_pre_g=m_mix_pre_g, mix_post_g=m_mix_post_g, ffn_pre_g=m_ffn_pre_g, ffn_post_g=m_ffn_post_g,
                 w_ffn_in=m_w_ffn_in[0], w_ffn_out=m_w_ffn_out[0])
    mom_v = dict(meta_tokens=v_meta_tokens, w_in=v_w_in[0], b_gate=v_b_gate, lb_logits=v_lb_logits,
                 hg_norm_g=v_hg_norm_g, w_hg_o=v_w_hg_o[0], q_a_norm_g=v_q_a_norm_g, w_q_b=v_w_q_b[0],
                 kv_a_norm_g=v_kv_a_norm_g, w_kv_b=v_w_kv_b[0], w_mla_o=v_w_mla_o[0], w_out=v_w_out[0],
                 mix_pre_g=v_mix_pre_g, mix_post_g=v_mix_post_g, ffn_pre_g=v_ffn_pre_g, ffn_post_g=v_ffn_post_g,
                 w_ffn_in=v_w_ffn_in[0], w_ffn_out=v_w_ffn_out[0])

    bl, seq, d = x.shape
    lp = PAD_FRONT + N_META + seq
    t_rows = bl * lp
    nh = d // HEAD
    ql, kvl = wts["w_q_b"].shape[0], wts["w_kv_b"].shape[0]
    nm = (4 * wts["w_mla_o"].shape[0]) // HEAD
    ffn = 4 * wts["w_ffn_out"].shape[0]
    mla_w = ql + kvl + HEAD
    assert ql == kvl and ql % HEAD == 0 and seq % SEQ_BLOCK == 0 and d % HEAD == 0
    scale = (HEAD + ROPE) ** -0.5
    my_chip = 2 * lax.axis_index("x") + lax.axis_index("y")

    mcols = meta_tokens.shape[1]
    meta_all = gather_shards(meta_tokens)
    meta_full = jnp.concatenate([meta_all[s] for s in range(4)], axis=1)

    def start_gather(name, names, order_after):
        srcs = [_bf(wts[n]) for n in names]
        if order_after is not None:
            srcs[0] = srcs[0] + order_after[0, 0].astype(BF16)
        return push_start(name, srcs, per_chip=False)

    def finish_gather(name, names, started, after):
        owns, landed = push_wait(name, started[0], after, per_chip=False)
        return {n: join_gathered(n, own, land, my_chip) for n, own, land in zip(names, owns, landed)}

    rest_names = tuple(n for n in BIG if n != "w_in")
    my_c = lax.axis_index("c")
    w_in_bf = _bf(wts["w_in"])
    half = w_in_bf.shape[0] // 2
    own_half = (lax.dynamic_slice_in_dim(w_in_bf, my_c * half, half, axis=0)
                + (meta_all[0, :1, :1] * 0.0)[0, 0].astype(BF16))
    gather_1 = push_start("gather_w_in_start", [own_half], per_chip=False)
    gather_2 = start_gather("gather_rest_start", rest_names, gather_1[1])

    h0 = jnp.concatenate([jnp.zeros((bl, PAD_FRONT, d), F32), jnp.broadcast_to(meta_full[None], (bl, N_META, d)), x],
                         axis=1).reshape(t_rows, d)
    tiles_seq, tiles_real = lp // SEQ_BLOCK, seq // SEQ_BLOCK
    assert PAD_FRONT + N_META == SEQ_BLOCK

    def real_block(i):
        return (i // tiles_seq) * tiles_real + jnp.maximum(i % tiles_seq - 1, 0)

    meta_rows = jnp.broadcast_to(((jnp.arange(lp) >= PAD_FRONT) & (jnp.arange(lp) < PAD_FRONT + N_META)
                                  ).astype(F32)[:, None], (lp, HEAD))
    pos = (jnp.arange(lp, dtype=jnp.int32) - PAD_FRONT).astype(F32)
    inv_freq = 1.0 / (ROPE_THETA ** (jnp.arange(0, ROPE, 2, dtype=F32) / ROPE))
    ang = pos[:, None] * inv_freq[None, :]
    zeros32 = jnp.zeros((lp, ROPE_HALF), F32)
    zeros64 = jnp.zeros((lp, HEAD - ROPE), F32)
    t_cos = jnp.concatenate([jnp.cos(ang), jnp.cos(ang), zeros64], axis=1)
    t_up = jnp.concatenate([zeros32, jnp.sin(ang), zeros64], axis=1)
    t_dn = jnp.concatenate([-jnp.sin(ang), zeros32, zeros64], axis=1)
    real = jnp.broadcast_to((jnp.arange(lp) >= PAD_FRONT + N_META).astype(F32)[:, None], (lp, d))
    lb_soft = jax.nn.softmax(lb_logits.astype(F32), axis=0)
    lb = lb_soft[0:1]

    (u1,) = rowwise("norm_mix_pre", lambda h, g: _rms(h, g), [(h0, d, 0)], [], [mix_pre_g + gather_2[1][0, 0]],
                    [(d, BF16)])
    _, (fetched,) = push_wait("gather_w_in_wait", gather_1[0], u1, per_chip=False)
    (handed,) = swap_with_sibling("swap_w_in", [fetched])
    halves = jnp.stack([fetched, handed])
    remote = jnp.concatenate([lax.dynamic_index_in_dim(halves, my_c, 0, keepdims=False),
                              lax.dynamic_index_in_dim(halves, 1 - my_c, 0, keepdims=False)], axis=1)
    full = {"w_in": join_gathered("w_in", w_in_bf, remote, my_chip)}
    w_main = jnp.concatenate([full["w_in"][:, :4 * d], full["w_in"][:, -2 * d:]], axis=1)
    w_mla = jnp.pad(full["w_in"][:, 4 * d:4 * d + ql + kvl + ROPE], ((0, 0), (0, HEAD - ROPE)))
    proj_main = matmul("proj_main", u1, w_main, "nn", out_dtype=BF16)
    proj_mla = matmul("proj_mla", u1, w_mla, "nn", out_dtype=BF16)
    hg_consts = _hg_constants()
    o_scan, states, a_mats = hgrn_fwd(proj_main, lb, hg_consts, bl, lp, d)

    def hg_out_fn(o, hg, g):
        return jnp.concatenate([_rms(o[:, h * HEAD:(h + 1) * HEAD], g) for h in range(nh)], axis=1) * _silu(hg)

    (o_hg,) = rowwise("hgrn_out", hg_out_fn, [(o_scan, d, 0), (proj_main, d, 3)], [], [hg_norm_g], [(d, BF16)])
    full.update(finish_gather("gather_rest_wait", rest_names, gather_2, o_hg))
    w_qb = jnp.pad(full["w_q_b"].reshape(ql, nm, HEAD + ROPE), ((0, 0), (0, 0), (0, QK_PAD - HEAD - ROPE))
                   ).reshape(ql, nm * QK_PAD)
    w_kvb = full["w_kv_b"]
    y_a = matmul("y_a", o_hg, _bf(full["w_hg_o"]), "nn", out_dtype=BF16)

    qn, kvn = rowwise("mla_norms", lambda cq, ckv, gq, gk: (_rms(cq, gq), _rms(ckv, gk)),
                      [(proj_mla, ql, 0), (proj_mla, kvl, 1)], [], [q_a_norm_g, kv_a_norm_g],
                      [(ql, BF16), (kvl, BF16)])
    q_full = matmul("q_up", qn, w_qb, "nn", out_dtype=BF16)
    kv_full = matmul("kv_up", kvn, w_kvb, "nn", out_dtype=BF16)

    def mla_prep_fn(qf, kvf, kpe, cos, s_up, s_dn):
        qf = qf * scale
        kpe_r = _rope(kpe, cos, s_up, s_dn)
        qs, ks, vs = [], [], []
        for h in range(nm):
            qs += [qf[:, h * QK_PAD:h * QK_PAD + HEAD], _rope(qf[:, h * QK_PAD + HEAD:(h + 1) * QK_PAD], cos, s_up, s_dn)]
            ks += [kvf[:, h * QK_PAD:h * QK_PAD + HEAD], kpe_r]
            vs += [kvf[:, h * QK_PAD + HEAD:(h + 1) * QK_PAD]]
        return jnp.concatenate(qs, axis=1), jnp.concatenate(ks, axis=1), jnp.concatenate(vs, axis=1)

    kpe_blk = (ql + kvl) // HEAD
    q_cat, k_cat, v_att = rowwise("mla_prep", mla_prep_fn,
                                  [(q_full, nm * QK_PAD, 0), (kv_full, nm * QK_PAD, 0), (proj_mla, HEAD, kpe_blk)],
                                  [t_cos, t_up, t_dn], [], [(nm * QK_PAD, BF16), (nm * QK_PAD, BF16), (nm * HEAD, BF16)])
    at = _attn_tile(lp)
    v_t = v_att.reshape(bl, lp // at, at, nm, HEAD).transpose(0, 3, 1, 4, 2)
    k_t = k_cat.reshape(bl, lp // at, at, nm, QK_PAD).transpose(0, 3, 1, 4, 2)
    o_mla, lse = attn_fwd_t(q_cat, k_cat, v_t, bl, lp, nm)
    y_b = matmul("y_b", o_mla, _bf(full["w_mla_o"]), "nn", out_dtype=BF16)

    def gate_fn(ya, yb, ga, gb, bias):
        zv = _sigmoid(ga + bias[:, :d]) * ya + _sigmoid(gb + bias[:, d:]) * yb
        return zv, zv

    mixed, z = matmul_fused("gate_mix_out", gate_fn,
                            [(y_a, d, 0), (y_b, d, 0), (proj_main, d, 4), (proj_main, d, 5)], [b_gate],
                            _bf(full["w_out"]), [(0, d)], "nn", [(d, BF16)], tm=512)

    def mid_fn(h, mx, g_post, g_pre):
        h1 = h + _rms(mx, g_post)
        return h1, _rms(h1, g_pre)

    h1, u2 = rowwise("norm_mid", mid_fn, [(h0, d, 0), (mixed, d, 0)], [], [mix_post_g, ffn_pre_g],
                     [(d, F32), (d, BF16)])
    gu = matmul("ffn_in", u2, _bf(full["w_ffn_in"]), "nn", out_dtype=BF16)
    def swiglu_fn(gt, up):
        a = _silu(gt) * up
        return a, a

    f_out, act = matmul_fused("swiglu_ffn_out", swiglu_fn, [(gu, ffn, 0), (gu, ffn, 1)], [],
                              _bf(full["w_ffn_out"]), [(0, ffn)], "nn", [(ffn, BF16)])

    def loss_fn(h1v, fv, tg, realv, g_post):
        h2 = h1v + _rms(fv, g_post)
        diff = (h2 - tg) * realv
        part = jnp.broadcast_to(0.5 * jnp.sum(diff * diff, keepdims=True) / d, (1, HEAD))
        dy = diff / d
        df, dg = _rms_bwd(fv, g_post, dy)
        return dy, df, part, dg

    dy, df, loss_part, g_ffn_post = rowwise(
        "loss_head", loss_fn,
        [(h1, d, 0), (f_out, d, 0), (loss_target.reshape(bl * seq, d), d, 0, real_block)], [real],
        [ffn_post_g], [(d, BF16), (d, BF16)], [(1, HEAD), (1, d)])
    grads = {}
    d_act = matmul("d_act", df, _bf(full["w_ffn_out"]), "nt", out_dtype=BF16)
    grads["w_ffn_out"] = matmul("gw_ffn_out", act, df, "tn")

    def swiglu_bwd_fn(gt, up, da):
        dgt, dup = da * up * _silu_grad(gt), da * _silu(gt)
        return dgt, dup, jnp.concatenate([dgt, dup], axis=1)

    du2, dgu = matmul_fused("swiglu_bwd_d_u2", swiglu_bwd_fn, [(gu, ffn, 0), (gu, ffn, 1), (d_act, ffn, 0)], [],
                            _bf(full["w_ffn_in"]), [(0, ffn), (ffn, 2 * ffn)], "nt", [(2 * ffn, BF16)])
    grads["w_ffn_in"] = matmul("gw_ffn_in", u2, dgu, "tn")

    def mid_bwd_fn(dyv, h1v, du2v, mx, g_pre, g_post):
        dx, dg_pre = _rms_bwd(h1v, g_pre, du2v)
        dh1 = dyv + dx
        dmx, dg_post = _rms_bwd(mx, g_post, dh1)
        return dh1, dmx, dg_pre, dg_post

    dh1, dmixed, g_ffn_pre, g_mix_post = rowwise("norm_mid_bwd", mid_bwd_fn,
                                                 [(dy, d, 0), (h1, d, 0), (du2, d, 0), (mixed, d, 0)], [],
                                                 [ffn_pre_g, mix_post_g], [(d, BF16), (d, BF16)], [(1, d), (1, d)])
    dz = matmul("d_z", dmixed, _bf(full["w_out"]), "nt", out_dtype=BF16)
    grads["w_out"] = matmul("gw_out", z, dmixed, "tn")

    def gate_bwd_fn(dzv, ya, yb, ga, gb, bias):
        sa, sb = _sigmoid(ga + bias[:, :d]), _sigmoid(gb + bias[:, d:])
        dga = dzv * ya * sa * (1.0 - sa)
        dgb = dzv * yb * sb * (1.0 - sb)
        dgates = jnp.concatenate([dga, dgb], axis=1)
        return dzv * sa, dzv * sb, dgates, jnp.sum(dgates, axis=0, keepdims=True)

    dy_a, dy_b, dgates, g_b_gate = rowwise("gate_mix_bwd", gate_bwd_fn,
                                           [(dz, d, 0), (y_a, d, 0), (y_b, d, 0), (proj_main, d, 4), (proj_main, d, 5)],
                                           [], [b_gate], [(d, BF16), (d, BF16), (2 * d, BF16)], [(1, 2 * d)])
    do_hg = matmul("d_o_hg", dy_a, _bf(full["w_hg_o"]), "nt", out_dtype=BF16)
    grads["w_hg_o"] = matmul("gw_hg_o", o_hg, dy_a, "tn")
    do_mla = matmul("d_o_mla", dy_b, _bf(full["w_mla_o"]), "nt", out_dtype=BF16)
    grads["w_mla_o"] = matmul("gw_mla_o", o_mla, dy_b, "tn")

    early = ("w_hg_o", "w_mla_o", "w_out", "w_ffn_in", "w_ffn_out")
    late = ("w_in", "w_q_b", "w_kv_b")

    def start_grads(name, names):
        sends = [_bf(jnp.stack([split_full(n, grads[n], s) for s in range(4)])) for n in names]
        mines = []
        for n in names:
            r, c = wts[n].shape
            axis, size = (1, c) if n in COL_SHARDED else (0, r)
            mines.append(lax.dynamic_slice_in_dim(grads[n], my_chip * size, size, axis=axis))
        handle, token = push_start(name, sends, per_chip=True)
        return handle, token, mines

    def finish_grads(tag, names, started, after):
        handle, _, mines = started
        _, landed = push_wait(f"grads_{tag}_wait", handle, after, per_chip=True)
        parts = []
        for n, mine, land in zip(names, mines, landed):
            r, c = mine.shape
            tr = _tile(r, 256, 16)
            land2 = land.reshape(3 * r, c)
            parts.append(rowwise(f"sum_chips_{n}", lambda a, r0, r1, r2: a + r0 + r1 + r2,
                                 [(mine, c, 0)] + [(land2, c, 0, k * (r // tr)) for k in range(3)],
                                 [], [], [(c, F32)], tm=tr)[0])
        sibs = swap_with_sibling(f"swap_{tag}", parts)
        return {n: [p, s] for n, p, s in zip(names, parts, sibs)}

    grads_early = start_grads("grads_early_start", early)
    token_a = grads_early[1]

    def hg_out_bwd_fn(do, o, hg, g):
        sg = _silu(hg)
        dn = do * sg
        dos, dgs, ons = [], 0.0, []
        for h in range(nh):
            sl = slice(h * HEAD, (h + 1) * HEAD)
            dx, dg = _rms_bwd(o[:, sl], g, dn[:, sl])
            dos.append(dx)
            dgs = dgs + dg
            ons.append(_rms(o[:, sl], g))
        dhg = do * jnp.concatenate(ons, axis=1) * _silu_grad(hg)
        return jnp.concatenate(dos, axis=1), dhg, dgs

    do_scan, dhg, g_hg_norm = rowwise("hgrn_out_bwd", hg_out_bwd_fn, [(do_hg, d, 0), (o_scan, d, 0), (proj_main, d, 3)],
                                      [], [hg_norm_g], [(d, BF16), (d, BF16)], [(1, HEAD)])
    dhq, dhf, dhi, g_lb = hgrn_bwd(proj_main, lb + token_a[0, 0], hg_consts, states, a_mats, do_scan, bl, lp, d)

    dq_cat, dk_cat, dv_att = attn_bwd_t(q_cat, k_cat, k_t, v_att, o_mla, do_mla, lse, bl, lp, nm)

    def mla_prep_bwd_fn(dqc, dkc, dvv, cos, s_up, s_dn):
        dqc = dqc * scale
        dqs, dkvs, dkpe = [], [], 0.0
        for h in range(nm):
            dqs += [dqc[:, h * QK_PAD:h * QK_PAD + HEAD],
                    _rope_bwd(dqc[:, h * QK_PAD + HEAD:(h + 1) * QK_PAD], cos, s_up, s_dn)]
            dkvs += [dkc[:, h * QK_PAD:h * QK_PAD + HEAD], dvv[:, h * HEAD:(h + 1) * HEAD]]
            dkpe = dkpe + dkc[:, h * QK_PAD + HEAD:(h + 1) * QK_PAD]
        return jnp.concatenate(dqs, axis=1), jnp.concatenate(dkvs, axis=1), _rope_bwd(dkpe, cos, s_up, s_dn)

    dq_full, dkv_full, dkpe = rowwise("mla_prep_bwd", mla_prep_bwd_fn,
                                      [(dq_cat, nm * QK_PAD, 0), (dk_cat, nm * QK_PAD, 0), (dv_att, nm * HEAD, 0)],
                                      [t_cos, t_up, t_dn], [],
                                      [(nm * QK_PAD, BF16), (nm * QK_PAD, BF16), (HEAD, F32)])
    dqn = matmul("d_qn", dq_full, w_qb, "nt", out_dtype=BF16)
    g_wqb = matmul("gw_q_b", qn, dq_full, "tn")
    grads["w_q_b"] = g_wqb.reshape(ql, nm, QK_PAD)[:, :, :HEAD + ROPE].reshape(ql, nm * (HEAD + ROPE))
    dkvn = matmul("d_kvn", dkv_full, w_kvb, "nt", out_dtype=BF16)
    grads["w_kv_b"] = matmul("gw_kv_b", kvn, dkv_full, "tn")

    def mla_norms_bwd_fn(dqnv, dkvnv, cq, ckv, dkpev, gq, gk):
        dcq, dgq = _rms_bwd(cq, gq, dqnv)
        dckv, dgk = _rms_bwd(ckv, gk, dkvnv)
        return jnp.concatenate([dcq, dckv, dkpev], axis=1), dgq, dgk

    dmla, g_q_norm, g_kv_norm = rowwise("mla_norms_bwd", mla_norms_bwd_fn,
                                        [(dqn, ql, 0), (dkvn, kvl, 0), (proj_mla, ql, 0), (proj_mla, kvl, 1),
                                         (dkpe, HEAD, 0)], [], [q_a_norm_g, kv_a_norm_g],
                                        [(mla_w, BF16)], [(1, ql), (1, kvl)])

    d_pieces = [dhq, dhf, dhi, dhg, dgates, dmla]
    gw_parts = [matmul(f"gw_in_{k}", u1, dp, "tn") for k, dp in enumerate(d_pieces)]
    grads["w_in"] = jnp.concatenate(gw_parts[:4] + [gw_parts[5][:, :ql + kvl + ROPE], gw_parts[4]], axis=1)
    grads_late = start_grads("grads_late_start", late)
    w_mla_after = w_mla + grads_late[1][0, 0].astype(BF16)
    w_pieces = [w_main[:, 0:d], w_main[:, d:2 * d], w_main[:, 2 * d:3 * d], w_main[:, 3 * d:4 * d],
                w_main[:, 4 * d:6 * d], w_mla_after]
    du1 = matmul("d_u1", d_pieces, w_pieces, "nt", out_dtype=BF16)

    def first_bwd_fn(dh1v, h, du1v, is_meta, g):
        dx, dg = _rms_bwd(h, g, du1v)
        dh0v = dh1v + dx
        return dh0v, dg, dh0v * jnp.tile(is_meta, (1, d // HEAD))

    grad_x, g_mix_pre, meta_tile = rowwise(
        "norm_mix_pre_bwd", first_bwd_fn, [(dh1, d, 0), (h0, d, 0), (du1, d, 0)], [meta_rows], [mix_pre_g],
        [(d, F32, bl * seq, real_block)], [(1, d), (SEQ_BLOCK, d)])
    grad_x = grad_x.reshape(bl, seq, d)

    g_parts = finish_grads("early", early, grads_early, g_mix_pre)
    updates = {}

    def update(n, parts):
        w2 = wts[n].reshape(-1, wts[n].shape[-1])
        updates[n] = adamw("adamw_" + n, w2, [p.reshape(w2.shape) for p in parts], mom_m[n].reshape(w2.shape),
                           mom_v[n].reshape(w2.shape))

    for n in early:
        update(n, g_parts[n])
    g_parts = finish_grads("late", late, grads_late, updates[early[-1]][0])
    for n in late:
        update(n, g_parts[n])
    p0 = lb_soft[0:1]
    g_lb_logits = jnp.concatenate([g_lb * p0 * (1.0 - p0), -g_lb * p0 * (1.0 - p0)], axis=0)

    def row_of(vec):
        return vec.reshape(-1, d) if vec.size >= d else jnp.pad(vec.reshape(1, -1), ((0, 0), (0, d - vec.size)))

    small_parts = dict(b_gate=g_b_gate, lb_logits=g_lb_logits, hg_norm_g=g_hg_norm, q_a_norm_g=g_q_norm,
                       kv_a_norm_g=g_kv_norm, mix_pre_g=g_mix_pre, mix_post_g=g_mix_post, ffn_pre_g=g_ffn_pre,
                       ffn_post_g=g_ffn_post)
    g_meta = meta_tile[PAD_FRONT:PAD_FRONT + N_META]
    small_rows = [row_of(small_parts[n]) for n in SMALL] + [row_of(g_meta)]
    n_small = sum(r.shape[0] for r in small_rows)
    small = jnp.pad(jnp.concatenate(small_rows, axis=0), ((0, -(-n_small // 8) * 8 - n_small), (0, 0)))
    all_small = gather_small(small)
    small_t = small.shape[0]

    def sum8_fn(*slabs):
        acc = slabs[0]
        for s in slabs[1:]:
            acc = acc + s
        return acc

    (g_small,) = rowwise("sum_small", sum8_fn, [(all_small.reshape(8 * small_t, d), d, 0, k) for k in range(8)],
                         [], [], [(d, F32)], tm=small_t, n_rows=small_t)

    off = 0
    for n, part in zip(SMALL, small_rows[:-1]):
        rows = part.shape[0]
        update(n, [g_small[off:off + rows, :d].reshape(-1)[:wts[n].size]])
        off += rows
    update("meta_tokens", [lax.dynamic_slice_in_dim(g_small[off:off + N_META, :d], my_chip * mcols, mcols, axis=1)])

    loss = lax.psum(loss_part[0, 0], ("x", "y", "c"))

    def shaped(n, a):
        return a.reshape((1,) + wts[n].shape) if n in BIG else a.reshape(wts[n].shape)

    return (loss, grad_x, *[shaped(n, updates[n][k]) for k in range(4) for n in WEIGHTS])
```

```python
import functools
import math

import jax
import jax.numpy as jnp
from jax import lax
from jax.experimental import pallas as pl
from jax.experimental.pallas import tpu as pltpu

F32 = jnp.float32
BF16 = jnp.bfloat16
MESH = pl.DeviceIdType.MESH

N_META = 16
NORM_EPS = 1e-6
HEAD = 128
ROPE = 64
ROPE_HALF = ROPE // 2
QK_PAD = 2 * HEAD
ROPE_THETA = 10000.0
SEQ_BLOCK = 256
PAD_FRONT = SEQ_BLOCK - N_META
NEG = -1e30
VMEM_LIMIT = 56 * 1024 * 1024
ATTN_HEADS_PER_STEP = 1
ATTN_TILE_MAX = 768

ADAM_LR, ADAM_B1, ADAM_B2, ADAM_EPS, ADAM_WD, ADAM_STEP = 0.001, 0.9, 0.999, 1e-08, 0.01, 10

BIG = ("w_in", "w_hg_o", "w_q_b", "w_kv_b", "w_mla_o", "w_out", "w_ffn_in", "w_ffn_out")
COL_SHARDED = ("w_in", "w_q_b", "w_kv_b", "w_ffn_in")
SMALL = ("b_gate", "lb_logits", "hg_norm_g", "q_a_norm_g", "kv_a_norm_g", "mix_pre_g", "mix_post_g",
         "ffn_pre_g", "ffn_post_g")
WEIGHTS = ("meta_tokens", "w_in", "b_gate", "lb_logits", "hg_norm_g", "w_hg_o", "q_a_norm_g", "w_q_b",
           "kv_a_norm_g", "w_kv_b", "w_mla_o", "w_out", "mix_pre_g", "mix_post_g", "ffn_pre_g", "ffn_post_g",
           "w_ffn_in", "w_ffn_out")


def _tile(n, cap, unit=128):
    if n <= cap:
        return n
    best = None
    for t in range(unit, cap + 1, unit):
        if n % t == 0:
            best = t
    assert best is not None, (n, cap, unit)
    return best


def _sigmoid(x):
    return 1.0 / (1.0 + jnp.exp(-x))


def _bf(x):
    return x.astype(BF16)


def rowwise(name, fn, row_ins, seq_tabs, consts, row_outs, acc_outs=(), tm=SEQ_BLOCK, n_rows=None):
    t_rows = row_ins[0][0].shape[0] if n_rows is None else n_rows
    nt = t_rows // tm
    assert t_rows % tm == 0
    n_in = len(row_ins) + len(seq_tabs) + len(consts)
    n_row = len(row_outs)

    def body(*refs):
        vals = [r[...].astype(F32) for r in refs[:n_in]]
        res = fn(*vals)
        if not isinstance(res, (tuple, list)):
            res = (res,)
        outs = refs[n_in:]
        for k in range(n_row):
            outs[k][...] = res[k].astype(outs[k].dtype)
        if acc_outs:
            @pl.when(pl.program_id(0) == 0)
            def _():
                for k in range(len(acc_outs)):
                    outs[n_row + k][...] = jnp.zeros_like(outs[n_row + k])

            for k in range(len(acc_outs)):
                outs[n_row + k][...] += res[n_row + k]

    row_ins = [tuple(e) + (0,) * (4 - len(e)) for e in row_ins]
    in_specs = [pl.BlockSpec((tm, w), functools.partial(lambda i, j, ro: (ro(i) if callable(ro) else i + ro, j),
                                                        j=j, ro=ro)) for (_, w, j, ro) in row_ins]
    for tab in seq_tabs:
        per = tab.shape[0] // tm
        in_specs.append(pl.BlockSpec((tm, tab.shape[1]), functools.partial(lambda i, per: (i % per, 0), per=per)))
    for c in consts:
        in_specs.append(pl.BlockSpec(c.shape, lambda i: (0, 0)))
    row_outs = [tuple(e) + (t_rows, None)[len(e) - 2:] for e in row_outs]
    out_specs = [pl.BlockSpec((tm, w), functools.partial(lambda i, rm: (i if rm is None else rm(i), 0), rm=rm))
                 for (w, _, _, rm) in row_outs]
    out_specs += [pl.BlockSpec(s, lambda i: (0, 0)) for s in acc_outs]
    out_shape = [jax.ShapeDtypeStruct((rows, w), dt) for (w, dt, rows, _) in row_outs]
    out_shape += [jax.ShapeDtypeStruct(s, F32) for s in acc_outs]
    res = pl.pallas_call(
        body, name=name, grid=(nt,), in_specs=in_specs, out_specs=out_specs, out_shape=out_shape,
        compiler_params=pltpu.CompilerParams(dimension_semantics=("arbitrary",)),
    )(*[e[0] for e in row_ins], *seq_tabs, *consts)
    return res


def matmul(name, a, b, mode, out_dtype=F32):
    if mode != "tn":
        return _matmul_resident(name, a if isinstance(a, (list, tuple)) else [a],
                                b if isinstance(b, (list, tuple)) else [b], mode, out_dtype)
    kdim, m = a.shape
    n = b.shape[1]
    tn = _tile(n, 1536)
    tm, tk = _tile(m, 1408 if tn <= 1024 else 1024), _tile(kdim, 1536)
    nk = kdim // tk

    def body(a_ref, b_ref, o_ref, acc_ref):
        k = pl.program_id(2)

        @pl.when(k == 0)
        def _():
            acc_ref[...] = jnp.zeros_like(acc_ref)

        acc_ref[...] += lax.dot_general(a_ref[...], b_ref[...], TN_DIMS, preferred_element_type=F32)

        @pl.when(k == nk - 1)
        def _():
            o_ref[...] = acc_ref[...].astype(o_ref.dtype)

    return pl.pallas_call(
        body, name=name, grid=(m // tm, n // tn, nk),
        in_specs=[pl.BlockSpec((tk, tm), lambda i, j, k: (k, i)), pl.BlockSpec((tk, tn), lambda i, j, k: (k, j))],
        out_specs=pl.BlockSpec((tm, tn), lambda i, j, k: (i, j)),
        out_shape=jax.ShapeDtypeStruct((m, n), out_dtype),
        scratch_shapes=[pltpu.VMEM((tm, tn), F32)],
        compiler_params=pltpu.CompilerParams(dimension_semantics=("arbitrary", "arbitrary", "arbitrary"),
                                             vmem_limit_bytes=VMEM_LIMIT),
    )(a, b)


def _matmul_resident(name, a_list, b_list, mode, out_dtype):
    m = a_list[0].shape[0]
    n = b_list[0].shape[1] if mode == "nn" else b_list[0].shape[0]
    k_total = sum(a.shape[1] for a in a_list)
    out_bytes = 2 if out_dtype == BF16 else 4
    budget = VMEM_LIMIT - 4 * k_total * n - (6 << 20)
    tm = 1024
    while tm > 128 and 2 * tm * (2 * k_total + out_bytes * n) > budget:
        tm //= 2
    tm = _tile(m, tm)
    cn = _tile(n, 1024)
    npairs = len(a_list)

    def body(*refs):
        a_refs, b_refs, o_ref = refs[:npairs], refs[npairs:2 * npairs], refs[2 * npairs]
        for c in range(n // cn):
            acc = None
            for a_ref, b_ref in zip(a_refs, b_refs):
                if mode == "nn":
                    part = jnp.dot(a_ref[...], b_ref[:, pl.ds(c * cn, cn)], preferred_element_type=F32)
                else:
                    part = lax.dot_general(a_ref[...], b_ref[pl.ds(c * cn, cn), :], NT_DIMS,
                                           preferred_element_type=F32)
                acc = part if acc is None else acc + part
            o_ref[:, pl.ds(c * cn, cn)] = acc.astype(o_ref.dtype)

    in_specs = [pl.BlockSpec((tm, a.shape[1]), lambda i: (i, 0)) for a in a_list]
    in_specs += [pl.BlockSpec(b.shape, lambda i: (0, 0)) for b in b_list]
    return pl.pallas_call(
        body, name=name, grid=(m // tm,), in_specs=in_specs,
        out_specs=pl.BlockSpec((tm, n), lambda i: (i, 0)),
        out_shape=jax.ShapeDtypeStruct((m, n), out_dtype),
        compiler_params=pltpu.CompilerParams(dimension_semantics=("arbitrary",), vmem_limit_bytes=VMEM_LIMIT),
    )(*a_list, *b_list)


def matmul_fused(name, fn, row_ins, consts, weight, pieces, mode, extra_outs, out_dtype=BF16, tm=256):
    row_ins = [tuple(e) + (0,) * (4 - len(e)) for e in row_ins]
    t_rows = row_ins[0][0].shape[0]
    tm = _tile(t_rows, tm)
    n = weight.shape[1] if mode == "nn" else weight.shape[0]
    n_in = len(row_ins) + len(consts)
    n_parts = len(pieces)

    def body(*refs):
        w_hbm = refs[n_in]
        outs = refs[n_in + 1:n_in + 2 + len(extra_outs)]
        w_ref, sem = refs[-2], refs[-1]

        @pl.when(pl.program_id(0) == 0)
        def _():
            cp = pltpu.make_async_copy(w_hbm, w_ref, sem)
            cp.start()
            cp.wait()

        res = fn(*[r[...].astype(F32) for r in refs[:n_in]])
        acc = None
        for a_p, (k0, k1) in zip(res[:n_parts], pieces):
            if mode == "nn":
                part = jnp.dot(_bf(a_p), w_ref[pl.ds(k0, k1 - k0), :], preferred_element_type=F32)
            else:
                part = lax.dot_general(_bf(a_p), w_ref[:, pl.ds(k0, k1 - k0)], NT_DIMS, preferred_element_type=F32)
            acc = part if acc is None else acc + part
        outs[0][...] = acc.astype(outs[0].dtype)
        for o_ref, val in zip(outs[1:], res[n_parts:]):
            o_ref[...] = val.astype(o_ref.dtype)

    in_specs = [pl.BlockSpec((tm, w), functools.partial(lambda i, j, ro: (i + ro, j), j=j, ro=ro))
                for (_, w, j, ro) in row_ins]
    in_specs += [pl.BlockSpec(c.shape, lambda i: (0, 0)) for c in consts]
    in_specs.append(pl.BlockSpec(memory_space=pl.ANY))
    widths = [(n, out_dtype)] + list(extra_outs)
    return pl.pallas_call(
        body, name=name, grid=(t_rows // tm,), in_specs=in_specs,
        out_specs=[pl.BlockSpec((tm, w), lambda i: (i, 0)) for (w, _) in widths],
        out_shape=[jax.ShapeDtypeStruct((t_rows, w), dt) for (w, dt) in widths],
        scratch_shapes=[pltpu.VMEM(weight.shape, weight.dtype), pltpu.SemaphoreType.DMA],
        compiler_params=pltpu.CompilerParams(dimension_semantics=("arbitrary",), vmem_limit_bytes=VMEM_LIMIT),
    )(*[e[0] for e in row_ins], *consts, weight)


def _rms(x, g):
    r = lax.rsqrt(jnp.mean(x * x, axis=-1, keepdims=True) + NORM_EPS)
    return x * r * g


def _rms_bwd(x, g, dy):
    r = lax.rsqrt(jnp.mean(x * x, axis=-1, keepdims=True) + NORM_EPS)
    xh = x * r
    dyg = dy * g
    dx = r * (dyg - xh * jnp.mean(dyg * xh, axis=-1, keepdims=True))
    return dx, jnp.sum(dy * xh, axis=0, keepdims=True)


def _silu(x):
    return x * _sigmoid(x)


def _silu_grad(x):
    s = _sigmoid(x)
    return s * (1.0 + x * (1.0 - s))


def _rope(xs, cos, s_up, s_dn):
    return xs * cos + pltpu.roll(xs, ROPE_HALF, 1) * s_up + pltpu.roll(xs, HEAD - ROPE_HALF, 1) * s_dn


def _rope_bwd(dy, cos, s_up, s_dn):
    return dy * cos + pltpu.roll(dy * s_up, HEAD - ROPE_HALF, 1) + pltpu.roll(dy * s_dn, ROPE_HALF, 1)


HG_SUB = 128
HG_LEVELS = 7
HG_E_ROWS = (HG_LEVELS + 1) * HG_SUB
HG_BWD_GROUP = 6
TN_DIMS = (((0,), (0,)), ((), ()))
NT_DIMS = (((1,), (1,)), ((), ()))


def _hg_constants():
    import numpy as np
    n = HG_SUB
    r = np.arange(n)[:, None]
    c = np.arange(n)[None, :]
    cs, ps = [], []
    for lvl in range(HG_LEVELS):
        m = (n // 2) >> lvl
        upper = (r % (2 * m)) >= m
        mid = (r // (2 * m)) * (2 * m) + m - 1
        cs.append(np.where(upper, (c > mid) & (c <= r), (c > r) & (c <= mid)))
        ps.append(((r // (2 * m)) == (c // (2 * m))) & upper & ((c % (2 * m)) < m))
    cs.append(c <= r)
    cs.append(np.ones((8, n), bool))
    cstack = np.concatenate(cs, 0).astype(np.float32)
    pstack = np.concatenate(ps, 0).astype(np.float32)
    pstack_t = np.concatenate([p.T for p in ps], 0).astype(np.float32)
    return (jnp.asarray(cstack, BF16), jnp.asarray(cstack[:HG_E_ROWS].T, BF16), jnp.asarray(pstack, F32),
            jnp.asarray(pstack_t, F32))


def _split_dot(c_bf, x):
    hi = _bf(x)
    lo = _bf(x - hi.astype(F32))
    r2 = jnp.dot(c_bf, jnp.concatenate([hi, lo], axis=1), preferred_element_type=F32)
    return r2[:, :HEAD] + r2[:, HEAD:]


def _hg_gates(hq, hf, lb):
    sq = _sigmoid(hq)
    sg = _sigmoid(hf)
    fg = lb + (1.0 - lb) * sg
    return sq, hq * sq, sg, fg, 1.0 - fg, jnp.log(fg)


def hgrn_fwd(proj_main, lb, consts, bl, lp, d):
    nh = d // HEAD
    rows_blk = _tile(lp, 768, SEQ_BLOCK)
    nb = lp // rows_blk
    spb = rows_blk // HG_SUB
    cstack, _, pstack, _ = consts

    def body(hq_ref, hf_ref, hi_ref, lb_ref, c_ref, p_ref, o_ref, st_ref, a_ref, s_ref):
        j = pl.program_id(2)

        @pl.when(j == 0)
        def _():
            s_ref[...] = jnp.zeros_like(s_ref)

        lbv = lb_ref[...]
        cs = c_ref[...]
        rows = [pl.ds(s * HG_SUB, HG_SUB) for s in range(spb)]
        gates = [_hg_gates(hq_ref[r, :].astype(F32), hf_ref[r, :].astype(F32), lbv) for r in rows]
        qs, ks = [g_[1] for g_ in gates], [g_[4] for g_ in gates]
        vs = [hi_ref[r, :].astype(F32) for r in rows]
        es = [_split_dot(cs, g_[5]) for g_ in gates]
        a_acc = [jnp.zeros((HG_SUB, HG_SUB), F32) for _ in rows]
        for lvl in range(HG_LEVELS):
            for s in range(spb):
                x = jnp.exp(es[s][lvl * HG_SUB:(lvl + 1) * HG_SUB])
                a_acc[s] = a_acc[s] + p_ref[pl.ds(lvl * HG_SUB, HG_SUB), :] * lax.dot_general(
                    _bf(qs[s] * x), _bf(ks[s] * x), NT_DIMS, preferred_element_type=F32)
        o_intra, qbs, kds, e_lasts = [], [], [], []
        for s in range(spb):
            a_bf = _bf(a_acc[s])
            a_ref[0, 0, s] = a_bf
            bc = es[s][HG_LEVELS * HG_SUB:HG_E_ROWS]
            b_last = jnp.tile(es[s][HG_E_ROWS:], (HG_SUB // 8, 1))
            o_intra.append(jnp.dot(a_bf, _bf(vs[s]), preferred_element_type=F32)
                           + jnp.sum(qs[s] * ks[s], axis=1, keepdims=True) * vs[s])
            qbs.append(_bf(qs[s] * jnp.exp(bc)))
            kds.append(_bf(ks[s] * jnp.exp(b_last - bc)))
            e_lasts.append(jnp.exp(b_last))
        st = s_ref[...]
        for s in range(spb):
            st_ref[0, 0, s] = st
            o_ref[rows[s], :] = (o_intra[s] + lax.dot_general(qbs[s], _bf(st), NT_DIMS, preferred_element_type=F32)
                                 ).astype(o_ref.dtype)
            st = st * e_lasts[s] + lax.dot_general(_bf(vs[s]), kds[s], TN_DIMS, preferred_element_type=F32)
        s_ref[...] = st

    def colspec(off):
        return pl.BlockSpec((rows_blk, HEAD), functools.partial(lambda h, b, j, off: (b * nb + j, off + h), off=off))

    whole = lambda arr: pl.BlockSpec(arr.shape, lambda h, b, j: (0, 0))
    return pl.pallas_call(
        body, name="hgrn_fwd", grid=(nh, bl, nb),
        in_specs=[colspec(0), colspec(nh), colspec(2 * nh), pl.BlockSpec((1, HEAD), lambda h, b, j: (0, h)),
                  whole(cstack), whole(pstack)],
        out_specs=[pl.BlockSpec((rows_blk, HEAD), lambda h, b, j: (b * nb + j, h)),
                   pl.BlockSpec((1, 1, spb, HEAD, HEAD), lambda h, b, j: (b, h, j, 0, 0)),
                   pl.BlockSpec((1, 1, spb, HG_SUB, HG_SUB), lambda h, b, j: (b, h, j, 0, 0))],
        out_shape=[jax.ShapeDtypeStruct((bl * lp, d), BF16),
                   jax.ShapeDtypeStruct((bl, nh, lp // HG_SUB, HEAD, HEAD), F32),
                   jax.ShapeDtypeStruct((bl, nh, lp // HG_SUB, HG_SUB, HG_SUB), BF16)],
        scratch_shapes=[pltpu.VMEM((HEAD, HEAD), F32)],
        compiler_params=pltpu.CompilerParams(dimension_semantics=("arbitrary", "arbitrary", "arbitrary")),
    )(proj_main, proj_main, proj_main, lb, cstack, pstack)


def hgrn_bwd(proj_main, lb, consts, states, a_mats, do_scan, bl, lp, d):
    nh = d // HEAD
    rows_blk = _tile(lp, 768, SEQ_BLOCK)
    nb = lp // rows_blk
    spb = rows_blk // HG_SUB
    cstack, cstack_t = consts[0], consts[1]
    pstack, pstack_t = _bf(consts[2]), _bf(consts[3])

    def body(hq_ref, hf_ref, hi_ref, lb_ref, c_ref, ct_ref, p_ref, pt_ref, st_ref, a_ref, do_ref,
             dq_ref, df_ref, di_ref, dlb_ref, ds_ref):
        b_id, j = pl.program_id(1), pl.program_id(2)
        blk = nb - 1 - j

        @pl.when(j == 0)
        def _():
            ds_ref[...] = jnp.zeros_like(ds_ref)

        @pl.when((j == 0) & (b_id == 0))
        def _():
            dlb_ref[...] = jnp.zeros_like(dlb_ref)

        lbv = lb_ref[...]
        cs = c_ref[...]
        cst = ct_ref[...]

        dlb = jnp.zeros((1, HEAD), F32)
        for first in reversed(range(0, spb, HG_BWD_GROUP)):
            dlb = dlb + _hg_group_bwd(list(range(first, min(first + HG_BWD_GROUP, spb))), lbv, cs, cst, hq_ref,
                                      hf_ref, hi_ref, st_ref, a_ref, do_ref, p_ref, pt_ref, dq_ref, df_ref, di_ref,
                                      ds_ref)
        dlb_ref[...] += dlb

    def _hg_group_bwd(ids, lbv, cs, cst, hq_ref, hf_ref, hi_ref, st_ref, a_ref, do_ref, p_ref, pt_ref, dq_ref,
                      df_ref, di_ref, ds_ref):
        rng = range(len(ids))
        rows = [pl.ds(s * HG_SUB, HG_SUB) for s in ids]
        hqs = [hq_ref[r, :].astype(F32) for r in rows]
        gates = [_hg_gates(hqs[s], hf_ref[rows[s], :].astype(F32), lbv) for s in rng]
        sqs, qs, sgs, fgs, ks = ([g_[i] for g_ in gates] for i in range(5))
        vs = [hi_ref[r, :].astype(F32) for r in rows]
        dos = [do_ref[r, :].astype(F32) for r in rows]
        sts = [st_ref[0, 0, s] for s in ids]
        es = [_split_dot(cs, g_[5]) for g_ in gates]
        bcs = [e[HG_LEVELS * HG_SUB:HG_E_ROWS] for e in es]
        b_lasts = [jnp.tile(e[HG_E_ROWS:], (HG_SUB // 8, 1)) for e in es]
        ebs = [jnp.exp(bc) for bc in bcs]
        qbs = [qs[s] * ebs[s] for s in rng]
        ers = [jnp.exp(b_lasts[s] - bcs[s]) for s in rng]
        kds = [ks[s] * ers[s] for s in rng]
        e_lasts = [jnp.exp(b) for b in b_lasts]
        do_bfs, v_bfs = [_bf(x) for x in dos], [_bf(x) for x in vs]
        das = [_bf(lax.dot_general(do_bfs[s], v_bfs[s], NT_DIMS, preferred_element_type=F32)) for s in rng]
        dats = [_bf(lax.dot_general(v_bfs[s], do_bfs[s], NT_DIMS, preferred_element_type=F32)) for s in rng]
        dqbs = [jnp.dot(do_bfs[s], _bf(sts[s]), preferred_element_type=F32) for s in rng]
        m_s = [lax.dot_general(do_bfs[s], _bf(qbs[s]), TN_DIMS, preferred_element_type=F32) for s in rng]
        dst_outs = [None] * len(ids)
        dst = ds_ref[...]
        for s in reversed(rng):
            dst_outs[s] = dst
            dst = dst * e_lasts[s] + m_s[s]
        ds_ref[...] = dst
        dst_bfs = [_bf(x) for x in dst_outs]
        d_diags = [jnp.sum(dos[s] * vs[s], axis=1, keepdims=True) for s in rng]
        dvs = [lax.dot_general(a_ref[0, 0, ids[s]], do_bfs[s], TN_DIMS, preferred_element_type=F32)
               + jnp.sum(qs[s] * ks[s], axis=1, keepdims=True) * dos[s]
               + lax.dot_general(_bf(kds[s]), dst_bfs[s], NT_DIMS, preferred_element_type=F32) for s in rng]
        dkds = [jnp.dot(v_bfs[s], dst_bfs[s], preferred_element_type=F32) for s in rng]
        dqs = [dqbs[s] * ebs[s] + d_diags[s] * ks[s] for s in rng]
        dks = [dkds[s] * ers[s] + d_diags[s] * qs[s] for s in rng]
        d_lasts = [jnp.sum(dst_outs[s] * sts[s] * e_lasts[s], axis=0, keepdims=True)
                   + jnp.sum(dkds[s] * kds[s], axis=0, keepdims=True) for s in rng]
        des = [[] for _ in rng]
        for lvl in range(HG_LEVELS):
            for s in rng:
                x = jnp.exp(es[s][lvl * HG_SUB:(lvl + 1) * HG_SUB])
                qh, kh = qs[s] * x, ks[s] * x
                dm = p_ref[pl.ds(lvl * HG_SUB, HG_SUB), :] * das[s]
                dmt = pt_ref[pl.ds(lvl * HG_SUB, HG_SUB), :] * dats[s]
                dqh = jnp.dot(dm, _bf(kh), preferred_element_type=F32)
                dkh = jnp.dot(dmt, _bf(qh), preferred_element_type=F32)
                dqs[s] = dqs[s] + dqh * x
                dks[s] = dks[s] + dkh * x
                des[s].append(dqh * qh + dkh * kh)
        dlb = jnp.zeros((1, HEAD), F32)
        for s in rng:
            des[s].append(dqbs[s] * qbs[s] - dkds[s] * kds[s])
            dg = _split_dot(cst, jnp.concatenate(des[s], axis=0)) + d_lasts[s]
            dfg = dg / fgs[s] - dks[s]
            dq_ref[rows[s], :] = (dqs[s] * (sqs[s] * (1.0 + hqs[s] * (1.0 - sqs[s])))).astype(dq_ref.dtype)
            df_ref[rows[s], :] = (dfg * (1.0 - lbv) * sgs[s] * (1.0 - sgs[s])).astype(df_ref.dtype)
            di_ref[rows[s], :] = dvs[s].astype(di_ref.dtype)
            dlb = dlb + jnp.sum(dfg * (1.0 - sgs[s]), axis=0, keepdims=True)
        return dlb

    def colspec(off):
        return pl.BlockSpec((rows_blk, HEAD),
                            functools.partial(lambda h, b, j, off: (b * nb + nb - 1 - j, off + h), off=off))

    whole = lambda arr: pl.BlockSpec(arr.shape, lambda h, b, j: (0, 0))
    mats = lambda: pl.BlockSpec((1, 1, spb, HEAD, HEAD), lambda h, b, j: (b, h, nb - 1 - j, 0, 0))
    t_rows = bl * lp
    return pl.pallas_call(
        body, name="hgrn_bwd", grid=(nh, bl, nb),
        in_specs=[colspec(0), colspec(nh), colspec(2 * nh), pl.BlockSpec((1, HEAD), lambda h, b, j: (0, h)),
                  whole(cstack), whole(cstack_t), whole(pstack), whole(pstack_t), mats(), mats(), colspec(0)],
        out_specs=[colspec(0), colspec(0), colspec(0), pl.BlockSpec((1, HEAD), lambda h, b, j: (0, h))],
        out_shape=[jax.ShapeDtypeStruct((t_rows, d), BF16)] * 3 + [jax.ShapeDtypeStruct((1, d), F32)],
        scratch_shapes=[pltpu.VMEM((HEAD, HEAD), F32)],
        compiler_params=pltpu.CompilerParams(dimension_semantics=("arbitrary", "arbitrary", "arbitrary")),
    )(proj_main, proj_main, proj_main, lb, cstack, cstack_t, pstack, pstack_t, states, a_mats, do_scan)


def _key_query_mask(key0, qry0, nk, nq_, causal):
    key = key0 + lax.broadcasted_iota(jnp.int32, (nk, 1), 0)
    if not causal:
        return key >= PAD_FRONT
    qry = qry0 + lax.broadcasted_iota(jnp.int32, (1, nq_), 1)
    return (key <= qry) & (key >= PAD_FRONT)


def _attn_tile(lp):
    return _tile(lp, ATTN_TILE_MAX, SEQ_BLOCK)


def attn_fwd_t(q_cat, k_cat, v_t, bl, lp, nm):
    tq = tk = _attn_tile(lp)
    nq = lp // tq
    hp = ATTN_HEADS_PER_STEP
    assert nm % hp == 0

    def body(q_ref, k_ref, vt_ref, o_ref, lse_ref, m_ref, l_ref, acc_ref):
        i = pl.program_id(2)
        m_ref[...] = jnp.full_like(m_ref, NEG)
        l_ref[...] = jnp.zeros_like(l_ref)
        acc_ref[...] = jnp.zeros_like(acc_ref)

        def step(c, mask):
            c0 = pl.multiple_of(c * tk, tk)
            for hh in range(hp):
                cols = pl.ds(hh * QK_PAD, QK_PAD)
                st = lax.dot_general(k_ref[pl.ds(c0, tk), cols], q_ref[:, cols], NT_DIMS,
                                     preferred_element_type=F32)
                if mask is not None:
                    st = jnp.where(_key_query_mask(c * tk, i * tq, tk, tq, mask == "causal"), st, NEG)
                m_old = m_ref[hh]
                m_new = jnp.maximum(m_old, jnp.max(st, axis=0, keepdims=True))
                alpha = jnp.exp(m_old - m_new)
                pt = jnp.exp(st - m_new)
                l_ref[hh] = alpha * l_ref[hh] + jnp.sum(pt, axis=0, keepdims=True)
                acc_ref[hh] = alpha * acc_ref[hh] + jnp.dot(vt_ref[0, hh, pl.ds(c, 1)][0], _bf(pt),
                                                            preferred_element_type=F32)
                m_ref[hh] = m_new

        def mid(c, carry):
            step(c, None)
            return carry

        @pl.when(i == 0)
        def _():
            step(0, "causal")

        @pl.when(i > 0)
        def _():
            step(0, "pad")
            lax.fori_loop(1, i, mid, 0)
            step(i, "causal")

        for hh in range(hp):
            o_ref[:, pl.ds(hh * HEAD, HEAD)] = jnp.transpose(acc_ref[hh] / l_ref[hh]).astype(o_ref.dtype)
            lse_ref[0, hh, 0] = m_ref[hh] + jnp.log(l_ref[hh])

    return pl.pallas_call(
        body, name="attn_fwd", grid=(bl, nm // hp, nq),
        in_specs=[pl.BlockSpec((tq, hp * QK_PAD), lambda b, h, i: (b * nq + i, h)),
                  pl.BlockSpec((lp, hp * QK_PAD), lambda b, h, i: (b, h)),
                  pl.BlockSpec((1, hp, nq, HEAD, tk), lambda b, h, i: (b, h, 0, 0, 0))],
        out_specs=[pl.BlockSpec((tq, hp * HEAD), lambda b, h, i: (b * nq + i, h)),
                   pl.BlockSpec((1, hp, 1, 1, tq), lambda b, h, i: (b, h, i, 0, 0))],
        out_shape=[jax.ShapeDtypeStruct((bl * lp, nm * HEAD), BF16),
                   jax.ShapeDtypeStruct((bl, nm, nq, 1, tq), F32)],
        scratch_shapes=[pltpu.VMEM((hp, 1, tq), F32), pltpu.VMEM((hp, 1, tq), F32), pltpu.VMEM((hp, HEAD, tq), F32)],
        compiler_params=pltpu.CompilerParams(dimension_semantics=("arbitrary", "arbitrary", "arbitrary")),
    )(q_cat, k_cat, v_t)


def attn_bwd_t(q_cat, k_cat, k_t, v, o, do, lse, bl, lp, nm):
    tq = tk = _attn_tile(lp)
    nq = lp // tq
    hp = ATTN_HEADS_PER_STEP
    assert nm % hp == 0

    def body(q_ref, k_ref, kt_ref, v_ref, o_ref, do_ref, lse_ref, dq_ref, dk_ref, dv_ref, dqt_ref, dka_ref, dva_ref):
        i = pl.program_id(2)

        @pl.when(i == 0)
        def _():
            dqt_ref[...] = jnp.zeros_like(dqt_ref)

        dka_ref[...] = jnp.zeros_like(dka_ref)
        dva_ref[...] = jnp.zeros_like(dva_ref)
        ones8 = jnp.ones((8, HEAD), BF16)

        def step(c, mask):
            c0 = pl.multiple_of(c * tq, tq)
            for hh in range(hp):
                qcols, vcols = pl.ds(hh * QK_PAD, QK_PAD), pl.ds(hh * HEAD, HEAD)
                qs = q_ref[pl.ds(c0, tq), qcols]
                dos = do_ref[pl.ds(c0, tq), vcols]
                prod = dos.astype(F32) * o_ref[pl.ds(c0, tq), vcols].astype(F32)
                hi = _bf(prod)
                lo = _bf(prod - hi.astype(F32))
                delta8 = (lax.dot_general(ones8, hi, NT_DIMS, preferred_element_type=F32)
                          + lax.dot_general(ones8, lo, NT_DIMS, preferred_element_type=F32))
                st = lax.dot_general(k_ref[:, qcols], qs, NT_DIMS, preferred_element_type=F32)
                pt = jnp.exp(st - lse_ref[0, hh, pl.ds(c, 1)][0])
                if mask is not None:
                    pt = jnp.where(_key_query_mask(i * tk, c * tq, tk, tq, mask == "causal"), pt, 0.0)
                dva_ref[hh] += jnp.dot(_bf(pt), dos, preferred_element_type=F32)
                dpt = lax.dot_general(v_ref[:, vcols], dos, NT_DIMS, preferred_element_type=F32)
                dst = _bf(pt * (dpt - jnp.tile(delta8, (tk // 8, 1))))
                dka_ref[hh] += jnp.dot(dst, qs, preferred_element_type=F32)
                dqt_ref[hh, pl.ds(c, 1)] += jnp.dot(kt_ref[0, hh, 0], dst, preferred_element_type=F32)[None]

        step(i, "causal")

        def rest_masked(c, carry):
            step(c, "pad")
            return carry

        def rest(c, carry):
            step(c, None)
            return carry

        @pl.when(i == 0)
        def _():
            lax.fori_loop(1, nq, rest_masked, 0)

        @pl.when(i > 0)
        def _():
            lax.fori_loop(i + 1, nq, rest, 0)

        for hh in range(hp):
            dk_ref[:, pl.ds(hh * QK_PAD, QK_PAD)] = dka_ref[hh].astype(dk_ref.dtype)
            dv_ref[:, pl.ds(hh * HEAD, HEAD)] = dva_ref[hh].astype(dv_ref.dtype)

        @pl.when(i == nq - 1)
        def _():
            for hh in range(hp):
                for c in range(nq):
                    dq_ref[pl.ds(c * tq, tq), pl.ds(hh * QK_PAD, QK_PAD)] = (
                        jnp.transpose(dqt_ref[hh, c])).astype(dq_ref.dtype)

    return pl.pallas_call(
        body, name="attn_bwd", grid=(bl, nm // hp, nq),
        in_specs=[pl.BlockSpec((lp, hp * QK_PAD), lambda b, h, i: (b, h)),
                  pl.BlockSpec((tk, hp * QK_PAD), lambda b, h, i: (b * nq + i, h)),
                  pl.BlockSpec((1, hp, 1, QK_PAD, tk), lambda b, h, i: (b, h, i, 0, 0)),
                  pl.BlockSpec((tk, hp * HEAD), lambda b, h, i: (b * nq + i, h)),
                  pl.BlockSpec((lp, hp * HEAD), lambda b, h, i: (b, h)),
                  pl.BlockSpec((lp, hp * HEAD), lambda b, h, i: (b, h)),
                  pl.BlockSpec((1, hp, nq, 1, tq), lambda b, h, i: (b, h, 0, 0, 0))],
        out_specs=[pl.BlockSpec((lp, hp * QK_PAD), lambda b, h, i: (b, h)),
                   pl.BlockSpec((tk, hp * QK_PAD), lambda b, h, i: (b * nq + i, h)),
                   pl.BlockSpec((tk, hp * HEAD), lambda b, h, i: (b * nq + i, h))],
        out_shape=[jax.ShapeDtypeStruct((bl * lp, nm * QK_PAD), BF16),
                   jax.ShapeDtypeStruct((bl * lp, nm * QK_PAD), BF16),
                   jax.ShapeDtypeStruct((bl * lp, nm * HEAD), BF16)],
        scratch_shapes=[pltpu.VMEM((hp, nq, QK_PAD, tq), F32), pltpu.VMEM((hp, tk, QK_PAD), F32),
                        pltpu.VMEM((hp, tk, HEAD), F32)],
        compiler_params=pltpu.CompilerParams(dimension_semantics=("arbitrary", "arbitrary", "arbitrary")),
    )(q_cat, k_cat, k_t, v, o, do, lse)


def _place():
    return lax.axis_index("x"), lax.axis_index("y"), lax.axis_index("c")


def gather_shards(packed):
    hbm = pl.BlockSpec(memory_space=pl.ANY)

    def body(src_ref, out_ref, send_sems, recv_sems, local_sem):
        x, y, c = _place()
        me = 2 * x + y
        chips = [(1 - x, y), (x, 1 - y), (1 - x, 1 - y)]
        local = pltpu.make_async_copy(src_ref, out_ref.at[me], local_sem)
        local.start()
        sends = []
        for k, (px, py) in enumerate(chips):
            cp = pltpu.make_async_remote_copy(src_ref=src_ref, dst_ref=out_ref.at[me], send_sem=send_sems.at[k],
                                              recv_sem=recv_sems.at[k], device_id=(px, py, c), device_id_type=MESH)
            cp.start()
            sends.append(cp)
        for k, (px, py) in enumerate(chips):
            pltpu.make_async_remote_copy(src_ref=src_ref, dst_ref=out_ref.at[2 * px + py], send_sem=send_sems.at[k],
                                         recv_sem=recv_sems.at[k], device_id=(px, py, c),
                                         device_id_type=MESH).wait_recv()
        for cp in sends:
            cp.wait_send()
        local.wait()

    return pl.pallas_call(
        body, name="gather_shards", in_specs=[hbm], out_specs=hbm,
        out_shape=jax.ShapeDtypeStruct((4,) + packed.shape, packed.dtype),
        scratch_shapes=[pltpu.SemaphoreType.DMA((3,)), pltpu.SemaphoreType.DMA((3,)), pltpu.SemaphoreType.DMA],
    )(packed)


def gather_small(small):
    hbm = pl.BlockSpec(memory_space=pl.ANY)

    def body(small_ref, all_ref, send_sems, recv_sems, local_sem):
        x, y, c = _place()
        me = 4 * x + 2 * y + c
        local = pltpu.make_async_copy(small_ref, all_ref.at[me], local_sem)
        local.start()
        others = [(x ^ ((r >> 2) & 1), y ^ ((r >> 1) & 1), c ^ (r & 1)) for r in range(1, 8)]
        sends = []
        for r, peer in enumerate(others):
            cp = pltpu.make_async_remote_copy(src_ref=small_ref, dst_ref=all_ref.at[me], send_sem=send_sems.at[r],
                                              recv_sem=recv_sems.at[r], device_id=peer, device_id_type=MESH)
            cp.start()
            sends.append(cp)
        for r, (px, py, pc) in enumerate(others):
            pltpu.make_async_remote_copy(src_ref=small_ref, dst_ref=all_ref.at[4 * px + 2 * py + pc],
                                         send_sem=send_sems.at[r], recv_sem=recv_sems.at[r],
                                         device_id=(px, py, pc), device_id_type=MESH).wait_recv()
        for cp in sends:
            cp.wait_send()
        local.wait()

    return pl.pallas_call(
        body, name="gather_small", in_specs=[hbm], out_specs=hbm,
        out_shape=jax.ShapeDtypeStruct((8,) + small.shape, small.dtype),
        scratch_shapes=[pltpu.SemaphoreType.DMA((7,)), pltpu.SemaphoreType.DMA((7,)), pltpu.SemaphoreType.DMA],
    )(small)


def swap_with_sibling(name, parts):
    n = len(parts)
    hbm = pl.BlockSpec(memory_space=pl.ANY)

    def body(*refs):
        x, y, c = _place()
        cps = [pltpu.make_async_remote_copy(src_ref=refs[j], dst_ref=refs[n + j], send_sem=refs[2 * n].at[j],
                                            recv_sem=refs[2 * n + 1].at[j], device_id=(x, y, 1 - c),
                                            device_id_type=MESH) for j in range(n)]
        for cp in cps:
            cp.start()
        for cp in cps:
            cp.wait()

    return pl.pallas_call(
        body, name=name, in_specs=[hbm] * n, out_specs=[hbm] * n,
        out_shape=[jax.ShapeDtypeStruct(p.shape, p.dtype) for p in parts],
        scratch_shapes=[pltpu.SemaphoreType.DMA((n,)), pltpu.SemaphoreType.DMA((n,))],
    )(*parts)


def _chips3():
    x, y, c = _place()
    return [(1 - x, y, c), (x, 1 - y, c), (1 - x, 1 - y, c)]


def _push_copies(src_refs, land_refs, send_sems, recv_sems, per_chip):
    x, y, _ = _place()
    cps = []
    for j, (src_ref, land_ref) in enumerate(zip(src_refs, land_refs)):
        for k, (px, py, pc) in enumerate(_chips3()):
            part = src_ref.at[2 * px + py] if per_chip else src_ref
            slot = k if per_chip else 2 * x + y
            cps.append(pltpu.make_async_remote_copy(
                src_ref=part, dst_ref=land_ref.at[slot], send_sem=send_sems.at[3 * j + k],
                recv_sem=recv_sems.at[3 * j + k], device_id=(px, py, pc), device_id_type=MESH))
    return cps


def push_start(name, srcs, per_chip):
    n = len(srcs)
    hbm = pl.BlockSpec(memory_space=pltpu.HBM)
    sem = pl.BlockSpec(memory_space=pltpu.SEMAPHORE)
    lands = [lax.empty((3 if per_chip else 4,) + s.shape[-2:], s.dtype) for s in srcs]

    def body(*refs):
        src_refs, land_refs = refs[:n], refs[n:2 * n]
        send_sems, recv_sems = refs[2 * n], refs[2 * n + 1]
        for cp in _push_copies(src_refs, land_refs, send_sems, recv_sems, per_chip):
            cp.start()
        refs[-1][...] = jnp.zeros_like(refs[-1])

    outs = pl.pallas_call(
        body, name=name,
        out_shape=(pltpu.SemaphoreType.DMA((3 * n,)), pltpu.SemaphoreType.DMA((3 * n,)),
                   *[pltpu.HBM(a.shape, a.dtype) for a in list(srcs) + lands], jax.ShapeDtypeStruct((8, HEAD), F32)),
        in_specs=(hbm,) * (2 * n),
        out_specs=(sem, sem) + (hbm,) * (2 * n) + (pl.BlockSpec(memory_space=pltpu.VMEM),),
        input_output_aliases={j: 2 + j for j in range(2 * n)},
        compiler_params=pltpu.CompilerParams(has_side_effects=pltpu.SideEffectType.DATAFLOW_SIDE_EFFECTING),
    )(*[pltpu.with_memory_space_constraint(a, pltpu.HBM) for a in list(srcs) + lands])
    return tuple(outs[:-1]), outs[-1]


def push_wait(name, handle, after, per_chip):
    send_sems, recv_sems = handle[0], handle[1]
    thru = handle[2:]
    n = len(thru) // 2
    hbm = pl.BlockSpec(memory_space=pltpu.HBM)
    sem = pl.BlockSpec(memory_space=pltpu.SEMAPHORE)

    def body(*refs):
        src_refs, land_refs = refs[:n], refs[n:2 * n]
        for cp in _push_copies(src_refs, land_refs, refs[2 * n], refs[2 * n + 1], per_chip):
            cp.wait_send()
            cp.wait_recv()

    outs = pl.pallas_call(
        body, name=name,
        out_shape=tuple(pltpu.HBM(a.shape, a.dtype) for a in thru),
        in_specs=(hbm,) * (2 * n) + (sem, sem, pl.BlockSpec(memory_space=pl.ANY)), out_specs=(hbm,) * (2 * n),
        input_output_aliases={j: j for j in range(2 * n)},
        compiler_params=pltpu.CompilerParams(has_side_effects=pltpu.SideEffectType.DATAFLOW_SIDE_EFFECTING),
    )(*thru, send_sems, recv_sems, after)
    return outs[:n], outs[n:]


def join_gathered(name, own, landed, my_chip):
    blocks = lax.dynamic_update_index_in_dim(landed, own, my_chip, 0)
    _, r, c = blocks.shape
    if name in COL_SHARDED:
        return blocks.transpose(1, 0, 2).reshape(r, 4 * c)
    return blocks.reshape(4 * r, c)


def adamw(name, w, g_parts, m, v):
    r, c = w.shape
    tr = r if r * c <= 65536 else _tile(r, 128, 8)
    ng = len(g_parts)

    def body(*refs):
        w_ref, m_ref, v_ref = refs[0], refs[1 + ng], refs[2 + ng]
        g_ref, d_ref, nm_ref, nv_ref = refs[3 + ng:]
        gv = refs[1][...]
        for k in range(1, ng):
            gv = gv + refs[1 + k][...]
        m_new = ADAM_B1 * m_ref[...] + (1.0 - ADAM_B1) * gv
        v_new = ADAM_B2 * v_ref[...] + (1.0 - ADAM_B2) * (gv * gv)
        m_hat = m_new / (1.0 - ADAM_B1 ** ADAM_STEP)
        v_hat = v_new / (1.0 - ADAM_B2 ** ADAM_STEP)
        g_ref[...] = gv
        d_ref[...] = -ADAM_LR * (m_hat / (jnp.sqrt(v_hat) + ADAM_EPS) + ADAM_WD * w_ref[...])
        nm_ref[...] = m_new
        nv_ref[...] = v_new

    spec = pl.BlockSpec((tr, c), lambda i: (i, 0))
    return pl.pallas_call(
        body, name=name, grid=(r // tr,), in_specs=[spec] * (3 + ng), out_specs=[spec] * 4,
        out_shape=[jax.ShapeDtypeStruct((r, c), F32)] * 4,
        compiler_params=pltpu.CompilerParams(dimension_semantics=("arbitrary",)),
    )(w, *g_parts, m, v)


def split_full(name, full, s):
    if name in COL_SHARDED:
        c = full.shape[1] // 4
        return full[:, s * c:(s + 1) * c]
    r = full.shape[0] // 4
    return full[s * r:(s + 1) * r]


def kernel(x, meta_tokens, w_in, b_gate, lb_logits, hg_norm_g, w_hg_o, q_a_norm_g, w_q_b, kv_a_norm_g, w_kv_b, w_mla_o, w_out, mix_pre_g, mix_post_g, ffn_pre_g, ffn_post_g, w_ffn_in, w_ffn_out, loss_target, m_meta_tokens, m_w_in, m_b_gate, m_lb_logits, m_hg_norm_g, m_w_hg_o, m_q_a_norm_g, m_w_q_b, m_kv_a_norm_g, m_w_kv_b, m_w_mla_o, m_w_out, m_mix_pre_g, m_mix_post_g, m_ffn_pre_g, m_ffn_post_g, m_w_ffn_in, m_w_ffn_out, v_meta_tokens, v_w_in, v_b_gate, v_lb_logits, v_hg_norm_g, v_w_hg_o, v_q_a_norm_g, v_w_q_b, v_kv_a_norm_g, v_w_kv_b, v_w_mla_o, v_w_out, v_mix_pre_g, v_mix_post_g, v_ffn_pre_g, v_ffn_post_g, v_w_ffn_in, v_w_ffn_out):
    wts = dict(meta_tokens=meta_tokens, w_in=w_in[0], b_gate=b_gate, lb_logits=lb_logits, hg_norm_g=hg_norm_g,
               w_hg_o=w_hg_o[0], q_a_norm_g=q_a_norm_g, w_q_b=w_q_b[0], kv_a_norm_g=kv_a_norm_g, w_kv_b=w_kv_b[0],
               w_mla_o=w_mla_o[0], w_out=w_out[0], mix_pre_g=mix_pre_g, mix_post_g=mix_post_g, ffn_pre_g=ffn_pre_g,
               ffn_post_g=ffn_post_g, w_ffn_in=w_ffn_in[0], w_ffn_out=w_ffn_out[0])
    mom_m = dict(meta_tokens=m_meta_tokens, w_in=m_w_in[0], b_gate=m_b_gate, lb_logits=m_lb_logits,
                 hg_norm_g=m_hg_norm_g, w_hg_o=m_w_hg_o[0], q_a_norm_g=m_q_a_norm_g, w_q_b=m_w_q_b[0],
                 kv_a_norm_g=m_kv_a_norm_g, w_kv_b=m_w_kv_b[0], w_mla_o=m_w_mla_o[0], w_out=m_w_out[0],
                 mix_pre_g=m_mix_pre_g, mix_post_g=m_mix_post_g, ffn_pre_g=m_ffn_pre_g, ffn_post_g=m_ffn_post_g,
                 w_ffn_in=m_w_ffn_in[0], w_ffn_out=m_w_ffn_out[0])
    mom_v = dict(meta_tokens=v_meta_tokens, w_in=v_w_in[0], b_gate=v_b_gate, lb_logits=v_lb_logits,
                 hg_norm_g=v_hg_norm_g, w_hg_o=v_w_hg_o[0], q_a_norm_g=v_q_a_norm_g, w_q_b=v_w_q_b[0],
                 kv_a_norm_g=v_kv_a_norm_g, w_kv_b=v_w_kv_b[0], w_mla_o=v_w_mla_o[0], w_out=v_w_out[0],
                 mix_pre_g=v_mix_pre_g, mix_post_g=v_mix_post_g, ffn_pre_g=v_ffn_pre_g, ffn_post_g=v_ffn_post_g,
                 w_ffn_in=v_w_ffn_in[0], w_ffn_out=v_w_ffn_out[0])

    bl, seq, d = x.shape
    lp = PAD_FRONT + N_META + seq
    t_rows = bl * lp
    nh = d // HEAD
    ql, kvl = wts["w_q_b"].shape[0], wts["w_kv_b"].shape[0]
    nm = (4 * wts["w_mla_o"].shape[0]) // HEAD
    ffn = 4 * wts["w_ffn_out"].shape[0]
    mla_w = ql + kvl + HEAD
    assert ql == kvl and ql % HEAD == 0 and seq % SEQ_BLOCK == 0 and d % HEAD == 0
    scale = (HEAD + ROPE) ** -0.5
    my_chip = 2 * lax.axis_index("x") + lax.axis_index("y")

    mcols = meta_tokens.shape[1]
    meta_all = gather_shards(meta_tokens)
    meta_full = jnp.concatenate([meta_all[s] for s in range(4)], axis=1)

    def start_gather(name, names, order_after):
        srcs = [_bf(wts[n]) for n in names]
        if order_after is not None:
            srcs[0] = srcs[0] + order_after[0, 0].astype(BF16)
        return push_start(name, srcs, per_chip=False)

    def finish_gather(name, names, started, after):
        owns, landed = push_wait(name, started[0], after, per_chip=False)
        return {n: join_gathered(n, own, land, my_chip) for n, own, land in zip(names, owns, landed)}

    rest_names = tuple(n for n in BIG if n != "w_in")
    my_c = lax.axis_index("c")
    w_in_bf = _bf(wts["w_in"])
    half = w_in_bf.shape[0] // 2
    own_half = (lax.dynamic_slice_in_dim(w_in_bf, my_c * half, half, axis=0)
                + (meta_all[0, :1, :1] * 0.0)[0, 0].astype(BF16))
    gather_1 = push_start("gather_w_in_start", [own_half], per_chip=False)
    gather_2 = start_gather("gather_rest_start", rest_names, gather_1[1])

    h0 = jnp.concatenate([jnp.zeros((bl, PAD_FRONT, d), F32), jnp.broadcast_to(meta_full[None], (bl, N_META, d)), x],
                         axis=1).reshape(t_rows, d)
    tiles_seq, tiles_real = lp // SEQ_BLOCK, seq // SEQ_BLOCK
    assert PAD_FRONT + N_META == SEQ_BLOCK

    def real_block(i):
        return (i // tiles_seq) * tiles_real + jnp.maximum(i % tiles_seq - 1, 0)

    meta_rows = jnp.broadcast_to(((jnp.arange(lp) >= PAD_FRONT) & (jnp.arange(lp) < PAD_FRONT + N_META)
                                  ).astype(F32)[:, None], (lp, HEAD))
    pos = (jnp.arange(lp, dtype=jnp.int32) - PAD_FRONT).astype(F32)
    inv_freq = 1.0 / (ROPE_THETA ** (jnp.arange(0, ROPE, 2, dtype=F32) / ROPE))
    ang = pos[:, None] * inv_freq[None, :]
    zeros32 = jnp.zeros((lp, ROPE_HALF), F32)
    zeros64 = jnp.zeros((lp, HEAD - ROPE), F32)
    t_cos = jnp.concatenate([jnp.cos(ang), jnp.cos(ang), zeros64], axis=1)
    t_up = jnp.concatenate([zeros32, jnp.sin(ang), zeros64], axis=1)
    t_dn = jnp.concatenate([-jnp.sin(ang), zeros32, zeros64], axis=1)
    real = jnp.broadcast_to((jnp.arange(lp) >= PAD_FRONT + N_META).astype(F32)[:, None], (lp, d))
    lb_soft = jax.nn.softmax(lb_logits.astype(F32), axis=0)
    lb = lb_soft[0:1]

    (u1,) = rowwise("norm_mix_pre", lambda h, g: _rms(h, g), [(h0, d, 0)], [], [mix_pre_g + gather_2[1][0, 0]],
                    [(d, BF16)])
    _, (fetched,) = push_wait("gather_w_in_wait", gather_1[0], u1, per_chip=False)
    (handed,) = swap_with_sibling("swap_w_in", [fetched])
    halves = jnp.stack([fetched, handed])
    remote = jnp.concatenate([lax.dynamic_index_in_dim(halves, my_c, 0, keepdims=False),
                              lax.dynamic_index_in_dim(halves, 1 - my_c, 0, keepdims=False)], axis=1)
    full = {"w_in": join_gathered("w_in", w_in_bf, remote, my_chip)}
    w_main = jnp.concatenate([full["w_in"][:, :4 * d], full["w_in"][:, -2 * d:]], axis=1)
    w_mla = jnp.pad(full["w_in"][:, 4 * d:4 * d + ql + kvl + ROPE], ((0, 0), (0, HEAD - ROPE)))
    proj_main = matmul("proj_main", u1, w_main, "nn", out_dtype=BF16)
    proj_mla = matmul("proj_mla", u1, w_mla, "nn", out_dtype=BF16)
    hg_consts = _hg_constants()
    o_scan, states, a_mats = hgrn_fwd(proj_main, lb, hg_consts, bl, lp, d)

    full.update(finish_gather("gather_rest_wait", rest_names, gather_2, o_scan))
    w_qb = jnp.pad(full["w_q_b"].reshape(ql, nm, HEAD + ROPE), ((0, 0), (0, 0), (0, QK_PAD - HEAD - ROPE))
                   ).reshape(ql, nm * QK_PAD)
    w_kvb = full["w_kv_b"]

    def hg_out_fn(o, hg, g):
        ov = jnp.concatenate([_rms(o[:, h * HEAD:(h + 1) * HEAD], g) for h in range(nh)], axis=1) * _silu(hg)
        return ov, ov

    y_a, o_hg = matmul_fused("hgrn_out_y_a", hg_out_fn, [(o_scan, d, 0), (proj_main, d, 3)], [hg_norm_g],
                             _bf(full["w_hg_o"]), [(0, d)], "nn", [(d, BF16)], tm=512)

    def norm_fn(c, g):
        cn = _rms(c, g)
        return cn, cn

    q_full, qn = matmul_fused("q_norm_up", norm_fn, [(proj_mla, ql, 0)], [q_a_norm_g], w_qb, [(0, ql)], "nn",
                              [(ql, BF16)], tm=512)
    kv_full, kvn = matmul_fused("kv_norm_up", norm_fn, [(proj_mla, kvl, 1)], [kv_a_norm_g], w_kvb, [(0, kvl)], "nn",
                                [(kvl, BF16)], tm=512)

    def mla_prep_fn(qf, kvf, kpe, cos, s_up, s_dn):
        qf = qf * scale
        kpe_r = _rope(kpe, cos, s_up, s_dn)
        qs, ks, vs = [], [], []
        for h in range(nm):
            qs += [qf[:, h * QK_PAD:h * QK_PAD + HEAD], _rope(qf[:, h * QK_PAD + HEAD:(h + 1) * QK_PAD], cos, s_up, s_dn)]
            ks += [kvf[:, h * QK_PAD:h * QK_PAD + HEAD], kpe_r]
            vs += [kvf[:, h * QK_PAD + HEAD:(h + 1) * QK_PAD]]
        return jnp.concatenate(qs, axis=1), jnp.concatenate(ks, axis=1), jnp.concatenate(vs, axis=1)

    kpe_blk = (ql + kvl) // HEAD
    q_cat, k_cat, v_att = rowwise("mla_prep", mla_prep_fn,
                                  [(q_full, nm * QK_PAD, 0), (kv_full, nm * QK_PAD, 0), (proj_mla, HEAD, kpe_blk)],
                                  [t_cos, t_up, t_dn], [], [(nm * QK_PAD, BF16), (nm * QK_PAD, BF16), (nm * HEAD, BF16)])
    at = _attn_tile(lp)
    v_t = v_att.reshape(bl, lp // at, at, nm, HEAD).transpose(0, 3, 1, 4, 2)
    k_t = k_cat.reshape(bl, lp // at, at, nm, QK_PAD).transpose(0, 3, 1, 4, 2)
    o_mla, lse = attn_fwd_t(q_cat, k_cat, v_t, bl, lp, nm)
    y_b = matmul("y_b", o_mla, _bf(full["w_mla_o"]), "nn", out_dtype=BF16)

    def gate_fn(ya, yb, ga, gb, bias):
        zv = _sigmoid(ga + bias[:, :d]) * ya + _sigmoid(gb + bias[:, d:]) * yb
        return zv, zv

    mixed, z = matmul_fused("gate_mix_out", gate_fn,
                            [(y_a, d, 0), (y_b, d, 0), (proj_main, d, 4), (proj_main, d, 5)], [b_gate],
                            _bf(full["w_out"]), [(0, d)], "nn", [(d, BF16)], tm=512)

    def mid_fn(h, mx, g_post, g_pre):
        h1v = h + _rms(mx, g_post)
        u2v = _rms(h1v, g_pre)
        return u2v, h1v, u2v

    gu, h1, u2 = matmul_fused("norm_mid_ffn_in", mid_fn, [(h0, d, 0), (mixed, d, 0)], [mix_post_g, ffn_pre_g],
                              _bf(full["w_ffn_in"]), [(0, d)], "nn", [(d, F32), (d, BF16)])
    def swiglu_fn(gt, up):
        a = _silu(gt) * up
        return a, a

    f_out, act = matmul_fused("swiglu_ffn_out", swiglu_fn, [(gu, ffn, 0), (gu, ffn, 1)], [],
                              _bf(full["w_ffn_out"]), [(0, ffn)], "nn", [(ffn, BF16)])

    def loss_fn(h1v, fv, tg, realv, g_post):
        h2 = h1v + _rms(fv, g_post)
        diff = (h2 - tg) * realv
        part = jnp.broadcast_to(0.5 * jnp.sum(diff * diff, keepdims=True) / d, (1, HEAD))
        dy = diff / d
        df, dg = _rms_bwd(fv, g_post, dy)
        return dy, df, part, dg

    dy, df, loss_part, g_ffn_post = rowwise(
        "loss_head", loss_fn,
        [(h1, d, 0), (f_out, d, 0), (loss_target.reshape(bl * seq, d), d, 0, real_block)], [real],
        [ffn_post_g], [(d, BF16), (d, BF16)], [(1, HEAD), (1, d)])
    grads = {}
    d_act = matmul("d_act", df, _bf(full["w_ffn_out"]), "nt", out_dtype=BF16)
    grads["w_ffn_out"] = matmul("gw_ffn_out", act, df, "tn")

    def swiglu_bwd_fn(gt, up, da):
        dgt, dup = da * up * _silu_grad(gt), da * _silu(gt)
        return dgt, dup, jnp.concatenate([dgt, dup], axis=1)

    du2, dgu = matmul_fused("swiglu_bwd_d_u2", swiglu_bwd_fn, [(gu, ffn, 0), (gu, ffn, 1), (d_act, ffn, 0)], [],
                            _bf(full["w_ffn_in"]), [(0, ffn), (ffn, 2 * ffn)], "nt", [(2 * ffn, BF16)])
    grads["w_ffn_in"] = matmul("gw_ffn_in", u2, dgu, "tn")

    def mid_bwd_fn(dyv, h1v, du2v, mx, g_pre, g_post):
        dx, dg_pre = _rms_bwd(h1v, g_pre, du2v)
        dh1 = dyv + dx
        dmx, dg_post = _rms_bwd(mx, g_post, dh1)
        return dh1, dmx, dg_pre, dg_post

    dh1, dmixed, g_ffn_pre, g_mix_post = rowwise("norm_mid_bwd", mid_bwd_fn,
                                                 [(dy, d, 0), (h1, d, 0), (du2, d, 0), (mixed, d, 0)], [],
                                                 [ffn_pre_g, mix_post_g], [(d, BF16), (d, BF16)], [(1, d), (1, d)])
    dz = matmul("d_z", dmixed, _bf(full["w_out"]), "nt", out_dtype=BF16)
    grads["w_out"] = matmul("gw_out", z, dmixed, "tn")

    def gate_bwd_fn(dzv, ya, yb, ga, gb, bias):
        sa, sb = _sigmoid(ga + bias[:, :d]), _sigmoid(gb + bias[:, d:])
        dga = dzv * ya * sa * (1.0 - sa)
        dgb = dzv * yb * sb * (1.0 - sb)
        dgates = jnp.concatenate([dga, dgb], axis=1)
        return dzv * sa, dzv * sb, dgates, jnp.sum(dgates, axis=0, keepdims=True)

    dy_a, dy_b, dgates, g_b_gate = rowwise("gate_mix_bwd", gate_bwd_fn,
                                           [(dz, d, 0), (y_a, d, 0), (y_b, d, 0), (proj_main, d, 4), (proj_main, d, 5)],
                                           [], [b_gate], [(d, BF16), (d, BF16), (2 * d, BF16)], [(1, 2 * d)])
    do_hg = matmul("d_o_hg", dy_a, _bf(full["w_hg_o"]), "nt", out_dtype=BF16)
    grads["w_hg_o"] = matmul("gw_hg_o", o_hg, dy_a, "tn")
    do_mla = matmul("d_o_mla", dy_b, _bf(full["w_mla_o"]), "nt", out_dtype=BF16)
    grads["w_mla_o"] = matmul("gw_mla_o", o_mla, dy_b, "tn")

    early = ("w_hg_o", "w_mla_o", "w_out", "w_ffn_in", "w_ffn_out")
    late = ("w_in", "w_q_b", "w_kv_b")

    def start_grads(name, names):
        sends = [_bf(jnp.stack([split_full(n, grads[n], s) for s in range(4)])) for n in names]
        mines = []
        for n in names:
            r, c = wts[n].shape
            axis, size = (1, c) if n in COL_SHARDED else (0, r)
            mines.append(lax.dynamic_slice_in_dim(grads[n], my_chip * size, size, axis=axis))
        handle, token = push_start(name, sends, per_chip=True)
        return handle, token, mines

    def finish_grads(tag, names, started, after):
        handle, _, mines = started
        _, landed = push_wait(f"grads_{tag}_wait", handle, after, per_chip=True)
        parts = []
        for n, mine, land in zip(names, mines, landed):
            r, c = mine.shape
            tr = _tile(r, 256, 16)
            land2 = land.reshape(3 * r, c)
            parts.append(rowwise(f"sum_chips_{n}", lambda a, r0, r1, r2: a + r0 + r1 + r2,
                                 [(mine, c, 0)] + [(land2, c, 0, k * (r // tr)) for k in range(3)],
                                 [], [], [(c, F32)], tm=tr)[0])
        sibs = swap_with_sibling(f"swap_{tag}", parts)
        return {n: [p, s] for n, p, s in zip(names, parts, sibs)}

    grads_early = start_grads("grads_early_start", early)
    token_a = grads_early[1]

    def hg_out_bwd_fn(do, o, hg, g):
        sg = _silu(hg)
        dn = do * sg
        dos, dgs, ons = [], 0.0, []
        for h in range(nh):
            sl = slice(h * HEAD, (h + 1) * HEAD)
            dx, dg = _rms_bwd(o[:, sl], g, dn[:, sl])
            dos.append(dx)
            dgs = dgs + dg
            ons.append(_rms(o[:, sl], g))
        dhg = do * jnp.concatenate(ons, axis=1) * _silu_grad(hg)
        return jnp.concatenate(dos, axis=1), dhg, dgs

    do_scan, dhg, g_hg_norm = rowwise("hgrn_out_bwd", hg_out_bwd_fn, [(do_hg, d, 0), (o_scan, d, 0), (proj_main, d, 3)],
                                      [], [hg_norm_g], [(d, BF16), (d, BF16)], [(1, HEAD)])
    dhq, dhf, dhi, g_lb = hgrn_bwd(proj_main, lb + token_a[0, 0], hg_consts, states, a_mats, do_scan, bl, lp, d)

    dq_cat, dk_cat, dv_att = attn_bwd_t(q_cat, k_cat, k_t, v_att, o_mla, do_mla, lse, bl, lp, nm)

    def mla_prep_bwd_fn(dqc, dkc, dvv, cos, s_up, s_dn):
        dqc = dqc * scale
        dqs, dkvs, dkpe = [], [], 0.0
        for h in range(nm):
            dqs += [dqc[:, h * QK_PAD:h * QK_PAD + HEAD],
                    _rope_bwd(dqc[:, h * QK_PAD + HEAD:(h + 1) * QK_PAD], cos, s_up, s_dn)]
            dkvs += [dkc[:, h * QK_PAD:h * QK_PAD + HEAD], dvv[:, h * HEAD:(h + 1) * HEAD]]
            dkpe = dkpe + dkc[:, h * QK_PAD + HEAD:(h + 1) * QK_PAD]
        return jnp.concatenate(dqs, axis=1), jnp.concatenate(dkvs, axis=1), _rope_bwd(dkpe, cos, s_up, s_dn)

    dq_full, dkv_full, dkpe = rowwise("mla_prep_bwd", mla_prep_bwd_fn,
                                      [(dq_cat, nm * QK_PAD, 0), (dk_cat, nm * QK_PAD, 0), (dv_att, nm * HEAD, 0)],
                                      [t_cos, t_up, t_dn], [],
                                      [(nm * QK_PAD, BF16), (nm * QK_PAD, BF16), (HEAD, F32)])
    dqn = matmul("d_qn", dq_full, w_qb, "nt", out_dtype=BF16)
    g_wqb = matmul("gw_q_b", qn, dq_full, "tn")
    grads["w_q_b"] = g_wqb.reshape(ql, nm, QK_PAD)[:, :, :HEAD + ROPE].reshape(ql, nm * (HEAD + ROPE))
    dkvn = matmul("d_kvn", dkv_full, w_kvb, "nt", out_dtype=BF16)
    grads["w_kv_b"] = matmul("gw_kv_b", kvn, dkv_full, "tn")

    def mla_norms_bwd_fn(dqnv, dkvnv, cq, ckv, dkpev, gq, gk):
        dcq, dgq = _rms_bwd(cq, gq, dqnv)
        dckv, dgk = _rms_bwd(ckv, gk, dkvnv)
        return jnp.concatenate([dcq, dckv, dkpev], axis=1), dgq, dgk

    dmla, g_q_norm, g_kv_norm = rowwise("mla_norms_bwd", mla_norms_bwd_fn,
                                        [(dqn, ql, 0), (dkvn, kvl, 0), (proj_mla, ql, 0), (proj_mla, kvl, 1),
                                         (dkpe, HEAD, 0)], [], [q_a_norm_g, kv_a_norm_g],
                                        [(mla_w, BF16)], [(1, ql), (1, kvl)])

    d_pieces = [dhq, dhf, dhi, dhg, dgates, dmla]
    gw_parts = [matmul(f"gw_in_{k}", u1, dp, "tn") for k, dp in enumerate(d_pieces)]
    grads["w_in"] = jnp.concatenate(gw_parts[:4] + [gw_parts[5][:, :ql + kvl + ROPE], gw_parts[4]], axis=1)
    grads_late = start_grads("grads_late_start", late)
    w_mla_after = w_mla + grads_late[1][0, 0].astype(BF16)
    w_pieces = [w_main[:, 0:d], w_main[:, d:2 * d], w_main[:, 2 * d:3 * d], w_main[:, 3 * d:4 * d],
                w_main[:, 4 * d:6 * d], w_mla_after]
    du1 = matmul("d_u1", d_pieces, w_pieces, "nt", out_dtype=BF16)

    def first_bwd_fn(dh1v, h, du1v, is_meta, g):
        dx, dg = _rms_bwd(h, g, du1v)
        dh0v = dh1v + dx
        return dh0v, dg, dh0v * jnp.tile(is_meta, (1, d // HEAD))

    grad_x, g_mix_pre, meta_tile = rowwise(
        "norm_mix_pre_bwd", first_bwd_fn, [(dh1, d, 0), (h0, d, 0), (du1, d, 0)], [meta_rows], [mix_pre_g],
        [(d, F32, bl * seq, real_block)], [(1, d), (SEQ_BLOCK, d)])
    grad_x = grad_x.reshape(bl, seq, d)

    g_parts = finish_grads("early", early, grads_early, g_mix_pre)
    updates = {}

    def update(n, parts):
        w2 = wts[n].reshape(-1, wts[n].shape[-1])
        updates[n] = adamw("adamw_" + n, w2, [p.reshape(w2.shape) for p in parts], mom_m[n].reshape(w2.shape),
                           mom_v[n].reshape(w2.shape))

    for n in early:
        update(n, g_parts[n])
    g_parts = finish_grads("late", late, grads_late, updates[early[-1]][0])
    for n in late:
        update(n, g_parts[n])
    p0 = lb_soft[0:1]
    g_lb_logits = jnp.concatenate([g_lb * p0 * (1.0 - p0), -g_lb * p0 * (1.0 - p0)], axis=0)

    def row_of(vec):
        return vec.reshape(-1, d) if vec.size >= d else jnp.pad(vec.reshape(1, -1), ((0, 0), (0, d - vec.size)))

    small_parts = dict(b_gate=g_b_gate, lb_logits=g_lb_logits, hg_norm_g=g_hg_norm, q_a_norm_g=g_q_norm,
                       kv_a_norm_g=g_kv_norm, mix_pre_g=g_mix_pre, mix_post_g=g_mix_post, ffn_pre_g=g_ffn_pre,
                       ffn_post_g=g_ffn_post)
    g_meta = meta_tile[PAD_FRONT:PAD_FRONT + N_META]
    small_rows = [row_of(small_parts[n]) for n in SMALL] + [row_of(g_meta)]
    n_small = sum(r.shape[0] for r in small_rows)
    small = jnp.pad(jnp.concatenate(small_rows, axis=0), ((0, -(-n_small // 8) * 8 - n_small), (0, 0)))
    all_small = gather_small(small)
    small_t = small.shape[0]

    def sum8_fn(*slabs):
        acc = slabs[0]
        for s in slabs[1:]:
            acc = acc + s
        return acc

    (g_small,) = rowwise("sum_small", sum8_fn, [(all_small.reshape(8 * small_t, d), d, 0, k) for k in range(8)],
                         [], [], [(d, F32)], tm=small_t, n_rows=small_t)

    off = 0
    for n, part in zip(SMALL, small_rows[:-1]):
        rows = part.shape[0]
        update(n, [g_small[off:off + rows, :d].reshape(-1)[:wts[n].size]])
        off += rows
    update("meta_tokens", [lax.dynamic_slice_in_dim(g_small[off:off + N_META, :d], my_chip * mcols, mcols, axis=1)])

    loss = lax.psum(loss_part[0, 0], ("x", "y", "c"))

    def shaped(n, a):
        return a.reshape((1,) + wts[n].shape) if n in BIG else a.reshape(wts[n].shape)

    return (loss, grad_x, *[shaped(n, updates[n][k]) for k in range(4) for n in WEIGHTS])
```

```python
import functools
import math

import jax
import jax.numpy as jnp
from jax import lax
from jax.experimental import pallas as pl
from jax.experimental.pallas import tpu as pltpu

F32 = jnp.float32
BF16 = jnp.bfloat16
MESH = pl.DeviceIdType.MESH

N_META = 16
NORM_EPS = 1e-6
HEAD = 128
ROPE = 64
ROPE_HALF = ROPE // 2
QK_PAD = 2 * HEAD
ROPE_THETA = 10000.0
SEQ_BLOCK = 256
PAD_FRONT = SEQ_BLOCK - N_META
NEG = -1e30
VMEM_LIMIT = 56 * 1024 * 1024
ATTN_HEADS_PER_STEP = 1
ATTN_TILE_MAX = 768

ADAM_LR, ADAM_B1, ADAM_B2, ADAM_EPS, ADAM_WD, ADAM_STEP = 0.001, 0.9, 0.999, 1e-08, 0.01, 10

BIG = ("w_in", "w_hg_o", "w_q_b", "w_kv_b", "w_mla_o", "w_out", "w_ffn_in", "w_ffn_out")
COL_SHARDED = ("w_in", "w_q_b", "w_kv_b", "w_ffn_in")
SMALL = ("b_gate", "lb_logits", "hg_norm_g", "q_a_norm_g", "kv_a_norm_g", "mix_pre_g", "mix_post_g",
         "ffn_pre_g", "ffn_post_g")
WEIGHTS = ("meta_tokens", "w_in", "b_gate", "lb_logits", "hg_norm_g", "w_hg_o", "q_a_norm_g", "w_q_b",
           "kv_a_norm_g", "w_kv_b", "w_mla_o", "w_out", "mix_pre_g", "mix_post_g", "ffn_pre_g", "ffn_post_g",
           "w_ffn_in", "w_ffn_out")


def _tile(n, cap, unit=128):
    if n <= cap:
        return n
    best = None
    for t in range(unit, cap + 1, unit):
        if n % t == 0:
            best = t
    assert best is not None, (n, cap, unit)
    return best


def _sigmoid(x):
    return 1.0 / (1.0 + jnp.exp(-x))


def _bf(x):
    return x.astype(BF16)


def rowwise(name, fn, row_ins, seq_tabs, consts, row_outs, acc_outs=(), tm=SEQ_BLOCK, n_rows=None):
    t_rows = row_ins[0][0].shape[0] if n_rows is None else n_rows
    nt = t_rows // tm
    assert t_rows % tm == 0
    n_in = len(row_ins) + len(seq_tabs) + len(consts)
    n_row = len(row_outs)

    def body(*refs):
        vals = [r[...].astype(F32) for r in refs[:n_in]]
        res = fn(*vals)
        if not isinstance(res, (tuple, list)):
            res = (res,)
        outs = refs[n_in:]
        for k in range(n_row):
            outs[k][...] = res[k].astype(outs[k].dtype)
        if acc_outs:
            @pl.when(pl.program_id(0) == 0)
            def _():
                for k in range(len(acc_outs)):
                    outs[n_row + k][...] = jnp.zeros_like(outs[n_row + k])

            for k in range(len(acc_outs)):
                outs[n_row + k][...] += res[n_row + k]

    row_ins = [tuple(e) + (0,) * (4 - len(e)) for e in row_ins]
    in_specs = [pl.BlockSpec((tm, w), functools.partial(lambda i, j, ro: (ro(i) if callable(ro) else i + ro, j),
                                                        j=j, ro=ro)) for (_, w, j, ro) in row_ins]
    for tab in seq_tabs:
        per = tab.shape[0] // tm
        in_specs.append(pl.BlockSpec((tm, tab.shape[1]), functools.partial(lambda i, per: (i % per, 0), per=per)))
    for c in consts:
        in_specs.append(pl.BlockSpec(c.shape, lambda i: (0, 0)))
    row_outs = [tuple(e) + (t_rows, None)[len(e) - 2:] for e in row_outs]
    out_specs = [pl.BlockSpec((tm, w), functools.partial(lambda i, rm: (i if rm is None else rm(i), 0), rm=rm))
                 for (w, _, _, rm) in row_outs]
    out_specs += [pl.BlockSpec(s, lambda i: (0, 0)) for s in acc_outs]
    out_shape = [jax.ShapeDtypeStruct((rows, w), dt) for (w, dt, rows, _) in row_outs]
    out_shape += [jax.ShapeDtypeStruct(s, F32) for s in acc_outs]
    res = pl.pallas_call(
        body, name=name, grid=(nt,), in_specs=in_specs, out_specs=out_specs, out_shape=out_shape,
        compiler_params=pltpu.CompilerParams(dimension_semantics=("arbitrary",)),
    )(*[e[0] for e in row_ins], *seq_tabs, *consts)
    return res


def matmul(name, a, b, mode, out_dtype=F32):
    if mode != "tn":
        return _matmul_resident(name, a if isinstance(a, (list, tuple)) else [a],
                                b if isinstance(b, (list, tuple)) else [b], mode, out_dtype)
    kdim, m = a.shape
    n = b.shape[1]
    tn = _tile(n, 1536)
    tm, tk = _tile(m, 1408 if tn <= 1024 else 1024), _tile(kdim, 1536)
    nk = kdim // tk

    def body(a_ref, b_ref, o_ref, acc_ref):
        k = pl.program_id(2)

        @pl.when(k == 0)
        def _():
            acc_ref[...] = jnp.zeros_like(acc_ref)

        acc_ref[...] += lax.dot_general(a_ref[...], b_ref[...], TN_DIMS, preferred_element_type=F32)

        @pl.when(k == nk - 1)
        def _():
            o_ref[...] = acc_ref[...].astype(o_ref.dtype)

    return pl.pallas_call(
        body, name=name, grid=(m // tm, n // tn, nk),
        in_specs=[pl.BlockSpec((tk, tm), lambda i, j, k: (k, i)), pl.BlockSpec((tk, tn), lambda i, j, k: (k, j))],
        out_specs=pl.BlockSpec((tm, tn), lambda i, j, k: (i, j)),
        out_shape=jax.ShapeDtypeStruct((m, n), out_dtype),
        scratch_shapes=[pltpu.VMEM((tm, tn), F32)],
        compiler_params=pltpu.CompilerParams(dimension_semantics=("arbitrary", "arbitrary", "arbitrary"),
                                             vmem_limit_bytes=VMEM_LIMIT),
    )(a, b)


def _matmul_resident(name, a_list, b_list, mode, out_dtype):
    m = a_list[0].shape[0]
    n = b_list[0].shape[1] if mode == "nn" else b_list[0].shape[0]
    k_total = sum(a.shape[1] for a in a_list)
    out_bytes = 2 if out_dtype == BF16 else 4
    budget = VMEM_LIMIT - 4 * k_total * n - (6 << 20)
    tm = 1024
    while tm > 128 and 2 * tm * (2 * k_total + out_bytes * n) > budget:
        tm //= 2
    tm = _tile(m, tm)
    cn = _tile(n, 1024)
    npairs = len(a_list)

    def body(*refs):
        a_refs, b_refs, o_ref = refs[:npairs], refs[npairs:2 * npairs], refs[2 * npairs]
        for c in range(n // cn):
            acc = None
            for a_ref, b_ref in zip(a_refs, b_refs):
                if mode == "nn":
                    part = jnp.dot(a_ref[...], b_ref[:, pl.ds(c * cn, cn)], preferred_element_type=F32)
                else:
                    part = lax.dot_general(a_ref[...], b_ref[pl.ds(c * cn, cn), :], NT_DIMS,
                                           preferred_element_type=F32)
                acc = part if acc is None else acc + part
            o_ref[:, pl.ds(c * cn, cn)] = acc.astype(o_ref.dtype)

    in_specs = [pl.BlockSpec((tm, a.shape[1]), lambda i: (i, 0)) for a in a_list]
    in_specs += [pl.BlockSpec(b.shape, lambda i: (0, 0)) for b in b_list]
    return pl.pallas_call(
        body, name=name, grid=(m // tm,), in_specs=in_specs,
        out_specs=pl.BlockSpec((tm, n), lambda i: (i, 0)),
        out_shape=jax.ShapeDtypeStruct((m, n), out_dtype),
        compiler_params=pltpu.CompilerParams(dimension_semantics=("arbitrary",), vmem_limit_bytes=VMEM_LIMIT),
    )(*a_list, *b_list)


def matmul_fused(name, fn, row_ins, consts, weight, pieces, mode, extra_outs, out_dtype=BF16, tm=256, acc_outs=()):
    row_ins = [tuple(e) + (0,) * (4 - len(e)) for e in row_ins]
    t_rows = row_ins[0][0].shape[0]
    tm = _tile(t_rows, tm)
    several = isinstance(weight, (list, tuple))
    weights = list(weight) if several else [weight]
    n_in = len(row_ins) + len(consts)
    n_w = len(weights)
    n_parts = len(pieces)
    n_mm = n_parts if several else 1
    n_row_out = n_mm + len(extra_outs)

    def width(w_arr):
        return w_arr.shape[1] if mode == "nn" else w_arr.shape[0]

    def body(*refs):
        w_hbms = refs[n_in:n_in + n_w]
        outs = refs[n_in + n_w:n_in + n_w + n_row_out + len(acc_outs)]
        w_refs, sems = refs[-n_w - 1:-1], refs[-1]

        @pl.when(pl.program_id(0) == 0)
        def _():
            cps = [pltpu.make_async_copy(w_hbms[k], w_refs[k], sems.at[k]) for k in range(n_w)]
            for cp in cps:
                cp.start()
            for cp in cps:
                cp.wait()
            for k in range(len(acc_outs)):
                outs[n_row_out + k][...] = jnp.zeros_like(outs[n_row_out + k])

        res = fn(*[r[...].astype(F32) for r in refs[:n_in]])
        acc = None
        for p, (a_p, piece) in enumerate(zip(res[:n_parts], pieces)):
            w_ref, (k0, k1) = (w_refs[piece[0]], piece[1:]) if several else (w_refs[0], piece)
            if mode == "nn":
                part = jnp.dot(_bf(a_p), w_ref[pl.ds(k0, k1 - k0), :], preferred_element_type=F32)
            else:
                part = lax.dot_general(_bf(a_p), w_ref[:, pl.ds(k0, k1 - k0)], NT_DIMS, preferred_element_type=F32)
            if several:
                outs[p][...] = part.astype(outs[p].dtype)
            else:
                acc = part if acc is None else acc + part
        if not several:
            outs[0][...] = acc.astype(outs[0].dtype)
        for o_ref, val in zip(outs[n_mm:n_row_out], res[n_parts:]):
            o_ref[...] = val.astype(o_ref.dtype)
        for k in range(len(acc_outs)):
            outs[n_row_out + k][...] += res[n_parts + len(extra_outs) + k]

    in_specs = [pl.BlockSpec((tm, w), functools.partial(lambda i, j, ro: (ro(i) if callable(ro) else i + ro, j),
                                                        j=j, ro=ro)) for (_, w, j, ro) in row_ins]
    in_specs += [pl.BlockSpec(c.shape, lambda i: (0, 0)) for c in consts]
    in_specs += [pl.BlockSpec(memory_space=pl.ANY)] * n_w
    if several:
        widths = [(width(weights[piece[0]]), out_dtype) for piece in pieces] + list(extra_outs)
    else:
        widths = [(width(weights[0]), out_dtype)] + list(extra_outs)
    return pl.pallas_call(
        body, name=name, grid=(t_rows // tm,), in_specs=in_specs,
        out_specs=[pl.BlockSpec((tm, w), lambda i: (i, 0)) for (w, _) in widths]
        + [pl.BlockSpec(s, lambda i: (0, 0)) for s in acc_outs],
        out_shape=[jax.ShapeDtypeStruct((t_rows, w), dt) for (w, dt) in widths]
        + [jax.ShapeDtypeStruct(s, F32) for s in acc_outs],
        scratch_shapes=[pltpu.VMEM(w_arr.shape, w_arr.dtype) for w_arr in weights] + [pltpu.SemaphoreType.DMA((n_w,))],
        compiler_params=pltpu.CompilerParams(dimension_semantics=("arbitrary",), vmem_limit_bytes=VMEM_LIMIT),
    )(*[e[0] for e in row_ins], *consts, *weights)


def _rms(x, g):
    r = lax.rsqrt(jnp.mean(x * x, axis=-1, keepdims=True) + NORM_EPS)
    return x * r * g


def _rms_bwd(x, g, dy):
    r = lax.rsqrt(jnp.mean(x * x, axis=-1, keepdims=True) + NORM_EPS)
    xh = x * r
    dyg = dy * g
    dx = r * (dyg - xh * jnp.mean(dyg * xh, axis=-1, keepdims=True))
    return dx, jnp.sum(dy * xh, axis=0, keepdims=True)


def _silu(x):
    return x * _sigmoid(x)


def _silu_grad(x):
    s = _sigmoid(x)
    return s * (1.0 + x * (1.0 - s))


def _rope(xs, cos, s_up, s_dn):
    return xs * cos + pltpu.roll(xs, ROPE_HALF, 1) * s_up + pltpu.roll(xs, HEAD - ROPE_HALF, 1) * s_dn


def _rope_bwd(dy, cos, s_up, s_dn):
    return dy * cos + pltpu.roll(dy * s_up, HEAD - ROPE_HALF, 1) + pltpu.roll(dy * s_dn, ROPE_HALF, 1)


HG_SUB = 128
HG_LEVELS = 7
HG_E_ROWS = (HG_LEVELS + 1) * HG_SUB
HG_BWD_GROUP = 6
TN_DIMS = (((0,), (0,)), ((), ()))
NT_DIMS = (((1,), (1,)), ((), ()))


def _hg_constants():
    import numpy as np
    n = HG_SUB
    r = np.arange(n)[:, None]
    c = np.arange(n)[None, :]
    cs, ps = [], []
    for lvl in range(HG_LEVELS):
        m = (n // 2) >> lvl
        upper = (r % (2 * m)) >= m
        mid = (r // (2 * m)) * (2 * m) + m - 1
        cs.append(np.where(upper, (c > mid) & (c <= r), (c > r) & (c <= mid)))
        ps.append(((r // (2 * m)) == (c // (2 * m))) & upper & ((c % (2 * m)) < m))
    cs.append(c <= r)
    cs.append(np.ones((8, n), bool))
    cstack = np.concatenate(cs, 0).astype(np.float32)
    pstack = np.concatenate(ps, 0).astype(np.float32)
    pstack_t = np.concatenate([p.T for p in ps], 0).astype(np.float32)
    return (jnp.asarray(cstack, BF16), jnp.asarray(cstack[:HG_E_ROWS].T, BF16), jnp.asarray(pstack, F32),
            jnp.asarray(pstack_t, F32))


def _split_dot(c_bf, x):
    hi = _bf(x)
    lo = _bf(x - hi.astype(F32))
    r2 = jnp.dot(c_bf, jnp.concatenate([hi, lo], axis=1), preferred_element_type=F32)
    return r2[:, :HEAD] + r2[:, HEAD:]


def _hg_gates(hq, hf, lb):
    sq = _sigmoid(hq)
    sg = _sigmoid(hf)
    fg = lb + (1.0 - lb) * sg
    return sq, hq * sq, sg, fg, 1.0 - fg, jnp.log(fg)


def hgrn_fwd(proj_main, lb, consts, bl, lp, d):
    nh = d // HEAD
    rows_blk = _tile(lp, 768, SEQ_BLOCK)
    nb = lp // rows_blk
    spb = rows_blk // HG_SUB
    cstack, _, pstack, _ = consts

    def body(hq_ref, hf_ref, hi_ref, lb_ref, c_ref, p_ref, o_ref, st_ref, a_ref, s_ref):
        j = pl.program_id(2)

        @pl.when(j == 0)
        def _():
            s_ref[...] = jnp.zeros_like(s_ref)

        lbv = lb_ref[...]
        cs = c_ref[...]
        rows = [pl.ds(s * HG_SUB, HG_SUB) for s in range(spb)]
        gates = [_hg_gates(hq_ref[r, :].astype(F32), hf_ref[r, :].astype(F32), lbv) for r in rows]
        qs, ks = [g_[1] for g_ in gates], [g_[4] for g_ in gates]
        vs = [hi_ref[r, :].astype(F32) for r in rows]
        es = [_split_dot(cs, g_[5]) for g_ in gates]
        a_acc = [jnp.zeros((HG_SUB, HG_SUB), F32) for _ in rows]
        for lvl in range(HG_LEVELS):
            for s in range(spb):
                x = jnp.exp(es[s][lvl * HG_SUB:(lvl + 1) * HG_SUB])
                a_acc[s] = a_acc[s] + p_ref[pl.ds(lvl * HG_SUB, HG_SUB), :] * lax.dot_general(
                    _bf(qs[s] * x), _bf(ks[s] * x), NT_DIMS, preferred_element_type=F32)
        o_intra, qbs, kds, e_lasts = [], [], [], []
        for s in range(spb):
            a_bf = _bf(a_acc[s])
            a_ref[0, 0, s] = a_bf
            bc = es[s][HG_LEVELS * HG_SUB:HG_E_ROWS]
            b_last = jnp.tile(es[s][HG_E_ROWS:], (HG_SUB // 8, 1))
            o_intra.append(jnp.dot(a_bf, _bf(vs[s]), preferred_element_type=F32)
                           + jnp.sum(qs[s] * ks[s], axis=1, keepdims=True) * vs[s])
            qbs.append(_bf(qs[s] * jnp.exp(bc)))
            kds.append(_bf(ks[s] * jnp.exp(b_last - bc)))
            e_lasts.append(jnp.exp(b_last))
        st = s_ref[...]
        for s in range(spb):
            st_ref[0, 0, s] = st
            o_ref[rows[s], :] = (o_intra[s] + lax.dot_general(qbs[s], _bf(st), NT_DIMS, preferred_element_type=F32)
                                 ).astype(o_ref.dtype)
            st = st * e_lasts[s] + lax.dot_general(_bf(vs[s]), kds[s], TN_DIMS, preferred_element_type=F32)
        s_ref[...] = st

    def colspec(off):
        return pl.BlockSpec((rows_blk, HEAD), functools.partial(lambda h, b, j, off: (b * nb + j, off + h), off=off))

    whole = lambda arr: pl.BlockSpec(arr.shape, lambda h, b, j: (0, 0))
    return pl.pallas_call(
        body, name="hgrn_fwd", grid=(nh, bl, nb),
        in_specs=[colspec(0), colspec(nh), colspec(2 * nh), pl.BlockSpec((1, HEAD), lambda h, b, j: (0, h)),
                  whole(cstack), whole(pstack)],
        out_specs=[pl.BlockSpec((rows_blk, HEAD), lambda h, b, j: (b * nb + j, h)),
                   pl.BlockSpec((1, 1, spb, HEAD, HEAD), lambda h, b, j: (b, h, j, 0, 0)),
                   pl.BlockSpec((1, 1, spb, HG_SUB, HG_SUB), lambda h, b, j: (b, h, j, 0, 0))],
        out_shape=[jax.ShapeDtypeStruct((bl * lp, d), BF16),
                   jax.ShapeDtypeStruct((bl, nh, lp // HG_SUB, HEAD, HEAD), F32),
                   jax.ShapeDtypeStruct((bl, nh, lp // HG_SUB, HG_SUB, HG_SUB), BF16)],
        scratch_shapes=[pltpu.VMEM((HEAD, HEAD), F32)],
        compiler_params=pltpu.CompilerParams(dimension_semantics=("arbitrary", "arbitrary", "arbitrary")),
    )(proj_main, proj_main, proj_main, lb, cstack, pstack)


def hgrn_bwd(proj_main, lb, consts, states, a_mats, do_scan, bl, lp, d):
    nh = d // HEAD
    rows_blk = _tile(lp, 768, SEQ_BLOCK)
    nb = lp // rows_blk
    spb = rows_blk // HG_SUB
    cstack, cstack_t = consts[0], consts[1]
    pstack, pstack_t = _bf(consts[2]), _bf(consts[3])

    def body(hq_ref, hf_ref, hi_ref, lb_ref, c_ref, ct_ref, p_ref, pt_ref, st_ref, a_ref, do_ref,
             dq_ref, df_ref, di_ref, dlb_ref, ds_ref):
        b_id, j = pl.program_id(1), pl.program_id(2)
        blk = nb - 1 - j

        @pl.when(j == 0)
        def _():
            ds_ref[...] = jnp.zeros_like(ds_ref)

        @pl.when((j == 0) & (b_id == 0))
        def _():
            dlb_ref[...] = jnp.zeros_like(dlb_ref)

        lbv = lb_ref[...]
        cs = c_ref[...]
        cst = ct_ref[...]

        dlb = jnp.zeros((1, HEAD), F32)
        for first in reversed(range(0, spb, HG_BWD_GROUP)):
            dlb = dlb + _hg_group_bwd(list(range(first, min(first + HG_BWD_GROUP, spb))), lbv, cs, cst, hq_ref,
                                      hf_ref, hi_ref, st_ref, a_ref, do_ref, p_ref, pt_ref, dq_ref, df_ref, di_ref,
                                      ds_ref)
        dlb_ref[...] += dlb

    def _hg_group_bwd(ids, lbv, cs, cst, hq_ref, hf_ref, hi_ref, st_ref, a_ref, do_ref, p_ref, pt_ref, dq_ref,
                      df_ref, di_ref, ds_ref):
        rng = range(len(ids))
        rows = [pl.ds(s * HG_SUB, HG_SUB) for s in ids]
        hqs = [hq_ref[r, :].astype(F32) for r in rows]
        gates = [_hg_gates(hqs[s], hf_ref[rows[s], :].astype(F32), lbv) for s in rng]
        sqs, qs, sgs, fgs, ks = ([g_[i] for g_ in gates] for i in range(5))
        vs = [hi_ref[r, :].astype(F32) for r in rows]
        dos = [do_ref[r, :].astype(F32) for r in rows]
        sts = [st_ref[0, 0, s] for s in ids]
        es = [_split_dot(cs, g_[5]) for g_ in gates]
        bcs = [e[HG_LEVELS * HG_SUB:HG_E_ROWS] for e in es]
        b_lasts = [jnp.tile(e[HG_E_ROWS:], (HG_SUB // 8, 1)) for e in es]
        ebs = [jnp.exp(bc) for bc in bcs]
        qbs = [qs[s] * ebs[s] for s in rng]
        ers = [jnp.exp(b_lasts[s] - bcs[s]) for s in rng]
        kds = [ks[s] * ers[s] for s in rng]
        e_lasts = [jnp.exp(b) for b in b_lasts]
        do_bfs, v_bfs = [_bf(x) for x in dos], [_bf(x) for x in vs]
        das = [_bf(lax.dot_general(do_bfs[s], v_bfs[s], NT_DIMS, preferred_element_type=F32)) for s in rng]
        dats = [_bf(lax.dot_general(v_bfs[s], do_bfs[s], NT_DIMS, preferred_element_type=F32)) for s in rng]
        dqbs = [jnp.dot(do_bfs[s], _bf(sts[s]), preferred_element_type=F32) for s in rng]
        m_s = [lax.dot_general(do_bfs[s], _bf(qbs[s]), TN_DIMS, preferred_element_type=F32) for s in rng]
        dst_outs = [None] * len(ids)
        dst = ds_ref[...]
        for s in reversed(rng):
            dst_outs[s] = dst
            dst = dst * e_lasts[s] + m_s[s]
        ds_ref[...] = dst
        dst_bfs = [_bf(x) for x in dst_outs]
        d_diags = [jnp.sum(dos[s] * vs[s], axis=1, keepdims=True) for s in rng]
        dvs = [lax.dot_general(a_ref[0, 0, ids[s]], do_bfs[s], TN_DIMS, preferred_element_type=F32)
               + jnp.sum(qs[s] * ks[s], axis=1, keepdims=True) * dos[s]
               + lax.dot_general(_bf(kds[s]), dst_bfs[s], NT_DIMS, preferred_element_type=F32) for s in rng]
        dkds = [jnp.dot(v_bfs[s], dst_bfs[s], preferred_element_type=F32) for s in rng]
        dqs = [dqbs[s] * ebs[s] + d_diags[s] * ks[s] for s in rng]
        dks = [dkds[s] * ers[s] + d_diags[s] * qs[s] for s in rng]
        d_lasts = [jnp.sum(dst_outs[s] * sts[s] * e_lasts[s], axis=0, keepdims=True)
                   + jnp.sum(dkds[s] * kds[s], axis=0, keepdims=True) for s in rng]
        des = [[] for _ in rng]
        for lvl in range(HG_LEVELS):
            for s in rng:
                x = jnp.exp(es[s][lvl * HG_SUB:(lvl + 1) * HG_SUB])
                qh, kh = qs[s] * x, ks[s] * x
                dm = p_ref[pl.ds(lvl * HG_SUB, HG_SUB), :] * das[s]
                dmt = pt_ref[pl.ds(lvl * HG_SUB, HG_SUB), :] * dats[s]
                dqh = jnp.dot(dm, _bf(kh), preferred_element_type=F32)
                dkh = jnp.dot(dmt, _bf(qh), preferred_element_type=F32)
                dqs[s] = dqs[s] + dqh * x
                dks[s] = dks[s] + dkh * x
                des[s].append(dqh * qh + dkh * kh)
        dlb = jnp.zeros((1, HEAD), F32)
        for s in rng:
            des[s].append(dqbs[s] * qbs[s] - dkds[s] * kds[s])
            dg = _split_dot(cst, jnp.concatenate(des[s], axis=0)) + d_lasts[s]
            dfg = dg / fgs[s] - dks[s]
            dq_ref[rows[s], :] = (dqs[s] * (sqs[s] * (1.0 + hqs[s] * (1.0 - sqs[s])))).astype(dq_ref.dtype)
            df_ref[rows[s], :] = (dfg * (1.0 - lbv) * sgs[s] * (1.0 - sgs[s])).astype(df_ref.dtype)
            di_ref[rows[s], :] = dvs[s].astype(di_ref.dtype)
            dlb = dlb + jnp.sum(dfg * (1.0 - sgs[s]), axis=0, keepdims=True)
        return dlb

    def colspec(off):
        return pl.BlockSpec((rows_blk, HEAD),
                            functools.partial(lambda h, b, j, off: (b * nb + nb - 1 - j, off + h), off=off))

    whole = lambda arr: pl.BlockSpec(arr.shape, lambda h, b, j: (0, 0))
    mats = lambda: pl.BlockSpec((1, 1, spb, HEAD, HEAD), lambda h, b, j: (b, h, nb - 1 - j, 0, 0))
    t_rows = bl * lp
    return pl.pallas_call(
        body, name="hgrn_bwd", grid=(nh, bl, nb),
        in_specs=[colspec(0), colspec(nh), colspec(2 * nh), pl.BlockSpec((1, HEAD), lambda h, b, j: (0, h)),
                  whole(cstack), whole(cstack_t), whole(pstack), whole(pstack_t), mats(), mats(), colspec(0)],
        out_specs=[colspec(0), colspec(0), colspec(0), pl.BlockSpec((1, HEAD), lambda h, b, j: (0, h))],
        out_shape=[jax.ShapeDtypeStruct((t_rows, d), BF16)] * 3 + [jax.ShapeDtypeStruct((1, d), F32)],
        scratch_shapes=[pltpu.VMEM((HEAD, HEAD), F32)],
        compiler_params=pltpu.CompilerParams(dimension_semantics=("arbitrary", "arbitrary", "arbitrary")),
    )(proj_main, proj_main, proj_main, lb, cstack, cstack_t, pstack, pstack_t, states, a_mats, do_scan)


def _key_query_mask(key0, qry0, nk, nq_, causal):
    key = key0 + lax.broadcasted_iota(jnp.int32, (nk, 1), 0)
    if not causal:
        return key >= PAD_FRONT
    qry = qry0 + lax.broadcasted_iota(jnp.int32, (1, nq_), 1)
    return (key <= qry) & (key >= PAD_FRONT)


def _attn_tile(lp):
    return _tile(lp, ATTN_TILE_MAX, SEQ_BLOCK)


def attn_fwd_t(q_cat, k_cat, v_t, bl, lp, nm):
    tq = tk = _attn_tile(lp)
    nq = lp // tq
    hp = ATTN_HEADS_PER_STEP
    assert nm % hp == 0

    def body(q_ref, k_ref, vt_ref, o_ref, lse_ref, m_ref, l_ref, acc_ref):
        i = pl.program_id(2)
        m_ref[...] = jnp.full_like(m_ref, NEG)
        l_ref[...] = jnp.zeros_like(l_ref)
        acc_ref[...] = jnp.zeros_like(acc_ref)

        def step(c, mask):
            c0 = pl.multiple_of(c * tk, tk)
            for hh in range(hp):
                cols = pl.ds(hh * QK_PAD, QK_PAD)
                st = lax.dot_general(k_ref[pl.ds(c0, tk), cols], q_ref[:, cols], NT_DIMS,
                                     preferred_element_type=F32)
                if mask is not None:
                    st = jnp.where(_key_query_mask(c * tk, i * tq, tk, tq, mask == "causal"), st, NEG)
                m_old = m_ref[hh]
                m_new = jnp.maximum(m_old, jnp.max(st, axis=0, keepdims=True))
                alpha = jnp.exp(m_old - m_new)
                pt = jnp.exp(st - m_new)
                l_ref[hh] = alpha * l_ref[hh] + jnp.sum(pt, axis=0, keepdims=True)
                acc_ref[hh] = alpha * acc_ref[hh] + jnp.dot(vt_ref[0, hh, pl.ds(c, 1)][0], _bf(pt),
                                                            preferred_element_type=F32)
                m_ref[hh] = m_new

        def mid(c, carry):
            step(c, None)
            return carry

        @pl.when(i == 0)
        def _():
            step(0, "causal")

        @pl.when(i > 0)
        def _():
            step(0, "pad")
            lax.fori_loop(1, i, mid, 0)
            step(i, "causal")

        for hh in range(hp):
            o_ref[:, pl.ds(hh * HEAD, HEAD)] = jnp.transpose(acc_ref[hh] / l_ref[hh]).astype(o_ref.dtype)
            lse_ref[0, hh, 0] = m_ref[hh] + jnp.log(l_ref[hh])

    return pl.pallas_call(
        body, name="attn_fwd", grid=(bl, nm // hp, nq),
        in_specs=[pl.BlockSpec((tq, hp * QK_PAD), lambda b, h, i: (b * nq + i, h)),
                  pl.BlockSpec((lp, hp * QK_PAD), lambda b, h, i: (b, h)),
                  pl.BlockSpec((1, hp, nq, HEAD, tk), lambda b, h, i: (b, h, 0, 0, 0))],
        out_specs=[pl.BlockSpec((tq, hp * HEAD), lambda b, h, i: (b * nq + i, h)),
                   pl.BlockSpec((1, hp, 1, 1, tq), lambda b, h, i: (b, h, i, 0, 0))],
        out_shape=[jax.ShapeDtypeStruct((bl * lp, nm * HEAD), BF16),
                   jax.ShapeDtypeStruct((bl, nm, nq, 1, tq), F32)],
        scratch_shapes=[pltpu.VMEM((hp, 1, tq), F32), pltpu.VMEM((hp, 1, tq), F32), pltpu.VMEM((hp, HEAD, tq), F32)],
        compiler_params=pltpu.CompilerParams(dimension_semantics=("arbitrary", "arbitrary", "arbitrary")),
    )(q_cat, k_cat, v_t)


def attn_bwd_t(q_cat, k_cat, k_t, v, o, do, lse, bl, lp, nm):
    tq = tk = _attn_tile(lp)
    nq = lp // tq
    hp = ATTN_HEADS_PER_STEP
    assert nm % hp == 0

    def body(q_ref, k_ref, kt_ref, v_ref, o_ref, do_ref, lse_ref, dq_ref, dk_ref, dv_ref, dqt_ref, dka_ref, dva_ref):
        i = pl.program_id(2)

        @pl.when(i == 0)
        def _():
            dqt_ref[...] = jnp.zeros_like(dqt_ref)

        dka_ref[...] = jnp.zeros_like(dka_ref)
        dva_ref[...] = jnp.zeros_like(dva_ref)
        ones8 = jnp.ones((8, HEAD), BF16)

        def step(c, mask):
            c0 = pl.multiple_of(c * tq, tq)
            for hh in range(hp):
                qcols, vcols = pl.ds(hh * QK_PAD, QK_PAD), pl.ds(hh * HEAD, HEAD)
                qs = q_ref[pl.ds(c0, tq), qcols]
                dos = do_ref[pl.ds(c0, tq), vcols]
                prod = dos.astype(F32) * o_ref[pl.ds(c0, tq), vcols].astype(F32)
                hi = _bf(prod)
                lo = _bf(prod - hi.astype(F32))
                delta8 = (lax.dot_general(ones8, hi, NT_DIMS, preferred_element_type=F32)
                          + lax.dot_general(ones8, lo, NT_DIMS, preferred_element_type=F32))
                st = lax.dot_general(k_ref[:, qcols], qs, NT_DIMS, preferred_element_type=F32)
                pt = jnp.exp(st - lse_ref[0, hh, pl.ds(c, 1)][0])
                if mask is not None:
                    pt = jnp.where(_key_query_mask(i * tk, c * tq, tk, tq, mask == "causal"), pt, 0.0)
                dva_ref[hh] += jnp.dot(_bf(pt), dos, preferred_element_type=F32)
                dpt = lax.dot_general(v_ref[:, vcols], dos, NT_DIMS, preferred_element_type=F32)
                dst = _bf(pt * (dpt - jnp.tile(delta8, (tk // 8, 1))))
                dka_ref[hh] += jnp.dot(dst, qs, preferred_element_type=F32)
                dqt_ref[hh, pl.ds(c, 1)] += jnp.dot(kt_ref[0, hh, 0], dst, preferred_element_type=F32)[None]

        step(i, "causal")

        def rest_masked(c, carry):
            step(c, "pad")
            return carry

        def rest(c, carry):
            step(c, None)
            return carry

        @pl.when(i == 0)
        def _():
            lax.fori_loop(1, nq, rest_masked, 0)

        @pl.when(i > 0)
        def _():
            lax.fori_loop(i + 1, nq, rest, 0)

        for hh in range(hp):
            dk_ref[:, pl.ds(hh * QK_PAD, QK_PAD)] = dka_ref[hh].astype(dk_ref.dtype)
            dv_ref[:, pl.ds(hh * HEAD, HEAD)] = dva_ref[hh].astype(dv_ref.dtype)

        @pl.when(i == nq - 1)
        def _():
            for hh in range(hp):
                for c in range(nq):
                    dq_ref[pl.ds(c * tq, tq), pl.ds(hh * QK_PAD, QK_PAD)] = (
                        jnp.transpose(dqt_ref[hh, c])).astype(dq_ref.dtype)

    return pl.pallas_call(
        body, name="attn_bwd", grid=(bl, nm // hp, nq),
        in_specs=[pl.BlockSpec((lp, hp * QK_PAD), lambda b, h, i: (b, h)),
                  pl.BlockSpec((tk, hp * QK_PAD), lambda b, h, i: (b * nq + i, h)),
                  pl.BlockSpec((1, hp, 1, QK_PAD, tk), lambda b, h, i: (b, h, i, 0, 0)),
                  pl.BlockSpec((tk, hp * HEAD), lambda b, h, i: (b * nq + i, h)),
                  pl.BlockSpec((lp, hp * HEAD), lambda b, h, i: (b, h)),
                  pl.BlockSpec((lp, hp * HEAD), lambda b, h, i: (b, h)),
                  pl.BlockSpec((1, hp, nq, 1, tq), lambda b, h, i: (b, h, 0, 0, 0))],
        out_specs=[pl.BlockSpec((lp, hp * QK_PAD), lambda b, h, i: (b, h)),
                   pl.BlockSpec((tk, hp * QK_PAD), lambda b, h, i: (b * nq + i, h)),
                   pl.BlockSpec((tk, hp * HEAD), lambda b, h, i: (b * nq + i, h))],
        out_shape=[jax.ShapeDtypeStruct((bl * lp, nm * QK_PAD), BF16),
                   jax.ShapeDtypeStruct((bl * lp, nm * QK_PAD), BF16),
                   jax.ShapeDtypeStruct((bl * lp, nm * HEAD), BF16)],
        scratch_shapes=[pltpu.VMEM((hp, nq, QK_PAD, tq), F32), pltpu.VMEM((hp, tk, QK_PAD), F32),
                        pltpu.VMEM((hp, tk, HEAD), F32)],
        compiler_params=pltpu.CompilerParams(dimension_semantics=("arbitrary", "arbitrary", "arbitrary")),
    )(q_cat, k_cat, k_t, v, o, do, lse)


def _place():
    return lax.axis_index("x"), lax.axis_index("y"), lax.axis_index("c")


def gather_shards(packed):
    hbm = pl.BlockSpec(memory_space=pl.ANY)

    def body(src_ref, out_ref, send_sems, recv_sems, local_sem):
        x, y, c = _place()
        me = 2 * x + y
        chips = [(1 - x, y), (x, 1 - y), (1 - x, 1 - y)]
        local = pltpu.make_async_copy(src_ref, out_ref.at[me], local_sem)
        local.start()
        sends = []
        for k, (px, py) in enumerate(chips):
            cp = pltpu.make_async_remote_copy(src_ref=src_ref, dst_ref=out_ref.at[me], send_sem=send_sems.at[k],
                                              recv_sem=recv_sems.at[k], device_id=(px, py, c), device_id_type=MESH)
            cp.start()
            sends.append(cp)
        for k, (px, py) in enumerate(chips):
            pltpu.make_async_remote_copy(src_ref=src_ref, dst_ref=out_ref.at[2 * px + py], send_sem=send_sems.at[k],
                                         recv_sem=recv_sems.at[k], device_id=(px, py, c),
                                         device_id_type=MESH).wait_recv()
        for cp in sends:
            cp.wait_send()
        local.wait()

    return pl.pallas_call(
        body, name="gather_shards", in_specs=[hbm], out_specs=hbm,
        out_shape=jax.ShapeDtypeStruct((4,) + packed.shape, packed.dtype),
        scratch_shapes=[pltpu.SemaphoreType.DMA((3,)), pltpu.SemaphoreType.DMA((3,)), pltpu.SemaphoreType.DMA],
    )(packed)


def gather_small(small):
    hbm = pl.BlockSpec(memory_space=pl.ANY)

    def body(small_ref, all_ref, send_sems, recv_sems, local_sem):
        x, y, c = _place()
        me = 4 * x + 2 * y + c
        local = pltpu.make_async_copy(small_ref, all_ref.at[me], local_sem)
        local.start()
        others = [(x ^ ((r >> 2) & 1), y ^ ((r >> 1) & 1), c ^ (r & 1)) for r in range(1, 8)]
        sends = []
        for r, peer in enumerate(others):
            cp = pltpu.make_async_remote_copy(src_ref=small_ref, dst_ref=all_ref.at[me], send_sem=send_sems.at[r],
                                              recv_sem=recv_sems.at[r], device_id=peer, device_id_type=MESH)
            cp.start()
            sends.append(cp)
        for r, (px, py, pc) in enumerate(others):
            pltpu.make_async_remote_copy(src_ref=small_ref, dst_ref=all_ref.at[4 * px + 2 * py + pc],
                                         send_sem=send_sems.at[r], recv_sem=recv_sems.at[r],
                                         device_id=(px, py, pc), device_id_type=MESH).wait_recv()
        for cp in sends:
            cp.wait_send()
        local.wait()

    return pl.pallas_call(
        body, name="gather_small", in_specs=[hbm], out_specs=hbm,
        out_shape=jax.ShapeDtypeStruct((8,) + small.shape, small.dtype),
        scratch_shapes=[pltpu.SemaphoreType.DMA((7,)), pltpu.SemaphoreType.DMA((7,)), pltpu.SemaphoreType.DMA],
    )(small)


def swap_with_sibling(name, parts):
    n = len(parts)
    hbm = pl.BlockSpec(memory_space=pl.ANY)

    def body(*refs):
        x, y, c = _place()
        cps = [pltpu.make_async_remote_copy(src_ref=refs[j], dst_ref=refs[n + j], send_sem=refs[2 * n].at[j],
                                            recv_sem=refs[2 * n + 1].at[j], device_id=(x, y, 1 - c),
                                            device_id_type=MESH) for j in range(n)]
        for cp in cps:
            cp.start()
        for cp in cps:
            cp.wait()

    return pl.pallas_call(
        body, name=name, in_specs=[hbm] * n, out_specs=[hbm] * n,
        out_shape=[jax.ShapeDtypeStruct(p.shape, p.dtype) for p in parts],
        scratch_shapes=[pltpu.SemaphoreType.DMA((n,)), pltpu.SemaphoreType.DMA((n,))],
    )(*parts)


def _chips3():
    x, y, c = _place()
    return [(1 - x, y, c), (x, 1 - y, c), (1 - x, 1 - y, c)]


def _push_copies(src_refs, land_refs, send_sems, recv_sems, per_chip):
    x, y, _ = _place()
    cps = []
    for j, (src_ref, land_ref) in enumerate(zip(src_refs, land_refs)):
        for k, (px, py, pc) in enumerate(_chips3()):
            part = src_ref.at[2 * px + py] if per_chip else src_ref
            slot = k if per_chip else 2 * x + y
            cps.append(pltpu.make_async_remote_copy(
                src_ref=part, dst_ref=land_ref.at[slot], send_sem=send_sems.at[3 * j + k],
                recv_sem=recv_sems.at[3 * j + k], device_id=(px, py, pc), device_id_type=MESH))
    return cps


def push_start(name, srcs, per_chip):
    n = len(srcs)
    hbm = pl.BlockSpec(memory_space=pltpu.HBM)
    sem = pl.BlockSpec(memory_space=pltpu.SEMAPHORE)
    lands = [lax.empty((3 if per_chip else 4,) + s.shape[-2:], s.dtype) for s in srcs]

    def body(*refs):
        src_refs, land_refs = refs[:n], refs[n:2 * n]
        send_sems, recv_sems = refs[2 * n], refs[2 * n + 1]
        for cp in _push_copies(src_refs, land_refs, send_sems, recv_sems, per_chip):
            cp.start()
        refs[-1][...] = jnp.zeros_like(refs[-1])

    outs = pl.pallas_call(
        body, name=name,
        out_shape=(pltpu.SemaphoreType.DMA((3 * n,)), pltpu.SemaphoreType.DMA((3 * n,)),
                   *[pltpu.HBM(a.shape, a.dtype) for a in list(srcs) + lands], jax.ShapeDtypeStruct((8, HEAD), F32)),
        in_specs=(hbm,) * (2 * n),
        out_specs=(sem, sem) + (hbm,) * (2 * n) + (pl.BlockSpec(memory_space=pltpu.VMEM),),
        input_output_aliases={j: 2 + j for j in range(2 * n)},
        compiler_params=pltpu.CompilerParams(has_side_effects=pltpu.SideEffectType.DATAFLOW_SIDE_EFFECTING),
    )(*[pltpu.with_memory_space_constraint(a, pltpu.HBM) for a in list(srcs) + lands])
    return tuple(outs[:-1]), outs[-1]


def push_wait(name, handle, after, per_chip):
    send_sems, recv_sems = handle[0], handle[1]
    thru = handle[2:]
    n = len(thru) // 2
    hbm = pl.BlockSpec(memory_space=pltpu.HBM)
    sem = pl.BlockSpec(memory_space=pltpu.SEMAPHORE)

    def body(*refs):
        src_refs, land_refs = refs[:n], refs[n:2 * n]
        for cp in _push_copies(src_refs, land_refs, refs[2 * n], refs[2 * n + 1], per_chip):
            cp.wait_send()
            cp.wait_recv()

    outs = pl.pallas_call(
        body, name=name,
        out_shape=tuple(pltpu.HBM(a.shape, a.dtype) for a in thru),
        in_specs=(hbm,) * (2 * n) + (sem, sem, pl.BlockSpec(memory_space=pl.ANY)), out_specs=(hbm,) * (2 * n),
        input_output_aliases={j: j for j in range(2 * n)},
        compiler_params=pltpu.CompilerParams(has_side_effects=pltpu.SideEffectType.DATAFLOW_SIDE_EFFECTING),
    )(*thru, send_sems, recv_sems, after)
    return outs[:n], outs[n:]


def join_gathered(name, own, landed, my_chip):
    blocks = lax.dynamic_update_index_in_dim(landed, own, my_chip, 0)
    _, r, c = blocks.shape
    if name in COL_SHARDED:
        return blocks.transpose(1, 0, 2).reshape(r, 4 * c)
    return blocks.reshape(4 * r, c)


def adamw(name, w, g_parts, m, v):
    r, c = w.shape
    tr = r if r * c <= 65536 else _tile(r, 128, 8)
    ng = len(g_parts)

    def body(*refs):
        w_ref, m_ref, v_ref = refs[0], refs[1 + ng], refs[2 + ng]
        g_ref, d_ref, nm_ref, nv_ref = refs[3 + ng:]
        gv = refs[1][...]
        for k in range(1, ng):
            gv = gv + refs[1 + k][...]
        m_new = ADAM_B1 * m_ref[...] + (1.0 - ADAM_B1) * gv
        v_new = ADAM_B2 * v_ref[...] + (1.0 - ADAM_B2) * (gv * gv)
        m_hat = m_new / (1.0 - ADAM_B1 ** ADAM_STEP)
        v_hat = v_new / (1.0 - ADAM_B2 ** ADAM_STEP)
        g_ref[...] = gv
        d_ref[...] = -ADAM_LR * (m_hat / (jnp.sqrt(v_hat) + ADAM_EPS) + ADAM_WD * w_ref[...])
        nm_ref[...] = m_new
        nv_ref[...] = v_new

    spec = pl.BlockSpec((tr, c), lambda i: (i, 0))
    return pl.pallas_call(
        body, name=name, grid=(r // tr,), in_specs=[spec] * (3 + ng), out_specs=[spec] * 4,
        out_shape=[jax.ShapeDtypeStruct((r, c), F32)] * 4,
        compiler_params=pltpu.CompilerParams(dimension_semantics=("arbitrary",)),
    )(w, *g_parts, m, v)


def split_full(name, full, s):
    if name in COL_SHARDED:
        c = full.shape[1] // 4
        return full[:, s * c:(s + 1) * c]
    r = full.shape[0] // 4
    return full[s * r:(s + 1) * r]


def kernel(x, meta_tokens, w_in, b_gate, lb_logits, hg_norm_g, w_hg_o, q_a_norm_g, w_q_b, kv_a_norm_g, w_kv_b, w_mla_o, w_out, mix_pre_g, mix_post_g, ffn_pre_g, ffn_post_g, w_ffn_in, w_ffn_out, loss_target, m_meta_tokens, m_w_in, m_b_gate, m_lb_logits, m_hg_norm_g, m_w_hg_o, m_q_a_norm_g, m_w_q_b, m_kv_a_norm_g, m_w_kv_b, m_w_mla_o, m_w_out, m_mix_pre_g, m_mix_post_g, m_ffn_pre_g, m_ffn_post_g, m_w_ffn_in, m_w_ffn_out, v_meta_tokens, v_w_in, v_b_gate, v_lb_logits, v_hg_norm_g, v_w_hg_o, v_q_a_norm_g, v_w_q_b, v_kv_a_norm_g, v_w_kv_b, v_w_mla_o, v_w_out, v_mix_pre_g, v_mix_post_g, v_ffn_pre_g, v_ffn_post_g, v_w_ffn_in, v_w_ffn_out):
    wts = dict(meta_tokens=meta_tokens, w_in=w_in[0], b_gate=b_gate, lb_logits=lb_logits, hg_norm_g=hg_norm_g,
               w_hg_o=w_hg_o[0], q_a_norm_g=q_a_norm_g, w_q_b=w_q_b[0], kv_a_norm_g=kv_a_norm_g, w_kv_b=w_kv_b[0],
               w_mla_o=w_mla_o[0], w_out=w_out[0], mix_pre_g=mix_pre_g, mix_post_g=mix_post_g, ffn_pre_g=ffn_pre_g,
               ffn_post_g=ffn_post_g, w_ffn_in=w_ffn_in[0], w_ffn_out=w_ffn_out[0])
    mom_m = dict(meta_tokens=m_meta_tokens, w_in=m_w_in[0], b_gate=m_b_gate, lb_logits=m_lb_logits,
                 hg_norm_g=m_hg_norm_g, w_hg_o=m_w_hg_o[0], q_a_norm_g=m_q_a_norm_g, w_q_b=m_w_q_b[0],
                 kv_a_norm_g=m_kv_a_norm_g, w_kv_b=m_w_kv_b[0], w_mla_o=m_w_mla_o[0], w_out=m_w_out[0],
                 mix_pre_g=m_mix_pre_g, mix_post_g=m_mix_post_g, ffn_pre_g=m_ffn_pre_g, ffn_post_g=m_ffn_post_g,
                 w_ffn_in=m_w_ffn_in[0], w_ffn_out=m_w_ffn_out[0])
    mom_v = dict(meta_tokens=v_meta_tokens, w_in=v_w_in[0], b_gate=v_b_gate, lb_logits=v_lb_logits,
                 hg_norm_g=v_hg_norm_g, w_hg_o=v_w_hg_o[0], q_a_norm_g=v_q_a_norm_g, w_q_b=v_w_q_b[0],
                 kv_a_norm_g=v_kv_a_norm_g, w_kv_b=v_w_kv_b[0], w_mla_o=v_w_mla_o[0], w_out=v_w_out[0],
                 mix_pre_g=v_mix_pre_g, mix_post_g=v_mix_post_g, ffn_pre_g=v_ffn_pre_g, ffn_post_g=v_ffn_post_g,
                 w_ffn_in=v_w_ffn_in[0], w_ffn_out=v_w_ffn_out[0])

    bl, seq, d = x.shape
    lp = PAD_FRONT + N_META + seq
    t_rows = bl * lp
    nh = d // HEAD
    ql, kvl = wts["w_q_b"].shape[0], wts["w_kv_b"].shape[0]
    nm = (4 * wts["w_mla_o"].shape[0]) // HEAD
    ffn = 4 * wts["w_ffn_out"].shape[0]
    mla_w = ql + kvl + HEAD
    assert ql == kvl and ql % HEAD == 0 and seq % SEQ_BLOCK == 0 and d % HEAD == 0
    scale = (HEAD + ROPE) ** -0.5
    my_chip = 2 * lax.axis_index("x") + lax.axis_index("y")

    mcols = meta_tokens.shape[1]
    meta_all = gather_shards(meta_tokens)
    meta_full = jnp.concatenate([meta_all[s] for s in range(4)], axis=1)

    def start_gather(name, names, order_after):
        srcs = [_bf(wts[n]) for n in names]
        if order_after is not None:
            srcs[0] = srcs[0] + order_after[0, 0].astype(BF16)
        return push_start(name, srcs, per_chip=False)

    def finish_gather(name, names, started, after):
        owns, landed = push_wait(name, started[0], after, per_chip=False)
        return {n: join_gathered(n, own, land, my_chip) for n, own, land in zip(names, owns, landed)}

    rest_names = tuple(n for n in BIG if n != "w_in")
    my_c = lax.axis_index("c")
    w_in_bf = _bf(wts["w_in"])
    half = w_in_bf.shape[0] // 2
    own_half = (lax.dynamic_slice_in_dim(w_in_bf, my_c * half, half, axis=0)
                + (meta_all[0, :1, :1] * 0.0)[0, 0].astype(BF16))
    gather_1 = push_start("gather_w_in_start", [own_half], per_chip=False)
    gather_2 = start_gather("gather_rest_start", rest_names, gather_1[1])

    h0 = jnp.concatenate([jnp.zeros((bl, PAD_FRONT, d), F32), jnp.broadcast_to(meta_full[None], (bl, N_META, d)), x],
                         axis=1).reshape(t_rows, d)
    tiles_seq, tiles_real = lp // SEQ_BLOCK, seq // SEQ_BLOCK
    assert PAD_FRONT + N_META == SEQ_BLOCK

    def real_block(i):
        return (i // tiles_seq) * tiles_real + jnp.maximum(i % tiles_seq - 1, 0)

    meta_rows = jnp.broadcast_to(((jnp.arange(lp) >= PAD_FRONT) & (jnp.arange(lp) < PAD_FRONT + N_META)
                                  ).astype(F32)[:, None], (lp, HEAD))
    pos = (jnp.arange(lp, dtype=jnp.int32) - PAD_FRONT).astype(F32)
    inv_freq = 1.0 / (ROPE_THETA ** (jnp.arange(0, ROPE, 2, dtype=F32) / ROPE))
    ang = pos[:, None] * inv_freq[None, :]
    zeros32 = jnp.zeros((lp, ROPE_HALF), F32)
    zeros64 = jnp.zeros((lp, HEAD - ROPE), F32)
    t_cos = jnp.concatenate([jnp.cos(ang), jnp.cos(ang), zeros64], axis=1)
    t_up = jnp.concatenate([zeros32, jnp.sin(ang), zeros64], axis=1)
    t_dn = jnp.concatenate([-jnp.sin(ang), zeros32, zeros64], axis=1)
    real = jnp.broadcast_to((jnp.arange(lp) >= PAD_FRONT + N_META).astype(F32)[:, None], (lp, d))
    lb_soft = jax.nn.softmax(lb_logits.astype(F32), axis=0)
    lb = lb_soft[0:1]

    (u1,) = rowwise("norm_mix_pre", lambda h, g: _rms(h, g), [(h0, d, 0)], [], [mix_pre_g + gather_2[1][0, 0]],
                    [(d, BF16)])
    _, (fetched,) = push_wait("gather_w_in_wait", gather_1[0], u1, per_chip=False)
    (handed,) = swap_with_sibling("swap_w_in", [fetched])
    halves = jnp.stack([fetched, handed])
    remote = jnp.concatenate([lax.dynamic_index_in_dim(halves, my_c, 0, keepdims=False),
                              lax.dynamic_index_in_dim(halves, 1 - my_c, 0, keepdims=False)], axis=1)
    full = {"w_in": join_gathered("w_in", w_in_bf, remote, my_chip)}
    w_main = jnp.concatenate([full["w_in"][:, :4 * d], full["w_in"][:, -2 * d:]], axis=1)
    w_mla = jnp.pad(full["w_in"][:, 4 * d:4 * d + ql + kvl + ROPE], ((0, 0), (0, HEAD - ROPE)))
    proj_main = matmul("proj_main", u1, w_main, "nn", out_dtype=BF16)
    proj_mla = matmul("proj_mla", u1, w_mla, "nn", out_dtype=BF16)
    hg_consts = _hg_constants()
    o_scan, states, a_mats = hgrn_fwd(proj_main, lb, hg_consts, bl, lp, d)

    full.update(finish_gather("gather_rest_wait", rest_names, gather_2, o_scan))
    w_qb = jnp.pad(full["w_q_b"].reshape(ql, nm, HEAD + ROPE), ((0, 0), (0, 0), (0, QK_PAD - HEAD - ROPE))
                   ).reshape(ql, nm * QK_PAD)
    w_kvb = full["w_kv_b"]

    def hg_out_fn(o, hg, g):
        ov = jnp.concatenate([_rms(o[:, h * HEAD:(h + 1) * HEAD], g) for h in range(nh)], axis=1) * _silu(hg)
        return ov, ov

    y_a, o_hg = matmul_fused("hgrn_out_y_a", hg_out_fn, [(o_scan, d, 0), (proj_main, d, 3)], [hg_norm_g],
                             _bf(full["w_hg_o"]), [(0, d)], "nn", [(d, BF16)], tm=512)

    def norm_fn(c, g):
        cn = _rms(c, g)
        return cn, cn

    q_full, qn = matmul_fused("q_norm_up", norm_fn, [(proj_mla, ql, 0)], [q_a_norm_g], w_qb, [(0, ql)], "nn",
                              [(ql, BF16)], tm=512)
    kv_full, kvn = matmul_fused("kv_norm_up", norm_fn, [(proj_mla, kvl, 1)], [kv_a_norm_g], w_kvb, [(0, kvl)], "nn",
                                [(kvl, BF16)], tm=512)

    def mla_prep_fn(qf, kvf, kpe, cos, s_up, s_dn):
        qf = qf * scale
        kpe_r = _rope(kpe, cos, s_up, s_dn)
        qs, ks, vs = [], [], []
        for h in range(nm):
            qs += [qf[:, h * QK_PAD:h * QK_PAD + HEAD], _rope(qf[:, h * QK_PAD + HEAD:(h + 1) * QK_PAD], cos, s_up, s_dn)]
            ks += [kvf[:, h * QK_PAD:h * QK_PAD + HEAD], kpe_r]
            vs += [kvf[:, h * QK_PAD + HEAD:(h + 1) * QK_PAD]]
        return jnp.concatenate(qs, axis=1), jnp.concatenate(ks, axis=1), jnp.concatenate(vs, axis=1)

    kpe_blk = (ql + kvl) // HEAD
    q_cat, k_cat, v_att = rowwise("mla_prep", mla_prep_fn,
                                  [(q_full, nm * QK_PAD, 0), (kv_full, nm * QK_PAD, 0), (proj_mla, HEAD, kpe_blk)],
                                  [t_cos, t_up, t_dn], [], [(nm * QK_PAD, BF16), (nm * QK_PAD, BF16), (nm * HEAD, BF16)])
    at = _attn_tile(lp)
    v_t = v_att.reshape(bl, lp // at, at, nm, HEAD).transpose(0, 3, 1, 4, 2)
    k_t = k_cat.reshape(bl, lp // at, at, nm, QK_PAD).transpose(0, 3, 1, 4, 2)
    o_mla, lse = attn_fwd_t(q_cat, k_cat, v_t, bl, lp, nm)
    y_b = matmul("y_b", o_mla, _bf(full["w_mla_o"]), "nn", out_dtype=BF16)

    def gate_fn(ya, yb, ga, gb, bias):
        zv = _sigmoid(ga + bias[:, :d]) * ya + _sigmoid(gb + bias[:, d:]) * yb
        return zv, zv

    mixed, z = matmul_fused("gate_mix_out", gate_fn,
                            [(y_a, d, 0), (y_b, d, 0), (proj_main, d, 4), (proj_main, d, 5)], [b_gate],
                            _bf(full["w_out"]), [(0, d)], "nn", [(d, BF16)], tm=512)

    def mid_fn(h, mx, g_post, g_pre):
        h1v = h + _rms(mx, g_post)
        u2v = _rms(h1v, g_pre)
        return u2v, h1v, u2v

    gu, h1, u2 = matmul_fused("norm_mid_ffn_in", mid_fn, [(h0, d, 0), (mixed, d, 0)], [mix_post_g, ffn_pre_g],
                              _bf(full["w_ffn_in"]), [(0, d)], "nn", [(d, F32), (d, BF16)])
    def swiglu_fn(gt, up):
        a = _silu(gt) * up
        return a, a

    f_out, act = matmul_fused("swiglu_ffn_out", swiglu_fn, [(gu, ffn, 0), (gu, ffn, 1)], [],
                              _bf(full["w_ffn_out"]), [(0, ffn)], "nn", [(ffn, BF16)])

    def loss_fn(h1v, fv, tg, realv, g_post):
        h2 = h1v + _rms(fv, g_post)
        diff = (h2 - tg) * realv
        part = jnp.broadcast_to(0.5 * jnp.sum(diff * diff, keepdims=True) / d, (1, HEAD))
        dy = diff / d
        df, dg = _rms_bwd(fv, g_post, dy)
        return df, dy, df, part, dg

    def seq_tile(i):
        return i % tiles_seq

    d_act, dy, df, loss_part, g_ffn_post = matmul_fused(
        "loss_head_d_act", loss_fn,
        [(h1, d, 0), (f_out, d, 0), (loss_target.reshape(bl * seq, d), d, 0, real_block), (real, d, 0, seq_tile)],
        [ffn_post_g], _bf(full["w_ffn_out"]), [(0, d)], "nt", [(d, BF16), (d, BF16)],
        acc_outs=[(1, HEAD), (1, d)])
    grads = {}
    grads["w_ffn_out"] = matmul("gw_ffn_out", act, df, "tn")

    def swiglu_bwd_fn(gt, up, da):
        dgt, dup = da * up * _silu_grad(gt), da * _silu(gt)
        return dgt, dup, jnp.concatenate([dgt, dup], axis=1)

    du2, dgu = matmul_fused("swiglu_bwd_d_u2", swiglu_bwd_fn, [(gu, ffn, 0), (gu, ffn, 1), (d_act, ffn, 0)], [],
                            _bf(full["w_ffn_in"]), [(0, ffn), (ffn, 2 * ffn)], "nt", [(2 * ffn, BF16)])
    grads["w_ffn_in"] = matmul("gw_ffn_in", u2, dgu, "tn")

    def mid_bwd_fn(dyv, h1v, du2v, mx, g_pre, g_post):
        dx, dg_pre = _rms_bwd(h1v, g_pre, du2v)
        dh1 = dyv + dx
        dmx, dg_post = _rms_bwd(mx, g_post, dh1)
        return dmx, dh1, dmx, dg_pre, dg_post

    dz, dh1, dmixed, g_ffn_pre, g_mix_post = matmul_fused(
        "norm_mid_bwd_d_z", mid_bwd_fn, [(dy, d, 0), (h1, d, 0), (du2, d, 0), (mixed, d, 0)],
        [ffn_pre_g, mix_post_g], _bf(full["w_out"]), [(0, d)], "nt", [(d, BF16), (d, BF16)], tm=512,
        acc_outs=[(1, d), (1, d)])
    grads["w_out"] = matmul("gw_out", z, dmixed, "tn")

    def gate_bwd_fn(dzv, ya, yb, ga, gb, bias):
        sa, sb = _sigmoid(ga + bias[:, :d]), _sigmoid(gb + bias[:, d:])
        dga = dzv * ya * sa * (1.0 - sa)
        dgb = dzv * yb * sb * (1.0 - sb)
        dgates = jnp.concatenate([dga, dgb], axis=1)
        dya, dyb = dzv * sa, dzv * sb
        return dya, dyb, dya, dyb, dgates, jnp.sum(dgates, axis=0, keepdims=True)

    do_hg, do_mla, dy_a, dy_b, dgates, g_b_gate = matmul_fused(
        "gate_mix_bwd_d_o", gate_bwd_fn,
        [(dz, d, 0), (y_a, d, 0), (y_b, d, 0), (proj_main, d, 4), (proj_main, d, 5)], [b_gate],
        [_bf(full["w_hg_o"]), _bf(full["w_mla_o"])], [(0, 0, d), (1, 0, d)], "nt",
        [(d, BF16), (d, BF16), (2 * d, BF16)], acc_outs=[(1, 2 * d)])
    grads["w_hg_o"] = matmul("gw_hg_o", o_hg, dy_a, "tn")
    grads["w_mla_o"] = matmul("gw_mla_o", o_mla, dy_b, "tn")

    early = ("w_hg_o", "w_mla_o", "w_out", "w_ffn_in", "w_ffn_out")
    late = ("w_in", "w_q_b", "w_kv_b")

    def start_grads(name, names):
        sends = [_bf(jnp.stack([split_full(n, grads[n], s) for s in range(4)])) for n in names]
        mines = []
        for n in names:
            r, c = wts[n].shape
            axis, size = (1, c) if n in COL_SHARDED else (0, r)
            mines.append(lax.dynamic_slice_in_dim(grads[n], my_chip * size, size, axis=axis))
        handle, token = push_start(name, sends, per_chip=True)
        return handle, token, mines

    def finish_grads(tag, names, started, after):
        handle, _, mines = started
        _, landed = push_wait(f"grads_{tag}_wait", handle, after, per_chip=True)
        parts = []
        for n, mine, land in zip(names, mines, landed):
            r, c = mine.shape
            tr = _tile(r, 256, 16)
            land2 = land.reshape(3 * r, c)
            parts.append(rowwise(f"sum_chips_{n}", lambda a, r0, r1, r2: a + r0 + r1 + r2,
                                 [(mine, c, 0)] + [(land2, c, 0, k * (r // tr)) for k in range(3)],
                                 [], [], [(c, F32)], tm=tr)[0])
        sibs = swap_with_sibling(f"swap_{tag}", parts)
        return {n: [p, s] for n, p, s in zip(names, parts, sibs)}

    grads_early = start_grads("grads_early_start", early)
    token_a = grads_early[1]

    def hg_out_bwd_fn(do, o, hg, g):
        sg = _silu(hg)
        dn = do * sg
        dos, dgs, ons = [], 0.0, []
        for h in range(nh):
            sl = slice(h * HEAD, (h + 1) * HEAD)
            dx, dg = _rms_bwd(o[:, sl], g, dn[:, sl])
            dos.append(dx)
            dgs = dgs + dg
            ons.append(_rms(o[:, sl], g))
        dhg = do * jnp.concatenate(ons, axis=1) * _silu_grad(hg)
        return jnp.concatenate(dos, axis=1), dhg, dgs

    do_scan, dhg, g_hg_norm = rowwise("hgrn_out_bwd", hg_out_bwd_fn, [(do_hg, d, 0), (o_scan, d, 0), (proj_main, d, 3)],
                                      [], [hg_norm_g], [(d, BF16), (d, BF16)], [(1, HEAD)])
    dhq, dhf, dhi, g_lb = hgrn_bwd(proj_main, lb + token_a[0, 0], hg_consts, states, a_mats, do_scan, bl, lp, d)

    dq_cat, dk_cat, dv_att = attn_bwd_t(q_cat, k_cat, k_t, v_att, o_mla, do_mla, lse, bl, lp, nm)

    def mla_prep_bwd_fn(dqc, dkc, dvv, cos, s_up, s_dn):
        dqc = dqc * scale
        dqs, dkvs, dkpe = [], [], 0.0
        for h in range(nm):
            dqs += [dqc[:, h * QK_PAD:h * QK_PAD + HEAD],
                    _rope_bwd(dqc[:, h * QK_PAD + HEAD:(h + 1) * QK_PAD], cos, s_up, s_dn)]
            dkvs += [dkc[:, h * QK_PAD:h * QK_PAD + HEAD], dvv[:, h * HEAD:(h + 1) * HEAD]]
            dkpe = dkpe + dkc[:, h * QK_PAD + HEAD:(h + 1) * QK_PAD]
        dqf, dkvf = jnp.concatenate(dqs, axis=1), jnp.concatenate(dkvs, axis=1)
        return dqf, dkvf, dqf, dkvf, _rope_bwd(dkpe, cos, s_up, s_dn)

    dqn, dkvn, dq_full, dkv_full, dkpe = matmul_fused(
        "mla_prep_bwd_d_norms", mla_prep_bwd_fn,
        [(dq_cat, nm * QK_PAD, 0), (dk_cat, nm * QK_PAD, 0), (dv_att, nm * HEAD, 0),
         (t_cos, HEAD, 0, seq_tile), (t_up, HEAD, 0, seq_tile), (t_dn, HEAD, 0, seq_tile)], [],
        [w_qb, w_kvb], [(0, 0, nm * QK_PAD), (1, 0, nm * QK_PAD)], "nt",
        [(nm * QK_PAD, BF16), (nm * QK_PAD, BF16), (HEAD, F32)])
    g_wqb = matmul("gw_q_b", qn, dq_full, "tn")
    grads["w_q_b"] = g_wqb.reshape(ql, nm, QK_PAD)[:, :, :HEAD + ROPE].reshape(ql, nm * (HEAD + ROPE))
    grads["w_kv_b"] = matmul("gw_kv_b", kvn, dkv_full, "tn")

    def mla_norms_bwd_fn(dqnv, dkvnv, cq, ckv, dkpev, gq, gk):
        dcq, dgq = _rms_bwd(cq, gq, dqnv)
        dckv, dgk = _rms_bwd(ckv, gk, dkvnv)
        return jnp.concatenate([dcq, dckv, dkpev], axis=1), dgq, dgk

    dmla, g_q_norm, g_kv_norm = rowwise("mla_norms_bwd", mla_norms_bwd_fn,
                                        [(dqn, ql, 0), (dkvn, kvl, 0), (proj_mla, ql, 0), (proj_mla, kvl, 1),
                                         (dkpe, HEAD, 0)], [], [q_a_norm_g, kv_a_norm_g],
                                        [(mla_w, BF16)], [(1, ql), (1, kvl)])

    d_pieces = [dhq, dhf, dhi, dhg, dgates, dmla]
    gw_parts = [matmul(f"gw_in_{k}", u1, dp, "tn") for k, dp in enumerate(d_pieces)]
    grads["w_in"] = jnp.concatenate(gw_parts[:4] + [gw_parts[5][:, :ql + kvl + ROPE], gw_parts[4]], axis=1)
    grads_late = start_grads("grads_late_start", late)
    w_mla_after = w_mla + grads_late[1][0, 0].astype(BF16)
    w_pieces = [w_main[:, 0:d], w_main[:, d:2 * d], w_main[:, 2 * d:3 * d], w_main[:, 3 * d:4 * d],
                w_main[:, 4 * d:6 * d], w_mla_after]
    du1 = matmul("d_u1", d_pieces, w_pieces, "nt", out_dtype=BF16)

    def first_bwd_fn(dh1v, h, du1v, is_meta, g):
        dx, dg = _rms_bwd(h, g, du1v)
        dh0v = dh1v + dx
        return dh0v, dg, dh0v * jnp.tile(is_meta, (1, d // HEAD))

    grad_x, g_mix_pre, meta_tile = rowwise(
        "norm_mix_pre_bwd", first_bwd_fn, [(dh1, d, 0), (h0, d, 0), (du1, d, 0)], [meta_rows], [mix_pre_g],
        [(d, F32, bl * seq, real_block)], [(1, d), (SEQ_BLOCK, d)])
    grad_x = grad_x.reshape(bl, seq, d)

    g_parts = finish_grads("early", early, grads_early, g_mix_pre)
    updates = {}

    def update(n, parts):
        w2 = wts[n].reshape(-1, wts[n].shape[-1])
        updates[n] = adamw("adamw_" + n, w2, [p.reshape(w2.shape) for p in parts], mom_m[n].reshape(w2.shape),
                           mom_v[n].reshape(w2.shape))

    for n in early:
        update(n, g_parts[n])
    g_parts = finish_grads("late", late, grads_late, updates[early[-1]][0])
    for n in late:
        update(n, g_parts[n])
    p0 = lb_soft[0:1]
    g_lb_logits = jnp.concatenate([g_lb * p0 * (1.0 - p0), -g_lb * p0 * (1.0 - p0)], axis=0)

    def row_of(vec):
        return vec.reshape(-1, d) if vec.size >= d else jnp.pad(vec.reshape(1, -1), ((0, 0), (0, d - vec.size)))

    small_parts = dict(b_gate=g_b_gate, lb_logits=g_lb_logits, hg_norm_g=g_hg_norm, q_a_norm_g=g_q_norm,
                       kv_a_norm_g=g_kv_norm, mix_pre_g=g_mix_pre, mix_post_g=g_mix_post, ffn_pre_g=g_ffn_pre,
                       ffn_post_g=g_ffn_post)
    g_meta = meta_tile[PAD_FRONT:PAD_FRONT + N_META]
    small_rows = [row_of(small_parts[n]) for n in SMALL] + [row_of(g_meta)]
    n_small = sum(r.shape[0] for r in small_rows)
    small = jnp.pad(jnp.concatenate(small_rows, axis=0), ((0, -(-n_small // 8) * 8 - n_small), (0, 0)))
    all_small = gather_small(small)
    small_t = small.shape[0]

    def sum8_fn(*slabs):
        acc = slabs[0]
        for s in slabs[1:]:
            acc = acc + s
        return acc

    (g_small,) = rowwise("sum_small", sum8_fn, [(all_small.reshape(8 * small_t, d), d, 0, k) for k in range(8)],
                         [], [], [(d, F32)], tm=small_t, n_rows=small_t)

    off = 0
    for n, part in zip(SMALL, small_rows[:-1]):
        rows = part.shape[0]
        update(n, [g_small[off:off + rows, :d].reshape(-1)[:wts[n].size]])
        off += rows
    update("meta_tokens", [lax.dynamic_slice_in_dim(g_small[off:off + N_META, :d], my_chip * mcols, mcols, axis=1)])

    loss = lax.psum(loss_part[0, 0], ("x", "y", "c"))

    def shaped(n, a):
        return a.reshape((1,) + wts[n].shape) if n in BIG else a.reshape(wts[n].shape)

    return (loss, grad_x, *[shaped(n, updates[n][k]) for k in range(4) for n in WEIGHTS])
```

```python
import functools
import math

import jax
import jax.numpy as jnp
from jax import lax
from jax.experimental import pallas as pl
from jax.experimental.pallas import tpu as pltpu

F32 = jnp.float32
BF16 = jnp.bfloat16
MESH = pl.DeviceIdType.MESH

N_META = 16
NORM_EPS = 1e-6
HEAD = 128
ROPE = 64
ROPE_HALF = ROPE // 2
QK_PAD = 2 * HEAD
ROPE_THETA = 10000.0
SEQ_BLOCK = 256
PAD_FRONT = SEQ_BLOCK - N_META
NEG = -1e30
VMEM_LIMIT = 56 * 1024 * 1024
ATTN_HEADS_PER_STEP = 1
ATTN_TILE_MAX = 768

ADAM_LR, ADAM_B1, ADAM_B2, ADAM_EPS, ADAM_WD, ADAM_STEP = 0.001, 0.9, 0.999, 1e-08, 0.01, 10

BIG = ("w_in", "w_hg_o", "w_q_b", "w_kv_b", "w_mla_o", "w_out", "w_ffn_in", "w_ffn_out")
COL_SHARDED = ("w_in", "w_q_b", "w_kv_b", "w_ffn_in")
SMALL = ("b_gate", "lb_logits", "hg_norm_g", "q_a_norm_g", "kv_a_norm_g", "mix_pre_g", "mix_post_g",
         "ffn_pre_g", "ffn_post_g")
WEIGHTS = ("meta_tokens", "w_in", "b_gate", "lb_logits", "hg_norm_g", "w_hg_o", "q_a_norm_g", "w_q_b",
           "kv_a_norm_g", "w_kv_b", "w_mla_o", "w_out", "mix_pre_g", "mix_post_g", "ffn_pre_g", "ffn_post_g",
           "w_ffn_in", "w_ffn_out")


def _tile(n, cap, unit=128):
    if n <= cap:
        return n
    best = None
    for t in range(unit, cap + 1, unit):
        if n % t == 0:
            best = t
    assert best is not None, (n, cap, unit)
    return best


def _sigmoid(x):
    return 1.0 / (1.0 + jnp.exp(-x))


def _bf(x):
    return x.astype(BF16)


def rowwise(name, fn, row_ins, seq_tabs, consts, row_outs, acc_outs=(), tm=SEQ_BLOCK, n_rows=None):
    t_rows = row_ins[0][0].shape[0] if n_rows is None else n_rows
    nt = t_rows // tm
    assert t_rows % tm == 0
    n_in = len(row_ins) + len(seq_tabs) + len(consts)
    n_row = len(row_outs)

    def body(*refs):
        vals = [r[...].astype(F32) for r in refs[:n_in]]
        res = fn(*vals)
        if not isinstance(res, (tuple, list)):
            res = (res,)
        outs = refs[n_in:]
        for k in range(n_row):
            outs[k][...] = res[k].astype(outs[k].dtype)
        if acc_outs:
            @pl.when(pl.program_id(0) == 0)
            def _():
                for k in range(len(acc_outs)):
                    outs[n_row + k][...] = jnp.zeros_like(outs[n_row + k])

            for k in range(len(acc_outs)):
                outs[n_row + k][...] += res[n_row + k]

    row_ins = [tuple(e) + (0,) * (4 - len(e)) for e in row_ins]
    in_specs = [pl.BlockSpec((tm, w), functools.partial(lambda i, j, ro: (ro(i) if callable(ro) else i + ro, j),
                                                        j=j, ro=ro)) for (_, w, j, ro) in row_ins]
    for tab in seq_tabs:
        per = tab.shape[0] // tm
        in_specs.append(pl.BlockSpec((tm, tab.shape[1]), functools.partial(lambda i, per: (i % per, 0), per=per)))
    for c in consts:
        in_specs.append(pl.BlockSpec(c.shape, lambda i: (0, 0)))
    row_outs = [tuple(e) + (t_rows, None)[len(e) - 2:] for e in row_outs]
    out_specs = [pl.BlockSpec((tm, w), functools.partial(lambda i, rm: (i if rm is None else rm(i), 0), rm=rm))
                 for (w, _, _, rm) in row_outs]
    out_specs += [pl.BlockSpec(s, lambda i: (0, 0)) for s in acc_outs]
    out_shape = [jax.ShapeDtypeStruct((rows, w), dt) for (w, dt, rows, _) in row_outs]
    out_shape += [jax.ShapeDtypeStruct(s, F32) for s in acc_outs]
    res = pl.pallas_call(
        body, name=name, grid=(nt,), in_specs=in_specs, out_specs=out_specs, out_shape=out_shape,
        compiler_params=pltpu.CompilerParams(dimension_semantics=("arbitrary",)),
    )(*[e[0] for e in row_ins], *seq_tabs, *consts)
    return res


def matmul(name, a, b, mode, out_dtype=F32):
    if mode != "tn":
        return _matmul_resident(name, a if isinstance(a, (list, tuple)) else [a],
                                b if isinstance(b, (list, tuple)) else [b], mode, out_dtype)
    kdim, m = a.shape
    n = b.shape[1]
    tn = _tile(n, 1536)
    tm, tk = _tile(m, 1408 if tn <= 1024 else 1024), _tile(kdim, 1536)
    nk = kdim // tk

    def body(a_ref, b_ref, o_ref, acc_ref):
        k = pl.program_id(2)

        @pl.when(k == 0)
        def _():
            acc_ref[...] = jnp.zeros_like(acc_ref)

        acc_ref[...] += lax.dot_general(a_ref[...], b_ref[...], TN_DIMS, preferred_element_type=F32)

        @pl.when(k == nk - 1)
        def _():
            o_ref[...] = acc_ref[...].astype(o_ref.dtype)

    return pl.pallas_call(
        body, name=name, grid=(m // tm, n // tn, nk),
        in_specs=[pl.BlockSpec((tk, tm), lambda i, j, k: (k, i)), pl.BlockSpec((tk, tn), lambda i, j, k: (k, j))],
        out_specs=pl.BlockSpec((tm, tn), lambda i, j, k: (i, j)),
        out_shape=jax.ShapeDtypeStruct((m, n), out_dtype),
        scratch_shapes=[pltpu.VMEM((tm, tn), F32)],
        compiler_params=pltpu.CompilerParams(dimension_semantics=("arbitrary", "arbitrary", "arbitrary"),
                                             vmem_limit_bytes=VMEM_LIMIT),
    )(a, b)


def _matmul_resident(name, a_list, b_list, mode, out_dtype):
    m = a_list[0].shape[0]
    n = b_list[0].shape[1] if mode == "nn" else b_list[0].shape[0]
    k_total = sum(a.shape[1] for a in a_list)
    out_bytes = 2 if out_dtype == BF16 else 4
    budget = VMEM_LIMIT - 4 * k_total * n - (6 << 20)
    tm = 1024
    while tm > 128 and 2 * tm * (2 * k_total + out_bytes * n) > budget:
        tm //= 2
    tm = _tile(m, tm)
    cn = _tile(n, 1024)
    npairs = len(a_list)

    def body(*refs):
        a_refs, b_refs, o_ref = refs[:npairs], refs[npairs:2 * npairs], refs[2 * npairs]
        for c in range(n // cn):
            acc = None
            for a_ref, b_ref in zip(a_refs, b_refs):
                if mode == "nn":
                    part = jnp.dot(a_ref[...], b_ref[:, pl.ds(c * cn, cn)], preferred_element_type=F32)
                else:
                    part = lax.dot_general(a_ref[...], b_ref[pl.ds(c * cn, cn), :], NT_DIMS,
                                           preferred_element_type=F32)
                acc = part if acc is None else acc + part
            o_ref[:, pl.ds(c * cn, cn)] = acc.astype(o_ref.dtype)

    in_specs = [pl.BlockSpec((tm, a.shape[1]), lambda i: (i, 0)) for a in a_list]
    in_specs += [pl.BlockSpec(b.shape, lambda i: (0, 0)) for b in b_list]
    return pl.pallas_call(
        body, name=name, grid=(m // tm,), in_specs=in_specs,
        out_specs=pl.BlockSpec((tm, n), lambda i: (i, 0)),
        out_shape=jax.ShapeDtypeStruct((m, n), out_dtype),
        compiler_params=pltpu.CompilerParams(dimension_semantics=("arbitrary",), vmem_limit_bytes=VMEM_LIMIT),
    )(*a_list, *b_list)


def matmul_fused(name, fn, row_ins, consts, weight, pieces, mode, extra_outs, out_dtype=BF16, tm=256, acc_outs=(),
                 epilogue=None, epi_outs=()):
    row_ins = [tuple(e) + (0,) * (4 - len(e)) for e in row_ins]
    t_rows = row_ins[0][0].shape[0]
    tm = _tile(t_rows, tm)
    several = isinstance(weight, (list, tuple))
    weights = list(weight) if several else [weight]
    n_in = len(row_ins) + len(consts)
    n_w = len(weights)
    n_parts = len(pieces)
    n_mm = len(epi_outs) if epilogue is not None else (n_parts if several else 1)
    n_row_out = n_mm + len(extra_outs)

    def width(w_arr):
        return w_arr.shape[1] if mode == "nn" else w_arr.shape[0]

    def body(*refs):
        w_hbms = refs[n_in:n_in + n_w]
        outs = refs[n_in + n_w:n_in + n_w + n_row_out + len(acc_outs)]
        w_refs, sems = refs[-n_w - 1:-1], refs[-1]

        @pl.when(pl.program_id(0) == 0)
        def _():
            cps = [pltpu.make_async_copy(w_hbms[k], w_refs[k], sems.at[k]) for k in range(n_w)]
            for cp in cps:
                cp.start()
            for cp in cps:
                cp.wait()
            for k in range(len(acc_outs)):
                outs[n_row_out + k][...] = jnp.zeros_like(outs[n_row_out + k])

        vals = [r[...].astype(F32) for r in refs[:n_in]]
        res = fn(*vals)
        products = []
        for a_p, piece in zip(res[:n_parts], pieces):
            w_ref, (k0, k1) = (w_refs[piece[0]], piece[1:]) if several else (w_refs[0], piece)
            if mode == "nn":
                products.append(jnp.dot(_bf(a_p), w_ref[pl.ds(k0, k1 - k0), :], preferred_element_type=F32))
            else:
                products.append(lax.dot_general(_bf(a_p), w_ref[:, pl.ds(k0, k1 - k0)], NT_DIMS,
                                                preferred_element_type=F32))
        if not several:
            products = [functools.reduce(lambda u, w: u + w, products)]
        if epilogue is not None:
            products = epilogue(products, vals)
        for p, val in enumerate(products):
            outs[p][...] = val.astype(outs[p].dtype)
        for o_ref, val in zip(outs[n_mm:n_row_out], res[n_parts:]):
            o_ref[...] = val.astype(o_ref.dtype)
        for k in range(len(acc_outs)):
            outs[n_row_out + k][...] += res[n_parts + len(extra_outs) + k]

    in_specs = [pl.BlockSpec((tm, w), functools.partial(lambda i, j, ro: (ro(i) if callable(ro) else i + ro, j),
                                                        j=j, ro=ro)) for (_, w, j, ro) in row_ins]
    in_specs += [pl.BlockSpec(c.shape, lambda i: (0, 0)) for c in consts]
    in_specs += [pl.BlockSpec(memory_space=pl.ANY)] * n_w
    if epilogue is not None:
        widths = list(epi_outs) + list(extra_outs)
    elif several:
        widths = [(width(weights[piece[0]]), out_dtype) for piece in pieces] + list(extra_outs)
    else:
        widths = [(width(weights[0]), out_dtype)] + list(extra_outs)
    return pl.pallas_call(
        body, name=name, grid=(t_rows // tm,), in_specs=in_specs,
        out_specs=[pl.BlockSpec((tm, w), lambda i: (i, 0)) for (w, _) in widths]
        + [pl.BlockSpec(s, lambda i: (0, 0)) for s in acc_outs],
        out_shape=[jax.ShapeDtypeStruct((t_rows, w), dt) for (w, dt) in widths]
        + [jax.ShapeDtypeStruct(s, F32) for s in acc_outs],
        scratch_shapes=[pltpu.VMEM(w_arr.shape, w_arr.dtype) for w_arr in weights] + [pltpu.SemaphoreType.DMA((n_w,))],
        compiler_params=pltpu.CompilerParams(dimension_semantics=("arbitrary",), vmem_limit_bytes=VMEM_LIMIT),
    )(*[e[0] for e in row_ins], *consts, *weights)


def _rms(x, g):
    r = lax.rsqrt(jnp.mean(x * x, axis=-1, keepdims=True) + NORM_EPS)
    return x * r * g


def _rms_bwd(x, g, dy):
    r = lax.rsqrt(jnp.mean(x * x, axis=-1, keepdims=True) + NORM_EPS)
    xh = x * r
    dyg = dy * g
    dx = r * (dyg - xh * jnp.mean(dyg * xh, axis=-1, keepdims=True))
    return dx, jnp.sum(dy * xh, axis=0, keepdims=True)


def _silu(x):
    return x * _sigmoid(x)


def _silu_grad(x):
    s = _sigmoid(x)
    return s * (1.0 + x * (1.0 - s))


def _rope(xs, cos, s_up, s_dn):
    return xs * cos + pltpu.roll(xs, ROPE_HALF, 1) * s_up + pltpu.roll(xs, HEAD - ROPE_HALF, 1) * s_dn


def _rope_bwd(dy, cos, s_up, s_dn):
    return dy * cos + pltpu.roll(dy * s_up, HEAD - ROPE_HALF, 1) + pltpu.roll(dy * s_dn, ROPE_HALF, 1)


HG_SUB = 128
HG_LEVELS = 7
HG_E_ROWS = (HG_LEVELS + 1) * HG_SUB
HG_BWD_GROUP = 6
TN_DIMS = (((0,), (0,)), ((), ()))
NT_DIMS = (((1,), (1,)), ((), ()))


def _hg_constants():
    import numpy as np
    n = HG_SUB
    r = np.arange(n)[:, None]
    c = np.arange(n)[None, :]
    cs, ps = [], []
    for lvl in range(HG_LEVELS):
        m = (n // 2) >> lvl
        upper = (r % (2 * m)) >= m
        mid = (r // (2 * m)) * (2 * m) + m - 1
        cs.append(np.where(upper, (c > mid) & (c <= r), (c > r) & (c <= mid)))
        ps.append(((r // (2 * m)) == (c // (2 * m))) & upper & ((c % (2 * m)) < m))
    cs.append(c <= r)
    cs.append(np.ones((8, n), bool))
    cstack = np.concatenate(cs, 0).astype(np.float32)
    pstack = np.concatenate(ps, 0).astype(np.float32)
    pstack_t = np.concatenate([p.T for p in ps], 0).astype(np.float32)
    return (jnp.asarray(cstack, BF16), jnp.asarray(cstack[:HG_E_ROWS].T, BF16), jnp.asarray(pstack, F32),
            jnp.asarray(pstack_t, F32))


def _split_dot(c_bf, x):
    hi = _bf(x)
    lo = _bf(x - hi.astype(F32))
    r2 = jnp.dot(c_bf, jnp.concatenate([hi, lo], axis=1), preferred_element_type=F32)
    return r2[:, :HEAD] + r2[:, HEAD:]


def _hg_gates(hq, hf, lb):
    sq = _sigmoid(hq)
    sg = _sigmoid(hf)
    fg = lb + (1.0 - lb) * sg
    return sq, hq * sq, sg, fg, 1.0 - fg, jnp.log(fg)


def hgrn_fwd(proj_main, lb, consts, bl, lp, d):
    nh = d // HEAD
    rows_blk = _tile(lp, 768, SEQ_BLOCK)
    nb = lp // rows_blk
    spb = rows_blk // HG_SUB
    cstack, _, pstack, _ = consts

    def body(hq_ref, hf_ref, hi_ref, lb_ref, c_ref, p_ref, o_ref, st_ref, a_ref, s_ref):
        j = pl.program_id(2)

        @pl.when(j == 0)
        def _():
            s_ref[...] = jnp.zeros_like(s_ref)

        lbv = lb_ref[...]
        cs = c_ref[...]
        rows = [pl.ds(s * HG_SUB, HG_SUB) for s in range(spb)]
        gates = [_hg_gates(hq_ref[r, :].astype(F32), hf_ref[r, :].astype(F32), lbv) for r in rows]
        qs, ks = [g_[1] for g_ in gates], [g_[4] for g_ in gates]
        vs = [hi_ref[r, :].astype(F32) for r in rows]
        es = [_split_dot(cs, g_[5]) for g_ in gates]
        a_acc = [jnp.zeros((HG_SUB, HG_SUB), F32) for _ in rows]
        for lvl in range(HG_LEVELS):
            for s in range(spb):
                x = jnp.exp(es[s][lvl * HG_SUB:(lvl + 1) * HG_SUB])
                a_acc[s] = a_acc[s] + p_ref[pl.ds(lvl * HG_SUB, HG_SUB), :] * lax.dot_general(
                    _bf(qs[s] * x), _bf(ks[s] * x), NT_DIMS, preferred_element_type=F32)
        o_intra, qbs, kds, e_lasts = [], [], [], []
        for s in range(spb):
            a_bf = _bf(a_acc[s])
            a_ref[0, 0, s] = a_bf
            bc = es[s][HG_LEVELS * HG_SUB:HG_E_ROWS]
            b_last = jnp.tile(es[s][HG_E_ROWS:], (HG_SUB // 8, 1))
            o_intra.append(jnp.dot(a_bf, _bf(vs[s]), preferred_element_type=F32)
                           + jnp.sum(qs[s] * ks[s], axis=1, keepdims=True) * vs[s])
            qbs.append(_bf(qs[s] * jnp.exp(bc)))
            kds.append(_bf(ks[s] * jnp.exp(b_last - bc)))
            e_lasts.append(jnp.exp(b_last))
        st = s_ref[...]
        for s in range(spb):
            st_ref[0, 0, s] = st
            o_ref[rows[s], :] = (o_intra[s] + lax.dot_general(qbs[s], _bf(st), NT_DIMS, preferred_element_type=F32)
                                 ).astype(o_ref.dtype)
            st = st * e_lasts[s] + lax.dot_general(_bf(vs[s]), kds[s], TN_DIMS, preferred_element_type=F32)
        s_ref[...] = st

    def colspec(off):
        return pl.BlockSpec((rows_blk, HEAD), functools.partial(lambda h, b, j, off: (b * nb + j, off + h), off=off))

    whole = lambda arr: pl.BlockSpec(arr.shape, lambda h, b, j: (0, 0))
    return pl.pallas_call(
        body, name="hgrn_fwd", grid=(nh, bl, nb),
        in_specs=[colspec(0), colspec(nh), colspec(2 * nh), pl.BlockSpec((1, HEAD), lambda h, b, j: (0, h)),
                  whole(cstack), whole(pstack)],
        out_specs=[pl.BlockSpec((rows_blk, HEAD), lambda h, b, j: (b * nb + j, h)),
                   pl.BlockSpec((1, 1, spb, HEAD, HEAD), lambda h, b, j: (b, h, j, 0, 0)),
                   pl.BlockSpec((1, 1, spb, HG_SUB, HG_SUB), lambda h, b, j: (b, h, j, 0, 0))],
        out_shape=[jax.ShapeDtypeStruct((bl * lp, d), BF16),
                   jax.ShapeDtypeStruct((bl, nh, lp // HG_SUB, HEAD, HEAD), F32),
                   jax.ShapeDtypeStruct((bl, nh, lp // HG_SUB, HG_SUB, HG_SUB), BF16)],
        scratch_shapes=[pltpu.VMEM((HEAD, HEAD), F32)],
        compiler_params=pltpu.CompilerParams(dimension_semantics=("arbitrary", "arbitrary", "arbitrary")),
    )(proj_main, proj_main, proj_main, lb, cstack, pstack)


def hgrn_bwd(proj_main, lb, consts, states, a_mats, do_scan, bl, lp, d):
    nh = d // HEAD
    rows_blk = _tile(lp, 768, SEQ_BLOCK)
    nb = lp // rows_blk
    spb = rows_blk // HG_SUB
    cstack, cstack_t = consts[0], consts[1]
    pstack, pstack_t = _bf(consts[2]), _bf(consts[3])

    def body(hq_ref, hf_ref, hi_ref, lb_ref, c_ref, ct_ref, p_ref, pt_ref, st_ref, a_ref, do_ref,
             dq_ref, df_ref, di_ref, dlb_ref, ds_ref):
        b_id, j = pl.program_id(1), pl.program_id(2)
        blk = nb - 1 - j

        @pl.when(j == 0)
        def _():
            ds_ref[...] = jnp.zeros_like(ds_ref)

        @pl.when((j == 0) & (b_id == 0))
        def _():
            dlb_ref[...] = jnp.zeros_like(dlb_ref)

        lbv = lb_ref[...]
        cs = c_ref[...]
        cst = ct_ref[...]

        dlb = jnp.zeros((1, HEAD), F32)
        for first in reversed(range(0, spb, HG_BWD_GROUP)):
            dlb = dlb + _hg_group_bwd(list(range(first, min(first + HG_BWD_GROUP, spb))), lbv, cs, cst, hq_ref,
                                      hf_ref, hi_ref, st_ref, a_ref, do_ref, p_ref, pt_ref, dq_ref, df_ref, di_ref,
                                      ds_ref)
        dlb_ref[...] += dlb

    def _hg_group_bwd(ids, lbv, cs, cst, hq_ref, hf_ref, hi_ref, st_ref, a_ref, do_ref, p_ref, pt_ref, dq_ref,
                      df_ref, di_ref, ds_ref):
        rng = range(len(ids))
        rows = [pl.ds(s * HG_SUB, HG_SUB) for s in ids]
        hqs = [hq_ref[r, :].astype(F32) for r in rows]
        gates = [_hg_gates(hqs[s], hf_ref[rows[s], :].astype(F32), lbv) for s in rng]
        sqs, qs, sgs, fgs, ks = ([g_[i] for g_ in gates] for i in range(5))
        vs = [hi_ref[r, :].astype(F32) for r in rows]
        dos = [do_ref[r, :].astype(F32) for r in rows]
        sts = [st_ref[0, 0, s] for s in ids]
        es = [_split_dot(cs, g_[5]) for g_ in gates]
        bcs = [e[HG_LEVELS * HG_SUB:HG_E_ROWS] for e in es]
        b_lasts = [jnp.tile(e[HG_E_ROWS:], (HG_SUB // 8, 1)) for e in es]
        ebs = [jnp.exp(bc) for bc in bcs]
        qbs = [qs[s] * ebs[s] for s in rng]
        ers = [jnp.exp(b_lasts[s] - bcs[s]) for s in rng]
        kds = [ks[s] * ers[s] for s in rng]
        e_lasts = [jnp.exp(b) for b in b_lasts]
        do_bfs, v_bfs = [_bf(x) for x in dos], [_bf(x) for x in vs]
        das = [_bf(lax.dot_general(do_bfs[s], v_bfs[s], NT_DIMS, preferred_element_type=F32)) for s in rng]
        dats = [_bf(lax.dot_general(v_bfs[s], do_bfs[s], NT_DIMS, preferred_element_type=F32)) for s in rng]
        dqbs = [jnp.dot(do_bfs[s], _bf(sts[s]), preferred_element_type=F32) for s in rng]
        m_s = [lax.dot_general(do_bfs[s], _bf(qbs[s]), TN_DIMS, preferred_element_type=F32) for s in rng]
        dst_outs = [None] * len(ids)
        dst = ds_ref[...]
        for s in reversed(rng):
            dst_outs[s] = dst
            dst = dst * e_lasts[s] + m_s[s]
        ds_ref[...] = dst
        dst_bfs = [_bf(x) for x in dst_outs]
        d_diags = [jnp.sum(dos[s] * vs[s], axis=1, keepdims=True) for s in rng]
        dvs = [lax.dot_general(a_ref[0, 0, ids[s]], do_bfs[s], TN_DIMS, preferred_element_type=F32)
               + jnp.sum(qs[s] * ks[s], axis=1, keepdims=True) * dos[s]
               + lax.dot_general(_bf(kds[s]), dst_bfs[s], NT_DIMS, preferred_element_type=F32) for s in rng]
        dkds = [jnp.dot(v_bfs[s], dst_bfs[s], preferred_element_type=F32) for s in rng]
        dqs = [dqbs[s] * ebs[s] + d_diags[s] * ks[s] for s in rng]
        dks = [dkds[s] * ers[s] + d_diags[s] * qs[s] for s in rng]
        d_lasts = [jnp.sum(dst_outs[s] * sts[s] * e_lasts[s], axis=0, keepdims=True)
                   + jnp.sum(dkds[s] * kds[s], axis=0, keepdims=True) for s in rng]
        des = [[] for _ in rng]
        for lvl in range(HG_LEVELS):
            for s in rng:
                x = jnp.exp(es[s][lvl * HG_SUB:(lvl + 1) * HG_SUB])
                qh, kh = qs[s] * x, ks[s] * x
                dm = p_ref[pl.ds(lvl * HG_SUB, HG_SUB), :] * das[s]
                dmt = pt_ref[pl.ds(lvl * HG_SUB, HG_SUB), :] * dats[s]
                dqh = jnp.dot(dm, _bf(kh), preferred_element_type=F32)
                dkh = jnp.dot(dmt, _bf(qh), preferred_element_type=F32)
                dqs[s] = dqs[s] + dqh * x
                dks[s] = dks[s] + dkh * x
                des[s].append(dqh * qh + dkh * kh)
        dlb = jnp.zeros((1, HEAD), F32)
        for s in rng:
            des[s].append(dqbs[s] * qbs[s] - dkds[s] * kds[s])
            dg = _split_dot(cst, jnp.concatenate(des[s], axis=0)) + d_lasts[s]
            dfg = dg / fgs[s] - dks[s]
            dq_ref[rows[s], :] = (dqs[s] * (sqs[s] * (1.0 + hqs[s] * (1.0 - sqs[s])))).astype(dq_ref.dtype)
            df_ref[rows[s], :] = (dfg * (1.0 - lbv) * sgs[s] * (1.0 - sgs[s])).astype(df_ref.dtype)
            di_ref[rows[s], :] = dvs[s].astype(di_ref.dtype)
            dlb = dlb + jnp.sum(dfg * (1.0 - sgs[s]), axis=0, keepdims=True)
        return dlb

    def colspec(off):
        return pl.BlockSpec((rows_blk, HEAD),
                            functools.partial(lambda h, b, j, off: (b * nb + nb - 1 - j, off + h), off=off))

    whole = lambda arr: pl.BlockSpec(arr.shape, lambda h, b, j: (0, 0))
    mats = lambda: pl.BlockSpec((1, 1, spb, HEAD, HEAD), lambda h, b, j: (b, h, nb - 1 - j, 0, 0))
    t_rows = bl * lp
    return pl.pallas_call(
        body, name="hgrn_bwd", grid=(nh, bl, nb),
        in_specs=[colspec(0), colspec(nh), colspec(2 * nh), pl.BlockSpec((1, HEAD), lambda h, b, j: (0, h)),
                  whole(cstack), whole(cstack_t), whole(pstack), whole(pstack_t), mats(), mats(), colspec(0)],
        out_specs=[colspec(0), colspec(0), colspec(0), pl.BlockSpec((1, HEAD), lambda h, b, j: (0, h))],
        out_shape=[jax.ShapeDtypeStruct((t_rows, d), BF16)] * 3 + [jax.ShapeDtypeStruct((1, d), F32)],
        scratch_shapes=[pltpu.VMEM((HEAD, HEAD), F32)],
        compiler_params=pltpu.CompilerParams(dimension_semantics=("arbitrary", "arbitrary", "arbitrary")),
    )(proj_main, proj_main, proj_main, lb, cstack, cstack_t, pstack, pstack_t, states, a_mats, do_scan)


def _key_query_mask(key0, qry0, nk, nq_, causal):
    key = key0 + lax.broadcasted_iota(jnp.int32, (nk, 1), 0)
    if not causal:
        return key >= PAD_FRONT
    qry = qry0 + lax.broadcasted_iota(jnp.int32, (1, nq_), 1)
    return (key <= qry) & (key >= PAD_FRONT)


def _attn_tile(lp):
    return _tile(lp, ATTN_TILE_MAX, SEQ_BLOCK)


def attn_fwd_t(q_cat, k_cat, v_t, bl, lp, nm):
    tq = tk = _attn_tile(lp)
    nq = lp // tq
    hp = ATTN_HEADS_PER_STEP
    assert nm % hp == 0

    def body(q_ref, k_ref, vt_ref, o_ref, lse_ref, m_ref, l_ref, acc_ref):
        i = pl.program_id(2)
        m_ref[...] = jnp.full_like(m_ref, NEG)
        l_ref[...] = jnp.zeros_like(l_ref)
        acc_ref[...] = jnp.zeros_like(acc_ref)

        def step(c, mask):
            c0 = pl.multiple_of(c * tk, tk)
            for hh in range(hp):
                cols = pl.ds(hh * QK_PAD, QK_PAD)
                st = lax.dot_general(k_ref[pl.ds(c0, tk), cols], q_ref[:, cols], NT_DIMS,
                                     preferred_element_type=F32)
                if mask is not None:
                    st = jnp.where(_key_query_mask(c * tk, i * tq, tk, tq, mask == "causal"), st, NEG)
                m_old = m_ref[hh]
                m_new = jnp.maximum(m_old, jnp.max(st, axis=0, keepdims=True))
                alpha = jnp.exp(m_old - m_new)
                pt = jnp.exp(st - m_new)
                l_ref[hh] = alpha * l_ref[hh] + jnp.sum(pt, axis=0, keepdims=True)
                acc_ref[hh] = alpha * acc_ref[hh] + jnp.dot(vt_ref[0, hh, pl.ds(c, 1)][0], _bf(pt),
                                                            preferred_element_type=F32)
                m_ref[hh] = m_new

        def mid(c, carry):
            step(c, None)
            return carry

        @pl.when(i == 0)
        def _():
            step(0, "causal")

        @pl.when(i > 0)
        def _():
            step(0, "pad")
            lax.fori_loop(1, i, mid, 0)
            step(i, "causal")

        for hh in range(hp):
            o_ref[:, pl.ds(hh * HEAD, HEAD)] = jnp.transpose(acc_ref[hh] / l_ref[hh]).astype(o_ref.dtype)
            lse_ref[0, hh, 0] = m_ref[hh] + jnp.log(l_ref[hh])

    return pl.pallas_call(
        body, name="attn_fwd", grid=(bl, nm // hp, nq),
        in_specs=[pl.BlockSpec((tq, hp * QK_PAD), lambda b, h, i: (b * nq + i, h)),
                  pl.BlockSpec((lp, hp * QK_PAD), lambda b, h, i: (b, h)),
                  pl.BlockSpec((1, hp, nq, HEAD, tk), lambda b, h, i: (b, h, 0, 0, 0))],
        out_specs=[pl.BlockSpec((tq, hp * HEAD), lambda b, h, i: (b * nq + i, h)),
                   pl.BlockSpec((1, hp, 1, 1, tq), lambda b, h, i: (b, h, i, 0, 0))],
        out_shape=[jax.ShapeDtypeStruct((bl * lp, nm * HEAD), BF16),
                   jax.ShapeDtypeStruct((bl, nm, nq, 1, tq), F32)],
        scratch_shapes=[pltpu.VMEM((hp, 1, tq), F32), pltpu.VMEM((hp, 1, tq), F32), pltpu.VMEM((hp, HEAD, tq), F32)],
        compiler_params=pltpu.CompilerParams(dimension_semantics=("arbitrary", "arbitrary", "arbitrary")),
    )(q_cat, k_cat, v_t)


def attn_bwd_t(q_cat, k_cat, k_t, v, o, do, lse, bl, lp, nm):
    tq = tk = _attn_tile(lp)
    nq = lp // tq
    hp = ATTN_HEADS_PER_STEP
    assert nm % hp == 0

    def body(q_ref, k_ref, kt_ref, v_ref, o_ref, do_ref, lse_ref, dq_ref, dk_ref, dv_ref, dqt_ref, dka_ref, dva_ref):
        i = pl.program_id(2)

        @pl.when(i == 0)
        def _():
            dqt_ref[...] = jnp.zeros_like(dqt_ref)

        dka_ref[...] = jnp.zeros_like(dka_ref)
        dva_ref[...] = jnp.zeros_like(dva_ref)
        ones8 = jnp.ones((8, HEAD), BF16)

        def step(c, mask):
            c0 = pl.multiple_of(c * tq, tq)
            for hh in range(hp):
                qcols, vcols = pl.ds(hh * QK_PAD, QK_PAD), pl.ds(hh * HEAD, HEAD)
                qs = q_ref[pl.ds(c0, tq), qcols]
                dos = do_ref[pl.ds(c0, tq), vcols]
                prod = dos.astype(F32) * o_ref[pl.ds(c0, tq), vcols].astype(F32)
                hi = _bf(prod)
                lo = _bf(prod - hi.astype(F32))
                delta8 = (lax.dot_general(ones8, hi, NT_DIMS, preferred_element_type=F32)
                          + lax.dot_general(ones8, lo, NT_DIMS, preferred_element_type=F32))
                st = lax.dot_general(k_ref[:, qcols], qs, NT_DIMS, preferred_element_type=F32)
                pt = jnp.exp(st - lse_ref[0, hh, pl.ds(c, 1)][0])
                if mask is not None:
                    pt = jnp.where(_key_query_mask(i * tk, c * tq, tk, tq, mask == "causal"), pt, 0.0)
                dva_ref[hh] += jnp.dot(_bf(pt), dos, preferred_element_type=F32)
                dpt = lax.dot_general(v_ref[:, vcols], dos, NT_DIMS, preferred_element_type=F32)
                dst = _bf(pt * (dpt - jnp.tile(delta8, (tk // 8, 1))))
                dka_ref[hh] += jnp.dot(dst, qs, preferred_element_type=F32)
                dqt_ref[hh, pl.ds(c, 1)] += jnp.dot(kt_ref[0, hh, 0], dst, preferred_element_type=F32)[None]

        step(i, "causal")

        def rest_masked(c, carry):
            step(c, "pad")
            return carry

        def rest(c, carry):
            step(c, None)
            return carry

        @pl.when(i == 0)
        def _():
            lax.fori_loop(1, nq, rest_masked, 0)

        @pl.when(i > 0)
        def _():
            lax.fori_loop(i + 1, nq, rest, 0)

        for hh in range(hp):
            dk_ref[:, pl.ds(hh * QK_PAD, QK_PAD)] = dka_ref[hh].astype(dk_ref.dtype)
            dv_ref[:, pl.ds(hh * HEAD, HEAD)] = dva_ref[hh].astype(dv_ref.dtype)

        @pl.when(i == nq - 1)
        def _():
            for hh in range(hp):
                for c in range(nq):
                    dq_ref[pl.ds(c * tq, tq), pl.ds(hh * QK_PAD, QK_PAD)] = (
                        jnp.transpose(dqt_ref[hh, c])).astype(dq_ref.dtype)

    return pl.pallas_call(
        body, name="attn_bwd", grid=(bl, nm // hp, nq),
        in_specs=[pl.BlockSpec((lp, hp * QK_PAD), lambda b, h, i: (b, h)),
                  pl.BlockSpec((tk, hp * QK_PAD), lambda b, h, i: (b * nq + i, h)),
                  pl.BlockSpec((1, hp, 1, QK_PAD, tk), lambda b, h, i: (b, h, i, 0, 0)),
                  pl.BlockSpec((tk, hp * HEAD), lambda b, h, i: (b * nq + i, h)),
                  pl.BlockSpec((lp, hp * HEAD), lambda b, h, i: (b, h)),
                  pl.BlockSpec((lp, hp * HEAD), lambda b, h, i: (b, h)),
                  pl.BlockSpec((1, hp, nq, 1, tq), lambda b, h, i: (b, h, 0, 0, 0))],
        out_specs=[pl.BlockSpec((lp, hp * QK_PAD), lambda b, h, i: (b, h)),
                   pl.BlockSpec((tk, hp * QK_PAD), lambda b, h, i: (b * nq + i, h)),
                   pl.BlockSpec((tk, hp * HEAD), lambda b, h, i: (b * nq + i, h))],
        out_shape=[jax.ShapeDtypeStruct((bl * lp, nm * QK_PAD), BF16),
                   jax.ShapeDtypeStruct((bl * lp, nm * QK_PAD), BF16),
                   jax.ShapeDtypeStruct((bl * lp, nm * HEAD), BF16)],
        scratch_shapes=[pltpu.VMEM((hp, nq, QK_PAD, tq), F32), pltpu.VMEM((hp, tk, QK_PAD), F32),
                        pltpu.VMEM((hp, tk, HEAD), F32)],
        compiler_params=pltpu.CompilerParams(dimension_semantics=("arbitrary", "arbitrary", "arbitrary")),
    )(q_cat, k_cat, k_t, v, o, do, lse)


def _place():
    return lax.axis_index("x"), lax.axis_index("y"), lax.axis_index("c")


def gather_shards(packed):
    hbm = pl.BlockSpec(memory_space=pl.ANY)

    def body(src_ref, out_ref, send_sems, recv_sems, local_sem):
        x, y, c = _place()
        me = 2 * x + y
        chips = [(1 - x, y), (x, 1 - y), (1 - x, 1 - y)]
        local = pltpu.make_async_copy(src_ref, out_ref.at[me], local_sem)
        local.start()
        sends = []
        for k, (px, py) in enumerate(chips):
            cp = pltpu.make_async_remote_copy(src_ref=src_ref, dst_ref=out_ref.at[me], send_sem=send_sems.at[k],
                                              recv_sem=recv_sems.at[k], device_id=(px, py, c), device_id_type=MESH)
            cp.start()
            sends.append(cp)
        for k, (px, py) in enumerate(chips):
            pltpu.make_async_remote_copy(src_ref=src_ref, dst_ref=out_ref.at[2 * px + py], send_sem=send_sems.at[k],
                                         recv_sem=recv_sems.at[k], device_id=(px, py, c),
                                         device_id_type=MESH).wait_recv()
        for cp in sends:
            cp.wait_send()
        local.wait()

    return pl.pallas_call(
        body, name="gather_shards", in_specs=[hbm], out_specs=hbm,
        out_shape=jax.ShapeDtypeStruct((4,) + packed.shape, packed.dtype),
        scratch_shapes=[pltpu.SemaphoreType.DMA((3,)), pltpu.SemaphoreType.DMA((3,)), pltpu.SemaphoreType.DMA],
    )(packed)


def gather_small(small):
    hbm = pl.BlockSpec(memory_space=pl.ANY)

    def body(small_ref, all_ref, send_sems, recv_sems, local_sem):
        x, y, c = _place()
        me = 4 * x + 2 * y + c
        local = pltpu.make_async_copy(small_ref, all_ref.at[me], local_sem)
        local.start()
        others = [(x ^ ((r >> 2) & 1), y ^ ((r >> 1) & 1), c ^ (r & 1)) for r in range(1, 8)]
        sends = []
        for r, peer in enumerate(others):
            cp = pltpu.make_async_remote_copy(src_ref=small_ref, dst_ref=all_ref.at[me], send_sem=send_sems.at[r],
                                              recv_sem=recv_sems.at[r], device_id=peer, device_id_type=MESH)
            cp.start()
            sends.append(cp)
        for r, (px, py, pc) in enumerate(others):
            pltpu.make_async_remote_copy(src_ref=small_ref, dst_ref=all_ref.at[4 * px + 2 * py + pc],
                                         send_sem=send_sems.at[r], recv_sem=recv_sems.at[r],
                                         device_id=(px, py, pc), device_id_type=MESH).wait_recv()
        for cp in sends:
            cp.wait_send()
        local.wait()

    return pl.pallas_call(
        body, name="gather_small", in_specs=[hbm], out_specs=hbm,
        out_shape=jax.ShapeDtypeStruct((8,) + small.shape, small.dtype),
        scratch_shapes=[pltpu.SemaphoreType.DMA((7,)), pltpu.SemaphoreType.DMA((7,)), pltpu.SemaphoreType.DMA],
    )(small)


def swap_with_sibling(name, parts):
    n = len(parts)
    hbm = pl.BlockSpec(memory_space=pl.ANY)

    def body(*refs):
        x, y, c = _place()
        cps = [pltpu.make_async_remote_copy(src_ref=refs[j], dst_ref=refs[n + j], send_sem=refs[2 * n].at[j],
                                            recv_sem=refs[2 * n + 1].at[j], device_id=(x, y, 1 - c),
                                            device_id_type=MESH) for j in range(n)]
        for cp in cps:
            cp.start()
        for cp in cps:
            cp.wait()

    return pl.pallas_call(
        body, name=name, in_specs=[hbm] * n, out_specs=[hbm] * n,
        out_shape=[jax.ShapeDtypeStruct(p.shape, p.dtype) for p in parts],
        scratch_shapes=[pltpu.SemaphoreType.DMA((n,)), pltpu.SemaphoreType.DMA((n,))],
    )(*parts)


def _chips3():
    x, y, c = _place()
    return [(1 - x, y, c), (x, 1 - y, c), (1 - x, 1 - y, c)]


def _push_copies(src_refs, land_refs, send_sems, recv_sems, per_chip):
    x, y, _ = _place()
    cps = []
    for j, (src_ref, land_ref) in enumerate(zip(src_refs, land_refs)):
        for k, (px, py, pc) in enumerate(_chips3()):
            part = src_ref.at[2 * px + py] if per_chip else src_ref
            slot = k if per_chip else 2 * x + y
            cps.append(pltpu.make_async_remote_copy(
                src_ref=part, dst_ref=land_ref.at[slot], send_sem=send_sems.at[3 * j + k],
                recv_sem=recv_sems.at[3 * j + k], device_id=(px, py, pc), device_id_type=MESH))
    return cps


def push_start(name, srcs, per_chip):
    n = len(srcs)
    hbm = pl.BlockSpec(memory_space=pltpu.HBM)
    sem = pl.BlockSpec(memory_space=pltpu.SEMAPHORE)
    lands = [lax.empty((3 if per_chip else 4,) + s.shape[-2:], s.dtype) for s in srcs]

    def body(*refs):
        src_refs, land_refs = refs[:n], refs[n:2 * n]
        send_sems, recv_sems = refs[2 * n], refs[2 * n + 1]
        for cp in _push_copies(src_refs, land_refs, send_sems, recv_sems, per_chip):
            cp.start()
        refs[-1][...] = jnp.zeros_like(refs[-1])

    outs = pl.pallas_call(
        body, name=name,
        out_shape=(pltpu.SemaphoreType.DMA((3 * n,)), pltpu.SemaphoreType.DMA((3 * n,)),
                   *[pltpu.HBM(a.shape, a.dtype) for a in list(srcs) + lands], jax.ShapeDtypeStruct((8, HEAD), F32)),
        in_specs=(hbm,) * (2 * n),
        out_specs=(sem, sem) + (hbm,) * (2 * n) + (pl.BlockSpec(memory_space=pltpu.VMEM),),
        input_output_aliases={j: 2 + j for j in range(2 * n)},
        compiler_params=pltpu.CompilerParams(has_side_effects=pltpu.SideEffectType.DATAFLOW_SIDE_EFFECTING),
    )(*[pltpu.with_memory_space_constraint(a, pltpu.HBM) for a in list(srcs) + lands])
    return tuple(outs[:-1]), outs[-1]


def push_wait(name, handle, after, per_chip):
    send_sems, recv_sems = handle[0], handle[1]
    thru = handle[2:]
    n = len(thru) // 2
    hbm = pl.BlockSpec(memory_space=pltpu.HBM)
    sem = pl.BlockSpec(memory_space=pltpu.SEMAPHORE)

    def body(*refs):
        src_refs, land_refs = refs[:n], refs[n:2 * n]
        for cp in _push_copies(src_refs, land_refs, refs[2 * n], refs[2 * n + 1], per_chip):
            cp.wait_send()
            cp.wait_recv()

    outs = pl.pallas_call(
        body, name=name,
        out_shape=tuple(pltpu.HBM(a.shape, a.dtype) for a in thru),
        in_specs=(hbm,) * (2 * n) + (sem, sem, pl.BlockSpec(memory_space=pl.ANY)), out_specs=(hbm,) * (2 * n),
        input_output_aliases={j: j for j in range(2 * n)},
        compiler_params=pltpu.CompilerParams(has_side_effects=pltpu.SideEffectType.DATAFLOW_SIDE_EFFECTING),
    )(*thru, send_sems, recv_sems, after)
    return outs[:n], outs[n:]


def join_gathered(name, own, landed, my_chip):
    blocks = lax.dynamic_update_index_in_dim(landed, own, my_chip, 0)
    _, r, c = blocks.shape
    if name in COL_SHARDED:
        return blocks.transpose(1, 0, 2).reshape(r, 4 * c)
    return blocks.reshape(4 * r, c)


def adamw(name, w, g_parts, m, v):
    r, c = w.shape
    tr = r if r * c <= 65536 else _tile(r, 128, 8)
    ng = len(g_parts)

    def body(*refs):
        w_ref, m_ref, v_ref = refs[0], refs[1 + ng], refs[2 + ng]
        g_ref, d_ref, nm_ref, nv_ref = refs[3 + ng:]
        gv = refs[1][...]
        for k in range(1, ng):
            gv = gv + refs[1 + k][...]
        m_new = ADAM_B1 * m_ref[...] + (1.0 - ADAM_B1) * gv
        v_new = ADAM_B2 * v_ref[...] + (1.0 - ADAM_B2) * (gv * gv)
        m_hat = m_new / (1.0 - ADAM_B1 ** ADAM_STEP)
        v_hat = v_new / (1.0 - ADAM_B2 ** ADAM_STEP)
        g_ref[...] = gv
        d_ref[...] = -ADAM_LR * (m_hat / (jnp.sqrt(v_hat) + ADAM_EPS) + ADAM_WD * w_ref[...])
        nm_ref[...] = m_new
        nv_ref[...] = v_new

    spec = pl.BlockSpec((tr, c), lambda i: (i, 0))
    return pl.pallas_call(
        body, name=name, grid=(r // tr,), in_specs=[spec] * (3 + ng), out_specs=[spec] * 4,
        out_shape=[jax.ShapeDtypeStruct((r, c), F32)] * 4,
        compiler_params=pltpu.CompilerParams(dimension_semantics=("arbitrary",)),
    )(w, *g_parts, m, v)


def split_full(name, full, s):
    if name in COL_SHARDED:
        c = full.shape[1] // 4
        return full[:, s * c:(s + 1) * c]
    r = full.shape[0] // 4
    return full[s * r:(s + 1) * r]


def kernel(x, meta_tokens, w_in, b_gate, lb_logits, hg_norm_g, w_hg_o, q_a_norm_g, w_q_b, kv_a_norm_g, w_kv_b, w_mla_o, w_out, mix_pre_g, mix_post_g, ffn_pre_g, ffn_post_g, w_ffn_in, w_ffn_out, loss_target, m_meta_tokens, m_w_in, m_b_gate, m_lb_logits, m_hg_norm_g, m_w_hg_o, m_q_a_norm_g, m_w_q_b, m_kv_a_norm_g, m_w_kv_b, m_w_mla_o, m_w_out, m_mix_pre_g, m_mix_post_g, m_ffn_pre_g, m_ffn_post_g, m_w_ffn_in, m_w_ffn_out, v_meta_tokens, v_w_in, v_b_gate, v_lb_logits, v_hg_norm_g, v_w_hg_o, v_q_a_norm_g, v_w_q_b, v_kv_a_norm_g, v_w_kv_b, v_w_mla_o, v_w_out, v_mix_pre_g, v_mix_post_g, v_ffn_pre_g, v_ffn_post_g, v_w_ffn_in, v_w_ffn_out):
    wts = dict(meta_tokens=meta_tokens, w_in=w_in[0], b_gate=b_gate, lb_logits=lb_logits, hg_norm_g=hg_norm_g,
               w_hg_o=w_hg_o[0], q_a_norm_g=q_a_norm_g, w_q_b=w_q_b[0], kv_a_norm_g=kv_a_norm_g, w_kv_b=w_kv_b[0],
               w_mla_o=w_mla_o[0], w_out=w_out[0], mix_pre_g=mix_pre_g, mix_post_g=mix_post_g, ffn_pre_g=ffn_pre_g,
               ffn_post_g=ffn_post_g, w_ffn_in=w_ffn_in[0], w_ffn_out=w_ffn_out[0])
    mom_m = dict(meta_tokens=m_meta_tokens, w_in=m_w_in[0], b_gate=m_b_gate, lb_logits=m_lb_logits,
                 hg_norm_g=m_hg_norm_g, w_hg_o=m_w_hg_o[0], q_a_norm_g=m_q_a_norm_g, w_q_b=m_w_q_b[0],
                 kv_a_norm_g=m_kv_a_norm_g, w_kv_b=m_w_kv_b[0], w_mla_o=m_w_mla_o[0], w_out=m_w_out[0],
                 mix_pre_g=m_mix_pre_g, mix_post_g=m_mix_post_g, ffn_pre_g=m_ffn_pre_g, ffn_post_g=m_ffn_post_g,
                 w_ffn_in=m_w_ffn_in[0], w_ffn_out=m_w_ffn_out[0])
    mom_v = dict(meta_tokens=v_meta_tokens, w_in=v_w_in[0], b_gate=v_b_gate, lb_logits=v_lb_logits,
                 hg_norm_g=v_hg_norm_g, w_hg_o=v_w_hg_o[0], q_a_norm_g=v_q_a_norm_g, w_q_b=v_w_q_b[0],
                 kv_a_norm_g=v_kv_a_norm_g, w_kv_b=v_w_kv_b[0], w_mla_o=v_w_mla_o[0], w_out=v_w_out[0],
                 mix_pre_g=v_mix_pre_g, mix_post_g=v_mix_post_g, ffn_pre_g=v_ffn_pre_g, ffn_post_g=v_ffn_post_g,
                 w_ffn_in=v_w_ffn_in[0], w_ffn_out=v_w_ffn_out[0])

    bl, seq, d = x.shape
    lp = PAD_FRONT + N_META + seq
    t_rows = bl * lp
    nh = d // HEAD
    ql, kvl = wts["w_q_b"].shape[0], wts["w_kv_b"].shape[0]
    nm = (4 * wts["w_mla_o"].shape[0]) // HEAD
    ffn = 4 * wts["w_ffn_out"].shape[0]
    mla_w = ql + kvl + HEAD
    assert ql == kvl and ql % HEAD == 0 and seq % SEQ_BLOCK == 0 and d % HEAD == 0
    scale = (HEAD + ROPE) ** -0.5
    my_chip = 2 * lax.axis_index("x") + lax.axis_index("y")

    mcols = meta_tokens.shape[1]
    meta_all = gather_shards(meta_tokens)
    meta_full = jnp.concatenate([meta_all[s] for s in range(4)], axis=1)

    def start_gather(name, names, order_after):
        srcs = [_bf(wts[n]) for n in names]
        if order_after is not None:
            srcs[0] = srcs[0] + order_after[0, 0].astype(BF16)
        return push_start(name, srcs, per_chip=False)

    def finish_gather(name, names, started, after):
        owns, landed = push_wait(name, started[0], after, per_chip=False)
        return {n: join_gathered(n, own, land, my_chip) for n, own, land in zip(names, owns, landed)}

    rest_names = tuple(n for n in BIG if n != "w_in")
    my_c = lax.axis_index("c")
    w_in_bf = _bf(wts["w_in"])
    half = w_in_bf.shape[0] // 2
    own_half = (lax.dynamic_slice_in_dim(w_in_bf, my_c * half, half, axis=0)
                + (meta_all[0, :1, :1] * 0.0)[0, 0].astype(BF16))
    gather_1 = push_start("gather_w_in_start", [own_half], per_chip=False)
    gather_2 = start_gather("gather_rest_start", rest_names, gather_1[1])

    h0 = jnp.concatenate([jnp.zeros((bl, PAD_FRONT, d), F32), jnp.broadcast_to(meta_full[None], (bl, N_META, d)), x],
                         axis=1).reshape(t_rows, d)
    tiles_seq, tiles_real = lp // SEQ_BLOCK, seq // SEQ_BLOCK
    assert PAD_FRONT + N_META == SEQ_BLOCK

    def real_block(i):
        return (i // tiles_seq) * tiles_real + jnp.maximum(i % tiles_seq - 1, 0)

    meta_rows = jnp.broadcast_to(((jnp.arange(lp) >= PAD_FRONT) & (jnp.arange(lp) < PAD_FRONT + N_META)
                                  ).astype(F32)[:, None], (lp, HEAD))
    pos = (jnp.arange(lp, dtype=jnp.int32) - PAD_FRONT).astype(F32)
    inv_freq = 1.0 / (ROPE_THETA ** (jnp.arange(0, ROPE, 2, dtype=F32) / ROPE))
    ang = pos[:, None] * inv_freq[None, :]
    zeros32 = jnp.zeros((lp, ROPE_HALF), F32)
    zeros64 = jnp.zeros((lp, HEAD - ROPE), F32)
    t_cos = jnp.concatenate([jnp.cos(ang), jnp.cos(ang), zeros64], axis=1)
    t_up = jnp.concatenate([zeros32, jnp.sin(ang), zeros64], axis=1)
    t_dn = jnp.concatenate([-jnp.sin(ang), zeros32, zeros64], axis=1)
    real = jnp.broadcast_to((jnp.arange(lp) >= PAD_FRONT + N_META).astype(F32)[:, None], (lp, d))
    lb_soft = jax.nn.softmax(lb_logits.astype(F32), axis=0)
    lb = lb_soft[0:1]

    (u1,) = rowwise("norm_mix_pre", lambda h, g: _rms(h, g), [(h0, d, 0)], [], [mix_pre_g + gather_2[1][0, 0]],
                    [(d, BF16)])
    _, (fetched,) = push_wait("gather_w_in_wait", gather_1[0], u1, per_chip=False)
    (handed,) = swap_with_sibling("swap_w_in", [fetched])
    halves = jnp.stack([fetched, handed])
    remote = jnp.concatenate([lax.dynamic_index_in_dim(halves, my_c, 0, keepdims=False),
                              lax.dynamic_index_in_dim(halves, 1 - my_c, 0, keepdims=False)], axis=1)
    full = {"w_in": join_gathered("w_in", w_in_bf, remote, my_chip)}
    w_main = jnp.concatenate([full["w_in"][:, :4 * d], full["w_in"][:, -2 * d:]], axis=1)
    w_mla = jnp.pad(full["w_in"][:, 4 * d:4 * d + ql + kvl + ROPE], ((0, 0), (0, HEAD - ROPE)))
    proj_main = matmul("proj_main", u1, w_main, "nn", out_dtype=BF16)
    proj_mla = matmul("proj_mla", u1, w_mla, "nn", out_dtype=BF16)
    hg_consts = _hg_constants()
    o_scan, states, a_mats = hgrn_fwd(proj_main, lb, hg_consts, bl, lp, d)

    full.update(finish_gather("gather_rest_wait", rest_names, gather_2, o_scan))
    w_qb = jnp.pad(full["w_q_b"].reshape(ql, nm, HEAD + ROPE), ((0, 0), (0, 0), (0, QK_PAD - HEAD - ROPE))
                   ).reshape(ql, nm * QK_PAD)
    w_kvb = full["w_kv_b"]

    def hg_out_fn(o, hg, g):
        ov = jnp.concatenate([_rms(o[:, h * HEAD:(h + 1) * HEAD], g) for h in range(nh)], axis=1) * _silu(hg)
        return ov, ov

    y_a, o_hg = matmul_fused("hgrn_out_y_a", hg_out_fn, [(o_scan, d, 0), (proj_main, d, 3)], [hg_norm_g],
                             _bf(full["w_hg_o"]), [(0, d)], "nn", [(d, BF16)], tm=512)

    def seq_tile(i):
        return i % tiles_seq

    tables = [(t_cos, HEAD, 0, seq_tile), (t_up, HEAD, 0, seq_tile), (t_dn, HEAD, 0, seq_tile)]

    def q_norm_fn(cq, cos, s_up, s_dn, g):
        cn = _rms(cq, g)
        return cn, cn

    def q_rope_fn(products, vals):
        qf = products[0] * scale
        cos, s_up, s_dn = vals[1:4]
        qs = []
        for h in range(nm):
            qs += [qf[:, h * QK_PAD:h * QK_PAD + HEAD], _rope(qf[:, h * QK_PAD + HEAD:(h + 1) * QK_PAD], cos, s_up, s_dn)]
        return [jnp.concatenate(qs, axis=1)]

    q_cat, qn = matmul_fused("q_norm_up_rope", q_norm_fn, [(proj_mla, ql, 0)] + tables, [q_a_norm_g], w_qb,
                             [(0, ql)], "nn", [(ql, BF16)], epilogue=q_rope_fn, epi_outs=[(nm * QK_PAD, BF16)])

    def kv_norm_fn(ckv, kpe, cos, s_up, s_dn, g):
        cn = _rms(ckv, g)
        return cn, cn

    def kv_rope_fn(products, vals):
        kvf = products[0]
        kpe_r = _rope(vals[1], *vals[2:5])
        ks, vs = [], []
        for h in range(nm):
            ks += [kvf[:, h * QK_PAD:h * QK_PAD + HEAD], kpe_r]
            vs += [kvf[:, h * QK_PAD + HEAD:(h + 1) * QK_PAD]]
        return [jnp.concatenate(ks, axis=1), jnp.concatenate(vs, axis=1)]

    kpe_blk = (ql + kvl) // HEAD
    k_cat, v_att, kvn = matmul_fused("kv_norm_up_rope", kv_norm_fn,
                                     [(proj_mla, kvl, 1), (proj_mla, HEAD, kpe_blk)] + tables, [kv_a_norm_g], w_kvb,
                                     [(0, kvl)], "nn", [(kvl, BF16)], epilogue=kv_rope_fn,
                                     epi_outs=[(nm * QK_PAD, BF16), (nm * HEAD, BF16)])
    at = _attn_tile(lp)
    v_t = v_att.reshape(bl, lp // at, at, nm, HEAD).transpose(0, 3, 1, 4, 2)
    k_t = k_cat.reshape(bl, lp // at, at, nm, QK_PAD).transpose(0, 3, 1, 4, 2)
    o_mla, lse = attn_fwd_t(q_cat, k_cat, v_t, bl, lp, nm)
    y_b = matmul("y_b", o_mla, _bf(full["w_mla_o"]), "nn", out_dtype=BF16)

    def gate_fn(ya, yb, ga, gb, bias):
        zv = _sigmoid(ga + bias[:, :d]) * ya + _sigmoid(gb + bias[:, d:]) * yb
        return zv, zv

    mixed, z = matmul_fused("gate_mix_out", gate_fn,
                            [(y_a, d, 0), (y_b, d, 0), (proj_main, d, 4), (proj_main, d, 5)], [b_gate],
                            _bf(full["w_out"]), [(0, d)], "nn", [(d, BF16)], tm=512)

    def mid_fn(h, mx, g_post, g_pre):
        h1v = h + _rms(mx, g_post)
        u2v = _rms(h1v, g_pre)
        return u2v, h1v, u2v

    gu, h1, u2 = matmul_fused("norm_mid_ffn_in", mid_fn, [(h0, d, 0), (mixed, d, 0)], [mix_post_g, ffn_pre_g],
                              _bf(full["w_ffn_in"]), [(0, d)], "nn", [(d, F32), (d, BF16)])
    def swiglu_fn(gt, up):
        a = _silu(gt) * up
        return a, a

    f_out, act = matmul_fused("swiglu_ffn_out", swiglu_fn, [(gu, ffn, 0), (gu, ffn, 1)], [],
                              _bf(full["w_ffn_out"]), [(0, ffn)], "nn", [(ffn, BF16)])

    def loss_fn(h1v, fv, tg, realv, g_post):
        h2 = h1v + _rms(fv, g_post)
        diff = (h2 - tg) * realv
        part = jnp.broadcast_to(0.5 * jnp.sum(diff * diff, keepdims=True) / d, (1, HEAD))
        dy = diff / d
        df, dg = _rms_bwd(fv, g_post, dy)
        return df, dy, df, part, dg

    d_act, dy, df, loss_part, g_ffn_post = matmul_fused(
        "loss_head_d_act", loss_fn,
        [(h1, d, 0), (f_out, d, 0), (loss_target.reshape(bl * seq, d), d, 0, real_block), (real, d, 0, seq_tile)],
        [ffn_post_g], _bf(full["w_ffn_out"]), [(0, d)], "nt", [(d, BF16), (d, BF16)],
        acc_outs=[(1, HEAD), (1, d)])
    grads = {}
    grads["w_ffn_out"] = matmul("gw_ffn_out", act, df, "tn")

    def swiglu_bwd_fn(gt, up, da):
        dgt, dup = da * up * _silu_grad(gt), da * _silu(gt)
        return dgt, dup, jnp.concatenate([dgt, dup], axis=1)

    du2, dgu = matmul_fused("swiglu_bwd_d_u2", swiglu_bwd_fn, [(gu, ffn, 0), (gu, ffn, 1), (d_act, ffn, 0)], [],
                            _bf(full["w_ffn_in"]), [(0, ffn), (ffn, 2 * ffn)], "nt", [(2 * ffn, BF16)])
    grads["w_ffn_in"] = matmul("gw_ffn_in", u2, dgu, "tn")

    def mid_bwd_fn(dyv, h1v, du2v, mx, g_pre, g_post):
        dx, dg_pre = _rms_bwd(h1v, g_pre, du2v)
        dh1 = dyv + dx
        dmx, dg_post = _rms_bwd(mx, g_post, dh1)
        return dmx, dh1, dmx, dg_pre, dg_post

    dz, dh1, dmixed, g_ffn_pre, g_mix_post = matmul_fused(
        "norm_mid_bwd_d_z", mid_bwd_fn, [(dy, d, 0), (h1, d, 0), (du2, d, 0), (mixed, d, 0)],
        [ffn_pre_g, mix_post_g], _bf(full["w_out"]), [(0, d)], "nt", [(d, BF16), (d, BF16)], tm=512,
        acc_outs=[(1, d), (1, d)])
    grads["w_out"] = matmul("gw_out", z, dmixed, "tn")

    def gate_bwd_fn(dzv, ya, yb, ga, gb, bias):
        sa, sb = _sigmoid(ga + bias[:, :d]), _sigmoid(gb + bias[:, d:])
        dga = dzv * ya * sa * (1.0 - sa)
        dgb = dzv * yb * sb * (1.0 - sb)
        dgates = jnp.concatenate([dga, dgb], axis=1)
        dya, dyb = dzv * sa, dzv * sb
        return dya, dyb, dya, dyb, dgates, jnp.sum(dgates, axis=0, keepdims=True)

    do_hg, do_mla, dy_a, dy_b, dgates, g_b_gate = matmul_fused(
        "gate_mix_bwd_d_o", gate_bwd_fn,
        [(dz, d, 0), (y_a, d, 0), (y_b, d, 0), (proj_main, d, 4), (proj_main, d, 5)], [b_gate],
        [_bf(full["w_hg_o"]), _bf(full["w_mla_o"])], [(0, 0, d), (1, 0, d)], "nt",
        [(d, BF16), (d, BF16), (2 * d, BF16)], acc_outs=[(1, 2 * d)])
    grads["w_hg_o"] = matmul("gw_hg_o", o_hg, dy_a, "tn")
    grads["w_mla_o"] = matmul("gw_mla_o", o_mla, dy_b, "tn")

    early = ("w_hg_o", "w_mla_o", "w_out", "w_ffn_in", "w_ffn_out")
    late = ("w_in", "w_q_b", "w_kv_b")

    def start_grads(name, names):
        sends = [_bf(jnp.stack([split_full(n, grads[n], s) for s in range(4)])) for n in names]
        mines = []
        for n in names:
            r, c = wts[n].shape
            axis, size = (1, c) if n in COL_SHARDED else (0, r)
            mines.append(lax.dynamic_slice_in_dim(grads[n], my_chip * size, size, axis=axis))
        handle, token = push_start(name, sends, per_chip=True)
        return handle, token, mines

    def finish_grads(tag, names, started, after):
        handle, _, mines = started
        _, landed = push_wait(f"grads_{tag}_wait", handle, after, per_chip=True)
        parts = []
        for n, mine, land in zip(names, mines, landed):
            r, c = mine.shape
            tr = _tile(r, 256, 16)
            land2 = land.reshape(3 * r, c)
            parts.append(rowwise(f"sum_chips_{n}", lambda a, r0, r1, r2: a + r0 + r1 + r2,
                                 [(mine, c, 0)] + [(land2, c, 0, k * (r // tr)) for k in range(3)],
                                 [], [], [(c, F32)], tm=tr)[0])
        sibs = swap_with_sibling(f"swap_{tag}", parts)
        return {n: [p, s] for n, p, s in zip(names, parts, sibs)}

    grads_early = start_grads("grads_early_start", early)
    token_a = grads_early[1]

    def hg_out_bwd_fn(do, o, hg, g):
        sg = _silu(hg)
        dn = do * sg
        dos, dgs, ons = [], 0.0, []
        for h in range(nh):
            sl = slice(h * HEAD, (h + 1) * HEAD)
            dx, dg = _rms_bwd(o[:, sl], g, dn[:, sl])
            dos.append(dx)
            dgs = dgs + dg
            ons.append(_rms(o[:, sl], g))
        dhg = do * jnp.concatenate(ons, axis=1) * _silu_grad(hg)
        return jnp.concatenate(dos, axis=1), dhg, dgs

    do_scan, dhg, g_hg_norm = rowwise("hgrn_out_bwd", hg_out_bwd_fn, [(do_hg, d, 0), (o_scan, d, 0), (proj_main, d, 3)],
                                      [], [hg_norm_g], [(d, BF16), (d, BF16)], [(1, HEAD)])
    dhq, dhf, dhi, g_lb = hgrn_bwd(proj_main, lb + token_a[0, 0], hg_consts, states, a_mats, do_scan, bl, lp, d)

    dq_cat, dk_cat, dv_att = attn_bwd_t(q_cat, k_cat, k_t, v_att, o_mla, do_mla, lse, bl, lp, nm)

    def mla_prep_bwd_fn(dqc, dkc, dvv, cos, s_up, s_dn):
        dqc = dqc * scale
        dqs, dkvs, dkpe = [], [], 0.0
        for h in range(nm):
            dqs += [dqc[:, h * QK_PAD:h * QK_PAD + HEAD],
                    _rope_bwd(dqc[:, h * QK_PAD + HEAD:(h + 1) * QK_PAD], cos, s_up, s_dn)]
            dkvs += [dkc[:, h * QK_PAD:h * QK_PAD + HEAD], dvv[:, h * HEAD:(h + 1) * HEAD]]
            dkpe = dkpe + dkc[:, h * QK_PAD + HEAD:(h + 1) * QK_PAD]
        dqf, dkvf = jnp.concatenate(dqs, axis=1), jnp.concatenate(dkvs, axis=1)
        return dqf, dkvf, dqf, dkvf, _rope_bwd(dkpe, cos, s_up, s_dn)

    dqn, dkvn, dq_full, dkv_full, dkpe = matmul_fused(
        "mla_prep_bwd_d_norms", mla_prep_bwd_fn,
        [(dq_cat, nm * QK_PAD, 0), (dk_cat, nm * QK_PAD, 0), (dv_att, nm * HEAD, 0),
         (t_cos, HEAD, 0, seq_tile), (t_up, HEAD, 0, seq_tile), (t_dn, HEAD, 0, seq_tile)], [],
        [w_qb, w_kvb], [(0, 0, nm * QK_PAD), (1, 0, nm * QK_PAD)], "nt",
        [(nm * QK_PAD, BF16), (nm * QK_PAD, BF16), (HEAD, F32)])
    g_wqb = matmul("gw_q_b", qn, dq_full, "tn")
    grads["w_q_b"] = g_wqb.reshape(ql, nm, QK_PAD)[:, :, :HEAD + ROPE].reshape(ql, nm * (HEAD + ROPE))
    grads["w_kv_b"] = matmul("gw_kv_b", kvn, dkv_full, "tn")

    def mla_norms_bwd_fn(dqnv, dkvnv, cq, ckv, dkpev, gq, gk):
        dcq, dgq = _rms_bwd(cq, gq, dqnv)
        dckv, dgk = _rms_bwd(ckv, gk, dkvnv)
        return jnp.concatenate([dcq, dckv, dkpev], axis=1), dgq, dgk

    dmla, g_q_norm, g_kv_norm = rowwise("mla_norms_bwd", mla_norms_bwd_fn,
                                        [(dqn, ql, 0), (dkvn, kvl, 0), (proj_mla, ql, 0), (proj_mla, kvl, 1),
                                         (dkpe, HEAD, 0)], [], [q_a_norm_g, kv_a_norm_g],
                                        [(mla_w, BF16)], [(1, ql), (1, kvl)])

    d_pieces = [dhq, dhf, dhi, dhg, dgates, dmla]
    gw_parts = [matmul(f"gw_in_{k}", u1, dp, "tn") for k, dp in enumerate(d_pieces)]
    grads["w_in"] = jnp.concatenate(gw_parts[:4] + [gw_parts[5][:, :ql + kvl + ROPE], gw_parts[4]], axis=1)
    grads_late = start_grads("grads_late_start", late)
    w_mla_after = w_mla + grads_late[1][0, 0].astype(BF16)
    w_pieces = [w_main[:, 0:d], w_main[:, d:2 * d], w_main[:, 2 * d:3 * d], w_main[:, 3 * d:4 * d],
                w_main[:, 4 * d:6 * d], w_mla_after]
    du1 = matmul("d_u1", d_pieces, w_pieces, "nt", out_dtype=BF16)

    def first_bwd_fn(dh1v, h, du1v, is_meta, g):
        dx, dg = _rms_bwd(h, g, du1v)
        dh0v = dh1v + dx
        return dh0v, dg, dh0v * jnp.tile(is_meta, (1, d // HEAD))

    grad_x, g_mix_pre, meta_tile = rowwise(
        "norm_mix_pre_bwd", first_bwd_fn, [(dh1, d, 0), (h0, d, 0), (du1, d, 0)], [meta_rows], [mix_pre_g],
        [(d, F32, bl * seq, real_block)], [(1, d), (SEQ_BLOCK, d)])
    grad_x = grad_x.reshape(bl, seq, d)

    g_parts = finish_grads("early", early, grads_early, g_mix_pre)
    updates = {}

    def update(n, parts):
        w2 = wts[n].reshape(-1, wts[n].shape[-1])
        updates[n] = adamw("adamw_" + n, w2, [p.reshape(w2.shape) for p in parts], mom_m[n].reshape(w2.shape),
                           mom_v[n].reshape(w2.shape))

    for n in early:
        update(n, g_parts[n])
    g_parts = finish_grads("late", late, grads_late, updates[early[-1]][0])
    for n in late:
        update(n, g_parts[n])
    p0 = lb_soft[0:1]
    g_lb_logits = jnp.concatenate([g_lb * p0 * (1.0 - p0), -g_lb * p0 * (1.0 - p0)], axis=0)

    def row_of(vec):
        return vec.reshape(-1, d) if vec.size >= d else jnp.pad(vec.reshape(1, -1), ((0, 0), (0, d - vec.size)))

    small_parts = dict(b_gate=g_b_gate, lb_logits=g_lb_logits, hg_norm_g=g_hg_norm, q_a_norm_g=g_q_norm,
                       kv_a_norm_g=g_kv_norm, mix_pre_g=g_mix_pre, mix_post_g=g_mix_post, ffn_pre_g=g_ffn_pre,
                       ffn_post_g=g_ffn_post)
    g_meta = meta_tile[PAD_FRONT:PAD_FRONT + N_META]
    small_rows = [row_of(small_parts[n]) for n in SMALL] + [row_of(g_meta)]
    n_small = sum(r.shape[0] for r in small_rows)
    small = jnp.pad(jnp.concatenate(small_rows, axis=0), ((0, -(-n_small // 8) * 8 - n_small), (0, 0)))
    all_small = gather_small(small)
    small_t = small.shape[0]

    def sum8_fn(*slabs):
        acc = slabs[0]
        for s in slabs[1:]:
            acc = acc + s
        return acc

    (g_small,) = rowwise("sum_small", sum8_fn, [(all_small.reshape(8 * small_t, d), d, 0, k) for k in range(8)],
                         [], [], [(d, F32)], tm=small_t, n_rows=small_t)

    off = 0
    for n, part in zip(SMALL, small_rows[:-1]):
        rows = part.shape[0]
        update(n, [g_small[off:off + rows, :d].reshape(-1)[:wts[n].size]])
        off += rows
    update("meta_tokens", [lax.dynamic_slice_in_dim(g_small[off:off + N_META, :d], my_chip * mcols, mcols, axis=1)])

    loss = lax.psum(loss_part[0, 0], ("x", "y", "c"))

    def shaped(n, a):
        return a.reshape((1,) + wts[n].shape) if n in BIG else a.reshape(wts[n].shape)

    return (loss, grad_x, *[shaped(n, updates[n][k]) for k in range(4) for n in WEIGHTS])
```

```python
import functools
import math

import jax
import jax.numpy as jnp
from jax import lax
from jax.experimental import pallas as pl
from jax.experimental.pallas import tpu as pltpu

F32 = jnp.float32
BF16 = jnp.bfloat16
MESH = pl.DeviceIdType.MESH

N_META = 16
NORM_EPS = 1e-6
HEAD = 128
ROPE = 64
ROPE_HALF = ROPE // 2
QK_PAD = 2 * HEAD
ROPE_THETA = 10000.0
SEQ_BLOCK = 256
PAD_FRONT = SEQ_BLOCK - N_META
NEG = -1e30
VMEM_LIMIT = 56 * 1024 * 1024
ATTN_HEADS_PER_STEP = 1
ATTN_TILE_MAX = 768

ADAM_LR, ADAM_B1, ADAM_B2, ADAM_EPS, ADAM_WD, ADAM_STEP = 0.001, 0.9, 0.999, 1e-08, 0.01, 10

BIG = ("w_in", "w_hg_o", "w_q_b", "w_kv_b", "w_mla_o", "w_out", "w_ffn_in", "w_ffn_out")
COL_SHARDED = ("w_in", "w_q_b", "w_kv_b", "w_ffn_in")
SMALL = ("b_gate", "lb_logits", "hg_norm_g", "q_a_norm_g", "kv_a_norm_g", "mix_pre_g", "mix_post_g",
         "ffn_pre_g", "ffn_post_g")
WEIGHTS = ("meta_tokens", "w_in", "b_gate", "lb_logits", "hg_norm_g", "w_hg_o", "q_a_norm_g", "w_q_b",
           "kv_a_norm_g", "w_kv_b", "w_mla_o", "w_out", "mix_pre_g", "mix_post_g", "ffn_pre_g", "ffn_post_g",
           "w_ffn_in", "w_ffn_out")


def _tile(n, cap, unit=128):
    if n <= cap:
        return n
    best = None
    for t in range(unit, cap + 1, unit):
        if n % t == 0:
            best = t
    assert best is not None, (n, cap, unit)
    return best


def _sigmoid(x):
    return 1.0 / (1.0 + jnp.exp(-x))


def _bf(x):
    return x.astype(BF16)


def rowwise(name, fn, row_ins, seq_tabs, consts, row_outs, acc_outs=(), tm=SEQ_BLOCK, n_rows=None):
    t_rows = row_ins[0][0].shape[0] if n_rows is None else n_rows
    nt = t_rows // tm
    assert t_rows % tm == 0
    n_in = len(row_ins) + len(seq_tabs) + len(consts)
    n_row = len(row_outs)

    def body(*refs):
        vals = [r[...].astype(F32) for r in refs[:n_in]]
        res = fn(*vals)
        if not isinstance(res, (tuple, list)):
            res = (res,)
        outs = refs[n_in:]
        for k in range(n_row):
            outs[k][...] = res[k].astype(outs[k].dtype)
        if acc_outs:
            @pl.when(pl.program_id(0) == 0)
            def _():
                for k in range(len(acc_outs)):
                    outs[n_row + k][...] = jnp.zeros_like(outs[n_row + k])

            for k in range(len(acc_outs)):
                outs[n_row + k][...] += res[n_row + k]

    row_ins = [tuple(e) + (0,) * (4 - len(e)) for e in row_ins]
    in_specs = [pl.BlockSpec((tm, w), functools.partial(lambda i, j, ro: (ro(i) if callable(ro) else i + ro, j),
                                                        j=j, ro=ro)) for (_, w, j, ro) in row_ins]
    for tab in seq_tabs:
        per = tab.shape[0] // tm
        in_specs.append(pl.BlockSpec((tm, tab.shape[1]), functools.partial(lambda i, per: (i % per, 0), per=per)))
    for c in consts:
        in_specs.append(pl.BlockSpec(c.shape, lambda i: (0, 0)))
    row_outs = [tuple(e) + (t_rows, None)[len(e) - 2:] for e in row_outs]
    out_specs = [pl.BlockSpec((tm, w), functools.partial(lambda i, rm: (i if rm is None else rm(i), 0), rm=rm))
                 for (w, _, _, rm) in row_outs]
    out_specs += [pl.BlockSpec(s, lambda i: (0, 0)) for s in acc_outs]
    out_shape = [jax.ShapeDtypeStruct((rows, w), dt) for (w, dt, rows, _) in row_outs]
    out_shape += [jax.ShapeDtypeStruct(s, F32) for s in acc_outs]
    res = pl.pallas_call(
        body, name=name, grid=(nt,), in_specs=in_specs, out_specs=out_specs, out_shape=out_shape,
        compiler_params=pltpu.CompilerParams(dimension_semantics=("arbitrary",)),
    )(*[e[0] for e in row_ins], *seq_tabs, *consts)
    return res


def matmul(name, a, b, mode, out_dtype=F32):
    if mode != "tn":
        return _matmul_resident(name, a if isinstance(a, (list, tuple)) else [a],
                                b if isinstance(b, (list, tuple)) else [b], mode, out_dtype)
    kdim, m = a.shape
    n = b.shape[1]
    tn = _tile(n, 1536)
    tm, tk = _tile(m, 1408 if tn <= 1024 else 1024), _tile(kdim, 1536)
    nk = kdim // tk

    def body(a_ref, b_ref, o_ref, acc_ref):
        k = pl.program_id(2)

        @pl.when(k == 0)
        def _():
            acc_ref[...] = jnp.zeros_like(acc_ref)

        acc_ref[...] += lax.dot_general(a_ref[...], b_ref[...], TN_DIMS, preferred_element_type=F32)

        @pl.when(k == nk - 1)
        def _():
            o_ref[...] = acc_ref[...].astype(o_ref.dtype)

    return pl.pallas_call(
        body, name=name, grid=(m // tm, n // tn, nk),
        in_specs=[pl.BlockSpec((tk, tm), lambda i, j, k: (k, i)), pl.BlockSpec((tk, tn), lambda i, j, k: (k, j))],
        out_specs=pl.BlockSpec((tm, tn), lambda i, j, k: (i, j)),
        out_shape=jax.ShapeDtypeStruct((m, n), out_dtype),
        scratch_shapes=[pltpu.VMEM((tm, tn), F32)],
        compiler_params=pltpu.CompilerParams(dimension_semantics=("arbitrary", "arbitrary", "arbitrary"),
                                             vmem_limit_bytes=VMEM_LIMIT),
    )(a, b)


def _matmul_resident(name, a_list, b_list, mode, out_dtype):
    m = a_list[0].shape[0]
    n = b_list[0].shape[1] if mode == "nn" else b_list[0].shape[0]
    k_total = sum(a.shape[1] for a in a_list)
    out_bytes = 2 if out_dtype == BF16 else 4
    budget = VMEM_LIMIT - 4 * k_total * n - (6 << 20)
    tm = 1024
    while tm > 128 and 2 * tm * (2 * k_total + out_bytes * n) > budget:
        tm //= 2
    tm = _tile(m, tm)
    cn = _tile(n, 1024)
    npairs = len(a_list)

    def body(*refs):
        a_refs, b_refs, o_ref = refs[:npairs], refs[npairs:2 * npairs], refs[2 * npairs]
        for c in range(n // cn):
            acc = None
            for a_ref, b_ref in zip(a_refs, b_refs):
                if mode == "nn":
                    part = jnp.dot(a_ref[...], b_ref[:, pl.ds(c * cn, cn)], preferred_element_type=F32)
                else:
                    part = lax.dot_general(a_ref[...], b_ref[pl.ds(c * cn, cn), :], NT_DIMS,
                                           preferred_element_type=F32)
                acc = part if acc is None else acc + part
            o_ref[:, pl.ds(c * cn, cn)] = acc.astype(o_ref.dtype)

    in_specs = [pl.BlockSpec((tm, a.shape[1]), lambda i: (i, 0)) for a in a_list]
    in_specs += [pl.BlockSpec(b.shape, lambda i: (0, 0)) for b in b_list]
    return pl.pallas_call(
        body, name=name, grid=(m // tm,), in_specs=in_specs,
        out_specs=pl.BlockSpec((tm, n), lambda i: (i, 0)),
        out_shape=jax.ShapeDtypeStruct((m, n), out_dtype),
        compiler_params=pltpu.CompilerParams(dimension_semantics=("arbitrary",), vmem_limit_bytes=VMEM_LIMIT),
    )(*a_list, *b_list)


def matmul_fused(name, fn, row_ins, consts, weight, pieces, mode, extra_outs, out_dtype=BF16, tm=256, acc_outs=(),
                 epilogue=None, epi_outs=()):
    row_ins = [tuple(e) + (0,) * (4 - len(e)) for e in row_ins]
    t_rows = row_ins[0][0].shape[0]
    tm = _tile(t_rows, tm)
    several = isinstance(weight, (list, tuple))
    weights = list(weight) if several else [weight]
    n_in = len(row_ins) + len(consts)
    n_w = len(weights)
    n_parts = len(pieces)
    n_mm = len(epi_outs) if epilogue is not None else (n_parts if several else 1)
    n_row_out = n_mm + len(extra_outs)

    def width(w_arr):
        return w_arr.shape[1] if mode == "nn" else w_arr.shape[0]

    def body(*refs):
        w_hbms = refs[n_in:n_in + n_w]
        outs = refs[n_in + n_w:n_in + n_w + n_row_out + len(acc_outs)]
        w_refs, sems = refs[-n_w - 1:-1], refs[-1]

        @pl.when(pl.program_id(0) == 0)
        def _():
            cps = [pltpu.make_async_copy(w_hbms[k], w_refs[k], sems.at[k]) for k in range(n_w)]
            for cp in cps:
                cp.start()
            for cp in cps:
                cp.wait()
            for k in range(len(acc_outs)):
                outs[n_row_out + k][...] = jnp.zeros_like(outs[n_row_out + k])

        vals = [r[...].astype(F32) for r in refs[:n_in]]
        res = fn(*vals)
        products = []
        for a_p, piece in zip(res[:n_parts], pieces):
            w_ref, (k0, k1) = (w_refs[piece[0]], piece[1:]) if several else (w_refs[0], piece)
            if mode == "nn":
                products.append(jnp.dot(_bf(a_p), w_ref[pl.ds(k0, k1 - k0), :], preferred_element_type=F32))
            else:
                products.append(lax.dot_general(_bf(a_p), w_ref[:, pl.ds(k0, k1 - k0)], NT_DIMS,
                                                preferred_element_type=F32))
        if not several:
            products = [functools.reduce(lambda u, w: u + w, products)]
        sums = list(res[n_parts + len(extra_outs):])
        if epilogue is not None:
            products, more_sums = epilogue(products, vals)
            sums += list(more_sums)
        for p, val in enumerate(products):
            outs[p][...] = val.astype(outs[p].dtype)
        for o_ref, val in zip(outs[n_mm:n_row_out], res[n_parts:]):
            o_ref[...] = val.astype(o_ref.dtype)
        for k in range(len(acc_outs)):
            outs[n_row_out + k][...] += sums[k]

    in_specs = [pl.BlockSpec((tm, w), functools.partial(lambda i, j, ro: (ro(i) if callable(ro) else i + ro, j),
                                                        j=j, ro=ro)) for (_, w, j, ro) in row_ins]
    in_specs += [pl.BlockSpec(c.shape, lambda i: (0, 0)) for c in consts]
    in_specs += [pl.BlockSpec(memory_space=pl.ANY)] * n_w
    if epilogue is not None:
        widths = list(epi_outs) + list(extra_outs)
    elif several:
        widths = [(width(weights[piece[0]]), out_dtype) for piece in pieces] + list(extra_outs)
    else:
        widths = [(width(weights[0]), out_dtype)] + list(extra_outs)
    return pl.pallas_call(
        body, name=name, grid=(t_rows // tm,), in_specs=in_specs,
        out_specs=[pl.BlockSpec((tm, w), lambda i: (i, 0)) for (w, _) in widths]
        + [pl.BlockSpec(s, lambda i: (0, 0)) for s in acc_outs],
        out_shape=[jax.ShapeDtypeStruct((t_rows, w), dt) for (w, dt) in widths]
        + [jax.ShapeDtypeStruct(s, F32) for s in acc_outs],
        scratch_shapes=[pltpu.VMEM(w_arr.shape, w_arr.dtype) for w_arr in weights] + [pltpu.SemaphoreType.DMA((n_w,))],
        compiler_params=pltpu.CompilerParams(dimension_semantics=("arbitrary",), vmem_limit_bytes=VMEM_LIMIT),
    )(*[e[0] for e in row_ins], *consts, *weights)


def _rms(x, g):
    r = lax.rsqrt(jnp.mean(x * x, axis=-1, keepdims=True) + NORM_EPS)
    return x * r * g


def _rms_bwd(x, g, dy):
    r = lax.rsqrt(jnp.mean(x * x, axis=-1, keepdims=True) + NORM_EPS)
    xh = x * r
    dyg = dy * g
    dx = r * (dyg - xh * jnp.mean(dyg * xh, axis=-1, keepdims=True))
    return dx, jnp.sum(dy * xh, axis=0, keepdims=True)


def _silu(x):
    return x * _sigmoid(x)


def _silu_grad(x):
    s = _sigmoid(x)
    return s * (1.0 + x * (1.0 - s))


def _rope(xs, cos, s_up, s_dn):
    return xs * cos + pltpu.roll(xs, ROPE_HALF, 1) * s_up + pltpu.roll(xs, HEAD - ROPE_HALF, 1) * s_dn


def _rope_bwd(dy, cos, s_up, s_dn):
    return dy * cos + pltpu.roll(dy * s_up, HEAD - ROPE_HALF, 1) + pltpu.roll(dy * s_dn, ROPE_HALF, 1)


HG_SUB = 128
HG_LEVELS = 7
HG_E_ROWS = (HG_LEVELS + 1) * HG_SUB
HG_BWD_GROUP = 6
TN_DIMS = (((0,), (0,)), ((), ()))
NT_DIMS = (((1,), (1,)), ((), ()))


def _hg_constants():
    import numpy as np
    n = HG_SUB
    r = np.arange(n)[:, None]
    c = np.arange(n)[None, :]
    cs, ps = [], []
    for lvl in range(HG_LEVELS):
        m = (n // 2) >> lvl
        upper = (r % (2 * m)) >= m
        mid = (r // (2 * m)) * (2 * m) + m - 1
        cs.append(np.where(upper, (c > mid) & (c <= r), (c > r) & (c <= mid)))
        ps.append(((r // (2 * m)) == (c // (2 * m))) & upper & ((c % (2 * m)) < m))
    cs.append(c <= r)
    cs.append(np.ones((8, n), bool))
    cstack = np.concatenate(cs, 0).astype(np.float32)
    pstack = np.concatenate(ps, 0).astype(np.float32)
    pstack_t = np.concatenate([p.T for p in ps], 0).astype(np.float32)
    return (jnp.asarray(cstack, BF16), jnp.asarray(cstack[:HG_E_ROWS].T, BF16), jnp.asarray(pstack, F32),
            jnp.asarray(pstack_t, F32))


def _split_dot(c_bf, x):
    hi = _bf(x)
    lo = _bf(x - hi.astype(F32))
    r2 = jnp.dot(c_bf, jnp.concatenate([hi, lo], axis=1), preferred_element_type=F32)
    return r2[:, :HEAD] + r2[:, HEAD:]


def _hg_gates(hq, hf, lb):
    sq = _sigmoid(hq)
    sg = _sigmoid(hf)
    fg = lb + (1.0 - lb) * sg
    return sq, hq * sq, sg, fg, 1.0 - fg, jnp.log(fg)


def hgrn_fwd(proj_main, lb, consts, bl, lp, d):
    nh = d // HEAD
    rows_blk = _tile(lp, 768, SEQ_BLOCK)
    nb = lp // rows_blk
    spb = rows_blk // HG_SUB
    cstack, _, pstack, _ = consts

    def body(hq_ref, hf_ref, hi_ref, lb_ref, c_ref, p_ref, o_ref, st_ref, a_ref, s_ref):
        j = pl.program_id(2)

        @pl.when(j == 0)
        def _():
            s_ref[...] = jnp.zeros_like(s_ref)

        lbv = lb_ref[...]
        cs = c_ref[...]
        rows = [pl.ds(s * HG_SUB, HG_SUB) for s in range(spb)]
        gates = [_hg_gates(hq_ref[r, :].astype(F32), hf_ref[r, :].astype(F32), lbv) for r in rows]
        qs, ks = [g_[1] for g_ in gates], [g_[4] for g_ in gates]
        vs = [hi_ref[r, :].astype(F32) for r in rows]
        es = [_split_dot(cs, g_[5]) for g_ in gates]
        a_acc = [jnp.zeros((HG_SUB, HG_SUB), F32) for _ in rows]
        for lvl in range(HG_LEVELS):
            for s in range(spb):
                x = jnp.exp(es[s][lvl * HG_SUB:(lvl + 1) * HG_SUB])
                a_acc[s] = a_acc[s] + p_ref[pl.ds(lvl * HG_SUB, HG_SUB), :] * lax.dot_general(
                    _bf(qs[s] * x), _bf(ks[s] * x), NT_DIMS, preferred_element_type=F32)
        o_intra, qbs, kds, e_lasts = [], [], [], []
        for s in range(spb):
            a_bf = _bf(a_acc[s])
            a_ref[0, 0, s] = a_bf
            bc = es[s][HG_LEVELS * HG_SUB:HG_E_ROWS]
            b_last = jnp.tile(es[s][HG_E_ROWS:], (HG_SUB // 8, 1))
            o_intra.append(jnp.dot(a_bf, _bf(vs[s]), preferred_element_type=F32)
                           + jnp.sum(qs[s] * ks[s], axis=1, keepdims=True) * vs[s])
            qbs.append(_bf(qs[s] * jnp.exp(bc)))
            kds.append(_bf(ks[s] * jnp.exp(b_last - bc)))
            e_lasts.append(jnp.exp(b_last))
        st = s_ref[...]
        for s in range(spb):
            st_ref[0, 0, s] = st
            o_ref[rows[s], :] = (o_intra[s] + lax.dot_general(qbs[s], _bf(st), NT_DIMS, preferred_element_type=F32)
                                 ).astype(o_ref.dtype)
            st = st * e_lasts[s] + lax.dot_general(_bf(vs[s]), kds[s], TN_DIMS, preferred_element_type=F32)
        s_ref[...] = st

    def colspec(off):
        return pl.BlockSpec((rows_blk, HEAD), functools.partial(lambda h, b, j, off: (b * nb + j, off + h), off=off))

    whole = lambda arr: pl.BlockSpec(arr.shape, lambda h, b, j: (0, 0))
    return pl.pallas_call(
        body, name="hgrn_fwd", grid=(nh, bl, nb),
        in_specs=[colspec(0), colspec(nh), colspec(2 * nh), pl.BlockSpec((1, HEAD), lambda h, b, j: (0, h)),
                  whole(cstack), whole(pstack)],
        out_specs=[pl.BlockSpec((rows_blk, HEAD), lambda h, b, j: (b * nb + j, h)),
                   pl.BlockSpec((1, 1, spb, HEAD, HEAD), lambda h, b, j: (b, h, j, 0, 0)),
                   pl.BlockSpec((1, 1, spb, HG_SUB, HG_SUB), lambda h, b, j: (b, h, j, 0, 0))],
        out_shape=[jax.ShapeDtypeStruct((bl * lp, d), BF16),
                   jax.ShapeDtypeStruct((bl, nh, lp // HG_SUB, HEAD, HEAD), F32),
                   jax.ShapeDtypeStruct((bl, nh, lp // HG_SUB, HG_SUB, HG_SUB), BF16)],
        scratch_shapes=[pltpu.VMEM((HEAD, HEAD), F32)],
        compiler_params=pltpu.CompilerParams(dimension_semantics=("arbitrary", "arbitrary", "arbitrary")),
    )(proj_main, proj_main, proj_main, lb, cstack, pstack)


def hgrn_bwd(proj_main, lb, consts, states, a_mats, do_scan, bl, lp, d):
    nh = d // HEAD
    rows_blk = _tile(lp, 768, SEQ_BLOCK)
    nb = lp // rows_blk
    spb = rows_blk // HG_SUB
    cstack, cstack_t = consts[0], consts[1]
    pstack, pstack_t = _bf(consts[2]), _bf(consts[3])

    def body(hq_ref, hf_ref, hi_ref, lb_ref, c_ref, ct_ref, p_ref, pt_ref, st_ref, a_ref, do_ref,
             dq_ref, df_ref, di_ref, dlb_ref, ds_ref):
        b_id, j = pl.program_id(1), pl.program_id(2)
        blk = nb - 1 - j

        @pl.when(j == 0)
        def _():
            ds_ref[...] = jnp.zeros_like(ds_ref)

        @pl.when((j == 0) & (b_id == 0))
        def _():
            dlb_ref[...] = jnp.zeros_like(dlb_ref)

        lbv = lb_ref[...]
        cs = c_ref[...]
        cst = ct_ref[...]

        dlb = jnp.zeros((1, HEAD), F32)
        for first in reversed(range(0, spb, HG_BWD_GROUP)):
            dlb = dlb + _hg_group_bwd(list(range(first, min(first + HG_BWD_GROUP, spb))), lbv, cs, cst, hq_ref,
                                      hf_ref, hi_ref, st_ref, a_ref, do_ref, p_ref, pt_ref, dq_ref, df_ref, di_ref,
                                      ds_ref)
        dlb_ref[...] += dlb

    def _hg_group_bwd(ids, lbv, cs, cst, hq_ref, hf_ref, hi_ref, st_ref, a_ref, do_ref, p_ref, pt_ref, dq_ref,
                      df_ref, di_ref, ds_ref):
        rng = range(len(ids))
        rows = [pl.ds(s * HG_SUB, HG_SUB) for s in ids]
        hqs = [hq_ref[r, :].astype(F32) for r in rows]
        gates = [_hg_gates(hqs[s], hf_ref[rows[s], :].astype(F32), lbv) for s in rng]
        sqs, qs, sgs, fgs, ks = ([g_[i] for g_ in gates] for i in range(5))
        vs = [hi_ref[r, :].astype(F32) for r in rows]
        dos = [do_ref[r, :].astype(F32) for r in rows]
        sts = [st_ref[0, 0, s] for s in ids]
        es = [_split_dot(cs, g_[5]) for g_ in gates]
        bcs = [e[HG_LEVELS * HG_SUB:HG_E_ROWS] for e in es]
        b_lasts = [jnp.tile(e[HG_E_ROWS:], (HG_SUB // 8, 1)) for e in es]
        ebs = [jnp.exp(bc) for bc in bcs]
        qbs = [qs[s] * ebs[s] for s in rng]
        ers = [jnp.exp(b_lasts[s] - bcs[s]) for s in rng]
        kds = [ks[s] * ers[s] for s in rng]
        e_lasts = [jnp.exp(b) for b in b_lasts]
        do_bfs, v_bfs = [_bf(x) for x in dos], [_bf(x) for x in vs]
        das = [_bf(lax.dot_general(do_bfs[s], v_bfs[s], NT_DIMS, preferred_element_type=F32)) for s in rng]
        dats = [_bf(lax.dot_general(v_bfs[s], do_bfs[s], NT_DIMS, preferred_element_type=F32)) for s in rng]
        dqbs = [jnp.dot(do_bfs[s], _bf(sts[s]), preferred_element_type=F32) for s in rng]
        m_s = [lax.dot_general(do_bfs[s], _bf(qbs[s]), TN_DIMS, preferred_element_type=F32) for s in rng]
        dst_outs = [None] * len(ids)
        dst = ds_ref[...]
        for s in reversed(rng):
            dst_outs[s] = dst
            dst = dst * e_lasts[s] + m_s[s]
        ds_ref[...] = dst
        dst_bfs = [_bf(x) for x in dst_outs]
        d_diags = [jnp.sum(dos[s] * vs[s], axis=1, keepdims=True) for s in rng]
        dvs = [lax.dot_general(a_ref[0, 0, ids[s]], do_bfs[s], TN_DIMS, preferred_element_type=F32)
               + jnp.sum(qs[s] * ks[s], axis=1, keepdims=True) * dos[s]
               + lax.dot_general(_bf(kds[s]), dst_bfs[s], NT_DIMS, preferred_element_type=F32) for s in rng]
        dkds = [jnp.dot(v_bfs[s], dst_bfs[s], preferred_element_type=F32) for s in rng]
        dqs = [dqbs[s] * ebs[s] + d_diags[s] * ks[s] for s in rng]
        dks = [dkds[s] * ers[s] + d_diags[s] * qs[s] for s in rng]
        d_lasts = [jnp.sum(dst_outs[s] * sts[s] * e_lasts[s], axis=0, keepdims=True)
                   + jnp.sum(dkds[s] * kds[s], axis=0, keepdims=True) for s in rng]
        des = [[] for _ in rng]
        for lvl in range(HG_LEVELS):
            for s in rng:
                x = jnp.exp(es[s][lvl * HG_SUB:(lvl + 1) * HG_SUB])
                qh, kh = qs[s] * x, ks[s] * x
                dm = p_ref[pl.ds(lvl * HG_SUB, HG_SUB), :] * das[s]
                dmt = pt_ref[pl.ds(lvl * HG_SUB, HG_SUB), :] * dats[s]
                dqh = jnp.dot(dm, _bf(kh), preferred_element_type=F32)
                dkh = jnp.dot(dmt, _bf(qh), preferred_element_type=F32)
                dqs[s] = dqs[s] + dqh * x
                dks[s] = dks[s] + dkh * x
                des[s].append(dqh * qh + dkh * kh)
        dlb = jnp.zeros((1, HEAD), F32)
        for s in rng:
            des[s].append(dqbs[s] * qbs[s] - dkds[s] * kds[s])
            dg = _split_dot(cst, jnp.concatenate(des[s], axis=0)) + d_lasts[s]
            dfg = dg / fgs[s] - dks[s]
            dq_ref[rows[s], :] = (dqs[s] * (sqs[s] * (1.0 + hqs[s] * (1.0 - sqs[s])))).astype(dq_ref.dtype)
            df_ref[rows[s], :] = (dfg * (1.0 - lbv) * sgs[s] * (1.0 - sgs[s])).astype(df_ref.dtype)
            di_ref[rows[s], :] = dvs[s].astype(di_ref.dtype)
            dlb = dlb + jnp.sum(dfg * (1.0 - sgs[s]), axis=0, keepdims=True)
        return dlb

    def colspec(off):
        return pl.BlockSpec((rows_blk, HEAD),
                            functools.partial(lambda h, b, j, off: (b * nb + nb - 1 - j, off + h), off=off))

    whole = lambda arr: pl.BlockSpec(arr.shape, lambda h, b, j: (0, 0))
    mats = lambda: pl.BlockSpec((1, 1, spb, HEAD, HEAD), lambda h, b, j: (b, h, nb - 1 - j, 0, 0))
    t_rows = bl * lp
    return pl.pallas_call(
        body, name="hgrn_bwd", grid=(nh, bl, nb),
        in_specs=[colspec(0), colspec(nh), colspec(2 * nh), pl.BlockSpec((1, HEAD), lambda h, b, j: (0, h)),
                  whole(cstack), whole(cstack_t), whole(pstack), whole(pstack_t), mats(), mats(), colspec(0)],
        out_specs=[colspec(0), colspec(0), colspec(0), pl.BlockSpec((1, HEAD), lambda h, b, j: (0, h))],
        out_shape=[jax.ShapeDtypeStruct((t_rows, d), BF16)] * 3 + [jax.ShapeDtypeStruct((1, d), F32)],
        scratch_shapes=[pltpu.VMEM((HEAD, HEAD), F32)],
        compiler_params=pltpu.CompilerParams(dimension_semantics=("arbitrary", "arbitrary", "arbitrary")),
    )(proj_main, proj_main, proj_main, lb, cstack, cstack_t, pstack, pstack_t, states, a_mats, do_scan)


def _key_query_mask(key0, qry0, nk, nq_, causal):
    key = key0 + lax.broadcasted_iota(jnp.int32, (nk, 1), 0)
    if not causal:
        return key >= PAD_FRONT
    qry = qry0 + lax.broadcasted_iota(jnp.int32, (1, nq_), 1)
    return (key <= qry) & (key >= PAD_FRONT)


def _attn_tile(lp):
    return _tile(lp, ATTN_TILE_MAX, SEQ_BLOCK)


def attn_fwd_t(q_cat, k_cat, v_t, bl, lp, nm):
    tq = tk = _attn_tile(lp)
    nq = lp // tq
    hp = ATTN_HEADS_PER_STEP
    assert nm % hp == 0

    def body(q_ref, k_ref, vt_ref, o_ref, lse_ref, m_ref, l_ref, acc_ref):
        i = pl.program_id(2)
        m_ref[...] = jnp.full_like(m_ref, NEG)
        l_ref[...] = jnp.zeros_like(l_ref)
        acc_ref[...] = jnp.zeros_like(acc_ref)

        def step(c, mask):
            c0 = pl.multiple_of(c * tk, tk)
            for hh in range(hp):
                cols = pl.ds(hh * QK_PAD, QK_PAD)
                st = lax.dot_general(k_ref[pl.ds(c0, tk), cols], q_ref[:, cols], NT_DIMS,
                                     preferred_element_type=F32)
                if mask is not None:
                    st = jnp.where(_key_query_mask(c * tk, i * tq, tk, tq, mask == "causal"), st, NEG)
                m_old = m_ref[hh]
                m_new = jnp.maximum(m_old, jnp.max(st, axis=0, keepdims=True))
                alpha = jnp.exp(m_old - m_new)
                pt = jnp.exp(st - m_new)
                l_ref[hh] = alpha * l_ref[hh] + jnp.sum(pt, axis=0, keepdims=True)
                acc_ref[hh] = alpha * acc_ref[hh] + jnp.dot(vt_ref[0, hh, pl.ds(c, 1)][0], _bf(pt),
                                                            preferred_element_type=F32)
                m_ref[hh] = m_new

        def mid(c, carry):
            step(c, None)
            return carry

        @pl.when(i == 0)
        def _():
            step(0, "causal")

        @pl.when(i > 0)
        def _():
            step(0, "pad")
            lax.fori_loop(1, i, mid, 0)
            step(i, "causal")

        for hh in range(hp):
            o_ref[:, pl.ds(hh * HEAD, HEAD)] = jnp.transpose(acc_ref[hh] / l_ref[hh]).astype(o_ref.dtype)
            lse_ref[0, hh, 0] = m_ref[hh] + jnp.log(l_ref[hh])

    return pl.pallas_call(
        body, name="attn_fwd", grid=(bl, nm // hp, nq),
        in_specs=[pl.BlockSpec((tq, hp * QK_PAD), lambda b, h, i: (b * nq + i, h)),
                  pl.BlockSpec((lp, hp * QK_PAD), lambda b, h, i: (b, h)),
                  pl.BlockSpec((1, hp, nq, HEAD, tk), lambda b, h, i: (b, h, 0, 0, 0))],
        out_specs=[pl.BlockSpec((tq, hp * HEAD), lambda b, h, i: (b * nq + i, h)),
                   pl.BlockSpec((1, hp, 1, 1, tq), lambda b, h, i: (b, h, i, 0, 0))],
        out_shape=[jax.ShapeDtypeStruct((bl * lp, nm * HEAD), BF16),
                   jax.ShapeDtypeStruct((bl, nm, nq, 1, tq), F32)],
        scratch_shapes=[pltpu.VMEM((hp, 1, tq), F32), pltpu.VMEM((hp, 1, tq), F32), pltpu.VMEM((hp, HEAD, tq), F32)],
        compiler_params=pltpu.CompilerParams(dimension_semantics=("arbitrary", "arbitrary", "arbitrary")),
    )(q_cat, k_cat, v_t)


def attn_bwd_t(q_cat, k_cat, k_t, v, o, do, lse, bl, lp, nm):
    tq = tk = _attn_tile(lp)
    nq = lp // tq
    hp = ATTN_HEADS_PER_STEP
    assert nm % hp == 0

    def body(q_ref, k_ref, kt_ref, v_ref, o_ref, do_ref, lse_ref, dq_ref, dk_ref, dv_ref, dqt_ref, dka_ref, dva_ref):
        i = pl.program_id(2)

        @pl.when(i == 0)
        def _():
            dqt_ref[...] = jnp.zeros_like(dqt_ref)

        dka_ref[...] = jnp.zeros_like(dka_ref)
        dva_ref[...] = jnp.zeros_like(dva_ref)
        ones8 = jnp.ones((8, HEAD), BF16)

        def step(c, mask):
            c0 = pl.multiple_of(c * tq, tq)
            for hh in range(hp):
                qcols, vcols = pl.ds(hh * QK_PAD, QK_PAD), pl.ds(hh * HEAD, HEAD)
                qs = q_ref[pl.ds(c0, tq), qcols]
                dos = do_ref[pl.ds(c0, tq), vcols]
                prod = dos.astype(F32) * o_ref[pl.ds(c0, tq), vcols].astype(F32)
                hi = _bf(prod)
                lo = _bf(prod - hi.astype(F32))
                delta8 = (lax.dot_general(ones8, hi, NT_DIMS, preferred_element_type=F32)
                          + lax.dot_general(ones8, lo, NT_DIMS, preferred_element_type=F32))
                st = lax.dot_general(k_ref[:, qcols], qs, NT_DIMS, preferred_element_type=F32)
                pt = jnp.exp(st - lse_ref[0, hh, pl.ds(c, 1)][0])
                if mask is not None:
                    pt = jnp.where(_key_query_mask(i * tk, c * tq, tk, tq, mask == "causal"), pt, 0.0)
                dva_ref[hh] += jnp.dot(_bf(pt), dos, preferred_element_type=F32)
                dpt = lax.dot_general(v_ref[:, vcols], dos, NT_DIMS, preferred_element_type=F32)
                dst = _bf(pt * (dpt - jnp.tile(delta8, (tk // 8, 1))))
                dka_ref[hh] += jnp.dot(dst, qs, preferred_element_type=F32)
                dqt_ref[hh, pl.ds(c, 1)] += jnp.dot(kt_ref[0, hh, 0], dst, preferred_element_type=F32)[None]

        step(i, "causal")

        def rest_masked(c, carry):
            step(c, "pad")
            return carry

        def rest(c, carry):
            step(c, None)
            return carry

        @pl.when(i == 0)
        def _():
            lax.fori_loop(1, nq, rest_masked, 0)

        @pl.when(i > 0)
        def _():
            lax.fori_loop(i + 1, nq, rest, 0)

        for hh in range(hp):
            dk_ref[:, pl.ds(hh * QK_PAD, QK_PAD)] = dka_ref[hh].astype(dk_ref.dtype)
            dv_ref[:, pl.ds(hh * HEAD, HEAD)] = dva_ref[hh].astype(dv_ref.dtype)

        @pl.when(i == nq - 1)
        def _():
            for hh in range(hp):
                for c in range(nq):
                    dq_ref[pl.ds(c * tq, tq), pl.ds(hh * QK_PAD, QK_PAD)] = (
                        jnp.transpose(dqt_ref[hh, c])).astype(dq_ref.dtype)

    return pl.pallas_call(
        body, name="attn_bwd", grid=(bl, nm // hp, nq),
        in_specs=[pl.BlockSpec((lp, hp * QK_PAD), lambda b, h, i: (b, h)),
                  pl.BlockSpec((tk, hp * QK_PAD), lambda b, h, i: (b * nq + i, h)),
                  pl.BlockSpec((1, hp, 1, QK_PAD, tk), lambda b, h, i: (b, h, i, 0, 0)),
                  pl.BlockSpec((tk, hp * HEAD), lambda b, h, i: (b * nq + i, h)),
                  pl.BlockSpec((lp, hp * HEAD), lambda b, h, i: (b, h)),
                  pl.BlockSpec((lp, hp * HEAD), lambda b, h, i: (b, h)),
                  pl.BlockSpec((1, hp, nq, 1, tq), lambda b, h, i: (b, h, 0, 0, 0))],
        out_specs=[pl.BlockSpec((lp, hp * QK_PAD), lambda b, h, i: (b, h)),
                   pl.BlockSpec((tk, hp * QK_PAD), lambda b, h, i: (b * nq + i, h)),
                   pl.BlockSpec((tk, hp * HEAD), lambda b, h, i: (b * nq + i, h))],
        out_shape=[jax.ShapeDtypeStruct((bl * lp, nm * QK_PAD), BF16),
                   jax.ShapeDtypeStruct((bl * lp, nm * QK_PAD), BF16),
                   jax.ShapeDtypeStruct((bl * lp, nm * HEAD), BF16)],
        scratch_shapes=[pltpu.VMEM((hp, nq, QK_PAD, tq), F32), pltpu.VMEM((hp, tk, QK_PAD), F32),
                        pltpu.VMEM((hp, tk, HEAD), F32)],
        compiler_params=pltpu.CompilerParams(dimension_semantics=("arbitrary", "arbitrary", "arbitrary")),
    )(q_cat, k_cat, k_t, v, o, do, lse)


def _place():
    return lax.axis_index("x"), lax.axis_index("y"), lax.axis_index("c")


def gather_shards(packed):
    hbm = pl.BlockSpec(memory_space=pl.ANY)

    def body(src_ref, out_ref, send_sems, recv_sems, local_sem):
        x, y, c = _place()
        me = 2 * x + y
        chips = [(1 - x, y), (x, 1 - y), (1 - x, 1 - y)]
        local = pltpu.make_async_copy(src_ref, out_ref.at[me], local_sem)
        local.start()
        sends = []
        for k, (px, py) in enumerate(chips):
            cp = pltpu.make_async_remote_copy(src_ref=src_ref, dst_ref=out_ref.at[me], send_sem=send_sems.at[k],
                                              recv_sem=recv_sems.at[k], device_id=(px, py, c), device_id_type=MESH)
            cp.start()
            sends.append(cp)
        for k, (px, py) in enumerate(chips):
            pltpu.make_async_remote_copy(src_ref=src_ref, dst_ref=out_ref.at[2 * px + py], send_sem=send_sems.at[k],
                                         recv_sem=recv_sems.at[k], device_id=(px, py, c),
                                         device_id_type=MESH).wait_recv()
        for cp in sends:
            cp.wait_send()
        local.wait()

    return pl.pallas_call(
        body, name="gather_shards", in_specs=[hbm], out_specs=hbm,
        out_shape=jax.ShapeDtypeStruct((4,) + packed.shape, packed.dtype),
        scratch_shapes=[pltpu.SemaphoreType.DMA((3,)), pltpu.SemaphoreType.DMA((3,)), pltpu.SemaphoreType.DMA],
    )(packed)


def gather_small(small):
    hbm = pl.BlockSpec(memory_space=pl.ANY)

    def body(small_ref, all_ref, send_sems, recv_sems, local_sem):
        x, y, c = _place()
        me = 4 * x + 2 * y + c
        local = pltpu.make_async_copy(small_ref, all_ref.at[me], local_sem)
        local.start()
        others = [(x ^ ((r >> 2) & 1), y ^ ((r >> 1) & 1), c ^ (r & 1)) for r in range(1, 8)]
        sends = []
        for r, peer in enumerate(others):
            cp = pltpu.make_async_remote_copy(src_ref=small_ref, dst_ref=all_ref.at[me], send_sem=send_sems.at[r],
                                              recv_sem=recv_sems.at[r], device_id=peer, device_id_type=MESH)
            cp.start()
            sends.append(cp)
        for r, (px, py, pc) in enumerate(others):
            pltpu.make_async_remote_copy(src_ref=small_ref, dst_ref=all_ref.at[4 * px + 2 * py + pc],
                                         send_sem=send_sems.at[r], recv_sem=recv_sems.at[r],
                                         device_id=(px, py, pc), device_id_type=MESH).wait_recv()
        for cp in sends:
            cp.wait_send()
        local.wait()

    return pl.pallas_call(
        body, name="gather_small", in_specs=[hbm], out_specs=hbm,
        out_shape=jax.ShapeDtypeStruct((8,) + small.shape, small.dtype),
        scratch_shapes=[pltpu.SemaphoreType.DMA((7,)), pltpu.SemaphoreType.DMA((7,)), pltpu.SemaphoreType.DMA],
    )(small)


def swap_with_sibling(name, parts):
    n = len(parts)
    hbm = pl.BlockSpec(memory_space=pl.ANY)

    def body(*refs):
        x, y, c = _place()
        cps = [pltpu.make_async_remote_copy(src_ref=refs[j], dst_ref=refs[n + j], send_sem=refs[2 * n].at[j],
                                            recv_sem=refs[2 * n + 1].at[j], device_id=(x, y, 1 - c),
                                            device_id_type=MESH) for j in range(n)]
        for cp in cps:
            cp.start()
        for cp in cps:
            cp.wait()

    return pl.pallas_call(
        body, name=name, in_specs=[hbm] * n, out_specs=[hbm] * n,
        out_shape=[jax.ShapeDtypeStruct(p.shape, p.dtype) for p in parts],
        scratch_shapes=[pltpu.SemaphoreType.DMA((n,)), pltpu.SemaphoreType.DMA((n,))],
    )(*parts)


def _chips3():
    x, y, c = _place()
    return [(1 - x, y, c), (x, 1 - y, c), (1 - x, 1 - y, c)]


def _push_copies(src_refs, land_refs, send_sems, recv_sems, per_chip):
    x, y, _ = _place()
    cps = []
    for j, (src_ref, land_ref) in enumerate(zip(src_refs, land_refs)):
        for k, (px, py, pc) in enumerate(_chips3()):
            part = src_ref.at[2 * px + py] if per_chip else src_ref
            slot = k if per_chip else 2 * x + y
            cps.append(pltpu.make_async_remote_copy(
                src_ref=part, dst_ref=land_ref.at[slot], send_sem=send_sems.at[3 * j + k],
                recv_sem=recv_sems.at[3 * j + k], device_id=(px, py, pc), device_id_type=MESH))
    return cps


def push_start(name, srcs, per_chip):
    n = len(srcs)
    hbm = pl.BlockSpec(memory_space=pltpu.HBM)
    sem = pl.BlockSpec(memory_space=pltpu.SEMAPHORE)
    lands = [lax.empty((3 if per_chip else 4,) + s.shape[-2:], s.dtype) for s in srcs]

    def body(*refs):
        src_refs, land_refs = refs[:n], refs[n:2 * n]
        send_sems, recv_sems = refs[2 * n], refs[2 * n + 1]
        for cp in _push_copies(src_refs, land_refs, send_sems, recv_sems, per_chip):
            cp.start()
        refs[-1][...] = jnp.zeros_like(refs[-1])

    outs = pl.pallas_call(
        body, name=name,
        out_shape=(pltpu.SemaphoreType.DMA((3 * n,)), pltpu.SemaphoreType.DMA((3 * n,)),
                   *[pltpu.HBM(a.shape, a.dtype) for a in list(srcs) + lands], jax.ShapeDtypeStruct((8, HEAD), F32)),
        in_specs=(hbm,) * (2 * n),
        out_specs=(sem, sem) + (hbm,) * (2 * n) + (pl.BlockSpec(memory_space=pltpu.VMEM),),
        input_output_aliases={j: 2 + j for j in range(2 * n)},
        compiler_params=pltpu.CompilerParams(has_side_effects=pltpu.SideEffectType.DATAFLOW_SIDE_EFFECTING),
    )(*[pltpu.with_memory_space_constraint(a, pltpu.HBM) for a in list(srcs) + lands])
    return tuple(outs[:-1]), outs[-1]


def push_wait(name, handle, after, per_chip):
    send_sems, recv_sems = handle[0], handle[1]
    thru = handle[2:]
    n = len(thru) // 2
    hbm = pl.BlockSpec(memory_space=pltpu.HBM)
    sem = pl.BlockSpec(memory_space=pltpu.SEMAPHORE)

    def body(*refs):
        src_refs, land_refs = refs[:n], refs[n:2 * n]
        for cp in _push_copies(src_refs, land_refs, refs[2 * n], refs[2 * n + 1], per_chip):
            cp.wait_send()
            cp.wait_recv()

    outs = pl.pallas_call(
        body, name=name,
        out_shape=tuple(pltpu.HBM(a.shape, a.dtype) for a in thru),
        in_specs=(hbm,) * (2 * n) + (sem, sem, pl.BlockSpec(memory_space=pl.ANY)), out_specs=(hbm,) * (2 * n),
        input_output_aliases={j: j for j in range(2 * n)},
        compiler_params=pltpu.CompilerParams(has_side_effects=pltpu.SideEffectType.DATAFLOW_SIDE_EFFECTING),
    )(*thru, send_sems, recv_sems, after)
    return outs[:n], outs[n:]


def join_gathered(name, own, landed, my_chip):
    blocks = lax.dynamic_update_index_in_dim(landed, own, my_chip, 0)
    _, r, c = blocks.shape
    if name in COL_SHARDED:
        return blocks.transpose(1, 0, 2).reshape(r, 4 * c)
    return blocks.reshape(4 * r, c)


def adamw(name, w, g_parts, m, v):
    r, c = w.shape
    tr = r if r * c <= 65536 else _tile(r, 128, 8)
    ng = len(g_parts)

    def body(*refs):
        w_ref, m_ref, v_ref = refs[0], refs[1 + ng], refs[2 + ng]
        g_ref, d_ref, nm_ref, nv_ref = refs[3 + ng:]
        gv = refs[1][...]
        for k in range(1, ng):
            gv = gv + refs[1 + k][...]
        m_new = ADAM_B1 * m_ref[...] + (1.0 - ADAM_B1) * gv
        v_new = ADAM_B2 * v_ref[...] + (1.0 - ADAM_B2) * (gv * gv)
        m_hat = m_new / (1.0 - ADAM_B1 ** ADAM_STEP)
        v_hat = v_new / (1.0 - ADAM_B2 ** ADAM_STEP)
        g_ref[...] = gv
        d_ref[...] = -ADAM_LR * (m_hat / (jnp.sqrt(v_hat) + ADAM_EPS) + ADAM_WD * w_ref[...])
        nm_ref[...] = m_new
        nv_ref[...] = v_new

    spec = pl.BlockSpec((tr, c), lambda i: (i, 0))
    return pl.pallas_call(
        body, name=name, grid=(r // tr,), in_specs=[spec] * (3 + ng), out_specs=[spec] * 4,
        out_shape=[jax.ShapeDtypeStruct((r, c), F32)] * 4,
        compiler_params=pltpu.CompilerParams(dimension_semantics=("arbitrary",)),
    )(w, *g_parts, m, v)


def split_full(name, full, s):
    if name in COL_SHARDED:
        c = full.shape[1] // 4
        return full[:, s * c:(s + 1) * c]
    r = full.shape[0] // 4
    return full[s * r:(s + 1) * r]


def kernel(x, meta_tokens, w_in, b_gate, lb_logits, hg_norm_g, w_hg_o, q_a_norm_g, w_q_b, kv_a_norm_g, w_kv_b, w_mla_o, w_out, mix_pre_g, mix_post_g, ffn_pre_g, ffn_post_g, w_ffn_in, w_ffn_out, loss_target, m_meta_tokens, m_w_in, m_b_gate, m_lb_logits, m_hg_norm_g, m_w_hg_o, m_q_a_norm_g, m_w_q_b, m_kv_a_norm_g, m_w_kv_b, m_w_mla_o, m_w_out, m_mix_pre_g, m_mix_post_g, m_ffn_pre_g, m_ffn_post_g, m_w_ffn_in, m_w_ffn_out, v_meta_tokens, v_w_in, v_b_gate, v_lb_logits, v_hg_norm_g, v_w_hg_o, v_q_a_norm_g, v_w_q_b, v_kv_a_norm_g, v_w_kv_b, v_w_mla_o, v_w_out, v_mix_pre_g, v_mix_post_g, v_ffn_pre_g, v_ffn_post_g, v_w_ffn_in, v_w_ffn_out):
    wts = dict(meta_tokens=meta_tokens, w_in=w_in[0], b_gate=b_gate, lb_logits=lb_logits, hg_norm_g=hg_norm_g,
               w_hg_o=w_hg_o[0], q_a_norm_g=q_a_norm_g, w_q_b=w_q_b[0], kv_a_norm_g=kv_a_norm_g, w_kv_b=w_kv_b[0],
               w_mla_o=w_mla_o[0], w_out=w_out[0], mix_pre_g=mix_pre_g, mix_post_g=mix_post_g, ffn_pre_g=ffn_pre_g,
               ffn_post_g=ffn_post_g, w_ffn_in=w_ffn_in[0], w_ffn_out=w_ffn_out[0])
    mom_m = dict(meta_tokens=m_meta_tokens, w_in=m_w_in[0], b_gate=m_b_gate, lb_logits=m_lb_logits,
                 hg_norm_g=m_hg_norm_g, w_hg_o=m_w_hg_o[0], q_a_norm_g=m_q_a_norm_g, w_q_b=m_w_q_b[0],
                 kv_a_norm_g=m_kv_a_norm_g, w_kv_b=m_w_kv_b[0], w_mla_o=m_w_mla_o[0], w_out=m_w_out[0],
                 mix_pre_g=m_mix_pre_g, mix_post_g=m_mix_post_g, ffn_pre_g=m_ffn_pre_g, ffn_post_g=m_ffn_post_g,
                 w_ffn_in=m_w_ffn_in[0], w_ffn_out=m_w_ffn_out[0])
    mom_v = dict(meta_tokens=v_meta_tokens, w_in=v_w_in[0], b_gate=v_b_gate, lb_logits=v_lb_logits,
                 hg_norm_g=v_hg_norm_g, w_hg_o=v_w_hg_o[0], q_a_norm_g=v_q_a_norm_g, w_q_b=v_w_q_b[0],
                 kv_a_norm_g=v_kv_a_norm_g, w_kv_b=v_w_kv_b[0], w_mla_o=v_w_mla_o[0], w_out=v_w_out[0],
                 mix_pre_g=v_mix_pre_g, mix_post_g=v_mix_post_g, ffn_pre_g=v_ffn_pre_g, ffn_post_g=v_ffn_post_g,
                 w_ffn_in=v_w_ffn_in[0], w_ffn_out=v_w_ffn_out[0])

    bl, seq, d = x.shape
    lp = PAD_FRONT + N_META + seq
    t_rows = bl * lp
    nh = d // HEAD
    ql, kvl = wts["w_q_b"].shape[0], wts["w_kv_b"].shape[0]
    nm = (4 * wts["w_mla_o"].shape[0]) // HEAD
    ffn = 4 * wts["w_ffn_out"].shape[0]
    mla_w = ql + kvl + HEAD
    assert ql == kvl and ql % HEAD == 0 and seq % SEQ_BLOCK == 0 and d % HEAD == 0
    scale = (HEAD + ROPE) ** -0.5
    my_chip = 2 * lax.axis_index("x") + lax.axis_index("y")

    mcols = meta_tokens.shape[1]
    meta_all = gather_shards(meta_tokens)
    meta_full = jnp.concatenate([meta_all[s] for s in range(4)], axis=1)

    def start_gather(name, names, order_after):
        srcs = [_bf(wts[n]) for n in names]
        if order_after is not None:
            srcs[0] = srcs[0] + order_after[0, 0].astype(BF16)
        return push_start(name, srcs, per_chip=False)

    def finish_gather(name, names, started, after):
        owns, landed = push_wait(name, started[0], after, per_chip=False)
        return {n: join_gathered(n, own, land, my_chip) for n, own, land in zip(names, owns, landed)}

    rest_names = tuple(n for n in BIG if n != "w_in")
    my_c = lax.axis_index("c")
    w_in_bf = _bf(wts["w_in"])
    half = w_in_bf.shape[0] // 2
    own_half = (lax.dynamic_slice_in_dim(w_in_bf, my_c * half, half, axis=0)
                + (meta_all[0, :1, :1] * 0.0)[0, 0].astype(BF16))
    gather_1 = push_start("gather_w_in_start", [own_half], per_chip=False)
    gather_2 = start_gather("gather_rest_start", rest_names, gather_1[1])

    h0 = jnp.concatenate([jnp.zeros((bl, PAD_FRONT, d), F32), jnp.broadcast_to(meta_full[None], (bl, N_META, d)), x],
                         axis=1).reshape(t_rows, d)
    tiles_seq, tiles_real = lp // SEQ_BLOCK, seq // SEQ_BLOCK
    assert PAD_FRONT + N_META == SEQ_BLOCK

    def real_block(i):
        return (i // tiles_seq) * tiles_real + jnp.maximum(i % tiles_seq - 1, 0)

    meta_rows = jnp.broadcast_to(((jnp.arange(lp) >= PAD_FRONT) & (jnp.arange(lp) < PAD_FRONT + N_META)
                                  ).astype(F32)[:, None], (lp, HEAD))
    pos = (jnp.arange(lp, dtype=jnp.int32) - PAD_FRONT).astype(F32)
    inv_freq = 1.0 / (ROPE_THETA ** (jnp.arange(0, ROPE, 2, dtype=F32) / ROPE))
    ang = pos[:, None] * inv_freq[None, :]
    zeros32 = jnp.zeros((lp, ROPE_HALF), F32)
    zeros64 = jnp.zeros((lp, HEAD - ROPE), F32)
    t_cos = jnp.concatenate([jnp.cos(ang), jnp.cos(ang), zeros64], axis=1)
    t_up = jnp.concatenate([zeros32, jnp.sin(ang), zeros64], axis=1)
    t_dn = jnp.concatenate([-jnp.sin(ang), zeros32, zeros64], axis=1)
    real = jnp.broadcast_to((jnp.arange(lp) >= PAD_FRONT + N_META).astype(F32)[:, None], (lp, d))
    lb_soft = jax.nn.softmax(lb_logits.astype(F32), axis=0)
    lb = lb_soft[0:1]

    (u1,) = rowwise("norm_mix_pre", lambda h, g: _rms(h, g), [(h0, d, 0)], [], [mix_pre_g + gather_2[1][0, 0]],
                    [(d, BF16)])
    _, (fetched,) = push_wait("gather_w_in_wait", gather_1[0], u1, per_chip=False)
    (handed,) = swap_with_sibling("swap_w_in", [fetched])
    halves = jnp.stack([fetched, handed])
    remote = jnp.concatenate([lax.dynamic_index_in_dim(halves, my_c, 0, keepdims=False),
                              lax.dynamic_index_in_dim(halves, 1 - my_c, 0, keepdims=False)], axis=1)
    full = {"w_in": join_gathered("w_in", w_in_bf, remote, my_chip)}
    w_main = jnp.concatenate([full["w_in"][:, :4 * d], full["w_in"][:, -2 * d:]], axis=1)
    w_mla = jnp.pad(full["w_in"][:, 4 * d:4 * d + ql + kvl + ROPE], ((0, 0), (0, HEAD - ROPE)))
    proj_main = matmul("proj_main", u1, w_main, "nn", out_dtype=BF16)
    proj_mla = matmul("proj_mla", u1, w_mla, "nn", out_dtype=BF16)
    hg_consts = _hg_constants()
    o_scan, states, a_mats = hgrn_fwd(proj_main, lb, hg_consts, bl, lp, d)

    full.update(finish_gather("gather_rest_wait", rest_names, gather_2, o_scan))
    w_qb = jnp.pad(full["w_q_b"].reshape(ql, nm, HEAD + ROPE), ((0, 0), (0, 0), (0, QK_PAD - HEAD - ROPE))
                   ).reshape(ql, nm * QK_PAD)
    w_kvb = full["w_kv_b"]

    def hg_out_fn(o, hg, g):
        ov = jnp.concatenate([_rms(o[:, h * HEAD:(h + 1) * HEAD], g) for h in range(nh)], axis=1) * _silu(hg)
        return ov, ov

    y_a, o_hg = matmul_fused("hgrn_out_y_a", hg_out_fn, [(o_scan, d, 0), (proj_main, d, 3)], [hg_norm_g],
                             _bf(full["w_hg_o"]), [(0, d)], "nn", [(d, BF16)], tm=512)

    def seq_tile(i):
        return i % tiles_seq

    tables = [(t_cos, HEAD, 0, seq_tile), (t_up, HEAD, 0, seq_tile), (t_dn, HEAD, 0, seq_tile)]

    def q_norm_fn(cq, cos, s_up, s_dn, g):
        cn = _rms(cq, g)
        return cn, cn

    def q_rope_fn(products, vals):
        qf = products[0] * scale
        cos, s_up, s_dn = vals[1:4]
        qs = []
        for h in range(nm):
            qs += [qf[:, h * QK_PAD:h * QK_PAD + HEAD], _rope(qf[:, h * QK_PAD + HEAD:(h + 1) * QK_PAD], cos, s_up, s_dn)]
        return [jnp.concatenate(qs, axis=1)], []

    q_cat, qn = matmul_fused("q_norm_up_rope", q_norm_fn, [(proj_mla, ql, 0)] + tables, [q_a_norm_g], w_qb,
                             [(0, ql)], "nn", [(ql, BF16)], epilogue=q_rope_fn, epi_outs=[(nm * QK_PAD, BF16)])

    def kv_norm_fn(ckv, kpe, cos, s_up, s_dn, g):
        cn = _rms(ckv, g)
        return cn, cn

    def kv_rope_fn(products, vals):
        kvf = products[0]
        kpe_r = _rope(vals[1], *vals[2:5])
        ks, vs = [], []
        for h in range(nm):
            ks += [kvf[:, h * QK_PAD:h * QK_PAD + HEAD], kpe_r]
            vs += [kvf[:, h * QK_PAD + HEAD:(h + 1) * QK_PAD]]
        return [jnp.concatenate(ks, axis=1), jnp.concatenate(vs, axis=1)], []

    kpe_blk = (ql + kvl) // HEAD
    k_cat, v_att, kvn = matmul_fused("kv_norm_up_rope", kv_norm_fn,
                                     [(proj_mla, kvl, 1), (proj_mla, HEAD, kpe_blk)] + tables, [kv_a_norm_g], w_kvb,
                                     [(0, kvl)], "nn", [(kvl, BF16)], epilogue=kv_rope_fn,
                                     epi_outs=[(nm * QK_PAD, BF16), (nm * HEAD, BF16)])
    at = _attn_tile(lp)
    v_t = v_att.reshape(bl, lp // at, at, nm, HEAD).transpose(0, 3, 1, 4, 2)
    k_t = k_cat.reshape(bl, lp // at, at, nm, QK_PAD).transpose(0, 3, 1, 4, 2)
    o_mla, lse = attn_fwd_t(q_cat, k_cat, v_t, bl, lp, nm)
    y_b = matmul("y_b", o_mla, _bf(full["w_mla_o"]), "nn", out_dtype=BF16)

    def gate_fn(ya, yb, ga, gb, bias):
        zv = _sigmoid(ga + bias[:, :d]) * ya + _sigmoid(gb + bias[:, d:]) * yb
        return zv, zv

    mixed, z = matmul_fused("gate_mix_out", gate_fn,
                            [(y_a, d, 0), (y_b, d, 0), (proj_main, d, 4), (proj_main, d, 5)], [b_gate],
                            _bf(full["w_out"]), [(0, d)], "nn", [(d, BF16)], tm=512)

    def mid_fn(h, mx, g_post, g_pre):
        h1v = h + _rms(mx, g_post)
        u2v = _rms(h1v, g_pre)
        return u2v, h1v, u2v

    gu, h1, u2 = matmul_fused("norm_mid_ffn_in", mid_fn, [(h0, d, 0), (mixed, d, 0)], [mix_post_g, ffn_pre_g],
                              _bf(full["w_ffn_in"]), [(0, d)], "nn", [(d, F32), (d, BF16)])
    def swiglu_fn(gt, up):
        a = _silu(gt) * up
        return a, a

    f_out, act = matmul_fused("swiglu_ffn_out", swiglu_fn, [(gu, ffn, 0), (gu, ffn, 1)], [],
                              _bf(full["w_ffn_out"]), [(0, ffn)], "nn", [(ffn, BF16)])

    def loss_fn(h1v, fv, tg, realv, g_post):
        h2 = h1v + _rms(fv, g_post)
        diff = (h2 - tg) * realv
        part = jnp.broadcast_to(0.5 * jnp.sum(diff * diff, keepdims=True) / d, (1, HEAD))
        dy = diff / d
        df, dg = _rms_bwd(fv, g_post, dy)
        return df, dy, df, part, dg

    d_act, dy, df, loss_part, g_ffn_post = matmul_fused(
        "loss_head_d_act", loss_fn,
        [(h1, d, 0), (f_out, d, 0), (loss_target.reshape(bl * seq, d), d, 0, real_block), (real, d, 0, seq_tile)],
        [ffn_post_g], _bf(full["w_ffn_out"]), [(0, d)], "nt", [(d, BF16), (d, BF16)],
        acc_outs=[(1, HEAD), (1, d)])
    grads = {}
    grads["w_ffn_out"] = matmul("gw_ffn_out", act, df, "tn")

    def swiglu_bwd_fn(gt, up, da):
        dgt, dup = da * up * _silu_grad(gt), da * _silu(gt)
        return dgt, dup, jnp.concatenate([dgt, dup], axis=1)

    du2, dgu = matmul_fused("swiglu_bwd_d_u2", swiglu_bwd_fn, [(gu, ffn, 0), (gu, ffn, 1), (d_act, ffn, 0)], [],
                            _bf(full["w_ffn_in"]), [(0, ffn), (ffn, 2 * ffn)], "nt", [(2 * ffn, BF16)])
    grads["w_ffn_in"] = matmul("gw_ffn_in", u2, dgu, "tn")

    def mid_bwd_fn(dyv, h1v, du2v, mx, g_pre, g_post):
        dx, dg_pre = _rms_bwd(h1v, g_pre, du2v)
        dh1 = dyv + dx
        dmx, dg_post = _rms_bwd(mx, g_post, dh1)
        return dmx, dh1, dmx, dg_pre, dg_post

    dz, dh1, dmixed, g_ffn_pre, g_mix_post = matmul_fused(
        "norm_mid_bwd_d_z", mid_bwd_fn, [(dy, d, 0), (h1, d, 0), (du2, d, 0), (mixed, d, 0)],
        [ffn_pre_g, mix_post_g], _bf(full["w_out"]), [(0, d)], "nt", [(d, BF16), (d, BF16)], tm=512,
        acc_outs=[(1, d), (1, d)])
    grads["w_out"] = matmul("gw_out", z, dmixed, "tn")

    def gate_bwd_fn(dzv, ya, yb, ga, gb, bias):
        sa, sb = _sigmoid(ga + bias[:, :d]), _sigmoid(gb + bias[:, d:])
        dga = dzv * ya * sa * (1.0 - sa)
        dgb = dzv * yb * sb * (1.0 - sb)
        dgates = jnp.concatenate([dga, dgb], axis=1)
        dya, dyb = dzv * sa, dzv * sb
        return dya, dyb, dya, dyb, dgates, jnp.sum(dgates, axis=0, keepdims=True)

    def hg_out_bwd_fn(products, vals):
        do, o, hg, g = products[0], vals[5], vals[6], vals[8]
        dn = do * _silu(hg)
        dos, dgs, ons = [], 0.0, []
        for h in range(nh):
            sl = slice(h * HEAD, (h + 1) * HEAD)
            dx, dg = _rms_bwd(o[:, sl], g, dn[:, sl])
            dos.append(dx)
            dgs = dgs + dg
            ons.append(_rms(o[:, sl], g))
        dhg_v = do * jnp.concatenate(ons, axis=1) * _silu_grad(hg)
        return [jnp.concatenate(dos, axis=1), products[1], dhg_v], [dgs]

    do_scan, do_mla, dhg, dy_a, dy_b, dgates, g_b_gate, g_hg_norm = matmul_fused(
        "gate_mix_bwd_d_o", lambda dzv, ya, yb, ga, gb, o, hg, bias, g: gate_bwd_fn(dzv, ya, yb, ga, gb, bias),
        [(dz, d, 0), (y_a, d, 0), (y_b, d, 0), (proj_main, d, 4), (proj_main, d, 5), (o_scan, d, 0),
         (proj_main, d, 3)], [b_gate, hg_norm_g],
        [_bf(full["w_hg_o"]), _bf(full["w_mla_o"])], [(0, 0, d), (1, 0, d)], "nt",
        [(d, BF16), (d, BF16), (2 * d, BF16)], acc_outs=[(1, 2 * d), (1, HEAD)],
        epilogue=hg_out_bwd_fn, epi_outs=[(d, BF16), (d, BF16), (d, BF16)])
    grads["w_hg_o"] = matmul("gw_hg_o", o_hg, dy_a, "tn")
    grads["w_mla_o"] = matmul("gw_mla_o", o_mla, dy_b, "tn")

    early = ("w_hg_o", "w_mla_o", "w_out", "w_ffn_in", "w_ffn_out")
    late = ("w_in", "w_q_b", "w_kv_b")

    def start_grads(name, names):
        sends = [_bf(jnp.stack([split_full(n, grads[n], s) for s in range(4)])) for n in names]
        mines = []
        for n in names:
            r, c = wts[n].shape
            axis, size = (1, c) if n in COL_SHARDED else (0, r)
            mines.append(lax.dynamic_slice_in_dim(grads[n], my_chip * size, size, axis=axis))
        handle, token = push_start(name, sends, per_chip=True)
        return handle, token, mines

    def finish_grads(tag, names, started, after):
        handle, _, mines = started
        _, landed = push_wait(f"grads_{tag}_wait", handle, after, per_chip=True)
        parts = []
        for n, mine, land in zip(names, mines, landed):
            r, c = mine.shape
            tr = _tile(r, 256, 16)
            land2 = land.reshape(3 * r, c)
            parts.append(rowwise(f"sum_chips_{n}", lambda a, r0, r1, r2: a + r0 + r1 + r2,
                                 [(mine, c, 0)] + [(land2, c, 0, k * (r // tr)) for k in range(3)],
                                 [], [], [(c, F32)], tm=tr)[0])
        sibs = swap_with_sibling(f"swap_{tag}", parts)
        return {n: [p, s] for n, p, s in zip(names, parts, sibs)}

    grads_early = start_grads("grads_early_start", early)
    token_a = grads_early[1]

    dhq, dhf, dhi, g_lb = hgrn_bwd(proj_main, lb + token_a[0, 0], hg_consts, states, a_mats, do_scan, bl, lp, d)

    dq_cat, dk_cat, dv_att = attn_bwd_t(q_cat, k_cat, k_t, v_att, o_mla, do_mla, lse, bl, lp, nm)

    def mla_prep_bwd_fn(dqc, dkc, dvv, cos, s_up, s_dn, cq, ckv, gq, gk):
        dqc = dqc * scale
        dqs, dkvs = [], []
        for h in range(nm):
            dqs += [dqc[:, h * QK_PAD:h * QK_PAD + HEAD],
                    _rope_bwd(dqc[:, h * QK_PAD + HEAD:(h + 1) * QK_PAD], cos, s_up, s_dn)]
            dkvs += [dkc[:, h * QK_PAD:h * QK_PAD + HEAD], dvv[:, h * HEAD:(h + 1) * HEAD]]
        dqf, dkvf = jnp.concatenate(dqs, axis=1), jnp.concatenate(dkvs, axis=1)
        return dqf, dkvf, dqf, dkvf

    def mla_norms_bwd_fn(products, vals):
        dkc, cos, s_up, s_dn, cq, ckv, gq, gk = vals[1], vals[3], vals[4], vals[5], vals[6], vals[7], vals[8], vals[9]
        dkpe = 0.0
        for h in range(nm):
            dkpe = dkpe + dkc[:, h * QK_PAD + HEAD:(h + 1) * QK_PAD]
        dcq, dgq = _rms_bwd(cq, gq, products[0])
        dckv, dgk = _rms_bwd(ckv, gk, products[1])
        return [jnp.concatenate([dcq, dckv, _rope_bwd(dkpe, cos, s_up, s_dn)], axis=1)], [dgq, dgk]

    dmla, dq_full, dkv_full, g_q_norm, g_kv_norm = matmul_fused(
        "mla_prep_bwd_d_norms", mla_prep_bwd_fn,
        [(dq_cat, nm * QK_PAD, 0), (dk_cat, nm * QK_PAD, 0), (dv_att, nm * HEAD, 0)] + tables
        + [(proj_mla, ql, 0), (proj_mla, kvl, 1)], [q_a_norm_g, kv_a_norm_g],
        [w_qb, w_kvb], [(0, 0, nm * QK_PAD), (1, 0, nm * QK_PAD)], "nt",
        [(nm * QK_PAD, BF16), (nm * QK_PAD, BF16)], acc_outs=[(1, ql), (1, kvl)],
        epilogue=mla_norms_bwd_fn, epi_outs=[(mla_w, BF16)])
    g_wqb = matmul("gw_q_b", qn, dq_full, "tn")
    grads["w_q_b"] = g_wqb.reshape(ql, nm, QK_PAD)[:, :, :HEAD + ROPE].reshape(ql, nm * (HEAD + ROPE))
    grads["w_kv_b"] = matmul("gw_kv_b", kvn, dkv_full, "tn")

    d_pieces = [dhq, dhf, dhi, dhg, dgates, dmla]
    gw_parts = [matmul(f"gw_in_{k}", u1, dp, "tn") for k, dp in enumerate(d_pieces)]
    grads["w_in"] = jnp.concatenate(gw_parts[:4] + [gw_parts[5][:, :ql + kvl + ROPE], gw_parts[4]], axis=1)
    grads_late = start_grads("grads_late_start", late)
    w_mla_after = w_mla + grads_late[1][0, 0].astype(BF16)
    w_pieces = [w_main[:, 0:d], w_main[:, d:2 * d], w_main[:, 2 * d:3 * d], w_main[:, 3 * d:4 * d],
                w_main[:, 4 * d:6 * d], w_mla_after]
    du1 = matmul("d_u1", d_pieces, w_pieces, "nt", out_dtype=BF16)

    def first_bwd_fn(dh1v, h, du1v, is_meta, g):
        dx, dg = _rms_bwd(h, g, du1v)
        dh0v = dh1v + dx
        return dh0v, dg, dh0v * jnp.tile(is_meta, (1, d // HEAD))

    grad_x, g_mix_pre, meta_tile = rowwise(
        "norm_mix_pre_bwd", first_bwd_fn, [(dh1, d, 0), (h0, d, 0), (du1, d, 0)], [meta_rows], [mix_pre_g],
        [(d, F32, bl * seq, real_block)], [(1, d), (SEQ_BLOCK, d)])
    grad_x = grad_x.reshape(bl, seq, d)

    g_parts = finish_grads("early", early, grads_early, g_mix_pre)
    updates = {}

    def update(n, parts):
        w2 = wts[n].reshape(-1, wts[n].shape[-1])
        updates[n] = adamw("adamw_" + n, w2, [p.reshape(w2.shape) for p in parts], mom_m[n].reshape(w2.shape),
                           mom_v[n].reshape(w2.shape))

    for n in early:
        update(n, g_parts[n])
    g_parts = finish_grads("late", late, grads_late, updates[early[-1]][0])
    for n in late:
        update(n, g_parts[n])
    p0 = lb_soft[0:1]
    g_lb_logits = jnp.concatenate([g_lb * p0 * (1.0 - p0), -g_lb * p0 * (1.0 - p0)], axis=0)

    def row_of(vec):
        return vec.reshape(-1, d) if vec.size >= d else jnp.pad(vec.reshape(1, -1), ((0, 0), (0, d - vec.size)))

    small_parts = dict(b_gate=g_b_gate, lb_logits=g_lb_logits, hg_norm_g=g_hg_norm, q_a_norm_g=g_q_norm,
                       kv_a_norm_g=g_kv_norm, mix_pre_g=g_mix_pre, mix_post_g=g_mix_post, ffn_pre_g=g_ffn_pre,
                       ffn_post_g=g_ffn_post)
    g_meta = meta_tile[PAD_FRONT:PAD_FRONT + N_META]
    small_rows = [row_of(small_parts[n]) for n in SMALL] + [row_of(g_meta)]
    n_small = sum(r.shape[0] for r in small_rows)
    small = jnp.pad(jnp.concatenate(small_rows, axis=0), ((0, -(-n_small // 8) * 8 - n_small), (0, 0)))
    all_small = gather_small(small)
    small_t = small.shape[0]

    def sum8_fn(*slabs):
        acc = slabs[0]
        for s in slabs[1:]:
            acc = acc + s
        return acc

    (g_small,) = rowwise("sum_small", sum8_fn, [(all_small.reshape(8 * small_t, d), d, 0, k) for k in range(8)],
                         [], [], [(d, F32)], tm=small_t, n_rows=small_t)

    off = 0
    for n, part in zip(SMALL, small_rows[:-1]):
        rows = part.shape[0]
        update(n, [g_small[off:off + rows, :d].reshape(-1)[:wts[n].size]])
        off += rows
    update("meta_tokens", [lax.dynamic_slice_in_dim(g_small[off:off + N_META, :d], my_chip * mcols, mcols, axis=1)])

    loss = lax.psum(loss_part[0, 0], ("x", "y", "c"))

    def shaped(n, a):
        return a.reshape((1,) + wts[n].shape) if n in BIG else a.reshape(wts[n].shape)

    return (loss, grad_x, *[shaped(n, updates[n][k]) for k in range(4) for n in WEIGHTS])
```

```python
import functools
import math

import jax
import jax.numpy as jnp
from jax import lax
from jax.experimental import pallas as pl
from jax.experimental.pallas import tpu as pltpu

F32 = jnp.float32
BF16 = jnp.bfloat16
MESH = pl.DeviceIdType.MESH

N_META = 16
NORM_EPS = 1e-6
HEAD = 128
ROPE = 64
ROPE_HALF = ROPE // 2
QK_PAD = 2 * HEAD
ROPE_THETA = 10000.0
SEQ_BLOCK = 256
PAD_FRONT = SEQ_BLOCK - N_META
NEG = -1e30
VMEM_LIMIT = 56 * 1024 * 1024
ATTN_HEADS_PER_STEP = 1
ATTN_TILE_MAX = 768

ADAM_LR, ADAM_B1, ADAM_B2, ADAM_EPS, ADAM_WD, ADAM_STEP = 0.001, 0.9, 0.999, 1e-08, 0.01, 10

BIG = ("w_in", "w_hg_o", "w_q_b", "w_kv_b", "w_mla_o", "w_out", "w_ffn_in", "w_ffn_out")
COL_SHARDED = ("w_in", "w_q_b", "w_kv_b", "w_ffn_in")
SMALL = ("b_gate", "lb_logits", "hg_norm_g", "q_a_norm_g", "kv_a_norm_g", "mix_pre_g", "mix_post_g",
         "ffn_pre_g", "ffn_post_g")
WEIGHTS = ("meta_tokens", "w_in", "b_gate", "lb_logits", "hg_norm_g", "w_hg_o", "q_a_norm_g", "w_q_b",
           "kv_a_norm_g", "w_kv_b", "w_mla_o", "w_out", "mix_pre_g", "mix_post_g", "ffn_pre_g", "ffn_post_g",
           "w_ffn_in", "w_ffn_out")


def _tile(n, cap, unit=128):
    if n <= cap:
        return n
    best = None
    for t in range(unit, cap + 1, unit):
        if n % t == 0:
            best = t
    assert best is not None, (n, cap, unit)
    return best


def _sigmoid(x):
    return 1.0 / (1.0 + jnp.exp(-x))


def _bf(x):
    return x.astype(BF16)


def rowwise(name, fn, row_ins, seq_tabs, consts, row_outs, acc_outs=(), tm=SEQ_BLOCK, n_rows=None):
    t_rows = row_ins[0][0].shape[0] if n_rows is None else n_rows
    nt = t_rows // tm
    assert t_rows % tm == 0
    n_in = len(row_ins) + len(seq_tabs) + len(consts)
    n_row = len(row_outs)

    def body(*refs):
        vals = [r[...].astype(F32) for r in refs[:n_in]]
        res = fn(*vals)
        if not isinstance(res, (tuple, list)):
            res = (res,)
        outs = refs[n_in:]
        for k in range(n_row):
            outs[k][...] = res[k].astype(outs[k].dtype)
        if acc_outs:
            @pl.when(pl.program_id(0) == 0)
            def _():
                for k in range(len(acc_outs)):
                    outs[n_row + k][...] = jnp.zeros_like(outs[n_row + k])

            for k in range(len(acc_outs)):
                outs[n_row + k][...] += res[n_row + k]

    row_ins = [tuple(e) + (0,) * (4 - len(e)) for e in row_ins]
    in_specs = [pl.BlockSpec((tm, w), functools.partial(lambda i, j, ro: (ro(i) if callable(ro) else i + ro, j),
                                                        j=j, ro=ro)) for (_, w, j, ro) in row_ins]
    for tab in seq_tabs:
        per = tab.shape[0] // tm
        in_specs.append(pl.BlockSpec((tm, tab.shape[1]), functools.partial(lambda i, per: (i % per, 0), per=per)))
    for c in consts:
        in_specs.append(pl.BlockSpec(c.shape, lambda i: (0, 0)))
    row_outs = [tuple(e) + (t_rows, None)[len(e) - 2:] for e in row_outs]
    out_specs = [pl.BlockSpec((tm, w), functools.partial(lambda i, rm: (i if rm is None else rm(i), 0), rm=rm))
                 for (w, _, _, rm) in row_outs]
    out_specs += [pl.BlockSpec(s, lambda i: (0, 0)) for s in acc_outs]
    out_shape = [jax.ShapeDtypeStruct((rows, w), dt) for (w, dt, rows, _) in row_outs]
    out_shape += [jax.ShapeDtypeStruct(s, F32) for s in acc_outs]
    res = pl.pallas_call(
        body, name=name, grid=(nt,), in_specs=in_specs, out_specs=out_specs, out_shape=out_shape,
        compiler_params=pltpu.CompilerParams(dimension_semantics=("arbitrary",)),
    )(*[e[0] for e in row_ins], *seq_tabs, *consts)
    return res


def matmul(name, a, b, mode, out_dtype=F32):
    if mode != "tn":
        return _matmul_resident(name, a if isinstance(a, (list, tuple)) else [a],
                                b if isinstance(b, (list, tuple)) else [b], mode, out_dtype)
    kdim, m = a.shape
    n = b.shape[1]
    tn = _tile(n, 1536)
    tm, tk = _tile(m, 1408 if tn <= 1024 else 1024), _tile(kdim, 1536)
    nk = kdim // tk

    def body(a_ref, b_ref, o_ref, acc_ref):
        k = pl.program_id(2)

        @pl.when(k == 0)
        def _():
            acc_ref[...] = jnp.zeros_like(acc_ref)

        acc_ref[...] += lax.dot_general(a_ref[...], b_ref[...], TN_DIMS, preferred_element_type=F32)

        @pl.when(k == nk - 1)
        def _():
            o_ref[...] = acc_ref[...].astype(o_ref.dtype)

    return pl.pallas_call(
        body, name=name, grid=(m // tm, n // tn, nk),
        in_specs=[pl.BlockSpec((tk, tm), lambda i, j, k: (k, i)), pl.BlockSpec((tk, tn), lambda i, j, k: (k, j))],
        out_specs=pl.BlockSpec((tm, tn), lambda i, j, k: (i, j)),
        out_shape=jax.ShapeDtypeStruct((m, n), out_dtype),
        scratch_shapes=[pltpu.VMEM((tm, tn), F32)],
        compiler_params=pltpu.CompilerParams(dimension_semantics=("arbitrary", "arbitrary", "arbitrary"),
                                             vmem_limit_bytes=VMEM_LIMIT),
    )(a, b)


def _matmul_resident(name, a_list, b_list, mode, out_dtype):
    m = a_list[0].shape[0]
    n = b_list[0].shape[1] if mode == "nn" else b_list[0].shape[0]
    k_total = sum(a.shape[1] for a in a_list)
    out_bytes = 2 if out_dtype == BF16 else 4
    budget = VMEM_LIMIT - 4 * k_total * n - (6 << 20)
    tm = 1024
    while tm > 128 and 2 * tm * (2 * k_total + out_bytes * n) > budget:
        tm //= 2
    tm = _tile(m, tm)
    cn = _tile(n, 1024)
    npairs = len(a_list)

    def body(*refs):
        a_refs, b_refs, o_ref = refs[:npairs], refs[npairs:2 * npairs], refs[2 * npairs]
        for c in range(n // cn):
            acc = None
            for a_ref, b_ref in zip(a_refs, b_refs):
                if mode == "nn":
                    part = jnp.dot(a_ref[...], b_ref[:, pl.ds(c * cn, cn)], preferred_element_type=F32)
                else:
                    part = lax.dot_general(a_ref[...], b_ref[pl.ds(c * cn, cn), :], NT_DIMS,
                                           preferred_element_type=F32)
                acc = part if acc is None else acc + part
            o_ref[:, pl.ds(c * cn, cn)] = acc.astype(o_ref.dtype)

    in_specs = [pl.BlockSpec((tm, a.shape[1]), lambda i: (i, 0)) for a in a_list]
    in_specs += [pl.BlockSpec(b.shape, lambda i: (0, 0)) for b in b_list]
    return pl.pallas_call(
        body, name=name, grid=(m // tm,), in_specs=in_specs,
        out_specs=pl.BlockSpec((tm, n), lambda i: (i, 0)),
        out_shape=jax.ShapeDtypeStruct((m, n), out_dtype),
        compiler_params=pltpu.CompilerParams(dimension_semantics=("arbitrary",), vmem_limit_bytes=VMEM_LIMIT),
    )(*a_list, *b_list)


def matmul_fused(name, fn, row_ins, consts, weight, pieces, mode, extra_outs, out_dtype=BF16, tm=256, acc_outs=(),
                 epilogue=None, epi_outs=()):
    row_ins = [tuple(e) + (0,) * (4 - len(e)) for e in row_ins]
    t_rows = row_ins[0][0].shape[0]
    tm = _tile(t_rows, tm)
    several = isinstance(weight, (list, tuple))
    weights = list(weight) if several else [weight]
    n_in = len(row_ins) + len(consts)
    n_w = len(weights)
    n_parts = len(pieces)
    n_mm = len(epi_outs) if epilogue is not None else (n_parts if several else 1)
    n_row_out = n_mm + len(extra_outs)

    def width(w_arr):
        return w_arr.shape[1] if mode == "nn" else w_arr.shape[0]

    def body(*refs):
        w_hbms = refs[n_in:n_in + n_w]
        outs = refs[n_in + n_w:n_in + n_w + n_row_out + len(acc_outs)]
        w_refs, sems = refs[-n_w - 1:-1], refs[-1]

        @pl.when(pl.program_id(0) == 0)
        def _():
            cps = [pltpu.make_async_copy(w_hbms[k], w_refs[k], sems.at[k]) for k in range(n_w)]
            for cp in cps:
                cp.start()
            for cp in cps:
                cp.wait()
            for k in range(len(acc_outs)):
                outs[n_row_out + k][...] = jnp.zeros_like(outs[n_row_out + k])

        vals = [r[...].astype(F32) for r in refs[:n_in]]
        res = fn(*vals)
        products = []
        for a_p, piece in zip(res[:n_parts], pieces):
            w_ref, (k0, k1) = (w_refs[piece[0]], piece[1:]) if several else (w_refs[0], piece)
            if mode == "nn":
                products.append(jnp.dot(_bf(a_p), w_ref[pl.ds(k0, k1 - k0), :], preferred_element_type=F32))
            else:
                products.append(lax.dot_general(_bf(a_p), w_ref[:, pl.ds(k0, k1 - k0)], NT_DIMS,
                                                preferred_element_type=F32))
        if not several:
            products = [functools.reduce(lambda u, w: u + w, products)]
        sums = list(res[n_parts + len(extra_outs):])
        if epilogue is not None:
            products, more_sums = epilogue(products, vals)
            sums += list(more_sums)
        for p, val in enumerate(products):
            outs[p][...] = val.astype(outs[p].dtype)
        for o_ref, val in zip(outs[n_mm:n_row_out], res[n_parts:]):
            o_ref[...] = val.astype(o_ref.dtype)
        for k in range(len(acc_outs)):
            outs[n_row_out + k][...] += sums[k]

    in_specs = [pl.BlockSpec((tm, w), functools.partial(lambda i, j, ro: (ro(i) if callable(ro) else i + ro, j),
                                                        j=j, ro=ro)) for (_, w, j, ro) in row_ins]
    in_specs += [pl.BlockSpec(c.shape, lambda i: (0, 0)) for c in consts]
    in_specs += [pl.BlockSpec(memory_space=pl.ANY)] * n_w
    if epilogue is not None:
        widths = list(epi_outs) + list(extra_outs)
    elif several:
        widths = [(width(weights[piece[0]]), out_dtype) for piece in pieces] + list(extra_outs)
    else:
        widths = [(width(weights[0]), out_dtype)] + list(extra_outs)
    return pl.pallas_call(
        body, name=name, grid=(t_rows // tm,), in_specs=in_specs,
        out_specs=[pl.BlockSpec((tm, w), lambda i: (i, 0)) for (w, _) in widths]
        + [pl.BlockSpec(s, lambda i: (0, 0)) for s in acc_outs],
        out_shape=[jax.ShapeDtypeStruct((t_rows, w), dt) for (w, dt) in widths]
        + [jax.ShapeDtypeStruct(s, F32) for s in acc_outs],
        scratch_shapes=[pltpu.VMEM(w_arr.shape, w_arr.dtype) for w_arr in weights] + [pltpu.SemaphoreType.DMA((n_w,))],
        compiler_params=pltpu.CompilerParams(dimension_semantics=("arbitrary",), vmem_limit_bytes=VMEM_LIMIT),
    )(*[e[0] for e in row_ins], *consts, *weights)


def _rms(x, g):
    r = lax.rsqrt(jnp.mean(x * x, axis=-1, keepdims=True) + NORM_EPS)
    return x * r * g


def _rms_bwd(x, g, dy):
    r = lax.rsqrt(jnp.mean(x * x, axis=-1, keepdims=True) + NORM_EPS)
    xh = x * r
    dyg = dy * g
    dx = r * (dyg - xh * jnp.mean(dyg * xh, axis=-1, keepdims=True))
    return dx, jnp.sum(dy * xh, axis=0, keepdims=True)


def _silu(x):
    return x * _sigmoid(x)


def _silu_grad(x):
    s = _sigmoid(x)
    return s * (1.0 + x * (1.0 - s))


def _rope(xs, cos, s_up, s_dn):
    return xs * cos + pltpu.roll(xs, ROPE_HALF, 1) * s_up + pltpu.roll(xs, HEAD - ROPE_HALF, 1) * s_dn


def _rope_bwd(dy, cos, s_up, s_dn):
    return dy * cos + pltpu.roll(dy * s_up, HEAD - ROPE_HALF, 1) + pltpu.roll(dy * s_dn, ROPE_HALF, 1)


HG_SUB = 128
HG_LEVELS = 7
HG_E_ROWS = (HG_LEVELS + 1) * HG_SUB
HG_BWD_GROUP = 6
TN_DIMS = (((0,), (0,)), ((), ()))
NT_DIMS = (((1,), (1,)), ((), ()))


def _hg_constants():
    import numpy as np
    n = HG_SUB
    r = np.arange(n)[:, None]
    c = np.arange(n)[None, :]
    cs, ps = [], []
    for lvl in range(HG_LEVELS):
        m = (n // 2) >> lvl
        upper = (r % (2 * m)) >= m
        mid = (r // (2 * m)) * (2 * m) + m - 1
        cs.append(np.where(upper, (c > mid) & (c <= r), (c > r) & (c <= mid)))
        ps.append(((r // (2 * m)) == (c // (2 * m))) & upper & ((c % (2 * m)) < m))
    cs.append(c <= r)
    cs.append(np.ones((8, n), bool))
    cstack = np.concatenate(cs, 0).astype(np.float32)
    pstack = np.concatenate(ps, 0).astype(np.float32)
    pstack_t = np.concatenate([p.T for p in ps], 0).astype(np.float32)
    return (jnp.asarray(cstack, BF16), jnp.asarray(cstack[:HG_E_ROWS].T, BF16), jnp.asarray(pstack, F32),
            jnp.asarray(pstack_t, F32))


def _split_dot(c_bf, x):
    hi = _bf(x)
    lo = _bf(x - hi.astype(F32))
    r2 = jnp.dot(c_bf, jnp.concatenate([hi, lo], axis=1), preferred_element_type=F32)
    return r2[:, :HEAD] + r2[:, HEAD:]


def _hg_gates(hq, hf, lb):
    sq = _sigmoid(hq)
    sg = _sigmoid(hf)
    fg = lb + (1.0 - lb) * sg
    return sq, hq * sq, sg, fg, 1.0 - fg, jnp.log(fg)


def hgrn_fwd(proj_main, lb, consts, bl, lp, d):
    nh = d // HEAD
    rows_blk = _tile(lp, 768, SEQ_BLOCK)
    nb = lp // rows_blk
    spb = rows_blk // HG_SUB
    cstack, _, pstack, _ = consts

    def body(hq_ref, hf_ref, hi_ref, lb_ref, c_ref, p_ref, o_ref, st_ref, a_ref, s_ref):
        j = pl.program_id(2)

        @pl.when(j == 0)
        def _():
            s_ref[...] = jnp.zeros_like(s_ref)

        lbv = lb_ref[...]
        cs = c_ref[...]
        rows = [pl.ds(s * HG_SUB, HG_SUB) for s in range(spb)]
        gates = [_hg_gates(hq_ref[r, :].astype(F32), hf_ref[r, :].astype(F32), lbv) for r in rows]
        qs, ks = [g_[1] for g_ in gates], [g_[4] for g_ in gates]
        vs = [hi_ref[r, :].astype(F32) for r in rows]
        es = [_split_dot(cs, g_[5]) for g_ in gates]
        a_acc = [jnp.zeros((HG_SUB, HG_SUB), F32) for _ in rows]
        for lvl in range(HG_LEVELS):
            for s in range(spb):
                x = jnp.exp(es[s][lvl * HG_SUB:(lvl + 1) * HG_SUB])
                a_acc[s] = a_acc[s] + p_ref[pl.ds(lvl * HG_SUB, HG_SUB), :] * lax.dot_general(
                    _bf(qs[s] * x), _bf(ks[s] * x), NT_DIMS, preferred_element_type=F32)
        o_intra, qbs, kds, e_lasts = [], [], [], []
        for s in range(spb):
            a_bf = _bf(a_acc[s])
            a_ref[0, 0, s] = a_bf
            bc = es[s][HG_LEVELS * HG_SUB:HG_E_ROWS]
            b_last = jnp.tile(es[s][HG_E_ROWS:], (HG_SUB // 8, 1))
            o_intra.append(jnp.dot(a_bf, _bf(vs[s]), preferred_element_type=F32)
                           + jnp.sum(qs[s] * ks[s], axis=1, keepdims=True) * vs[s])
            qbs.append(_bf(qs[s] * jnp.exp(bc)))
            kds.append(_bf(ks[s] * jnp.exp(b_last - bc)))
            e_lasts.append(jnp.exp(b_last))
        st = s_ref[...]
        for s in range(spb):
            st_ref[0, 0, s] = st
            o_ref[rows[s], :] = (o_intra[s] + lax.dot_general(qbs[s], _bf(st), NT_DIMS, preferred_element_type=F32)
                                 ).astype(o_ref.dtype)
            st = st * e_lasts[s] + lax.dot_general(_bf(vs[s]), kds[s], TN_DIMS, preferred_element_type=F32)
        s_ref[...] = st

    def colspec(off):
        return pl.BlockSpec((rows_blk, HEAD), functools.partial(lambda h, b, j, off: (b * nb + j, off + h), off=off))

    whole = lambda arr: pl.BlockSpec(arr.shape, lambda h, b, j: (0, 0))
    return pl.pallas_call(
        body, name="hgrn_fwd", grid=(nh, bl, nb),
        in_specs=[colspec(0), colspec(nh), colspec(2 * nh), pl.BlockSpec((1, HEAD), lambda h, b, j: (0, h)),
                  whole(cstack), whole(pstack)],
        out_specs=[pl.BlockSpec((rows_blk, HEAD), lambda h, b, j: (b * nb + j, h)),
                   pl.BlockSpec((1, 1, spb, HEAD, HEAD), lambda h, b, j: (b, h, j, 0, 0)),
                   pl.BlockSpec((1, 1, spb, HG_SUB, HG_SUB), lambda h, b, j: (b, h, j, 0, 0))],
        out_shape=[jax.ShapeDtypeStruct((bl * lp, d), BF16),
                   jax.ShapeDtypeStruct((bl, nh, lp // HG_SUB, HEAD, HEAD), F32),
                   jax.ShapeDtypeStruct((bl, nh, lp // HG_SUB, HG_SUB, HG_SUB), BF16)],
        scratch_shapes=[pltpu.VMEM((HEAD, HEAD), F32)],
        compiler_params=pltpu.CompilerParams(dimension_semantics=("arbitrary", "arbitrary", "arbitrary")),
    )(proj_main, proj_main, proj_main, lb, cstack, pstack)


def hgrn_bwd(proj_main, lb, consts, states, a_mats, do_scan, bl, lp, d):
    nh = d // HEAD
    rows_blk = _tile(lp, 768, SEQ_BLOCK)
    nb = lp // rows_blk
    spb = rows_blk // HG_SUB
    cstack, cstack_t = consts[0], consts[1]
    pstack, pstack_t = _bf(consts[2]), _bf(consts[3])

    def body(hq_ref, hf_ref, hi_ref, lb_ref, c_ref, ct_ref, p_ref, pt_ref, st_ref, a_ref, do_ref,
             dq_ref, df_ref, di_ref, dlb_ref, ds_ref):
        b_id, j = pl.program_id(1), pl.program_id(2)
        blk = nb - 1 - j

        @pl.when(j == 0)
        def _():
            ds_ref[...] = jnp.zeros_like(ds_ref)

        @pl.when((j == 0) & (b_id == 0))
        def _():
            dlb_ref[...] = jnp.zeros_like(dlb_ref)

        lbv = lb_ref[...]
        cs = c_ref[...]
        cst = ct_ref[...]

        dlb = jnp.zeros((1, HEAD), F32)
        for first in reversed(range(0, spb, HG_BWD_GROUP)):
            dlb = dlb + _hg_group_bwd(list(range(first, min(first + HG_BWD_GROUP, spb))), lbv, cs, cst, hq_ref,
                                      hf_ref, hi_ref, st_ref, a_ref, do_ref, p_ref, pt_ref, dq_ref, df_ref, di_ref,
                                      ds_ref)
        dlb_ref[...] += dlb

    def _hg_group_bwd(ids, lbv, cs, cst, hq_ref, hf_ref, hi_ref, st_ref, a_ref, do_ref, p_ref, pt_ref, dq_ref,
                      df_ref, di_ref, ds_ref):
        rng = range(len(ids))
        rows = [pl.ds(s * HG_SUB, HG_SUB) for s in ids]
        hqs = [hq_ref[r, :].astype(F32) for r in rows]
        gates = [_hg_gates(hqs[s], hf_ref[rows[s], :].astype(F32), lbv) for s in rng]
        sqs, qs, sgs, fgs, ks = ([g_[i] for g_ in gates] for i in range(5))
        vs = [hi_ref[r, :].astype(F32) for r in rows]
        dos = [do_ref[r, :].astype(F32) for r in rows]
        sts = [st_ref[0, 0, s] for s in ids]
        es = [_split_dot(cs, g_[5]) for g_ in gates]
        bcs = [e[HG_LEVELS * HG_SUB:HG_E_ROWS] for e in es]
        b_lasts = [jnp.tile(e[HG_E_ROWS:], (HG_SUB // 8, 1)) for e in es]
        ebs = [jnp.exp(bc) for bc in bcs]
        qbs = [qs[s] * ebs[s] for s in rng]
        ers = [jnp.exp(b_lasts[s] - bcs[s]) for s in rng]
        kds = [ks[s] * ers[s] for s in rng]
        e_lasts = [jnp.exp(b) for b in b_lasts]
        do_bfs, v_bfs = [_bf(x) for x in dos], [_bf(x) for x in vs]
        das = [_bf(lax.dot_general(do_bfs[s], v_bfs[s], NT_DIMS, preferred_element_type=F32)) for s in rng]
        dats = [_bf(lax.dot_general(v_bfs[s], do_bfs[s], NT_DIMS, preferred_element_type=F32)) for s in rng]
        dqbs = [jnp.dot(do_bfs[s], _bf(sts[s]), preferred_element_type=F32) for s in rng]
        m_s = [lax.dot_general(do_bfs[s], _bf(qbs[s]), TN_DIMS, preferred_element_type=F32) for s in rng]
        dst_outs = [None] * len(ids)
        dst = ds_ref[...]
        for s in reversed(rng):
            dst_outs[s] = dst
            dst = dst * e_lasts[s] + m_s[s]
        ds_ref[...] = dst
        dst_bfs = [_bf(x) for x in dst_outs]
        d_diags = [jnp.sum(dos[s] * vs[s], axis=1, keepdims=True) for s in rng]
        dvs = [lax.dot_general(a_ref[0, 0, ids[s]], do_bfs[s], TN_DIMS, preferred_element_type=F32)
               + jnp.sum(qs[s] * ks[s], axis=1, keepdims=True) * dos[s]
               + lax.dot_general(_bf(kds[s]), dst_bfs[s], NT_DIMS, preferred_element_type=F32) for s in rng]
        dkds = [jnp.dot(v_bfs[s], dst_bfs[s], preferred_element_type=F32) for s in rng]
        dqs = [dqbs[s] * ebs[s] + d_diags[s] * ks[s] for s in rng]
        dks = [dkds[s] * ers[s] + d_diags[s] * qs[s] for s in rng]
        d_lasts = [jnp.sum(dst_outs[s] * sts[s] * e_lasts[s], axis=0, keepdims=True)
                   + jnp.sum(dkds[s] * kds[s], axis=0, keepdims=True) for s in rng]
        des = [[] for _ in rng]
        for lvl in range(HG_LEVELS):
            for s in rng:
                x = jnp.exp(es[s][lvl * HG_SUB:(lvl + 1) * HG_SUB])
                qh, kh = qs[s] * x, ks[s] * x
                dm = p_ref[pl.ds(lvl * HG_SUB, HG_SUB), :] * das[s]
                dmt = pt_ref[pl.ds(lvl * HG_SUB, HG_SUB), :] * dats[s]
                dqh = jnp.dot(dm, _bf(kh), preferred_element_type=F32)
                dkh = jnp.dot(dmt, _bf(qh), preferred_element_type=F32)
                dqs[s] = dqs[s] + dqh * x
                dks[s] = dks[s] + dkh * x
                des[s].append(dqh * qh + dkh * kh)
        dlb = jnp.zeros((1, HEAD), F32)
        for s in rng:
            des[s].append(dqbs[s] * qbs[s] - dkds[s] * kds[s])
            dg = _split_dot(cst, jnp.concatenate(des[s], axis=0)) + d_lasts[s]
            dfg = dg / fgs[s] - dks[s]
            dq_ref[rows[s], :] = (dqs[s] * (sqs[s] * (1.0 + hqs[s] * (1.0 - sqs[s])))).astype(dq_ref.dtype)
            df_ref[rows[s], :] = (dfg * (1.0 - lbv) * sgs[s] * (1.0 - sgs[s])).astype(df_ref.dtype)
            di_ref[rows[s], :] = dvs[s].astype(di_ref.dtype)
            dlb = dlb + jnp.sum(dfg * (1.0 - sgs[s]), axis=0, keepdims=True)
        return dlb

    def colspec(off):
        return pl.BlockSpec((rows_blk, HEAD),
                            functools.partial(lambda h, b, j, off: (b * nb + nb - 1 - j, off + h), off=off))

    whole = lambda arr: pl.BlockSpec(arr.shape, lambda h, b, j: (0, 0))
    mats = lambda: pl.BlockSpec((1, 1, spb, HEAD, HEAD), lambda h, b, j: (b, h, nb - 1 - j, 0, 0))
    t_rows = bl * lp
    return pl.pallas_call(
        body, name="hgrn_bwd", grid=(nh, bl, nb),
        in_specs=[colspec(0), colspec(nh), colspec(2 * nh), pl.BlockSpec((1, HEAD), lambda h, b, j: (0, h)),
                  whole(cstack), whole(cstack_t), whole(pstack), whole(pstack_t), mats(), mats(), colspec(0)],
        out_specs=[colspec(0), colspec(0), colspec(0), pl.BlockSpec((1, HEAD), lambda h, b, j: (0, h))],
        out_shape=[jax.ShapeDtypeStruct((t_rows, d), BF16)] * 3 + [jax.ShapeDtypeStruct((1, d), F32)],
        scratch_shapes=[pltpu.VMEM((HEAD, HEAD), F32)],
        compiler_params=pltpu.CompilerParams(dimension_semantics=("arbitrary", "arbitrary", "arbitrary")),
    )(proj_main, proj_main, proj_main, lb, cstack, cstack_t, pstack, pstack_t, states, a_mats, do_scan)


def _key_query_mask(key0, qry0, nk, nq_, causal):
    key = key0 + lax.broadcasted_iota(jnp.int32, (nk, 1), 0)
    if not causal:
        return key >= PAD_FRONT
    qry = qry0 + lax.broadcasted_iota(jnp.int32, (1, nq_), 1)
    return (key <= qry) & (key >= PAD_FRONT)


def _attn_tile(lp):
    return _tile(lp, ATTN_TILE_MAX, SEQ_BLOCK)


def attn_fwd_t(q_cat, k_cat, v_t, bl, lp, nm):
    tq = tk = _attn_tile(lp)
    nq = lp // tq
    hp = ATTN_HEADS_PER_STEP
    assert nm % hp == 0

    def body(q_ref, k_ref, vt_ref, o_ref, lse_ref, m_ref, l_ref, acc_ref):
        i = pl.program_id(2)
        m_ref[...] = jnp.full_like(m_ref, NEG)
        l_ref[...] = jnp.zeros_like(l_ref)
        acc_ref[...] = jnp.zeros_like(acc_ref)

        def step(c, mask):
            c0 = pl.multiple_of(c * tk, tk)
            for hh in range(hp):
                cols = pl.ds(hh * QK_PAD, QK_PAD)
                st = lax.dot_general(k_ref[pl.ds(c0, tk), cols], q_ref[:, cols], NT_DIMS,
                                     preferred_element_type=F32)
                if mask is not None:
                    st = jnp.where(_key_query_mask(c * tk, i * tq, tk, tq, mask == "causal"), st, NEG)
                m_old = m_ref[hh]
                m_new = jnp.maximum(m_old, jnp.max(st, axis=0, keepdims=True))
                alpha = jnp.exp(m_old - m_new)
                pt = jnp.exp(st - m_new)
                l_ref[hh] = alpha * l_ref[hh] + jnp.sum(pt, axis=0, keepdims=True)
                acc_ref[hh] = alpha * acc_ref[hh] + jnp.dot(vt_ref[0, hh, pl.ds(c, 1)][0], _bf(pt),
                                                            preferred_element_type=F32)
                m_ref[hh] = m_new

        def mid(c, carry):
            step(c, None)
            return carry

        @pl.when(i == 0)
        def _():
            step(0, "causal")

        @pl.when(i > 0)
        def _():
            step(0, "pad")
            lax.fori_loop(1, i, mid, 0)
            step(i, "causal")

        for hh in range(hp):
            o_ref[:, pl.ds(hh * HEAD, HEAD)] = jnp.transpose(acc_ref[hh] / l_ref[hh]).astype(o_ref.dtype)
            lse_ref[0, hh, 0] = m_ref[hh] + jnp.log(l_ref[hh])

    return pl.pallas_call(
        body, name="attn_fwd", grid=(bl, nm // hp, nq),
        in_specs=[pl.BlockSpec((tq, hp * QK_PAD), lambda b, h, i: (b * nq + i, h)),
                  pl.BlockSpec((lp, hp * QK_PAD), lambda b, h, i: (b, h)),
                  pl.BlockSpec((1, hp, nq, HEAD, tk), lambda b, h, i: (b, h, 0, 0, 0))],
        out_specs=[pl.BlockSpec((tq, hp * HEAD), lambda b, h, i: (b * nq + i, h)),
                   pl.BlockSpec((1, hp, 1, 1, tq), lambda b, h, i: (b, h, i, 0, 0))],
        out_shape=[jax.ShapeDtypeStruct((bl * lp, nm * HEAD), BF16),
                   jax.ShapeDtypeStruct((bl, nm, nq, 1, tq), F32)],
        scratch_shapes=[pltpu.VMEM((hp, 1, tq), F32), pltpu.VMEM((hp, 1, tq), F32), pltpu.VMEM((hp, HEAD, tq), F32)],
        compiler_params=pltpu.CompilerParams(dimension_semantics=("arbitrary", "arbitrary", "arbitrary")),
    )(q_cat, k_cat, v_t)


def attn_bwd_t(q_cat, k_cat, k_t, v, o, do, lse, bl, lp, nm):
    tq = tk = _attn_tile(lp)
    nq = lp // tq
    hp = ATTN_HEADS_PER_STEP
    assert nm % hp == 0

    def body(q_ref, k_ref, kt_ref, v_ref, o_ref, do_ref, lse_ref, dq_ref, dk_ref, dv_ref, dqt_ref, dka_ref, dva_ref):
        i = pl.program_id(2)

        @pl.when(i == 0)
        def _():
            dqt_ref[...] = jnp.zeros_like(dqt_ref)

        dka_ref[...] = jnp.zeros_like(dka_ref)
        dva_ref[...] = jnp.zeros_like(dva_ref)
        ones8 = jnp.ones((8, HEAD), BF16)

        def step(c, mask):
            c0 = pl.multiple_of(c * tq, tq)
            for hh in range(hp):
                qcols, vcols = pl.ds(hh * QK_PAD, QK_PAD), pl.ds(hh * HEAD, HEAD)
                qs = q_ref[pl.ds(c0, tq), qcols]
                dos = do_ref[pl.ds(c0, tq), vcols]
                prod = dos.astype(F32) * o_ref[pl.ds(c0, tq), vcols].astype(F32)
                hi = _bf(prod)
                lo = _bf(prod - hi.astype(F32))
                delta8 = (lax.dot_general(ones8, hi, NT_DIMS, preferred_element_type=F32)
                          + lax.dot_general(ones8, lo, NT_DIMS, preferred_element_type=F32))
                st = lax.dot_general(k_ref[:, qcols], qs, NT_DIMS, preferred_element_type=F32)
                pt = jnp.exp(st - lse_ref[0, hh, pl.ds(c, 1)][0])
                if mask is not None:
                    pt = jnp.where(_key_query_mask(i * tk, c * tq, tk, tq, mask == "causal"), pt, 0.0)
                dva_ref[hh] += jnp.dot(_bf(pt), dos, preferred_element_type=F32)
                dpt = lax.dot_general(v_ref[:, vcols], dos, NT_DIMS, preferred_element_type=F32)
                dst = _bf(pt * (dpt - jnp.tile(delta8, (tk // 8, 1))))
                dka_ref[hh] += jnp.dot(dst, qs, preferred_element_type=F32)
                dqt_ref[hh, pl.ds(c, 1)] += jnp.dot(kt_ref[0, hh, 0], dst, preferred_element_type=F32)[None]

        step(i, "causal")

        def rest_masked(c, carry):
            step(c, "pad")
            return carry

        def rest(c, carry):
            step(c, None)
            return carry

        @pl.when(i == 0)
        def _():
            lax.fori_loop(1, nq, rest_masked, 0)

        @pl.when(i > 0)
        def _():
            lax.fori_loop(i + 1, nq, rest, 0)

        for hh in range(hp):
            dk_ref[:, pl.ds(hh * QK_PAD, QK_PAD)] = dka_ref[hh].astype(dk_ref.dtype)
            dv_ref[:, pl.ds(hh * HEAD, HEAD)] = dva_ref[hh].astype(dv_ref.dtype)

        @pl.when(i == nq - 1)
        def _():
            for hh in range(hp):
                for c in range(nq):
                    dq_ref[pl.ds(c * tq, tq), pl.ds(hh * QK_PAD, QK_PAD)] = (
                        jnp.transpose(dqt_ref[hh, c])).astype(dq_ref.dtype)

    return pl.pallas_call(
        body, name="attn_bwd", grid=(bl, nm // hp, nq),
        in_specs=[pl.BlockSpec((lp, hp * QK_PAD), lambda b, h, i: (b, h)),
                  pl.BlockSpec((tk, hp * QK_PAD), lambda b, h, i: (b * nq + i, h)),
                  pl.BlockSpec((1, hp, 1, QK_PAD, tk), lambda b, h, i: (b, h, i, 0, 0)),
                  pl.BlockSpec((tk, hp * HEAD), lambda b, h, i: (b * nq + i, h)),
                  pl.BlockSpec((lp, hp * HEAD), lambda b, h, i: (b, h)),
                  pl.BlockSpec((lp, hp * HEAD), lambda b, h, i: (b, h)),
                  pl.BlockSpec((1, hp, nq, 1, tq), lambda b, h, i: (b, h, 0, 0, 0))],
        out_specs=[pl.BlockSpec((lp, hp * QK_PAD), lambda b, h, i: (b, h)),
                   pl.BlockSpec((tk, hp * QK_PAD), lambda b, h, i: (b * nq + i, h)),
                   pl.BlockSpec((tk, hp * HEAD), lambda b, h, i: (b * nq + i, h))],
        out_shape=[jax.ShapeDtypeStruct((bl * lp, nm * QK_PAD), BF16),
                   jax.ShapeDtypeStruct((bl * lp, nm * QK_PAD), BF16),
                   jax.ShapeDtypeStruct((bl * lp, nm * HEAD), BF16)],
        scratch_shapes=[pltpu.VMEM((hp, nq, QK_PAD, tq), F32), pltpu.VMEM((hp, tk, QK_PAD), F32),
                        pltpu.VMEM((hp, tk, HEAD), F32)],
        compiler_params=pltpu.CompilerParams(dimension_semantics=("arbitrary", "arbitrary", "arbitrary")),
    )(q_cat, k_cat, k_t, v, o, do, lse)


def _place():
    return lax.axis_index("x"), lax.axis_index("y"), lax.axis_index("c")


def gather_shards(packed):
    hbm = pl.BlockSpec(memory_space=pl.ANY)

    def body(src_ref, out_ref, send_sems, recv_sems, local_sem):
        x, y, c = _place()
        me = 2 * x + y
        chips = [(1 - x, y), (x, 1 - y), (1 - x, 1 - y)]
        local = pltpu.make_async_copy(src_ref, out_ref.at[me], local_sem)
        local.start()
        sends = []
        for k, (px, py) in enumerate(chips):
            cp = pltpu.make_async_remote_copy(src_ref=src_ref, dst_ref=out_ref.at[me], send_sem=send_sems.at[k],
                                              recv_sem=recv_sems.at[k], device_id=(px, py, c), device_id_type=MESH)
            cp.start()
            sends.append(cp)
        for k, (px, py) in enumerate(chips):
            pltpu.make_async_remote_copy(src_ref=src_ref, dst_ref=out_ref.at[2 * px + py], send_sem=send_sems.at[k],
                                         recv_sem=recv_sems.at[k], device_id=(px, py, c),
                                         device_id_type=MESH).wait_recv()
        for cp in sends:
            cp.wait_send()
        local.wait()

    return pl.pallas_call(
        body, name="gather_shards", in_specs=[hbm], out_specs=hbm,
        out_shape=jax.ShapeDtypeStruct((4,) + packed.shape, packed.dtype),
        scratch_shapes=[pltpu.SemaphoreType.DMA((3,)), pltpu.SemaphoreType.DMA((3,)), pltpu.SemaphoreType.DMA],
    )(packed)


def gather_small(small):
    hbm = pl.BlockSpec(memory_space=pl.ANY)

    def body(small_ref, all_ref, send_sems, recv_sems, local_sem):
        x, y, c = _place()
        me = 4 * x + 2 * y + c
        local = pltpu.make_async_copy(small_ref, all_ref.at[me], local_sem)
        local.start()
        others = [(x ^ ((r >> 2) & 1), y ^ ((r >> 1) & 1), c ^ (r & 1)) for r in range(1, 8)]
        sends = []
        for r, peer in enumerate(others):
            cp = pltpu.make_async_remote_copy(src_ref=small_ref, dst_ref=all_ref.at[me], send_sem=send_sems.at[r],
                                              recv_sem=recv_sems.at[r], device_id=peer, device_id_type=MESH)
            cp.start()
            sends.append(cp)
        for r, (px, py, pc) in enumerate(others):
            pltpu.make_async_remote_copy(src_ref=small_ref, dst_ref=all_ref.at[4 * px + 2 * py + pc],
                                         send_sem=send_sems.at[r], recv_sem=recv_sems.at[r],
                                         device_id=(px, py, pc), device_id_type=MESH).wait_recv()
        for cp in sends:
            cp.wait_send()
        local.wait()

    return pl.pallas_call(
        body, name="gather_small", in_specs=[hbm], out_specs=hbm,
        out_shape=jax.ShapeDtypeStruct((8,) + small.shape, small.dtype),
        scratch_shapes=[pltpu.SemaphoreType.DMA((7,)), pltpu.SemaphoreType.DMA((7,)), pltpu.SemaphoreType.DMA],
    )(small)


def swap_with_sibling(name, parts):
    n = len(parts)
    hbm = pl.BlockSpec(memory_space=pl.ANY)

    def body(*refs):
        x, y, c = _place()
        cps = [pltpu.make_async_remote_copy(src_ref=refs[j], dst_ref=refs[n + j], send_sem=refs[2 * n].at[j],
                                            recv_sem=refs[2 * n + 1].at[j], device_id=(x, y, 1 - c),
                                            device_id_type=MESH) for j in range(n)]
        for cp in cps:
            cp.start()
        for cp in cps:
            cp.wait()

    return pl.pallas_call(
        body, name=name, in_specs=[hbm] * n, out_specs=[hbm] * n,
        out_shape=[jax.ShapeDtypeStruct(p.shape, p.dtype) for p in parts],
        scratch_shapes=[pltpu.SemaphoreType.DMA((n,)), pltpu.SemaphoreType.DMA((n,))],
    )(*parts)


def _chips3():
    x, y, c = _place()
    return [(1 - x, y, c), (x, 1 - y, c), (1 - x, 1 - y, c)]


def _push_copies(src_refs, land_refs, send_sems, recv_sems, per_chip):
    x, y, _ = _place()
    cps = []
    for j, (src_ref, land_ref) in enumerate(zip(src_refs, land_refs)):
        for k, (px, py, pc) in enumerate(_chips3()):
            part = src_ref.at[2 * px + py] if per_chip else src_ref
            slot = k if per_chip else 2 * x + y
            cps.append(pltpu.make_async_remote_copy(
                src_ref=part, dst_ref=land_ref.at[slot], send_sem=send_sems.at[3 * j + k],
                recv_sem=recv_sems.at[3 * j + k], device_id=(px, py, pc), device_id_type=MESH))
    return cps


def push_start(name, srcs, per_chip):
    n = len(srcs)
    hbm = pl.BlockSpec(memory_space=pltpu.HBM)
    sem = pl.BlockSpec(memory_space=pltpu.SEMAPHORE)
    lands = [lax.empty((3 if per_chip else 4,) + s.shape[-2:], s.dtype) for s in srcs]

    def body(*refs):
        src_refs, land_refs = refs[:n], refs[n:2 * n]
        send_sems, recv_sems = refs[2 * n], refs[2 * n + 1]
        for cp in _push_copies(src_refs, land_refs, send_sems, recv_sems, per_chip):
            cp.start()
        refs[-1][...] = jnp.zeros_like(refs[-1])

    outs = pl.pallas_call(
        body, name=name,
        out_shape=(pltpu.SemaphoreType.DMA((3 * n,)), pltpu.SemaphoreType.DMA((3 * n,)),
                   *[pltpu.HBM(a.shape, a.dtype) for a in list(srcs) + lands], jax.ShapeDtypeStruct((8, HEAD), F32)),
        in_specs=(hbm,) * (2 * n),
        out_specs=(sem, sem) + (hbm,) * (2 * n) + (pl.BlockSpec(memory_space=pltpu.VMEM),),
        input_output_aliases={j: 2 + j for j in range(2 * n)},
        compiler_params=pltpu.CompilerParams(has_side_effects=pltpu.SideEffectType.DATAFLOW_SIDE_EFFECTING),
    )(*[pltpu.with_memory_space_constraint(a, pltpu.HBM) for a in list(srcs) + lands])
    return tuple(outs[:-1]), outs[-1]


def push_wait(name, handle, after, per_chip):
    send_sems, recv_sems = handle[0], handle[1]
    thru = handle[2:]
    n = len(thru) // 2
    hbm = pl.BlockSpec(memory_space=pltpu.HBM)
    sem = pl.BlockSpec(memory_space=pltpu.SEMAPHORE)

    def body(*refs):
        src_refs, land_refs = refs[:n], refs[n:2 * n]
        for cp in _push_copies(src_refs, land_refs, refs[2 * n], refs[2 * n + 1], per_chip):
            cp.wait_send()
            cp.wait_recv()

    outs = pl.pallas_call(
        body, name=name,
        out_shape=tuple(pltpu.HBM(a.shape, a.dtype) for a in thru),
        in_specs=(hbm,) * (2 * n) + (sem, sem, pl.BlockSpec(memory_space=pl.ANY)), out_specs=(hbm,) * (2 * n),
        input_output_aliases={j: j for j in range(2 * n)},
        compiler_params=pltpu.CompilerParams(has_side_effects=pltpu.SideEffectType.DATAFLOW_SIDE_EFFECTING),
    )(*thru, send_sems, recv_sems, after)
    return outs[:n], outs[n:]


def join_gathered(name, own, landed, my_chip):
    blocks = lax.dynamic_update_index_in_dim(landed, own, my_chip, 0)
    _, r, c = blocks.shape
    if name in COL_SHARDED:
        return blocks.transpose(1, 0, 2).reshape(r, 4 * c)
    return blocks.reshape(4 * r, c)


def adamw(name, w, g_parts, m, v):
    r, c = w.shape
    tr = r if r * c <= 65536 else _tile(r, 128, 8)
    ng = len(g_parts)

    def body(*refs):
        w_ref, m_ref, v_ref = refs[0], refs[1 + ng], refs[2 + ng]
        g_ref, d_ref, nm_ref, nv_ref = refs[3 + ng:]
        gv = refs[1][...]
        for k in range(1, ng):
            gv = gv + refs[1 + k][...]
        m_new = ADAM_B1 * m_ref[...] + (1.0 - ADAM_B1) * gv
        v_new = ADAM_B2 * v_ref[...] + (1.0 - ADAM_B2) * (gv * gv)
        m_hat = m_new / (1.0 - ADAM_B1 ** ADAM_STEP)
        v_hat = v_new / (1.0 - ADAM_B2 ** ADAM_STEP)
        g_ref[...] = gv
        d_ref[...] = -ADAM_LR * (m_hat / (jnp.sqrt(v_hat) + ADAM_EPS) + ADAM_WD * w_ref[...])
        nm_ref[...] = m_new
        nv_ref[...] = v_new

    spec = pl.BlockSpec((tr, c), lambda i: (i, 0))
    return pl.pallas_call(
        body, name=name, grid=(r // tr,), in_specs=[spec] * (3 + ng), out_specs=[spec] * 4,
        out_shape=[jax.ShapeDtypeStruct((r, c), F32)] * 4,
        compiler_params=pltpu.CompilerParams(dimension_semantics=("arbitrary",)),
    )(w, *g_parts, m, v)


def split_full(name, full, s):
    if name in COL_SHARDED:
        c = full.shape[1] // 4
        return full[:, s * c:(s + 1) * c]
    r = full.shape[0] // 4
    return full[s * r:(s + 1) * r]


def kernel(x, meta_tokens, w_in, b_gate, lb_logits, hg_norm_g, w_hg_o, q_a_norm_g, w_q_b, kv_a_norm_g, w_kv_b, w_mla_o, w_out, mix_pre_g, mix_post_g, ffn_pre_g, ffn_post_g, w_ffn_in, w_ffn_out, loss_target, m_meta_tokens, m_w_in, m_b_gate, m_lb_logits, m_hg_norm_g, m_w_hg_o, m_q_a_norm_g, m_w_q_b, m_kv_a_norm_g, m_w_kv_b, m_w_mla_o, m_w_out, m_mix_pre_g, m_mix_post_g, m_ffn_pre_g, m_ffn_post_g, m_w_ffn_in, m_w_ffn_out, v_meta_tokens, v_w_in, v_b_gate, v_lb_logits, v_hg_norm_g, v_w_hg_o, v_q_a_norm_g, v_w_q_b, v_kv_a_norm_g, v_w_kv_b, v_w_mla_o, v_w_out, v_mix_pre_g, v_mix_post_g, v_ffn_pre_g, v_ffn_post_g, v_w_ffn_in, v_w_ffn_out):
    wts = dict(meta_tokens=meta_tokens, w_in=w_in[0], b_gate=b_gate, lb_logits=lb_logits, hg_norm_g=hg_norm_g,
               w_hg_o=w_hg_o[0], q_a_norm_g=q_a_norm_g, w_q_b=w_q_b[0], kv_a_norm_g=kv_a_norm_g, w_kv_b=w_kv_b[0],
               w_mla_o=w_mla_o[0], w_out=w_out[0], mix_pre_g=mix_pre_g, mix_post_g=mix_post_g, ffn_pre_g=ffn_pre_g,
               ffn_post_g=ffn_post_g, w_ffn_in=w_ffn_in[0], w_ffn_out=w_ffn_out[0])
    mom_m = dict(meta_tokens=m_meta_tokens, w_in=m_w_in[0], b_gate=m_b_gate, lb_logits=m_lb_logits,
                 hg_norm_g=m_hg_norm_g, w_hg_o=m_w_hg_o[0], q_a_norm_g=m_q_a_norm_g, w_q_b=m_w_q_b[0],
                 kv_a_norm_g=m_kv_a_norm_g, w_kv_b=m_w_kv_b[0], w_mla_o=m_w_mla_o[0], w_out=m_w_out[0],
                 mix_pre_g=m_mix_pre_g, mix_post_g=m_mix_post_g, ffn_pre_g=m_ffn_pre_g, ffn_post_g=m_ffn_post_g,
                 w_ffn_in=m_w_ffn_in[0], w_ffn_out=m_w_ffn_out[0])
    mom_v = dict(meta_tokens=v_meta_tokens, w_in=v_w_in[0], b_gate=v_b_gate, lb_logits=v_lb_logits,
                 hg_norm_g=v_hg_norm_g, w_hg_o=v_w_hg_o[0], q_a_norm_g=v_q_a_norm_g, w_q_b=v_w_q_b[0],
                 kv_a_norm_g=v_kv_a_norm_g, w_kv_b=v_w_kv_b[0], w_mla_o=v_w_mla_o[0], w_out=v_w_out[0],
                 mix_pre_g=v_mix_pre_g, mix_post_g=v_mix_post_g, ffn_pre_g=v_ffn_pre_g, ffn_post_g=v_ffn_post_g,
                 w_ffn_in=v_w_ffn_in[0], w_ffn_out=v_w_ffn_out[0])

    bl, seq, d = x.shape
    lp = PAD_FRONT + N_META + seq
    t_rows = bl * lp
    nh = d // HEAD
    ql, kvl = wts["w_q_b"].shape[0], wts["w_kv_b"].shape[0]
    nm = (4 * wts["w_mla_o"].shape[0]) // HEAD
    ffn = 4 * wts["w_ffn_out"].shape[0]
    mla_w = ql + kvl + HEAD
    assert ql == kvl and ql % HEAD == 0 and seq % SEQ_BLOCK == 0 and d % HEAD == 0
    scale = (HEAD + ROPE) ** -0.5
    my_chip = 2 * lax.axis_index("x") + lax.axis_index("y")

    mcols = meta_tokens.shape[1]
    meta_all = gather_shards(meta_tokens)
    meta_full = jnp.concatenate([meta_all[s] for s in range(4)], axis=1)

    def start_gather(name, names, order_after):
        srcs = [_bf(wts[n]) for n in names]
        if order_after is not None:
            srcs[0] = srcs[0] + order_after[0, 0].astype(BF16)
        return push_start(name, srcs, per_chip=False)

    def finish_gather(name, names, started, after):
        owns, landed = push_wait(name, started[0], after, per_chip=False)
        return {n: join_gathered(n, own, land, my_chip) for n, own, land in zip(names, owns, landed)}

    rest_names = tuple(n for n in BIG if n != "w_in")
    my_c = lax.axis_index("c")
    w_in_bf = _bf(wts["w_in"])
    half = w_in_bf.shape[0] // 2
    own_half = (lax.dynamic_slice_in_dim(w_in_bf, my_c * half, half, axis=0)
                + (meta_all[0, :1, :1] * 0.0)[0, 0].astype(BF16))
    gather_1 = push_start("gather_w_in_start", [own_half], per_chip=False)
    gather_2 = start_gather("gather_rest_start", rest_names, gather_1[1])

    tiles_seq, tiles_real = lp // SEQ_BLOCK, seq // SEQ_BLOCK
    assert PAD_FRONT + N_META == SEQ_BLOCK

    def real_block(i):
        return (i // tiles_seq) * tiles_real + jnp.maximum(i % tiles_seq - 1, 0)

    meta_rows = jnp.broadcast_to(((jnp.arange(lp) >= PAD_FRONT) & (jnp.arange(lp) < PAD_FRONT + N_META)
                                  ).astype(F32)[:, None], (lp, HEAD))
    pos = (jnp.arange(lp, dtype=jnp.int32) - PAD_FRONT).astype(F32)
    inv_freq = 1.0 / (ROPE_THETA ** (jnp.arange(0, ROPE, 2, dtype=F32) / ROPE))
    ang = pos[:, None] * inv_freq[None, :]
    zeros32 = jnp.zeros((lp, ROPE_HALF), F32)
    zeros64 = jnp.zeros((lp, HEAD - ROPE), F32)
    t_cos = jnp.concatenate([jnp.cos(ang), jnp.cos(ang), zeros64], axis=1)
    t_up = jnp.concatenate([zeros32, jnp.sin(ang), zeros64], axis=1)
    t_dn = jnp.concatenate([-jnp.sin(ang), zeros32, zeros64], axis=1)
    real = jnp.broadcast_to((jnp.arange(lp) >= PAD_FRONT + N_META).astype(F32)[:, None], (lp, HEAD))
    lanes = d // HEAD
    lb_soft = jax.nn.softmax(lb_logits.astype(F32), axis=0)
    lb = lb_soft[0:1]

    meta_tile = jnp.concatenate([jnp.zeros((PAD_FRONT, d), F32), meta_full], axis=0)

    def first_fn(xv, is_real, is_meta, mtile, g):
        h = xv * jnp.tile(is_real, (1, lanes)) + mtile * jnp.tile(is_meta, (1, lanes))
        return _rms(h, g), h

    u1, h0 = rowwise("norm_mix_pre", first_fn, [(x.reshape(bl * seq, d), d, 0, real_block)], [real, meta_rows],
                     [meta_tile, mix_pre_g + gather_2[1][0, 0]], [(d, BF16), (d, F32)], n_rows=t_rows)
    _, (fetched,) = push_wait("gather_w_in_wait", gather_1[0], u1, per_chip=False)
    (handed,) = swap_with_sibling("swap_w_in", [fetched])
    halves = jnp.stack([fetched, handed])
    remote = jnp.concatenate([lax.dynamic_index_in_dim(halves, my_c, 0, keepdims=False),
                              lax.dynamic_index_in_dim(halves, 1 - my_c, 0, keepdims=False)], axis=1)
    full = {"w_in": join_gathered("w_in", w_in_bf, remote, my_chip)}
    w_main = jnp.concatenate([full["w_in"][:, :4 * d], full["w_in"][:, -2 * d:]], axis=1)
    w_mla = jnp.pad(full["w_in"][:, 4 * d:4 * d + ql + kvl + ROPE], ((0, 0), (0, HEAD - ROPE)))
    proj_main = matmul("proj_main", u1, w_main, "nn", out_dtype=BF16)
    proj_mla = matmul("proj_mla", u1, w_mla, "nn", out_dtype=BF16)
    hg_consts = _hg_constants()
    o_scan, states, a_mats = hgrn_fwd(proj_main, lb, hg_consts, bl, lp, d)

    full.update(finish_gather("gather_rest_wait", rest_names, gather_2, o_scan))
    w_qb = jnp.pad(full["w_q_b"].reshape(ql, nm, HEAD + ROPE), ((0, 0), (0, 0), (0, QK_PAD - HEAD - ROPE))
                   ).reshape(ql, nm * QK_PAD)
    w_kvb = full["w_kv_b"]

    def hg_out_fn(o, hg, g):
        ov = jnp.concatenate([_rms(o[:, h * HEAD:(h + 1) * HEAD], g) for h in range(nh)], axis=1) * _silu(hg)
        return ov, ov

    y_a, o_hg = matmul_fused("hgrn_out_y_a", hg_out_fn, [(o_scan, d, 0), (proj_main, d, 3)], [hg_norm_g],
                             _bf(full["w_hg_o"]), [(0, d)], "nn", [(d, BF16)], tm=512)

    def seq_tile(i):
        return i % tiles_seq

    tables = [(t_cos, HEAD, 0, seq_tile), (t_up, HEAD, 0, seq_tile), (t_dn, HEAD, 0, seq_tile)]

    def q_norm_fn(cq, cos, s_up, s_dn, g):
        cn = _rms(cq, g)
        return cn, cn

    def q_rope_fn(products, vals):
        qf = products[0] * scale
        cos, s_up, s_dn = vals[1:4]
        qs = []
        for h in range(nm):
            qs += [qf[:, h * QK_PAD:h * QK_PAD + HEAD], _rope(qf[:, h * QK_PAD + HEAD:(h + 1) * QK_PAD], cos, s_up, s_dn)]
        return [jnp.concatenate(qs, axis=1)], []

    q_cat, qn = matmul_fused("q_norm_up_rope", q_norm_fn, [(proj_mla, ql, 0)] + tables, [q_a_norm_g], w_qb,
                             [(0, ql)], "nn", [(ql, BF16)], epilogue=q_rope_fn, epi_outs=[(nm * QK_PAD, BF16)])

    def kv_norm_fn(ckv, kpe, cos, s_up, s_dn, g):
        cn = _rms(ckv, g)
        return cn, cn

    def kv_rope_fn(products, vals):
        kvf = products[0]
        kpe_r = _rope(vals[1], *vals[2:5])
        ks, vs = [], []
        for h in range(nm):
            ks += [kvf[:, h * QK_PAD:h * QK_PAD + HEAD], kpe_r]
            vs += [kvf[:, h * QK_PAD + HEAD:(h + 1) * QK_PAD]]
        return [jnp.concatenate(ks, axis=1), jnp.concatenate(vs, axis=1)], []

    kpe_blk = (ql + kvl) // HEAD
    k_cat, v_att, kvn = matmul_fused("kv_norm_up_rope", kv_norm_fn,
                                     [(proj_mla, kvl, 1), (proj_mla, HEAD, kpe_blk)] + tables, [kv_a_norm_g], w_kvb,
                                     [(0, kvl)], "nn", [(kvl, BF16)], epilogue=kv_rope_fn,
                                     epi_outs=[(nm * QK_PAD, BF16), (nm * HEAD, BF16)])
    at = _attn_tile(lp)
    v_t = v_att.reshape(bl, lp // at, at, nm, HEAD).transpose(0, 3, 1, 4, 2)
    k_t = k_cat.reshape(bl, lp // at, at, nm, QK_PAD).transpose(0, 3, 1, 4, 2)
    o_mla, lse = attn_fwd_t(q_cat, k_cat, v_t, bl, lp, nm)
    y_b = matmul("y_b", o_mla, _bf(full["w_mla_o"]), "nn", out_dtype=BF16)

    def gate_fn(ya, yb, ga, gb, bias):
        zv = _sigmoid(ga + bias[:, :d]) * ya + _sigmoid(gb + bias[:, d:]) * yb
        return zv, zv

    mixed, z = matmul_fused("gate_mix_out", gate_fn,
                            [(y_a, d, 0), (y_b, d, 0), (proj_main, d, 4), (proj_main, d, 5)], [b_gate],
                            _bf(full["w_out"]), [(0, d)], "nn", [(d, BF16)], tm=512)

    def mid_fn(h, mx, g_post, g_pre):
        h1v = h + _rms(mx, g_post)
        u2v = _rms(h1v, g_pre)
        return u2v, h1v, u2v

    gu, h1, u2 = matmul_fused("norm_mid_ffn_in", mid_fn, [(h0, d, 0), (mixed, d, 0)], [mix_post_g, ffn_pre_g],
                              _bf(full["w_ffn_in"]), [(0, d)], "nn", [(d, F32), (d, BF16)])
    def swiglu_fn(gt, up):
        a = _silu(gt) * up
        return a, a

    f_out, act = matmul_fused("swiglu_ffn_out", swiglu_fn, [(gu, ffn, 0), (gu, ffn, 1)], [],
                              _bf(full["w_ffn_out"]), [(0, ffn)], "nn", [(ffn, BF16)])

    def loss_fn(h1v, fv, tg, realv, g_post):
        h2 = h1v + _rms(fv, g_post)
        diff = (h2 - tg) * jnp.tile(realv, (1, lanes))
        part = jnp.broadcast_to(0.5 * jnp.sum(diff * diff, keepdims=True) / d, (1, HEAD))
        dy = diff / d
        df, dg = _rms_bwd(fv, g_post, dy)
        return df, dy, df, part, dg

    d_act, dy, df, loss_part, g_ffn_post = matmul_fused(
        "loss_head_d_act", loss_fn,
        [(h1, d, 0), (f_out, d, 0), (loss_target.reshape(bl * seq, d), d, 0, real_block), (real, HEAD, 0, seq_tile)],
        [ffn_post_g], _bf(full["w_ffn_out"]), [(0, d)], "nt", [(d, BF16), (d, BF16)],
        acc_outs=[(1, HEAD), (1, d)])
    grads = {}
    grads["w_ffn_out"] = matmul("gw_ffn_out", act, df, "tn")

    def swiglu_bwd_fn(gt, up, da):
        dgt, dup = da * up * _silu_grad(gt), da * _silu(gt)
        return dgt, dup, jnp.concatenate([dgt, dup], axis=1)

    du2, dgu = matmul_fused("swiglu_bwd_d_u2", swiglu_bwd_fn, [(gu, ffn, 0), (gu, ffn, 1), (d_act, ffn, 0)], [],
                            _bf(full["w_ffn_in"]), [(0, ffn), (ffn, 2 * ffn)], "nt", [(2 * ffn, BF16)])
    grads["w_ffn_in"] = matmul("gw_ffn_in", u2, dgu, "tn")

    def mid_bwd_fn(dyv, h1v, du2v, mx, g_pre, g_post):
        dx, dg_pre = _rms_bwd(h1v, g_pre, du2v)
        dh1 = dyv + dx
        dmx, dg_post = _rms_bwd(mx, g_post, dh1)
        return dmx, dh1, dmx, dg_pre, dg_post

    dz, dh1, dmixed, g_ffn_pre, g_mix_post = matmul_fused(
        "norm_mid_bwd_d_z", mid_bwd_fn, [(dy, d, 0), (h1, d, 0), (du2, d, 0), (mixed, d, 0)],
        [ffn_pre_g, mix_post_g], _bf(full["w_out"]), [(0, d)], "nt", [(d, BF16), (d, BF16)], tm=512,
        acc_outs=[(1, d), (1, d)])
    grads["w_out"] = matmul("gw_out", z, dmixed, "tn")

    def gate_bwd_fn(dzv, ya, yb, ga, gb, bias):
        sa, sb = _sigmoid(ga + bias[:, :d]), _sigmoid(gb + bias[:, d:])
        dga = dzv * ya * sa * (1.0 - sa)
        dgb = dzv * yb * sb * (1.0 - sb)
        dgates = jnp.concatenate([dga, dgb], axis=1)
        dya, dyb = dzv * sa, dzv * sb
        return dya, dyb, dya, dyb, dgates, jnp.sum(dgates, axis=0, keepdims=True)

    def hg_out_bwd_fn(products, vals):
        do, o, hg, g = products[0], vals[5], vals[6], vals[8]
        dn = do * _silu(hg)
        dos, dgs, ons = [], 0.0, []
        for h in range(nh):
            sl = slice(h * HEAD, (h + 1) * HEAD)
            dx, dg = _rms_bwd(o[:, sl], g, dn[:, sl])
            dos.append(dx)
            dgs = dgs + dg
            ons.append(_rms(o[:, sl], g))
        dhg_v = do * jnp.concatenate(ons, axis=1) * _silu_grad(hg)
        return [jnp.concatenate(dos, axis=1), products[1], dhg_v], [dgs]

    do_scan, do_mla, dhg, dy_a, dy_b, dgates, g_b_gate, g_hg_norm = matmul_fused(
        "gate_mix_bwd_d_o", lambda dzv, ya, yb, ga, gb, o, hg, bias, g: gate_bwd_fn(dzv, ya, yb, ga, gb, bias),
        [(dz, d, 0), (y_a, d, 0), (y_b, d, 0), (proj_main, d, 4), (proj_main, d, 5), (o_scan, d, 0),
         (proj_main, d, 3)], [b_gate, hg_norm_g],
        [_bf(full["w_hg_o"]), _bf(full["w_mla_o"])], [(0, 0, d), (1, 0, d)], "nt",
        [(d, BF16), (d, BF16), (2 * d, BF16)], acc_outs=[(1, 2 * d), (1, HEAD)],
        epilogue=hg_out_bwd_fn, epi_outs=[(d, BF16), (d, BF16), (d, BF16)])
    grads["w_hg_o"] = matmul("gw_hg_o", o_hg, dy_a, "tn")
    grads["w_mla_o"] = matmul("gw_mla_o", o_mla, dy_b, "tn")

    early = ("w_hg_o", "w_mla_o", "w_out", "w_ffn_in", "w_ffn_out")
    late = ("w_in", "w_q_b", "w_kv_b")

    def start_grads(name, names):
        sends = [_bf(jnp.stack([split_full(n, grads[n], s) for s in range(4)])) for n in names]
        mines = []
        for n in names:
            r, c = wts[n].shape
            axis, size = (1, c) if n in COL_SHARDED else (0, r)
            mines.append(lax.dynamic_slice_in_dim(grads[n], my_chip * size, size, axis=axis))
        handle, token = push_start(name, sends, per_chip=True)
        return handle, token, mines

    def finish_grads(tag, names, started, after):
        handle, _, mines = started
        _, landed = push_wait(f"grads_{tag}_wait", handle, after, per_chip=True)
        parts = []
        for n, mine, land in zip(names, mines, landed):
            r, c = mine.shape
            tr = _tile(r, 256, 16)
            land2 = land.reshape(3 * r, c)
            parts.append(rowwise(f"sum_chips_{n}", lambda a, r0, r1, r2: a + r0 + r1 + r2,
                                 [(mine, c, 0)] + [(land2, c, 0, k * (r // tr)) for k in range(3)],
                                 [], [], [(c, F32)], tm=tr)[0])
        sibs = swap_with_sibling(f"swap_{tag}", parts)
        return {n: [p, s] for n, p, s in zip(names, parts, sibs)}

    grads_early = start_grads("grads_early_start", early)
    token_a = grads_early[1]

    dhq, dhf, dhi, g_lb = hgrn_bwd(proj_main, lb + token_a[0, 0], hg_consts, states, a_mats, do_scan, bl, lp, d)

    dq_cat, dk_cat, dv_att = attn_bwd_t(q_cat, k_cat, k_t, v_att, o_mla, do_mla, lse, bl, lp, nm)

    def mla_prep_bwd_fn(dqc, dkc, dvv, cos, s_up, s_dn, cq, ckv, gq, gk):
        dqc = dqc * scale
        dqs, dkvs = [], []
        for h in range(nm):
            dqs += [dqc[:, h * QK_PAD:h * QK_PAD + HEAD],
                    _rope_bwd(dqc[:, h * QK_PAD + HEAD:(h + 1) * QK_PAD], cos, s_up, s_dn)]
            dkvs += [dkc[:, h * QK_PAD:h * QK_PAD + HEAD], dvv[:, h * HEAD:(h + 1) * HEAD]]
        dqf, dkvf = jnp.concatenate(dqs, axis=1), jnp.concatenate(dkvs, axis=1)
        return dqf, dkvf, dqf, dkvf

    def mla_norms_bwd_fn(products, vals):
        dkc, cos, s_up, s_dn, cq, ckv, gq, gk = vals[1], vals[3], vals[4], vals[5], vals[6], vals[7], vals[8], vals[9]
        dkpe = 0.0
        for h in range(nm):
            dkpe = dkpe + dkc[:, h * QK_PAD + HEAD:(h + 1) * QK_PAD]
        dcq, dgq = _rms_bwd(cq, gq, products[0])
        dckv, dgk = _rms_bwd(ckv, gk, products[1])
        return [jnp.concatenate([dcq, dckv, _rope_bwd(dkpe, cos, s_up, s_dn)], axis=1)], [dgq, dgk]

    dmla, dq_full, dkv_full, g_q_norm, g_kv_norm = matmul_fused(
        "mla_prep_bwd_d_norms", mla_prep_bwd_fn,
        [(dq_cat, nm * QK_PAD, 0), (dk_cat, nm * QK_PAD, 0), (dv_att, nm * HEAD, 0)] + tables
        + [(proj_mla, ql, 0), (proj_mla, kvl, 1)], [q_a_norm_g, kv_a_norm_g],
        [w_qb, w_kvb], [(0, 0, nm * QK_PAD), (1, 0, nm * QK_PAD)], "nt",
        [(nm * QK_PAD, BF16), (nm * QK_PAD, BF16)], acc_outs=[(1, ql), (1, kvl)],
        epilogue=mla_norms_bwd_fn, epi_outs=[(mla_w, BF16)])
    g_wqb = matmul("gw_q_b", qn, dq_full, "tn")
    grads["w_q_b"] = g_wqb.reshape(ql, nm, QK_PAD)[:, :, :HEAD + ROPE].reshape(ql, nm * (HEAD + ROPE))
    grads["w_kv_b"] = matmul("gw_kv_b", kvn, dkv_full, "tn")

    d_pieces = [dhq, dhf, dhi, dhg, dgates, dmla]
    gw_parts = [matmul(f"gw_in_{k}", u1, dp, "tn") for k, dp in enumerate(d_pieces)]
    grads["w_in"] = jnp.concatenate(gw_parts[:4] + [gw_parts[5][:, :ql + kvl + ROPE], gw_parts[4]], axis=1)
    grads_late = start_grads("grads_late_start", late)
    w_mla_after = w_mla + grads_late[1][0, 0].astype(BF16)
    w_pieces = [w_main[:, 0:d], w_main[:, d:2 * d], w_main[:, 2 * d:3 * d], w_main[:, 3 * d:4 * d],
                w_main[:, 4 * d:6 * d], w_mla_after]
    du1 = matmul("d_u1", d_pieces, w_pieces, "nt", out_dtype=BF16)

    def first_bwd_fn(dh1v, h, du1v, is_meta, g):
        dx, dg = _rms_bwd(h, g, du1v)
        dh0v = dh1v + dx
        return dh0v, dg, dh0v * jnp.tile(is_meta, (1, d // HEAD))

    grad_x, g_mix_pre, meta_tile = rowwise(
        "norm_mix_pre_bwd", first_bwd_fn, [(dh1, d, 0), (h0, d, 0), (du1, d, 0)], [meta_rows], [mix_pre_g],
        [(d, F32, bl * seq, real_block)], [(1, d), (SEQ_BLOCK, d)])
    grad_x = grad_x.reshape(bl, seq, d)

    g_parts = finish_grads("early", early, grads_early, g_mix_pre)
    updates = {}

    def update(n, parts):
        w2 = wts[n].reshape(-1, wts[n].shape[-1])
        updates[n] = adamw("adamw_" + n, w2, [p.reshape(w2.shape) for p in parts], mom_m[n].reshape(w2.shape),
                           mom_v[n].reshape(w2.shape))

    for n in early:
        update(n, g_parts[n])
    g_parts = finish_grads("late", late, grads_late, updates[early[-1]][0])
    for n in late:
        update(n, g_parts[n])
    p0 = lb_soft[0:1]
    g_lb_logits = jnp.concatenate([g_lb * p0 * (1.0 - p0), -g_lb * p0 * (1.0 - p0)], axis=0)

    def row_of(vec):
        return vec.reshape(-1, d) if vec.size >= d else jnp.pad(vec.reshape(1, -1), ((0, 0), (0, d - vec.size)))

    small_parts = dict(b_gate=g_b_gate, lb_logits=g_lb_logits, hg_norm_g=g_hg_norm, q_a_norm_g=g_q_norm,
                       kv_a_norm_g=g_kv_norm, mix_pre_g=g_mix_pre, mix_post_g=g_mix_post, ffn_pre_g=g_ffn_pre,
                       ffn_post_g=g_ffn_post)
    g_meta = meta_tile[PAD_FRONT:PAD_FRONT + N_META]
    small_rows = [row_of(small_parts[n]) for n in SMALL] + [row_of(g_meta)]
    n_small = sum(r.shape[0] for r in small_rows)
    small = jnp.pad(jnp.concatenate(small_rows, axis=0), ((0, -(-n_small // 8) * 8 - n_small), (0, 0)))
    all_small = gather_small(small)
    small_t = small.shape[0]

    def sum8_fn(*slabs):
        acc = slabs[0]
        for s in slabs[1:]:
            acc = acc + s
        return acc

    (g_small,) = rowwise("sum_small", sum8_fn, [(all_small.reshape(8 * small_t, d), d, 0, k) for k in range(8)],
                         [], [], [(d, F32)], tm=small_t, n_rows=small_t)

    off = 0
    for n, part in zip(SMALL, small_rows[:-1]):
        rows = part.shape[0]
        update(n, [g_small[off:off + rows, :d].reshape(-1)[:wts[n].size]])
        off += rows
    update("meta_tokens", [lax.dynamic_slice_in_dim(g_small[off:off + N_META, :d], my_chip * mcols, mcols, axis=1)])

    loss = lax.psum(loss_part[0, 0], ("x", "y", "c"))

    def shaped(n, a):
        return a.reshape((1,) + wts[n].shape) if n in BIG else a.reshape(wts[n].shape)

    return (loss, grad_x, *[shaped(n, updates[n][k]) for k in range(4) for n in WEIGHTS])
```

```python
import functools
import math

import jax
import jax.numpy as jnp
from jax import lax
from jax.experimental import pallas as pl
from jax.experimental.pallas import tpu as pltpu

F32 = jnp.float32
BF16 = jnp.bfloat16
MESH = pl.DeviceIdType.MESH

N_META = 16
NORM_EPS = 1e-6
HEAD = 128
ROPE = 64
ROPE_HALF = ROPE // 2
QK_PAD = 2 * HEAD
ROPE_THETA = 10000.0
SEQ_BLOCK = 256
PAD_FRONT = SEQ_BLOCK - N_META
NEG = -1e30
VMEM_LIMIT = 56 * 1024 * 1024
WGRAD_TILE_MAX = 1536
WGRAD_TILE_MIN = 1024
WGRAD_ROW_STEP = 1536
HG_TILE_MAX = 768
ATTN_HEADS_PER_STEP = 1
ATTN_TILE_MAX = 768

ADAM_LR, ADAM_B1, ADAM_B2, ADAM_EPS, ADAM_WD, ADAM_STEP = 0.001, 0.9, 0.999, 1e-08, 0.01, 10

BIG = ("w_in", "w_hg_o", "w_q_b", "w_kv_b", "w_mla_o", "w_out", "w_ffn_in", "w_ffn_out")
COL_SHARDED = ("w_in", "w_q_b", "w_kv_b", "w_ffn_in")
SMALL = ("b_gate", "lb_logits", "hg_norm_g", "q_a_norm_g", "kv_a_norm_g", "mix_pre_g", "mix_post_g",
         "ffn_pre_g", "ffn_post_g")
WEIGHTS = ("meta_tokens", "w_in", "b_gate", "lb_logits", "hg_norm_g", "w_hg_o", "q_a_norm_g", "w_q_b",
           "kv_a_norm_g", "w_kv_b", "w_mla_o", "w_out", "mix_pre_g", "mix_post_g", "ffn_pre_g", "ffn_post_g",
           "w_ffn_in", "w_ffn_out")


def _tile(n, cap, unit=128):
    if n <= cap:
        return n
    best = None
    for t in range(unit, cap + 1, unit):
        if n % t == 0:
            best = t
    assert best is not None, (n, cap, unit)
    return best


def _sigmoid(x):
    return 1.0 / (1.0 + jnp.exp(-x))


def _bf(x):
    return x.astype(BF16)


def rowwise(name, fn, row_ins, seq_tabs, consts, row_outs, acc_outs=(), tm=SEQ_BLOCK, n_rows=None):
    t_rows = row_ins[0][0].shape[0] if n_rows is None else n_rows
    nt = t_rows // tm
    assert t_rows % tm == 0
    n_in = len(row_ins) + len(seq_tabs) + len(consts)
    n_row = len(row_outs)

    def body(*refs):
        vals = [r[...].astype(F32) for r in refs[:n_in]]
        res = fn(*vals)
        if not isinstance(res, (tuple, list)):
            res = (res,)
        outs = refs[n_in:]
        for k in range(n_row):
            outs[k][...] = res[k].astype(outs[k].dtype)
        if acc_outs:
            @pl.when(pl.program_id(0) == 0)
            def _():
                for k in range(len(acc_outs)):
                    outs[n_row + k][...] = jnp.zeros_like(outs[n_row + k])

            for k in range(len(acc_outs)):
                outs[n_row + k][...] += res[n_row + k]

    row_ins = [tuple(e) + (0,) * (4 - len(e)) for e in row_ins]
    in_specs = [pl.BlockSpec((tm, w), functools.partial(lambda i, j, ro: (ro(i) if callable(ro) else i + ro, j),
                                                        j=j, ro=ro)) for (_, w, j, ro) in row_ins]
    for tab in seq_tabs:
        per = tab.shape[0] // tm
        in_specs.append(pl.BlockSpec((tm, tab.shape[1]), functools.partial(lambda i, per: (i % per, 0), per=per)))
    for c in consts:
        in_specs.append(pl.BlockSpec(c.shape, lambda i: (0, 0)))
    row_outs = [tuple(e) + (t_rows, None)[len(e) - 2:] for e in row_outs]
    out_specs = [pl.BlockSpec((tm, w), functools.partial(lambda i, rm: (i if rm is None else rm(i), 0), rm=rm))
                 for (w, _, _, rm) in row_outs]
    out_specs += [pl.BlockSpec(s, lambda i: (0, 0)) for s in acc_outs]
    out_shape = [jax.ShapeDtypeStruct((rows, w), dt) for (w, dt, rows, _) in row_outs]
    out_shape += [jax.ShapeDtypeStruct(s, F32) for s in acc_outs]
    res = pl.pallas_call(
        body, name=name, grid=(nt,), in_specs=in_specs, out_specs=out_specs, out_shape=out_shape,
        compiler_params=pltpu.CompilerParams(dimension_semantics=("arbitrary",)),
    )(*[e[0] for e in row_ins], *seq_tabs, *consts)
    return res


def matmul(name, a, b, mode, out_dtype=F32):
    if mode != "tn":
        return _matmul_resident(name, a if isinstance(a, (list, tuple)) else [a],
                                b if isinstance(b, (list, tuple)) else [b], mode, out_dtype)
    kdim, m = a.shape
    n = b.shape[1]
    tn = _tile(n, WGRAD_TILE_MAX)
    tm = _tile(m, WGRAD_TILE_MAX if tn <= WGRAD_TILE_MIN else WGRAD_TILE_MIN)
    tk = _tile(kdim, WGRAD_ROW_STEP)
    nk = kdim // tk

    def body(a_ref, b_ref, o_ref, acc_ref):
        k = pl.program_id(2)

        @pl.when(k == 0)
        def _():
            acc_ref[...] = jnp.zeros_like(acc_ref)

        acc_ref[...] += lax.dot_general(a_ref[...], b_ref[...], TN_DIMS, preferred_element_type=F32)

        @pl.when(k == nk - 1)
        def _():
            o_ref[...] = acc_ref[...].astype(o_ref.dtype)

    return pl.pallas_call(
        body, name=name, grid=(m // tm, n // tn, nk),
        in_specs=[pl.BlockSpec((tk, tm), lambda i, j, k: (k, i)), pl.BlockSpec((tk, tn), lambda i, j, k: (k, j))],
        out_specs=pl.BlockSpec((tm, tn), lambda i, j, k: (i, j)),
        out_shape=jax.ShapeDtypeStruct((m, n), out_dtype),
        scratch_shapes=[pltpu.VMEM((tm, tn), F32)],
        compiler_params=pltpu.CompilerParams(dimension_semantics=("arbitrary", "arbitrary", "arbitrary"),
                                             vmem_limit_bytes=VMEM_LIMIT),
    )(a, b)


def _matmul_resident(name, a_list, b_list, mode, out_dtype):
    m = a_list[0].shape[0]
    n = b_list[0].shape[1] if mode == "nn" else b_list[0].shape[0]
    k_total = sum(a.shape[1] for a in a_list)
    out_bytes = 2 if out_dtype == BF16 else 4
    budget = VMEM_LIMIT - 4 * k_total * n - (6 << 20)
    tm = 1024
    while tm > 128 and 2 * tm * (2 * k_total + out_bytes * n) > budget:
        tm //= 2
    tm = _tile(m, tm)
    cn = _tile(n, WGRAD_TILE_MIN)
    npairs = len(a_list)

    def body(*refs):
        a_refs, b_refs, o_ref = refs[:npairs], refs[npairs:2 * npairs], refs[2 * npairs]
        for c in range(n // cn):
            acc = None
            for a_ref, b_ref in zip(a_refs, b_refs):
                if mode == "nn":
                    part = jnp.dot(a_ref[...], b_ref[:, pl.ds(c * cn, cn)], preferred_element_type=F32)
                else:
                    part = lax.dot_general(a_ref[...], b_ref[pl.ds(c * cn, cn), :], NT_DIMS,
                                           preferred_element_type=F32)
                acc = part if acc is None else acc + part
            o_ref[:, pl.ds(c * cn, cn)] = acc.astype(o_ref.dtype)

    in_specs = [pl.BlockSpec((tm, a.shape[1]), lambda i: (i, 0)) for a in a_list]
    in_specs += [pl.BlockSpec(b.shape, lambda i: (0, 0)) for b in b_list]
    return pl.pallas_call(
        body, name=name, grid=(m // tm,), in_specs=in_specs,
        out_specs=pl.BlockSpec((tm, n), lambda i: (i, 0)),
        out_shape=jax.ShapeDtypeStruct((m, n), out_dtype),
        compiler_params=pltpu.CompilerParams(dimension_semantics=("arbitrary",), vmem_limit_bytes=VMEM_LIMIT),
    )(*a_list, *b_list)


def matmul_fused(name, fn, row_ins, consts, weight, pieces, mode, extra_outs, out_dtype=BF16, tm=256, acc_outs=(),
                 epilogue=None, epi_outs=()):
    row_ins = [tuple(e) + (0,) * (4 - len(e)) for e in row_ins]
    t_rows = row_ins[0][0].shape[0]
    tm = _tile(t_rows, tm)
    several = isinstance(weight, (list, tuple))
    weights = list(weight) if several else [weight]
    n_in = len(row_ins) + len(consts)
    n_w = len(weights)
    n_parts = len(pieces)
    n_mm = len(epi_outs) if epilogue is not None else (n_parts if several else 1)
    n_row_out = n_mm + len(extra_outs)

    def width(w_arr):
        return w_arr.shape[1] if mode == "nn" else w_arr.shape[0]

    def body(*refs):
        w_hbms = refs[n_in:n_in + n_w]
        outs = refs[n_in + n_w:n_in + n_w + n_row_out + len(acc_outs)]
        w_refs, sems = refs[-n_w - 1:-1], refs[-1]

        @pl.when(pl.program_id(0) == 0)
        def _():
            cps = [pltpu.make_async_copy(w_hbms[k], w_refs[k], sems.at[k]) for k in range(n_w)]
            for cp in cps:
                cp.start()
            for cp in cps:
                cp.wait()
            for k in range(len(acc_outs)):
                outs[n_row_out + k][...] = jnp.zeros_like(outs[n_row_out + k])

        vals = [r[...].astype(F32) for r in refs[:n_in]]
        res = fn(*vals)
        products = []
        for a_p, piece in zip(res[:n_parts], pieces):
            w_ref, (k0, k1) = (w_refs[piece[0]], piece[1:]) if several else (w_refs[0], piece)
            if mode == "nn":
                products.append(jnp.dot(_bf(a_p), w_ref[pl.ds(k0, k1 - k0), :], preferred_element_type=F32))
            else:
                products.append(lax.dot_general(_bf(a_p), w_ref[:, pl.ds(k0, k1 - k0)], NT_DIMS,
                                                preferred_element_type=F32))
        if not several:
            products = [functools.reduce(lambda u, w: u + w, products)]
        sums = list(res[n_parts + len(extra_outs):])
        if epilogue is not None:
            products, more_sums = epilogue(products, vals)
            sums += list(more_sums)
        for p, val in enumerate(products):
            outs[p][...] = val.astype(outs[p].dtype)
        for o_ref, val in zip(outs[n_mm:n_row_out], res[n_parts:]):
            o_ref[...] = val.astype(o_ref.dtype)
        for k in range(len(acc_outs)):
            outs[n_row_out + k][...] += sums[k]

    in_specs = [pl.BlockSpec((tm, w), functools.partial(lambda i, j, ro: (ro(i) if callable(ro) else i + ro, j),
                                                        j=j, ro=ro)) for (_, w, j, ro) in row_ins]
    in_specs += [pl.BlockSpec(c.shape, lambda i: (0, 0)) for c in consts]
    in_specs += [pl.BlockSpec(memory_space=pl.ANY)] * n_w
    if epilogue is not None:
        widths = list(epi_outs) + list(extra_outs)
    elif several:
        widths = [(width(weights[piece[0]]), out_dtype) for piece in pieces] + list(extra_outs)
    else:
        widths = [(width(weights[0]), out_dtype)] + list(extra_outs)
    widths = [tuple(e) + (t_rows, None)[len(e) - 2:] for e in widths]
    return pl.pallas_call(
        body, name=name, grid=(t_rows // tm,), in_specs=in_specs,
        out_specs=[pl.BlockSpec((tm, w), functools.partial(lambda i, rm: (i if rm is None else rm(i), 0), rm=rm))
                   for (w, _, _, rm) in widths]
        + [pl.BlockSpec(s, lambda i: (0, 0)) for s in acc_outs],
        out_shape=[jax.ShapeDtypeStruct((rows, w), dt) for (w, dt, rows, _) in widths]
        + [jax.ShapeDtypeStruct(s, F32) for s in acc_outs],
        scratch_shapes=[pltpu.VMEM(w_arr.shape, w_arr.dtype) for w_arr in weights] + [pltpu.SemaphoreType.DMA((n_w,))],
        compiler_params=pltpu.CompilerParams(dimension_semantics=("arbitrary",), vmem_limit_bytes=VMEM_LIMIT),
    )(*[e[0] for e in row_ins], *consts, *weights)


def _rms(x, g):
    r = lax.rsqrt(jnp.mean(x * x, axis=-1, keepdims=True) + NORM_EPS)
    return x * r * g


def _rms_bwd(x, g, dy):
    r = lax.rsqrt(jnp.mean(x * x, axis=-1, keepdims=True) + NORM_EPS)
    xh = x * r
    dyg = dy * g
    dx = r * (dyg - xh * jnp.mean(dyg * xh, axis=-1, keepdims=True))
    return dx, jnp.sum(dy * xh, axis=0, keepdims=True)


def _silu(x):
    return x * _sigmoid(x)


def _silu_grad(x):
    s = _sigmoid(x)
    return s * (1.0 + x * (1.0 - s))


def _rope(xs, cos, s_up, s_dn):
    return xs * cos + pltpu.roll(xs, ROPE_HALF, 1) * s_up + pltpu.roll(xs, HEAD - ROPE_HALF, 1) * s_dn


def _rope_bwd(dy, cos, s_up, s_dn):
    return dy * cos + pltpu.roll(dy * s_up, HEAD - ROPE_HALF, 1) + pltpu.roll(dy * s_dn, ROPE_HALF, 1)


HG_SUB = 128
HG_LEVELS = 7
HG_E_ROWS = (HG_LEVELS + 1) * HG_SUB
HG_BWD_GROUP = 6
TN_DIMS = (((0,), (0,)), ((), ()))
NT_DIMS = (((1,), (1,)), ((), ()))


def _hg_constants():
    import numpy as np
    n = HG_SUB
    r = np.arange(n)[:, None]
    c = np.arange(n)[None, :]
    cs, ps = [], []
    for lvl in range(HG_LEVELS):
        m = (n // 2) >> lvl
        upper = (r % (2 * m)) >= m
        mid = (r // (2 * m)) * (2 * m) + m - 1
        cs.append(np.where(upper, (c > mid) & (c <= r), (c > r) & (c <= mid)))
        ps.append(((r // (2 * m)) == (c // (2 * m))) & upper & ((c % (2 * m)) < m))
    cs.append(c <= r)
    cs.append(np.ones((8, n), bool))
    cstack = np.concatenate(cs, 0).astype(np.float32)
    pstack = np.concatenate(ps, 0).astype(np.float32)
    pstack_t = np.concatenate([p.T for p in ps], 0).astype(np.float32)
    return (jnp.asarray(cstack, BF16), jnp.asarray(cstack[:HG_E_ROWS].T, BF16), jnp.asarray(pstack, F32),
            jnp.asarray(pstack_t, F32))


def _split_dot(c_bf, x):
    hi = _bf(x)
    lo = _bf(x - hi.astype(F32))
    r2 = jnp.dot(c_bf, jnp.concatenate([hi, lo], axis=1), preferred_element_type=F32)
    return r2[:, :HEAD] + r2[:, HEAD:]


def _hg_gates(hq, hf, lb):
    sq = _sigmoid(hq)
    sg = _sigmoid(hf)
    fg = lb + (1.0 - lb) * sg
    return sq, hq * sq, sg, fg, 1.0 - fg, jnp.log(fg)


def hgrn_fwd(proj_main, lb, consts, bl, lp, d):
    nh = d // HEAD
    rows_blk = _tile(lp, HG_TILE_MAX, SEQ_BLOCK)
    nb = lp // rows_blk
    spb = rows_blk // HG_SUB
    cstack, _, pstack, _ = consts

    def body(hq_ref, hf_ref, hi_ref, lb_ref, c_ref, p_ref, o_ref, st_ref, a_ref, s_ref):
        j = pl.program_id(2)

        @pl.when(j == 0)
        def _():
            s_ref[...] = jnp.zeros_like(s_ref)

        lbv = lb_ref[...]
        cs = c_ref[...]
        rows = [pl.ds(s * HG_SUB, HG_SUB) for s in range(spb)]
        gates = [_hg_gates(hq_ref[r, :].astype(F32), hf_ref[r, :].astype(F32), lbv) for r in rows]
        qs, ks = [g_[1] for g_ in gates], [g_[4] for g_ in gates]
        vs = [hi_ref[r, :].astype(F32) for r in rows]
        es = [_split_dot(cs, g_[5]) for g_ in gates]
        a_acc = [jnp.zeros((HG_SUB, HG_SUB), F32) for _ in rows]
        for lvl in range(HG_LEVELS):
            for s in range(spb):
                x = jnp.exp(es[s][lvl * HG_SUB:(lvl + 1) * HG_SUB])
                a_acc[s] = a_acc[s] + p_ref[pl.ds(lvl * HG_SUB, HG_SUB), :] * lax.dot_general(
                    _bf(qs[s] * x), _bf(ks[s] * x), NT_DIMS, preferred_element_type=F32)
        o_intra, qbs, kds, e_lasts = [], [], [], []
        for s in range(spb):
            a_bf = _bf(a_acc[s])
            a_ref[0, 0, s] = a_bf
            bc = es[s][HG_LEVELS * HG_SUB:HG_E_ROWS]
            b_last = jnp.tile(es[s][HG_E_ROWS:], (HG_SUB // 8, 1))
            o_intra.append(jnp.dot(a_bf, _bf(vs[s]), preferred_element_type=F32)
                           + jnp.sum(qs[s] * ks[s], axis=1, keepdims=True) * vs[s])
            qbs.append(_bf(qs[s] * jnp.exp(bc)))
            kds.append(_bf(ks[s] * jnp.exp(b_last - bc)))
            e_lasts.append(jnp.exp(b_last))
        st = s_ref[...]
        for s in range(spb):
            st_ref[0, 0, s] = st
            o_ref[rows[s], :] = (o_intra[s] + lax.dot_general(qbs[s], _bf(st), NT_DIMS, preferred_element_type=F32)
                                 ).astype(o_ref.dtype)
            st = st * e_lasts[s] + lax.dot_general(_bf(vs[s]), kds[s], TN_DIMS, preferred_element_type=F32)
        s_ref[...] = st

    def colspec(off):
        return pl.BlockSpec((rows_blk, HEAD), functools.partial(lambda h, b, j, off: (b * nb + j, off + h), off=off))

    whole = lambda arr: pl.BlockSpec(arr.shape, lambda h, b, j: (0, 0))
    return pl.pallas_call(
        body, name="hgrn_fwd", grid=(nh, bl, nb),
        in_specs=[colspec(0), colspec(nh), colspec(2 * nh), pl.BlockSpec((1, HEAD), lambda h, b, j: (0, h)),
                  whole(cstack), whole(pstack)],
        out_specs=[pl.BlockSpec((rows_blk, HEAD), lambda h, b, j: (b * nb + j, h)),
                   pl.BlockSpec((1, 1, spb, HEAD, HEAD), lambda h, b, j: (b, h, j, 0, 0)),
                   pl.BlockSpec((1, 1, spb, HG_SUB, HG_SUB), lambda h, b, j: (b, h, j, 0, 0))],
        out_shape=[jax.ShapeDtypeStruct((bl * lp, d), BF16),
                   jax.ShapeDtypeStruct((bl, nh, lp // HG_SUB, HEAD, HEAD), F32),
                   jax.ShapeDtypeStruct((bl, nh, lp // HG_SUB, HG_SUB, HG_SUB), BF16)],
        scratch_shapes=[pltpu.VMEM((HEAD, HEAD), F32)],
        compiler_params=pltpu.CompilerParams(dimension_semantics=("arbitrary", "arbitrary", "arbitrary")),
    )(proj_main, proj_main, proj_main, lb, cstack, pstack)


def hgrn_bwd(proj_main, lb, consts, states, a_mats, do_scan, bl, lp, d):
    nh = d // HEAD
    rows_blk = _tile(lp, HG_TILE_MAX, SEQ_BLOCK)
    nb = lp // rows_blk
    spb = rows_blk // HG_SUB
    cstack, cstack_t = consts[0], consts[1]
    pstack, pstack_t = _bf(consts[2]), _bf(consts[3])

    def body(hq_ref, hf_ref, hi_ref, lb_ref, c_ref, ct_ref, p_ref, pt_ref, st_ref, a_ref, do_ref,
             dq_ref, df_ref, di_ref, dlb_ref, ds_ref):
        b_id, j = pl.program_id(1), pl.program_id(2)
        blk = nb - 1 - j

        @pl.when(j == 0)
        def _():
            ds_ref[...] = jnp.zeros_like(ds_ref)

        @pl.when((j == 0) & (b_id == 0))
        def _():
            dlb_ref[...] = jnp.zeros_like(dlb_ref)

        lbv = lb_ref[...]
        cs = c_ref[...]
        cst = ct_ref[...]

        dlb = jnp.zeros((1, HEAD), F32)
        for first in reversed(range(0, spb, HG_BWD_GROUP)):
            dlb = dlb + _hg_group_bwd(list(range(first, min(first + HG_BWD_GROUP, spb))), lbv, cs, cst, hq_ref,
                                      hf_ref, hi_ref, st_ref, a_ref, do_ref, p_ref, pt_ref, dq_ref, df_ref, di_ref,
                                      ds_ref)
        dlb_ref[...] += dlb

    def _hg_group_bwd(ids, lbv, cs, cst, hq_ref, hf_ref, hi_ref, st_ref, a_ref, do_ref, p_ref, pt_ref, dq_ref,
                      df_ref, di_ref, ds_ref):
        rng = range(len(ids))
        rows = [pl.ds(s * HG_SUB, HG_SUB) for s in ids]
        hqs = [hq_ref[r, :].astype(F32) for r in rows]
        gates = [_hg_gates(hqs[s], hf_ref[rows[s], :].astype(F32), lbv) for s in rng]
        sqs, qs, sgs, fgs, ks = ([g_[i] for g_ in gates] for i in range(5))
        vs = [hi_ref[r, :].astype(F32) for r in rows]
        dos = [do_ref[r, :].astype(F32) for r in rows]
        sts = [st_ref[0, 0, s] for s in ids]
        es = [_split_dot(cs, g_[5]) for g_ in gates]
        bcs = [e[HG_LEVELS * HG_SUB:HG_E_ROWS] for e in es]
        b_lasts = [jnp.tile(e[HG_E_ROWS:], (HG_SUB // 8, 1)) for e in es]
        ebs = [jnp.exp(bc) for bc in bcs]
        qbs = [qs[s] * ebs[s] for s in rng]
        ers = [jnp.exp(b_lasts[s] - bcs[s]) for s in rng]
        kds = [ks[s] * ers[s] for s in rng]
        e_lasts = [jnp.exp(b) for b in b_lasts]
        do_bfs, v_bfs = [_bf(x) for x in dos], [_bf(x) for x in vs]
        das = [_bf(lax.dot_general(do_bfs[s], v_bfs[s], NT_DIMS, preferred_element_type=F32)) for s in rng]
        dats = [_bf(lax.dot_general(v_bfs[s], do_bfs[s], NT_DIMS, preferred_element_type=F32)) for s in rng]
        dqbs = [jnp.dot(do_bfs[s], _bf(sts[s]), preferred_element_type=F32) for s in rng]
        m_s = [lax.dot_general(do_bfs[s], _bf(qbs[s]), TN_DIMS, preferred_element_type=F32) for s in rng]
        dst_outs = [None] * len(ids)
        dst = ds_ref[...]
        for s in reversed(rng):
            dst_outs[s] = dst
            dst = dst * e_lasts[s] + m_s[s]
        ds_ref[...] = dst
        dst_bfs = [_bf(x) for x in dst_outs]
        d_diags = [jnp.sum(dos[s] * vs[s], axis=1, keepdims=True) for s in rng]
        dvs = [lax.dot_general(a_ref[0, 0, ids[s]], do_bfs[s], TN_DIMS, preferred_element_type=F32)
               + jnp.sum(qs[s] * ks[s], axis=1, keepdims=True) * dos[s]
               + lax.dot_general(_bf(kds[s]), dst_bfs[s], NT_DIMS, preferred_element_type=F32) for s in rng]
        dkds = [jnp.dot(v_bfs[s], dst_bfs[s], preferred_element_type=F32) for s in rng]
        dqs = [dqbs[s] * ebs[s] + d_diags[s] * ks[s] for s in rng]
        dks = [dkds[s] * ers[s] + d_diags[s] * qs[s] for s in rng]
        d_lasts = [jnp.sum(dst_outs[s] * sts[s] * e_lasts[s], axis=0, keepdims=True)
                   + jnp.sum(dkds[s] * kds[s], axis=0, keepdims=True) for s in rng]
        des = [[] for _ in rng]
        for lvl in range(HG_LEVELS):
            for s in rng:
                x = jnp.exp(es[s][lvl * HG_SUB:(lvl + 1) * HG_SUB])
                qh, kh = qs[s] * x, ks[s] * x
                dm = p_ref[pl.ds(lvl * HG_SUB, HG_SUB), :] * das[s]
                dmt = pt_ref[pl.ds(lvl * HG_SUB, HG_SUB), :] * dats[s]
                dqh = jnp.dot(dm, _bf(kh), preferred_element_type=F32)
                dkh = jnp.dot(dmt, _bf(qh), preferred_element_type=F32)
                dqs[s] = dqs[s] + dqh * x
                dks[s] = dks[s] + dkh * x
                des[s].append(dqh * qh + dkh * kh)
        dlb = jnp.zeros((1, HEAD), F32)
        for s in rng:
            des[s].append(dqbs[s] * qbs[s] - dkds[s] * kds[s])
            dg = _split_dot(cst, jnp.concatenate(des[s], axis=0)) + d_lasts[s]
            dfg = dg / fgs[s] - dks[s]
            dq_ref[rows[s], :] = (dqs[s] * (sqs[s] * (1.0 + hqs[s] * (1.0 - sqs[s])))).astype(dq_ref.dtype)
            df_ref[rows[s], :] = (dfg * (1.0 - lbv) * sgs[s] * (1.0 - sgs[s])).astype(df_ref.dtype)
            di_ref[rows[s], :] = dvs[s].astype(di_ref.dtype)
            dlb = dlb + jnp.sum(dfg * (1.0 - sgs[s]), axis=0, keepdims=True)
        return dlb

    def colspec(off):
        return pl.BlockSpec((rows_blk, HEAD),
                            functools.partial(lambda h, b, j, off: (b * nb + nb - 1 - j, off + h), off=off))

    whole = lambda arr: pl.BlockSpec(arr.shape, lambda h, b, j: (0, 0))
    mats = lambda: pl.BlockSpec((1, 1, spb, HEAD, HEAD), lambda h, b, j: (b, h, nb - 1 - j, 0, 0))
    t_rows = bl * lp
    return pl.pallas_call(
        body, name="hgrn_bwd", grid=(nh, bl, nb),
        in_specs=[colspec(0), colspec(nh), colspec(2 * nh), pl.BlockSpec((1, HEAD), lambda h, b, j: (0, h)),
                  whole(cstack), whole(cstack_t), whole(pstack), whole(pstack_t), mats(), mats(), colspec(0)],
        out_specs=[colspec(0), colspec(0), colspec(0), pl.BlockSpec((1, HEAD), lambda h, b, j: (0, h))],
        out_shape=[jax.ShapeDtypeStruct((t_rows, d), BF16)] * 3 + [jax.ShapeDtypeStruct((1, d), F32)],
        scratch_shapes=[pltpu.VMEM((HEAD, HEAD), F32)],
        compiler_params=pltpu.CompilerParams(dimension_semantics=("arbitrary", "arbitrary", "arbitrary")),
    )(proj_main, proj_main, proj_main, lb, cstack, cstack_t, pstack, pstack_t, states, a_mats, do_scan)


def _key_query_mask(key0, qry0, nk, nq_, causal):
    key = key0 + lax.broadcasted_iota(jnp.int32, (nk, 1), 0)
    if not causal:
        return key >= PAD_FRONT
    qry = qry0 + lax.broadcasted_iota(jnp.int32, (1, nq_), 1)
    return (key <= qry) & (key >= PAD_FRONT)


def _attn_tile(lp):
    return _tile(lp, ATTN_TILE_MAX, SEQ_BLOCK)


def attn_fwd_t(q_cat, k_cat, v_t, bl, lp, nm):
    tq = tk = _attn_tile(lp)
    nq = lp // tq
    hp = ATTN_HEADS_PER_STEP
    assert nm % hp == 0

    def body(q_ref, k_ref, vt_ref, o_ref, lse_ref, m_ref, l_ref, acc_ref):
        i = pl.program_id(2)
        m_ref[...] = jnp.full_like(m_ref, NEG)
        l_ref[...] = jnp.zeros_like(l_ref)
        acc_ref[...] = jnp.zeros_like(acc_ref)

        def step(c, mask):
            c0 = pl.multiple_of(c * tk, tk)
            for hh in range(hp):
                cols = pl.ds(hh * QK_PAD, QK_PAD)
                st = lax.dot_general(k_ref[pl.ds(c0, tk), cols], q_ref[:, cols], NT_DIMS,
                                     preferred_element_type=F32)
                if mask is not None:
                    st = jnp.where(_key_query_mask(c * tk, i * tq, tk, tq, mask == "causal"), st, NEG)
                m_old = m_ref[hh]
                m_new = jnp.maximum(m_old, jnp.max(st, axis=0, keepdims=True))
                alpha = jnp.exp(m_old - m_new)
                pt = jnp.exp(st - m_new)
                l_ref[hh] = alpha * l_ref[hh] + jnp.sum(pt, axis=0, keepdims=True)
                acc_ref[hh] = alpha * acc_ref[hh] + jnp.dot(vt_ref[0, hh, pl.ds(c, 1)][0], _bf(pt),
                                                            preferred_element_type=F32)
                m_ref[hh] = m_new

        def mid(c, carry):
            step(c, None)
            return carry

        @pl.when(i == 0)
        def _():
            step(0, "causal")

        @pl.when(i > 0)
        def _():
            step(0, "pad")
            lax.fori_loop(1, i, mid, 0)
            step(i, "causal")

        for hh in range(hp):
            o_ref[:, pl.ds(hh * HEAD, HEAD)] = jnp.transpose(acc_ref[hh] / l_ref[hh]).astype(o_ref.dtype)
            lse_ref[0, hh, 0] = m_ref[hh] + jnp.log(l_ref[hh])

    return pl.pallas_call(
        body, name="attn_fwd", grid=(bl, nm // hp, nq),
        in_specs=[pl.BlockSpec((tq, hp * QK_PAD), lambda b, h, i: (b * nq + i, h)),
                  pl.BlockSpec((lp, hp * QK_PAD), lambda b, h, i: (b, h)),
                  pl.BlockSpec((1, hp, nq, HEAD, tk), lambda b, h, i: (b, h, 0, 0, 0))],
        out_specs=[pl.BlockSpec((tq, hp * HEAD), lambda b, h, i: (b * nq + i, h)),
                   pl.BlockSpec((1, hp, 1, 1, tq), lambda b, h, i: (b, h, i, 0, 0))],
        out_shape=[jax.ShapeDtypeStruct((bl * lp, nm * HEAD), BF16),
                   jax.ShapeDtypeStruct((bl, nm, nq, 1, tq), F32)],
        scratch_shapes=[pltpu.VMEM((hp, 1, tq), F32), pltpu.VMEM((hp, 1, tq), F32), pltpu.VMEM((hp, HEAD, tq), F32)],
        compiler_params=pltpu.CompilerParams(dimension_semantics=("arbitrary", "arbitrary", "arbitrary")),
    )(q_cat, k_cat, v_t)


def attn_bwd_t(q_cat, k_cat, k_t, v, o, do, lse, bl, lp, nm):
    tq = tk = _attn_tile(lp)
    nq = lp // tq
    hp = ATTN_HEADS_PER_STEP
    assert nm % hp == 0

    def body(q_ref, k_ref, kt_ref, v_ref, o_ref, do_ref, lse_ref, dq_ref, dk_ref, dv_ref, dqt_ref, dka_ref, dva_ref):
        i = pl.program_id(2)

        @pl.when(i == 0)
        def _():
            dqt_ref[...] = jnp.zeros_like(dqt_ref)

        dka_ref[...] = jnp.zeros_like(dka_ref)
        dva_ref[...] = jnp.zeros_like(dva_ref)
        ones8 = jnp.ones((8, HEAD), BF16)

        def step(c, mask):
            c0 = pl.multiple_of(c * tq, tq)
            for hh in range(hp):
                qcols, vcols = pl.ds(hh * QK_PAD, QK_PAD), pl.ds(hh * HEAD, HEAD)
                qs = q_ref[pl.ds(c0, tq), qcols]
                dos = do_ref[pl.ds(c0, tq), vcols]
                prod = dos.astype(F32) * o_ref[pl.ds(c0, tq), vcols].astype(F32)
                hi = _bf(prod)
                lo = _bf(prod - hi.astype(F32))
                delta8 = (lax.dot_general(ones8, hi, NT_DIMS, preferred_element_type=F32)
                          + lax.dot_general(ones8, lo, NT_DIMS, preferred_element_type=F32))
                st = lax.dot_general(k_ref[:, qcols], qs, NT_DIMS, preferred_element_type=F32)
                pt = jnp.exp(st - lse_ref[0, hh, pl.ds(c, 1)][0])
                if mask is not None:
                    pt = jnp.where(_key_query_mask(i * tk, c * tq, tk, tq, mask == "causal"), pt, 0.0)
                dva_ref[hh] += jnp.dot(_bf(pt), dos, preferred_element_type=F32)
                dpt = lax.dot_general(v_ref[:, vcols], dos, NT_DIMS, preferred_element_type=F32)
                dst = _bf(pt * (dpt - jnp.tile(delta8, (tk // 8, 1))))
                dka_ref[hh] += jnp.dot(dst, qs, preferred_element_type=F32)
                dqt_ref[hh, pl.ds(c, 1)] += jnp.dot(kt_ref[0, hh, 0], dst, preferred_element_type=F32)[None]

        step(i, "causal")

        def rest_masked(c, carry):
            step(c, "pad")
            return carry

        def rest(c, carry):
            step(c, None)
            return carry

        @pl.when(i == 0)
        def _():
            lax.fori_loop(1, nq, rest_masked, 0)

        @pl.when(i > 0)
        def _():
            lax.fori_loop(i + 1, nq, rest, 0)

        for hh in range(hp):
            dk_ref[:, pl.ds(hh * QK_PAD, QK_PAD)] = dka_ref[hh].astype(dk_ref.dtype)
            dv_ref[:, pl.ds(hh * HEAD, HEAD)] = dva_ref[hh].astype(dv_ref.dtype)

        @pl.when(i == nq - 1)
        def _():
            for hh in range(hp):
                for c in range(nq):
                    dq_ref[pl.ds(c * tq, tq), pl.ds(hh * QK_PAD, QK_PAD)] = (
                        jnp.transpose(dqt_ref[hh, c])).astype(dq_ref.dtype)

    return pl.pallas_call(
        body, name="attn_bwd", grid=(bl, nm // hp, nq),
        in_specs=[pl.BlockSpec((lp, hp * QK_PAD), lambda b, h, i: (b, h)),
                  pl.BlockSpec((tk, hp * QK_PAD), lambda b, h, i: (b * nq + i, h)),
                  pl.BlockSpec((1, hp, 1, QK_PAD, tk), lambda b, h, i: (b, h, i, 0, 0)),
                  pl.BlockSpec((tk, hp * HEAD), lambda b, h, i: (b * nq + i, h)),
                  pl.BlockSpec((lp, hp * HEAD), lambda b, h, i: (b, h)),
                  pl.BlockSpec((lp, hp * HEAD), lambda b, h, i: (b, h)),
                  pl.BlockSpec((1, hp, nq, 1, tq), lambda b, h, i: (b, h, 0, 0, 0))],
        out_specs=[pl.BlockSpec((lp, hp * QK_PAD), lambda b, h, i: (b, h)),
                   pl.BlockSpec((tk, hp * QK_PAD), lambda b, h, i: (b * nq + i, h)),
                   pl.BlockSpec((tk, hp * HEAD), lambda b, h, i: (b * nq + i, h))],
        out_shape=[jax.ShapeDtypeStruct((bl * lp, nm * QK_PAD), BF16),
                   jax.ShapeDtypeStruct((bl * lp, nm * QK_PAD), BF16),
                   jax.ShapeDtypeStruct((bl * lp, nm * HEAD), BF16)],
        scratch_shapes=[pltpu.VMEM((hp, nq, QK_PAD, tq), F32), pltpu.VMEM((hp, tk, QK_PAD), F32),
                        pltpu.VMEM((hp, tk, HEAD), F32)],
        compiler_params=pltpu.CompilerParams(dimension_semantics=("arbitrary", "arbitrary", "arbitrary")),
    )(q_cat, k_cat, k_t, v, o, do, lse)


def _place():
    return lax.axis_index("x"), lax.axis_index("y"), lax.axis_index("c")


def gather_shards(packed):
    hbm = pl.BlockSpec(memory_space=pl.ANY)

    def body(src_ref, out_ref, send_sems, recv_sems, local_sem):
        x, y, c = _place()
        me = 2 * x + y
        chips = [(1 - x, y), (x, 1 - y), (1 - x, 1 - y)]
        local = pltpu.make_async_copy(src_ref, out_ref.at[me], local_sem)
        local.start()
        sends = []
        for k, (px, py) in enumerate(chips):
            cp = pltpu.make_async_remote_copy(src_ref=src_ref, dst_ref=out_ref.at[me], send_sem=send_sems.at[k],
                                              recv_sem=recv_sems.at[k], device_id=(px, py, c), device_id_type=MESH)
            cp.start()
            sends.append(cp)
        for k, (px, py) in enumerate(chips):
            pltpu.make_async_remote_copy(src_ref=src_ref, dst_ref=out_ref.at[2 * px + py], send_sem=send_sems.at[k],
                                         recv_sem=recv_sems.at[k], device_id=(px, py, c),
                                         device_id_type=MESH).wait_recv()
        for cp in sends:
            cp.wait_send()
        local.wait()

    return pl.pallas_call(
        body, name="gather_shards", in_specs=[hbm], out_specs=hbm,
        out_shape=jax.ShapeDtypeStruct((4,) + packed.shape, packed.dtype),
        scratch_shapes=[pltpu.SemaphoreType.DMA((3,)), pltpu.SemaphoreType.DMA((3,)), pltpu.SemaphoreType.DMA],
    )(packed)


def gather_small(small):
    hbm = pl.BlockSpec(memory_space=pl.ANY)

    def body(small_ref, all_ref, send_sems, recv_sems, local_sem):
        x, y, c = _place()
        me = 4 * x + 2 * y + c
        local = pltpu.make_async_copy(small_ref, all_ref.at[me], local_sem)
        local.start()
        others = [(x ^ ((r >> 2) & 1), y ^ ((r >> 1) & 1), c ^ (r & 1)) for r in range(1, 8)]
        sends = []
        for r, peer in enumerate(others):
            cp = pltpu.make_async_remote_copy(src_ref=small_ref, dst_ref=all_ref.at[me], send_sem=send_sems.at[r],
                                              recv_sem=recv_sems.at[r], device_id=peer, device_id_type=MESH)
            cp.start()
            sends.append(cp)
        for r, (px, py, pc) in enumerate(others):
            pltpu.make_async_remote_copy(src_ref=small_ref, dst_ref=all_ref.at[4 * px + 2 * py + pc],
                                         send_sem=send_sems.at[r], recv_sem=recv_sems.at[r],
                                         device_id=(px, py, pc), device_id_type=MESH).wait_recv()
        for cp in sends:
            cp.wait_send()
        local.wait()

    return pl.pallas_call(
        body, name="gather_small", in_specs=[hbm], out_specs=hbm,
        out_shape=jax.ShapeDtypeStruct((8,) + small.shape, small.dtype),
        scratch_shapes=[pltpu.SemaphoreType.DMA((7,)), pltpu.SemaphoreType.DMA((7,)), pltpu.SemaphoreType.DMA],
    )(small)


def swap_with_sibling(name, parts):
    n = len(parts)
    hbm = pl.BlockSpec(memory_space=pl.ANY)

    def body(*refs):
        x, y, c = _place()
        cps = [pltpu.make_async_remote_copy(src_ref=refs[j], dst_ref=refs[n + j], send_sem=refs[2 * n].at[j],
                                            recv_sem=refs[2 * n + 1].at[j], device_id=(x, y, 1 - c),
                                            device_id_type=MESH) for j in range(n)]
        for cp in cps:
            cp.start()
        for cp in cps:
            cp.wait()

    return pl.pallas_call(
        body, name=name, in_specs=[hbm] * n, out_specs=[hbm] * n,
        out_shape=[jax.ShapeDtypeStruct(p.shape, p.dtype) for p in parts],
        scratch_shapes=[pltpu.SemaphoreType.DMA((n,)), pltpu.SemaphoreType.DMA((n,))],
    )(*parts)


def _chips3():
    x, y, c = _place()
    return [(1 - x, y, c), (x, 1 - y, c), (1 - x, 1 - y, c)]


def _push_copies(src_refs, land_refs, send_sems, recv_sems, per_chip):
    x, y, _ = _place()
    cps = []
    for j, (src_ref, land_ref) in enumerate(zip(src_refs, land_refs)):
        for k, (px, py, pc) in enumerate(_chips3()):
            part = src_ref.at[2 * px + py] if per_chip else src_ref
            slot = k if per_chip else 2 * x + y
            cps.append(pltpu.make_async_remote_copy(
                src_ref=part, dst_ref=land_ref.at[slot], send_sem=send_sems.at[3 * j + k],
                recv_sem=recv_sems.at[3 * j + k], device_id=(px, py, pc), device_id_type=MESH))
    return cps


def push_start(name, srcs, per_chip):
    n = len(srcs)
    hbm = pl.BlockSpec(memory_space=pltpu.HBM)
    sem = pl.BlockSpec(memory_space=pltpu.SEMAPHORE)
    lands = [lax.empty((3 if per_chip else 4,) + s.shape[-2:], s.dtype) for s in srcs]

    def body(*refs):
        src_refs, land_refs = refs[:n], refs[n:2 * n]
        send_sems, recv_sems = refs[2 * n], refs[2 * n + 1]
        for cp in _push_copies(src_refs, land_refs, send_sems, recv_sems, per_chip):
            cp.start()
        refs[-1][...] = jnp.zeros_like(refs[-1])

    outs = pl.pallas_call(
        body, name=name,
        out_shape=(pltpu.SemaphoreType.DMA((3 * n,)), pltpu.SemaphoreType.DMA((3 * n,)),
                   *[pltpu.HBM(a.shape, a.dtype) for a in list(srcs) + lands], jax.ShapeDtypeStruct((8, HEAD), F32)),
        in_specs=(hbm,) * (2 * n),
        out_specs=(sem, sem) + (hbm,) * (2 * n) + (pl.BlockSpec(memory_space=pltpu.VMEM),),
        input_output_aliases={j: 2 + j for j in range(2 * n)},
        compiler_params=pltpu.CompilerParams(has_side_effects=pltpu.SideEffectType.DATAFLOW_SIDE_EFFECTING),
    )(*[pltpu.with_memory_space_constraint(a, pltpu.HBM) for a in list(srcs) + lands])
    return tuple(outs[:-1]), outs[-1]


def push_wait(name, handle, after, per_chip):
    send_sems, recv_sems = handle[0], handle[1]
    thru = handle[2:]
    n = len(thru) // 2
    hbm = pl.BlockSpec(memory_space=pltpu.HBM)
    sem = pl.BlockSpec(memory_space=pltpu.SEMAPHORE)

    def body(*refs):
        src_refs, land_refs = refs[:n], refs[n:2 * n]
        for cp in _push_copies(src_refs, land_refs, refs[2 * n], refs[2 * n + 1], per_chip):
            cp.wait_send()
            cp.wait_recv()

    outs = pl.pallas_call(
        body, name=name,
        out_shape=tuple(pltpu.HBM(a.shape, a.dtype) for a in thru),
        in_specs=(hbm,) * (2 * n) + (sem, sem, pl.BlockSpec(memory_space=pl.ANY)), out_specs=(hbm,) * (2 * n),
        input_output_aliases={j: j for j in range(2 * n)},
        compiler_params=pltpu.CompilerParams(has_side_effects=pltpu.SideEffectType.DATAFLOW_SIDE_EFFECTING),
    )(*thru, send_sems, recv_sems, after)
    return outs[:n], outs[n:]


def join_gathered(name, own, landed, my_chip):
    blocks = lax.dynamic_update_index_in_dim(landed, own, my_chip, 0)
    _, r, c = blocks.shape
    if name in COL_SHARDED:
        return blocks.transpose(1, 0, 2).reshape(r, 4 * c)
    return blocks.reshape(4 * r, c)


def adamw(name, w, g_parts, m, v):
    r, c = w.shape
    tr = r if r * c <= 65536 else _tile(r, 128, 8)
    ng = len(g_parts)

    def body(*refs):
        w_ref, m_ref, v_ref = refs[0], refs[1 + ng], refs[2 + ng]
        g_ref, d_ref, nm_ref, nv_ref = refs[3 + ng:]
        gv = refs[1][...]
        for k in range(1, ng):
            gv = gv + refs[1 + k][...]
        m_new = ADAM_B1 * m_ref[...] + (1.0 - ADAM_B1) * gv
        v_new = ADAM_B2 * v_ref[...] + (1.0 - ADAM_B2) * (gv * gv)
        m_hat = m_new / (1.0 - ADAM_B1 ** ADAM_STEP)
        v_hat = v_new / (1.0 - ADAM_B2 ** ADAM_STEP)
        g_ref[...] = gv
        d_ref[...] = -ADAM_LR * (m_hat / (jnp.sqrt(v_hat) + ADAM_EPS) + ADAM_WD * w_ref[...])
        nm_ref[...] = m_new
        nv_ref[...] = v_new

    spec = pl.BlockSpec((tr, c), lambda i: (i, 0))
    return pl.pallas_call(
        body, name=name, grid=(r // tr,), in_specs=[spec] * (3 + ng), out_specs=[spec] * 4,
        out_shape=[jax.ShapeDtypeStruct((r, c), F32)] * 4,
        compiler_params=pltpu.CompilerParams(dimension_semantics=("arbitrary",)),
    )(w, *g_parts, m, v)


def split_full(name, full, s):
    if name in COL_SHARDED:
        c = full.shape[1] // 4
        return full[:, s * c:(s + 1) * c]
    r = full.shape[0] // 4
    return full[s * r:(s + 1) * r]


def kernel(x, meta_tokens, w_in, b_gate, lb_logits, hg_norm_g, w_hg_o, q_a_norm_g, w_q_b, kv_a_norm_g, w_kv_b, w_mla_o, w_out, mix_pre_g, mix_post_g, ffn_pre_g, ffn_post_g, w_ffn_in, w_ffn_out, loss_target, m_meta_tokens, m_w_in, m_b_gate, m_lb_logits, m_hg_norm_g, m_w_hg_o, m_q_a_norm_g, m_w_q_b, m_kv_a_norm_g, m_w_kv_b, m_w_mla_o, m_w_out, m_mix_pre_g, m_mix_post_g, m_ffn_pre_g, m_ffn_post_g, m_w_ffn_in, m_w_ffn_out, v_meta_tokens, v_w_in, v_b_gate, v_lb_logits, v_hg_norm_g, v_w_hg_o, v_q_a_norm_g, v_w_q_b, v_kv_a_norm_g, v_w_kv_b, v_w_mla_o, v_w_out, v_mix_pre_g, v_mix_post_g, v_ffn_pre_g, v_ffn_post_g, v_w_ffn_in, v_w_ffn_out):
    wts = dict(meta_tokens=meta_tokens, w_in=w_in[0], b_gate=b_gate, lb_logits=lb_logits, hg_norm_g=hg_norm_g,
               w_hg_o=w_hg_o[0], q_a_norm_g=q_a_norm_g, w_q_b=w_q_b[0], kv_a_norm_g=kv_a_norm_g, w_kv_b=w_kv_b[0],
               w_mla_o=w_mla_o[0], w_out=w_out[0], mix_pre_g=mix_pre_g, mix_post_g=mix_post_g, ffn_pre_g=ffn_pre_g,
               ffn_post_g=ffn_post_g, w_ffn_in=w_ffn_in[0], w_ffn_out=w_ffn_out[0])
    mom_m = dict(meta_tokens=m_meta_tokens, w_in=m_w_in[0], b_gate=m_b_gate, lb_logits=m_lb_logits,
                 hg_norm_g=m_hg_norm_g, w_hg_o=m_w_hg_o[0], q_a_norm_g=m_q_a_norm_g, w_q_b=m_w_q_b[0],
                 kv_a_norm_g=m_kv_a_norm_g, w_kv_b=m_w_kv_b[0], w_mla_o=m_w_mla_o[0], w_out=m_w_out[0],
                 mix_pre_g=m_mix_pre_g, mix_post_g=m_mix_post_g, ffn_pre_g=m_ffn_pre_g, ffn_post_g=m_ffn_post_g,
                 w_ffn_in=m_w_ffn_in[0], w_ffn_out=m_w_ffn_out[0])
    mom_v = dict(meta_tokens=v_meta_tokens, w_in=v_w_in[0], b_gate=v_b_gate, lb_logits=v_lb_logits,
                 hg_norm_g=v_hg_norm_g, w_hg_o=v_w_hg_o[0], q_a_norm_g=v_q_a_norm_g, w_q_b=v_w_q_b[0],
                 kv_a_norm_g=v_kv_a_norm_g, w_kv_b=v_w_kv_b[0], w_mla_o=v_w_mla_o[0], w_out=v_w_out[0],
                 mix_pre_g=v_mix_pre_g, mix_post_g=v_mix_post_g, ffn_pre_g=v_ffn_pre_g, ffn_post_g=v_ffn_post_g,
                 w_ffn_in=v_w_ffn_in[0], w_ffn_out=v_w_ffn_out[0])

    bl, seq, d = x.shape
    lp = PAD_FRONT + N_META + seq
    t_rows = bl * lp
    nh = d // HEAD
    ql, kvl = wts["w_q_b"].shape[0], wts["w_kv_b"].shape[0]
    nm = (4 * wts["w_mla_o"].shape[0]) // HEAD
    ffn = 4 * wts["w_ffn_out"].shape[0]
    mla_w = ql + kvl + HEAD
    assert ql == kvl and ql % HEAD == 0 and seq % SEQ_BLOCK == 0 and d % HEAD == 0
    scale = (HEAD + ROPE) ** -0.5
    my_chip = 2 * lax.axis_index("x") + lax.axis_index("y")

    mcols = meta_tokens.shape[1]
    meta_all = gather_shards(meta_tokens)
    meta_full = jnp.concatenate([meta_all[s] for s in range(4)], axis=1)

    def start_gather(name, names, order_after):
        srcs = [_bf(wts[n]) for n in names]
        if order_after is not None:
            srcs[0] = srcs[0] + order_after[0, 0].astype(BF16)
        return push_start(name, srcs, per_chip=False)

    def finish_gather(name, names, started, after):
        owns, landed = push_wait(name, started[0], after, per_chip=False)
        return {n: join_gathered(n, own, land, my_chip) for n, own, land in zip(names, owns, landed)}

    rest_names = tuple(n for n in BIG if n != "w_in")
    my_c = lax.axis_index("c")
    w_in_bf = _bf(wts["w_in"])
    half = w_in_bf.shape[0] // 2
    own_half = (lax.dynamic_slice_in_dim(w_in_bf, my_c * half, half, axis=0)
                + (meta_all[0, :1, :1] * 0.0)[0, 0].astype(BF16))
    gather_1 = push_start("gather_w_in_start", [own_half], per_chip=False)
    gather_2 = start_gather("gather_rest_start", rest_names, gather_1[1])

    tiles_seq, tiles_real = lp // SEQ_BLOCK, seq // SEQ_BLOCK
    assert PAD_FRONT + N_META == SEQ_BLOCK

    def real_block(i):
        return (i // tiles_seq) * tiles_real + jnp.maximum(i % tiles_seq - 1, 0)

    meta_rows = jnp.broadcast_to(((jnp.arange(lp) >= PAD_FRONT) & (jnp.arange(lp) < PAD_FRONT + N_META)
                                  ).astype(F32)[:, None], (lp, HEAD))
    pos = (jnp.arange(lp, dtype=jnp.int32) - PAD_FRONT).astype(F32)
    inv_freq = 1.0 / (ROPE_THETA ** (jnp.arange(0, ROPE, 2, dtype=F32) / ROPE))
    ang = pos[:, None] * inv_freq[None, :]
    zeros32 = jnp.zeros((lp, ROPE_HALF), F32)
    zeros64 = jnp.zeros((lp, HEAD - ROPE), F32)
    t_cos = jnp.concatenate([jnp.cos(ang), jnp.cos(ang), zeros64], axis=1)
    t_up = jnp.concatenate([zeros32, jnp.sin(ang), zeros64], axis=1)
    t_dn = jnp.concatenate([-jnp.sin(ang), zeros32, zeros64], axis=1)
    real = jnp.broadcast_to((jnp.arange(lp) >= PAD_FRONT + N_META).astype(F32)[:, None], (lp, HEAD))
    lanes = d // HEAD
    lb_soft = jax.nn.softmax(lb_logits.astype(F32), axis=0)
    lb = lb_soft[0:1]

    meta_tile = jnp.concatenate([jnp.zeros((PAD_FRONT, d), F32), meta_full], axis=0)

    def first_fn(xv, is_real, is_meta, mtile, g):
        h = xv * jnp.tile(is_real, (1, lanes)) + mtile * jnp.tile(is_meta, (1, lanes))
        return _rms(h, g), h

    u1, h0 = rowwise("norm_mix_pre", first_fn, [(x.reshape(bl * seq, d), d, 0, real_block)], [real, meta_rows],
                     [meta_tile, mix_pre_g + gather_2[1][0, 0]], [(d, BF16), (d, F32)], n_rows=t_rows)
    _, (fetched,) = push_wait("gather_w_in_wait", gather_1[0], u1, per_chip=False)
    (handed,) = swap_with_sibling("swap_w_in", [fetched])
    halves = jnp.stack([fetched, handed])
    remote = jnp.concatenate([lax.dynamic_index_in_dim(halves, my_c, 0, keepdims=False),
                              lax.dynamic_index_in_dim(halves, 1 - my_c, 0, keepdims=False)], axis=1)
    full = {"w_in": join_gathered("w_in", w_in_bf, remote, my_chip)}
    w_main = jnp.concatenate([full["w_in"][:, :4 * d], full["w_in"][:, -2 * d:]], axis=1)
    w_mla = jnp.pad(full["w_in"][:, 4 * d:4 * d + ql + kvl + ROPE], ((0, 0), (0, HEAD - ROPE)))
    proj_main = matmul("proj_main", u1, w_main, "nn", out_dtype=BF16)
    proj_mla = matmul("proj_mla", u1, w_mla, "nn", out_dtype=BF16)
    hg_consts = _hg_constants()
    o_scan, states, a_mats = hgrn_fwd(proj_main, lb, hg_consts, bl, lp, d)

    full.update(finish_gather("gather_rest_wait", rest_names, gather_2, o_scan))
    w_qb = jnp.pad(full["w_q_b"].reshape(ql, nm, HEAD + ROPE), ((0, 0), (0, 0), (0, QK_PAD - HEAD - ROPE))
                   ).reshape(ql, nm * QK_PAD)
    w_kvb = full["w_kv_b"]

    def hg_out_fn(o, hg, g):
        ov = jnp.concatenate([_rms(o[:, h * HEAD:(h + 1) * HEAD], g) for h in range(nh)], axis=1) * _silu(hg)
        return ov, ov

    y_a, o_hg = matmul_fused("hgrn_out_y_a", hg_out_fn, [(o_scan, d, 0), (proj_main, d, 3)], [hg_norm_g],
                             _bf(full["w_hg_o"]), [(0, d)], "nn", [(d, BF16)], tm=512)

    def seq_tile(i):
        return i % tiles_seq

    tables = [(t_cos, HEAD, 0, seq_tile), (t_up, HEAD, 0, seq_tile), (t_dn, HEAD, 0, seq_tile)]

    def q_norm_fn(cq, cos, s_up, s_dn, g):
        cn = _rms(cq, g)
        return cn, cn

    def q_rope_fn(products, vals):
        qf = products[0] * scale
        cos, s_up, s_dn = vals[1:4]
        qs = []
        for h in range(nm):
            qs += [qf[:, h * QK_PAD:h * QK_PAD + HEAD], _rope(qf[:, h * QK_PAD + HEAD:(h + 1) * QK_PAD], cos, s_up, s_dn)]
        return [jnp.concatenate(qs, axis=1)], []

    q_cat, qn = matmul_fused("q_norm_up_rope", q_norm_fn, [(proj_mla, ql, 0)] + tables, [q_a_norm_g], w_qb,
                             [(0, ql)], "nn", [(ql, BF16)], epilogue=q_rope_fn, epi_outs=[(nm * QK_PAD, BF16)])

    def kv_norm_fn(ckv, kpe, cos, s_up, s_dn, g):
        cn = _rms(ckv, g)
        return cn, cn

    def kv_rope_fn(products, vals):
        kvf = products[0]
        kpe_r = _rope(vals[1], *vals[2:5])
        ks, vs = [], []
        for h in range(nm):
            ks += [kvf[:, h * QK_PAD:h * QK_PAD + HEAD], kpe_r]
            vs += [kvf[:, h * QK_PAD + HEAD:(h + 1) * QK_PAD]]
        return [jnp.concatenate(ks, axis=1), jnp.concatenate(vs, axis=1)], []

    kpe_blk = (ql + kvl) // HEAD
    k_cat, v_att, kvn = matmul_fused("kv_norm_up_rope", kv_norm_fn,
                                     [(proj_mla, kvl, 1), (proj_mla, HEAD, kpe_blk)] + tables, [kv_a_norm_g], w_kvb,
                                     [(0, kvl)], "nn", [(kvl, BF16)], epilogue=kv_rope_fn,
                                     epi_outs=[(nm * QK_PAD, BF16), (nm * HEAD, BF16)])
    at = _attn_tile(lp)
    v_t = v_att.reshape(bl, lp // at, at, nm, HEAD).transpose(0, 3, 1, 4, 2)
    k_t = k_cat.reshape(bl, lp // at, at, nm, QK_PAD).transpose(0, 3, 1, 4, 2)
    o_mla, lse = attn_fwd_t(q_cat, k_cat, v_t, bl, lp, nm)
    y_b = matmul("y_b", o_mla, _bf(full["w_mla_o"]), "nn", out_dtype=BF16)

    def gate_fn(ya, yb, ga, gb, bias):
        zv = _sigmoid(ga + bias[:, :d]) * ya + _sigmoid(gb + bias[:, d:]) * yb
        return zv, zv

    mixed, z = matmul_fused("gate_mix_out", gate_fn,
                            [(y_a, d, 0), (y_b, d, 0), (proj_main, d, 4), (proj_main, d, 5)], [b_gate],
                            _bf(full["w_out"]), [(0, d)], "nn", [(d, BF16)], tm=512)

    def mid_fn(h, mx, g_post, g_pre):
        h1v = h + _rms(mx, g_post)
        u2v = _rms(h1v, g_pre)
        return u2v, h1v, u2v

    gu, h1, u2 = matmul_fused("norm_mid_ffn_in", mid_fn, [(h0, d, 0), (mixed, d, 0)], [mix_post_g, ffn_pre_g],
                              _bf(full["w_ffn_in"]), [(0, d)], "nn", [(d, F32), (d, BF16)])
    def swiglu_fn(gt, up):
        a = _silu(gt) * up
        return a, a

    f_out, act = matmul_fused("swiglu_ffn_out", swiglu_fn, [(gu, ffn, 0), (gu, ffn, 1)], [],
                              _bf(full["w_ffn_out"]), [(0, ffn)], "nn", [(ffn, BF16)])

    def loss_fn(h1v, fv, tg, realv, g_post):
        h2 = h1v + _rms(fv, g_post)
        diff = (h2 - tg) * jnp.tile(realv, (1, lanes))
        part = jnp.broadcast_to(0.5 * jnp.sum(diff * diff, keepdims=True) / d, (1, HEAD))
        dy = diff / d
        df, dg = _rms_bwd(fv, g_post, dy)
        return df, dy, df, part, dg

    d_act, dy, df, loss_part, g_ffn_post = matmul_fused(
        "loss_head_d_act", loss_fn,
        [(h1, d, 0), (f_out, d, 0), (loss_target.reshape(bl * seq, d), d, 0, real_block), (real, HEAD, 0, seq_tile)],
        [ffn_post_g], _bf(full["w_ffn_out"]), [(0, d)], "nt", [(d, BF16), (d, BF16)],
        acc_outs=[(1, HEAD), (1, d)])
    grads = {}
    grads["w_ffn_out"] = matmul("gw_ffn_out", act, df, "tn")

    def swiglu_bwd_fn(gt, up, da):
        dgt, dup = da * up * _silu_grad(gt), da * _silu(gt)
        return dgt, dup, jnp.concatenate([dgt, dup], axis=1)

    du2, dgu = matmul_fused("swiglu_bwd_d_u2", swiglu_bwd_fn, [(gu, ffn, 0), (gu, ffn, 1), (d_act, ffn, 0)], [],
                            _bf(full["w_ffn_in"]), [(0, ffn), (ffn, 2 * ffn)], "nt", [(2 * ffn, BF16)])
    grads["w_ffn_in"] = matmul("gw_ffn_in", u2, dgu, "tn")

    def mid_bwd_fn(dyv, h1v, du2v, mx, g_pre, g_post):
        dx, dg_pre = _rms_bwd(h1v, g_pre, du2v)
        dh1 = dyv + dx
        dmx, dg_post = _rms_bwd(mx, g_post, dh1)
        return dmx, dh1, dmx, dg_pre, dg_post

    dz, dh1, dmixed, g_ffn_pre, g_mix_post = matmul_fused(
        "norm_mid_bwd_d_z", mid_bwd_fn, [(dy, d, 0), (h1, d, 0), (du2, d, 0), (mixed, d, 0)],
        [ffn_pre_g, mix_post_g], _bf(full["w_out"]), [(0, d)], "nt", [(d, BF16), (d, BF16)], tm=512,
        acc_outs=[(1, d), (1, d)])
    grads["w_out"] = matmul("gw_out", z, dmixed, "tn")

    def gate_bwd_fn(dzv, ya, yb, ga, gb, bias):
        sa, sb = _sigmoid(ga + bias[:, :d]), _sigmoid(gb + bias[:, d:])
        dga = dzv * ya * sa * (1.0 - sa)
        dgb = dzv * yb * sb * (1.0 - sb)
        dgates = jnp.concatenate([dga, dgb], axis=1)
        dya, dyb = dzv * sa, dzv * sb
        return dya, dyb, dya, dyb, dgates, jnp.sum(dgates, axis=0, keepdims=True)

    def hg_out_bwd_fn(products, vals):
        do, o, hg, g = products[0], vals[5], vals[6], vals[8]
        dn = do * _silu(hg)
        dos, dgs, ons = [], 0.0, []
        for h in range(nh):
            sl = slice(h * HEAD, (h + 1) * HEAD)
            dx, dg = _rms_bwd(o[:, sl], g, dn[:, sl])
            dos.append(dx)
            dgs = dgs + dg
            ons.append(_rms(o[:, sl], g))
        dhg_v = do * jnp.concatenate(ons, axis=1) * _silu_grad(hg)
        return [jnp.concatenate(dos, axis=1), products[1], dhg_v], [dgs]

    do_scan, do_mla, dhg, dy_a, dy_b, dgates, g_b_gate, g_hg_norm = matmul_fused(
        "gate_mix_bwd_d_o", lambda dzv, ya, yb, ga, gb, o, hg, bias, g: gate_bwd_fn(dzv, ya, yb, ga, gb, bias),
        [(dz, d, 0), (y_a, d, 0), (y_b, d, 0), (proj_main, d, 4), (proj_main, d, 5), (o_scan, d, 0),
         (proj_main, d, 3)], [b_gate, hg_norm_g],
        [_bf(full["w_hg_o"]), _bf(full["w_mla_o"])], [(0, 0, d), (1, 0, d)], "nt",
        [(d, BF16), (d, BF16), (2 * d, BF16)], acc_outs=[(1, 2 * d), (1, HEAD)],
        epilogue=hg_out_bwd_fn, epi_outs=[(d, BF16), (d, BF16), (d, BF16)])
    grads["w_hg_o"] = matmul("gw_hg_o", o_hg, dy_a, "tn")
    grads["w_mla_o"] = matmul("gw_mla_o", o_mla, dy_b, "tn")

    early = ("w_hg_o", "w_mla_o", "w_out", "w_ffn_in", "w_ffn_out")
    late = ("w_in", "w_q_b", "w_kv_b")

    def start_grads(name, names):
        sends = [_bf(jnp.stack([split_full(n, grads[n], s) for s in range(4)])) for n in names]
        mines = []
        for n in names:
            r, c = wts[n].shape
            axis, size = (1, c) if n in COL_SHARDED else (0, r)
            mines.append(lax.dynamic_slice_in_dim(grads[n], my_chip * size, size, axis=axis))
        handle, token = push_start(name, sends, per_chip=True)
        return handle, token, mines

    def finish_grads(tag, names, started, after):
        handle, _, mines = started
        _, landed = push_wait(f"grads_{tag}_wait", handle, after, per_chip=True)
        parts = []
        for n, mine, land in zip(names, mines, landed):
            r, c = mine.shape
            tr = _tile(r, 256, 16)
            land2 = land.reshape(3 * r, c)
            parts.append(rowwise(f"sum_chips_{n}", lambda a, r0, r1, r2: a + r0 + r1 + r2,
                                 [(mine, c, 0)] + [(land2, c, 0, k * (r // tr)) for k in range(3)],
                                 [], [], [(c, F32)], tm=tr)[0])
        sibs = swap_with_sibling(f"swap_{tag}", parts)
        return {n: [p, s] for n, p, s in zip(names, parts, sibs)}

    grads_early = start_grads("grads_early_start", early)
    token_a = grads_early[1]

    dhq, dhf, dhi, g_lb = hgrn_bwd(proj_main, lb + token_a[0, 0], hg_consts, states, a_mats, do_scan, bl, lp, d)

    dq_cat, dk_cat, dv_att = attn_bwd_t(q_cat, k_cat, k_t, v_att, o_mla, do_mla, lse, bl, lp, nm)

    def mla_prep_bwd_fn(dqc, dkc, dvv, cos, s_up, s_dn, cq, ckv, gq, gk):
        dqc = dqc * scale
        dqs, dkvs = [], []
        for h in range(nm):
            dqs += [dqc[:, h * QK_PAD:h * QK_PAD + HEAD],
                    _rope_bwd(dqc[:, h * QK_PAD + HEAD:(h + 1) * QK_PAD], cos, s_up, s_dn)]
            dkvs += [dkc[:, h * QK_PAD:h * QK_PAD + HEAD], dvv[:, h * HEAD:(h + 1) * HEAD]]
        dqf, dkvf = jnp.concatenate(dqs, axis=1), jnp.concatenate(dkvs, axis=1)
        return dqf, dkvf, dqf, dkvf

    def mla_norms_bwd_fn(products, vals):
        dkc, cos, s_up, s_dn, cq, ckv, gq, gk = vals[1], vals[3], vals[4], vals[5], vals[6], vals[7], vals[8], vals[9]
        dkpe = 0.0
        for h in range(nm):
            dkpe = dkpe + dkc[:, h * QK_PAD + HEAD:(h + 1) * QK_PAD]
        dcq, dgq = _rms_bwd(cq, gq, products[0])
        dckv, dgk = _rms_bwd(ckv, gk, products[1])
        return [jnp.concatenate([dcq, dckv, _rope_bwd(dkpe, cos, s_up, s_dn)], axis=1)], [dgq, dgk]

    dmla, dq_full, dkv_full, g_q_norm, g_kv_norm = matmul_fused(
        "mla_prep_bwd_d_norms", mla_prep_bwd_fn,
        [(dq_cat, nm * QK_PAD, 0), (dk_cat, nm * QK_PAD, 0), (dv_att, nm * HEAD, 0)] + tables
        + [(proj_mla, ql, 0), (proj_mla, kvl, 1)], [q_a_norm_g, kv_a_norm_g],
        [w_qb, w_kvb], [(0, 0, nm * QK_PAD), (1, 0, nm * QK_PAD)], "nt",
        [(nm * QK_PAD, BF16), (nm * QK_PAD, BF16)], acc_outs=[(1, ql), (1, kvl)],
        epilogue=mla_norms_bwd_fn, epi_outs=[(mla_w, BF16)])
    g_wqb = matmul("gw_q_b", qn, dq_full, "tn")
    grads["w_q_b"] = g_wqb.reshape(ql, nm, QK_PAD)[:, :, :HEAD + ROPE].reshape(ql, nm * (HEAD + ROPE))
    grads["w_kv_b"] = matmul("gw_kv_b", kvn, dkv_full, "tn")

    d_pieces = [dhq, dhf, dhi, dhg, dgates, dmla]
    gw_parts = [matmul(f"gw_in_{k}", u1, dp, "tn") for k, dp in enumerate(d_pieces)]
    grads["w_in"] = jnp.concatenate(gw_parts[:4] + [gw_parts[5][:, :ql + kvl + ROPE], gw_parts[4]], axis=1)
    grads_late = start_grads("grads_late_start", late)
    w_mla_after = w_mla + grads_late[1][0, 0].astype(BF16)
    w_cat = jnp.concatenate([w_main, w_mla_after], axis=1)
    edges = [0, d, 2 * d, 3 * d, 4 * d, 6 * d, 6 * d + mla_w]

    def first_bwd_fn(products, vals):
        dh1v, h, is_meta, g = vals[6], vals[7], vals[8], vals[9]
        dx, dg = _rms_bwd(h, g, products[0])
        dh0v = dh1v + dx
        return [dh0v], [dg, dh0v * jnp.tile(is_meta, (1, lanes))]

    grad_x, g_mix_pre, meta_sum = matmul_fused(
        "d_u1_norm_mix_pre_bwd", lambda *v: v[:6],
        [(dp, dp.shape[1], 0) for dp in d_pieces] + [(dh1, d, 0), (h0, d, 0), (meta_rows, HEAD, 0, seq_tile)],
        [mix_pre_g], w_cat, list(zip(edges[:-1], edges[1:])), "nt", [], acc_outs=[(1, d), (SEQ_BLOCK, d)],
        epilogue=first_bwd_fn, epi_outs=[(d, F32, bl * seq, real_block)])
    grad_x = grad_x.reshape(bl, seq, d)

    g_parts = finish_grads("early", early, grads_early, g_mix_pre)
    updates = {}

    def update(n, parts):
        w2 = wts[n].reshape(-1, wts[n].shape[-1])
        updates[n] = adamw("adamw_" + n, w2, [p.reshape(w2.shape) for p in parts], mom_m[n].reshape(w2.shape),
                           mom_v[n].reshape(w2.shape))

    for n in early:
        update(n, g_parts[n])
    g_parts = finish_grads("late", late, grads_late, updates[early[-1]][0])
    for n in late:
        update(n, g_parts[n])
    p0 = lb_soft[0:1]
    g_lb_logits = jnp.concatenate([g_lb * p0 * (1.0 - p0), -g_lb * p0 * (1.0 - p0)], axis=0)

    def row_of(vec):
        return vec.reshape(-1, d) if vec.size >= d else jnp.pad(vec.reshape(1, -1), ((0, 0), (0, d - vec.size)))

    small_parts = dict(b_gate=g_b_gate, lb_logits=g_lb_logits, hg_norm_g=g_hg_norm, q_a_norm_g=g_q_norm,
                       kv_a_norm_g=g_kv_norm, mix_pre_g=g_mix_pre, mix_post_g=g_mix_post, ffn_pre_g=g_ffn_pre,
                       ffn_post_g=g_ffn_post)
    g_meta = meta_sum[PAD_FRONT:PAD_FRONT + N_META]
    small_rows = [row_of(small_parts[n]) for n in SMALL] + [row_of(g_meta)]
    n_small = sum(r.shape[0] for r in small_rows)
    small = jnp.pad(jnp.concatenate(small_rows, axis=0), ((0, -(-n_small // 8) * 8 - n_small), (0, 0)))
    all_small = gather_small(small)
    small_t = small.shape[0]

    def sum8_fn(*slabs):
        acc = slabs[0]
        for s in slabs[1:]:
            acc = acc + s
        return acc

    (g_small,) = rowwise("sum_small", sum8_fn, [(all_small.reshape(8 * small_t, d), d, 0, k) for k in range(8)],
                         [], [], [(d, F32)], tm=small_t, n_rows=small_t)

    off = 0
    for n, part in zip(SMALL, small_rows[:-1]):
        rows = part.shape[0]
        update(n, [g_small[off:off + rows, :d].reshape(-1)[:wts[n].size]])
        off += rows
    update("meta_tokens", [lax.dynamic_slice_in_dim(g_small[off:off + N_META, :d], my_chip * mcols, mcols, axis=1)])

    loss = lax.psum(loss_part[0, 0], ("x", "y", "c"))

    def shaped(n, a):
        return a.reshape((1,) + wts[n].shape) if n in BIG else a.reshape(wts[n].shape)

    return (loss, grad_x, *[shaped(n, updates[n][k]) for k in range(4) for n in WEIGHTS])
```

```python
import functools
import math

import jax
import jax.numpy as jnp
from jax import lax
from jax.experimental import pallas as pl
from jax.experimental.pallas import tpu as pltpu

F32 = jnp.float32
BF16 = jnp.bfloat16
MESH = pl.DeviceIdType.MESH

N_META = 16
NORM_EPS = 1e-6
HEAD = 128
ROPE = 64
ROPE_HALF = ROPE // 2
QK_PAD = 2 * HEAD
ROPE_THETA = 10000.0
SEQ_BLOCK = 256
PAD_FRONT = SEQ_BLOCK - N_META
NEG = -1e30
VMEM_LIMIT = 56 * 1024 * 1024
WGRAD_TILE_MAX = 1536
WGRAD_TILE_MIN = 1024
WGRAD_ROW_STEP = 1536
HG_TILE_MAX = 1152
ATTN_HEADS_PER_STEP = 1
ATTN_TILE_MAX = 768

ADAM_LR, ADAM_B1, ADAM_B2, ADAM_EPS, ADAM_WD, ADAM_STEP = 0.001, 0.9, 0.999, 1e-08, 0.01, 10

BIG = ("w_in", "w_hg_o", "w_q_b", "w_kv_b", "w_mla_o", "w_out", "w_ffn_in", "w_ffn_out")
COL_SHARDED = ("w_in", "w_q_b", "w_kv_b", "w_ffn_in")
SMALL = ("b_gate", "lb_logits", "hg_norm_g", "q_a_norm_g", "kv_a_norm_g", "mix_pre_g", "mix_post_g",
         "ffn_pre_g", "ffn_post_g")
WEIGHTS = ("meta_tokens", "w_in", "b_gate", "lb_logits", "hg_norm_g", "w_hg_o", "q_a_norm_g", "w_q_b",
           "kv_a_norm_g", "w_kv_b", "w_mla_o", "w_out", "mix_pre_g", "mix_post_g", "ffn_pre_g", "ffn_post_g",
           "w_ffn_in", "w_ffn_out")


def _tile(n, cap, unit=128):
    if n <= cap:
        return n
    best = None
    for t in range(unit, cap + 1, unit):
        if n % t == 0:
            best = t
    assert best is not None, (n, cap, unit)
    return best


def _sigmoid(x):
    return 1.0 / (1.0 + jnp.exp(-x))


def _bf(x):
    return x.astype(BF16)


def rowwise(name, fn, row_ins, seq_tabs, consts, row_outs, acc_outs=(), tm=SEQ_BLOCK, n_rows=None):
    t_rows = row_ins[0][0].shape[0] if n_rows is None else n_rows
    nt = t_rows // tm
    assert t_rows % tm == 0
    n_in = len(row_ins) + len(seq_tabs) + len(consts)
    n_row = len(row_outs)

    def body(*refs):
        vals = [r[...].astype(F32) for r in refs[:n_in]]
        res = fn(*vals)
        if not isinstance(res, (tuple, list)):
            res = (res,)
        outs = refs[n_in:]
        for k in range(n_row):
            outs[k][...] = res[k].astype(outs[k].dtype)
        if acc_outs:
            @pl.when(pl.program_id(0) == 0)
            def _():
                for k in range(len(acc_outs)):
                    outs[n_row + k][...] = jnp.zeros_like(outs[n_row + k])

            for k in range(len(acc_outs)):
                outs[n_row + k][...] += res[n_row + k]

    row_ins = [tuple(e) + (0,) * (4 - len(e)) for e in row_ins]
    in_specs = [pl.BlockSpec((tm, w), functools.partial(lambda i, j, ro: (ro(i) if callable(ro) else i + ro, j),
                                                        j=j, ro=ro)) for (_, w, j, ro) in row_ins]
    for tab in seq_tabs:
        per = tab.shape[0] // tm
        in_specs.append(pl.BlockSpec((tm, tab.shape[1]), functools.partial(lambda i, per: (i % per, 0), per=per)))
    for c in consts:
        in_specs.append(pl.BlockSpec(c.shape, lambda i: (0, 0)))
    row_outs = [tuple(e) + (t_rows, None)[len(e) - 2:] for e in row_outs]
    out_specs = [pl.BlockSpec((tm, w), functools.partial(lambda i, rm: (i if rm is None else rm(i), 0), rm=rm))
                 for (w, _, _, rm) in row_outs]
    out_specs += [pl.BlockSpec(s, lambda i: (0, 0)) for s in acc_outs]
    out_shape = [jax.ShapeDtypeStruct((rows, w), dt) for (w, dt, rows, _) in row_outs]
    out_shape += [jax.ShapeDtypeStruct(s, F32) for s in acc_outs]
    res = pl.pallas_call(
        body, name=name, grid=(nt,), in_specs=in_specs, out_specs=out_specs, out_shape=out_shape,
        compiler_params=pltpu.CompilerParams(dimension_semantics=("arbitrary",)),
    )(*[e[0] for e in row_ins], *seq_tabs, *consts)
    return res


def matmul(name, a, b, mode, out_dtype=F32):
    if mode != "tn":
        return _matmul_resident(name, a if isinstance(a, (list, tuple)) else [a],
                                b if isinstance(b, (list, tuple)) else [b], mode, out_dtype)
    kdim, m = a.shape
    n = b.shape[1]
    tn = _tile(n, WGRAD_TILE_MAX)
    tm = _tile(m, WGRAD_TILE_MAX if tn <= WGRAD_TILE_MIN else WGRAD_TILE_MIN)
    tk = _tile(kdim, WGRAD_ROW_STEP)
    nk = kdim // tk

    def body(a_ref, b_ref, o_ref, acc_ref):
        k = pl.program_id(2)

        @pl.when(k == 0)
        def _():
            acc_ref[...] = jnp.zeros_like(acc_ref)

        acc_ref[...] += lax.dot_general(a_ref[...], b_ref[...], TN_DIMS, preferred_element_type=F32)

        @pl.when(k == nk - 1)
        def _():
            o_ref[...] = acc_ref[...].astype(o_ref.dtype)

    return pl.pallas_call(
        body, name=name, grid=(m // tm, n // tn, nk),
        in_specs=[pl.BlockSpec((tk, tm), lambda i, j, k: (k, i)), pl.BlockSpec((tk, tn), lambda i, j, k: (k, j))],
        out_specs=pl.BlockSpec((tm, tn), lambda i, j, k: (i, j)),
        out_shape=jax.ShapeDtypeStruct((m, n), out_dtype),
        scratch_shapes=[pltpu.VMEM((tm, tn), F32)],
        compiler_params=pltpu.CompilerParams(dimension_semantics=("arbitrary", "arbitrary", "arbitrary"),
                                             vmem_limit_bytes=VMEM_LIMIT),
    )(a, b)


def _matmul_resident(name, a_list, b_list, mode, out_dtype):
    m = a_list[0].shape[0]
    n = b_list[0].shape[1] if mode == "nn" else b_list[0].shape[0]
    k_total = sum(a.shape[1] for a in a_list)
    out_bytes = 2 if out_dtype == BF16 else 4
    budget = VMEM_LIMIT - 4 * k_total * n - (6 << 20)
    tm = 1024
    while tm > 128 and 2 * tm * (2 * k_total + out_bytes * n) > budget:
        tm //= 2
    tm = _tile(m, tm)
    cn = _tile(n, WGRAD_TILE_MIN)
    npairs = len(a_list)

    def body(*refs):
        a_refs, b_refs, o_ref = refs[:npairs], refs[npairs:2 * npairs], refs[2 * npairs]
        for c in range(n // cn):
            acc = None
            for a_ref, b_ref in zip(a_refs, b_refs):
                if mode == "nn":
                    part = jnp.dot(a_ref[...], b_ref[:, pl.ds(c * cn, cn)], preferred_element_type=F32)
                else:
                    part = lax.dot_general(a_ref[...], b_ref[pl.ds(c * cn, cn), :], NT_DIMS,
                                           preferred_element_type=F32)
                acc = part if acc is None else acc + part
            o_ref[:, pl.ds(c * cn, cn)] = acc.astype(o_ref.dtype)

    in_specs = [pl.BlockSpec((tm, a.shape[1]), lambda i: (i, 0)) for a in a_list]
    in_specs += [pl.BlockSpec(b.shape, lambda i: (0, 0)) for b in b_list]
    return pl.pallas_call(
        body, name=name, grid=(m // tm,), in_specs=in_specs,
        out_specs=pl.BlockSpec((tm, n), lambda i: (i, 0)),
        out_shape=jax.ShapeDtypeStruct((m, n), out_dtype),
        compiler_params=pltpu.CompilerParams(dimension_semantics=("arbitrary",), vmem_limit_bytes=VMEM_LIMIT),
    )(*a_list, *b_list)


def matmul_fused(name, fn, row_ins, consts, weight, pieces, mode, extra_outs, out_dtype=BF16, tm=256, acc_outs=(),
                 epilogue=None, epi_outs=()):
    row_ins = [tuple(e) + (0,) * (4 - len(e)) for e in row_ins]
    t_rows = row_ins[0][0].shape[0]
    tm = _tile(t_rows, tm)
    several = isinstance(weight, (list, tuple))
    weights = list(weight) if several else [weight]
    n_in = len(row_ins) + len(consts)
    n_w = len(weights)
    n_parts = len(pieces)
    n_mm = len(epi_outs) if epilogue is not None else (n_parts if several else 1)
    n_row_out = n_mm + len(extra_outs)

    def width(w_arr):
        return w_arr.shape[1] if mode == "nn" else w_arr.shape[0]

    def body(*refs):
        w_hbms = refs[n_in:n_in + n_w]
        outs = refs[n_in + n_w:n_in + n_w + n_row_out + len(acc_outs)]
        w_refs, sems = refs[-n_w - 1:-1], refs[-1]

        @pl.when(pl.program_id(0) == 0)
        def _():
            cps = [pltpu.make_async_copy(w_hbms[k], w_refs[k], sems.at[k]) for k in range(n_w)]
            for cp in cps:
                cp.start()
            for cp in cps:
                cp.wait()
            for k in range(len(acc_outs)):
                outs[n_row_out + k][...] = jnp.zeros_like(outs[n_row_out + k])

        vals = [r[...].astype(F32) for r in refs[:n_in]]
        res = fn(*vals)
        products = []
        for a_p, piece in zip(res[:n_parts], pieces):
            w_ref, (k0, k1) = (w_refs[piece[0]], piece[1:]) if several else (w_refs[0], piece)
            if mode == "nn":
                products.append(jnp.dot(_bf(a_p), w_ref[pl.ds(k0, k1 - k0), :], preferred_element_type=F32))
            else:
                products.append(lax.dot_general(_bf(a_p), w_ref[:, pl.ds(k0, k1 - k0)], NT_DIMS,
                                                preferred_element_type=F32))
        if not several:
            products = [functools.reduce(lambda u, w: u + w, products)]
        sums = list(res[n_parts + len(extra_outs):])
        if epilogue is not None:
            products, more_sums = epilogue(products, vals)
            sums += list(more_sums)
        for p, val in enumerate(products):
            outs[p][...] = val.astype(outs[p].dtype)
        for o_ref, val in zip(outs[n_mm:n_row_out], res[n_parts:]):
            o_ref[...] = val.astype(o_ref.dtype)
        for k in range(len(acc_outs)):
            outs[n_row_out + k][...] += sums[k]

    in_specs = [pl.BlockSpec((tm, w), functools.partial(lambda i, j, ro: (ro(i) if callable(ro) else i + ro, j),
                                                        j=j, ro=ro)) for (_, w, j, ro) in row_ins]
    in_specs += [pl.BlockSpec(c.shape, lambda i: (0, 0)) for c in consts]
    in_specs += [pl.BlockSpec(memory_space=pl.ANY)] * n_w
    if epilogue is not None:
        widths = list(epi_outs) + list(extra_outs)
    elif several:
        widths = [(width(weights[piece[0]]), out_dtype) for piece in pieces] + list(extra_outs)
    else:
        widths = [(width(weights[0]), out_dtype)] + list(extra_outs)
    widths = [tuple(e) + (t_rows, None)[len(e) - 2:] for e in widths]
    return pl.pallas_call(
        body, name=name, grid=(t_rows // tm,), in_specs=in_specs,
        out_specs=[pl.BlockSpec((tm, w), functools.partial(lambda i, rm: (i if rm is None else rm(i), 0), rm=rm))
                   for (w, _, _, rm) in widths]
        + [pl.BlockSpec(s, lambda i: (0, 0)) for s in acc_outs],
        out_shape=[jax.ShapeDtypeStruct((rows, w), dt) for (w, dt, rows, _) in widths]
        + [jax.ShapeDtypeStruct(s, F32) for s in acc_outs],
        scratch_shapes=[pltpu.VMEM(w_arr.shape, w_arr.dtype) for w_arr in weights] + [pltpu.SemaphoreType.DMA((n_w,))],
        compiler_params=pltpu.CompilerParams(dimension_semantics=("arbitrary",), vmem_limit_bytes=VMEM_LIMIT),
    )(*[e[0] for e in row_ins], *consts, *weights)


def _rms(x, g):
    r = lax.rsqrt(jnp.mean(x * x, axis=-1, keepdims=True) + NORM_EPS)
    return x * r * g


def _rms_bwd(x, g, dy):
    r = lax.rsqrt(jnp.mean(x * x, axis=-1, keepdims=True) + NORM_EPS)
    xh = x * r
    dyg = dy * g
    dx = r * (dyg - xh * jnp.mean(dyg * xh, axis=-1, keepdims=True))
    return dx, jnp.sum(dy * xh, axis=0, keepdims=True)


def _silu(x):
    return x * _sigmoid(x)


def _silu_grad(x):
    s = _sigmoid(x)
    return s * (1.0 + x * (1.0 - s))


def _rope(xs, cos, s_up, s_dn):
    return xs * cos + pltpu.roll(xs, ROPE_HALF, 1) * s_up + pltpu.roll(xs, HEAD - ROPE_HALF, 1) * s_dn


def _rope_bwd(dy, cos, s_up, s_dn):
    return dy * cos + pltpu.roll(dy * s_up, HEAD - ROPE_HALF, 1) + pltpu.roll(dy * s_dn, ROPE_HALF, 1)


HG_SUB = 128
HG_LEVELS = 7
HG_E_ROWS = (HG_LEVELS + 1) * HG_SUB
HG_BWD_GROUP = 6
TN_DIMS = (((0,), (0,)), ((), ()))
NT_DIMS = (((1,), (1,)), ((), ()))


def _hg_constants():
    import numpy as np
    n = HG_SUB
    r = np.arange(n)[:, None]
    c = np.arange(n)[None, :]
    cs, ps = [], []
    for lvl in range(HG_LEVELS):
        m = (n // 2) >> lvl
        upper = (r % (2 * m)) >= m
        mid = (r // (2 * m)) * (2 * m) + m - 1
        cs.append(np.where(upper, (c > mid) & (c <= r), (c > r) & (c <= mid)))
        ps.append(((r // (2 * m)) == (c // (2 * m))) & upper & ((c % (2 * m)) < m))
    cs.append(c <= r)
    cs.append(np.ones((8, n), bool))
    cstack = np.concatenate(cs, 0).astype(np.float32)
    pstack = np.concatenate(ps, 0).astype(np.float32)
    pstack_t = np.concatenate([p.T for p in ps], 0).astype(np.float32)
    return (jnp.asarray(cstack, BF16), jnp.asarray(cstack[:HG_E_ROWS].T, BF16), jnp.asarray(pstack, F32),
            jnp.asarray(pstack_t, F32))


def _split_dot(c_bf, x):
    hi = _bf(x)
    lo = _bf(x - hi.astype(F32))
    r2 = jnp.dot(c_bf, jnp.concatenate([hi, lo], axis=1), preferred_element_type=F32)
    return r2[:, :HEAD] + r2[:, HEAD:]


def _hg_gates(hq, hf, lb):
    sq = _sigmoid(hq)
    sg = _sigmoid(hf)
    fg = lb + (1.0 - lb) * sg
    return sq, hq * sq, sg, fg, 1.0 - fg, jnp.log(fg)


def hgrn_fwd(proj_main, lb, consts, bl, lp, d):
    nh = d // HEAD
    rows_blk = _tile(lp, HG_TILE_MAX, HG_SUB)
    nb = lp // rows_blk
    spb = rows_blk // HG_SUB
    cstack, _, pstack, _ = consts

    def body(hq_ref, hf_ref, hi_ref, lb_ref, c_ref, p_ref, o_ref, st_ref, a_ref, s_ref):
        j = pl.program_id(2)

        @pl.when(j == 0)
        def _():
            s_ref[...] = jnp.zeros_like(s_ref)

        lbv = lb_ref[...]
        cs = c_ref[...]
        rows = [pl.ds(s * HG_SUB, HG_SUB) for s in range(spb)]
        gates = [_hg_gates(hq_ref[r, :].astype(F32), hf_ref[r, :].astype(F32), lbv) for r in rows]
        qs, ks = [g_[1] for g_ in gates], [g_[4] for g_ in gates]
        vs = [hi_ref[r, :].astype(F32) for r in rows]
        es = [_split_dot(cs, g_[5]) for g_ in gates]
        a_acc = [jnp.zeros((HG_SUB, HG_SUB), F32) for _ in rows]
        for lvl in range(HG_LEVELS):
            for s in range(spb):
                x = jnp.exp(es[s][lvl * HG_SUB:(lvl + 1) * HG_SUB])
                a_acc[s] = a_acc[s] + p_ref[pl.ds(lvl * HG_SUB, HG_SUB), :] * lax.dot_general(
                    _bf(qs[s] * x), _bf(ks[s] * x), NT_DIMS, preferred_element_type=F32)
        o_intra, qbs, kds, e_lasts = [], [], [], []
        for s in range(spb):
            a_bf = _bf(a_acc[s])
            a_ref[0, 0, s] = a_bf
            bc = es[s][HG_LEVELS * HG_SUB:HG_E_ROWS]
            b_last = jnp.tile(es[s][HG_E_ROWS:], (HG_SUB // 8, 1))
            o_intra.append(jnp.dot(a_bf, _bf(vs[s]), preferred_element_type=F32)
                           + jnp.sum(qs[s] * ks[s], axis=1, keepdims=True) * vs[s])
            qbs.append(_bf(qs[s] * jnp.exp(bc)))
            kds.append(_bf(ks[s] * jnp.exp(b_last - bc)))
            e_lasts.append(jnp.exp(b_last))
        st = s_ref[...]
        for s in range(spb):
            st_ref[0, 0, s] = st
            o_ref[rows[s], :] = (o_intra[s] + lax.dot_general(qbs[s], _bf(st), NT_DIMS, preferred_element_type=F32)
                                 ).astype(o_ref.dtype)
            st = st * e_lasts[s] + lax.dot_general(_bf(vs[s]), kds[s], TN_DIMS, preferred_element_type=F32)
        s_ref[...] = st

    def colspec(off):
        return pl.BlockSpec((rows_blk, HEAD), functools.partial(lambda h, b, j, off: (b * nb + j, off + h), off=off))

    whole = lambda arr: pl.BlockSpec(arr.shape, lambda h, b, j: (0, 0))
    return pl.pallas_call(
        body, name="hgrn_fwd", grid=(nh, bl, nb),
        in_specs=[colspec(0), colspec(nh), colspec(2 * nh), pl.BlockSpec((1, HEAD), lambda h, b, j: (0, h)),
                  whole(cstack), whole(pstack)],
        out_specs=[pl.BlockSpec((rows_blk, HEAD), lambda h, b, j: (b * nb + j, h)),
                   pl.BlockSpec((1, 1, spb, HEAD, HEAD), lambda h, b, j: (b, h, j, 0, 0)),
                   pl.BlockSpec((1, 1, spb, HG_SUB, HG_SUB), lambda h, b, j: (b, h, j, 0, 0))],
        out_shape=[jax.ShapeDtypeStruct((bl * lp, d), BF16),
                   jax.ShapeDtypeStruct((bl, nh, lp // HG_SUB, HEAD, HEAD), F32),
                   jax.ShapeDtypeStruct((bl, nh, lp // HG_SUB, HG_SUB, HG_SUB), BF16)],
        scratch_shapes=[pltpu.VMEM((HEAD, HEAD), F32)],
        compiler_params=pltpu.CompilerParams(dimension_semantics=("arbitrary", "arbitrary", "arbitrary")),
    )(proj_main, proj_main, proj_main, lb, cstack, pstack)


def hgrn_bwd(proj_main, lb, consts, states, a_mats, do_scan, bl, lp, d):
    nh = d // HEAD
    rows_blk = _tile(lp, HG_TILE_MAX, HG_SUB)
    nb = lp // rows_blk
    spb = rows_blk // HG_SUB
    cstack, cstack_t = consts[0], consts[1]
    pstack, pstack_t = _bf(consts[2]), _bf(consts[3])

    def body(hq_ref, hf_ref, hi_ref, lb_ref, c_ref, ct_ref, p_ref, pt_ref, st_ref, a_ref, do_ref,
             dq_ref, df_ref, di_ref, dlb_ref, ds_ref):
        b_id, j = pl.program_id(1), pl.program_id(2)
        blk = nb - 1 - j

        @pl.when(j == 0)
        def _():
            ds_ref[...] = jnp.zeros_like(ds_ref)

        @pl.when((j == 0) & (b_id == 0))
        def _():
            dlb_ref[...] = jnp.zeros_like(dlb_ref)

        lbv = lb_ref[...]
        cs = c_ref[...]
        cst = ct_ref[...]

        dlb = jnp.zeros((1, HEAD), F32)
        for first in reversed(range(0, spb, HG_BWD_GROUP)):
            dlb = dlb + _hg_group_bwd(list(range(first, min(first + HG_BWD_GROUP, spb))), lbv, cs, cst, hq_ref,
                                      hf_ref, hi_ref, st_ref, a_ref, do_ref, p_ref, pt_ref, dq_ref, df_ref, di_ref,
                                      ds_ref)
        dlb_ref[...] += dlb

    def _hg_group_bwd(ids, lbv, cs, cst, hq_ref, hf_ref, hi_ref, st_ref, a_ref, do_ref, p_ref, pt_ref, dq_ref,
                      df_ref, di_ref, ds_ref):
        rng = range(len(ids))
        rows = [pl.ds(s * HG_SUB, HG_SUB) for s in ids]
        hqs = [hq_ref[r, :].astype(F32) for r in rows]
        gates = [_hg_gates(hqs[s], hf_ref[rows[s], :].astype(F32), lbv) for s in rng]
        sqs, qs, sgs, fgs, ks = ([g_[i] for g_ in gates] for i in range(5))
        vs = [hi_ref[r, :].astype(F32) for r in rows]
        dos = [do_ref[r, :].astype(F32) for r in rows]
        sts = [st_ref[0, 0, s] for s in ids]
        es = [_split_dot(cs, g_[5]) for g_ in gates]
        bcs = [e[HG_LEVELS * HG_SUB:HG_E_ROWS] for e in es]
        b_lasts = [jnp.tile(e[HG_E_ROWS:], (HG_SUB // 8, 1)) for e in es]
        ebs = [jnp.exp(bc) for bc in bcs]
        qbs = [qs[s] * ebs[s] for s in rng]
        ers = [jnp.exp(b_lasts[s] - bcs[s]) for s in rng]
        kds = [ks[s] * ers[s] for s in rng]
        e_lasts = [jnp.exp(b) for b in b_lasts]
        do_bfs, v_bfs = [_bf(x) for x in dos], [_bf(x) for x in vs]
        das = [_bf(lax.dot_general(do_bfs[s], v_bfs[s], NT_DIMS, preferred_element_type=F32)) for s in rng]
        dats = [_bf(lax.dot_general(v_bfs[s], do_bfs[s], NT_DIMS, preferred_element_type=F32)) for s in rng]
        dqbs = [jnp.dot(do_bfs[s], _bf(sts[s]), preferred_element_type=F32) for s in rng]
        m_s = [lax.dot_general(do_bfs[s], _bf(qbs[s]), TN_DIMS, preferred_element_type=F32) for s in rng]
        dst_outs = [None] * len(ids)
        dst = ds_ref[...]
        for s in reversed(rng):
            dst_outs[s] = dst
            dst = dst * e_lasts[s] + m_s[s]
        ds_ref[...] = dst
        dst_bfs = [_bf(x) for x in dst_outs]
        d_diags = [jnp.sum(dos[s] * vs[s], axis=1, keepdims=True) for s in rng]
        dvs = [lax.dot_general(a_ref[0, 0, ids[s]], do_bfs[s], TN_DIMS, preferred_element_type=F32)
               + jnp.sum(qs[s] * ks[s], axis=1, keepdims=True) * dos[s]
               + lax.dot_general(_bf(kds[s]), dst_bfs[s], NT_DIMS, preferred_element_type=F32) for s in rng]
        dkds = [jnp.dot(v_bfs[s], dst_bfs[s], preferred_element_type=F32) for s in rng]
        dqs = [dqbs[s] * ebs[s] + d_diags[s] * ks[s] for s in rng]
        dks = [dkds[s] * ers[s] + d_diags[s] * qs[s] for s in rng]
        d_lasts = [jnp.sum(dst_outs[s] * sts[s] * e_lasts[s], axis=0, keepdims=True)
                   + jnp.sum(dkds[s] * kds[s], axis=0, keepdims=True) for s in rng]
        des = [[] for _ in rng]
        for lvl in range(HG_LEVELS):
            for s in rng:
                x = jnp.exp(es[s][lvl * HG_SUB:(lvl + 1) * HG_SUB])
                qh, kh = qs[s] * x, ks[s] * x
                dm = p_ref[pl.ds(lvl * HG_SUB, HG_SUB), :] * das[s]
                dmt = pt_ref[pl.ds(lvl * HG_SUB, HG_SUB), :] * dats[s]
                dqh = jnp.dot(dm, _bf(kh), preferred_element_type=F32)
                dkh = jnp.dot(dmt, _bf(qh), preferred_element_type=F32)
                dqs[s] = dqs[s] + dqh * x
                dks[s] = dks[s] + dkh * x
                des[s].append(dqh * qh + dkh * kh)
        dlb = jnp.zeros((1, HEAD), F32)
        for s in rng:
            des[s].append(dqbs[s] * qbs[s] - dkds[s] * kds[s])
            dg = _split_dot(cst, jnp.concatenate(des[s], axis=0)) + d_lasts[s]
            dfg = dg / fgs[s] - dks[s]
            dq_ref[rows[s], :] = (dqs[s] * (sqs[s] * (1.0 + hqs[s] * (1.0 - sqs[s])))).astype(dq_ref.dtype)
            df_ref[rows[s], :] = (dfg * (1.0 - lbv) * sgs[s] * (1.0 - sgs[s])).astype(df_ref.dtype)
            di_ref[rows[s], :] = dvs[s].astype(di_ref.dtype)
            dlb = dlb + jnp.sum(dfg * (1.0 - sgs[s]), axis=0, keepdims=True)
        return dlb

    def colspec(off):
        return pl.BlockSpec((rows_blk, HEAD),
                            functools.partial(lambda h, b, j, off: (b * nb + nb - 1 - j, off + h), off=off))

    whole = lambda arr: pl.BlockSpec(arr.shape, lambda h, b, j: (0, 0))
    mats = lambda: pl.BlockSpec((1, 1, spb, HEAD, HEAD), lambda h, b, j: (b, h, nb - 1 - j, 0, 0))
    t_rows = bl * lp
    return pl.pallas_call(
        body, name="hgrn_bwd", grid=(nh, bl, nb),
        in_specs=[colspec(0), colspec(nh), colspec(2 * nh), pl.BlockSpec((1, HEAD), lambda h, b, j: (0, h)),
                  whole(cstack), whole(cstack_t), whole(pstack), whole(pstack_t), mats(), mats(), colspec(0)],
        out_specs=[colspec(0), colspec(0), colspec(0), pl.BlockSpec((1, HEAD), lambda h, b, j: (0, h))],
        out_shape=[jax.ShapeDtypeStruct((t_rows, d), BF16)] * 3 + [jax.ShapeDtypeStruct((1, d), F32)],
        scratch_shapes=[pltpu.VMEM((HEAD, HEAD), F32)],
        compiler_params=pltpu.CompilerParams(dimension_semantics=("arbitrary", "arbitrary", "arbitrary")),
    )(proj_main, proj_main, proj_main, lb, cstack, cstack_t, pstack, pstack_t, states, a_mats, do_scan)


def _key_query_mask(key0, qry0, nk, nq_, causal):
    key = key0 + lax.broadcasted_iota(jnp.int32, (nk, 1), 0)
    if not causal:
        return key >= PAD_FRONT
    qry = qry0 + lax.broadcasted_iota(jnp.int32, (1, nq_), 1)
    return (key <= qry) & (key >= PAD_FRONT)


def _attn_tile(lp):
    return _tile(lp, ATTN_TILE_MAX, SEQ_BLOCK)


def attn_fwd_t(q_cat, k_cat, v_t, bl, lp, nm):
    tq = tk = _attn_tile(lp)
    nq = lp // tq
    hp = ATTN_HEADS_PER_STEP
    assert nm % hp == 0

    def body(q_ref, k_ref, vt_ref, o_ref, lse_ref, m_ref, l_ref, acc_ref):
        i = pl.program_id(2)
        m_ref[...] = jnp.full_like(m_ref, NEG)
        l_ref[...] = jnp.zeros_like(l_ref)
        acc_ref[...] = jnp.zeros_like(acc_ref)

        def step(c, mask):
            c0 = pl.multiple_of(c * tk, tk)
            for hh in range(hp):
                cols = pl.ds(hh * QK_PAD, QK_PAD)
                st = lax.dot_general(k_ref[pl.ds(c0, tk), cols], q_ref[:, cols], NT_DIMS,
                                     preferred_element_type=F32)
                if mask is not None:
                    st = jnp.where(_key_query_mask(c * tk, i * tq, tk, tq, mask == "causal"), st, NEG)
                m_old = m_ref[hh]
                m_new = jnp.maximum(m_old, jnp.max(st, axis=0, keepdims=True))
                alpha = jnp.exp(m_old - m_new)
                pt = jnp.exp(st - m_new)
                l_ref[hh] = alpha * l_ref[hh] + jnp.sum(pt, axis=0, keepdims=True)
                acc_ref[hh] = alpha * acc_ref[hh] + jnp.dot(vt_ref[0, hh, pl.ds(c, 1)][0], _bf(pt),
                                                            preferred_element_type=F32)
                m_ref[hh] = m_new

        def mid(c, carry):
            step(c, None)
            return carry

        @pl.when(i == 0)
        def _():
            step(0, "causal")

        @pl.when(i > 0)
        def _():
            step(0, "pad")
            lax.fori_loop(1, i, mid, 0)
            step(i, "causal")

        for hh in range(hp):
            o_ref[:, pl.ds(hh * HEAD, HEAD)] = jnp.transpose(acc_ref[hh] / l_ref[hh]).astype(o_ref.dtype)
            lse_ref[0, hh, 0] = m_ref[hh] + jnp.log(l_ref[hh])

    return pl.pallas_call(
        body, name="attn_fwd", grid=(bl, nm // hp, nq),
        in_specs=[pl.BlockSpec((tq, hp * QK_PAD), lambda b, h, i: (b * nq + i, h)),
                  pl.BlockSpec((lp, hp * QK_PAD), lambda b, h, i: (b, h)),
                  pl.BlockSpec((1, hp, nq, HEAD, tk), lambda b, h, i: (b, h, 0, 0, 0))],
        out_specs=[pl.BlockSpec((tq, hp * HEAD), lambda b, h, i: (b * nq + i, h)),
                   pl.BlockSpec((1, hp, 1, 1, tq), lambda b, h, i: (b, h, i, 0, 0))],
        out_shape=[jax.ShapeDtypeStruct((bl * lp, nm * HEAD), BF16),
                   jax.ShapeDtypeStruct((bl, nm, nq, 1, tq), F32)],
        scratch_shapes=[pltpu.VMEM((hp, 1, tq), F32), pltpu.VMEM((hp, 1, tq), F32), pltpu.VMEM((hp, HEAD, tq), F32)],
        compiler_params=pltpu.CompilerParams(dimension_semantics=("arbitrary", "arbitrary", "arbitrary")),
    )(q_cat, k_cat, v_t)


def attn_bwd_t(q_cat, k_cat, k_t, v, o, do, lse, bl, lp, nm):
    tq = tk = _attn_tile(lp)
    nq = lp // tq
    hp = ATTN_HEADS_PER_STEP
    assert nm % hp == 0

    def body(q_ref, k_ref, kt_ref, v_ref, o_ref, do_ref, lse_ref, dq_ref, dk_ref, dv_ref, dqt_ref, dka_ref, dva_ref):
        i = pl.program_id(2)

        @pl.when(i == 0)
        def _():
            dqt_ref[...] = jnp.zeros_like(dqt_ref)

        dka_ref[...] = jnp.zeros_like(dka_ref)
        dva_ref[...] = jnp.zeros_like(dva_ref)
        ones8 = jnp.ones((8, HEAD), BF16)

        def step(c, mask):
            c0 = pl.multiple_of(c * tq, tq)
            for hh in range(hp):
                qcols, vcols = pl.ds(hh * QK_PAD, QK_PAD), pl.ds(hh * HEAD, HEAD)
                qs = q_ref[pl.ds(c0, tq), qcols]
                dos = do_ref[pl.ds(c0, tq), vcols]
                prod = dos.astype(F32) * o_ref[pl.ds(c0, tq), vcols].astype(F32)
                hi = _bf(prod)
                lo = _bf(prod - hi.astype(F32))
                delta8 = (lax.dot_general(ones8, hi, NT_DIMS, preferred_element_type=F32)
                          + lax.dot_general(ones8, lo, NT_DIMS, preferred_element_type=F32))
                st = lax.dot_general(k_ref[:, qcols], qs, NT_DIMS, preferred_element_type=F32)
                pt = jnp.exp(st - lse_ref[0, hh, pl.ds(c, 1)][0])
                if mask is not None:
                    pt = jnp.where(_key_query_mask(i * tk, c * tq, tk, tq, mask == "causal"), pt, 0.0)
                dva_ref[hh] += jnp.dot(_bf(pt), dos, preferred_element_type=F32)
                dpt = lax.dot_general(v_ref[:, vcols], dos, NT_DIMS, preferred_element_type=F32)
                dst = _bf(pt * (dpt - jnp.tile(delta8, (tk // 8, 1))))
                dka_ref[hh] += jnp.dot(dst, qs, preferred_element_type=F32)
                dqt_ref[hh, pl.ds(c, 1)] += jnp.dot(kt_ref[0, hh, 0], dst, preferred_element_type=F32)[None]

        step(i, "causal")

        def rest_masked(c, carry):
            step(c, "pad")
            return carry

        def rest(c, carry):
            step(c, None)
            return carry

        @pl.when(i == 0)
        def _():
            lax.fori_loop(1, nq, rest_masked, 0)

        @pl.when(i > 0)
        def _():
            lax.fori_loop(i + 1, nq, rest, 0)

        for hh in range(hp):
            dk_ref[:, pl.ds(hh * QK_PAD, QK_PAD)] = dka_ref[hh].astype(dk_ref.dtype)
            dv_ref[:, pl.ds(hh * HEAD, HEAD)] = dva_ref[hh].astype(dv_ref.dtype)

        @pl.when(i == nq - 1)
        def _():
            for hh in range(hp):
                for c in range(nq):
                    dq_ref[pl.ds(c * tq, tq), pl.ds(hh * QK_PAD, QK_PAD)] = (
                        jnp.transpose(dqt_ref[hh, c])).astype(dq_ref.dtype)

    return pl.pallas_call(
        body, name="attn_bwd", grid=(bl, nm // hp, nq),
        in_specs=[pl.BlockSpec((lp, hp * QK_PAD), lambda b, h, i: (b, h)),
                  pl.BlockSpec((tk, hp * QK_PAD), lambda b, h, i: (b * nq + i, h)),
                  pl.BlockSpec((1, hp, 1, QK_PAD, tk), lambda b, h, i: (b, h, i, 0, 0)),
                  pl.BlockSpec((tk, hp * HEAD), lambda b, h, i: (b * nq + i, h)),
                  pl.BlockSpec((lp, hp * HEAD), lambda b, h, i: (b, h)),
                  pl.BlockSpec((lp, hp * HEAD), lambda b, h, i: (b, h)),
                  pl.BlockSpec((1, hp, nq, 1, tq), lambda b, h, i: (b, h, 0, 0, 0))],
        out_specs=[pl.BlockSpec((lp, hp * QK_PAD), lambda b, h, i: (b, h)),
                   pl.BlockSpec((tk, hp * QK_PAD), lambda b, h, i: (b * nq + i, h)),
                   pl.BlockSpec((tk, hp * HEAD), lambda b, h, i: (b * nq + i, h))],
        out_shape=[jax.ShapeDtypeStruct((bl * lp, nm * QK_PAD), BF16),
                   jax.ShapeDtypeStruct((bl * lp, nm * QK_PAD), BF16),
                   jax.ShapeDtypeStruct((bl * lp, nm * HEAD), BF16)],
        scratch_shapes=[pltpu.VMEM((hp, nq, QK_PAD, tq), F32), pltpu.VMEM((hp, tk, QK_PAD), F32),
                        pltpu.VMEM((hp, tk, HEAD), F32)],
        compiler_params=pltpu.CompilerParams(dimension_semantics=("arbitrary", "arbitrary", "arbitrary")),
    )(q_cat, k_cat, k_t, v, o, do, lse)


def _place():
    return lax.axis_index("x"), lax.axis_index("y"), lax.axis_index("c")


def gather_shards(packed):
    hbm = pl.BlockSpec(memory_space=pl.ANY)

    def body(src_ref, out_ref, send_sems, recv_sems, local_sem):
        x, y, c = _place()
        me = 2 * x + y
        chips = [(1 - x, y), (x, 1 - y), (1 - x, 1 - y)]
        local = pltpu.make_async_copy(src_ref, out_ref.at[me], local_sem)
        local.start()
        sends = []
        for k, (px, py) in enumerate(chips):
            cp = pltpu.make_async_remote_copy(src_ref=src_ref, dst_ref=out_ref.at[me], send_sem=send_sems.at[k],
                                              recv_sem=recv_sems.at[k], device_id=(px, py, c), device_id_type=MESH)
            cp.start()
            sends.append(cp)
        for k, (px, py) in enumerate(chips):
            pltpu.make_async_remote_copy(src_ref=src_ref, dst_ref=out_ref.at[2 * px + py], send_sem=send_sems.at[k],
                                         recv_sem=recv_sems.at[k], device_id=(px, py, c),
                                         device_id_type=MESH).wait_recv()
        for cp in sends:
            cp.wait_send()
        local.wait()

    return pl.pallas_call(
        body, name="gather_shards", in_specs=[hbm], out_specs=hbm,
        out_shape=jax.ShapeDtypeStruct((4,) + packed.shape, packed.dtype),
        scratch_shapes=[pltpu.SemaphoreType.DMA((3,)), pltpu.SemaphoreType.DMA((3,)), pltpu.SemaphoreType.DMA],
    )(packed)


def gather_small(small):
    hbm = pl.BlockSpec(memory_space=pl.ANY)

    def body(small_ref, all_ref, send_sems, recv_sems, local_sem):
        x, y, c = _place()
        me = 4 * x + 2 * y + c
        local = pltpu.make_async_copy(small_ref, all_ref.at[me], local_sem)
        local.start()
        others = [(x ^ ((r >> 2) & 1), y ^ ((r >> 1) & 1), c ^ (r & 1)) for r in range(1, 8)]
        sends = []
        for r, peer in enumerate(others):
            cp = pltpu.make_async_remote_copy(src_ref=small_ref, dst_ref=all_ref.at[me], send_sem=send_sems.at[r],
                                              recv_sem=recv_sems.at[r], device_id=peer, device_id_type=MESH)
            cp.start()
            sends.append(cp)
        for r, (px, py, pc) in enumerate(others):
            pltpu.make_async_remote_copy(src_ref=small_ref, dst_ref=all_ref.at[4 * px + 2 * py + pc],
                                         send_sem=send_sems.at[r], recv_sem=recv_sems.at[r],
                                         device_id=(px, py, pc), device_id_type=MESH).wait_recv()
        for cp in sends:
            cp.wait_send()
        local.wait()

    return pl.pallas_call(
        body, name="gather_small", in_specs=[hbm], out_specs=hbm,
        out_shape=jax.ShapeDtypeStruct((8,) + small.shape, small.dtype),
        scratch_shapes=[pltpu.SemaphoreType.DMA((7,)), pltpu.SemaphoreType.DMA((7,)), pltpu.SemaphoreType.DMA],
    )(small)


def swap_with_sibling(name, parts):
    n = len(parts)
    hbm = pl.BlockSpec(memory_space=pl.ANY)

    def body(*refs):
        x, y, c = _place()
        cps = [pltpu.make_async_remote_copy(src_ref=refs[j], dst_ref=refs[n + j], send_sem=refs[2 * n].at[j],
                                            recv_sem=refs[2 * n + 1].at[j], device_id=(x, y, 1 - c),
                                            device_id_type=MESH) for j in range(n)]
        for cp in cps:
            cp.start()
        for cp in cps:
            cp.wait()

    return pl.pallas_call(
        body, name=name, in_specs=[hbm] * n, out_specs=[hbm] * n,
        out_shape=[jax.ShapeDtypeStruct(p.shape, p.dtype) for p in parts],
        scratch_shapes=[pltpu.SemaphoreType.DMA((n,)), pltpu.SemaphoreType.DMA((n,))],
    )(*parts)


def _chips3():
    x, y, c = _place()
    return [(1 - x, y, c), (x, 1 - y, c), (1 - x, 1 - y, c)]


def _push_copies(src_refs, land_refs, send_sems, recv_sems, per_chip):
    x, y, _ = _place()
    cps = []
    for j, (src_ref, land_ref) in enumerate(zip(src_refs, land_refs)):
        for k, (px, py, pc) in enumerate(_chips3()):
            part = src_ref.at[2 * px + py] if per_chip else src_ref
            slot = k if per_chip else 2 * x + y
            cps.append(pltpu.make_async_remote_copy(
                src_ref=part, dst_ref=land_ref.at[slot], send_sem=send_sems.at[3 * j + k],
                recv_sem=recv_sems.at[3 * j + k], device_id=(px, py, pc), device_id_type=MESH))
    return cps


def push_start(name, srcs, per_chip):
    n = len(srcs)
    hbm = pl.BlockSpec(memory_space=pltpu.HBM)
    sem = pl.BlockSpec(memory_space=pltpu.SEMAPHORE)
    lands = [lax.empty((3 if per_chip else 4,) + s.shape[-2:], s.dtype) for s in srcs]

    def body(*refs):
        src_refs, land_refs = refs[:n], refs[n:2 * n]
        send_sems, recv_sems = refs[2 * n], refs[2 * n + 1]
        for cp in _push_copies(src_refs, land_refs, send_sems, recv_sems, per_chip):
            cp.start()
        refs[-1][...] = jnp.zeros_like(refs[-1])

    outs = pl.pallas_call(
        body, name=name,
        out_shape=(pltpu.SemaphoreType.DMA((3 * n,)), pltpu.SemaphoreType.DMA((3 * n,)),
                   *[pltpu.HBM(a.shape, a.dtype) for a in list(srcs) + lands], jax.ShapeDtypeStruct((8, HEAD), F32)),
        in_specs=(hbm,) * (2 * n),
        out_specs=(sem, sem) + (hbm,) * (2 * n) + (pl.BlockSpec(memory_space=pltpu.VMEM),),
        input_output_aliases={j: 2 + j for j in range(2 * n)},
        compiler_params=pltpu.CompilerParams(has_side_effects=pltpu.SideEffectType.DATAFLOW_SIDE_EFFECTING),
    )(*[pltpu.with_memory_space_constraint(a, pltpu.HBM) for a in list(srcs) + lands])
    return tuple(outs[:-1]), outs[-1]


def push_wait(name, handle, after, per_chip):
    send_sems, recv_sems = handle[0], handle[1]
    thru = handle[2:]
    n = len(thru) // 2
    hbm = pl.BlockSpec(memory_space=pltpu.HBM)
    sem = pl.BlockSpec(memory_space=pltpu.SEMAPHORE)

    def body(*refs):
        src_refs, land_refs = refs[:n], refs[n:2 * n]
        for cp in _push_copies(src_refs, land_refs, refs[2 * n], refs[2 * n + 1], per_chip):
            cp.wait_send()
            cp.wait_recv()

    outs = pl.pallas_call(
        body, name=name,
        out_shape=tuple(pltpu.HBM(a.shape, a.dtype) for a in thru),
        in_specs=(hbm,) * (2 * n) + (sem, sem, pl.BlockSpec(memory_space=pl.ANY)), out_specs=(hbm,) * (2 * n),
        input_output_aliases={j: j for j in range(2 * n)},
        compiler_params=pltpu.CompilerParams(has_side_effects=pltpu.SideEffectType.DATAFLOW_SIDE_EFFECTING),
    )(*thru, send_sems, recv_sems, after)
    return outs[:n], outs[n:]


def join_gathered(name, own, landed, my_chip):
    blocks = lax.dynamic_update_index_in_dim(landed, own, my_chip, 0)
    _, r, c = blocks.shape
    if name in COL_SHARDED:
        return blocks.transpose(1, 0, 2).reshape(r, 4 * c)
    return blocks.reshape(4 * r, c)


def adamw(name, w, g_parts, m, v):
    r, c = w.shape
    tr = r if r * c <= 65536 else _tile(r, 128, 8)
    ng = len(g_parts)

    def body(*refs):
        w_ref, m_ref, v_ref = refs[0], refs[1 + ng], refs[2 + ng]
        g_ref, d_ref, nm_ref, nv_ref = refs[3 + ng:]
        gv = refs[1][...]
        for k in range(1, ng):
            gv = gv + refs[1 + k][...]
        m_new = ADAM_B1 * m_ref[...] + (1.0 - ADAM_B1) * gv
        v_new = ADAM_B2 * v_ref[...] + (1.0 - ADAM_B2) * (gv * gv)
        m_hat = m_new / (1.0 - ADAM_B1 ** ADAM_STEP)
        v_hat = v_new / (1.0 - ADAM_B2 ** ADAM_STEP)
        g_ref[...] = gv
        d_ref[...] = -ADAM_LR * (m_hat / (jnp.sqrt(v_hat) + ADAM_EPS) + ADAM_WD * w_ref[...])
        nm_ref[...] = m_new
        nv_ref[...] = v_new

    spec = pl.BlockSpec((tr, c), lambda i: (i, 0))
    return pl.pallas_call(
        body, name=name, grid=(r // tr,), in_specs=[spec] * (3 + ng), out_specs=[spec] * 4,
        out_shape=[jax.ShapeDtypeStruct((r, c), F32)] * 4,
        compiler_params=pltpu.CompilerParams(dimension_semantics=("arbitrary",)),
    )(w, *g_parts, m, v)


def split_full(name, full, s):
    if name in COL_SHARDED:
        c = full.shape[1] // 4
        return full[:, s * c:(s + 1) * c]
    r = full.shape[0] // 4
    return full[s * r:(s + 1) * r]


def kernel(x, meta_tokens, w_in, b_gate, lb_logits, hg_norm_g, w_hg_o, q_a_norm_g, w_q_b, kv_a_norm_g, w_kv_b, w_mla_o, w_out, mix_pre_g, mix_post_g, ffn_pre_g, ffn_post_g, w_ffn_in, w_ffn_out, loss_target, m_meta_tokens, m_w_in, m_b_gate, m_lb_logits, m_hg_norm_g, m_w_hg_o, m_q_a_norm_g, m_w_q_b, m_kv_a_norm_g, m_w_kv_b, m_w_mla_o, m_w_out, m_mix_pre_g, m_mix_post_g, m_ffn_pre_g, m_ffn_post_g, m_w_ffn_in, m_w_ffn_out, v_meta_tokens, v_w_in, v_b_gate, v_lb_logits, v_hg_norm_g, v_w_hg_o, v_q_a_norm_g, v_w_q_b, v_kv_a_norm_g, v_w_kv_b, v_w_mla_o, v_w_out, v_mix_pre_g, v_mix_post_g, v_ffn_pre_g, v_ffn_post_g, v_w_ffn_in, v_w_ffn_out):
    wts = dict(meta_tokens=meta_tokens, w_in=w_in[0], b_gate=b_gate, lb_logits=lb_logits, hg_norm_g=hg_norm_g,
               w_hg_o=w_hg_o[0], q_a_norm_g=q_a_norm_g, w_q_b=w_q_b[0], kv_a_norm_g=kv_a_norm_g, w_kv_b=w_kv_b[0],
               w_mla_o=w_mla_o[0], w_out=w_out[0], mix_pre_g=mix_pre_g, mix_post_g=mix_post_g, ffn_pre_g=ffn_pre_g,
               ffn_post_g=ffn_post_g, w_ffn_in=w_ffn_in[0], w_ffn_out=w_ffn_out[0])
    mom_m = dict(meta_tokens=m_meta_tokens, w_in=m_w_in[0], b_gate=m_b_gate, lb_logits=m_lb_logits,
                 hg_norm_g=m_hg_norm_g, w_hg_o=m_w_hg_o[0], q_a_norm_g=m_q_a_norm_g, w_q_b=m_w_q_b[0],
                 kv_a_norm_g=m_kv_a_norm_g, w_kv_b=m_w_kv_b[0], w_mla_o=m_w_mla_o[0], w_out=m_w_out[0],
                 mix_pre_g=m_mix_pre_g, mix_post_g=m_mix_post_g, ffn_pre_g=m_ffn_pre_g, ffn_post_g=m_ffn_post_g,
                 w_ffn_in=m_w_ffn_in[0], w_ffn_out=m_w_ffn_out[0])
    mom_v = dict(meta_tokens=v_meta_tokens, w_in=v_w_in[0], b_gate=v_b_gate, lb_logits=v_lb_logits,
                 hg_norm_g=v_hg_norm_g, w_hg_o=v_w_hg_o[0], q_a_norm_g=v_q_a_norm_g, w_q_b=v_w_q_b[0],
                 kv_a_norm_g=v_kv_a_norm_g, w_kv_b=v_w_kv_b[0], w_mla_o=v_w_mla_o[0], w_out=v_w_out[0],
                 mix_pre_g=v_mix_pre_g, mix_post_g=v_mix_post_g, ffn_pre_g=v_ffn_pre_g, ffn_post_g=v_ffn_post_g,
                 w_ffn_in=v_w_ffn_in[0], w_ffn_out=v_w_ffn_out[0])

    bl, seq, d = x.shape
    lp = PAD_FRONT + N_META + seq
    t_rows = bl * lp
    nh = d // HEAD
    ql, kvl = wts["w_q_b"].shape[0], wts["w_kv_b"].shape[0]
    nm = (4 * wts["w_mla_o"].shape[0]) // HEAD
    ffn = 4 * wts["w_ffn_out"].shape[0]
    mla_w = ql + kvl + HEAD
    assert ql == kvl and ql % HEAD == 0 and seq % SEQ_BLOCK == 0 and d % HEAD == 0
    scale = (HEAD + ROPE) ** -0.5
    my_chip = 2 * lax.axis_index("x") + lax.axis_index("y")

    mcols = meta_tokens.shape[1]
    meta_all = gather_shards(meta_tokens)
    meta_full = jnp.concatenate([meta_all[s] for s in range(4)], axis=1)

    def start_gather(name, names, order_after):
        srcs = [_bf(wts[n]) for n in names]
        if order_after is not None:
            srcs[0] = srcs[0] + order_after[0, 0].astype(BF16)
        return push_start(name, srcs, per_chip=False)

    def finish_gather(name, names, started, after):
        owns, landed = push_wait(name, started[0], after, per_chip=False)
        return {n: join_gathered(n, own, land, my_chip) for n, own, land in zip(names, owns, landed)}

    rest_names = tuple(n for n in BIG if n != "w_in")
    my_c = lax.axis_index("c")
    w_in_bf = _bf(wts["w_in"])
    half = w_in_bf.shape[0] // 2
    own_half = (lax.dynamic_slice_in_dim(w_in_bf, my_c * half, half, axis=0)
                + (meta_all[0, :1, :1] * 0.0)[0, 0].astype(BF16))
    gather_1 = push_start("gather_w_in_start", [own_half], per_chip=False)
    gather_2 = start_gather("gather_rest_start", rest_names, gather_1[1])

    tiles_seq, tiles_real = lp // SEQ_BLOCK, seq // SEQ_BLOCK
    assert PAD_FRONT + N_META == SEQ_BLOCK

    def real_block(i):
        return (i // tiles_seq) * tiles_real + jnp.maximum(i % tiles_seq - 1, 0)

    meta_rows = jnp.broadcast_to(((jnp.arange(lp) >= PAD_FRONT) & (jnp.arange(lp) < PAD_FRONT + N_META)
                                  ).astype(F32)[:, None], (lp, HEAD))
    pos = (jnp.arange(lp, dtype=jnp.int32) - PAD_FRONT).astype(F32)
    inv_freq = 1.0 / (ROPE_THETA ** (jnp.arange(0, ROPE, 2, dtype=F32) / ROPE))
    ang = pos[:, None] * inv_freq[None, :]
    zeros32 = jnp.zeros((lp, ROPE_HALF), F32)
    zeros64 = jnp.zeros((lp, HEAD - ROPE), F32)
    t_cos = jnp.concatenate([jnp.cos(ang), jnp.cos(ang), zeros64], axis=1)
    t_up = jnp.concatenate([zeros32, jnp.sin(ang), zeros64], axis=1)
    t_dn = jnp.concatenate([-jnp.sin(ang), zeros32, zeros64], axis=1)
    real = jnp.broadcast_to((jnp.arange(lp) >= PAD_FRONT + N_META).astype(F32)[:, None], (lp, HEAD))
    lanes = d // HEAD
    lb_soft = jax.nn.softmax(lb_logits.astype(F32), axis=0)
    lb = lb_soft[0:1]

    meta_tile = jnp.concatenate([jnp.zeros((PAD_FRONT, d), F32), meta_full], axis=0)

    def first_fn(xv, is_real, is_meta, mtile, g):
        h = xv * jnp.tile(is_real, (1, lanes)) + mtile * jnp.tile(is_meta, (1, lanes))
        return _rms(h, g), h

    u1, h0 = rowwise("norm_mix_pre", first_fn, [(x.reshape(bl * seq, d), d, 0, real_block)], [real, meta_rows],
                     [meta_tile, mix_pre_g + gather_2[1][0, 0]], [(d, BF16), (d, F32)], n_rows=t_rows)
    _, (fetched,) = push_wait("gather_w_in_wait", gather_1[0], u1, per_chip=False)
    (handed,) = swap_with_sibling("swap_w_in", [fetched])
    halves = jnp.stack([fetched, handed])
    remote = jnp.concatenate([lax.dynamic_index_in_dim(halves, my_c, 0, keepdims=False),
                              lax.dynamic_index_in_dim(halves, 1 - my_c, 0, keepdims=False)], axis=1)
    full = {"w_in": join_gathered("w_in", w_in_bf, remote, my_chip)}
    w_main = jnp.concatenate([full["w_in"][:, :4 * d], full["w_in"][:, -2 * d:]], axis=1)
    w_mla = jnp.pad(full["w_in"][:, 4 * d:4 * d + ql + kvl + ROPE], ((0, 0), (0, HEAD - ROPE)))
    proj_main = matmul("proj_main", u1, w_main, "nn", out_dtype=BF16)
    proj_mla = matmul("proj_mla", u1, w_mla, "nn", out_dtype=BF16)
    hg_consts = _hg_constants()
    o_scan, states, a_mats = hgrn_fwd(proj_main, lb, hg_consts, bl, lp, d)

    full.update(finish_gather("gather_rest_wait", rest_names, gather_2, o_scan))
    w_qb = jnp.pad(full["w_q_b"].reshape(ql, nm, HEAD + ROPE), ((0, 0), (0, 0), (0, QK_PAD - HEAD - ROPE))
                   ).reshape(ql, nm * QK_PAD)
    w_kvb = full["w_kv_b"]

    def hg_out_fn(o, hg, g):
        ov = jnp.concatenate([_rms(o[:, h * HEAD:(h + 1) * HEAD], g) for h in range(nh)], axis=1) * _silu(hg)
        return ov, ov

    y_a, o_hg = matmul_fused("hgrn_out_y_a", hg_out_fn, [(o_scan, d, 0), (proj_main, d, 3)], [hg_norm_g],
                             _bf(full["w_hg_o"]), [(0, d)], "nn", [(d, BF16)], tm=512)

    def seq_tile(i):
        return i % tiles_seq

    tables = [(t_cos, HEAD, 0, seq_tile), (t_up, HEAD, 0, seq_tile), (t_dn, HEAD, 0, seq_tile)]

    def q_norm_fn(cq, cos, s_up, s_dn, g):
        cn = _rms(cq, g)
        return cn, cn

    def q_rope_fn(products, vals):
        qf = products[0] * scale
        cos, s_up, s_dn = vals[1:4]
        qs = []
        for h in range(nm):
            qs += [qf[:, h * QK_PAD:h * QK_PAD + HEAD], _rope(qf[:, h * QK_PAD + HEAD:(h + 1) * QK_PAD], cos, s_up, s_dn)]
        return [jnp.concatenate(qs, axis=1)], []

    q_cat, qn = matmul_fused("q_norm_up_rope", q_norm_fn, [(proj_mla, ql, 0)] + tables, [q_a_norm_g], w_qb,
                             [(0, ql)], "nn", [(ql, BF16)], epilogue=q_rope_fn, epi_outs=[(nm * QK_PAD, BF16)])

    def kv_norm_fn(ckv, kpe, cos, s_up, s_dn, g):
        cn = _rms(ckv, g)
        return cn, cn

    def kv_rope_fn(products, vals):
        kvf = products[0]
        kpe_r = _rope(vals[1], *vals[2:5])
        ks, vs = [], []
        for h in range(nm):
            ks += [kvf[:, h * QK_PAD:h * QK_PAD + HEAD], kpe_r]
            vs += [kvf[:, h * QK_PAD + HEAD:(h + 1) * QK_PAD]]
        return [jnp.concatenate(ks, axis=1), jnp.concatenate(vs, axis=1)], []

    kpe_blk = (ql + kvl) // HEAD
    k_cat, v_att, kvn = matmul_fused("kv_norm_up_rope", kv_norm_fn,
                                     [(proj_mla, kvl, 1), (proj_mla, HEAD, kpe_blk)] + tables, [kv_a_norm_g], w_kvb,
                                     [(0, kvl)], "nn", [(kvl, BF16)], epilogue=kv_rope_fn,
                                     epi_outs=[(nm * QK_PAD, BF16), (nm * HEAD, BF16)])
    at = _attn_tile(lp)
    v_t = v_att.reshape(bl, lp // at, at, nm, HEAD).transpose(0, 3, 1, 4, 2)
    k_t = k_cat.reshape(bl, lp // at, at, nm, QK_PAD).transpose(0, 3, 1, 4, 2)
    o_mla, lse = attn_fwd_t(q_cat, k_cat, v_t, bl, lp, nm)
    y_b = matmul("y_b", o_mla, _bf(full["w_mla_o"]), "nn", out_dtype=BF16)

    def gate_fn(ya, yb, ga, gb, bias):
        zv = _sigmoid(ga + bias[:, :d]) * ya + _sigmoid(gb + bias[:, d:]) * yb
        return zv, zv

    mixed, z = matmul_fused("gate_mix_out", gate_fn,
                            [(y_a, d, 0), (y_b, d, 0), (proj_main, d, 4), (proj_main, d, 5)], [b_gate],
                            _bf(full["w_out"]), [(0, d)], "nn", [(d, BF16)], tm=512)

    def mid_fn(h, mx, g_post, g_pre):
        h1v = h + _rms(mx, g_post)
        u2v = _rms(h1v, g_pre)
        return u2v, h1v, u2v

    gu, h1, u2 = matmul_fused("norm_mid_ffn_in", mid_fn, [(h0, d, 0), (mixed, d, 0)], [mix_post_g, ffn_pre_g],
                              _bf(full["w_ffn_in"]), [(0, d)], "nn", [(d, F32), (d, BF16)])
    def swiglu_fn(gt, up):
        a = _silu(gt) * up
        return a, a

    f_out, act = matmul_fused("swiglu_ffn_out", swiglu_fn, [(gu, ffn, 0), (gu, ffn, 1)], [],
                              _bf(full["w_ffn_out"]), [(0, ffn)], "nn", [(ffn, BF16)])

    def loss_fn(h1v, fv, tg, realv, g_post):
        h2 = h1v + _rms(fv, g_post)
        diff = (h2 - tg) * jnp.tile(realv, (1, lanes))
        part = jnp.broadcast_to(0.5 * jnp.sum(diff * diff, keepdims=True) / d, (1, HEAD))
        dy = diff / d
        df, dg = _rms_bwd(fv, g_post, dy)
        return df, dy, df, part, dg

    d_act, dy, df, loss_part, g_ffn_post = matmul_fused(
        "loss_head_d_act", loss_fn,
        [(h1, d, 0), (f_out, d, 0), (loss_target.reshape(bl * seq, d), d, 0, real_block), (real, HEAD, 0, seq_tile)],
        [ffn_post_g], _bf(full["w_ffn_out"]), [(0, d)], "nt", [(d, BF16), (d, BF16)],
        acc_outs=[(1, HEAD), (1, d)])
    grads = {}
    grads["w_ffn_out"] = matmul("gw_ffn_out", act, df, "tn")

    def swiglu_bwd_fn(gt, up, da):
        dgt, dup = da * up * _silu_grad(gt), da * _silu(gt)
        return dgt, dup, jnp.concatenate([dgt, dup], axis=1)

    du2, dgu = matmul_fused("swiglu_bwd_d_u2", swiglu_bwd_fn, [(gu, ffn, 0), (gu, ffn, 1), (d_act, ffn, 0)], [],
                            _bf(full["w_ffn_in"]), [(0, ffn), (ffn, 2 * ffn)], "nt", [(2 * ffn, BF16)])
    grads["w_ffn_in"] = matmul("gw_ffn_in", u2, dgu, "tn")

    def mid_bwd_fn(dyv, h1v, du2v, mx, g_pre, g_post):
        dx, dg_pre = _rms_bwd(h1v, g_pre, du2v)
        dh1 = dyv + dx
        dmx, dg_post = _rms_bwd(mx, g_post, dh1)
        return dmx, dh1, dmx, dg_pre, dg_post

    dz, dh1, dmixed, g_ffn_pre, g_mix_post = matmul_fused(
        "norm_mid_bwd_d_z", mid_bwd_fn, [(dy, d, 0), (h1, d, 0), (du2, d, 0), (mixed, d, 0)],
        [ffn_pre_g, mix_post_g], _bf(full["w_out"]), [(0, d)], "nt", [(d, BF16), (d, BF16)], tm=512,
        acc_outs=[(1, d), (1, d)])
    grads["w_out"] = matmul("gw_out", z, dmixed, "tn")

    def gate_bwd_fn(dzv, ya, yb, ga, gb, bias):
        sa, sb = _sigmoid(ga + bias[:, :d]), _sigmoid(gb + bias[:, d:])
        dga = dzv * ya * sa * (1.0 - sa)
        dgb = dzv * yb * sb * (1.0 - sb)
        dgates = jnp.concatenate([dga, dgb], axis=1)
        dya, dyb = dzv * sa, dzv * sb
        return dya, dyb, dya, dyb, dgates, jnp.sum(dgates, axis=0, keepdims=True)

    def hg_out_bwd_fn(products, vals):
        do, o, hg, g = products[0], vals[5], vals[6], vals[8]
        dn = do * _silu(hg)
        dos, dgs, ons = [], 0.0, []
        for h in range(nh):
            sl = slice(h * HEAD, (h + 1) * HEAD)
            dx, dg = _rms_bwd(o[:, sl], g, dn[:, sl])
            dos.append(dx)
            dgs = dgs + dg
            ons.append(_rms(o[:, sl], g))
        dhg_v = do * jnp.concatenate(ons, axis=1) * _silu_grad(hg)
        return [jnp.concatenate(dos, axis=1), products[1], dhg_v], [dgs]

    do_scan, do_mla, dhg, dy_a, dy_b, dgates, g_b_gate, g_hg_norm = matmul_fused(
        "gate_mix_bwd_d_o", lambda dzv, ya, yb, ga, gb, o, hg, bias, g: gate_bwd_fn(dzv, ya, yb, ga, gb, bias),
        [(dz, d, 0), (y_a, d, 0), (y_b, d, 0), (proj_main, d, 4), (proj_main, d, 5), (o_scan, d, 0),
         (proj_main, d, 3)], [b_gate, hg_norm_g],
        [_bf(full["w_hg_o"]), _bf(full["w_mla_o"])], [(0, 0, d), (1, 0, d)], "nt",
        [(d, BF16), (d, BF16), (2 * d, BF16)], acc_outs=[(1, 2 * d), (1, HEAD)],
        epilogue=hg_out_bwd_fn, epi_outs=[(d, BF16), (d, BF16), (d, BF16)])
    grads["w_hg_o"] = matmul("gw_hg_o", o_hg, dy_a, "tn")
    grads["w_mla_o"] = matmul("gw_mla_o", o_mla, dy_b, "tn")

    early = ("w_hg_o", "w_mla_o", "w_out", "w_ffn_in", "w_ffn_out")
    late = ("w_in", "w_q_b", "w_kv_b")

    def start_grads(name, names):
        sends = [_bf(jnp.stack([split_full(n, grads[n], s) for s in range(4)])) for n in names]
        mines = []
        for n in names:
            r, c = wts[n].shape
            axis, size = (1, c) if n in COL_SHARDED else (0, r)
            mines.append(lax.dynamic_slice_in_dim(grads[n], my_chip * size, size, axis=axis))
        handle, token = push_start(name, sends, per_chip=True)
        return handle, token, mines

    def finish_grads(tag, names, started, after):
        handle, _, mines = started
        _, landed = push_wait(f"grads_{tag}_wait", handle, after, per_chip=True)
        parts = []
        for n, mine, land in zip(names, mines, landed):
            r, c = mine.shape
            tr = _tile(r, 256, 16)
            land2 = land.reshape(3 * r, c)
            parts.append(rowwise(f"sum_chips_{n}", lambda a, r0, r1, r2: a + r0 + r1 + r2,
                                 [(mine, c, 0)] + [(land2, c, 0, k * (r // tr)) for k in range(3)],
                                 [], [], [(c, F32)], tm=tr)[0])
        sibs = swap_with_sibling(f"swap_{tag}", parts)
        return {n: [p, s] for n, p, s in zip(names, parts, sibs)}

    grads_early = start_grads("grads_early_start", early)
    token_a = grads_early[1]

    dhq, dhf, dhi, g_lb = hgrn_bwd(proj_main, lb + token_a[0, 0], hg_consts, states, a_mats, do_scan, bl, lp, d)

    dq_cat, dk_cat, dv_att = attn_bwd_t(q_cat, k_cat, k_t, v_att, o_mla, do_mla, lse, bl, lp, nm)

    def mla_prep_bwd_fn(dqc, dkc, dvv, cos, s_up, s_dn, cq, ckv, gq, gk):
        dqc = dqc * scale
        dqs, dkvs = [], []
        for h in range(nm):
            dqs += [dqc[:, h * QK_PAD:h * QK_PAD + HEAD],
                    _rope_bwd(dqc[:, h * QK_PAD + HEAD:(h + 1) * QK_PAD], cos, s_up, s_dn)]
            dkvs += [dkc[:, h * QK_PAD:h * QK_PAD + HEAD], dvv[:, h * HEAD:(h + 1) * HEAD]]
        dqf, dkvf = jnp.concatenate(dqs, axis=1), jnp.concatenate(dkvs, axis=1)
        return dqf, dkvf, dqf, dkvf

    def mla_norms_bwd_fn(products, vals):
        dkc, cos, s_up, s_dn, cq, ckv, gq, gk = vals[1], vals[3], vals[4], vals[5], vals[6], vals[7], vals[8], vals[9]
        dkpe = 0.0
        for h in range(nm):
            dkpe = dkpe + dkc[:, h * QK_PAD + HEAD:(h + 1) * QK_PAD]
        dcq, dgq = _rms_bwd(cq, gq, products[0])
        dckv, dgk = _rms_bwd(ckv, gk, products[1])
        return [jnp.concatenate([dcq, dckv, _rope_bwd(dkpe, cos, s_up, s_dn)], axis=1)], [dgq, dgk]

    dmla, dq_full, dkv_full, g_q_norm, g_kv_norm = matmul_fused(
        "mla_prep_bwd_d_norms", mla_prep_bwd_fn,
        [(dq_cat, nm * QK_PAD, 0), (dk_cat, nm * QK_PAD, 0), (dv_att, nm * HEAD, 0)] + tables
        + [(proj_mla, ql, 0), (proj_mla, kvl, 1)], [q_a_norm_g, kv_a_norm_g],
        [w_qb, w_kvb], [(0, 0, nm * QK_PAD), (1, 0, nm * QK_PAD)], "nt",
        [(nm * QK_PAD, BF16), (nm * QK_PAD, BF16)], acc_outs=[(1, ql), (1, kvl)],
        epilogue=mla_norms_bwd_fn, epi_outs=[(mla_w, BF16)])
    g_wqb = matmul("gw_q_b", qn, dq_full, "tn")
    grads["w_q_b"] = g_wqb.reshape(ql, nm, QK_PAD)[:, :, :HEAD + ROPE].reshape(ql, nm * (HEAD + ROPE))
    grads["w_kv_b"] = matmul("gw_kv_b", kvn, dkv_full, "tn")

    d_pieces = [dhq, dhf, dhi, dhg, dgates, dmla]
    gw_parts = [matmul(f"gw_in_{k}", u1, dp, "tn") for k, dp in enumerate(d_pieces)]
    grads["w_in"] = jnp.concatenate(gw_parts[:4] + [gw_parts[5][:, :ql + kvl + ROPE], gw_parts[4]], axis=1)
    grads_late = start_grads("grads_late_start", late)
    w_mla_after = w_mla + grads_late[1][0, 0].astype(BF16)
    w_cat = jnp.concatenate([w_main, w_mla_after], axis=1)
    edges = [0, d, 2 * d, 3 * d, 4 * d, 6 * d, 6 * d + mla_w]

    def first_bwd_fn(products, vals):
        dh1v, h, is_meta, g = vals[6], vals[7], vals[8], vals[9]
        dx, dg = _rms_bwd(h, g, products[0])
        dh0v = dh1v + dx
        return [dh0v], [dg, dh0v * jnp.tile(is_meta, (1, lanes))]

    grad_x, g_mix_pre, meta_sum = matmul_fused(
        "d_u1_norm_mix_pre_bwd", lambda *v: v[:6],
        [(dp, dp.shape[1], 0) for dp in d_pieces] + [(dh1, d, 0), (h0, d, 0), (meta_rows, HEAD, 0, seq_tile)],
        [mix_pre_g], w_cat, list(zip(edges[:-1], edges[1:])), "nt", [], acc_outs=[(1, d), (SEQ_BLOCK, d)],
        epilogue=first_bwd_fn, epi_outs=[(d, F32, bl * seq, real_block)])
    grad_x = grad_x.reshape(bl, seq, d)

    g_parts = finish_grads("early", early, grads_early, g_mix_pre)
    updates = {}

    def update(n, parts):
        w2 = wts[n].reshape(-1, wts[n].shape[-1])
        updates[n] = adamw("adamw_" + n, w2, [p.reshape(w2.shape) for p in parts], mom_m[n].reshape(w2.shape),
                           mom_v[n].reshape(w2.shape))

    for n in early:
        update(n, g_parts[n])
    g_parts = finish_grads("late", late, grads_late, updates[early[-1]][0])
    for n in late:
        update(n, g_parts[n])
    p0 = lb_soft[0:1]
    g_lb_logits = jnp.concatenate([g_lb * p0 * (1.0 - p0), -g_lb * p0 * (1.0 - p0)], axis=0)

    def row_of(vec):
        return vec.reshape(-1, d) if vec.size >= d else jnp.pad(vec.reshape(1, -1), ((0, 0), (0, d - vec.size)))

    small_parts = dict(b_gate=g_b_gate, lb_logits=g_lb_logits, hg_norm_g=g_hg_norm, q_a_norm_g=g_q_norm,
                       kv_a_norm_g=g_kv_norm, mix_pre_g=g_mix_pre, mix_post_g=g_mix_post, ffn_pre_g=g_ffn_pre,
                       ffn_post_g=g_ffn_post)
    g_meta = meta_sum[PAD_FRONT:PAD_FRONT + N_META]
    small_rows = [row_of(small_parts[n]) for n in SMALL] + [row_of(g_meta)]
    n_small = sum(r.shape[0] for r in small_rows)
    small = jnp.pad(jnp.concatenate(small_rows, axis=0), ((0, -(-n_small // 8) * 8 - n_small), (0, 0)))
    all_small = gather_small(small)
    small_t = small.shape[0]

    def sum8_fn(*slabs):
        acc = slabs[0]
        for s in slabs[1:]:
            acc = acc + s
        return acc

    (g_small,) = rowwise("sum_small", sum8_fn, [(all_small.reshape(8 * small_t, d), d, 0, k) for k in range(8)],
                         [], [], [(d, F32)], tm=small_t, n_rows=small_t)

    off = 0
    for n, part in zip(SMALL, small_rows[:-1]):
        rows = part.shape[0]
        update(n, [g_small[off:off + rows, :d].reshape(-1)[:wts[n].size]])
        off += rows
    update("meta_tokens", [lax.dynamic_slice_in_dim(g_small[off:off + N_META, :d], my_chip * mcols, mcols, axis=1)])

    loss = lax.psum(loss_part[0, 0], ("x", "y", "c"))

    def shaped(n, a):
        return a.reshape((1,) + wts[n].shape) if n in BIG else a.reshape(wts[n].shape)

    return (loss, grad_x, *[shaped(n, updates[n][k]) for k in range(4) for n in WEIGHTS])
```

```python
import functools
import math

import jax
import jax.numpy as jnp
from jax import lax
from jax.experimental import pallas as pl
from jax.experimental.pallas import tpu as pltpu

F32 = jnp.float32
BF16 = jnp.bfloat16
MESH = pl.DeviceIdType.MESH

N_META = 16
NORM_EPS = 1e-6
HEAD = 128
ROPE = 64
ROPE_HALF = ROPE // 2
QK_PAD = 2 * HEAD
ROPE_THETA = 10000.0
SEQ_BLOCK = 256
PAD_FRONT = SEQ_BLOCK - N_META
NEG = -1e30
VMEM_LIMIT = 56 * 1024 * 1024
WGRAD_TILE_MAX = 1536
WGRAD_TILE_MIN = 1024
WGRAD_ROW_STEP = 1536
HG_TILE_MAX = 1152
ATTN_HEADS_PER_STEP = 1
ATTN_TILE_MAX = 768

ADAM_LR, ADAM_B1, ADAM_B2, ADAM_EPS, ADAM_WD, ADAM_STEP = 0.001, 0.9, 0.999, 1e-08, 0.01, 10

BIG = ("w_in", "w_hg_o", "w_q_b", "w_kv_b", "w_mla_o", "w_out", "w_ffn_in", "w_ffn_out")
COL_SHARDED = ("w_in", "w_q_b", "w_kv_b", "w_ffn_in")
SMALL = ("b_gate", "lb_logits", "hg_norm_g", "q_a_norm_g", "kv_a_norm_g", "mix_pre_g", "mix_post_g",
         "ffn_pre_g", "ffn_post_g")
WEIGHTS = ("meta_tokens", "w_in", "b_gate", "lb_logits", "hg_norm_g", "w_hg_o", "q_a_norm_g", "w_q_b",
           "kv_a_norm_g", "w_kv_b", "w_mla_o", "w_out", "mix_pre_g", "mix_post_g", "ffn_pre_g", "ffn_post_g",
           "w_ffn_in", "w_ffn_out")


def _tile(n, cap, unit=128):
    if n <= cap:
        return n
    best = None
    for t in range(unit, cap + 1, unit):
        if n % t == 0:
            best = t
    assert best is not None, (n, cap, unit)
    return best


def _sigmoid(x):
    return 1.0 / (1.0 + jnp.exp(-x))


def _bf(x):
    return x.astype(BF16)


def rowwise(name, fn, row_ins, seq_tabs, consts, row_outs, acc_outs=(), tm=SEQ_BLOCK, n_rows=None):
    t_rows = row_ins[0][0].shape[0] if n_rows is None else n_rows
    nt = t_rows // tm
    assert t_rows % tm == 0
    n_in = len(row_ins) + len(seq_tabs) + len(consts)
    n_row = len(row_outs)

    def body(*refs):
        vals = [r[...].astype(F32) for r in refs[:n_in]]
        res = fn(*vals)
        if not isinstance(res, (tuple, list)):
            res = (res,)
        outs = refs[n_in:]
        for k in range(n_row):
            outs[k][...] = res[k].astype(outs[k].dtype)
        if acc_outs:
            @pl.when(pl.program_id(0) == 0)
            def _():
                for k in range(len(acc_outs)):
                    outs[n_row + k][...] = jnp.zeros_like(outs[n_row + k])

            for k in range(len(acc_outs)):
                outs[n_row + k][...] += res[n_row + k]

    row_ins = [tuple(e) + (0,) * (4 - len(e)) for e in row_ins]
    in_specs = [pl.BlockSpec((tm, w), functools.partial(lambda i, j, ro: (ro(i) if callable(ro) else i + ro, j),
                                                        j=j, ro=ro)) for (_, w, j, ro) in row_ins]
    for tab in seq_tabs:
        per = tab.shape[0] // tm
        in_specs.append(pl.BlockSpec((tm, tab.shape[1]), functools.partial(lambda i, per: (i % per, 0), per=per)))
    for c in consts:
        in_specs.append(pl.BlockSpec(c.shape, lambda i: (0, 0)))
    row_outs = [tuple(e) + (t_rows, None)[len(e) - 2:] for e in row_outs]
    out_specs = [pl.BlockSpec((tm, w), functools.partial(lambda i, rm: (i if rm is None else rm(i), 0), rm=rm))
                 for (w, _, _, rm) in row_outs]
    out_specs += [pl.BlockSpec(s, lambda i: (0, 0)) for s in acc_outs]
    out_shape = [jax.ShapeDtypeStruct((rows, w), dt) for (w, dt, rows, _) in row_outs]
    out_shape += [jax.ShapeDtypeStruct(s, F32) for s in acc_outs]
    res = pl.pallas_call(
        body, name=name, grid=(nt,), in_specs=in_specs, out_specs=out_specs, out_shape=out_shape,
        compiler_params=pltpu.CompilerParams(dimension_semantics=("arbitrary",)),
    )(*[e[0] for e in row_ins], *seq_tabs, *consts)
    return res


def matmul(name, a, b, mode, out_dtype=F32):
    if mode != "tn":
        return _matmul_resident(name, a if isinstance(a, (list, tuple)) else [a],
                                b if isinstance(b, (list, tuple)) else [b], mode, out_dtype)
    kdim, m = a.shape
    n = b.shape[1]
    tn = _tile(n, WGRAD_TILE_MAX)
    tm = _tile(m, WGRAD_TILE_MAX if tn <= WGRAD_TILE_MIN else WGRAD_TILE_MIN)
    tk = _tile(kdim, WGRAD_ROW_STEP)
    nk = kdim // tk

    def body(a_ref, b_ref, o_ref, acc_ref):
        k = pl.program_id(2)

        @pl.when(k == 0)
        def _():
            acc_ref[...] = jnp.zeros_like(acc_ref)

        acc_ref[...] += lax.dot_general(a_ref[...], b_ref[...], TN_DIMS, preferred_element_type=F32)

        @pl.when(k == nk - 1)
        def _():
            o_ref[...] = acc_ref[...].astype(o_ref.dtype)

    return pl.pallas_call(
        body, name=name, grid=(m // tm, n // tn, nk),
        in_specs=[pl.BlockSpec((tk, tm), lambda i, j, k: (k, i)), pl.BlockSpec((tk, tn), lambda i, j, k: (k, j))],
        out_specs=pl.BlockSpec((tm, tn), lambda i, j, k: (i, j)),
        out_shape=jax.ShapeDtypeStruct((m, n), out_dtype),
        scratch_shapes=[pltpu.VMEM((tm, tn), F32)],
        compiler_params=pltpu.CompilerParams(dimension_semantics=("arbitrary", "arbitrary", "arbitrary"),
                                             vmem_limit_bytes=VMEM_LIMIT),
    )(a, b)


def _matmul_resident(name, a_list, b_list, mode, out_dtype):
    m = a_list[0].shape[0]
    n = b_list[0].shape[1] if mode == "nn" else b_list[0].shape[0]
    k_total = sum(a.shape[1] for a in a_list)
    out_bytes = 2 if out_dtype == BF16 else 4
    budget = VMEM_LIMIT - 4 * k_total * n - (6 << 20)
    tm = 1024
    while tm > 128 and 2 * tm * (2 * k_total + out_bytes * n) > budget:
        tm //= 2
    tm = _tile(m, tm)
    cn = _tile(n, WGRAD_TILE_MIN)
    npairs = len(a_list)

    def body(*refs):
        a_refs, b_refs, o_ref = refs[:npairs], refs[npairs:2 * npairs], refs[2 * npairs]
        for c in range(n // cn):
            acc = None
            for a_ref, b_ref in zip(a_refs, b_refs):
                if mode == "nn":
                    part = jnp.dot(a_ref[...], b_ref[:, pl.ds(c * cn, cn)], preferred_element_type=F32)
                else:
                    part = lax.dot_general(a_ref[...], b_ref[pl.ds(c * cn, cn), :], NT_DIMS,
                                           preferred_element_type=F32)
                acc = part if acc is None else acc + part
            o_ref[:, pl.ds(c * cn, cn)] = acc.astype(o_ref.dtype)

    in_specs = [pl.BlockSpec((tm, a.shape[1]), lambda i: (i, 0)) for a in a_list]
    in_specs += [pl.BlockSpec(b.shape, lambda i: (0, 0)) for b in b_list]
    return pl.pallas_call(
        body, name=name, grid=(m // tm,), in_specs=in_specs,
        out_specs=pl.BlockSpec((tm, n), lambda i: (i, 0)),
        out_shape=jax.ShapeDtypeStruct((m, n), out_dtype),
        compiler_params=pltpu.CompilerParams(dimension_semantics=("arbitrary",), vmem_limit_bytes=VMEM_LIMIT),
    )(*a_list, *b_list)


def matmul_fused(name, fn, row_ins, consts, weight, pieces, mode, extra_outs, out_dtype=BF16, tm=256, acc_outs=(),
                 epilogue=None, epi_outs=()):
    row_ins = [tuple(e) + (0,) * (4 - len(e)) for e in row_ins]
    t_rows = row_ins[0][0].shape[0]
    tm = _tile(t_rows, tm)
    several = isinstance(weight, (list, tuple))
    weights = list(weight) if several else [weight]
    n_in = len(row_ins) + len(consts)
    n_w = len(weights)
    n_parts = len(pieces)
    n_mm = len(epi_outs) if epilogue is not None else (n_parts if several else 1)
    n_row_out = n_mm + len(extra_outs)

    def width(w_arr):
        return w_arr.shape[1] if mode == "nn" else w_arr.shape[0]

    def body(*refs):
        w_hbms = refs[n_in:n_in + n_w]
        outs = refs[n_in + n_w:n_in + n_w + n_row_out + len(acc_outs)]
        w_refs, sems = refs[-n_w - 1:-1], refs[-1]

        @pl.when(pl.program_id(0) == 0)
        def _():
            cps = [pltpu.make_async_copy(w_hbms[k], w_refs[k], sems.at[k]) for k in range(n_w)]
            for cp in cps:
                cp.start()
            for cp in cps:
                cp.wait()
            for k in range(len(acc_outs)):
                outs[n_row_out + k][...] = jnp.zeros_like(outs[n_row_out + k])

        vals = [r[...].astype(F32) for r in refs[:n_in]]
        res = fn(*vals)
        products = []
        for a_p, piece in zip(res[:n_parts], pieces):
            w_ref, (k0, k1) = (w_refs[piece[0]], piece[1:]) if several else (w_refs[0], piece)
            if mode == "nn":
                products.append(jnp.dot(_bf(a_p), w_ref[pl.ds(k0, k1 - k0), :], preferred_element_type=F32))
            else:
                products.append(lax.dot_general(_bf(a_p), w_ref[:, pl.ds(k0, k1 - k0)], NT_DIMS,
                                                preferred_element_type=F32))
        if not several:
            products = [functools.reduce(lambda u, w: u + w, products)]
        sums = list(res[n_parts + len(extra_outs):])
        if epilogue is not None:
            products, more_sums = epilogue(products, vals)
            sums += list(more_sums)
        for p, val in enumerate(products):
            outs[p][...] = val.astype(outs[p].dtype)
        for o_ref, val in zip(outs[n_mm:n_row_out], res[n_parts:]):
            o_ref[...] = val.astype(o_ref.dtype)
        for k in range(len(acc_outs)):
            outs[n_row_out + k][...] += sums[k]

    in_specs = [pl.BlockSpec((tm, w), functools.partial(lambda i, j, ro: (ro(i) if callable(ro) else i + ro, j),
                                                        j=j, ro=ro)) for (_, w, j, ro) in row_ins]
    in_specs += [pl.BlockSpec(c.shape, lambda i: (0, 0)) for c in consts]
    in_specs += [pl.BlockSpec(memory_space=pl.ANY)] * n_w
    if epilogue is not None:
        widths = list(epi_outs) + list(extra_outs)
    elif several:
        widths = [(width(weights[piece[0]]), out_dtype) for piece in pieces] + list(extra_outs)
    else:
        widths = [(width(weights[0]), out_dtype)] + list(extra_outs)
    widths = [tuple(e) + (t_rows, None)[len(e) - 2:] for e in widths]
    return pl.pallas_call(
        body, name=name, grid=(t_rows // tm,), in_specs=in_specs,
        out_specs=[pl.BlockSpec((tm, w), functools.partial(lambda i, rm: (i if rm is None else rm(i), 0), rm=rm))
                   for (w, _, _, rm) in widths]
        + [pl.BlockSpec(s, lambda i: (0, 0)) for s in acc_outs],
        out_shape=[jax.ShapeDtypeStruct((rows, w), dt) for (w, dt, rows, _) in widths]
        + [jax.ShapeDtypeStruct(s, F32) for s in acc_outs],
        scratch_shapes=[pltpu.VMEM(w_arr.shape, w_arr.dtype) for w_arr in weights] + [pltpu.SemaphoreType.DMA((n_w,))],
        compiler_params=pltpu.CompilerParams(dimension_semantics=("arbitrary",), vmem_limit_bytes=VMEM_LIMIT),
    )(*[e[0] for e in row_ins], *consts, *weights)


def _rms(x, g):
    r = lax.rsqrt(jnp.mean(x * x, axis=-1, keepdims=True) + NORM_EPS)
    return x * r * g


def _rms_bwd(x, g, dy):
    r = lax.rsqrt(jnp.mean(x * x, axis=-1, keepdims=True) + NORM_EPS)
    xh = x * r
    dyg = dy * g
    dx = r * (dyg - xh * jnp.mean(dyg * xh, axis=-1, keepdims=True))
    return dx, jnp.sum(dy * xh, axis=0, keepdims=True)


def _silu(x):
    return x * _sigmoid(x)


def _silu_grad(x):
    s = _sigmoid(x)
    return s * (1.0 + x * (1.0 - s))


def _rope(xs, cos, s_up, s_dn):
    return xs * cos + pltpu.roll(xs, ROPE_HALF, 1) * s_up + pltpu.roll(xs, HEAD - ROPE_HALF, 1) * s_dn


def _rope_bwd(dy, cos, s_up, s_dn):
    return dy * cos + pltpu.roll(dy * s_up, HEAD - ROPE_HALF, 1) + pltpu.roll(dy * s_dn, ROPE_HALF, 1)


HG_SUB = 128
HG_LEVELS = 7
HG_E_ROWS = (HG_LEVELS + 1) * HG_SUB
HG_BWD_GROUP = 6
TN_DIMS = (((0,), (0,)), ((), ()))
NT_DIMS = (((1,), (1,)), ((), ()))


def _hg_constants():
    import numpy as np
    n = HG_SUB
    r = np.arange(n)[:, None]
    c = np.arange(n)[None, :]
    cs, ps = [], []
    for lvl in range(HG_LEVELS):
        m = (n // 2) >> lvl
        upper = (r % (2 * m)) >= m
        mid = (r // (2 * m)) * (2 * m) + m - 1
        cs.append(np.where(upper, (c > mid) & (c <= r), (c > r) & (c <= mid)))
        ps.append(((r // (2 * m)) == (c // (2 * m))) & upper & ((c % (2 * m)) < m))
    cs.append(c <= r)
    cs.append(np.ones((8, n), bool))
    cstack = np.concatenate(cs, 0).astype(np.float32)
    pstack = np.concatenate(ps, 0).astype(np.float32)
    pstack_t = np.concatenate([p.T for p in ps], 0).astype(np.float32)
    return (jnp.asarray(cstack, BF16), jnp.asarray(cstack[:HG_E_ROWS].T, BF16), jnp.asarray(pstack, F32),
            jnp.asarray(pstack_t, F32))


def _split_dot(c_bf, x):
    hi = _bf(x)
    lo = _bf(x - hi.astype(F32))
    r2 = jnp.dot(c_bf, jnp.concatenate([hi, lo], axis=1), preferred_element_type=F32)
    return r2[:, :HEAD] + r2[:, HEAD:]


def _hg_gates(hq, hf, lb):
    sq = _sigmoid(hq)
    sg = _sigmoid(hf)
    fg = lb + (1.0 - lb) * sg
    return sq, hq * sq, sg, fg, 1.0 - fg, jnp.log(fg)


def hgrn_fwd(proj_main, lb, consts, bl, lp, d):
    nh = d // HEAD
    rows_blk = _tile(lp, HG_TILE_MAX, HG_SUB)
    nb = lp // rows_blk
    spb = rows_blk // HG_SUB
    cstack, _, pstack, _ = consts

    def body(hq_ref, hf_ref, hi_ref, lb_ref, c_ref, p_ref, o_ref, st_ref, a_ref, s_ref):
        j = pl.program_id(2)

        @pl.when(j == 0)
        def _():
            s_ref[...] = jnp.zeros_like(s_ref)

        lbv = lb_ref[...]
        cs = c_ref[...]
        rows = [pl.ds(s * HG_SUB, HG_SUB) for s in range(spb)]
        gates = [_hg_gates(hq_ref[r, :].astype(F32), hf_ref[r, :].astype(F32), lbv) for r in rows]
        qs, ks = [g_[1] for g_ in gates], [g_[4] for g_ in gates]
        vs = [hi_ref[r, :].astype(F32) for r in rows]
        es = [_split_dot(cs, g_[5]) for g_ in gates]
        a_acc = [jnp.zeros((HG_SUB, HG_SUB), F32) for _ in rows]
        for lvl in range(HG_LEVELS):
            for s in range(spb):
                x = jnp.exp(es[s][lvl * HG_SUB:(lvl + 1) * HG_SUB])
                a_acc[s] = a_acc[s] + p_ref[pl.ds(lvl * HG_SUB, HG_SUB), :] * lax.dot_general(
                    _bf(qs[s] * x), _bf(ks[s] * x), NT_DIMS, preferred_element_type=F32)
        o_intra, qbs, kds, e_lasts = [], [], [], []
        for s in range(spb):
            a_bf = _bf(a_acc[s])
            a_ref[0, 0, s] = a_bf
            bc = es[s][HG_LEVELS * HG_SUB:HG_E_ROWS]
            b_last = jnp.tile(es[s][HG_E_ROWS:], (HG_SUB // 8, 1))
            o_intra.append(jnp.dot(a_bf, _bf(vs[s]), preferred_element_type=F32)
                           + jnp.sum(qs[s] * ks[s], axis=1, keepdims=True) * vs[s])
            qbs.append(_bf(qs[s] * jnp.exp(bc)))
            kds.append(_bf(ks[s] * jnp.exp(b_last - bc)))
            e_lasts.append(jnp.exp(b_last))
        st = s_ref[...]
        for s in range(spb):
            st_ref[0, 0, s] = st
            o_ref[rows[s], :] = (o_intra[s] + lax.dot_general(qbs[s], _bf(st), NT_DIMS, preferred_element_type=F32)
                                 ).astype(o_ref.dtype)
            st = st * e_lasts[s] + lax.dot_general(_bf(vs[s]), kds[s], TN_DIMS, preferred_element_type=F32)
        s_ref[...] = st

    def colspec(off):
        return pl.BlockSpec((rows_blk, HEAD), functools.partial(lambda h, b, j, off: (b * nb + j, off + h), off=off))

    whole = lambda arr: pl.BlockSpec(arr.shape, lambda h, b, j: (0, 0))
    return pl.pallas_call(
        body, name="hgrn_fwd", grid=(nh, bl, nb),
        in_specs=[colspec(0), colspec(nh), colspec(2 * nh), pl.BlockSpec((1, HEAD), lambda h, b, j: (0, h)),
                  whole(cstack), whole(pstack)],
        out_specs=[pl.BlockSpec((rows_blk, HEAD), lambda h, b, j: (b * nb + j, h)),
                   pl.BlockSpec((1, 1, spb, HEAD, HEAD), lambda h, b, j: (b, h, j, 0, 0)),
                   pl.BlockSpec((1, 1, spb, HG_SUB, HG_SUB), lambda h, b, j: (b, h, j, 0, 0))],
        out_shape=[jax.ShapeDtypeStruct((bl * lp, d), BF16),
                   jax.ShapeDtypeStruct((bl, nh, lp // HG_SUB, HEAD, HEAD), F32),
                   jax.ShapeDtypeStruct((bl, nh, lp // HG_SUB, HG_SUB, HG_SUB), BF16)],
        scratch_shapes=[pltpu.VMEM((HEAD, HEAD), F32)],
        compiler_params=pltpu.CompilerParams(dimension_semantics=("arbitrary", "arbitrary", "arbitrary")),
    )(proj_main, proj_main, proj_main, lb, cstack, pstack)


def hgrn_bwd(proj_main, lb, consts, states, a_mats, do_scan, bl, lp, d):
    nh = d // HEAD
    rows_blk = _tile(lp, HG_TILE_MAX, HG_SUB)
    nb = lp // rows_blk
    spb = rows_blk // HG_SUB
    cstack, cstack_t = consts[0], consts[1]
    pstack, pstack_t = _bf(consts[2]), _bf(consts[3])

    def body(hq_ref, hf_ref, hi_ref, lb_ref, c_ref, ct_ref, p_ref, pt_ref, st_ref, a_ref, do_ref,
             dq_ref, df_ref, di_ref, dlb_ref, ds_ref, e_ref, de_ref):
        b_id, j = pl.program_id(1), pl.program_id(2)

        @pl.when(j == 0)
        def _():
            ds_ref[...] = jnp.zeros_like(ds_ref)

        @pl.when((j == 0) & (b_id == 0))
        def _():
            dlb_ref[...] = jnp.zeros_like(dlb_ref)

        lbv = lb_ref[...]
        cs = c_ref[...]
        cst = ct_ref[...]

        dlb = jnp.zeros((1, HEAD), F32)
        for first in reversed(range(0, spb, HG_BWD_GROUP)):
            dlb = dlb + _hg_group_bwd(list(range(first, min(first + HG_BWD_GROUP, spb))), lbv, cs, cst, hq_ref,
                                      hf_ref, hi_ref, st_ref, a_ref, do_ref, p_ref, pt_ref, dq_ref, df_ref, di_ref,
                                      ds_ref, e_ref, de_ref)
        dlb_ref[...] += dlb

    def _hg_group_bwd(ids, lbv, cs, cst, hq_ref, hf_ref, hi_ref, st_ref, a_ref, do_ref, p_ref, pt_ref, dq_ref,
                      df_ref, di_ref, ds_ref, e_ref, de_ref):
        rng = range(len(ids))
        rows = [pl.ds(s * HG_SUB, HG_SUB) for s in ids]

        def block(s):
            hq = hq_ref[rows[s], :].astype(F32)
            sq, q, sg, fg, k, g = _hg_gates(hq, hf_ref[rows[s], :].astype(F32), lbv)
            return hq, sq, q, sg, fg, k, g, do_ref[rows[s], :].astype(F32)

        def decays(s, q, k):
            bc = e_ref[ids[s], pl.ds(HG_LEVELS * HG_SUB, HG_SUB), :]
            b_last = jnp.tile(e_ref[ids[s], pl.ds(HG_E_ROWS, 8), :], (HG_SUB // 8, 1))
            eb, er = jnp.exp(bc), jnp.exp(b_last - bc)
            return eb, q * eb, er, k * er, jnp.exp(b_last)

        e_lasts, m_s = [], []
        for s in rng:
            _, _, q, _, _, k, g, do = block(s)
            e_ref[ids[s]] = _split_dot(cs, g)
            _, qb, _, _, e_last = decays(s, q, k)
            e_lasts.append(e_last)
            m_s.append(lax.dot_general(_bf(do), _bf(qb), TN_DIMS, preferred_element_type=F32))
        dst_outs = [None] * len(ids)
        dst = ds_ref[...]
        for s in reversed(rng):
            dst_outs[s] = dst
            dst = dst * e_lasts[s] + m_s[s]
        ds_ref[...] = dst
        dlb = jnp.zeros((1, HEAD), F32)
        for s in rng:
            hq, sq, q, sg, fg, k, _, do = block(s)
            eb, qb, er, kd, e_last = decays(s, q, k)
            v = hi_ref[rows[s], :].astype(F32)
            st = st_ref[0, 0, ids[s]]
            do_bf, v_bf, dst_bf = _bf(do), _bf(v), _bf(dst_outs[s])
            da = _bf(lax.dot_general(do_bf, v_bf, NT_DIMS, preferred_element_type=F32))
            dat = _bf(lax.dot_general(v_bf, do_bf, NT_DIMS, preferred_element_type=F32))
            dqb = jnp.dot(do_bf, _bf(st), preferred_element_type=F32)
            d_diag = jnp.sum(do * v, axis=1, keepdims=True)
            dv = (lax.dot_general(a_ref[0, 0, ids[s]], do_bf, TN_DIMS, preferred_element_type=F32)
                  + jnp.sum(q * k, axis=1, keepdims=True) * do
                  + lax.dot_general(_bf(kd), dst_bf, NT_DIMS, preferred_element_type=F32))
            di_ref[rows[s], :] = dv.astype(di_ref.dtype)
            dkd = jnp.dot(v_bf, dst_bf, preferred_element_type=F32)
            dq = dqb * eb + d_diag * k
            dk = dkd * er + d_diag * q
            d_last = (jnp.sum(dst_outs[s] * st * e_last, axis=0, keepdims=True)
                      + jnp.sum(dkd * kd, axis=0, keepdims=True))
            de_ref[ids[s], pl.ds(HG_LEVELS * HG_SUB, HG_SUB), :] = dqb * qb - dkd * kd
            for lvl in range(HG_LEVELS):
                x = jnp.exp(e_ref[ids[s], pl.ds(lvl * HG_SUB, HG_SUB), :])
                qh, kh = q * x, k * x
                dm = p_ref[pl.ds(lvl * HG_SUB, HG_SUB), :] * da
                dmt = pt_ref[pl.ds(lvl * HG_SUB, HG_SUB), :] * dat
                dqh = jnp.dot(dm, _bf(kh), preferred_element_type=F32)
                dkh = jnp.dot(dmt, _bf(qh), preferred_element_type=F32)
                dq = dq + dqh * x
                dk = dk + dkh * x
                de_ref[ids[s], pl.ds(lvl * HG_SUB, HG_SUB), :] = dqh * qh + dkh * kh
            dg = _split_dot(cst, de_ref[ids[s]]) + d_last
            dfg = dg / fg - dk
            dq_ref[rows[s], :] = (dq * (sq * (1.0 + hq * (1.0 - sq)))).astype(dq_ref.dtype)
            df_ref[rows[s], :] = (dfg * (1.0 - lbv) * sg * (1.0 - sg)).astype(df_ref.dtype)
            dlb = dlb + jnp.sum(dfg * (1.0 - sg), axis=0, keepdims=True)
        return dlb

    def colspec(off):
        return pl.BlockSpec((rows_blk, HEAD),
                            functools.partial(lambda h, b, j, off: (b * nb + nb - 1 - j, off + h), off=off))

    whole = lambda arr: pl.BlockSpec(arr.shape, lambda h, b, j: (0, 0))
    mats = lambda: pl.BlockSpec((1, 1, spb, HEAD, HEAD), lambda h, b, j: (b, h, nb - 1 - j, 0, 0))
    t_rows = bl * lp
    return pl.pallas_call(
        body, name="hgrn_bwd", grid=(nh, bl, nb),
        in_specs=[colspec(0), colspec(nh), colspec(2 * nh), pl.BlockSpec((1, HEAD), lambda h, b, j: (0, h)),
                  whole(cstack), whole(cstack_t), whole(pstack), whole(pstack_t), mats(), mats(), colspec(0)],
        out_specs=[colspec(0), colspec(0), colspec(0), pl.BlockSpec((1, HEAD), lambda h, b, j: (0, h))],
        out_shape=[jax.ShapeDtypeStruct((t_rows, d), BF16)] * 3 + [jax.ShapeDtypeStruct((1, d), F32)],
        scratch_shapes=[pltpu.VMEM((HEAD, HEAD), F32), pltpu.VMEM((spb, HG_E_ROWS + 8, HEAD), F32),
                        pltpu.VMEM((spb, HG_E_ROWS, HEAD), F32)],
        compiler_params=pltpu.CompilerParams(dimension_semantics=("arbitrary", "arbitrary", "arbitrary")),
    )(proj_main, proj_main, proj_main, lb, cstack, cstack_t, pstack, pstack_t, states, a_mats, do_scan)


def _key_query_mask(key0, qry0, nk, nq_, causal):
    key = key0 + lax.broadcasted_iota(jnp.int32, (nk, 1), 0)
    if not causal:
        return key >= PAD_FRONT
    qry = qry0 + lax.broadcasted_iota(jnp.int32, (1, nq_), 1)
    return (key <= qry) & (key >= PAD_FRONT)


def _attn_tile(lp):
    return _tile(lp, ATTN_TILE_MAX, SEQ_BLOCK)


def attn_fwd_t(q_cat, k_cat, v_t, bl, lp, nm):
    tq = tk = _attn_tile(lp)
    nq = lp // tq
    hp = ATTN_HEADS_PER_STEP
    assert nm % hp == 0

    def body(q_ref, k_ref, vt_ref, o_ref, lse_ref, m_ref, l_ref, acc_ref):
        i = pl.program_id(2)
        m_ref[...] = jnp.full_like(m_ref, NEG)
        l_ref[...] = jnp.zeros_like(l_ref)
        acc_ref[...] = jnp.zeros_like(acc_ref)

        def step(c, mask):
            c0 = pl.multiple_of(c * tk, tk)
            for hh in range(hp):
                cols = pl.ds(hh * QK_PAD, QK_PAD)
                st = lax.dot_general(k_ref[pl.ds(c0, tk), cols], q_ref[:, cols], NT_DIMS,
                                     preferred_element_type=F32)
                if mask is not None:
                    st = jnp.where(_key_query_mask(c * tk, i * tq, tk, tq, mask == "causal"), st, NEG)
                m_old = m_ref[hh]
                m_new = jnp.maximum(m_old, jnp.max(st, axis=0, keepdims=True))
                alpha = jnp.exp(m_old - m_new)
                pt = jnp.exp(st - m_new)
                l_ref[hh] = alpha * l_ref[hh] + jnp.sum(pt, axis=0, keepdims=True)
                acc_ref[hh] = alpha * acc_ref[hh] + jnp.dot(vt_ref[0, hh, pl.ds(c, 1)][0], _bf(pt),
                                                            preferred_element_type=F32)
                m_ref[hh] = m_new

        def mid(c, carry):
            step(c, None)
            return carry

        @pl.when(i == 0)
        def _():
            step(0, "causal")

        @pl.when(i > 0)
        def _():
            step(0, "pad")
            lax.fori_loop(1, i, mid, 0)
            step(i, "causal")

        for hh in range(hp):
            o_ref[:, pl.ds(hh * HEAD, HEAD)] = jnp.transpose(acc_ref[hh] / l_ref[hh]).astype(o_ref.dtype)
            lse_ref[0, hh, 0] = m_ref[hh] + jnp.log(l_ref[hh])

    return pl.pallas_call(
        body, name="attn_fwd", grid=(bl, nm // hp, nq),
        in_specs=[pl.BlockSpec((tq, hp * QK_PAD), lambda b, h, i: (b * nq + i, h)),
                  pl.BlockSpec((lp, hp * QK_PAD), lambda b, h, i: (b, h)),
                  pl.BlockSpec((1, hp, nq, HEAD, tk), lambda b, h, i: (b, h, 0, 0, 0))],
        out_specs=[pl.BlockSpec((tq, hp * HEAD), lambda b, h, i: (b * nq + i, h)),
                   pl.BlockSpec((1, hp, 1, 1, tq), lambda b, h, i: (b, h, i, 0, 0))],
        out_shape=[jax.ShapeDtypeStruct((bl * lp, nm * HEAD), BF16),
                   jax.ShapeDtypeStruct((bl, nm, nq, 1, tq), F32)],
        scratch_shapes=[pltpu.VMEM((hp, 1, tq), F32), pltpu.VMEM((hp, 1, tq), F32), pltpu.VMEM((hp, HEAD, tq), F32)],
        compiler_params=pltpu.CompilerParams(dimension_semantics=("arbitrary", "arbitrary", "arbitrary")),
    )(q_cat, k_cat, v_t)


def attn_bwd_t(q_cat, k_cat, k_t, v, o, do, lse, bl, lp, nm):
    tq = tk = _attn_tile(lp)
    nq = lp // tq
    hp = ATTN_HEADS_PER_STEP
    assert nm % hp == 0

    def body(q_ref, k_ref, kt_ref, v_ref, o_ref, do_ref, lse_ref, dq_ref, dk_ref, dv_ref, dqt_ref, dka_ref, dva_ref):
        i = pl.program_id(2)

        @pl.when(i == 0)
        def _():
            dqt_ref[...] = jnp.zeros_like(dqt_ref)

        dka_ref[...] = jnp.zeros_like(dka_ref)
        dva_ref[...] = jnp.zeros_like(dva_ref)
        ones8 = jnp.ones((8, HEAD), BF16)

        def step(c, mask):
            c0 = pl.multiple_of(c * tq, tq)
            for hh in range(hp):
                qcols, vcols = pl.ds(hh * QK_PAD, QK_PAD), pl.ds(hh * HEAD, HEAD)
                qs = q_ref[pl.ds(c0, tq), qcols]
                dos = do_ref[pl.ds(c0, tq), vcols]
                prod = dos.astype(F32) * o_ref[pl.ds(c0, tq), vcols].astype(F32)
                hi = _bf(prod)
                lo = _bf(prod - hi.astype(F32))
                delta8 = (lax.dot_general(ones8, hi, NT_DIMS, preferred_element_type=F32)
                          + lax.dot_general(ones8, lo, NT_DIMS, preferred_element_type=F32))
                st = lax.dot_general(k_ref[:, qcols], qs, NT_DIMS, preferred_element_type=F32)
                pt = jnp.exp(st - lse_ref[0, hh, pl.ds(c, 1)][0])
                if mask is not None:
                    pt = jnp.where(_key_query_mask(i * tk, c * tq, tk, tq, mask == "causal"), pt, 0.0)
                dva_ref[hh] += jnp.dot(_bf(pt), dos, preferred_element_type=F32)
                dpt = lax.dot_general(v_ref[:, vcols], dos, NT_DIMS, preferred_element_type=F32)
                dst = _bf(pt * (dpt - jnp.tile(delta8, (tk // 8, 1))))
                dka_ref[hh] += jnp.dot(dst, qs, preferred_element_type=F32)
                dqt_ref[hh, pl.ds(c, 1)] += jnp.dot(kt_ref[0, hh, 0], dst, preferred_element_type=F32)[None]

        step(i, "causal")

        def rest_masked(c, carry):
            step(c, "pad")
            return carry

        def rest(c, carry):
            step(c, None)
            return carry

        @pl.when(i == 0)
        def _():
            lax.fori_loop(1, nq, rest_masked, 0)

        @pl.when(i > 0)
        def _():
            lax.fori_loop(i + 1, nq, rest, 0)

        for hh in range(hp):
            dk_ref[:, pl.ds(hh * QK_PAD, QK_PAD)] = dka_ref[hh].astype(dk_ref.dtype)
            dv_ref[:, pl.ds(hh * HEAD, HEAD)] = dva_ref[hh].astype(dv_ref.dtype)

        @pl.when(i == nq - 1)
        def _():
            for hh in range(hp):
                for c in range(nq):
                    dq_ref[pl.ds(c * tq, tq), pl.ds(hh * QK_PAD, QK_PAD)] = (
                        jnp.transpose(dqt_ref[hh, c])).astype(dq_ref.dtype)

    return pl.pallas_call(
        body, name="attn_bwd", grid=(bl, nm // hp, nq),
        in_specs=[pl.BlockSpec((lp, hp * QK_PAD), lambda b, h, i: (b, h)),
                  pl.BlockSpec((tk, hp * QK_PAD), lambda b, h, i: (b * nq + i, h)),
                  pl.BlockSpec((1, hp, 1, QK_PAD, tk), lambda b, h, i: (b, h, i, 0, 0)),
                  pl.BlockSpec((tk, hp * HEAD), lambda b, h, i: (b * nq + i, h)),
                  pl.BlockSpec((lp, hp * HEAD), lambda b, h, i: (b, h)),
                  pl.BlockSpec((lp, hp * HEAD), lambda b, h, i: (b, h)),
                  pl.BlockSpec((1, hp, nq, 1, tq), lambda b, h, i: (b, h, 0, 0, 0))],
        out_specs=[pl.BlockSpec((lp, hp * QK_PAD), lambda b, h, i: (b, h)),
                   pl.BlockSpec((tk, hp * QK_PAD), lambda b, h, i: (b * nq + i, h)),
                   pl.BlockSpec((tk, hp * HEAD), lambda b, h, i: (b * nq + i, h))],
        out_shape=[jax.ShapeDtypeStruct((bl * lp, nm * QK_PAD), BF16),
                   jax.ShapeDtypeStruct((bl * lp, nm * QK_PAD), BF16),
                   jax.ShapeDtypeStruct((bl * lp, nm * HEAD), BF16)],
        scratch_shapes=[pltpu.VMEM((hp, nq, QK_PAD, tq), F32), pltpu.VMEM((hp, tk, QK_PAD), F32),
                        pltpu.VMEM((hp, tk, HEAD), F32)],
        compiler_params=pltpu.CompilerParams(dimension_semantics=("arbitrary", "arbitrary", "arbitrary")),
    )(q_cat, k_cat, k_t, v, o, do, lse)


def _place():
    return lax.axis_index("x"), lax.axis_index("y"), lax.axis_index("c")


def gather_shards(packed):
    hbm = pl.BlockSpec(memory_space=pl.ANY)

    def body(src_ref, out_ref, send_sems, recv_sems, local_sem):
        x, y, c = _place()
        me = 2 * x + y
        chips = [(1 - x, y), (x, 1 - y), (1 - x, 1 - y)]
        local = pltpu.make_async_copy(src_ref, out_ref.at[me], local_sem)
        local.start()
        sends = []
        for k, (px, py) in enumerate(chips):
            cp = pltpu.make_async_remote_copy(src_ref=src_ref, dst_ref=out_ref.at[me], send_sem=send_sems.at[k],
                                              recv_sem=recv_sems.at[k], device_id=(px, py, c), device_id_type=MESH)
            cp.start()
            sends.append(cp)
        for k, (px, py) in enumerate(chips):
            pltpu.make_async_remote_copy(src_ref=src_ref, dst_ref=out_ref.at[2 * px + py], send_sem=send_sems.at[k],
                                         recv_sem=recv_sems.at[k], device_id=(px, py, c),
                                         device_id_type=MESH).wait_recv()
        for cp in sends:
            cp.wait_send()
        local.wait()

    return pl.pallas_call(
        body, name="gather_shards", in_specs=[hbm], out_specs=hbm,
        out_shape=jax.ShapeDtypeStruct((4,) + packed.shape, packed.dtype),
        scratch_shapes=[pltpu.SemaphoreType.DMA((3,)), pltpu.SemaphoreType.DMA((3,)), pltpu.SemaphoreType.DMA],
    )(packed)


def gather_small(small):
    hbm = pl.BlockSpec(memory_space=pl.ANY)

    def body(small_ref, all_ref, send_sems, recv_sems, local_sem):
        x, y, c = _place()
        me = 4 * x + 2 * y + c
        local = pltpu.make_async_copy(small_ref, all_ref.at[me], local_sem)
        local.start()
        others = [(x ^ ((r >> 2) & 1), y ^ ((r >> 1) & 1), c ^ (r & 1)) for r in range(1, 8)]
        sends = []
        for r, peer in enumerate(others):
            cp = pltpu.make_async_remote_copy(src_ref=small_ref, dst_ref=all_ref.at[me], send_sem=send_sems.at[r],
                                              recv_sem=recv_sems.at[r], device_id=peer, device_id_type=MESH)
            cp.start()
            sends.append(cp)
        for r, (px, py, pc) in enumerate(others):
            pltpu.make_async_remote_copy(src_ref=small_ref, dst_ref=all_ref.at[4 * px + 2 * py + pc],
                                         send_sem=send_sems.at[r], recv_sem=recv_sems.at[r],
                                         device_id=(px, py, pc), device_id_type=MESH).wait_recv()
        for cp in sends:
            cp.wait_send()
        local.wait()

    return pl.pallas_call(
        body, name="gather_small", in_specs=[hbm], out_specs=hbm,
        out_shape=jax.ShapeDtypeStruct((8,) + small.shape, small.dtype),
        scratch_shapes=[pltpu.SemaphoreType.DMA((7,)), pltpu.SemaphoreType.DMA((7,)), pltpu.SemaphoreType.DMA],
    )(small)


def swap_with_sibling(name, parts):
    n = len(parts)
    hbm = pl.BlockSpec(memory_space=pl.ANY)

    def body(*refs):
        x, y, c = _place()
        cps = [pltpu.make_async_remote_copy(src_ref=refs[j], dst_ref=refs[n + j], send_sem=refs[2 * n].at[j],
                                            recv_sem=refs[2 * n + 1].at[j], device_id=(x, y, 1 - c),
                                            device_id_type=MESH) for j in range(n)]
        for cp in cps:
            cp.start()
        for cp in cps:
            cp.wait()

    return pl.pallas_call(
        body, name=name, in_specs=[hbm] * n, out_specs=[hbm] * n,
        out_shape=[jax.ShapeDtypeStruct(p.shape, p.dtype) for p in parts],
        scratch_shapes=[pltpu.SemaphoreType.DMA((n,)), pltpu.SemaphoreType.DMA((n,))],
    )(*parts)


def _chips3():
    x, y, c = _place()
    return [(1 - x, y, c), (x, 1 - y, c), (1 - x, 1 - y, c)]


def _push_copies(src_refs, land_refs, send_sems, recv_sems, per_chip):
    x, y, _ = _place()
    cps = []
    for j, (src_ref, land_ref) in enumerate(zip(src_refs, land_refs)):
        for k, (px, py, pc) in enumerate(_chips3()):
            part = src_ref.at[2 * px + py] if per_chip else src_ref
            slot = k if per_chip else 2 * x + y
            cps.append(pltpu.make_async_remote_copy(
                src_ref=part, dst_ref=land_ref.at[slot], send_sem=send_sems.at[3 * j + k],
                recv_sem=recv_sems.at[3 * j + k], device_id=(px, py, pc), device_id_type=MESH))
    return cps


def push_start(name, srcs, per_chip):
    n = len(srcs)
    hbm = pl.BlockSpec(memory_space=pltpu.HBM)
    sem = pl.BlockSpec(memory_space=pltpu.SEMAPHORE)
    lands = [lax.empty((3 if per_chip else 4,) + s.shape[-2:], s.dtype) for s in srcs]

    def body(*refs):
        src_refs, land_refs = refs[:n], refs[n:2 * n]
        send_sems, recv_sems = refs[2 * n], refs[2 * n + 1]
        for cp in _push_copies(src_refs, land_refs, send_sems, recv_sems, per_chip):
            cp.start()
        refs[-1][...] = jnp.zeros_like(refs[-1])

    outs = pl.pallas_call(
        body, name=name,
        out_shape=(pltpu.SemaphoreType.DMA((3 * n,)), pltpu.SemaphoreType.DMA((3 * n,)),
                   *[pltpu.HBM(a.shape, a.dtype) for a in list(srcs) + lands], jax.ShapeDtypeStruct((8, HEAD), F32)),
        in_specs=(hbm,) * (2 * n),
        out_specs=(sem, sem) + (hbm,) * (2 * n) + (pl.BlockSpec(memory_space=pltpu.VMEM),),
        input_output_aliases={j: 2 + j for j in range(2 * n)},
        compiler_params=pltpu.CompilerParams(has_side_effects=pltpu.SideEffectType.DATAFLOW_SIDE_EFFECTING),
    )(*[pltpu.with_memory_space_constraint(a, pltpu.HBM) for a in list(srcs) + lands])
    return tuple(outs[:-1]), outs[-1]


def push_wait(name, handle, after, per_chip):
    send_sems, recv_sems = handle[0], handle[1]
    thru = handle[2:]
    n = len(thru) // 2
    hbm = pl.BlockSpec(memory_space=pltpu.HBM)
    sem = pl.BlockSpec(memory_space=pltpu.SEMAPHORE)

    def body(*refs):
        src_refs, land_refs = refs[:n], refs[n:2 * n]
        for cp in _push_copies(src_refs, land_refs, refs[2 * n], refs[2 * n + 1], per_chip):
            cp.wait_send()
            cp.wait_recv()

    outs = pl.pallas_call(
        body, name=name,
        out_shape=tuple(pltpu.HBM(a.shape, a.dtype) for a in thru),
        in_specs=(hbm,) * (2 * n) + (sem, sem, pl.BlockSpec(memory_space=pl.ANY)), out_specs=(hbm,) * (2 * n),
        input_output_aliases={j: j for j in range(2 * n)},
        compiler_params=pltpu.CompilerParams(has_side_effects=pltpu.SideEffectType.DATAFLOW_SIDE_EFFECTING),
    )(*thru, send_sems, recv_sems, after)
    return outs[:n], outs[n:]


def join_gathered(name, own, landed, my_chip):
    blocks = lax.dynamic_update_index_in_dim(landed, own, my_chip, 0)
    _, r, c = blocks.shape
    if name in COL_SHARDED:
        return blocks.transpose(1, 0, 2).reshape(r, 4 * c)
    return blocks.reshape(4 * r, c)


def adamw(name, w, g_parts, m, v):
    r, c = w.shape
    tr = r if r * c <= 65536 else _tile(r, 128, 8)
    ng = len(g_parts)

    def body(*refs):
        w_ref, m_ref, v_ref = refs[0], refs[1 + ng], refs[2 + ng]
        g_ref, d_ref, nm_ref, nv_ref = refs[3 + ng:]
        gv = refs[1][...]
        for k in range(1, ng):
            gv = gv + refs[1 + k][...]
        m_new = ADAM_B1 * m_ref[...] + (1.0 - ADAM_B1) * gv
        v_new = ADAM_B2 * v_ref[...] + (1.0 - ADAM_B2) * (gv * gv)
        m_hat = m_new / (1.0 - ADAM_B1 ** ADAM_STEP)
        v_hat = v_new / (1.0 - ADAM_B2 ** ADAM_STEP)
        g_ref[...] = gv
        d_ref[...] = -ADAM_LR * (m_hat / (jnp.sqrt(v_hat) + ADAM_EPS) + ADAM_WD * w_ref[...])
        nm_ref[...] = m_new
        nv_ref[...] = v_new

    spec = pl.BlockSpec((tr, c), lambda i: (i, 0))
    return pl.pallas_call(
        body, name=name, grid=(r // tr,), in_specs=[spec] * (3 + ng), out_specs=[spec] * 4,
        out_shape=[jax.ShapeDtypeStruct((r, c), F32)] * 4,
        compiler_params=pltpu.CompilerParams(dimension_semantics=("arbitrary",)),
    )(w, *g_parts, m, v)


def split_full(name, full, s):
    if name in COL_SHARDED:
        c = full.shape[1] // 4
        return full[:, s * c:(s + 1) * c]
    r = full.shape[0] // 4
    return full[s * r:(s + 1) * r]


def kernel(x, meta_tokens, w_in, b_gate, lb_logits, hg_norm_g, w_hg_o, q_a_norm_g, w_q_b, kv_a_norm_g, w_kv_b, w_mla_o, w_out, mix_pre_g, mix_post_g, ffn_pre_g, ffn_post_g, w_ffn_in, w_ffn_out, loss_target, m_meta_tokens, m_w_in, m_b_gate, m_lb_logits, m_hg_norm_g, m_w_hg_o, m_q_a_norm_g, m_w_q_b, m_kv_a_norm_g, m_w_kv_b, m_w_mla_o, m_w_out, m_mix_pre_g, m_mix_post_g, m_ffn_pre_g, m_ffn_post_g, m_w_ffn_in, m_w_ffn_out, v_meta_tokens, v_w_in, v_b_gate, v_lb_logits, v_hg_norm_g, v_w_hg_o, v_q_a_norm_g, v_w_q_b, v_kv_a_norm_g, v_w_kv_b, v_w_mla_o, v_w_out, v_mix_pre_g, v_mix_post_g, v_ffn_pre_g, v_ffn_post_g, v_w_ffn_in, v_w_ffn_out):
    wts = dict(meta_tokens=meta_tokens, w_in=w_in[0], b_gate=b_gate, lb_logits=lb_logits, hg_norm_g=hg_norm_g,
               w_hg_o=w_hg_o[0], q_a_norm_g=q_a_norm_g, w_q_b=w_q_b[0], kv_a_norm_g=kv_a_norm_g, w_kv_b=w_kv_b[0],
               w_mla_o=w_mla_o[0], w_out=w_out[0], mix_pre_g=mix_pre_g, mix_post_g=mix_post_g, ffn_pre_g=ffn_pre_g,
               ffn_post_g=ffn_post_g, w_ffn_in=w_ffn_in[0], w_ffn_out=w_ffn_out[0])
    mom_m = dict(meta_tokens=m_meta_tokens, w_in=m_w_in[0], b_gate=m_b_gate, lb_logits=m_lb_logits,
                 hg_norm_g=m_hg_norm_g, w_hg_o=m_w_hg_o[0], q_a_norm_g=m_q_a_norm_g, w_q_b=m_w_q_b[0],
                 kv_a_norm_g=m_kv_a_norm_g, w_kv_b=m_w_kv_b[0], w_mla_o=m_w_mla_o[0], w_out=m_w_out[0],
                 mix_pre_g=m_mix_pre_g, mix_post_g=m_mix_post_g, ffn_pre_g=m_ffn_pre_g, ffn_post_g=m_ffn_post_g,
                 w_ffn_in=m_w_ffn_in[0], w_ffn_out=m_w_ffn_out[0])
    mom_v = dict(meta_tokens=v_meta_tokens, w_in=v_w_in[0], b_gate=v_b_gate, lb_logits=v_lb_logits,
                 hg_norm_g=v_hg_norm_g, w_hg_o=v_w_hg_o[0], q_a_norm_g=v_q_a_norm_g, w_q_b=v_w_q_b[0],
                 kv_a_norm_g=v_kv_a_norm_g, w_kv_b=v_w_kv_b[0], w_mla_o=v_w_mla_o[0], w_out=v_w_out[0],
                 mix_pre_g=v_mix_pre_g, mix_post_g=v_mix_post_g, ffn_pre_g=v_ffn_pre_g, ffn_post_g=v_ffn_post_g,
                 w_ffn_in=v_w_ffn_in[0], w_ffn_out=v_w_ffn_out[0])

    bl, seq, d = x.shape
    lp = PAD_FRONT + N_META + seq
    t_rows = bl * lp
    nh = d // HEAD
    ql, kvl = wts["w_q_b"].shape[0], wts["w_kv_b"].shape[0]
    nm = (4 * wts["w_mla_o"].shape[0]) // HEAD
    ffn = 4 * wts["w_ffn_out"].shape[0]
    mla_w = ql + kvl + HEAD
    assert ql == kvl and ql % HEAD == 0 and seq % SEQ_BLOCK == 0 and d % HEAD == 0
    scale = (HEAD + ROPE) ** -0.5
    my_chip = 2 * lax.axis_index("x") + lax.axis_index("y")

    mcols = meta_tokens.shape[1]
    meta_all = gather_shards(meta_tokens)
    meta_full = jnp.concatenate([meta_all[s] for s in range(4)], axis=1)

    def start_gather(name, names, order_after):
        srcs = [_bf(wts[n]) for n in names]
        if order_after is not None:
            srcs[0] = srcs[0] + order_after[0, 0].astype(BF16)
        return push_start(name, srcs, per_chip=False)

    def finish_gather(name, names, started, after):
        owns, landed = push_wait(name, started[0], after, per_chip=False)
        return {n: join_gathered(n, own, land, my_chip) for n, own, land in zip(names, owns, landed)}

    rest_names = tuple(n for n in BIG if n != "w_in")
    my_c = lax.axis_index("c")
    w_in_bf = _bf(wts["w_in"])
    half = w_in_bf.shape[0] // 2
    own_half = (lax.dynamic_slice_in_dim(w_in_bf, my_c * half, half, axis=0)
                + (meta_all[0, :1, :1] * 0.0)[0, 0].astype(BF16))
    gather_1 = push_start("gather_w_in_start", [own_half], per_chip=False)
    gather_2 = start_gather("gather_rest_start", rest_names, gather_1[1])

    tiles_seq, tiles_real = lp // SEQ_BLOCK, seq // SEQ_BLOCK
    assert PAD_FRONT + N_META == SEQ_BLOCK

    def real_block(i):
        return (i // tiles_seq) * tiles_real + jnp.maximum(i % tiles_seq - 1, 0)

    meta_rows = jnp.broadcast_to(((jnp.arange(lp) >= PAD_FRONT) & (jnp.arange(lp) < PAD_FRONT + N_META)
                                  ).astype(F32)[:, None], (lp, HEAD))
    pos = (jnp.arange(lp, dtype=jnp.int32) - PAD_FRONT).astype(F32)
    inv_freq = 1.0 / (ROPE_THETA ** (jnp.arange(0, ROPE, 2, dtype=F32) / ROPE))
    ang = pos[:, None] * inv_freq[None, :]
    zeros32 = jnp.zeros((lp, ROPE_HALF), F32)
    zeros64 = jnp.zeros((lp, HEAD - ROPE), F32)
    t_cos = jnp.concatenate([jnp.cos(ang), jnp.cos(ang), zeros64], axis=1)
    t_up = jnp.concatenate([zeros32, jnp.sin(ang), zeros64], axis=1)
    t_dn = jnp.concatenate([-jnp.sin(ang), zeros32, zeros64], axis=1)
    real = jnp.broadcast_to((jnp.arange(lp) >= PAD_FRONT + N_META).astype(F32)[:, None], (lp, HEAD))
    lanes = d // HEAD
    lb_soft = jax.nn.softmax(lb_logits.astype(F32), axis=0)
    lb = lb_soft[0:1]

    meta_tile = jnp.concatenate([jnp.zeros((PAD_FRONT, d), F32), meta_full], axis=0)

    def first_fn(xv, is_real, is_meta, mtile, g):
        h = xv * jnp.tile(is_real, (1, lanes)) + mtile * jnp.tile(is_meta, (1, lanes))
        return _rms(h, g), h

    u1, h0 = rowwise("norm_mix_pre", first_fn, [(x.reshape(bl * seq, d), d, 0, real_block)], [real, meta_rows],
                     [meta_tile, mix_pre_g + gather_2[1][0, 0]], [(d, BF16), (d, F32)], n_rows=t_rows)
    _, (fetched,) = push_wait("gather_w_in_wait", gather_1[0], u1, per_chip=False)
    (handed,) = swap_with_sibling("swap_w_in", [fetched])
    halves = jnp.stack([fetched, handed])
    remote = jnp.concatenate([lax.dynamic_index_in_dim(halves, my_c, 0, keepdims=False),
                              lax.dynamic_index_in_dim(halves, 1 - my_c, 0, keepdims=False)], axis=1)
    full = {"w_in": join_gathered("w_in", w_in_bf, remote, my_chip)}
    w_main = jnp.concatenate([full["w_in"][:, :4 * d], full["w_in"][:, -2 * d:]], axis=1)
    w_mla = jnp.pad(full["w_in"][:, 4 * d:4 * d + ql + kvl + ROPE], ((0, 0), (0, HEAD - ROPE)))
    proj_main = matmul("proj_main", u1, w_main, "nn", out_dtype=BF16)
    proj_mla = matmul("proj_mla", u1, w_mla, "nn", out_dtype=BF16)
    hg_consts = _hg_constants()
    o_scan, states, a_mats = hgrn_fwd(proj_main, lb, hg_consts, bl, lp, d)

    full.update(finish_gather("gather_rest_wait", rest_names, gather_2, o_scan))
    w_qb = jnp.pad(full["w_q_b"].reshape(ql, nm, HEAD + ROPE), ((0, 0), (0, 0), (0, QK_PAD - HEAD - ROPE))
                   ).reshape(ql, nm * QK_PAD)
    w_kvb = full["w_kv_b"]

    def hg_out_fn(o, hg, g):
        ov = jnp.concatenate([_rms(o[:, h * HEAD:(h + 1) * HEAD], g) for h in range(nh)], axis=1) * _silu(hg)
        return ov, ov

    y_a, o_hg = matmul_fused("hgrn_out_y_a", hg_out_fn, [(o_scan, d, 0), (proj_main, d, 3)], [hg_norm_g],
                             _bf(full["w_hg_o"]), [(0, d)], "nn", [(d, BF16)], tm=512)

    def seq_tile(i):
        return i % tiles_seq

    tables = [(t_cos, HEAD, 0, seq_tile), (t_up, HEAD, 0, seq_tile), (t_dn, HEAD, 0, seq_tile)]

    def q_norm_fn(cq, cos, s_up, s_dn, g):
        cn = _rms(cq, g)
        return cn, cn

    def q_rope_fn(products, vals):
        qf = products[0] * scale
        cos, s_up, s_dn = vals[1:4]
        qs = []
        for h in range(nm):
            qs += [qf[:, h * QK_PAD:h * QK_PAD + HEAD], _rope(qf[:, h * QK_PAD + HEAD:(h + 1) * QK_PAD], cos, s_up, s_dn)]
        return [jnp.concatenate(qs, axis=1)], []

    q_cat, qn = matmul_fused("q_norm_up_rope", q_norm_fn, [(proj_mla, ql, 0)] + tables, [q_a_norm_g], w_qb,
                             [(0, ql)], "nn", [(ql, BF16)], epilogue=q_rope_fn, epi_outs=[(nm * QK_PAD, BF16)])

    def kv_norm_fn(ckv, kpe, cos, s_up, s_dn, g):
        cn = _rms(ckv, g)
        return cn, cn

    def kv_rope_fn(products, vals):
        kvf = products[0]
        kpe_r = _rope(vals[1], *vals[2:5])
        ks, vs = [], []
        for h in range(nm):
            ks += [kvf[:, h * QK_PAD:h * QK_PAD + HEAD], kpe_r]
            vs += [kvf[:, h * QK_PAD + HEAD:(h + 1) * QK_PAD]]
        return [jnp.concatenate(ks, axis=1), jnp.concatenate(vs, axis=1)], []

    kpe_blk = (ql + kvl) // HEAD
    k_cat, v_att, kvn = matmul_fused("kv_norm_up_rope", kv_norm_fn,
                                     [(proj_mla, kvl, 1), (proj_mla, HEAD, kpe_blk)] + tables, [kv_a_norm_g], w_kvb,
                                     [(0, kvl)], "nn", [(kvl, BF16)], epilogue=kv_rope_fn,
                                     epi_outs=[(nm * QK_PAD, BF16), (nm * HEAD, BF16)])
    at = _attn_tile(lp)
    v_t = v_att.reshape(bl, lp // at, at, nm, HEAD).transpose(0, 3, 1, 4, 2)
    k_t = k_cat.reshape(bl, lp // at, at, nm, QK_PAD).transpose(0, 3, 1, 4, 2)
    o_mla, lse = attn_fwd_t(q_cat, k_cat, v_t, bl, lp, nm)
    y_b = matmul("y_b", o_mla, _bf(full["w_mla_o"]), "nn", out_dtype=BF16)

    def gate_fn(ya, yb, ga, gb, bias):
        zv = _sigmoid(ga + bias[:, :d]) * ya + _sigmoid(gb + bias[:, d:]) * yb
        return zv, zv

    mixed, z = matmul_fused("gate_mix_out", gate_fn,
                            [(y_a, d, 0), (y_b, d, 0), (proj_main, d, 4), (proj_main, d, 5)], [b_gate],
                            _bf(full["w_out"]), [(0, d)], "nn", [(d, BF16)], tm=512)

    def mid_fn(h, mx, g_post, g_pre):
        h1v = h + _rms(mx, g_post)
        u2v = _rms(h1v, g_pre)
        return u2v, h1v, u2v

    gu, h1, u2 = matmul_fused("norm_mid_ffn_in", mid_fn, [(h0, d, 0), (mixed, d, 0)], [mix_post_g, ffn_pre_g],
                              _bf(full["w_ffn_in"]), [(0, d)], "nn", [(d, F32), (d, BF16)])
    def swiglu_fn(gt, up):
        a = _silu(gt) * up
        return a, a

    f_out, act = matmul_fused("swiglu_ffn_out", swiglu_fn, [(gu, ffn, 0), (gu, ffn, 1)], [],
                              _bf(full["w_ffn_out"]), [(0, ffn)], "nn", [(ffn, BF16)])

    def loss_fn(h1v, fv, tg, realv, g_post):
        h2 = h1v + _rms(fv, g_post)
        diff = (h2 - tg) * jnp.tile(realv, (1, lanes))
        part = jnp.broadcast_to(0.5 * jnp.sum(diff * diff, keepdims=True) / d, (1, HEAD))
        dy = diff / d
        df, dg = _rms_bwd(fv, g_post, dy)
        return df, dy, df, part, dg

    d_act, dy, df, loss_part, g_ffn_post = matmul_fused(
        "loss_head_d_act", loss_fn,
        [(h1, d, 0), (f_out, d, 0), (loss_target.reshape(bl * seq, d), d, 0, real_block), (real, HEAD, 0, seq_tile)],
        [ffn_post_g], _bf(full["w_ffn_out"]), [(0, d)], "nt", [(d, BF16), (d, BF16)],
        acc_outs=[(1, HEAD), (1, d)])
    grads = {}
    grads["w_ffn_out"] = matmul("gw_ffn_out", act, df, "tn")

    def swiglu_bwd_fn(gt, up, da):
        dgt, dup = da * up * _silu_grad(gt), da * _silu(gt)
        return dgt, dup, jnp.concatenate([dgt, dup], axis=1)

    du2, dgu = matmul_fused("swiglu_bwd_d_u2", swiglu_bwd_fn, [(gu, ffn, 0), (gu, ffn, 1), (d_act, ffn, 0)], [],
                            _bf(full["w_ffn_in"]), [(0, ffn), (ffn, 2 * ffn)], "nt", [(2 * ffn, BF16)])
    grads["w_ffn_in"] = matmul("gw_ffn_in", u2, dgu, "tn")

    def mid_bwd_fn(dyv, h1v, du2v, mx, g_pre, g_post):
        dx, dg_pre = _rms_bwd(h1v, g_pre, du2v)
        dh1 = dyv + dx
        dmx, dg_post = _rms_bwd(mx, g_post, dh1)
        return dmx, dh1, dmx, dg_pre, dg_post

    dz, dh1, dmixed, g_ffn_pre, g_mix_post = matmul_fused(
        "norm_mid_bwd_d_z", mid_bwd_fn, [(dy, d, 0), (h1, d, 0), (du2, d, 0), (mixed, d, 0)],
        [ffn_pre_g, mix_post_g], _bf(full["w_out"]), [(0, d)], "nt", [(d, BF16), (d, BF16)], tm=512,
        acc_outs=[(1, d), (1, d)])
    grads["w_out"] = matmul("gw_out", z, dmixed, "tn")

    def gate_bwd_fn(dzv, ya, yb, ga, gb, bias):
        sa, sb = _sigmoid(ga + bias[:, :d]), _sigmoid(gb + bias[:, d:])
        dga = dzv * ya * sa * (1.0 - sa)
        dgb = dzv * yb * sb * (1.0 - sb)
        dgates = jnp.concatenate([dga, dgb], axis=1)
        dya, dyb = dzv * sa, dzv * sb
        return dya, dyb, dya, dyb, dgates, jnp.sum(dgates, axis=0, keepdims=True)

    def hg_out_bwd_fn(products, vals):
        do, o, hg, g = products[0], vals[5], vals[6], vals[8]
        dn = do * _silu(hg)
        dos, dgs, ons = [], 0.0, []
        for h in range(nh):
            sl = slice(h * HEAD, (h + 1) * HEAD)
            dx, dg = _rms_bwd(o[:, sl], g, dn[:, sl])
            dos.append(dx)
            dgs = dgs + dg
            ons.append(_rms(o[:, sl], g))
        dhg_v = do * jnp.concatenate(ons, axis=1) * _silu_grad(hg)
        return [jnp.concatenate(dos, axis=1), products[1], dhg_v], [dgs]

    do_scan, do_mla, dhg, dy_a, dy_b, dgates, g_b_gate, g_hg_norm = matmul_fused(
        "gate_mix_bwd_d_o", lambda dzv, ya, yb, ga, gb, o, hg, bias, g: gate_bwd_fn(dzv, ya, yb, ga, gb, bias),
        [(dz, d, 0), (y_a, d, 0), (y_b, d, 0), (proj_main, d, 4), (proj_main, d, 5), (o_scan, d, 0),
         (proj_main, d, 3)], [b_gate, hg_norm_g],
        [_bf(full["w_hg_o"]), _bf(full["w_mla_o"])], [(0, 0, d), (1, 0, d)], "nt",
        [(d, BF16), (d, BF16), (2 * d, BF16)], acc_outs=[(1, 2 * d), (1, HEAD)],
        epilogue=hg_out_bwd_fn, epi_outs=[(d, BF16), (d, BF16), (d, BF16)])
    grads["w_hg_o"] = matmul("gw_hg_o", o_hg, dy_a, "tn")
    grads["w_mla_o"] = matmul("gw_mla_o", o_mla, dy_b, "tn")

    early = ("w_hg_o", "w_mla_o", "w_out", "w_ffn_in", "w_ffn_out")
    late = ("w_in", "w_q_b", "w_kv_b")

    def start_grads(name, names):
        sends = [_bf(jnp.stack([split_full(n, grads[n], s) for s in range(4)])) for n in names]
        mines = []
        for n in names:
            r, c = wts[n].shape
            axis, size = (1, c) if n in COL_SHARDED else (0, r)
            mines.append(lax.dynamic_slice_in_dim(grads[n], my_chip * size, size, axis=axis))
        handle, token = push_start(name, sends, per_chip=True)
        return handle, token, mines

    def finish_grads(tag, names, started, after):
        handle, _, mines = started
        _, landed = push_wait(f"grads_{tag}_wait", handle, after, per_chip=True)
        parts = []
        for n, mine, land in zip(names, mines, landed):
            r, c = mine.shape
            tr = _tile(r, 256, 16)
            land2 = land.reshape(3 * r, c)
            parts.append(rowwise(f"sum_chips_{n}", lambda a, r0, r1, r2: a + r0 + r1 + r2,
                                 [(mine, c, 0)] + [(land2, c, 0, k * (r // tr)) for k in range(3)],
                                 [], [], [(c, F32)], tm=tr)[0])
        sibs = swap_with_sibling(f"swap_{tag}", parts)
        return {n: [p, s] for n, p, s in zip(names, parts, sibs)}

    grads_early = start_grads("grads_early_start", early)
    token_a = grads_early[1]

    dhq, dhf, dhi, g_lb = hgrn_bwd(proj_main, lb + token_a[0, 0], hg_consts, states, a_mats, do_scan, bl, lp, d)

    dq_cat, dk_cat, dv_att = attn_bwd_t(q_cat, k_cat, k_t, v_att, o_mla, do_mla, lse, bl, lp, nm)

    def mla_prep_bwd_fn(dqc, dkc, dvv, cos, s_up, s_dn, cq, ckv, gq, gk):
        dqc = dqc * scale
        dqs, dkvs = [], []
        for h in range(nm):
            dqs += [dqc[:, h * QK_PAD:h * QK_PAD + HEAD],
                    _rope_bwd(dqc[:, h * QK_PAD + HEAD:(h + 1) * QK_PAD], cos, s_up, s_dn)]
            dkvs += [dkc[:, h * QK_PAD:h * QK_PAD + HEAD], dvv[:, h * HEAD:(h + 1) * HEAD]]
        dqf, dkvf = jnp.concatenate(dqs, axis=1), jnp.concatenate(dkvs, axis=1)
        return dqf, dkvf, dqf, dkvf

    def mla_norms_bwd_fn(products, vals):
        dkc, cos, s_up, s_dn, cq, ckv, gq, gk = vals[1], vals[3], vals[4], vals[5], vals[6], vals[7], vals[8], vals[9]
        dkpe = 0.0
        for h in range(nm):
            dkpe = dkpe + dkc[:, h * QK_PAD + HEAD:(h + 1) * QK_PAD]
        dcq, dgq = _rms_bwd(cq, gq, products[0])
        dckv, dgk = _rms_bwd(ckv, gk, products[1])
        return [jnp.concatenate([dcq, dckv, _rope_bwd(dkpe, cos, s_up, s_dn)], axis=1)], [dgq, dgk]

    dmla, dq_full, dkv_full, g_q_norm, g_kv_norm = matmul_fused(
        "mla_prep_bwd_d_norms", mla_prep_bwd_fn,
        [(dq_cat, nm * QK_PAD, 0), (dk_cat, nm * QK_PAD, 0), (dv_att, nm * HEAD, 0)] + tables
        + [(proj_mla, ql, 0), (proj_mla, kvl, 1)], [q_a_norm_g, kv_a_norm_g],
        [w_qb, w_kvb], [(0, 0, nm * QK_PAD), (1, 0, nm * QK_PAD)], "nt",
        [(nm * QK_PAD, BF16), (nm * QK_PAD, BF16)], acc_outs=[(1, ql), (1, kvl)],
        epilogue=mla_norms_bwd_fn, epi_outs=[(mla_w, BF16)])
    g_wqb = matmul("gw_q_b", qn, dq_full, "tn")
    grads["w_q_b"] = g_wqb.reshape(ql, nm, QK_PAD)[:, :, :HEAD + ROPE].reshape(ql, nm * (HEAD + ROPE))
    grads["w_kv_b"] = matmul("gw_kv_b", kvn, dkv_full, "tn")

    d_pieces = [dhq, dhf, dhi, dhg, dgates, dmla]
    gw_parts = [matmul(f"gw_in_{k}", u1, dp, "tn") for k, dp in enumerate(d_pieces)]
    grads["w_in"] = jnp.concatenate(gw_parts[:4] + [gw_parts[5][:, :ql + kvl + ROPE], gw_parts[4]], axis=1)
    grads_late = start_grads("grads_late_start", late)
    w_mla_after = w_mla + grads_late[1][0, 0].astype(BF16)
    w_cat = jnp.concatenate([w_main, w_mla_after], axis=1)
    edges = [0, d, 2 * d, 3 * d, 4 * d, 6 * d, 6 * d + mla_w]

    def first_bwd_fn(products, vals):
        dh1v, h, is_meta, g = vals[6], vals[7], vals[8], vals[9]
        dx, dg = _rms_bwd(h, g, products[0])
        dh0v = dh1v + dx
        return [dh0v], [dg, dh0v * jnp.tile(is_meta, (1, lanes))]

    grad_x, g_mix_pre, meta_sum = matmul_fused(
        "d_u1_norm_mix_pre_bwd", lambda *v: v[:6],
        [(dp, dp.shape[1], 0) for dp in d_pieces] + [(dh1, d, 0), (h0, d, 0), (meta_rows, HEAD, 0, seq_tile)],
        [mix_pre_g], w_cat, list(zip(edges[:-1], edges[1:])), "nt", [], acc_outs=[(1, d), (SEQ_BLOCK, d)],
        epilogue=first_bwd_fn, epi_outs=[(d, F32, bl * seq, real_block)])
    grad_x = grad_x.reshape(bl, seq, d)

    g_parts = finish_grads("early", early, grads_early, g_mix_pre)
    updates = {}

    def update(n, parts):
        w2 = wts[n].reshape(-1, wts[n].shape[-1])
        updates[n] = adamw("adamw_" + n, w2, [p.reshape(w2.shape) for p in parts], mom_m[n].reshape(w2.shape),
                           mom_v[n].reshape(w2.shape))

    for n in early:
        update(n, g_parts[n])
    g_parts = finish_grads("late", late, grads_late, updates[early[-1]][0])
    for n in late:
        update(n, g_parts[n])
    p0 = lb_soft[0:1]
    g_lb_logits = jnp.concatenate([g_lb * p0 * (1.0 - p0), -g_lb * p0 * (1.0 - p0)], axis=0)

    def row_of(vec):
        return vec.reshape(-1, d) if vec.size >= d else jnp.pad(vec.reshape(1, -1), ((0, 0), (0, d - vec.size)))

    small_parts = dict(b_gate=g_b_gate, lb_logits=g_lb_logits, hg_norm_g=g_hg_norm, q_a_norm_g=g_q_norm,
                       kv_a_norm_g=g_kv_norm, mix_pre_g=g_mix_pre, mix_post_g=g_mix_post, ffn_pre_g=g_ffn_pre,
                       ffn_post_g=g_ffn_post)
    g_meta = meta_sum[PAD_FRONT:PAD_FRONT + N_META]
    small_rows = [row_of(small_parts[n]) for n in SMALL] + [row_of(g_meta)]
    n_small = sum(r.shape[0] for r in small_rows)
    small = jnp.pad(jnp.concatenate(small_rows, axis=0), ((0, -(-n_small // 8) * 8 - n_small), (0, 0)))
    all_small = gather_small(small)
    small_t = small.shape[0]

    def sum8_fn(*slabs):
        acc = slabs[0]
        for s in slabs[1:]:
            acc = acc + s
        return acc

    (g_small,) = rowwise("sum_small", sum8_fn, [(all_small.reshape(8 * small_t, d), d, 0, k) for k in range(8)],
                         [], [], [(d, F32)], tm=small_t, n_rows=small_t)

    off = 0
    for n, part in zip(SMALL, small_rows[:-1]):
        rows = part.shape[0]
        update(n, [g_small[off:off + rows, :d].reshape(-1)[:wts[n].size]])
        off += rows
    update("meta_tokens", [lax.dynamic_slice_in_dim(g_small[off:off + N_META, :d], my_chip * mcols, mcols, axis=1)])

    loss = lax.psum(loss_part[0, 0], ("x", "y", "c"))

    def shaped(n, a):
        return a.reshape((1,) + wts[n].shape) if n in BIG else a.reshape(wts[n].shape)

    return (loss, grad_x, *[shaped(n, updates[n][k]) for k in range(4) for n in WEIGHTS])
```

```python
import functools
import math

import jax
import jax.numpy as jnp
from jax import lax
from jax.experimental import pallas as pl
from jax.experimental.pallas import tpu as pltpu

F32 = jnp.float32
BF16 = jnp.bfloat16
MESH = pl.DeviceIdType.MESH

N_META = 16
NORM_EPS = 1e-6
HEAD = 128
ROPE = 64
ROPE_HALF = ROPE // 2
QK_PAD = 2 * HEAD
ROPE_THETA = 10000.0
SEQ_BLOCK = 256
PAD_FRONT = SEQ_BLOCK - N_META
NEG = -1e30
VMEM_LIMIT = 56 * 1024 * 1024
WGRAD_TILE_MAX = 1536
WGRAD_TILE_MIN = 1024
WGRAD_ROW_STEP = 1536
HG_TILE_MAX = 1152
ATTN_HEADS_PER_STEP = 1
ATTN_TILE_MAX = 768

ADAM_LR, ADAM_B1, ADAM_B2, ADAM_EPS, ADAM_WD, ADAM_STEP = 0.001, 0.9, 0.999, 1e-08, 0.01, 10

BIG = ("w_in", "w_hg_o", "w_q_b", "w_kv_b", "w_mla_o", "w_out", "w_ffn_in", "w_ffn_out")
COL_SHARDED = ("w_in", "w_q_b", "w_kv_b", "w_ffn_in")
SMALL = ("b_gate", "lb_logits", "hg_norm_g", "q_a_norm_g", "kv_a_norm_g", "mix_pre_g", "mix_post_g",
         "ffn_pre_g", "ffn_post_g")
WEIGHTS = ("meta_tokens", "w_in", "b_gate", "lb_logits", "hg_norm_g", "w_hg_o", "q_a_norm_g", "w_q_b",
           "kv_a_norm_g", "w_kv_b", "w_mla_o", "w_out", "mix_pre_g", "mix_post_g", "ffn_pre_g", "ffn_post_g",
           "w_ffn_in", "w_ffn_out")


def _tile(n, cap, unit=128):
    if n <= cap:
        return n
    best = None
    for t in range(unit, cap + 1, unit):
        if n % t == 0:
            best = t
    assert best is not None, (n, cap, unit)
    return best


def _sigmoid(x):
    return 1.0 / (1.0 + jnp.exp(-x))


def _bf(x):
    return x.astype(BF16)


def rowwise(name, fn, row_ins, seq_tabs, consts, row_outs, acc_outs=(), tm=SEQ_BLOCK, n_rows=None):
    t_rows = row_ins[0][0].shape[0] if n_rows is None else n_rows
    nt = t_rows // tm
    assert t_rows % tm == 0
    n_in = len(row_ins) + len(seq_tabs) + len(consts)
    n_row = len(row_outs)

    def body(*refs):
        vals = [r[...].astype(F32) for r in refs[:n_in]]
        res = fn(*vals)
        if not isinstance(res, (tuple, list)):
            res = (res,)
        outs = refs[n_in:]
        for k in range(n_row):
            outs[k][...] = res[k].astype(outs[k].dtype)
        if acc_outs:
            @pl.when(pl.program_id(0) == 0)
            def _():
                for k in range(len(acc_outs)):
                    outs[n_row + k][...] = jnp.zeros_like(outs[n_row + k])

            for k in range(len(acc_outs)):
                outs[n_row + k][...] += res[n_row + k]

    row_ins = [tuple(e) + (0,) * (4 - len(e)) for e in row_ins]
    in_specs = [pl.BlockSpec((tm, w), functools.partial(lambda i, j, ro: (ro(i) if callable(ro) else i + ro, j),
                                                        j=j, ro=ro)) for (_, w, j, ro) in row_ins]
    for tab in seq_tabs:
        per = tab.shape[0] // tm
        in_specs.append(pl.BlockSpec((tm, tab.shape[1]), functools.partial(lambda i, per: (i % per, 0), per=per)))
    for c in consts:
        in_specs.append(pl.BlockSpec(c.shape, lambda i: (0, 0)))
    row_outs = [tuple(e) + (t_rows, None)[len(e) - 2:] for e in row_outs]
    out_specs = [pl.BlockSpec((tm, w), functools.partial(lambda i, rm: (i if rm is None else rm(i), 0), rm=rm))
                 for (w, _, _, rm) in row_outs]
    out_specs += [pl.BlockSpec(s, lambda i: (0, 0)) for s in acc_outs]
    out_shape = [jax.ShapeDtypeStruct((rows, w), dt) for (w, dt, rows, _) in row_outs]
    out_shape += [jax.ShapeDtypeStruct(s, F32) for s in acc_outs]
    res = pl.pallas_call(
        body, name=name, grid=(nt,), in_specs=in_specs, out_specs=out_specs, out_shape=out_shape,
        compiler_params=pltpu.CompilerParams(dimension_semantics=("arbitrary",)),
    )(*[e[0] for e in row_ins], *seq_tabs, *consts)
    return res


def matmul(name, a, b, mode, out_dtype=F32):
    if mode != "tn":
        return _matmul_resident(name, a if isinstance(a, (list, tuple)) else [a],
                                b if isinstance(b, (list, tuple)) else [b], mode, out_dtype)
    kdim, m = a.shape
    n = b.shape[1]
    tn = _tile(n, WGRAD_TILE_MAX)
    tm = _tile(m, WGRAD_TILE_MAX if tn <= WGRAD_TILE_MIN else WGRAD_TILE_MIN)
    tk = _tile(kdim, WGRAD_ROW_STEP)
    nk = kdim // tk

    def body(a_ref, b_ref, o_ref, acc_ref):
        k = pl.program_id(2)

        @pl.when(k == 0)
        def _():
            acc_ref[...] = jnp.zeros_like(acc_ref)

        acc_ref[...] += lax.dot_general(a_ref[...], b_ref[...], TN_DIMS, preferred_element_type=F32)

        @pl.when(k == nk - 1)
        def _():
            o_ref[...] = acc_ref[...].astype(o_ref.dtype)

    return pl.pallas_call(
        body, name=name, grid=(m // tm, n // tn, nk),
        in_specs=[pl.BlockSpec((tk, tm), lambda i, j, k: (k, i)), pl.BlockSpec((tk, tn), lambda i, j, k: (k, j))],
        out_specs=pl.BlockSpec((tm, tn), lambda i, j, k: (i, j)),
        out_shape=jax.ShapeDtypeStruct((m, n), out_dtype),
        scratch_shapes=[pltpu.VMEM((tm, tn), F32)],
        compiler_params=pltpu.CompilerParams(dimension_semantics=("arbitrary", "arbitrary", "arbitrary"),
                                             vmem_limit_bytes=VMEM_LIMIT),
    )(a, b)


def _matmul_resident(name, a_list, b_list, mode, out_dtype):
    m = a_list[0].shape[0]
    n = b_list[0].shape[1] if mode == "nn" else b_list[0].shape[0]
    k_total = sum(a.shape[1] for a in a_list)
    out_bytes = 2 if out_dtype == BF16 else 4
    budget = VMEM_LIMIT - 4 * k_total * n - (6 << 20)
    tm = 1024
    while tm > 128 and 2 * tm * (2 * k_total + out_bytes * n) > budget:
        tm //= 2
    tm = _tile(m, tm)
    cn = _tile(n, WGRAD_TILE_MIN)
    npairs = len(a_list)

    def body(*refs):
        a_refs, b_refs, o_ref = refs[:npairs], refs[npairs:2 * npairs], refs[2 * npairs]
        for c in range(n // cn):
            acc = None
            for a_ref, b_ref in zip(a_refs, b_refs):
                if mode == "nn":
                    part = jnp.dot(a_ref[...], b_ref[:, pl.ds(c * cn, cn)], preferred_element_type=F32)
                else:
                    part = lax.dot_general(a_ref[...], b_ref[pl.ds(c * cn, cn), :], NT_DIMS,
                                           preferred_element_type=F32)
                acc = part if acc is None else acc + part
            o_ref[:, pl.ds(c * cn, cn)] = acc.astype(o_ref.dtype)

    in_specs = [pl.BlockSpec((tm, a.shape[1]), lambda i: (i, 0)) for a in a_list]
    in_specs += [pl.BlockSpec(b.shape, lambda i: (0, 0)) for b in b_list]
    return pl.pallas_call(
        body, name=name, grid=(m // tm,), in_specs=in_specs,
        out_specs=pl.BlockSpec((tm, n), lambda i: (i, 0)),
        out_shape=jax.ShapeDtypeStruct((m, n), out_dtype),
        compiler_params=pltpu.CompilerParams(dimension_semantics=("arbitrary",), vmem_limit_bytes=VMEM_LIMIT),
    )(*a_list, *b_list)


def matmul_fused(name, fn, row_ins, consts, weight, pieces, mode, extra_outs, out_dtype=BF16, tm=256, acc_outs=(),
                 epilogue=None, epi_outs=()):
    row_ins = [tuple(e) + (0,) * (4 - len(e)) for e in row_ins]
    t_rows = row_ins[0][0].shape[0]
    tm = _tile(t_rows, tm)
    several = isinstance(weight, (list, tuple))
    weights = list(weight) if several else [weight]
    n_in = len(row_ins) + len(consts)
    n_w = len(weights)
    n_parts = len(pieces)
    n_mm = len(epi_outs) if epilogue is not None else (n_parts if several else 1)
    n_row_out = n_mm + len(extra_outs)

    def width(w_arr):
        return w_arr.shape[1] if mode == "nn" else w_arr.shape[0]

    def body(*refs):
        w_hbms = refs[n_in:n_in + n_w]
        outs = refs[n_in + n_w:n_in + n_w + n_row_out + len(acc_outs)]
        w_refs, sems = refs[-n_w - 1:-1], refs[-1]

        @pl.when(pl.program_id(0) == 0)
        def _():
            cps = [pltpu.make_async_copy(w_hbms[k], w_refs[k], sems.at[k]) for k in range(n_w)]
            for cp in cps:
                cp.start()
            for cp in cps:
                cp.wait()
            for k in range(len(acc_outs)):
                outs[n_row_out + k][...] = jnp.zeros_like(outs[n_row_out + k])

        vals = [r[...].astype(F32) for r in refs[:n_in]]
        res = fn(*vals)
        products = []
        for a_p, piece in zip(res[:n_parts], pieces):
            w_ref, (k0, k1) = (w_refs[piece[0]], piece[1:]) if several else (w_refs[0], piece)
            if mode == "nn":
                products.append(jnp.dot(_bf(a_p), w_ref[pl.ds(k0, k1 - k0), :], preferred_element_type=F32))
            else:
                products.append(lax.dot_general(_bf(a_p), w_ref[:, pl.ds(k0, k1 - k0)], NT_DIMS,
                                                preferred_element_type=F32))
        if not several:
            products = [functools.reduce(lambda u, w: u + w, products)]
        sums = list(res[n_parts + len(extra_outs):])
        if epilogue is not None:
            products, more_sums = epilogue(products, vals)
            sums += list(more_sums)
        for p, val in enumerate(products):
            outs[p][...] = val.astype(outs[p].dtype)
        for o_ref, val in zip(outs[n_mm:n_row_out], res[n_parts:]):
            o_ref[...] = val.astype(o_ref.dtype)
        for k in range(len(acc_outs)):
            outs[n_row_out + k][...] += sums[k]

    in_specs = [pl.BlockSpec((tm, w), functools.partial(lambda i, j, ro: (ro(i) if callable(ro) else i + ro, j),
                                                        j=j, ro=ro)) for (_, w, j, ro) in row_ins]
    in_specs += [pl.BlockSpec(c.shape, lambda i: (0, 0)) for c in consts]
    in_specs += [pl.BlockSpec(memory_space=pl.ANY)] * n_w
    if epilogue is not None:
        widths = list(epi_outs) + list(extra_outs)
    elif several:
        widths = [(width(weights[piece[0]]), out_dtype) for piece in pieces] + list(extra_outs)
    else:
        widths = [(width(weights[0]), out_dtype)] + list(extra_outs)
    widths = [tuple(e) + (t_rows, None)[len(e) - 2:] for e in widths]
    return pl.pallas_call(
        body, name=name, grid=(t_rows // tm,), in_specs=in_specs,
        out_specs=[pl.BlockSpec((tm, w), functools.partial(lambda i, rm: (i if rm is None else rm(i), 0), rm=rm))
                   for (w, _, _, rm) in widths]
        + [pl.BlockSpec(s, lambda i: (0, 0)) for s in acc_outs],
        out_shape=[jax.ShapeDtypeStruct((rows, w), dt) for (w, dt, rows, _) in widths]
        + [jax.ShapeDtypeStruct(s, F32) for s in acc_outs],
        scratch_shapes=[pltpu.VMEM(w_arr.shape, w_arr.dtype) for w_arr in weights] + [pltpu.SemaphoreType.DMA((n_w,))],
        compiler_params=pltpu.CompilerParams(dimension_semantics=("arbitrary",), vmem_limit_bytes=VMEM_LIMIT),
    )(*[e[0] for e in row_ins], *consts, *weights)


def _rms(x, g):
    r = lax.rsqrt(jnp.mean(x * x, axis=-1, keepdims=True) + NORM_EPS)
    return x * r * g


def _rms_bwd(x, g, dy):
    r = lax.rsqrt(jnp.mean(x * x, axis=-1, keepdims=True) + NORM_EPS)
    xh = x * r
    dyg = dy * g
    dx = r * (dyg - xh * jnp.mean(dyg * xh, axis=-1, keepdims=True))
    return dx, jnp.sum(dy * xh, axis=0, keepdims=True)


def _silu(x):
    return x * _sigmoid(x)


def _silu_grad(x):
    s = _sigmoid(x)
    return s * (1.0 + x * (1.0 - s))


def _rope(xs, cos, s_up, s_dn):
    return xs * cos + pltpu.roll(xs, ROPE_HALF, 1) * s_up + pltpu.roll(xs, HEAD - ROPE_HALF, 1) * s_dn


def _rope_bwd(dy, cos, s_up, s_dn):
    return dy * cos + pltpu.roll(dy * s_up, HEAD - ROPE_HALF, 1) + pltpu.roll(dy * s_dn, ROPE_HALF, 1)


HG_SUB = 128
HG_LEVELS = 7
HG_E_ROWS = (HG_LEVELS + 1) * HG_SUB
HG_BWD_GROUP = 6
TN_DIMS = (((0,), (0,)), ((), ()))
NT_DIMS = (((1,), (1,)), ((), ()))


def _hg_constants():
    import numpy as np
    n = HG_SUB
    r = np.arange(n)[:, None]
    c = np.arange(n)[None, :]
    cs, ps = [], []
    for lvl in range(HG_LEVELS):
        m = (n // 2) >> lvl
        upper = (r % (2 * m)) >= m
        mid = (r // (2 * m)) * (2 * m) + m - 1
        cs.append(np.where(upper, (c > mid) & (c <= r), (c > r) & (c <= mid)))
        ps.append(((r // (2 * m)) == (c // (2 * m))) & upper & ((c % (2 * m)) < m))
    cs.append(c <= r)
    cs.append(np.ones((8, n), bool))
    cstack = np.concatenate(cs, 0).astype(np.float32)
    pstack = np.concatenate(ps, 0).astype(np.float32)
    pstack_t = np.concatenate([p.T for p in ps], 0).astype(np.float32)
    return (jnp.asarray(cstack, BF16), jnp.asarray(cstack[:HG_E_ROWS].T, BF16), jnp.asarray(pstack, F32),
            jnp.asarray(pstack_t, F32))


def _split_dot(c_bf, x):
    hi = _bf(x)
    lo = _bf(x - hi.astype(F32))
    r2 = jnp.dot(c_bf, jnp.concatenate([hi, lo], axis=1), preferred_element_type=F32)
    return r2[:, :HEAD] + r2[:, HEAD:]


def _hg_gates(hq, hf, lb):
    sq = _sigmoid(hq)
    sg = _sigmoid(hf)
    fg = lb + (1.0 - lb) * sg
    return sq, hq * sq, sg, fg, 1.0 - fg, jnp.log(fg)


def hgrn_fwd(proj_main, lb, consts, bl, lp, d):
    nh = d // HEAD
    rows_blk = _tile(lp, HG_TILE_MAX, HG_SUB)
    nb = lp // rows_blk
    spb = rows_blk // HG_SUB
    cstack, _, pstack, _ = consts

    def body(hq_ref, hf_ref, hi_ref, lb_ref, c_ref, p_ref, o_ref, st_ref, a_ref, s_ref):
        j = pl.program_id(2)

        @pl.when(j == 0)
        def _():
            s_ref[...] = jnp.zeros_like(s_ref)

        lbv = lb_ref[...]
        cs = c_ref[...]
        rows = [pl.ds(s * HG_SUB, HG_SUB) for s in range(spb)]
        gates = [_hg_gates(hq_ref[r, :].astype(F32), hf_ref[r, :].astype(F32), lbv) for r in rows]
        qs, ks = [g_[1] for g_ in gates], [g_[4] for g_ in gates]
        vs = [hi_ref[r, :].astype(F32) for r in rows]
        es = [_split_dot(cs, g_[5]) for g_ in gates]
        a_acc = [jnp.zeros((HG_SUB, HG_SUB), F32) for _ in rows]
        for lvl in range(HG_LEVELS):
            for s in range(spb):
                x = jnp.exp(es[s][lvl * HG_SUB:(lvl + 1) * HG_SUB])
                a_acc[s] = a_acc[s] + p_ref[pl.ds(lvl * HG_SUB, HG_SUB), :] * lax.dot_general(
                    _bf(qs[s] * x), _bf(ks[s] * x), NT_DIMS, preferred_element_type=F32)
        o_intra, qbs, kds, e_lasts = [], [], [], []
        for s in range(spb):
            a_bf = _bf(a_acc[s])
            a_ref[0, 0, s] = a_bf
            bc = es[s][HG_LEVELS * HG_SUB:HG_E_ROWS]
            b_last = jnp.tile(es[s][HG_E_ROWS:], (HG_SUB // 8, 1))
            o_intra.append(jnp.dot(a_bf, _bf(vs[s]), preferred_element_type=F32)
                           + jnp.sum(qs[s] * ks[s], axis=1, keepdims=True) * vs[s])
            qbs.append(_bf(qs[s] * jnp.exp(bc)))
            kds.append(_bf(ks[s] * jnp.exp(b_last - bc)))
            e_lasts.append(jnp.exp(b_last))
        st = s_ref[...]
        for s in range(spb):
            st_ref[0, 0, s] = st
            o_ref[rows[s], :] = (o_intra[s] + lax.dot_general(qbs[s], _bf(st), NT_DIMS, preferred_element_type=F32)
                                 ).astype(o_ref.dtype)
            st = st * e_lasts[s] + lax.dot_general(_bf(vs[s]), kds[s], TN_DIMS, preferred_element_type=F32)
        s_ref[...] = st

    def colspec(off):
        return pl.BlockSpec((rows_blk, HEAD), functools.partial(lambda h, b, j, off: (b * nb + j, off + h), off=off))

    whole = lambda arr: pl.BlockSpec(arr.shape, lambda h, b, j: (0, 0))
    return pl.pallas_call(
        body, name="hgrn_fwd", grid=(nh, bl, nb),
        in_specs=[colspec(0), colspec(nh), colspec(2 * nh), pl.BlockSpec((1, HEAD), lambda h, b, j: (0, h)),
                  whole(cstack), whole(pstack)],
        out_specs=[pl.BlockSpec((rows_blk, HEAD), lambda h, b, j: (b * nb + j, h)),
                   pl.BlockSpec((1, 1, spb, HEAD, HEAD), lambda h, b, j: (b, h, j, 0, 0)),
                   pl.BlockSpec((1, 1, spb, HG_SUB, HG_SUB), lambda h, b, j: (b, h, j, 0, 0))],
        out_shape=[jax.ShapeDtypeStruct((bl * lp, d), BF16),
                   jax.ShapeDtypeStruct((bl, nh, lp // HG_SUB, HEAD, HEAD), F32),
                   jax.ShapeDtypeStruct((bl, nh, lp // HG_SUB, HG_SUB, HG_SUB), BF16)],
        scratch_shapes=[pltpu.VMEM((HEAD, HEAD), F32)],
        compiler_params=pltpu.CompilerParams(dimension_semantics=("arbitrary", "arbitrary", "arbitrary")),
    )(proj_main, proj_main, proj_main, lb, cstack, pstack)


def hgrn_bwd(proj_main, lb, consts, states, a_mats, do_scan, bl, lp, d):
    nh = d // HEAD
    rows_blk = _tile(lp, HG_TILE_MAX, HG_SUB)
    nb = lp // rows_blk
    spb = rows_blk // HG_SUB
    cstack, cstack_t = consts[0], consts[1]
    pstack, pstack_t = _bf(consts[2]), _bf(consts[3])

    def body(hq_ref, hf_ref, hi_ref, lb_ref, c_ref, ct_ref, p_ref, pt_ref, st_ref, a_ref, do_ref,
             dq_ref, df_ref, di_ref, dlb_ref, ds_ref):
        b_id, j = pl.program_id(1), pl.program_id(2)
        blk = nb - 1 - j

        @pl.when(j == 0)
        def _():
            ds_ref[...] = jnp.zeros_like(ds_ref)

        @pl.when((j == 0) & (b_id == 0))
        def _():
            dlb_ref[...] = jnp.zeros_like(dlb_ref)

        lbv = lb_ref[...]
        cs = c_ref[...]
        cst = ct_ref[...]

        dlb = jnp.zeros((1, HEAD), F32)
        for first in reversed(range(0, spb, HG_BWD_GROUP)):
            dlb = dlb + _hg_group_bwd(list(range(first, min(first + HG_BWD_GROUP, spb))), lbv, cs, cst, hq_ref,
                                      hf_ref, hi_ref, st_ref, a_ref, do_ref, p_ref, pt_ref, dq_ref, df_ref, di_ref,
                                      ds_ref)
        dlb_ref[...] += dlb

    def _hg_group_bwd(ids, lbv, cs, cst, hq_ref, hf_ref, hi_ref, st_ref, a_ref, do_ref, p_ref, pt_ref, dq_ref,
                      df_ref, di_ref, ds_ref):
        rng = range(len(ids))
        rows = [pl.ds(s * HG_SUB, HG_SUB) for s in ids]
        hqs = [hq_ref[r, :].astype(F32) for r in rows]
        gates = [_hg_gates(hqs[s], hf_ref[rows[s], :].astype(F32), lbv) for s in rng]
        sqs, qs, sgs, fgs, ks = ([g_[i] for g_ in gates] for i in range(5))
        vs = [hi_ref[r, :].astype(F32) for r in rows]
        dos = [do_ref[r, :].astype(F32) for r in rows]
        sts = [st_ref[0, 0, s] for s in ids]
        es = [_split_dot(cs, g_[5]) for g_ in gates]
        bcs = [e[HG_LEVELS * HG_SUB:HG_E_ROWS] for e in es]
        b_lasts = [jnp.tile(e[HG_E_ROWS:], (HG_SUB // 8, 1)) for e in es]
        ebs = [jnp.exp(bc) for bc in bcs]
        qbs = [qs[s] * ebs[s] for s in rng]
        ers = [jnp.exp(b_lasts[s] - bcs[s]) for s in rng]
        kds = [ks[s] * ers[s] for s in rng]
        e_lasts = [jnp.exp(b) for b in b_lasts]
        do_bfs, v_bfs = [_bf(x) for x in dos], [_bf(x) for x in vs]
        das = [_bf(lax.dot_general(do_bfs[s], v_bfs[s], NT_DIMS, preferred_element_type=F32)) for s in rng]
        dats = [_bf(lax.dot_general(v_bfs[s], do_bfs[s], NT_DIMS, preferred_element_type=F32)) for s in rng]
        dqbs = [jnp.dot(do_bfs[s], _bf(sts[s]), preferred_element_type=F32) for s in rng]
        m_s = [lax.dot_general(do_bfs[s], _bf(qbs[s]), TN_DIMS, preferred_element_type=F32) for s in rng]
        dst_outs = [None] * len(ids)
        dst = ds_ref[...]
        for s in reversed(rng):
            dst_outs[s] = dst
            dst = dst * e_lasts[s] + m_s[s]
        ds_ref[...] = dst
        dst_bfs = [_bf(x) for x in dst_outs]
        d_diags = [jnp.sum(dos[s] * vs[s], axis=1, keepdims=True) for s in rng]
        dvs = [lax.dot_general(a_ref[0, 0, ids[s]], do_bfs[s], TN_DIMS, preferred_element_type=F32)
               + jnp.sum(qs[s] * ks[s], axis=1, keepdims=True) * dos[s]
               + lax.dot_general(_bf(kds[s]), dst_bfs[s], NT_DIMS, preferred_element_type=F32) for s in rng]
        dkds = [jnp.dot(v_bfs[s], dst_bfs[s], preferred_element_type=F32) for s in rng]
        dqs = [dqbs[s] * ebs[s] + d_diags[s] * ks[s] for s in rng]
        dks = [dkds[s] * ers[s] + d_diags[s] * qs[s] for s in rng]
        d_lasts = [jnp.sum(dst_outs[s] * sts[s] * e_lasts[s], axis=0, keepdims=True)
                   + jnp.sum(dkds[s] * kds[s], axis=0, keepdims=True) for s in rng]
        des = [[] for _ in rng]
        for lvl in range(HG_LEVELS):
            for s in rng:
                x = jnp.exp(es[s][lvl * HG_SUB:(lvl + 1) * HG_SUB])
                qh, kh = qs[s] * x, ks[s] * x
                dm = p_ref[pl.ds(lvl * HG_SUB, HG_SUB), :] * das[s]
                dmt = pt_ref[pl.ds(lvl * HG_SUB, HG_SUB), :] * dats[s]
                dqh = jnp.dot(dm, _bf(kh), preferred_element_type=F32)
                dkh = jnp.dot(dmt, _bf(qh), preferred_element_type=F32)
                dqs[s] = dqs[s] + dqh * x
                dks[s] = dks[s] + dkh * x
                des[s].append(dqh * qh + dkh * kh)
        dlb = jnp.zeros((1, HEAD), F32)
        for s in rng:
            des[s].append(dqbs[s] * qbs[s] - dkds[s] * kds[s])
            dg = _split_dot(cst, jnp.concatenate(des[s], axis=0)) + d_lasts[s]
            dfg = dg / fgs[s] - dks[s]
            dq_ref[rows[s], :] = (dqs[s] * (sqs[s] * (1.0 + hqs[s] * (1.0 - sqs[s])))).astype(dq_ref.dtype)
            df_ref[rows[s], :] = (dfg * (1.0 - lbv) * sgs[s] * (1.0 - sgs[s])).astype(df_ref.dtype)
            di_ref[rows[s], :] = dvs[s].astype(di_ref.dtype)
            dlb = dlb + jnp.sum(dfg * (1.0 - sgs[s]), axis=0, keepdims=True)
        return dlb

    def colspec(off):
        return pl.BlockSpec((rows_blk, HEAD),
                            functools.partial(lambda h, b, j, off: (b * nb + nb - 1 - j, off + h), off=off))

    whole = lambda arr: pl.BlockSpec(arr.shape, lambda h, b, j: (0, 0))
    mats = lambda: pl.BlockSpec((1, 1, spb, HEAD, HEAD), lambda h, b, j: (b, h, nb - 1 - j, 0, 0))
    t_rows = bl * lp
    return pl.pallas_call(
        body, name="hgrn_bwd", grid=(nh, bl, nb),
        in_specs=[colspec(0), colspec(nh), colspec(2 * nh), pl.BlockSpec((1, HEAD), lambda h, b, j: (0, h)),
                  whole(cstack), whole(cstack_t), whole(pstack), whole(pstack_t), mats(), mats(), colspec(0)],
        out_specs=[colspec(0), colspec(0), colspec(0), pl.BlockSpec((1, HEAD), lambda h, b, j: (0, h))],
        out_shape=[jax.ShapeDtypeStruct((t_rows, d), BF16)] * 3 + [jax.ShapeDtypeStruct((1, d), F32)],
        scratch_shapes=[pltpu.VMEM((HEAD, HEAD), F32)],
        compiler_params=pltpu.CompilerParams(dimension_semantics=("arbitrary", "arbitrary", "arbitrary")),
    )(proj_main, proj_main, proj_main, lb, cstack, cstack_t, pstack, pstack_t, states, a_mats, do_scan)


def _key_query_mask(key0, qry0, nk, nq_, causal):
    key = key0 + lax.broadcasted_iota(jnp.int32, (nk, 1), 0)
    if not causal:
        return key >= PAD_FRONT
    qry = qry0 + lax.broadcasted_iota(jnp.int32, (1, nq_), 1)
    return (key <= qry) & (key >= PAD_FRONT)


def _attn_tile(lp):
    return _tile(lp, ATTN_TILE_MAX, SEQ_BLOCK)


def attn_fwd_t(q_cat, k_cat, v_t, bl, lp, nm):
    tq = tk = _attn_tile(lp)
    nq = lp // tq
    hp = ATTN_HEADS_PER_STEP
    assert nm % hp == 0

    def body(q_ref, k_ref, vt_ref, o_ref, lse_ref, m_ref, acc_ref):
        i = pl.program_id(2)
        m_ref[...] = jnp.full_like(m_ref, NEG)
        acc_ref[...] = jnp.zeros_like(acc_ref)

        def step(c, mask):
            c0 = pl.multiple_of(c * tk, tk)
            for hh in range(hp):
                cols = pl.ds(hh * QK_PAD, QK_PAD)
                st = lax.dot_general(k_ref[pl.ds(c0, tk), cols], q_ref[:, cols], NT_DIMS,
                                     preferred_element_type=F32)
                if mask is not None:
                    st = jnp.where(_key_query_mask(c * tk, i * tq, tk, tq, mask == "causal"), st, NEG)
                m_old = m_ref[hh]
                m_new = jnp.maximum(m_old, jnp.max(st, axis=0, keepdims=True))
                alpha = jnp.exp(m_old - m_new)
                pt = jnp.exp(st - m_new)
                acc_ref[hh] = alpha * acc_ref[hh] + jnp.dot(vt_ref[0, hh, pl.ds(c, 1)][0], _bf(pt),
                                                            preferred_element_type=F32)
                m_ref[hh] = m_new

        def mid(c, carry):
            step(c, None)
            return carry

        @pl.when(i == 0)
        def _():
            step(0, "causal")

        @pl.when(i > 0)
        def _():
            step(0, "pad")
            lax.fori_loop(1, i, mid, 0)
            step(i, "causal")

        for hh in range(hp):
            l8 = acc_ref[hh, pl.ds(HEAD, 8), :]
            o_ref[:, pl.ds(hh * HEAD, HEAD)] = jnp.transpose(
                acc_ref[hh, pl.ds(0, HEAD), :] / jnp.tile(l8, (HEAD // 8, 1))).astype(o_ref.dtype)
            lse_ref[0, hh, 0] = m_ref[hh] + jnp.log(jnp.max(l8, axis=0, keepdims=True))

    return pl.pallas_call(
        body, name="attn_fwd", grid=(bl, nm // hp, nq),
        in_specs=[pl.BlockSpec((tq, hp * QK_PAD), lambda b, h, i: (b * nq + i, h)),
                  pl.BlockSpec((lp, hp * QK_PAD), lambda b, h, i: (b, h)),
                  pl.BlockSpec((1, hp, nq, HEAD + 16, tk), lambda b, h, i: (b, h, 0, 0, 0))],
        out_specs=[pl.BlockSpec((tq, hp * HEAD), lambda b, h, i: (b * nq + i, h)),
                   pl.BlockSpec((1, hp, 1, 1, tq), lambda b, h, i: (b, h, i, 0, 0))],
        out_shape=[jax.ShapeDtypeStruct((bl * lp, nm * HEAD), BF16),
                   jax.ShapeDtypeStruct((bl, nm, nq, 1, tq), F32)],
        scratch_shapes=[pltpu.VMEM((hp, 1, tq), F32), pltpu.VMEM((hp, HEAD + 16, tq), F32)],
        compiler_params=pltpu.CompilerParams(dimension_semantics=("arbitrary", "arbitrary", "arbitrary")),
    )(q_cat, k_cat, v_t)


def attn_bwd_t(q_cat, k_cat, k_t, v, o, do, lse, bl, lp, nm):
    tq = tk = _attn_tile(lp)
    nq = lp // tq
    hp = ATTN_HEADS_PER_STEP
    assert nm % hp == 0

    def body(q_ref, k_ref, kt_ref, v_ref, o_ref, do_ref, lse_ref, dq_ref, dk_ref, dv_ref, dqt_ref, dka_ref, dva_ref):
        i = pl.program_id(2)

        @pl.when(i == 0)
        def _():
            dqt_ref[...] = jnp.zeros_like(dqt_ref)

        dka_ref[...] = jnp.zeros_like(dka_ref)
        dva_ref[...] = jnp.zeros_like(dva_ref)
        ones8 = jnp.ones((8, HEAD), BF16)

        def step(c, mask):
            c0 = pl.multiple_of(c * tq, tq)
            for hh in range(hp):
                qcols, vcols = pl.ds(hh * QK_PAD, QK_PAD), pl.ds(hh * HEAD, HEAD)
                qs = q_ref[pl.ds(c0, tq), qcols]
                dos = do_ref[pl.ds(c0, tq), vcols]
                prod = dos.astype(F32) * o_ref[pl.ds(c0, tq), vcols].astype(F32)
                hi = _bf(prod)
                lo = _bf(prod - hi.astype(F32))
                delta8 = (lax.dot_general(ones8, hi, NT_DIMS, preferred_element_type=F32)
                          + lax.dot_general(ones8, lo, NT_DIMS, preferred_element_type=F32))
                st = lax.dot_general(k_ref[:, qcols], qs, NT_DIMS, preferred_element_type=F32)
                pt = jnp.exp(st - lse_ref[0, hh, pl.ds(c, 1)][0])
                if mask is not None:
                    pt = jnp.where(_key_query_mask(i * tk, c * tq, tk, tq, mask == "causal"), pt, 0.0)
                dva_ref[hh] += jnp.dot(_bf(pt), dos, preferred_element_type=F32)
                dpt = lax.dot_general(v_ref[:, vcols], dos, NT_DIMS, preferred_element_type=F32)
                dst = _bf(pt * (dpt - jnp.tile(delta8, (tk // 8, 1))))
                dka_ref[hh] += jnp.dot(dst, qs, preferred_element_type=F32)
                dqt_ref[hh, pl.ds(c, 1)] += jnp.dot(kt_ref[0, hh, 0], dst, preferred_element_type=F32)[None]

        step(i, "causal")

        def rest_masked(c, carry):
            step(c, "pad")
            return carry

        def rest(c, carry):
            step(c, None)
            return carry

        @pl.when(i == 0)
        def _():
            lax.fori_loop(1, nq, rest_masked, 0)

        @pl.when(i > 0)
        def _():
            lax.fori_loop(i + 1, nq, rest, 0)

        for hh in range(hp):
            dk_ref[:, pl.ds(hh * QK_PAD, QK_PAD)] = dka_ref[hh].astype(dk_ref.dtype)
            dv_ref[:, pl.ds(hh * HEAD, HEAD)] = dva_ref[hh].astype(dv_ref.dtype)

        @pl.when(i == nq - 1)
        def _():
            for hh in range(hp):
                for c in range(nq):
                    dq_ref[pl.ds(c * tq, tq), pl.ds(hh * QK_PAD, QK_PAD)] = (
                        jnp.transpose(dqt_ref[hh, c])).astype(dq_ref.dtype)

    return pl.pallas_call(
        body, name="attn_bwd", grid=(bl, nm // hp, nq),
        in_specs=[pl.BlockSpec((lp, hp * QK_PAD), lambda b, h, i: (b, h)),
                  pl.BlockSpec((tk, hp * QK_PAD), lambda b, h, i: (b * nq + i, h)),
                  pl.BlockSpec((1, hp, 1, QK_PAD, tk), lambda b, h, i: (b, h, i, 0, 0)),
                  pl.BlockSpec((tk, hp * HEAD), lambda b, h, i: (b * nq + i, h)),
                  pl.BlockSpec((lp, hp * HEAD), lambda b, h, i: (b, h)),
                  pl.BlockSpec((lp, hp * HEAD), lambda b, h, i: (b, h)),
                  pl.BlockSpec((1, hp, nq, 1, tq), lambda b, h, i: (b, h, 0, 0, 0))],
        out_specs=[pl.BlockSpec((lp, hp * QK_PAD), lambda b, h, i: (b, h)),
                   pl.BlockSpec((tk, hp * QK_PAD), lambda b, h, i: (b * nq + i, h)),
                   pl.BlockSpec((tk, hp * HEAD), lambda b, h, i: (b * nq + i, h))],
        out_shape=[jax.ShapeDtypeStruct((bl * lp, nm * QK_PAD), BF16),
                   jax.ShapeDtypeStruct((bl * lp, nm * QK_PAD), BF16),
                   jax.ShapeDtypeStruct((bl * lp, nm * HEAD), BF16)],
        scratch_shapes=[pltpu.VMEM((hp, nq, QK_PAD, tq), F32), pltpu.VMEM((hp, tk, QK_PAD), F32),
                        pltpu.VMEM((hp, tk, HEAD), F32)],
        compiler_params=pltpu.CompilerParams(dimension_semantics=("arbitrary", "arbitrary", "arbitrary")),
    )(q_cat, k_cat, k_t, v, o, do, lse)


def _place():
    return lax.axis_index("x"), lax.axis_index("y"), lax.axis_index("c")


def gather_shards(packed):
    hbm = pl.BlockSpec(memory_space=pl.ANY)

    def body(src_ref, out_ref, send_sems, recv_sems, local_sem):
        x, y, c = _place()
        me = 2 * x + y
        chips = [(1 - x, y), (x, 1 - y), (1 - x, 1 - y)]
        local = pltpu.make_async_copy(src_ref, out_ref.at[me], local_sem)
        local.start()
        sends = []
        for k, (px, py) in enumerate(chips):
            cp = pltpu.make_async_remote_copy(src_ref=src_ref, dst_ref=out_ref.at[me], send_sem=send_sems.at[k],
                                              recv_sem=recv_sems.at[k], device_id=(px, py, c), device_id_type=MESH)
            cp.start()
            sends.append(cp)
        for k, (px, py) in enumerate(chips):
            pltpu.make_async_remote_copy(src_ref=src_ref, dst_ref=out_ref.at[2 * px + py], send_sem=send_sems.at[k],
                                         recv_sem=recv_sems.at[k], device_id=(px, py, c),
                                         device_id_type=MESH).wait_recv()
        for cp in sends:
            cp.wait_send()
        local.wait()

    return pl.pallas_call(
        body, name="gather_shards", in_specs=[hbm], out_specs=hbm,
        out_shape=jax.ShapeDtypeStruct((4,) + packed.shape, packed.dtype),
        scratch_shapes=[pltpu.SemaphoreType.DMA((3,)), pltpu.SemaphoreType.DMA((3,)), pltpu.SemaphoreType.DMA],
    )(packed)


def gather_small(small):
    hbm = pl.BlockSpec(memory_space=pl.ANY)

    def body(small_ref, all_ref, send_sems, recv_sems, local_sem):
        x, y, c = _place()
        me = 4 * x + 2 * y + c
        local = pltpu.make_async_copy(small_ref, all_ref.at[me], local_sem)
        local.start()
        others = [(x ^ ((r >> 2) & 1), y ^ ((r >> 1) & 1), c ^ (r & 1)) for r in range(1, 8)]
        sends = []
        for r, peer in enumerate(others):
            cp = pltpu.make_async_remote_copy(src_ref=small_ref, dst_ref=all_ref.at[me], send_sem=send_sems.at[r],
                                              recv_sem=recv_sems.at[r], device_id=peer, device_id_type=MESH)
            cp.start()
            sends.append(cp)
        for r, (px, py, pc) in enumerate(others):
            pltpu.make_async_remote_copy(src_ref=small_ref, dst_ref=all_ref.at[4 * px + 2 * py + pc],
                                         send_sem=send_sems.at[r], recv_sem=recv_sems.at[r],
                                         device_id=(px, py, pc), device_id_type=MESH).wait_recv()
        for cp in sends:
            cp.wait_send()
        local.wait()

    return pl.pallas_call(
        body, name="gather_small", in_specs=[hbm], out_specs=hbm,
        out_shape=jax.ShapeDtypeStruct((8,) + small.shape, small.dtype),
        scratch_shapes=[pltpu.SemaphoreType.DMA((7,)), pltpu.SemaphoreType.DMA((7,)), pltpu.SemaphoreType.DMA],
    )(small)


def swap_with_sibling(name, parts):
    n = len(parts)
    hbm = pl.BlockSpec(memory_space=pl.ANY)

    def body(*refs):
        x, y, c = _place()
        cps = [pltpu.make_async_remote_copy(src_ref=refs[j], dst_ref=refs[n + j], send_sem=refs[2 * n].at[j],
                                            recv_sem=refs[2 * n + 1].at[j], device_id=(x, y, 1 - c),
                                            device_id_type=MESH) for j in range(n)]
        for cp in cps:
            cp.start()
        for cp in cps:
            cp.wait()

    return pl.pallas_call(
        body, name=name, in_specs=[hbm] * n, out_specs=[hbm] * n,
        out_shape=[jax.ShapeDtypeStruct(p.shape, p.dtype) for p in parts],
        scratch_shapes=[pltpu.SemaphoreType.DMA((n,)), pltpu.SemaphoreType.DMA((n,))],
    )(*parts)


def _chips3():
    x, y, c = _place()
    return [(1 - x, y, c), (x, 1 - y, c), (1 - x, 1 - y, c)]


def _push_copies(src_refs, land_refs, send_sems, recv_sems, per_chip):
    x, y, _ = _place()
    cps = []
    for j, (src_ref, land_ref) in enumerate(zip(src_refs, land_refs)):
        for k, (px, py, pc) in enumerate(_chips3()):
            part = src_ref.at[2 * px + py] if per_chip else src_ref
            slot = k if per_chip else 2 * x + y
            cps.append(pltpu.make_async_remote_copy(
                src_ref=part, dst_ref=land_ref.at[slot], send_sem=send_sems.at[3 * j + k],
                recv_sem=recv_sems.at[3 * j + k], device_id=(px, py, pc), device_id_type=MESH))
    return cps


def push_start(name, srcs, per_chip):
    n = len(srcs)
    hbm = pl.BlockSpec(memory_space=pltpu.HBM)
    sem = pl.BlockSpec(memory_space=pltpu.SEMAPHORE)
    lands = [lax.empty((3 if per_chip else 4,) + s.shape[-2:], s.dtype) for s in srcs]

    def body(*refs):
        src_refs, land_refs = refs[:n], refs[n:2 * n]
        send_sems, recv_sems = refs[2 * n], refs[2 * n + 1]
        for cp in _push_copies(src_refs, land_refs, send_sems, recv_sems, per_chip):
            cp.start()
        refs[-1][...] = jnp.zeros_like(refs[-1])

    outs = pl.pallas_call(
        body, name=name,
        out_shape=(pltpu.SemaphoreType.DMA((3 * n,)), pltpu.SemaphoreType.DMA((3 * n,)),
                   *[pltpu.HBM(a.shape, a.dtype) for a in list(srcs) + lands], jax.ShapeDtypeStruct((8, HEAD), F32)),
        in_specs=(hbm,) * (2 * n),
        out_specs=(sem, sem) + (hbm,) * (2 * n) + (pl.BlockSpec(memory_space=pltpu.VMEM),),
        input_output_aliases={j: 2 + j for j in range(2 * n)},
        compiler_params=pltpu.CompilerParams(has_side_effects=pltpu.SideEffectType.DATAFLOW_SIDE_EFFECTING),
    )(*[pltpu.with_memory_space_constraint(a, pltpu.HBM) for a in list(srcs) + lands])
    return tuple(outs[:-1]), outs[-1]


def push_wait(name, handle, after, per_chip):
    send_sems, recv_sems = handle[0], handle[1]
    thru = handle[2:]
    n = len(thru) // 2
    hbm = pl.BlockSpec(memory_space=pltpu.HBM)
    sem = pl.BlockSpec(memory_space=pltpu.SEMAPHORE)

    def body(*refs):
        src_refs, land_refs = refs[:n], refs[n:2 * n]
        for cp in _push_copies(src_refs, land_refs, refs[2 * n], refs[2 * n + 1], per_chip):
            cp.wait_send()
            cp.wait_recv()

    outs = pl.pallas_call(
        body, name=name,
        out_shape=tuple(pltpu.HBM(a.shape, a.dtype) for a in thru),
        in_specs=(hbm,) * (2 * n) + (sem, sem, pl.BlockSpec(memory_space=pl.ANY)), out_specs=(hbm,) * (2 * n),
        input_output_aliases={j: j for j in range(2 * n)},
        compiler_params=pltpu.CompilerParams(has_side_effects=pltpu.SideEffectType.DATAFLOW_SIDE_EFFECTING),
    )(*thru, send_sems, recv_sems, after)
    return outs[:n], outs[n:]


def join_gathered(name, own, landed, my_chip):
    blocks = lax.dynamic_update_index_in_dim(landed, own, my_chip, 0)
    _, r, c = blocks.shape
    if name in COL_SHARDED:
        return blocks.transpose(1, 0, 2).reshape(r, 4 * c)
    return blocks.reshape(4 * r, c)


def adamw(name, w, g_parts, m, v):
    r, c = w.shape
    tr = r if r * c <= 65536 else _tile(r, 128, 8)
    ng = len(g_parts)

    def body(*refs):
        w_ref, m_ref, v_ref = refs[0], refs[1 + ng], refs[2 + ng]
        g_ref, d_ref, nm_ref, nv_ref = refs[3 + ng:]
        gv = refs[1][...]
        for k in range(1, ng):
            gv = gv + refs[1 + k][...]
        m_new = ADAM_B1 * m_ref[...] + (1.0 - ADAM_B1) * gv
        v_new = ADAM_B2 * v_ref[...] + (1.0 - ADAM_B2) * (gv * gv)
        m_hat = m_new / (1.0 - ADAM_B1 ** ADAM_STEP)
        v_hat = v_new / (1.0 - ADAM_B2 ** ADAM_STEP)
        g_ref[...] = gv
        d_ref[...] = -ADAM_LR * (m_hat / (jnp.sqrt(v_hat) + ADAM_EPS) + ADAM_WD * w_ref[...])
        nm_ref[...] = m_new
        nv_ref[...] = v_new

    spec = pl.BlockSpec((tr, c), lambda i: (i, 0))
    return pl.pallas_call(
        body, name=name, grid=(r // tr,), in_specs=[spec] * (3 + ng), out_specs=[spec] * 4,
        out_shape=[jax.ShapeDtypeStruct((r, c), F32)] * 4,
        compiler_params=pltpu.CompilerParams(dimension_semantics=("arbitrary",)),
    )(w, *g_parts, m, v)


def split_full(name, full, s):
    if name in COL_SHARDED:
        c = full.shape[1] // 4
        return full[:, s * c:(s + 1) * c]
    r = full.shape[0] // 4
    return full[s * r:(s + 1) * r]


def kernel(x, meta_tokens, w_in, b_gate, lb_logits, hg_norm_g, w_hg_o, q_a_norm_g, w_q_b, kv_a_norm_g, w_kv_b, w_mla_o, w_out, mix_pre_g, mix_post_g, ffn_pre_g, ffn_post_g, w_ffn_in, w_ffn_out, loss_target, m_meta_tokens, m_w_in, m_b_gate, m_lb_logits, m_hg_norm_g, m_w_hg_o, m_q_a_norm_g, m_w_q_b, m_kv_a_norm_g, m_w_kv_b, m_w_mla_o, m_w_out, m_mix_pre_g, m_mix_post_g, m_ffn_pre_g, m_ffn_post_g, m_w_ffn_in, m_w_ffn_out, v_meta_tokens, v_w_in, v_b_gate, v_lb_logits, v_hg_norm_g, v_w_hg_o, v_q_a_norm_g, v_w_q_b, v_kv_a_norm_g, v_w_kv_b, v_w_mla_o, v_w_out, v_mix_pre_g, v_mix_post_g, v_ffn_pre_g, v_ffn_post_g, v_w_ffn_in, v_w_ffn_out):
    wts = dict(meta_tokens=meta_tokens, w_in=w_in[0], b_gate=b_gate, lb_logits=lb_logits, hg_norm_g=hg_norm_g,
               w_hg_o=w_hg_o[0], q_a_norm_g=q_a_norm_g, w_q_b=w_q_b[0], kv_a_norm_g=kv_a_norm_g, w_kv_b=w_kv_b[0],
               w_mla_o=w_mla_o[0], w_out=w_out[0], mix_pre_g=mix_pre_g, mix_post_g=mix_post_g, ffn_pre_g=ffn_pre_g,
               ffn_post_g=ffn_post_g, w_ffn_in=w_ffn_in[0], w_ffn_out=w_ffn_out[0])
    mom_m = dict(meta_tokens=m_meta_tokens, w_in=m_w_in[0], b_gate=m_b_gate, lb_logits=m_lb_logits,
                 hg_norm_g=m_hg_norm_g, w_hg_o=m_w_hg_o[0], q_a_norm_g=m_q_a_norm_g, w_q_b=m_w_q_b[0],
                 kv_a_norm_g=m_kv_a_norm_g, w_kv_b=m_w_kv_b[0], w_mla_o=m_w_mla_o[0], w_out=m_w_out[0],
                 mix_pre_g=m_mix_pre_g, mix_post_g=m_mix_post_g, ffn_pre_g=m_ffn_pre_g, ffn_post_g=m_ffn_post_g,
                 w_ffn_in=m_w_ffn_in[0], w_ffn_out=m_w_ffn_out[0])
    mom_v = dict(meta_tokens=v_meta_tokens, w_in=v_w_in[0], b_gate=v_b_gate, lb_logits=v_lb_logits,
                 hg_norm_g=v_hg_norm_g, w_hg_o=v_w_hg_o[0], q_a_norm_g=v_q_a_norm_g, w_q_b=v_w_q_b[0],
                 kv_a_norm_g=v_kv_a_norm_g, w_kv_b=v_w_kv_b[0], w_mla_o=v_w_mla_o[0], w_out=v_w_out[0],
                 mix_pre_g=v_mix_pre_g, mix_post_g=v_mix_post_g, ffn_pre_g=v_ffn_pre_g, ffn_post_g=v_ffn_post_g,
                 w_ffn_in=v_w_ffn_in[0], w_ffn_out=v_w_ffn_out[0])

    bl, seq, d = x.shape
    lp = PAD_FRONT + N_META + seq
    t_rows = bl * lp
    nh = d // HEAD
    ql, kvl = wts["w_q_b"].shape[0], wts["w_kv_b"].shape[0]
    nm = (4 * wts["w_mla_o"].shape[0]) // HEAD
    ffn = 4 * wts["w_ffn_out"].shape[0]
    mla_w = ql + kvl + HEAD
    assert ql == kvl and ql % HEAD == 0 and seq % SEQ_BLOCK == 0 and d % HEAD == 0
    scale = (HEAD + ROPE) ** -0.5
    my_chip = 2 * lax.axis_index("x") + lax.axis_index("y")

    mcols = meta_tokens.shape[1]
    meta_all = gather_shards(meta_tokens)
    meta_full = jnp.concatenate([meta_all[s] for s in range(4)], axis=1)

    def start_gather(name, names, order_after):
        srcs = [_bf(wts[n]) for n in names]
        if order_after is not None:
            srcs[0] = srcs[0] + order_after[0, 0].astype(BF16)
        return push_start(name, srcs, per_chip=False)

    def finish_gather(name, names, started, after):
        owns, landed = push_wait(name, started[0], after, per_chip=False)
        return {n: join_gathered(n, own, land, my_chip) for n, own, land in zip(names, owns, landed)}

    rest_names = tuple(n for n in BIG if n != "w_in")
    my_c = lax.axis_index("c")
    w_in_bf = _bf(wts["w_in"])
    half = w_in_bf.shape[0] // 2
    own_half = (lax.dynamic_slice_in_dim(w_in_bf, my_c * half, half, axis=0)
                + (meta_all[0, :1, :1] * 0.0)[0, 0].astype(BF16))
    gather_1 = push_start("gather_w_in_start", [own_half], per_chip=False)
    gather_2 = start_gather("gather_rest_start", rest_names, gather_1[1])

    tiles_seq, tiles_real = lp // SEQ_BLOCK, seq // SEQ_BLOCK
    assert PAD_FRONT + N_META == SEQ_BLOCK

    def real_block(i):
        return (i // tiles_seq) * tiles_real + jnp.maximum(i % tiles_seq - 1, 0)

    meta_rows = jnp.broadcast_to(((jnp.arange(lp) >= PAD_FRONT) & (jnp.arange(lp) < PAD_FRONT + N_META)
                                  ).astype(F32)[:, None], (lp, HEAD))
    pos = (jnp.arange(lp, dtype=jnp.int32) - PAD_FRONT).astype(F32)
    inv_freq = 1.0 / (ROPE_THETA ** (jnp.arange(0, ROPE, 2, dtype=F32) / ROPE))
    ang = pos[:, None] * inv_freq[None, :]
    zeros32 = jnp.zeros((lp, ROPE_HALF), F32)
    zeros64 = jnp.zeros((lp, HEAD - ROPE), F32)
    t_cos = jnp.concatenate([jnp.cos(ang), jnp.cos(ang), zeros64], axis=1)
    t_up = jnp.concatenate([zeros32, jnp.sin(ang), zeros64], axis=1)
    t_dn = jnp.concatenate([-jnp.sin(ang), zeros32, zeros64], axis=1)
    real = jnp.broadcast_to((jnp.arange(lp) >= PAD_FRONT + N_META).astype(F32)[:, None], (lp, HEAD))
    lanes = d // HEAD
    lb_soft = jax.nn.softmax(lb_logits.astype(F32), axis=0)
    lb = lb_soft[0:1]

    meta_tile = jnp.concatenate([jnp.zeros((PAD_FRONT, d), F32), meta_full], axis=0)

    def first_fn(xv, is_real, is_meta, mtile, g):
        h = xv * jnp.tile(is_real, (1, lanes)) + mtile * jnp.tile(is_meta, (1, lanes))
        return _rms(h, g), h

    u1, h0 = rowwise("norm_mix_pre", first_fn, [(x.reshape(bl * seq, d), d, 0, real_block)], [real, meta_rows],
                     [meta_tile, mix_pre_g + gather_2[1][0, 0]], [(d, BF16), (d, F32)], n_rows=t_rows)
    _, (fetched,) = push_wait("gather_w_in_wait", gather_1[0], u1, per_chip=False)
    (handed,) = swap_with_sibling("swap_w_in", [fetched])
    halves = jnp.stack([fetched, handed])
    remote = jnp.concatenate([lax.dynamic_index_in_dim(halves, my_c, 0, keepdims=False),
                              lax.dynamic_index_in_dim(halves, 1 - my_c, 0, keepdims=False)], axis=1)
    full = {"w_in": join_gathered("w_in", w_in_bf, remote, my_chip)}
    w_main = jnp.concatenate([full["w_in"][:, :4 * d], full["w_in"][:, -2 * d:]], axis=1)
    w_mla = jnp.pad(full["w_in"][:, 4 * d:4 * d + ql + kvl + ROPE], ((0, 0), (0, HEAD - ROPE)))
    proj_main = matmul("proj_main", u1, w_main, "nn", out_dtype=BF16)
    proj_mla = matmul("proj_mla", u1, w_mla, "nn", out_dtype=BF16)
    hg_consts = _hg_constants()
    o_scan, states, a_mats = hgrn_fwd(proj_main, lb, hg_consts, bl, lp, d)

    full.update(finish_gather("gather_rest_wait", rest_names, gather_2, o_scan))
    w_qb = jnp.pad(full["w_q_b"].reshape(ql, nm, HEAD + ROPE), ((0, 0), (0, 0), (0, QK_PAD - HEAD - ROPE))
                   ).reshape(ql, nm * QK_PAD)
    w_kvb = full["w_kv_b"]

    def hg_out_fn(o, hg, g):
        ov = jnp.concatenate([_rms(o[:, h * HEAD:(h + 1) * HEAD], g) for h in range(nh)], axis=1) * _silu(hg)
        return ov, ov

    y_a, o_hg = matmul_fused("hgrn_out_y_a", hg_out_fn, [(o_scan, d, 0), (proj_main, d, 3)], [hg_norm_g],
                             _bf(full["w_hg_o"]), [(0, d)], "nn", [(d, BF16)], tm=512)

    def seq_tile(i):
        return i % tiles_seq

    tables = [(t_cos, HEAD, 0, seq_tile), (t_up, HEAD, 0, seq_tile), (t_dn, HEAD, 0, seq_tile)]

    def q_norm_fn(cq, cos, s_up, s_dn, g):
        cn = _rms(cq, g)
        return cn, cn

    def q_rope_fn(products, vals):
        qf = products[0] * scale
        cos, s_up, s_dn = vals[1:4]
        qs = []
        for h in range(nm):
            qs += [qf[:, h * QK_PAD:h * QK_PAD + HEAD], _rope(qf[:, h * QK_PAD + HEAD:(h + 1) * QK_PAD], cos, s_up, s_dn)]
        return [jnp.concatenate(qs, axis=1)], []

    q_cat, qn = matmul_fused("q_norm_up_rope", q_norm_fn, [(proj_mla, ql, 0)] + tables, [q_a_norm_g], w_qb,
                             [(0, ql)], "nn", [(ql, BF16)], epilogue=q_rope_fn, epi_outs=[(nm * QK_PAD, BF16)])

    def kv_norm_fn(ckv, kpe, cos, s_up, s_dn, g):
        cn = _rms(ckv, g)
        return cn, cn

    def kv_rope_fn(products, vals):
        kvf = products[0]
        kpe_r = _rope(vals[1], *vals[2:5])
        ks, vs = [], []
        for h in range(nm):
            ks += [kvf[:, h * QK_PAD:h * QK_PAD + HEAD], kpe_r]
            vs += [kvf[:, h * QK_PAD + HEAD:(h + 1) * QK_PAD]]
        return [jnp.concatenate(ks, axis=1), jnp.concatenate(vs, axis=1)], []

    kpe_blk = (ql + kvl) // HEAD
    k_cat, v_att, kvn = matmul_fused("kv_norm_up_rope", kv_norm_fn,
                                     [(proj_mla, kvl, 1), (proj_mla, HEAD, kpe_blk)] + tables, [kv_a_norm_g], w_kvb,
                                     [(0, kvl)], "nn", [(kvl, BF16)], epilogue=kv_rope_fn,
                                     epi_outs=[(nm * QK_PAD, BF16), (nm * HEAD, BF16)])
    at = _attn_tile(lp)
    v_t = v_att.reshape(bl, lp // at, at, nm, HEAD).transpose(0, 3, 1, 4, 2)
    v_t = jnp.concatenate([v_t, jnp.ones((bl, nm, lp // at, 16, at), BF16)], axis=3)
    k_t = k_cat.reshape(bl, lp // at, at, nm, QK_PAD).transpose(0, 3, 1, 4, 2)
    o_mla, lse = attn_fwd_t(q_cat, k_cat, v_t, bl, lp, nm)
    y_b = matmul("y_b", o_mla, _bf(full["w_mla_o"]), "nn", out_dtype=BF16)

    def gate_fn(ya, yb, ga, gb, bias):
        zv = _sigmoid(ga + bias[:, :d]) * ya + _sigmoid(gb + bias[:, d:]) * yb
        return zv, zv

    mixed, z = matmul_fused("gate_mix_out", gate_fn,
                            [(y_a, d, 0), (y_b, d, 0), (proj_main, d, 4), (proj_main, d, 5)], [b_gate],
                            _bf(full["w_out"]), [(0, d)], "nn", [(d, BF16)], tm=512)

    def mid_fn(h, mx, g_post, g_pre):
        h1v = h + _rms(mx, g_post)
        u2v = _rms(h1v, g_pre)
        return u2v, h1v, u2v

    gu, h1, u2 = matmul_fused("norm_mid_ffn_in", mid_fn, [(h0, d, 0), (mixed, d, 0)], [mix_post_g, ffn_pre_g],
                              _bf(full["w_ffn_in"]), [(0, d)], "nn", [(d, F32), (d, BF16)])
    def swiglu_fn(gt, up):
        a = _silu(gt) * up
        return a, a

    f_out, act = matmul_fused("swiglu_ffn_out", swiglu_fn, [(gu, ffn, 0), (gu, ffn, 1)], [],
                              _bf(full["w_ffn_out"]), [(0, ffn)], "nn", [(ffn, BF16)])

    def loss_fn(h1v, fv, tg, realv, g_post):
        h2 = h1v + _rms(fv, g_post)
        diff = (h2 - tg) * jnp.tile(realv, (1, lanes))
        part = jnp.broadcast_to(0.5 * jnp.sum(diff * diff, keepdims=True) / d, (1, HEAD))
        dy = diff / d
        df, dg = _rms_bwd(fv, g_post, dy)
        return df, dy, df, part, dg

    d_act, dy, df, loss_part, g_ffn_post = matmul_fused(
        "loss_head_d_act", loss_fn,
        [(h1, d, 0), (f_out, d, 0), (loss_target.reshape(bl * seq, d), d, 0, real_block), (real, HEAD, 0, seq_tile)],
        [ffn_post_g], _bf(full["w_ffn_out"]), [(0, d)], "nt", [(d, BF16), (d, BF16)],
        acc_outs=[(1, HEAD), (1, d)])
    grads = {}
    grads["w_ffn_out"] = matmul("gw_ffn_out", act, df, "tn")

    def swiglu_bwd_fn(gt, up, da):
        dgt, dup = da * up * _silu_grad(gt), da * _silu(gt)
        return dgt, dup, jnp.concatenate([dgt, dup], axis=1)

    du2, dgu = matmul_fused("swiglu_bwd_d_u2", swiglu_bwd_fn, [(gu, ffn, 0), (gu, ffn, 1), (d_act, ffn, 0)], [],
                            _bf(full["w_ffn_in"]), [(0, ffn), (ffn, 2 * ffn)], "nt", [(2 * ffn, BF16)])
    grads["w_ffn_in"] = matmul("gw_ffn_in", u2, dgu, "tn")

    def mid_bwd_fn(dyv, h1v, du2v, mx, g_pre, g_post):
        dx, dg_pre = _rms_bwd(h1v, g_pre, du2v)
        dh1 = dyv + dx
        dmx, dg_post = _rms_bwd(mx, g_post, dh1)
        return dmx, dh1, dmx, dg_pre, dg_post

    dz, dh1, dmixed, g_ffn_pre, g_mix_post = matmul_fused(
        "norm_mid_bwd_d_z", mid_bwd_fn, [(dy, d, 0), (h1, d, 0), (du2, d, 0), (mixed, d, 0)],
        [ffn_pre_g, mix_post_g], _bf(full["w_out"]), [(0, d)], "nt", [(d, BF16), (d, BF16)], tm=512,
        acc_outs=[(1, d), (1, d)])
    grads["w_out"] = matmul("gw_out", z, dmixed, "tn")

    def gate_bwd_fn(dzv, ya, yb, ga, gb, bias):
        sa, sb = _sigmoid(ga + bias[:, :d]), _sigmoid(gb + bias[:, d:])
        dga = dzv * ya * sa * (1.0 - sa)
        dgb = dzv * yb * sb * (1.0 - sb)
        dgates = jnp.concatenate([dga, dgb], axis=1)
        dya, dyb = dzv * sa, dzv * sb
        return dya, dyb, dya, dyb, dgates, jnp.sum(dgates, axis=0, keepdims=True)

    def hg_out_bwd_fn(products, vals):
        do, o, hg, g = products[0], vals[5], vals[6], vals[8]
        dn = do * _silu(hg)
        dos, dgs, ons = [], 0.0, []
        for h in range(nh):
            sl = slice(h * HEAD, (h + 1) * HEAD)
            dx, dg = _rms_bwd(o[:, sl], g, dn[:, sl])
            dos.append(dx)
            dgs = dgs + dg
            ons.append(_rms(o[:, sl], g))
        dhg_v = do * jnp.concatenate(ons, axis=1) * _silu_grad(hg)
        return [jnp.concatenate(dos, axis=1), products[1], dhg_v], [dgs]

    do_scan, do_mla, dhg, dy_a, dy_b, dgates, g_b_gate, g_hg_norm = matmul_fused(
        "gate_mix_bwd_d_o", lambda dzv, ya, yb, ga, gb, o, hg, bias, g: gate_bwd_fn(dzv, ya, yb, ga, gb, bias),
        [(dz, d, 0), (y_a, d, 0), (y_b, d, 0), (proj_main, d, 4), (proj_main, d, 5), (o_scan, d, 0),
         (proj_main, d, 3)], [b_gate, hg_norm_g],
        [_bf(full["w_hg_o"]), _bf(full["w_mla_o"])], [(0, 0, d), (1, 0, d)], "nt",
        [(d, BF16), (d, BF16), (2 * d, BF16)], acc_outs=[(1, 2 * d), (1, HEAD)],
        epilogue=hg_out_bwd_fn, epi_outs=[(d, BF16), (d, BF16), (d, BF16)])
    grads["w_hg_o"] = matmul("gw_hg_o", o_hg, dy_a, "tn")
    grads["w_mla_o"] = matmul("gw_mla_o", o_mla, dy_b, "tn")

    early = ("w_hg_o", "w_mla_o", "w_out", "w_ffn_in", "w_ffn_out")
    late = ("w_in", "w_q_b", "w_kv_b")

    def start_grads(name, names):
        sends = [_bf(jnp.stack([split_full(n, grads[n], s) for s in range(4)])) for n in names]
        mines = []
        for n in names:
            r, c = wts[n].shape
            axis, size = (1, c) if n in COL_SHARDED else (0, r)
            mines.append(lax.dynamic_slice_in_dim(grads[n], my_chip * size, size, axis=axis))
        handle, token = push_start(name, sends, per_chip=True)
        return handle, token, mines

    def finish_grads(tag, names, started, after):
        handle, _, mines = started
        _, landed = push_wait(f"grads_{tag}_wait", handle, after, per_chip=True)
        parts = []
        for n, mine, land in zip(names, mines, landed):
            r, c = mine.shape
            tr = _tile(r, 256, 16)
            land2 = land.reshape(3 * r, c)
            parts.append(rowwise(f"sum_chips_{n}", lambda a, r0, r1, r2: a + r0 + r1 + r2,
                                 [(mine, c, 0)] + [(land2, c, 0, k * (r // tr)) for k in range(3)],
                                 [], [], [(c, F32)], tm=tr)[0])
        sibs = swap_with_sibling(f"swap_{tag}", parts)
        return {n: [p, s] for n, p, s in zip(names, parts, sibs)}

    grads_early = start_grads("grads_early_start", early)
    token_a = grads_early[1]

    dhq, dhf, dhi, g_lb = hgrn_bwd(proj_main, lb + token_a[0, 0], hg_consts, states, a_mats, do_scan, bl, lp, d)

    dq_cat, dk_cat, dv_att = attn_bwd_t(q_cat, k_cat, k_t, v_att, o_mla, do_mla, lse, bl, lp, nm)

    def mla_prep_bwd_fn(dqc, dkc, dvv, cos, s_up, s_dn, cq, ckv, gq, gk):
        dqc = dqc * scale
        dqs, dkvs = [], []
        for h in range(nm):
            dqs += [dqc[:, h * QK_PAD:h * QK_PAD + HEAD],
                    _rope_bwd(dqc[:, h * QK_PAD + HEAD:(h + 1) * QK_PAD], cos, s_up, s_dn)]
            dkvs += [dkc[:, h * QK_PAD:h * QK_PAD + HEAD], dvv[:, h * HEAD:(h + 1) * HEAD]]
        dqf, dkvf = jnp.concatenate(dqs, axis=1), jnp.concatenate(dkvs, axis=1)
        return dqf, dkvf, dqf, dkvf

    def mla_norms_bwd_fn(products, vals):
        dkc, cos, s_up, s_dn, cq, ckv, gq, gk = vals[1], vals[3], vals[4], vals[5], vals[6], vals[7], vals[8], vals[9]
        dkpe = 0.0
        for h in range(nm):
            dkpe = dkpe + dkc[:, h * QK_PAD + HEAD:(h + 1) * QK_PAD]
        dcq, dgq = _rms_bwd(cq, gq, products[0])
        dckv, dgk = _rms_bwd(ckv, gk, products[1])
        return [jnp.concatenate([dcq, dckv, _rope_bwd(dkpe, cos, s_up, s_dn)], axis=1)], [dgq, dgk]

    dmla, dq_full, dkv_full, g_q_norm, g_kv_norm = matmul_fused(
        "mla_prep_bwd_d_norms", mla_prep_bwd_fn,
        [(dq_cat, nm * QK_PAD, 0), (dk_cat, nm * QK_PAD, 0), (dv_att, nm * HEAD, 0)] + tables
        + [(proj_mla, ql, 0), (proj_mla, kvl, 1)], [q_a_norm_g, kv_a_norm_g],
        [w_qb, w_kvb], [(0, 0, nm * QK_PAD), (1, 0, nm * QK_PAD)], "nt",
        [(nm * QK_PAD, BF16), (nm * QK_PAD, BF16)], acc_outs=[(1, ql), (1, kvl)],
        epilogue=mla_norms_bwd_fn, epi_outs=[(mla_w, BF16)])
    g_wqb = matmul("gw_q_b", qn, dq_full, "tn")
    grads["w_q_b"] = g_wqb.reshape(ql, nm, QK_PAD)[:, :, :HEAD + ROPE].reshape(ql, nm * (HEAD + ROPE))
    grads["w_kv_b"] = matmul("gw_kv_b", kvn, dkv_full, "tn")

    d_pieces = [dhq, dhf, dhi, dhg, dgates, dmla]
    gw_parts = [matmul(f"gw_in_{k}", u1, dp, "tn") for k, dp in enumerate(d_pieces)]
    grads["w_in"] = jnp.concatenate(gw_parts[:4] + [gw_parts[5][:, :ql + kvl + ROPE], gw_parts[4]], axis=1)
    grads_late = start_grads("grads_late_start", late)
    w_mla_after = w_mla + grads_late[1][0, 0].astype(BF16)
    w_cat = jnp.concatenate([w_main, w_mla_after], axis=1)
    edges = [0, d, 2 * d, 3 * d, 4 * d, 6 * d, 6 * d + mla_w]

    def first_bwd_fn(products, vals):
        dh1v, h, is_meta, g = vals[6], vals[7], vals[8], vals[9]
        dx, dg = _rms_bwd(h, g, products[0])
        dh0v = dh1v + dx
        return [dh0v], [dg, dh0v * jnp.tile(is_meta, (1, lanes))]

    grad_x, g_mix_pre, meta_sum = matmul_fused(
        "d_u1_norm_mix_pre_bwd", lambda *v: v[:6],
        [(dp, dp.shape[1], 0) for dp in d_pieces] + [(dh1, d, 0), (h0, d, 0), (meta_rows, HEAD, 0, seq_tile)],
        [mix_pre_g], w_cat, list(zip(edges[:-1], edges[1:])), "nt", [], acc_outs=[(1, d), (SEQ_BLOCK, d)],
        epilogue=first_bwd_fn, epi_outs=[(d, F32, bl * seq, real_block)])
    grad_x = grad_x.reshape(bl, seq, d)

    g_parts = finish_grads("early", early, grads_early, g_mix_pre)
    updates = {}

    def update(n, parts):
        w2 = wts[n].reshape(-1, wts[n].shape[-1])
        updates[n] = adamw("adamw_" + n, w2, [p.reshape(w2.shape) for p in parts], mom_m[n].reshape(w2.shape),
                           mom_v[n].reshape(w2.shape))

    for n in early:
        update(n, g_parts[n])
    g_parts = finish_grads("late", late, grads_late, updates[early[-1]][0])
    for n in late:
        update(n, g_parts[n])
    p0 = lb_soft[0:1]
    g_lb_logits = jnp.concatenate([g_lb * p0 * (1.0 - p0), -g_lb * p0 * (1.0 - p0)], axis=0)

    def row_of(vec):
        return vec.reshape(-1, d) if vec.size >= d else jnp.pad(vec.reshape(1, -1), ((0, 0), (0, d - vec.size)))

    small_parts = dict(b_gate=g_b_gate, lb_logits=g_lb_logits, hg_norm_g=g_hg_norm, q_a_norm_g=g_q_norm,
                       kv_a_norm_g=g_kv_norm, mix_pre_g=g_mix_pre, mix_post_g=g_mix_post, ffn_pre_g=g_ffn_pre,
                       ffn_post_g=g_ffn_post)
    g_meta = meta_sum[PAD_FRONT:PAD_FRONT + N_META]
    small_rows = [row_of(small_parts[n]) for n in SMALL] + [row_of(g_meta)]
    n_small = sum(r.shape[0] for r in small_rows)
    small = jnp.pad(jnp.concatenate(small_rows, axis=0), ((0, -(-n_small // 8) * 8 - n_small), (0, 0)))
    all_small = gather_small(small)
    small_t = small.shape[0]

    def sum8_fn(*slabs):
        acc = slabs[0]
        for s in slabs[1:]:
            acc = acc + s
        return acc

    (g_small,) = rowwise("sum_small", sum8_fn, [(all_small.reshape(8 * small_t, d), d, 0, k) for k in range(8)],
                         [], [], [(d, F32)], tm=small_t, n_rows=small_t)

    off = 0
    for n, part in zip(SMALL, small_rows[:-1]):
        rows = part.shape[0]
        update(n, [g_small[off:off + rows, :d].reshape(-1)[:wts[n].size]])
        off += rows
    update("meta_tokens", [lax.dynamic_slice_in_dim(g_small[off:off + N_META, :d], my_chip * mcols, mcols, axis=1)])

    loss = lax.psum(loss_part[0, 0], ("x", "y", "c"))

    def shaped(n, a):
        return a.reshape((1,) + wts[n].shape) if n in BIG else a.reshape(wts[n].shape)

    return (loss, grad_x, *[shaped(n, updates[n][k]) for k in range(4) for n in WEIGHTS])
```
